```python
import jax, jax.numpy as jnp
from jax import lax
import numpy as np

D_MODEL = 1024
BATCH = 8
SEQ = 8192
DEPTH = 1

HEAD_DIM = 64
N_Q_HEADS = D_MODEL // HEAD_DIM
N_KV_HEADS = N_Q_HEADS // 8
GROUP = N_Q_HEADS // N_KV_HEADS
WINDOW = 128
BLOCK = 128
ATTN_WIDTH = N_Q_HEADS * HEAD_DIM
KV_WIDTH = N_KV_HEADS * HEAD_DIM
CONV_WIDTH = D_MODEL
CONV_K = 3
N_BRANCH = 2
D_FF = -(-8 * D_MODEL // (3 * 256)) * 256
IN_WIDTH = ATTN_WIDTH + 2 * KV_WIDTH + 3 * CONV_WIDTH + N_BRANCH * D_MODEL
N_MOD = 6
EPS = 1e-6

kernel_name = "hybrid_swa_sink_shortconv_gated_merge_adaln_block"


def rms_norm(x, g):
    xf = x.astype(jnp.float32)
    y = xf * lax.rsqrt(jnp.mean(xf * xf, axis=-1, keepdims=True) + EPS)
    return (y * g.astype(jnp.float32)).astype(x.dtype)


def with_prev_block(t):
    prev = jnp.pad(t, ((0, 0), (1, 0), (0, 0), (0, 0), (0, 0)))[:, :-1]
    return jnp.concatenate([prev, t], axis=2)


def sliding_window_sink_attention(q, k, v, sinks):
    B, T = q.shape[0], q.shape[1]
    nb = T // BLOCK
    qb = q.reshape(B, nb, BLOCK, N_KV_HEADS, GROUP, HEAD_DIM)
    kx = with_prev_block(k.reshape(B, nb, BLOCK, N_KV_HEADS, HEAD_DIM))
    vx = with_prev_block(v.reshape(B, nb, BLOCK, N_KV_HEADS, HEAD_DIM))
    s = jnp.einsum('bnqhgd,bnkhd->bnhgqk', qb, kx).astype(jnp.float32) * (HEAD_DIM ** -0.5)
    qpos = jnp.arange(BLOCK)[:, None] + BLOCK
    kpos = jnp.arange(2 * BLOCK)[None, :]
    rel = qpos - kpos
    band = (rel >= 0) & (rel < WINDOW)
    blk = jnp.arange(nb)[:, None, None]
    valid = band[None] & ((blk > 0) | (kpos[None] >= BLOCK))
    s = jnp.where(valid[None, :, None, None], s, -jnp.inf)
    sink = sinks.astype(jnp.float32).reshape(1, 1, N_KV_HEADS, GROUP, 1, 1)
    m = jnp.maximum(jnp.max(s, axis=-1, keepdims=True), sink)
    p = jnp.exp(s - m)
    denom = jnp.sum(p, axis=-1, keepdims=True) + jnp.exp(sink - m)
    p = (p / denom).astype(v.dtype)
    o = jnp.einsum('bnhgqk,bnkhd->bnqhgd', p, vx)
    return o.reshape(B, T, ATTN_WIDTH)


def causal_depthwise_conv(u, w):
    C = u.shape[-1]
    return lax.conv_general_dilated(
        u, w[:, None, :].astype(u.dtype), window_strides=(1,),
        padding=[(CONV_K - 1, 0)], dimension_numbers=('NWC', 'WIO', 'NWC'),
        feature_group_count=C)


def _fwd_setup_inputs(seed: int = 0) -> dict:
    key = jax.random.key(seed)
    ks = jax.random.split(key, 16)
    f32 = jnp.float32
    nrm = lambda k, shape, s: jax.random.normal(k, shape, f32) * s
    return {
        "x": nrm(ks[0], (BATCH, SEQ, D_MODEL), 1.0),
        "c": nrm(ks[1], (BATCH, D_MODEL), 1.0),
        "w_ada": nrm(ks[2], (DEPTH, D_MODEL, N_MOD * D_MODEL), D_MODEL ** -0.5),
        "b_ada": nrm(ks[3], (DEPTH, N_MOD * D_MODEL), 0.02),
        "g_mix": 1.0 + nrm(ks[4], (DEPTH, D_MODEL), 0.02),
        "w_in": nrm(ks[5], (DEPTH, D_MODEL, IN_WIDTH), D_MODEL ** -0.5),
        "b_in": nrm(ks[6], (DEPTH, IN_WIDTH), 0.02),
        "sinks": nrm(ks[7], (DEPTH, N_Q_HEADS), 0.5),
        "conv_w": nrm(ks[8], (DEPTH, CONV_K, CONV_WIDTH), CONV_K ** -0.5),
        "w_out": nrm(ks[9], (DEPTH, D_MODEL, D_MODEL), D_MODEL ** -0.5),
        "g_ffn": 1.0 + nrm(ks[10], (DEPTH, D_MODEL), 0.02),
        "w_ffn_in": nrm(ks[11], (DEPTH, D_MODEL, 2 * D_FF), D_MODEL ** -0.5),
        "w_ffn_out": nrm(ks[12], (DEPTH, D_FF, D_MODEL), D_FF ** -0.5),
        "g_final": 1.0 + nrm(ks[13], (D_MODEL,), 0.02),
    }


def _fwd_reference(x, c, w_ada, b_ada, g_mix, w_in, b_in, sinks, conv_w, w_out,
              g_ffn, w_ffn_in, w_ffn_out, g_final):
    splits = [ATTN_WIDTH,
              ATTN_WIDTH + KV_WIDTH,
              ATTN_WIDTH + 2 * KV_WIDTH,
              ATTN_WIDTH + 2 * KV_WIDTH + CONV_WIDTH,
              ATTN_WIDTH + 2 * KV_WIDTH + 2 * CONV_WIDTH,
              ATTN_WIDTH + 2 * KV_WIDTH + 3 * CONV_WIDTH,
              ATTN_WIDTH + 2 * KV_WIDTH + 3 * CONV_WIDTH + D_MODEL]
    for l in range(DEPTH):
        mod = (jax.nn.silu(c) @ w_ada[l] + b_ada[l])[:, None, :]
        sh1, sc1, ga1, sh2, sc2, ga2 = jnp.split(mod, N_MOD, axis=-1)

        h = rms_norm(x, g_mix[l]) * (1 + sc1) + sh1
        z = h @ w_in[l] + b_in[l]
        q, k, v, conv_b, conv_c, conv_x, gate_a, gate_c = jnp.split(z, splits, axis=-1)
        attn = sliding_window_sink_attention(q, k, v, sinks[l])
        conv = conv_b * causal_depthwise_conv(conv_c * conv_x, conv_w[l])
        merged = jax.nn.sigmoid(gate_a) * attn + jax.nn.sigmoid(gate_c) * conv
        x = x + ga1 * (merged @ w_out[l])

        h = rms_norm(x, g_ffn[l]) * (1 + sc2) + sh2
        gu = h @ w_ffn_in[l]
        g_part, u_part = jnp.split(gu, 2, axis=-1)
        x = x + ga2 * ((jax.nn.silu(g_part) * u_part) @ w_ffn_out[l])
    return rms_norm(x, g_final)


import jax as _jax
import jax.numpy as _jnp

TWIN_FORMAT = 'train_step'
FWD_PARAMS = ['x', 'c', 'w_ada', 'b_ada', 'g_mix', 'w_in', 'b_in', 'sinks', 'conv_w', 'w_out', 'g_ffn', 'w_ffn_in', 'w_ffn_out', 'g_final']
TWIN_WEIGHTS = ['w_ada', 'b_ada', 'g_mix', 'w_in', 'b_in', 'sinks', 'conv_w', 'w_out', 'g_ffn', 'w_ffn_in', 'w_ffn_out', 'g_final']
TWIN_DIFF_INPUT = 'x'
TWIN_INPUTS = ['x', 'c', 'w_ada', 'b_ada', 'g_mix', 'w_in', 'b_in', 'sinks', 'conv_w', 'w_out', 'g_ffn', 'w_ffn_in', 'w_ffn_out', 'g_final', 'loss_target', 'm_w_ada', 'm_b_ada', 'm_g_mix', 'm_w_in', 'm_b_in', 'm_sinks', 'm_conv_w', 'm_w_out', 'm_g_ffn', 'm_w_ffn_in', 'm_w_ffn_out', 'm_g_final', 'v_w_ada', 'v_b_ada', 'v_g_mix', 'v_w_in', 'v_b_in', 'v_sinks', 'v_conv_w', 'v_w_out', 'v_g_ffn', 'v_w_ffn_in', 'v_w_ffn_out', 'v_g_final']
TWIN_OUTPUTS = ['loss', 'grad_x', 'grad_w_ada', 'grad_b_ada', 'grad_g_mix', 'grad_w_in', 'grad_b_in', 'grad_sinks', 'grad_conv_w', 'grad_w_out', 'grad_g_ffn', 'grad_w_ffn_in', 'grad_w_ffn_out', 'grad_g_final', 'delta_w_ada', 'delta_b_ada', 'delta_g_mix', 'delta_w_in', 'delta_b_in', 'delta_sinks', 'delta_conv_w', 'delta_w_out', 'delta_g_ffn', 'delta_w_ffn_in', 'delta_w_ffn_out', 'delta_g_final', 'new_m_w_ada', 'new_m_b_ada', 'new_m_g_mix', 'new_m_w_in', 'new_m_b_in', 'new_m_sinks', 'new_m_conv_w', 'new_m_w_out', 'new_m_g_ffn', 'new_m_w_ffn_in', 'new_m_w_ffn_out', 'new_m_g_final', 'new_v_w_ada', 'new_v_b_ada', 'new_v_g_mix', 'new_v_w_in', 'new_v_b_in', 'new_v_sinks', 'new_v_conv_w', 'new_v_w_out', 'new_v_g_ffn', 'new_v_w_ffn_in', 'new_v_w_ffn_out', 'new_v_g_final']
TWIN_LEAF_KINDS = {'loss': 'loss', 'grad_x': 'grad_x', 'grad_w_ada': 'grad_w', 'grad_b_ada': 'grad_w', 'grad_g_mix': 'grad_w', 'grad_w_in': 'grad_w', 'grad_b_in': 'grad_w', 'grad_sinks': 'grad_w', 'grad_conv_w': 'grad_w', 'grad_w_out': 'grad_w', 'grad_g_ffn': 'grad_w', 'grad_w_ffn_in': 'grad_w', 'grad_w_ffn_out': 'grad_w', 'grad_g_final': 'grad_w', 'delta_w_ada': 'delta_w', 'delta_b_ada': 'delta_w', 'delta_g_mix': 'delta_w', 'delta_w_in': 'delta_w', 'delta_b_in': 'delta_w', 'delta_sinks': 'delta_w', 'delta_conv_w': 'delta_w', 'delta_w_out': 'delta_w', 'delta_g_ffn': 'delta_w', 'delta_w_ffn_in': 'delta_w', 'delta_w_ffn_out': 'delta_w', 'delta_g_final': 'delta_w', 'new_m_w_ada': 'new_m', 'new_m_b_ada': 'new_m', 'new_m_g_mix': 'new_m', 'new_m_w_in': 'new_m', 'new_m_b_in': 'new_m', 'new_m_sinks': 'new_m', 'new_m_conv_w': 'new_m', 'new_m_w_out': 'new_m', 'new_m_g_ffn': 'new_m', 'new_m_w_ffn_in': 'new_m', 'new_m_w_ffn_out': 'new_m', 'new_m_g_final': 'new_m', 'new_v_w_ada': 'new_v', 'new_v_b_ada': 'new_v', 'new_v_g_mix': 'new_v', 'new_v_w_in': 'new_v', 'new_v_b_in': 'new_v', 'new_v_sinks': 'new_v', 'new_v_conv_w': 'new_v', 'new_v_w_out': 'new_v', 'new_v_g_ffn': 'new_v', 'new_v_w_ffn_in': 'new_v', 'new_v_w_ffn_out': 'new_v', 'new_v_g_final': 'new_v'}


def _forward(args):
    return _fwd_reference(*[args[k] for k in FWD_PARAMS])


def _output_shape():
    def fwd():
        inp = _fwd_setup_inputs(0)
        return _fwd_reference(*[inp[k] for k in FWD_PARAMS])
    out = _jax.eval_shape(fwd)
    return out.shape, out.dtype

N_MICROBATCH = 1
ADAM_LR = 0.001
ADAM_B1 = 0.9
ADAM_B2 = 0.999
ADAM_EPS = 1e-08
ADAM_WD = 0.01
ADAM_STEP = 10
PER_EXAMPLE_BATCH_AXIS = {'x': 0, 'c': 0, 'loss_target': 0}
SHARED_INPUTS = []
_WEIGHT_DTYPES = {'w_ada': _jnp.float32, 'b_ada': _jnp.float32, 'g_mix': _jnp.float32, 'w_in': _jnp.float32, 'b_in': _jnp.float32, 'sinks': _jnp.float32, 'conv_w': _jnp.float32, 'w_out': _jnp.float32, 'g_ffn': _jnp.float32, 'w_ffn_in': _jnp.float32, 'w_ffn_out': _jnp.float32, 'g_final': _jnp.float32}
MOMENT_SCALE = {'w_ada': 1.459222e-01, 'b_ada': 2.727871e-01, 'g_mix': 2.646596e-01, 'w_in': 1.179949e-01, 'b_in': 9.607622e-02, 'sinks': 1.139910e-02, 'conv_w': 1.707568e-01, 'w_out': 1.679595e-01, 'g_ffn': 1.597470e-01, 'w_ffn_in': 7.097046e-02, 'w_ffn_out': 1.165017e-01, 'g_final': 6.533154e+01}


def _to_microbatches(a, axis):
    t = _jnp.moveaxis(a, axis, 0)
    t = t.reshape((N_MICROBATCH, t.shape[0] // N_MICROBATCH) + t.shape[1:])
    return _jnp.moveaxis(t, 1, axis + 1)


def setup_inputs(seed: int = 0) -> dict:
    inp = _fwd_setup_inputs(seed)
    key = _jax.random.fold_in(_jax.random.key(seed), 7919)
    shape, _ = _output_shape()
    out = dict(inp)
    out["loss_target"] = _jax.random.normal(_jax.random.fold_in(key, 0), shape, _jnp.float32)
    for i, name in enumerate(TWIN_WEIGHTS):
        w = inp[name].astype(_jnp.float32)
        if MOMENT_SCALE is None:
            s = _jnp.sqrt(_jnp.mean(_jnp.square(w)) + 1e-30)
        else:
            s = MOMENT_SCALE[name]
        km, kv = _jax.random.split(_jax.random.fold_in(key, i + 1))
        out[name] = w
        out["m_" + name] = s * _jax.random.normal(km, w.shape, _jnp.float32)
        out["v_" + name] = (s * s) * _jax.random.uniform(kv, w.shape, _jnp.float32, 0.5, 1.5)
    if N_MICROBATCH > 1:
        for name, axis in PER_EXAMPLE_BATCH_AXIS.items():
            out[name] = _to_microbatches(out[name], axis)
    return {'x': out['x'], 'c': out['c'], 'w_ada': out['w_ada'], 'b_ada': out['b_ada'], 'g_mix': out['g_mix'], 'w_in': out['w_in'], 'b_in': out['b_in'], 'sinks': out['sinks'], 'conv_w': out['conv_w'], 'w_out': out['w_out'], 'g_ffn': out['g_ffn'], 'w_ffn_in': out['w_ffn_in'], 'w_ffn_out': out['w_ffn_out'], 'g_final': out['g_final'], 'loss_target': out['loss_target'], 'm_w_ada': out['m_w_ada'], 'm_b_ada': out['m_b_ada'], 'm_g_mix': out['m_g_mix'], 'm_w_in': out['m_w_in'], 'm_b_in': out['m_b_in'], 'm_sinks': out['m_sinks'], 'm_conv_w': out['m_conv_w'], 'm_w_out': out['m_w_out'], 'm_g_ffn': out['m_g_ffn'], 'm_w_ffn_in': out['m_w_ffn_in'], 'm_w_ffn_out': out['m_w_ffn_out'], 'm_g_final': out['m_g_final'], 'v_w_ada': out['v_w_ada'], 'v_b_ada': out['v_b_ada'], 'v_g_mix': out['v_g_mix'], 'v_w_in': out['v_w_in'], 'v_b_in': out['v_b_in'], 'v_sinks': out['v_sinks'], 'v_conv_w': out['v_conv_w'], 'v_w_out': out['v_w_out'], 'v_g_ffn': out['v_g_ffn'], 'v_w_ffn_in': out['v_w_ffn_in'], 'v_w_ffn_out': out['v_w_ffn_out'], 'v_g_final': out['v_g_final']}


def _loss(weights, diff, rest, loss_target):
    with _jax.named_scope("forward"):
        args = {**rest, TWIN_DIFF_INPUT: diff, **{k: w.astype(_WEIGHT_DTYPES[k]) for k, w in weights.items()}}
        y = _forward(args)
    with _jax.named_scope("loss_head"):
        err = _jnp.square(y.astype(_jnp.float32) - loss_target)
        return 0.5 * _jnp.sum(_jnp.mean(err, axis=-1)) if err.ndim else 0.5 * err


def _adamw(w, g, m, v):
    m = ADAM_B1 * m + (1.0 - ADAM_B1) * g
    v = ADAM_B2 * v + (1.0 - ADAM_B2) * _jnp.square(g)
    m_hat = m / (1.0 - ADAM_B1 ** ADAM_STEP)
    v_hat = v / (1.0 - ADAM_B2 ** ADAM_STEP)
    delta = -ADAM_LR * (m_hat / (_jnp.sqrt(v_hat) + ADAM_EPS) + ADAM_WD * w)
    return delta, m, v


def reference(x, c, w_ada, b_ada, g_mix, w_in, b_in, sinks, conv_w, w_out, g_ffn, w_ffn_in, w_ffn_out, g_final, loss_target, m_w_ada, m_b_ada, m_g_mix, m_w_in, m_b_in, m_sinks, m_conv_w, m_w_out, m_g_ffn, m_w_ffn_in, m_w_ffn_out, m_g_final, v_w_ada, v_b_ada, v_g_mix, v_w_in, v_b_in, v_sinks, v_conv_w, v_w_out, v_g_ffn, v_w_ffn_in, v_w_ffn_out, v_g_final):
    given = dict(x=x, c=c, w_ada=w_ada, b_ada=b_ada, g_mix=g_mix, w_in=w_in, b_in=b_in, sinks=sinks, conv_w=conv_w, w_out=w_out, g_ffn=g_ffn, w_ffn_in=w_ffn_in, w_ffn_out=w_ffn_out, g_final=g_final, loss_target=loss_target, m_w_ada=m_w_ada, m_b_ada=m_b_ada, m_g_mix=m_g_mix, m_w_in=m_w_in, m_b_in=m_b_in, m_sinks=m_sinks, m_conv_w=m_conv_w, m_w_out=m_w_out, m_g_ffn=m_g_ffn, m_w_ffn_in=m_w_ffn_in, m_w_ffn_out=m_w_ffn_out, m_g_final=m_g_final, v_w_ada=v_w_ada, v_b_ada=v_b_ada, v_g_mix=v_g_mix, v_w_in=v_w_in, v_b_in=v_b_in, v_sinks=v_sinks, v_conv_w=v_conv_w, v_w_out=v_w_out, v_g_ffn=v_g_ffn, v_w_ffn_in=v_w_ffn_in, v_w_ffn_out=v_w_ffn_out, v_g_final=v_g_final)
    weights = {n: given[n] for n in TWIN_WEIGHTS}
    shared = {n: given[n] for n in SHARED_INPUTS}
    per_example = {n: given[n] for n in ['x', 'c']}
    grad_fn = _jax.value_and_grad(_loss, argnums=(0, 1))

    def one_microbatch(ex, loss_target):
        ex = dict(ex)
        diff = ex.pop(TWIN_DIFF_INPUT)
        return grad_fn(weights, diff, {**shared, **ex}, loss_target)

    if N_MICROBATCH == 1:
        loss, (grad_w, grad_x) = one_microbatch(per_example, given["loss_target"])
    else:
        def body(carry, xs):
            loss_sum, grad_sum = carry
            l_k, (gw_k, gx_k) = one_microbatch(xs[0], xs[1])
            with _jax.named_scope("update"):
                return (loss_sum + l_k, _jax.tree.map(_jnp.add, grad_sum, gw_k)), gx_k

        init = (_jnp.zeros((), _jnp.float32), _jax.tree.map(_jnp.zeros_like, weights))
        (loss, grad_w), grad_x = _jax.lax.scan(body, init, (per_example, given["loss_target"]))
    with _jax.named_scope("update"):
        delta_w, new_m, new_v = {}, {}, {}
        for n in TWIN_WEIGHTS:
            delta_w[n], new_m[n], new_v[n] = _adamw(weights[n], grad_w[n], given["m_" + n], given["v_" + n])
    return (loss, grad_x, *[grad_w[n] for n in TWIN_WEIGHTS], *[delta_w[n] for n in TWIN_WEIGHTS],
            *[new_m[n] for n in TWIN_WEIGHTS], *[new_v[n] for n in TWIN_WEIGHTS])
```

```python
import functools

import jax
import jax.numpy as jnp
from jax import lax
from jax.experimental import pallas as pl
from jax.experimental.pallas import tpu as pltpu

F32 = jnp.float32
BF16 = jnp.bfloat16
EPS = 1e-6
HEAD_DIM = 64
GROUP = 8
BLOCK = 128
LANES = 128
SUBLANES_BF16 = 16
N_DEV = 8
N_CHIP = 4
VMEM_LIMIT = 56 * 1024 * 1024
MESH = pl.DeviceIdType.MESH

ADAM_LR = 0.001
ADAM_B1 = 0.9
ADAM_B2 = 0.999
ADAM_EPS = 1e-08
ADAM_WD = 0.01
ADAM_STEP = 10

SDS = jax.ShapeDtypeStruct
ANY = pl.BlockSpec(memory_space=pl.ANY)
VMEM_SPEC = pl.BlockSpec(memory_space=pltpu.VMEM)
SMEM_SPEC = pl.BlockSpec(memory_space=pltpu.SMEM)


def _params(*sem):
    return pltpu.CompilerParams(dimension_semantics=sem, vmem_limit_bytes=VMEM_LIMIT)


def _mesh_pos():
    return lax.axis_index("x"), lax.axis_index("y"), lax.axis_index("c")


def _row_tile(rows, cols, itemsize=4, budget=1 << 20, mult=8):
    best = None
    for t in range(mult, rows + 1, mult):
        if rows % t == 0 and t * cols * itemsize <= budget:
            best = t
    if best is None:
        best = rows
    return best


def _all_gather_small(v, name):
    rows, cols = v.shape

    def body(v_ref, out_ref, send_sems, recv_sems, local_sem):
        x, y, c = _mesh_pos()
        me = 4 * x + 2 * y + c
        mine = pltpu.make_async_copy(v_ref, out_ref.at[me], local_sem)
        mine.start()
        peers = []
        for k in range(1, N_DEV):
            px = 1 - x if k & 4 else x
            py = 1 - y if k & 2 else y
            pc = 1 - c if k & 1 else c
            peers.append((px, py, pc))

        def copy(k, block):
            return pltpu.make_async_remote_copy(
                src_ref=v_ref, dst_ref=out_ref.at[block], send_sem=send_sems.at[k], recv_sem=recv_sems.at[k],
                device_id=peers[k], device_id_type=MESH)

        sends = [copy(k, me) for k in range(N_DEV - 1)]
        for cp in sends:
            cp.start()
        for k, (px, py, pc) in enumerate(peers):
            copy(k, 4 * px + 2 * py + pc).wait_recv()
        for cp in sends:
            cp.wait_send()
        mine.wait()

    return pl.pallas_call(
        body, name=name,
        out_shape=SDS((N_DEV, rows, cols), v.dtype),
        in_specs=[VMEM_SPEC], out_specs=VMEM_SPEC,
        scratch_shapes=[pltpu.SemaphoreType.DMA((N_DEV - 1,)), pltpu.SemaphoreType.DMA((N_DEV - 1,)),
                        pltpu.SemaphoreType.DMA],
    )(v)


def _other_chips(x, y):
    return [(1 - x, y), (x, 1 - y), (1 - x, 1 - y)]


def _gather_weights(shards):
    n_w = len(shards)

    def body(*refs):
        ins, outs = refs[:n_w], refs[n_w:2 * n_w]
        send_sems, recv_sems, fsend_sems, frecv_sems, local_sems = refs[2 * n_w:]
        x, y, c = _mesh_pos()
        j_me = 2 * x + y
        chips = _other_chips(x, y)
        sibling = (x, y, 1 - c)
        locals_ = [pltpu.make_async_copy(ins[w], outs[w].at[j_me], local_sems.at[w]) for w in range(n_w)]
        for cp in locals_:
            cp.start()

        def half_rows(w, which):
            half = ins[w].shape[0] // 2
            return pl.ds(pl.multiple_of(which * half, SUBLANES_BF16), half)

        def ici(w, p, block, src=None):
            rows = half_rows(w, c)
            return pltpu.make_async_remote_copy(
                src_ref=ins[w].at[rows] if src is None else src, dst_ref=outs[w].at[block, rows],
                send_sem=send_sems.at[w * 3 + p], recv_sem=recv_sems.at[w * 3 + p],
                device_id=(*chips[p], c), device_id_type=MESH)

        def d2d(w, p, rows):
            block = 2 * chips[p][0] + chips[p][1]
            return pltpu.make_async_remote_copy(
                src_ref=outs[w].at[block, rows], dst_ref=outs[w].at[block, rows],
                send_sem=fsend_sems.at[w * 3 + p], recv_sem=frecv_sems.at[w * 3 + p],
                device_id=sibling, device_id_type=MESH)

        sends = [ici(w, p, j_me) for w in range(n_w) for p in range(3)]
        for cp in sends:
            cp.start()
        forwards = []
        for w in range(n_w):
            for p in range(3):
                ici(w, p, 2 * chips[p][0] + chips[p][1]).wait_recv()
                fw = d2d(w, p, half_rows(w, c))
                fw.start()
                forwards.append(fw)
        for w in range(n_w):
            for p in range(3):
                d2d(w, p, half_rows(w, 1 - c)).wait_recv()
        for cp in sends + forwards:
            cp.wait_send()
        for cp in locals_:
            cp.wait()

    return pl.pallas_call(
        body, name="gather_weights",
        out_shape=[SDS((N_CHIP,) + s.shape, s.dtype) for s in shards],
        in_specs=[ANY] * n_w, out_specs=[ANY] * n_w,
        scratch_shapes=[pltpu.SemaphoreType.DMA((3 * n_w,)), pltpu.SemaphoreType.DMA((3 * n_w,)),
                        pltpu.SemaphoreType.DMA((3 * n_w,)), pltpu.SemaphoreType.DMA((3 * n_w,)),
                        pltpu.SemaphoreType.DMA((n_w,))],
    )(*shards)


def _pair_send(grads):
    n_w = len(grads)

    def body(*refs):
        ins, outs = refs[:n_w], refs[n_w:2 * n_w]
        send_sems, recv_sems = refs[2 * n_w:]
        x, y, c = _mesh_pos()
        copies = []
        for w in range(n_w):
            half = ins[w].shape[1] // 2
            rows = pl.ds(pl.multiple_of((1 - c) * half, 8), half)
            cp = pltpu.make_async_remote_copy(
                src_ref=ins[w].at[:, rows, :], dst_ref=outs[w], send_sem=send_sems.at[w], recv_sem=recv_sems.at[w],
                device_id=(x, y, 1 - c), device_id_type=MESH)
            cp.start()
            copies.append(cp)
        for cp in copies:
            cp.wait()

    return pl.pallas_call(
        body, name="pair_send",
        out_shape=[SDS((N_CHIP, g.shape[1] // 2, g.shape[2]), g.dtype) for g in grads],
        in_specs=[ANY] * n_w, out_specs=[ANY] * n_w,
        scratch_shapes=[pltpu.SemaphoreType.DMA((n_w,)), pltpu.SemaphoreType.DMA((n_w,))],
    )(*grads)


def _chip_send(partials):
    n_w = len(partials)

    def body(*refs):
        ins, outs = refs[:n_w], refs[n_w:2 * n_w]
        send_sems, recv_sems = refs[2 * n_w:]
        x, y, c = _mesh_pos()
        chips = _other_chips(x, y)
        copies = []
        for w in range(n_w):
            for p in range(3):
                cp = pltpu.make_async_remote_copy(
                    src_ref=ins[w].at[2 * chips[p][0] + chips[p][1]], dst_ref=outs[w].at[p],
                    send_sem=send_sems.at[w * 3 + p], recv_sem=recv_sems.at[w * 3 + p],
                    device_id=(*chips[p], c), device_id_type=MESH)
                cp.start()
                copies.append(cp)
        for cp in copies:
            cp.wait()

    return pl.pallas_call(
        body, name="chip_send",
        out_shape=[SDS((3,) + p.shape[1:], p.dtype) for p in partials],
        in_specs=[ANY] * n_w, out_specs=[ANY] * n_w,
        scratch_shapes=[pltpu.SemaphoreType.DMA((3 * n_w,)), pltpu.SemaphoreType.DMA((3 * n_w,))],
    )(*partials)


def _pair_exchange(halves):
    n_w = len(halves)

    def body(*refs):
        ins, outs = refs[:n_w], refs[n_w:2 * n_w]
        send_sems, recv_sems, local_sems = refs[2 * n_w:]
        x, y, c = _mesh_pos()
        copies = []
        for w in range(n_w):
            half = ins[w].shape[0]
            rows = pl.ds(pl.multiple_of(c * half, 8), half)
            lc = pltpu.make_async_copy(ins[w], outs[w].at[rows], local_sems.at[w])
            lc.start()
            cp = pltpu.make_async_remote_copy(
                src_ref=ins[w], dst_ref=outs[w].at[rows], send_sem=send_sems.at[w], recv_sem=recv_sems.at[w],
                device_id=(x, y, 1 - c), device_id_type=MESH)
            cp.start()
            copies.append((lc, cp))
        for lc, cp in copies:
            cp.wait()
            lc.wait()

    return pl.pallas_call(
        body, name="pair_exchange",
        out_shape=[SDS((2 * h.shape[0], h.shape[1]), h.dtype) for h in halves],
        in_specs=[ANY] * n_w, out_specs=[ANY] * n_w,
        scratch_shapes=[pltpu.SemaphoreType.DMA((n_w,)), pltpu.SemaphoreType.DMA((n_w,)),
                        pltpu.SemaphoreType.DMA((n_w,))],
    )(*halves)


def _cast_bf16(w, name):
    rows, cols = w.shape
    tr = _row_tile(rows, cols, mult=SUBLANES_BF16)

    def body(w_ref, o_ref):
        o_ref[...] = w_ref[...].astype(BF16)

    spec = pl.BlockSpec((tr, cols), lambda i: (i, 0))
    return pl.pallas_call(body, name=name, grid=(rows // tr,), in_specs=[spec], out_specs=spec,
                          out_shape=SDS((rows, cols), BF16), compiler_params=_params("parallel"))(w)


def _pair_add(pos, grad, from_sibling, name):
    _, rows, cols = grad.shape
    half = rows // 2
    tr = _row_tile(half, cols, mult=SUBLANES_BF16)
    nblk = half // tr

    def body(pos_ref, g_ref, s_ref, o32_ref, o16_ref):
        del pos_ref
        s = g_ref[...] + s_ref[...]
        o32_ref[...] = s
        o16_ref[...] = s.astype(BF16)

    spec = pl.BlockSpec((None, tr, cols), lambda j, i, pos_ref: (j, i, 0))
    return pl.pallas_call(
        body, name=name,
        grid_spec=pltpu.PrefetchScalarGridSpec(
            num_scalar_prefetch=1, grid=(N_CHIP, nblk),
            in_specs=[pl.BlockSpec((None, tr, cols), lambda j, i, pos_ref: (j, pos_ref[0] * nblk + i, 0)), spec],
            out_specs=[spec, spec]),
        out_shape=[SDS((N_CHIP, half, cols), F32), SDS((N_CHIP, half, cols), BF16)],
        compiler_params=_params("parallel", "parallel"),
    )(pos, grad, from_sibling)


def _chip_add(pos, partial32, from_chips, name):
    _, half, cols = partial32.shape
    tr = _row_tile(half, cols, mult=SUBLANES_BF16)

    def body(pos_ref, p_ref, r_ref, o_ref):
        del pos_ref
        acc = p_ref[...]
        for p in range(3):
            acc = acc + r_ref[p].astype(F32)
        o_ref[...] = acc

    return pl.pallas_call(
        body, name=name,
        grid_spec=pltpu.PrefetchScalarGridSpec(
            num_scalar_prefetch=1, grid=(half // tr,),
            in_specs=[pl.BlockSpec((None, tr, cols), lambda i, pos_ref: (pos_ref[1], i, 0)),
                      pl.BlockSpec((3, tr, cols), lambda i, pos_ref: (0, i, 0))],
            out_specs=pl.BlockSpec((tr, cols), lambda i, pos_ref: (i, 0))),
        out_shape=SDS((half, cols), F32),
        compiler_params=_params("parallel"),
    )(pos, partial32, from_chips)


def _adamw(w, g, m, v, name):
    rows, cols = w.shape
    tr = _row_tile(rows, cols, budget=1 << 19)

    def body(w_ref, g_ref, m_ref, v_ref, d_ref, nm_ref, nv_ref):
        gg = g_ref[...]
        nm = ADAM_B1 * m_ref[...] + (1.0 - ADAM_B1) * gg
        nv = ADAM_B2 * v_ref[...] + (1.0 - ADAM_B2) * (gg * gg)
        m_hat = nm / (1.0 - ADAM_B1 ** ADAM_STEP)
        v_hat = nv / (1.0 - ADAM_B2 ** ADAM_STEP)
        d_ref[...] = -ADAM_LR * (m_hat / (jnp.sqrt(v_hat) + ADAM_EPS) + ADAM_WD * w_ref[...])
        nm_ref[...] = nm
        nv_ref[...] = nv

    spec = pl.BlockSpec((tr, cols), lambda i: (i, 0))
    return pl.pallas_call(body, name=name, grid=(rows // tr,), in_specs=[spec] * 4, out_specs=[spec] * 3,
                          out_shape=[SDS((rows, cols), F32)] * 3, compiler_params=_params("parallel"))(w, g, m, v)


def _pack_sum(gathered):
    _, rows, cols = gathered.shape

    def body(g_ref, o_ref):
        acc = g_ref[0]
        for d in range(1, N_DEV):
            acc = acc + g_ref[d]
        o_ref[...] = acc

    return pl.pallas_call(body, name="pack_sum", in_specs=[VMEM_SPEC], out_specs=VMEM_SPEC,
                          out_shape=SDS((rows, cols), F32))(gathered)


def _ada_fwd(c_all, w_sh, b_sh):
    d, n = w_sh.shape
    tn = 512

    def body(c_ref, w_ref, b_ref, o_ref):
        cc = c_ref[...]
        s = (cc * jax.nn.sigmoid(cc)).astype(BF16)
        o_ref[...] = jnp.dot(s, w_ref[...].astype(BF16), preferred_element_type=F32) + b_ref[...]

    return pl.pallas_call(
        body, name="ada_fwd", grid=(n // tn,),
        in_specs=[pl.BlockSpec((N_DEV, d), lambda j: (0, 0)), pl.BlockSpec((d, tn), lambda j: (0, j)),
                  pl.BlockSpec((1, tn), lambda j: (0, j))],
        out_specs=pl.BlockSpec((N_DEV, tn), lambda j: (0, j)),
        out_shape=SDS((N_DEV, n), F32), compiler_params=_params("parallel"))(c_all, w_sh, b_sh)


def _ada_wgrad(c_all, dmod_sh):
    d = c_all.shape[1]
    n = dmod_sh.shape[1]
    tn = 512

    def body(c_ref, g_ref, o_ref):
        cc = c_ref[...]
        s = cc * jax.nn.sigmoid(cc)
        o_ref[...] = lax.dot_general(s, g_ref[...], (((0,), (0,)), ((), ())), preferred_element_type=F32,
                                     precision=lax.Precision.HIGHEST)

    return pl.pallas_call(
        body, name="ada_wgrad", grid=(n // tn,),
        in_specs=[pl.BlockSpec((N_DEV, d), lambda j: (0, 0)), pl.BlockSpec((N_DEV, tn), lambda j: (0, j))],
        out_specs=pl.BlockSpec((d, tn), lambda j: (0, j)),
        out_shape=SDS((d, n), F32), compiler_params=_params("parallel"))(c_all, dmod_sh)


def _rms(xf):
    return lax.rsqrt(jnp.mean(xf * xf, axis=-1, keepdims=True) + EPS)


def _prenorm(x, g, sc, sh, tm):
    t, d = x.shape

    def body(x_ref, g_ref, sc_ref, sh_ref, h_ref):
        xf = x_ref[...]
        h_ref[...] = ((xf * _rms(xf) * g_ref[...]) * (1.0 + sc_ref[...]) + sh_ref[...]).astype(BF16)

    row = pl.BlockSpec((tm, d), lambda i: (i, 0))
    vec = pl.BlockSpec((1, d), lambda i: (0, 0))
    return pl.pallas_call(body, name="prenorm", grid=(t // tm,), in_specs=[row, vec, vec, vec], out_specs=row,
                          out_shape=SDS((t, d), BF16), compiler_params=_params("parallel"))(x, g, sc, sh)


def _in_proj(h, w, b, tm, tn):
    t, d = h.shape
    n = w.shape[1]

    def body(h_ref, w_ref, b_ref, z_ref):
        z_ref[...] = (jnp.dot(h_ref[...], w_ref[...], preferred_element_type=F32) + b_ref[...]).astype(BF16)

    return pl.pallas_call(
        body, name="in_proj", grid=(n // tn, t // tm),
        in_specs=[pl.BlockSpec((tm, d), lambda j, i: (i, 0)), pl.BlockSpec((d, tn), lambda j, i: (0, j)),
                  pl.BlockSpec((1, tn), lambda j, i: (0, j))],
        out_specs=pl.BlockSpec((tm, tn), lambda j, i: (i, j)),
        out_shape=SDS((t, n), BF16), compiler_params=_params("parallel", "parallel"))(h, w, b)


def _attn_masks():
    rows = 4 * BLOCK
    r = lax.broadcasted_iota(jnp.int32, (rows, 2 * BLOCK), 0) & (BLOCK - 1)
    col = lax.broadcasted_iota(jnp.int32, (rows, 2 * BLOCK), 1)
    return (col > r) & (col <= r + BLOCK), col


def _kv_variants(kv, n_kv_w):
    assert n_kv_w == LANES
    kb, vb = kv[:, :LANES], kv[:, LANES:]
    kr, vr = pltpu.roll(kb, HEAD_DIM, 1), pltpu.roll(vb, HEAD_DIM, 1)
    lane = lax.broadcasted_iota(jnp.int32, kb.shape, 1)
    lo = lane < HEAD_DIM
    zero = jnp.zeros_like(kb)
    k_eff = [[None, None], [None, None]]
    v_eff = [[None, None], [None, None]]
    for h in range(2):
        for e in range(2):
            ksrc, vsrc = (kb, vb) if e == h else (kr, vr)
            keep = lo if e == 0 else jnp.logical_not(lo)
            k_eff[h][e] = jnp.where(keep, ksrc, zero)
            v_eff[h][e] = jnp.where(keep, vsrc, zero)
    return k_eff, v_eff


def _sink_column(sinks_ref, h, e):
    rowblk = lax.broadcasted_iota(jnp.int32, (4 * BLOCK, 1), 0) // BLOCK
    col = jnp.zeros((4 * BLOCK, 1), F32)
    for j in range(4):
        col = jnp.where(rowblk == j, sinks_ref[0, GROUP * h + 2 * j + e], col)
    return col


def _softmax_sink(s, valid, sink):
    s = jnp.where(valid, s, -jnp.inf)
    m = jnp.maximum(jnp.max(s, axis=-1, keepdims=True), sink)
    p = jnp.exp(s - m)
    psink = jnp.exp(sink - m)
    den = jnp.sum(p, axis=-1, keepdims=True) + psink
    inv = 1.0 / den
    return p * inv, psink * inv


def _shift_down(a, s, prev):
    rows = a.shape[0]
    out = pltpu.roll(a, s, 0)
    row = lax.broadcasted_iota(jnp.int32, a.shape, 0)
    for t in range(s):
        out = jnp.where(row == t, prev[SUBLANES_BF16 - s + t:SUBLANES_BF16 - s + t + 1, :], out)
    del rows
    return out


def _shift_up(a, s, nxt):
    rows = a.shape[0]
    out = pltpu.roll(a, rows - s, 0)
    row = lax.broadcasted_iota(jnp.int32, a.shape, 0)
    for t in range(s):
        out = jnp.where(row == rows - s + t, nxt[t:t + 1, :], out)
    return out


def _stack_pairs(ref, h):
    return jnp.concatenate([ref[:, (4 * h + j) * LANES:(4 * h + j + 1) * LANES] for j in range(4)], axis=0)


def _mixer_fwd(z, sinks, conv_w, d):
    t, zw = z.shape
    kvw2 = zw - 6 * d
    nb = t // BLOCK
    kv_col = 6 * d // kvw2
    halo = BLOCK // SUBLANES_BF16

    def body(z_ref, kvp_ref, ccp_ref, cxp_ref, sinks_ref, cw_ref, attn_ref, merged_ref):
        n = pl.program_id(0)
        kv = jnp.concatenate([kvp_ref[...], z_ref[:, 6 * d:]], axis=0)
        k_eff, v_eff = _kv_variants(kv, kvw2 // 2)
        band, col = _attn_masks()
        valid = band & ((n > 0) | (col >= BLOCK))
        for h in range(2):
            q4 = _stack_pairs(z_ref, h)
            o4 = jnp.zeros((4 * BLOCK, LANES), F32)
            for e in range(2):
                s = lax.dot_general(q4, k_eff[h][e], (((1,), (1,)), ((), ())), preferred_element_type=F32)
                p, _ = _softmax_sink(s * (HEAD_DIM ** -0.5), valid, _sink_column(sinks_ref, h, e))
                o4 = o4 + jnp.dot(p.astype(BF16), v_eff[h][e], preferred_element_type=F32)
            for j in range(4):
                attn_ref[:, (4 * h + j) * LANES:(4 * h + j + 1) * LANES] = o4[j * BLOCK:(j + 1) * BLOCK].astype(BF16)
        cb = z_ref[:, d:2 * d].astype(F32)
        p_in = z_ref[:, 2 * d:3 * d].astype(F32) * z_ref[:, 3 * d:4 * d].astype(F32)
        prev = jnp.where(n > 0, ccp_ref[...].astype(F32) * cxp_ref[...].astype(F32), 0.0)
        cconv = (cw_ref[0:1, :] * _shift_down(p_in, 2, prev) + cw_ref[1:2, :] * _shift_down(p_in, 1, prev)
                 + cw_ref[2:3, :] * p_in)
        sa = jax.nn.sigmoid(z_ref[:, 4 * d:5 * d].astype(F32))
        sg = jax.nn.sigmoid(z_ref[:, 5 * d:6 * d].astype(F32))
        merged_ref[...] = (sa * attn_ref[...].astype(F32) + sg * (cb * cconv)).astype(BF16)

    blk = pl.BlockSpec((BLOCK, d), lambda n: (n, 0))
    return pl.pallas_call(
        body, name="mixer_fwd", grid=(nb,),
        in_specs=[pl.BlockSpec((BLOCK, zw), lambda n: (n, 0)),
                  pl.BlockSpec((BLOCK, kvw2), lambda n: (jnp.maximum(n - 1, 0), kv_col)),
                  pl.BlockSpec((SUBLANES_BF16, d), lambda n: (jnp.maximum(n * halo - 1, 0), 2)),
                  pl.BlockSpec((SUBLANES_BF16, d), lambda n: (jnp.maximum(n * halo - 1, 0), 3)),
                  SMEM_SPEC, pl.BlockSpec((3, d), lambda n: (0, 0))],
        out_specs=[blk, blk],
        out_shape=[SDS((t, d), BF16), SDS((t, d), BF16)],
        compiler_params=_params("parallel"))(z, z, z, z, sinks, conv_w)


def _out_proj_fwd(merged, w_out, x, ga1, g_ffn, sc2, sh2, tm):
    t, d = x.shape

    def body(m_ref, w_ref, x_ref, ga_ref, g_ref, sc_ref, sh_ref, y_ref, x1_ref, h_ref):
        y = jnp.dot(m_ref[...], w_ref[...], preferred_element_type=F32)
        x1 = x_ref[...] + ga_ref[...] * y
        y_ref[...] = y.astype(BF16)
        x1_ref[...] = x1
        h_ref[...] = ((x1 * _rms(x1) * g_ref[...]) * (1.0 + sc_ref[...]) + sh_ref[...]).astype(BF16)

    row = pl.BlockSpec((tm, d), lambda i: (i, 0))
    vec = pl.BlockSpec((1, d), lambda i: (0, 0))
    return pl.pallas_call(
        body, name="out_proj_fwd", grid=(t // tm,),
        in_specs=[row, pl.BlockSpec((d, d), lambda i: (0, 0)), row, vec, vec, vec, vec],
        out_specs=[row, row, row],
        out_shape=[SDS((t, d), BF16), SDS((t, d), F32), SDS((t, d), BF16)],
        compiler_params=_params("parallel"))(merged, w_out, x, ga1, g_ffn, sc2, sh2)


def _ffn_in_fwd(h2, w, ff, tm, tn):
    t, d = h2.shape
    nj = ff // tn

    def body(h_ref, wg_ref, wu_ref, gu_ref, act_ref):
        hh = h_ref[...]
        g = jnp.dot(hh, wg_ref[...], preferred_element_type=F32)
        u = jnp.dot(hh, wu_ref[...], preferred_element_type=F32)
        gu_ref[0] = g.astype(BF16)
        gu_ref[1] = u.astype(BF16)
        act_ref[...] = ((g * jax.nn.sigmoid(g)) * u).astype(BF16)

    return pl.pallas_call(
        body, name="ffn_in_fwd", grid=(nj, t // tm),
        in_specs=[pl.BlockSpec((tm, d), lambda j, i: (i, 0)), pl.BlockSpec((d, tn), lambda j, i: (0, j)),
                  pl.BlockSpec((d, tn), lambda j, i: (0, j + nj))],
        out_specs=[pl.BlockSpec((2, tm, tn), lambda j, i: (0, i, j)), pl.BlockSpec((tm, tn), lambda j, i: (i, j))],
        out_shape=[SDS((2, t, ff), BF16), SDS((t, ff), BF16)],
        compiler_params=_params("parallel", "parallel"))(h2, w, w)


def _ffn_out_loss(act, w, x1, target, ga2, g_final, tm):
    t, d = x1.shape
    ff = act.shape[1]

    def body(a_ref, w_ref, x1_ref, tg_ref, ga_ref, gf_ref, dx2_ref, dy2_ref, st_ref):
        @pl.when(pl.program_id(0) == 0)
        def _():
            st_ref[...] = jnp.zeros_like(st_ref)

        y2 = jnp.dot(a_ref[...], w_ref[...], preferred_element_type=F32)
        x2 = x1_ref[...] + ga_ref[...] * y2
        r = _rms(x2)
        yn = x2 * r
        err = yn * gf_ref[...] - tg_ref[...]
        loss = 0.5 * jnp.sum(jnp.mean(err * err, axis=-1, keepdims=True), axis=0, keepdims=True)
        dy = err * (1.0 / d)
        u = dy * gf_ref[...]
        dx2 = r * (u - yn * jnp.mean(u * yn, axis=-1, keepdims=True))
        dx2_ref[...] = dx2
        dy2_ref[...] = (ga_ref[...] * dx2).astype(BF16)
        st_ref[0:1, :] += jnp.sum(dx2 * y2, axis=0, keepdims=True)
        st_ref[1:2, :] += jnp.sum(dy * yn, axis=0, keepdims=True)
        st_ref[2:3, :] += jnp.broadcast_to(loss, (1, d))

    row = pl.BlockSpec((tm, d), lambda i: (i, 0))
    vec = pl.BlockSpec((1, d), lambda i: (0, 0))
    return pl.pallas_call(
        body, name="ffn_out_loss", grid=(t // tm,),
        in_specs=[pl.BlockSpec((tm, ff), lambda i: (i, 0)), pl.BlockSpec((ff, d), lambda i: (0, 0)), row, row,
                  vec, vec],
        out_specs=[row, row, pl.BlockSpec((8, d), lambda i: (0, 0))],
        out_shape=[SDS((t, d), F32), SDS((t, d), BF16), SDS((8, d), F32)],
        compiler_params=_params("arbitrary"))(act, w, x1, target, ga2, g_final)


def _ffn_out_bwd(dy2, w, gu, tm, tn):
    t, d = dy2.shape
    ff = w.shape[0]

    def body(dy_ref, w_ref, gu_ref, o_ref):
        dact = lax.dot_general(dy_ref[...], w_ref[...], (((1,), (1,)), ((), ())), preferred_element_type=F32)
        g = gu_ref[0].astype(F32)
        u = gu_ref[1].astype(F32)
        sg = jax.nn.sigmoid(g)
        o_ref[0] = (dact * u * (sg * (1.0 + g * (1.0 - sg)))).astype(BF16)
        o_ref[1] = (dact * (g * sg)).astype(BF16)

    gu_spec = pl.BlockSpec((2, tm, tn), lambda j, i: (0, i, j))
    return pl.pallas_call(
        body, name="ffn_out_bwd", grid=(ff // tn, t // tm),
        in_specs=[pl.BlockSpec((tm, d), lambda j, i: (i, 0)), pl.BlockSpec((tn, d), lambda j, i: (j, 0)), gu_spec],
        out_specs=gu_spec, out_shape=SDS((2, t, ff), BF16),
        compiler_params=_params("parallel", "parallel"))(dy2, w, gu)


def _wgrad(a, b, a_spec, b_spec, out_spec, out_shape, grid, name):
    def body(a_ref, b_ref, o_ref):
        @pl.when(pl.program_id(len(grid) - 1) == 0)
        def _():
            o_ref[...] = jnp.zeros_like(o_ref)

        o_ref[...] += lax.dot_general(a_ref[...], b_ref[...], (((0,), (0,)), ((), ())), preferred_element_type=F32)

    return pl.pallas_call(
        body, name=name, grid=grid, in_specs=[a_spec, b_spec], out_specs=out_spec, out_shape=out_shape,
        compiler_params=_params(*(["parallel"] * (len(grid) - 1) + ["arbitrary"])))(a, b)


def _ffn_in_bwd(dgu, w, x1, dx2, y1, g_ffn, sc2, ga1, tm, tk):
    t, d = x1.shape
    ff = dgu.shape[2]
    nh = ff // tk
    nk = 2 * nh

    def body(a_ref, w_ref, x1_ref, dx2_ref, y1_ref, g_ref, sc_ref, ga_ref, dx1_ref, dy1_ref, st_ref, acc_ref):
        i, k = pl.program_id(0), pl.program_id(1)

        @pl.when((i == 0) & (k == 0))
        def _():
            st_ref[...] = jnp.zeros_like(st_ref)

        part = lax.dot_general(a_ref[...], w_ref[...], (((1,), (1,)), ((), ())), preferred_element_type=F32)

        @pl.when(k == 0)
        def _():
            acc_ref[...] = part

        @pl.when(k > 0)
        def _():
            acc_ref[...] += part

        @pl.when(k == nk - 1)
        def _():
            dh = acc_ref[...]
            x1 = x1_ref[...]
            r = _rms(x1)
            xn = x1 * r
            g = g_ref[...]
            dn = dh * (1.0 + sc_ref[...])
            u = dn * g
            dx1 = dx2_ref[...] + r * (u - xn * jnp.mean(u * xn, axis=-1, keepdims=True))
            dx1_ref[...] = dx1
            dy1_ref[...] = (ga_ref[...] * dx1).astype(BF16)
            st_ref[0:1, :] += jnp.sum(dh, axis=0, keepdims=True)
            st_ref[1:2, :] += jnp.sum(dh * (xn * g), axis=0, keepdims=True)
            st_ref[2:3, :] += jnp.sum(dn * xn, axis=0, keepdims=True)
            st_ref[3:4, :] += jnp.sum(dx1 * y1_ref[...].astype(F32), axis=0, keepdims=True)

    row = pl.BlockSpec((tm, d), lambda i, k: (i, 0))
    vec = pl.BlockSpec((1, d), lambda i, k: (0, 0))
    return pl.pallas_call(
        body, name="ffn_in_bwd", grid=(t // tm, nk),
        in_specs=[pl.BlockSpec((None, tm, tk), lambda i, k: (k // nh, i, k % nh)),
                  pl.BlockSpec((d, tk), lambda i, k: (0, k)), row, row, row, vec, vec, vec],
        out_specs=[row, row, pl.BlockSpec((8, d), lambda i, k: (0, 0))],
        out_shape=[SDS((t, d), F32), SDS((t, d), BF16), SDS((8, d), F32)],
        scratch_shapes=[pltpu.VMEM((tm, d), F32)],
        compiler_params=_params("arbitrary", "arbitrary"))(dgu, w, x1, dx2, y1, g_ffn, sc2, ga1)


def _out_proj_bwd(dy1, w_out, tm):
    t, d = dy1.shape

    def body(dy_ref, w_ref, o_ref):
        o_ref[...] = lax.dot_general(dy_ref[...], w_ref[...], (((1,), (1,)), ((), ())),
                                     preferred_element_type=F32).astype(BF16)

    row = pl.BlockSpec((tm, d), lambda i: (i, 0))
    return pl.pallas_call(body, name="out_proj_bwd", grid=(t // tm,),
                          in_specs=[row, pl.BlockSpec((d, d), lambda i: (0, 0))], out_specs=row,
                          out_shape=SDS((t, d), BF16), compiler_params=_params("parallel"))(dy1, w_out)


def _mixer_bwd(z, dmerged, attn, sinks, conv_w, d):
    t, zw = z.shape
    kvw2 = zw - 6 * d
    nb = t // BLOCK
    kv_col = 6 * d // kvw2
    halo = BLOCK // SUBLANES_BF16
    last_halo = t // SUBLANES_BF16 - 1
    scale = HEAD_DIM ** -0.5

    def body(z_ref, kvp_ref, ccp_ref, cxp_ref, cbn_ref, gcn_ref, dm_ref, dmn_ref, attn_ref, sinks_ref, cw_ref,
             dz_ref, dkv_ref, db_ref, dbkv_ref, dcw_ref, dsk_ref, carry_ref):
        n = pl.program_id(0)

        @pl.when(n == 0)
        def _():
            carry_ref[...] = jnp.zeros_like(carry_ref)
            db_ref[...] = jnp.zeros_like(db_ref)
            dbkv_ref[...] = jnp.zeros_like(dbkv_ref)
            dcw_ref[...] = jnp.zeros_like(dcw_ref)
            dsk_ref[...] = jnp.zeros_like(dsk_ref)

        @pl.when(n < nb)
        def _():
            dm = dm_ref[...].astype(F32)
            sa = jax.nn.sigmoid(z_ref[:, 4 * d:5 * d].astype(F32))
            att = attn_ref[...].astype(F32)
            dz_ref[:, 4 * d:5 * d] = (dm * att * sa * (1.0 - sa)).astype(BF16)
            db_ref[0:1, 4 * d:5 * d] += jnp.sum(dm * att * sa * (1.0 - sa), axis=0, keepdims=True)
            dattn = (dm * sa).astype(BF16)

            kv = jnp.concatenate([kvp_ref[...], z_ref[:, 6 * d:]], axis=0)
            k_eff, v_eff = _kv_variants(kv, kvw2 // 2)
            band, col = _attn_masks()
            valid = band & ((n > 0) | (col >= BLOCK))
            lane_lo = lax.broadcasted_iota(jnp.int32, (2 * BLOCK, LANES), 1) < HEAD_DIM
            sink_lane = lax.broadcasted_iota(jnp.int32, (1, LANES), 1)
            rowblk = lax.broadcasted_iota(jnp.int32, (4 * BLOCK, 1), 0) // BLOCK
            dk_acc = [jnp.zeros((2 * BLOCK, LANES), F32), jnp.zeros((2 * BLOCK, LANES), F32)]
            dv_acc = [jnp.zeros((2 * BLOCK, LANES), F32), jnp.zeros((2 * BLOCK, LANES), F32)]
            dsink = jnp.zeros((1, LANES), F32)
            for h in range(2):
                q4 = _stack_pairs(z_ref, h)
                do4 = jnp.concatenate([dattn[:, (4 * h + j) * LANES:(4 * h + j + 1) * LANES] for j in range(4)],
                                      axis=0)
                dq4 = jnp.zeros((4 * BLOCK, LANES), F32)
                for e in range(2):
                    s = lax.dot_general(q4, k_eff[h][e], (((1,), (1,)), ((), ())), preferred_element_type=F32)
                    p, psink = _softmax_sink(s * scale, valid, _sink_column(sinks_ref, h, e))
                    dp = lax.dot_general(do4, v_eff[h][e], (((1,), (1,)), ((), ())), preferred_element_type=F32)
                    delta = jnp.sum(p * dp, axis=-1, keepdims=True)
                    ds = (p * (dp - delta) * scale).astype(BF16)
                    dq4 = dq4 + jnp.dot(ds, k_eff[h][e], preferred_element_type=F32)
                    dk = lax.dot_general(ds, q4, (((0,), (0,)), ((), ())), preferred_element_type=F32)
                    dv = lax.dot_general(p.astype(BF16), do4, (((0,), (0,)), ((), ())), preferred_element_type=F32)
                    keep = lane_lo if e == 0 else jnp.logical_not(lane_lo)
                    slot = 0 if e == h else 1
                    dk_acc[slot] = dk_acc[slot] + jnp.where(keep, dk, 0.0)
                    dv_acc[slot] = dv_acc[slot] + jnp.where(keep, dv, 0.0)
                    dsk = -(psink * delta)
                    for j in range(4):
                        tot = jnp.sum(jnp.where(rowblk == j, dsk, 0.0), axis=0, keepdims=True)
                        dsink = dsink + jnp.where(sink_lane == GROUP * h + 2 * j + e, tot, 0.0)
                for j in range(4):
                    cols = slice((4 * h + j) * LANES, (4 * h + j + 1) * LANES)
                    dqj = dq4[j * BLOCK:(j + 1) * BLOCK]
                    dz_ref[:, cols] = dqj.astype(BF16)
                    db_ref[0:1, cols] += jnp.sum(dqj, axis=0, keepdims=True)
            dsk_ref[0:1, :] += dsink
            dkv_new = jnp.concatenate([dk_acc[0] + pltpu.roll(dk_acc[1], HEAD_DIM, 1),
                                       dv_acc[0] + pltpu.roll(dv_acc[1], HEAD_DIM, 1)], axis=1)
            done = carry_ref[...] + dkv_new[:BLOCK]
            dkv_ref[...] = done.astype(BF16)
            dbkv_ref[0:1, :] += jnp.sum(done, axis=0, keepdims=True)
            carry_ref[...] = dkv_new[BLOCK:]

            cb = z_ref[:, d:2 * d].astype(F32)
            cc = z_ref[:, 2 * d:3 * d].astype(F32)
            cx = z_ref[:, 3 * d:4 * d].astype(F32)
            sg = jax.nn.sigmoid(z_ref[:, 5 * d:6 * d].astype(F32))
            p_in = cc * cx
            prev = jnp.where(n > 0, ccp_ref[...].astype(F32) * cxp_ref[...].astype(F32), 0.0)
            p_m1 = _shift_down(p_in, 1, prev)
            p_m2 = _shift_down(p_in, 2, prev)
            w0, w1, w2 = cw_ref[0:1, :], cw_ref[1:2, :], cw_ref[2:3, :]
            cconv = w0 * p_m2 + w1 * p_m1 + w2 * p_in
            dconv = dm * sg
            dgc = dm * (cb * cconv) * sg * (1.0 - sg)
            dcb = dconv * cconv
            dcc_t = dconv * cb
            nxt = jnp.where(n < nb - 1, dmn_ref[...].astype(F32) * jax.nn.sigmoid(gcn_ref[...].astype(F32))
                            * cbn_ref[...].astype(F32), 0.0)
            dpin = w2 * dcc_t + w1 * _shift_up(dcc_t, 1, nxt) + w0 * _shift_up(dcc_t, 2, nxt)
            for seg, val in ((1, dcb), (2, dpin * cx), (3, dpin * cc), (5, dgc)):
                dz_ref[:, seg * d:(seg + 1) * d] = val.astype(BF16)
                db_ref[0:1, seg * d:(seg + 1) * d] += jnp.sum(val, axis=0, keepdims=True)
            dcw_ref[0:1, :] += jnp.sum(dcc_t * p_m2, axis=0, keepdims=True)
            dcw_ref[1:2, :] += jnp.sum(dcc_t * p_m1, axis=0, keepdims=True)
            dcw_ref[2:3, :] += jnp.sum(dcc_t * p_in, axis=0, keepdims=True)

        @pl.when(n == nb)
        def _():
            done = carry_ref[...]
            dkv_ref[...] = done.astype(BF16)
            dbkv_ref[0:1, :] += jnp.sum(done, axis=0, keepdims=True)

    def cur(n):
        return jnp.minimum(n, nb - 1)

    blk = pl.BlockSpec((BLOCK, d), lambda n: (cur(n), 0))
    return pl.pallas_call(
        body, name="mixer_bwd", grid=(nb + 1,),
        in_specs=[pl.BlockSpec((BLOCK, zw), lambda n: (cur(n), 0)),
                  pl.BlockSpec((BLOCK, kvw2), lambda n: (jnp.maximum(cur(n) - 1, 0), kv_col)),
                  pl.BlockSpec((SUBLANES_BF16, d), lambda n: (jnp.maximum(cur(n) * halo - 1, 0), 2)),
                  pl.BlockSpec((SUBLANES_BF16, d), lambda n: (jnp.maximum(cur(n) * halo - 1, 0), 3)),
                  pl.BlockSpec((SUBLANES_BF16, d), lambda n: (jnp.minimum((cur(n) + 1) * halo, last_halo), 1)),
                  pl.BlockSpec((SUBLANES_BF16, d), lambda n: (jnp.minimum((cur(n) + 1) * halo, last_halo), 5)),
                  blk,
                  pl.BlockSpec((SUBLANES_BF16, d), lambda n: (jnp.minimum((cur(n) + 1) * halo, last_halo), 0)),
                  blk, SMEM_SPEC, pl.BlockSpec((3, d), lambda n: (0, 0))],
        out_specs=[pl.BlockSpec((BLOCK, 6 * d), lambda n: (cur(n), 0)),
                   pl.BlockSpec((BLOCK, kvw2), lambda n: (jnp.maximum(n - 1, 0), 0)),
                   pl.BlockSpec((8, 6 * d), lambda n: (0, 0)), pl.BlockSpec((8, kvw2), lambda n: (0, 0)),
                   pl.BlockSpec((8, d), lambda n: (0, 0)), pl.BlockSpec((8, LANES), lambda n: (0, 0))],
        out_shape=[SDS((t, 6 * d), BF16), SDS((t, kvw2), BF16), SDS((8, 6 * d), F32), SDS((8, kvw2), F32),
                   SDS((8, d), F32), SDS((8, LANES), F32)],
        scratch_shapes=[pltpu.VMEM((BLOCK, kvw2), F32)],
        compiler_params=_params("arbitrary"))(z, z, z, z, z, z, dmerged, dmerged, attn, sinks, conv_w)


def _in_proj_bwd(dzm, dkv, w, x, dx1, g_mix, sc1, tm):
    t, d = x.shape
    kvw2 = dkv.shape[1]
    nk = dzm.shape[1] // d
    kv_col = nk * d // kvw2

    def body(a_ref, akv_ref, w_ref, wkv_ref, x_ref, dx1_ref, g_ref, sc_ref, gx_ref, st_ref, acc_ref):
        i, k = pl.program_id(0), pl.program_id(1)

        @pl.when((i == 0) & (k == 0))
        def _():
            st_ref[...] = jnp.zeros_like(st_ref)

        part = lax.dot_general(a_ref[...], w_ref[...], (((1,), (1,)), ((), ())), preferred_element_type=F32)

        @pl.when(k == 0)
        def _():
            acc_ref[...] = part + lax.dot_general(akv_ref[...], wkv_ref[...], (((1,), (1,)), ((), ())),
                                                  preferred_element_type=F32)

        @pl.when(k > 0)
        def _():
            acc_ref[...] += part

        @pl.when(k == nk - 1)
        def _():
            dh = acc_ref[...]
            xx = x_ref[...]
            r = _rms(xx)
            xn = xx * r
            g = g_ref[...]
            dn = dh * (1.0 + sc_ref[...])
            u = dn * g
            gx_ref[...] = dx1_ref[...] + r * (u - xn * jnp.mean(u * xn, axis=-1, keepdims=True))
            st_ref[0:1, :] += jnp.sum(dh, axis=0, keepdims=True)
            st_ref[1:2, :] += jnp.sum(dh * (xn * g), axis=0, keepdims=True)
            st_ref[2:3, :] += jnp.sum(dn * xn, axis=0, keepdims=True)

    row = pl.BlockSpec((tm, d), lambda i, k: (i, 0))
    vec = pl.BlockSpec((1, d), lambda i, k: (0, 0))
    return pl.pallas_call(
        body, name="in_proj_bwd", grid=(t // tm, nk),
        in_specs=[pl.BlockSpec((tm, d), lambda i, k: (i, k)), pl.BlockSpec((tm, kvw2), lambda i, k: (i, 0)),
                  pl.BlockSpec((d, d), lambda i, k: (0, k)), pl.BlockSpec((d, kvw2), lambda i, k: (0, kv_col)),
                  row, row, vec, vec],
        out_specs=[row, pl.BlockSpec((8, d), lambda i, k: (0, 0))],
        out_shape=[SDS((t, d), F32), SDS((8, d), F32)],
        scratch_shapes=[pltpu.VMEM((tm, d), F32)],
        compiler_params=_params("arbitrary", "arbitrary"))(dzm, dkv, w, w, x, dx1, g_mix, sc1)


def _permute_cols(a, d, kvw2):
    return jnp.concatenate([a[..., :d], a[..., d + kvw2:], a[..., d:d + kvw2]], axis=-1)


def _unpermute_cols(a, d, kvw2):
    n = a.shape[-1]
    return jnp.concatenate([a[..., :d], a[..., n - kvw2:], a[..., d:n - kvw2]], axis=-1)


def _to_lanes(v, rows=None):
    flat = v.reshape(-1)
    need = -(-flat.shape[0] // LANES)
    need = -(-need // 8) * 8 if rows is None else rows
    return jnp.pad(flat, (0, need * LANES - flat.shape[0])).reshape(need, LANES)


def kernel(x, c, w_ada, b_ada, g_mix, w_in, b_in, sinks, conv_w, w_out, g_ffn, w_ffn_in, w_ffn_out, g_final, loss_target, m_w_ada, m_b_ada, m_g_mix, m_w_in, m_b_in, m_sinks, m_conv_w, m_w_out, m_g_ffn, m_w_ffn_in, m_w_ffn_out, m_g_final, v_w_ada, v_b_ada, v_g_mix, v_w_in, v_b_in, v_sinks, v_conv_w, v_w_out, v_g_ffn, v_w_ffn_in, v_w_ffn_out, v_g_final):
    xs, tgt = x[0], loss_target[0]
    t, d = xs.shape
    zw = w_in.shape[2] * N_CHIP
    kvw2 = zw - 6 * d
    ff = w_ffn_out.shape[1] * N_CHIP
    n_mod = w_ada.shape[2] * N_CHIP // d
    mod_sh = w_ada.shape[2]
    cw_sh = conv_w.shape[2]
    assert d % (8 * LANES) == 0 and kvw2 == 2 * LANES and t % 512 == 0 and n_mod == 6
    xi, yi, ci = _mesh_pos()
    j_me = 2 * xi + yi
    b_me = 4 * xi + 2 * yi + ci
    pos = jnp.stack([ci, j_me]).astype(jnp.int32)
    tm = 512

    pack1 = jnp.concatenate([c.reshape(d // LANES, LANES), conv_w[0].reshape(-1, LANES)], axis=0)
    pack1 = jnp.pad(pack1, ((0, 16 - pack1.shape[0]), (0, 0)))
    g1 = _all_gather_small(pack1, "gather_c")
    c_all = g1[:, :d // LANES, :].reshape(N_DEV, d)
    cw_rows = 3 * cw_sh // LANES
    conv_w_full = jnp.concatenate(
        [g1[2 * j, d // LANES:d // LANES + cw_rows, :].reshape(3, cw_sh) for j in range(N_CHIP)], axis=1)
    b_ada_sh = lax.dynamic_slice(b_ada, (0, j_me * mod_sh), (1, mod_sh))
    mod_all = _all_gather_small(_ada_fwd(c_all, w_ada[0], b_ada_sh), "gather_mod")
    mod = jnp.concatenate([lax.dynamic_index_in_dim(mod_all[2 * j], b_me, 0, keepdims=True) for j in range(N_CHIP)],
                          axis=1)
    sh1, sc1, ga1, sh2, sc2, ga2 = [mod[:, k * d:(k + 1) * d] for k in range(6)]

    gathered = _gather_weights([_cast_bf16(w_in[0], "cast_w_in"), _cast_bf16(w_out[0], "cast_w_out"),
                                _cast_bf16(w_ffn_in[0], "cast_w_ffn_in"), _cast_bf16(w_ffn_out[0], "cast_w_ffn_out")])
    w_in_p = _permute_cols(gathered[0].transpose(1, 0, 2).reshape(d, zw), d, kvw2)
    b_in_p = _permute_cols(b_in, d, kvw2)
    w_out_f = gathered[1].reshape(d, d)
    w_ffn_in_f = gathered[2].transpose(1, 0, 2).reshape(d, 2 * ff)
    w_ffn_out_f = gathered[3].reshape(ff, d)

    h1 = _prenorm(xs, g_mix, sc1, sh1, tm)
    z = _in_proj(h1, w_in_p, b_in_p, tm, zw // 5)
    attn, merged = _mixer_fwd(z, sinks, conv_w_full, d)
    y1, x1, h2 = _out_proj_fwd(merged, w_out_f, xs, ga1, g_ffn, sc2, sh2, tm)
    gu, act = _ffn_in_fwd(h2, w_ffn_in_f, ff, tm, ff // 2)
    dx2, dy2, st_loss = _ffn_out_loss(act, w_ffn_out_f, x1, tgt, ga2, g_final.reshape(1, d), tm)

    dgu = _ffn_out_bwd(dy2, w_ffn_out_f, gu, tm, ff // 2)
    tk = 512
    dw_ffn_out = _wgrad(
        act, dy2, pl.BlockSpec((tk, ff // 2), lambda m, k: (k, m)), pl.BlockSpec((tk, d), lambda m, k: (k, 0)),
        pl.BlockSpec((ff // 2, d), lambda m, k: (m, 0)), SDS((ff, d), F32), (2, t // tk), "wgrad_ffn_out")
    dx1, dy1, st_ffn = _ffn_in_bwd(dgu, w_ffn_in_f, x1, dx2, y1, g_ffn, sc2, ga1, tm, ff // 2)
    dw_ffn_in = _wgrad(
        h2, dgu, pl.BlockSpec((tk, d), lambda n, k: (k, 0)),
        pl.BlockSpec((None, tk, ff // 2), lambda n, k: (n // 2, k, n % 2)),
        pl.BlockSpec((None, d, ff // 2), lambda n, k: (n, 0, 0)), SDS((N_CHIP, d, ff // 2), F32),
        (N_CHIP, t // tk), "wgrad_ffn_in")
    dmerged = _out_proj_bwd(dy1, w_out_f, tm)
    dw_out = _wgrad(
        merged, dy1, pl.BlockSpec((tk, d), lambda m, k: (k, 0)), pl.BlockSpec((tk, d), lambda m, k: (k, 0)),
        pl.BlockSpec((d, d), lambda m, k: (0, 0)), SDS((d, d), F32), (1, t // tk), "wgrad_out")
    dzm, dkv, db_main, db_kv, dcw, dsk = _mixer_bwd(z, dmerged, attn, sinks, conv_w_full, d)
    grad_x, st_in = _in_proj_bwd(dzm, dkv, w_in_p, xs, dx1, g_mix, sc1, tm)
    dw_in_main = _wgrad(
        h1, dzm, pl.BlockSpec((tk, d), lambda n, k: (k, 0)), pl.BlockSpec((tk, d), lambda n, k: (k, n)),
        pl.BlockSpec((d, d), lambda n, k: (0, n)), SDS((d, 6 * d), F32), (6, t // tk), "wgrad_in_main")
    dw_in_kv = _wgrad(
        h1, dkv, pl.BlockSpec((tk, d), lambda n, k: (k, 0)), pl.BlockSpec((tk, kvw2), lambda n, k: (k, 0)),
        pl.BlockSpec((d, kvw2), lambda n, k: (0, 0)), SDS((d, kvw2), F32), (1, t // tk), "wgrad_in_kv")
    dw_in = _unpermute_cols(jnp.concatenate([dw_in_main, dw_in_kv], axis=1), d, kvw2)
    dw_in = dw_in.reshape(d, N_CHIP, zw // N_CHIP).transpose(1, 0, 2)

    full = [dw_in, dw_out.reshape(N_CHIP, d // N_CHIP, d), dw_ffn_in, dw_ffn_out.reshape(N_CHIP, ff // N_CHIP, d)]
    names = ["w_in", "w_out", "w_ffn_in", "w_ffn_out"]
    from_sibling = _pair_send(full)
    chip32, chip16 = zip(*[_pair_add(pos, g, s, "pair_add_" + nm) for g, s, nm in zip(full, from_sibling, names)])
    from_chips = _chip_send(list(chip16))
    halves = [_chip_add(pos, p, r, "chip_add_" + nm) for p, r, nm in zip(chip32, from_chips, names)]
    g_w_in, g_w_out, g_w_ffn_in, g_w_ffn_out = _pair_exchange(halves)

    dmod = jnp.concatenate([st_in[0:1], st_in[1:2], st_ffn[3:4], st_ffn[0:1], st_ffn[1:2], st_loss[0:1]], axis=1)
    db_in = _unpermute_cols(jnp.concatenate([db_main[0:1], db_kv[0:1]], axis=1), d, kvw2)
    seg = [dmod, st_in[2:3], db_in, dsk[0:1], dcw[0:3].reshape(1, 3 * d), st_ffn[2:3], st_loss[1:2],
           st_loss[2:3, :LANES]]
    sizes = [s.shape[1] for s in seg]
    pack2 = _to_lanes(jnp.concatenate(seg, axis=1))
    tot = _pack_sum(_all_gather_small(pack2, "gather_small_grads")).reshape(-1)
    offs = [sum(sizes[:k]) for k in range(len(sizes))]
    gb_ada, gg_mix, gb_in, gsinks, gcw, gg_ffn, gg_final, loss_v = [tot[o:o + s] for o, s in zip(offs, sizes)]
    loss = loss_v[0]
    gsinks = gsinks[:sinks.shape[1]]
    gcw_sh = lax.dynamic_slice(gcw.reshape(3, d), (0, j_me * cw_sh), (3, cw_sh))

    dmod_all = _all_gather_small(_to_lanes(dmod), "gather_dmod")[:, :n_mod * d // LANES, :].reshape(N_DEV, n_mod * d)
    g_w_ada = _ada_wgrad(c_all, lax.dynamic_slice(dmod_all, (0, j_me * mod_sh), (N_DEV, mod_sh)))

    out_g, out_d, out_m, out_v = {}, {}, {}, {}
    big = {"w_ada": (w_ada, g_w_ada, m_w_ada, v_w_ada), "w_in": (w_in, g_w_in, m_w_in, v_w_in),
           "w_out": (w_out, g_w_out, m_w_out, v_w_out), "w_ffn_in": (w_ffn_in, g_w_ffn_in, m_w_ffn_in, v_w_ffn_in),
           "w_ffn_out": (w_ffn_out, g_w_ffn_out, m_w_ffn_out, v_w_ffn_out)}
    for nm, (w, g, m, v) in big.items():
        dl, nm_, nv_ = _adamw(w[0], g, m[0], v[0], "adamw_" + nm)
        out_g[nm], out_d[nm], out_m[nm], out_v[nm] = g[None], dl[None], nm_[None], nv_[None]
    small = {"b_ada": (b_ada, gb_ada, m_b_ada, v_b_ada), "g_mix": (g_mix, gg_mix, m_g_mix, v_g_mix),
             "b_in": (b_in, gb_in, m_b_in, v_b_in), "sinks": (sinks, gsinks, m_sinks, v_sinks),
             "conv_w": (conv_w, gcw_sh, m_conv_w, v_conv_w), "g_ffn": (g_ffn, gg_ffn, m_g_ffn, v_g_ffn),
             "g_final": (g_final, gg_final, m_g_final, v_g_final)}
    s_sizes = [w.size for w, _, _, _ in small.values()]
    s_rows = -(-sum(s_sizes) // LANES // 8) * 8

    def s_pack(k):
        return _to_lanes(jnp.concatenate([tup[k].reshape(-1) for tup in small.values()]), s_rows)

    s_out = _adamw(s_pack(0), s_pack(1), s_pack(2), s_pack(3), "adamw_small")
    s_off = 0
    for (nm, (w, g, _, _)), sz in zip(small.items(), s_sizes):
        out_g[nm] = g.reshape(w.shape)
        out_d[nm], out_m[nm], out_v[nm] = [o.reshape(-1)[s_off:s_off + sz].reshape(w.shape) for o in s_out]
        s_off += sz

    order = ["w_ada", "b_ada", "g_mix", "w_in", "b_in", "sinks", "conv_w", "w_out", "g_ffn", "w_ffn_in", "w_ffn_out",
             "g_final"]
    return (loss, grad_x[None], *[out_g[k] for k in order], *[out_d[k] for k in order],
            *[out_m[k] for k in order], *[out_v[k] for k in order])
```

```python
import functools

import jax
import jax.numpy as jnp
from jax import lax
from jax.experimental import pallas as pl
from jax.experimental.pallas import tpu as pltpu

F32 = jnp.float32
BF16 = jnp.bfloat16
EPS = 1e-6
HEAD_DIM = 64
GROUP = 8
BLOCK = 128
LANES = 128
SUBLANES_BF16 = 16
N_DEV = 8
N_CHIP = 4
VMEM_LIMIT = 56 * 1024 * 1024
MESH = pl.DeviceIdType.MESH

ADAM_LR = 0.001
ADAM_B1 = 0.9
ADAM_B2 = 0.999
ADAM_EPS = 1e-08
ADAM_WD = 0.01
ADAM_STEP = 10

SDS = jax.ShapeDtypeStruct
ANY = pl.BlockSpec(memory_space=pl.ANY)
VMEM_SPEC = pl.BlockSpec(memory_space=pltpu.VMEM)
SMEM_SPEC = pl.BlockSpec(memory_space=pltpu.SMEM)


def _params(*sem):
    return pltpu.CompilerParams(dimension_semantics=sem, vmem_limit_bytes=VMEM_LIMIT)


def _mesh_pos():
    return lax.axis_index("x"), lax.axis_index("y"), lax.axis_index("c")


def _row_tile(rows, cols, itemsize=4, budget=1 << 20, mult=8):
    best = None
    for t in range(mult, rows + 1, mult):
        if rows % t == 0 and t * cols * itemsize <= budget:
            best = t
    if best is None:
        best = rows
    return best


def _all_gather_small(v, name):
    rows, cols = v.shape

    def body(v_ref, out_ref, send_sems, recv_sems, local_sem):
        x, y, c = _mesh_pos()
        me = 4 * x + 2 * y + c
        mine = pltpu.make_async_copy(v_ref, out_ref.at[me], local_sem)
        mine.start()
        peers = []
        for k in range(1, N_DEV):
            px = 1 - x if k & 4 else x
            py = 1 - y if k & 2 else y
            pc = 1 - c if k & 1 else c
            peers.append((px, py, pc))

        def copy(k, block):
            return pltpu.make_async_remote_copy(
                src_ref=v_ref, dst_ref=out_ref.at[block], send_sem=send_sems.at[k], recv_sem=recv_sems.at[k],
                device_id=peers[k], device_id_type=MESH)

        sends = [copy(k, me) for k in range(N_DEV - 1)]
        for cp in sends:
            cp.start()
        for k, (px, py, pc) in enumerate(peers):
            copy(k, 4 * px + 2 * py + pc).wait_recv()
        for cp in sends:
            cp.wait_send()
        mine.wait()

    return pl.pallas_call(
        body, name=name,
        out_shape=SDS((N_DEV, rows, cols), v.dtype),
        in_specs=[VMEM_SPEC], out_specs=VMEM_SPEC,
        scratch_shapes=[pltpu.SemaphoreType.DMA((N_DEV - 1,)), pltpu.SemaphoreType.DMA((N_DEV - 1,)),
                        pltpu.SemaphoreType.DMA],
    )(v)


def _other_chips(x, y):
    return [(1 - x, y), (x, 1 - y), (1 - x, 1 - y)]


def _gather_weights(bufs):
    n_w = len(bufs)

    def body(*refs):
        outs = refs[n_w:2 * n_w]
        send_sems, recv_sems, fsend_sems, frecv_sems = refs[2 * n_w:]
        x, y, c = _mesh_pos()
        j_me = 2 * x + y
        chips = _other_chips(x, y)
        sibling = (x, y, 1 - c)

        def half_rows(w, which):
            half = outs[w].shape[1] // 2
            return pl.ds(pl.multiple_of(which * half, SUBLANES_BF16), half)

        def copy(w, p, block, rows, over_ici):
            sems = (send_sems, recv_sems) if over_ici else (fsend_sems, frecv_sems)
            return pltpu.make_async_remote_copy(
                src_ref=outs[w].at[block, rows], dst_ref=outs[w].at[block, rows],
                send_sem=sems[0].at[w * 3 + p], recv_sem=sems[1].at[w * 3 + p],
                device_id=(*chips[p], c) if over_ici else sibling, device_id_type=MESH)

        def block_of(p):
            return 2 * chips[p][0] + chips[p][1]

        sends = [copy(w, p, j_me, half_rows(w, c), True) for w in range(n_w) for p in range(3)]
        for cp in sends:
            cp.start()
        forwards = []
        for w in range(n_w):
            for p in range(3):
                copy(w, p, block_of(p), half_rows(w, c), True).wait_recv()
                fw = copy(w, p, block_of(p), half_rows(w, c), False)
                fw.start()
                forwards.append(fw)
        for w in range(n_w):
            for p in range(3):
                copy(w, p, block_of(p), half_rows(w, 1 - c), False).wait_recv()
        for cp in sends + forwards:
            cp.wait_send()

    return pl.pallas_call(
        body, name="gather_weights",
        out_shape=[SDS(b.shape, b.dtype) for b in bufs],
        in_specs=[ANY] * n_w, out_specs=[ANY] * n_w,
        input_output_aliases={w: w for w in range(n_w)},
        scratch_shapes=[pltpu.SemaphoreType.DMA((3 * n_w,)), pltpu.SemaphoreType.DMA((3 * n_w,)),
                        pltpu.SemaphoreType.DMA((3 * n_w,)), pltpu.SemaphoreType.DMA((3 * n_w,))],
    )(*bufs)


def _pair_send(grads):
    n_w = len(grads)

    def body(*refs):
        ins, outs = refs[:n_w], refs[n_w:2 * n_w]
        send_sems, recv_sems = refs[2 * n_w:]
        x, y, c = _mesh_pos()
        copies = []
        for w in range(n_w):
            half = ins[w].shape[1] // 2
            rows = pl.ds(pl.multiple_of((1 - c) * half, 8), half)
            cp = pltpu.make_async_remote_copy(
                src_ref=ins[w].at[:, rows, :], dst_ref=outs[w], send_sem=send_sems.at[w], recv_sem=recv_sems.at[w],
                device_id=(x, y, 1 - c), device_id_type=MESH)
            cp.start()
            copies.append(cp)
        for cp in copies:
            cp.wait()

    return pl.pallas_call(
        body, name="pair_send",
        out_shape=[SDS((N_CHIP, g.shape[1] // 2, g.shape[2]), g.dtype) for g in grads],
        in_specs=[ANY] * n_w, out_specs=[ANY] * n_w,
        scratch_shapes=[pltpu.SemaphoreType.DMA((n_w,)), pltpu.SemaphoreType.DMA((n_w,))],
    )(*grads)


def _chip_send(partials):
    n_w = len(partials)

    def body(*refs):
        ins, outs = refs[:n_w], refs[n_w:2 * n_w]
        send_sems, recv_sems = refs[2 * n_w:]
        x, y, c = _mesh_pos()
        chips = _other_chips(x, y)
        copies = []
        for w in range(n_w):
            for p in range(3):
                cp = pltpu.make_async_remote_copy(
                    src_ref=ins[w].at[2 * chips[p][0] + chips[p][1]], dst_ref=outs[w].at[p],
                    send_sem=send_sems.at[w * 3 + p], recv_sem=recv_sems.at[w * 3 + p],
                    device_id=(*chips[p], c), device_id_type=MESH)
                cp.start()
                copies.append(cp)
        for cp in copies:
            cp.wait()

    return pl.pallas_call(
        body, name="chip_send",
        out_shape=[SDS((3,) + p.shape[1:], p.dtype) for p in partials],
        in_specs=[ANY] * n_w, out_specs=[ANY] * n_w,
        scratch_shapes=[pltpu.SemaphoreType.DMA((3 * n_w,)), pltpu.SemaphoreType.DMA((3 * n_w,))],
    )(*partials)


def _pair_exchange(fulls):
    n_w = len(fulls)

    def body(*refs):
        outs = refs[n_w:2 * n_w]
        send_sems, recv_sems = refs[2 * n_w:]
        x, y, c = _mesh_pos()
        copies = []
        for w in range(n_w):
            half = outs[w].shape[0] // 2
            rows = pl.ds(pl.multiple_of(c * half, 8), half)
            cp = pltpu.make_async_remote_copy(
                src_ref=outs[w].at[rows], dst_ref=outs[w].at[rows], send_sem=send_sems.at[w],
                recv_sem=recv_sems.at[w], device_id=(x, y, 1 - c), device_id_type=MESH)
            cp.start()
            copies.append(cp)
        for cp in copies:
            cp.wait()

    return pl.pallas_call(
        body, name="pair_exchange",
        out_shape=[SDS(f.shape, f.dtype) for f in fulls],
        in_specs=[ANY] * n_w, out_specs=[ANY] * n_w,
        input_output_aliases={w: w for w in range(n_w)},
        scratch_shapes=[pltpu.SemaphoreType.DMA((n_w,)), pltpu.SemaphoreType.DMA((n_w,))],
    )(*fulls)


def _cast_into_block(pos, w, name):
    rows, cols = w.shape
    tr = _row_tile(rows, cols, mult=SUBLANES_BF16)

    def body(pos_ref, w_ref, o_ref):
        del pos_ref
        o_ref[...] = w_ref[...].astype(BF16)

    return pl.pallas_call(
        body, name=name,
        grid_spec=pltpu.PrefetchScalarGridSpec(
            num_scalar_prefetch=1, grid=(rows // tr,),
            in_specs=[pl.BlockSpec((tr, cols), lambda i, pos_ref: (i, 0))],
            out_specs=pl.BlockSpec((None, tr, cols), lambda i, pos_ref: (pos_ref[1], i, 0))),
        out_shape=SDS((N_CHIP, rows, cols), BF16), compiler_params=_params("parallel"))(pos, w)


def _pair_add(pos, grad, from_sibling, name):
    _, rows, cols = grad.shape
    half = rows // 2
    tr = _row_tile(half, cols, mult=SUBLANES_BF16)
    nblk = half // tr

    def body(pos_ref, g_ref, s_ref, o32_ref, o16_ref):
        del pos_ref
        s = g_ref[...] + s_ref[...]
        o32_ref[...] = s
        o16_ref[...] = s.astype(BF16)

    spec = pl.BlockSpec((None, tr, cols), lambda j, i, pos_ref: (j, i, 0))
    return pl.pallas_call(
        body, name=name,
        grid_spec=pltpu.PrefetchScalarGridSpec(
            num_scalar_prefetch=1, grid=(N_CHIP, nblk),
            in_specs=[pl.BlockSpec((None, tr, cols), lambda j, i, pos_ref: (j, pos_ref[0] * nblk + i, 0)), spec],
            out_specs=[spec, spec]),
        out_shape=[SDS((N_CHIP, half, cols), F32), SDS((N_CHIP, half, cols), BF16)],
        compiler_params=_params("parallel", "parallel"),
    )(pos, grad, from_sibling)


def _chip_add(pos, partial32, from_chips, name):
    _, half, cols = partial32.shape
    tr = _row_tile(half, cols, mult=SUBLANES_BF16)

    def body(pos_ref, p_ref, r_ref, o_ref):
        del pos_ref
        acc = p_ref[...]
        for p in range(3):
            acc = acc + r_ref[p].astype(F32)
        o_ref[...] = acc

    return pl.pallas_call(
        body, name=name,
        grid_spec=pltpu.PrefetchScalarGridSpec(
            num_scalar_prefetch=1, grid=(half // tr,),
            in_specs=[pl.BlockSpec((None, tr, cols), lambda i, pos_ref: (pos_ref[1], i, 0)),
                      pl.BlockSpec((3, tr, cols), lambda i, pos_ref: (0, i, 0))],
            out_specs=pl.BlockSpec((tr, cols), lambda i, pos_ref: (pos_ref[0] * (half // tr) + i, 0))),
        out_shape=SDS((2 * half, cols), F32),
        compiler_params=_params("parallel"),
    )(pos, partial32, from_chips)


def _adamw(w, g, m, v, name):
    rows, cols = w.shape
    tr = _row_tile(rows, cols, budget=1 << 19)

    def body(w_ref, g_ref, m_ref, v_ref, d_ref, nm_ref, nv_ref):
        gg = g_ref[...]
        nm = ADAM_B1 * m_ref[...] + (1.0 - ADAM_B1) * gg
        nv = ADAM_B2 * v_ref[...] + (1.0 - ADAM_B2) * (gg * gg)
        m_hat = nm / (1.0 - ADAM_B1 ** ADAM_STEP)
        v_hat = nv / (1.0 - ADAM_B2 ** ADAM_STEP)
        d_ref[...] = -ADAM_LR * (m_hat / (jnp.sqrt(v_hat) + ADAM_EPS) + ADAM_WD * w_ref[...])
        nm_ref[...] = nm
        nv_ref[...] = nv

    spec = pl.BlockSpec((tr, cols), lambda i: (i, 0))
    return pl.pallas_call(body, name=name, grid=(rows // tr,), in_specs=[spec] * 4, out_specs=[spec] * 3,
                          out_shape=[SDS((rows, cols), F32)] * 3, compiler_params=_params("parallel"))(w, g, m, v)


def _pack_sum(gathered):
    _, rows, cols = gathered.shape

    def body(g_ref, o_ref):
        acc = g_ref[0]
        for d in range(1, N_DEV):
            acc = acc + g_ref[d]
        o_ref[...] = acc

    return pl.pallas_call(body, name="pack_sum", in_specs=[VMEM_SPEC], out_specs=VMEM_SPEC,
                          out_shape=SDS((rows, cols), F32))(gathered)


def _ada_fwd(c_all, w_sh, b_sh):
    d, n = w_sh.shape
    tn = 512

    def body(c_ref, w_ref, b_ref, o_ref):
        cc = c_ref[...]
        s = (cc * jax.nn.sigmoid(cc)).astype(BF16)
        o_ref[...] = jnp.dot(s, w_ref[...].astype(BF16), preferred_element_type=F32) + b_ref[...]

    return pl.pallas_call(
        body, name="ada_fwd", grid=(n // tn,),
        in_specs=[pl.BlockSpec((N_DEV, d), lambda j: (0, 0)), pl.BlockSpec((d, tn), lambda j: (0, j)),
                  pl.BlockSpec((1, tn), lambda j: (0, j))],
        out_specs=pl.BlockSpec((N_DEV, tn), lambda j: (0, j)),
        out_shape=SDS((N_DEV, n), F32), compiler_params=_params("parallel"))(c_all, w_sh, b_sh)


def _ada_wgrad(c_all, dmod_sh):
    d = c_all.shape[1]
    n = dmod_sh.shape[1]
    tn = 512

    def body(c_ref, g_ref, o_ref):
        cc = c_ref[...]
        s = cc * jax.nn.sigmoid(cc)
        o_ref[...] = lax.dot_general(s, g_ref[...], (((0,), (0,)), ((), ())), preferred_element_type=F32,
                                     precision=lax.Precision.HIGHEST)

    return pl.pallas_call(
        body, name="ada_wgrad", grid=(n // tn,),
        in_specs=[pl.BlockSpec((N_DEV, d), lambda j: (0, 0)), pl.BlockSpec((N_DEV, tn), lambda j: (0, j))],
        out_specs=pl.BlockSpec((d, tn), lambda j: (0, j)),
        out_shape=SDS((d, n), F32), compiler_params=_params("parallel"))(c_all, dmod_sh)


def _rms(xf):
    return lax.rsqrt(jnp.mean(xf * xf, axis=-1, keepdims=True) + EPS)


def _prenorm(x, g, sc, sh, tm):
    t, d = x.shape

    def body(x_ref, g_ref, sc_ref, sh_ref, h_ref):
        xf = x_ref[...]
        h_ref[...] = ((xf * _rms(xf) * g_ref[...]) * (1.0 + sc_ref[...]) + sh_ref[...]).astype(BF16)

    row = pl.BlockSpec((tm, d), lambda i: (i, 0))
    vec = pl.BlockSpec((1, d), lambda i: (0, 0))
    return pl.pallas_call(body, name="prenorm", grid=(t // tm,), in_specs=[row, vec, vec, vec], out_specs=row,
                          out_shape=SDS((t, d), BF16), compiler_params=_params("parallel"))(x, g, sc, sh)


def _in_proj(h, w, b, tm, tn):
    t, d = h.shape
    n = w.shape[1]

    def body(h_ref, w_ref, b_ref, z_ref):
        z_ref[...] = (jnp.dot(h_ref[...], w_ref[...], preferred_element_type=F32) + b_ref[...]).astype(BF16)

    return pl.pallas_call(
        body, name="in_proj", grid=(n // tn, t // tm),
        in_specs=[pl.BlockSpec((tm, d), lambda j, i: (i, 0)), pl.BlockSpec((d, tn), lambda j, i: (0, j)),
                  pl.BlockSpec((1, tn), lambda j, i: (0, j))],
        out_specs=pl.BlockSpec((tm, tn), lambda j, i: (i, j)),
        out_shape=SDS((t, n), BF16), compiler_params=_params("parallel", "parallel"))(h, w, b)


def _attn_masks():
    rows = 4 * BLOCK
    r = lax.broadcasted_iota(jnp.int32, (rows, 2 * BLOCK), 0) & (BLOCK - 1)
    col = lax.broadcasted_iota(jnp.int32, (rows, 2 * BLOCK), 1)
    return (col > r) & (col <= r + BLOCK), col


def _kv_variants(kv, n_kv_w):
    assert n_kv_w == LANES
    kb, vb = kv[:, :LANES], kv[:, LANES:]
    kr, vr = pltpu.roll(kb, HEAD_DIM, 1), pltpu.roll(vb, HEAD_DIM, 1)
    lane = lax.broadcasted_iota(jnp.int32, kb.shape, 1)
    lo = lane < HEAD_DIM
    zero = jnp.zeros_like(kb)
    k_eff = [[None, None], [None, None]]
    v_eff = [[None, None], [None, None]]
    for h in range(2):
        for e in range(2):
            ksrc, vsrc = (kb, vb) if e == h else (kr, vr)
            keep = lo if e == 0 else jnp.logical_not(lo)
            k_eff[h][e] = jnp.where(keep, ksrc, zero)
            v_eff[h][e] = jnp.where(keep, vsrc, zero)
    return k_eff, v_eff


def _sink_column(sinks_ref, h, e):
    rowblk = lax.broadcasted_iota(jnp.int32, (4 * BLOCK, 1), 0) // BLOCK
    col = jnp.zeros((4 * BLOCK, 1), F32)
    for j in range(4):
        col = jnp.where(rowblk == j, sinks_ref[0, GROUP * h + 2 * j + e], col)
    return col


def _softmax_sink(s, valid, sink):
    s = jnp.where(valid, s, -jnp.inf)
    m = jnp.maximum(jnp.max(s, axis=-1, keepdims=True), sink)
    p = jnp.exp(s - m)
    psink = jnp.exp(sink - m)
    den = jnp.sum(p, axis=-1, keepdims=True) + psink
    inv = 1.0 / den
    return p * inv, psink * inv


def _shift_down(a, s, prev):
    rows = a.shape[0]
    out = pltpu.roll(a, s, 0)
    row = lax.broadcasted_iota(jnp.int32, a.shape, 0)
    for t in range(s):
        out = jnp.where(row == t, prev[SUBLANES_BF16 - s + t:SUBLANES_BF16 - s + t + 1, :], out)
    del rows
    return out


def _shift_up(a, s, nxt):
    rows = a.shape[0]
    out = pltpu.roll(a, rows - s, 0)
    row = lax.broadcasted_iota(jnp.int32, a.shape, 0)
    for t in range(s):
        out = jnp.where(row == rows - s + t, nxt[t:t + 1, :], out)
    return out


def _stack_pairs(ref, h):
    return jnp.concatenate([ref[:, (4 * h + j) * LANES:(4 * h + j + 1) * LANES] for j in range(4)], axis=0)


def _mixer_fwd(z, sinks, conv_w, d):
    t, zw = z.shape
    kvw2 = zw - 6 * d
    nb = t // BLOCK
    kv_col = 6 * d // kvw2
    halo = BLOCK // SUBLANES_BF16

    def body(z_ref, kvp_ref, ccp_ref, cxp_ref, sinks_ref, cw_ref, attn_ref, merged_ref):
        n = pl.program_id(0)
        kv = jnp.concatenate([kvp_ref[...], z_ref[:, 6 * d:]], axis=0)
        k_eff, v_eff = _kv_variants(kv, kvw2 // 2)
        band, col = _attn_masks()
        valid = band & ((n > 0) | (col >= BLOCK))
        for h in range(2):
            q4 = _stack_pairs(z_ref, h)
            o4 = jnp.zeros((4 * BLOCK, LANES), F32)
            for e in range(2):
                s = lax.dot_general(q4, k_eff[h][e], (((1,), (1,)), ((), ())), preferred_element_type=F32)
                p, _ = _softmax_sink(s * (HEAD_DIM ** -0.5), valid, _sink_column(sinks_ref, h, e))
                o4 = o4 + jnp.dot(p.astype(BF16), v_eff[h][e], preferred_element_type=F32)
            for j in range(4):
                attn_ref[:, (4 * h + j) * LANES:(4 * h + j + 1) * LANES] = o4[j * BLOCK:(j + 1) * BLOCK].astype(BF16)
        cb = z_ref[:, d:2 * d].astype(F32)
        p_in = z_ref[:, 2 * d:3 * d].astype(F32) * z_ref[:, 3 * d:4 * d].astype(F32)
        prev = jnp.where(n > 0, ccp_ref[...].astype(F32) * cxp_ref[...].astype(F32), 0.0)
        cconv = (cw_ref[0:1, :] * _shift_down(p_in, 2, prev) + cw_ref[1:2, :] * _shift_down(p_in, 1, prev)
                 + cw_ref[2:3, :] * p_in)
        sa = jax.nn.sigmoid(z_ref[:, 4 * d:5 * d].astype(F32))
        sg = jax.nn.sigmoid(z_ref[:, 5 * d:6 * d].astype(F32))
        merged_ref[...] = (sa * attn_ref[...].astype(F32) + sg * (cb * cconv)).astype(BF16)

    blk = pl.BlockSpec((BLOCK, d), lambda n: (n, 0))
    return pl.pallas_call(
        body, name="mixer_fwd", grid=(nb,),
        in_specs=[pl.BlockSpec((BLOCK, zw), lambda n: (n, 0)),
                  pl.BlockSpec((BLOCK, kvw2), lambda n: (jnp.maximum(n - 1, 0), kv_col)),
                  pl.BlockSpec((SUBLANES_BF16, d), lambda n: (jnp.maximum(n * halo - 1, 0), 2)),
                  pl.BlockSpec((SUBLANES_BF16, d), lambda n: (jnp.maximum(n * halo - 1, 0), 3)),
                  SMEM_SPEC, pl.BlockSpec((3, d), lambda n: (0, 0))],
        out_specs=[blk, blk],
        out_shape=[SDS((t, d), BF16), SDS((t, d), BF16)],
        compiler_params=_params("parallel"))(z, z, z, z, sinks, conv_w)


def _out_proj_fwd(merged, w_out, x, ga1, g_ffn, sc2, sh2, tm):
    t, d = x.shape

    def body(m_ref, w_ref, x_ref, ga_ref, g_ref, sc_ref, sh_ref, y_ref, x1_ref, h_ref):
        y = jnp.dot(m_ref[...], w_ref[...], preferred_element_type=F32)
        x1 = x_ref[...] + ga_ref[...] * y
        y_ref[...] = y.astype(BF16)
        x1_ref[...] = x1
        h_ref[...] = ((x1 * _rms(x1) * g_ref[...]) * (1.0 + sc_ref[...]) + sh_ref[...]).astype(BF16)

    row = pl.BlockSpec((tm, d), lambda i: (i, 0))
    vec = pl.BlockSpec((1, d), lambda i: (0, 0))
    return pl.pallas_call(
        body, name="out_proj_fwd", grid=(t // tm,),
        in_specs=[row, pl.BlockSpec((d, d), lambda i: (0, 0)), row, vec, vec, vec, vec],
        out_specs=[row, row, row],
        out_shape=[SDS((t, d), BF16), SDS((t, d), F32), SDS((t, d), BF16)],
        compiler_params=_params("parallel"))(merged, w_out, x, ga1, g_ffn, sc2, sh2)


def _ffn_in_fwd(h2, w, ff, tm, tn):
    t, d = h2.shape
    nj = ff // tn

    def body(h_ref, wg_ref, wu_ref, gu_ref, act_ref):
        hh = h_ref[...]
        g = jnp.dot(hh, wg_ref[...], preferred_element_type=F32)
        u = jnp.dot(hh, wu_ref[...], preferred_element_type=F32)
        gu_ref[0] = g.astype(BF16)
        gu_ref[1] = u.astype(BF16)
        act_ref[...] = ((g * jax.nn.sigmoid(g)) * u).astype(BF16)

    return pl.pallas_call(
        body, name="ffn_in_fwd", grid=(nj, t // tm),
        in_specs=[pl.BlockSpec((tm, d), lambda j, i: (i, 0)), pl.BlockSpec((d, tn), lambda j, i: (0, j)),
                  pl.BlockSpec((d, tn), lambda j, i: (0, j + nj))],
        out_specs=[pl.BlockSpec((2, tm, tn), lambda j, i: (0, i, j)), pl.BlockSpec((tm, tn), lambda j, i: (i, j))],
        out_shape=[SDS((2, t, ff), BF16), SDS((t, ff), BF16)],
        compiler_params=_params("parallel", "parallel"))(h2, w, w)


def _ffn_out_loss(act, w, x1, target, ga2, g_final, tm):
    t, d = x1.shape
    ff = act.shape[1]

    def body(a_ref, w_ref, x1_ref, tg_ref, ga_ref, gf_ref, dx2_ref, dy2_ref, st_ref):
        @pl.when(pl.program_id(0) == 0)
        def _():
            st_ref[...] = jnp.zeros_like(st_ref)

        y2 = jnp.dot(a_ref[...], w_ref[...], preferred_element_type=F32)
        x2 = x1_ref[...] + ga_ref[...] * y2
        r = _rms(x2)
        yn = x2 * r
        err = yn * gf_ref[...] - tg_ref[...]
        loss = 0.5 * jnp.sum(jnp.mean(err * err, axis=-1, keepdims=True), axis=0, keepdims=True)
        dy = err * (1.0 / d)
        u = dy * gf_ref[...]
        dx2 = r * (u - yn * jnp.mean(u * yn, axis=-1, keepdims=True))
        dx2_ref[...] = dx2
        dy2_ref[...] = (ga_ref[...] * dx2).astype(BF16)
        st_ref[0:1, :] += jnp.sum(dx2 * y2, axis=0, keepdims=True)
        st_ref[1:2, :] += jnp.sum(dy * yn, axis=0, keepdims=True)
        st_ref[2:3, :] += jnp.broadcast_to(loss, (1, d))

    row = pl.BlockSpec((tm, d), lambda i: (i, 0))
    vec = pl.BlockSpec((1, d), lambda i: (0, 0))
    return pl.pallas_call(
        body, name="ffn_out_loss", grid=(t // tm,),
        in_specs=[pl.BlockSpec((tm, ff), lambda i: (i, 0)), pl.BlockSpec((ff, d), lambda i: (0, 0)), row, row,
                  vec, vec],
        out_specs=[row, row, pl.BlockSpec((8, d), lambda i: (0, 0))],
        out_shape=[SDS((t, d), F32), SDS((t, d), BF16), SDS((8, d), F32)],
        compiler_params=_params("arbitrary"))(act, w, x1, target, ga2, g_final)


def _ffn_out_bwd(dy2, w, gu, tm, tn):
    t, d = dy2.shape
    ff = w.shape[0]

    def body(dy_ref, w_ref, gu_ref, o_ref):
        dact = lax.dot_general(dy_ref[...], w_ref[...], (((1,), (1,)), ((), ())), preferred_element_type=F32)
        g = gu_ref[0].astype(F32)
        u = gu_ref[1].astype(F32)
        sg = jax.nn.sigmoid(g)
        o_ref[0] = (dact * u * (sg * (1.0 + g * (1.0 - sg)))).astype(BF16)
        o_ref[1] = (dact * (g * sg)).astype(BF16)

    gu_spec = pl.BlockSpec((2, tm, tn), lambda j, i: (0, i, j))
    return pl.pallas_call(
        body, name="ffn_out_bwd", grid=(ff // tn, t // tm),
        in_specs=[pl.BlockSpec((tm, d), lambda j, i: (i, 0)), pl.BlockSpec((tn, d), lambda j, i: (j, 0)), gu_spec],
        out_specs=gu_spec, out_shape=SDS((2, t, ff), BF16),
        compiler_params=_params("parallel", "parallel"))(dy2, w, gu)


def _wgrad(a, b, a_spec, b_spec, out_spec, out_shape, grid, name):
    def body(a_ref, b_ref, o_ref):
        @pl.when(pl.program_id(len(grid) - 1) == 0)
        def _():
            o_ref[...] = jnp.zeros_like(o_ref)

        o_ref[...] += lax.dot_general(a_ref[...], b_ref[...], (((0,), (0,)), ((), ())), preferred_element_type=F32)

    return pl.pallas_call(
        body, name=name, grid=grid, in_specs=[a_spec, b_spec], out_specs=out_spec, out_shape=out_shape,
        compiler_params=_params(*(["parallel"] * (len(grid) - 1) + ["arbitrary"])))(a, b)


def _ffn_in_bwd(dgu, w, x1, dx2, y1, g_ffn, sc2, ga1, tm, tk):
    t, d = x1.shape
    ff = dgu.shape[2]
    nh = ff // tk
    nk = 2 * nh

    def body(a_ref, w_ref, x1_ref, dx2_ref, y1_ref, g_ref, sc_ref, ga_ref, dx1_ref, dy1_ref, st_ref, acc_ref):
        i, k = pl.program_id(0), pl.program_id(1)

        @pl.when((i == 0) & (k == 0))
        def _():
            st_ref[...] = jnp.zeros_like(st_ref)

        part = lax.dot_general(a_ref[...], w_ref[...], (((1,), (1,)), ((), ())), preferred_element_type=F32)

        @pl.when(k == 0)
        def _():
            acc_ref[...] = part

        @pl.when(k > 0)
        def _():
            acc_ref[...] += part

        @pl.when(k == nk - 1)
        def _():
            dh = acc_ref[...]
            x1 = x1_ref[...]
            r = _rms(x1)
            xn = x1 * r
            g = g_ref[...]
            dn = dh * (1.0 + sc_ref[...])
            u = dn * g
            dx1 = dx2_ref[...] + r * (u - xn * jnp.mean(u * xn, axis=-1, keepdims=True))
            dx1_ref[...] = dx1
            dy1_ref[...] = (ga_ref[...] * dx1).astype(BF16)
            st_ref[0:1, :] += jnp.sum(dh, axis=0, keepdims=True)
            st_ref[1:2, :] += jnp.sum(dh * (xn * g), axis=0, keepdims=True)
            st_ref[2:3, :] += jnp.sum(dn * xn, axis=0, keepdims=True)
            st_ref[3:4, :] += jnp.sum(dx1 * y1_ref[...].astype(F32), axis=0, keepdims=True)

    row = pl.BlockSpec((tm, d), lambda i, k: (i, 0))
    vec = pl.BlockSpec((1, d), lambda i, k: (0, 0))
    return pl.pallas_call(
        body, name="ffn_in_bwd", grid=(t // tm, nk),
        in_specs=[pl.BlockSpec((None, tm, tk), lambda i, k: (k // nh, i, k % nh)),
                  pl.BlockSpec((d, tk), lambda i, k: (0, k)), row, row, row, vec, vec, vec],
        out_specs=[row, row, pl.BlockSpec((8, d), lambda i, k: (0, 0))],
        out_shape=[SDS((t, d), F32), SDS((t, d), BF16), SDS((8, d), F32)],
        scratch_shapes=[pltpu.VMEM((tm, d), F32)],
        compiler_params=_params("arbitrary", "arbitrary"))(dgu, w, x1, dx2, y1, g_ffn, sc2, ga1)


def _out_proj_bwd(dy1, w_out, tm):
    t, d = dy1.shape

    def body(dy_ref, w_ref, o_ref):
        o_ref[...] = lax.dot_general(dy_ref[...], w_ref[...], (((1,), (1,)), ((), ())),
                                     preferred_element_type=F32).astype(BF16)

    row = pl.BlockSpec((tm, d), lambda i: (i, 0))
    return pl.pallas_call(body, name="out_proj_bwd", grid=(t // tm,),
                          in_specs=[row, pl.BlockSpec((d, d), lambda i: (0, 0))], out_specs=row,
                          out_shape=SDS((t, d), BF16), compiler_params=_params("parallel"))(dy1, w_out)


def _mixer_bwd(z, dmerged, attn, sinks, conv_w, d):
    t, zw = z.shape
    kvw2 = zw - 6 * d
    nb = t // BLOCK
    kv_col = 6 * d // kvw2
    halo = BLOCK // SUBLANES_BF16
    last_halo = t // SUBLANES_BF16 - 1
    scale = HEAD_DIM ** -0.5

    def body(z_ref, kvp_ref, ccp_ref, cxp_ref, cbn_ref, gcn_ref, dm_ref, dmn_ref, attn_ref, sinks_ref, cw_ref,
             dz_ref, dkv_ref, db_ref, dbkv_ref, dcw_ref, dsk_ref, carry_ref):
        n = pl.program_id(0)

        @pl.when(n == 0)
        def _():
            carry_ref[...] = jnp.zeros_like(carry_ref)
            db_ref[...] = jnp.zeros_like(db_ref)
            dbkv_ref[...] = jnp.zeros_like(dbkv_ref)
            dcw_ref[...] = jnp.zeros_like(dcw_ref)
            dsk_ref[...] = jnp.zeros_like(dsk_ref)

        @pl.when(n < nb)
        def _():
            dm = dm_ref[...].astype(F32)
            sa = jax.nn.sigmoid(z_ref[:, 4 * d:5 * d].astype(F32))
            att = attn_ref[...].astype(F32)
            dz_ref[:, 4 * d:5 * d] = (dm * att * sa * (1.0 - sa)).astype(BF16)
            db_ref[0:1, 4 * d:5 * d] += jnp.sum(dm * att * sa * (1.0 - sa), axis=0, keepdims=True)
            dattn = (dm * sa).astype(BF16)

            kv = jnp.concatenate([kvp_ref[...], z_ref[:, 6 * d:]], axis=0)
            k_eff, v_eff = _kv_variants(kv, kvw2 // 2)
            band, col = _attn_masks()
            valid = band & ((n > 0) | (col >= BLOCK))
            lane_lo = lax.broadcasted_iota(jnp.int32, (2 * BLOCK, LANES), 1) < HEAD_DIM
            sink_lane = lax.broadcasted_iota(jnp.int32, (1, LANES), 1)
            rowblk = lax.broadcasted_iota(jnp.int32, (4 * BLOCK, 1), 0) // BLOCK
            dk_acc = [jnp.zeros((2 * BLOCK, LANES), F32), jnp.zeros((2 * BLOCK, LANES), F32)]
            dv_acc = [jnp.zeros((2 * BLOCK, LANES), F32), jnp.zeros((2 * BLOCK, LANES), F32)]
            dsink = jnp.zeros((1, LANES), F32)
            for h in range(2):
                q4 = _stack_pairs(z_ref, h)
                do4 = jnp.concatenate([dattn[:, (4 * h + j) * LANES:(4 * h + j + 1) * LANES] for j in range(4)],
                                      axis=0)
                dq4 = jnp.zeros((4 * BLOCK, LANES), F32)
                for e in range(2):
                    s = lax.dot_general(q4, k_eff[h][e], (((1,), (1,)), ((), ())), preferred_element_type=F32)
                    p, psink = _softmax_sink(s * scale, valid, _sink_column(sinks_ref, h, e))
                    dp = lax.dot_general(do4, v_eff[h][e], (((1,), (1,)), ((), ())), preferred_element_type=F32)
                    delta = jnp.sum(p * dp, axis=-1, keepdims=True)
                    ds = (p * (dp - delta) * scale).astype(BF16)
                    dq4 = dq4 + jnp.dot(ds, k_eff[h][e], preferred_element_type=F32)
                    dk = lax.dot_general(ds, q4, (((0,), (0,)), ((), ())), preferred_element_type=F32)
                    dv = lax.dot_general(p.astype(BF16), do4, (((0,), (0,)), ((), ())), preferred_element_type=F32)
                    keep = lane_lo if e == 0 else jnp.logical_not(lane_lo)
                    slot = 0 if e == h else 1
                    dk_acc[slot] = dk_acc[slot] + jnp.where(keep, dk, 0.0)
                    dv_acc[slot] = dv_acc[slot] + jnp.where(keep, dv, 0.0)
                    dsk = -(psink * delta)
                    for j in range(4):
                        tot = jnp.sum(jnp.where(rowblk == j, dsk, 0.0), axis=0, keepdims=True)
                        dsink = dsink + jnp.where(sink_lane == GROUP * h + 2 * j + e, tot, 0.0)
                for j in range(4):
                    cols = slice((4 * h + j) * LANES, (4 * h + j + 1) * LANES)
                    dqj = dq4[j * BLOCK:(j + 1) * BLOCK]
                    dz_ref[:, cols] = dqj.astype(BF16)
                    db_ref[0:1, cols] += jnp.sum(dqj, axis=0, keepdims=True)
            dsk_ref[0:1, :] += dsink
            dkv_new = jnp.concatenate([dk_acc[0] + pltpu.roll(dk_acc[1], HEAD_DIM, 1),
                                       dv_acc[0] + pltpu.roll(dv_acc[1], HEAD_DIM, 1)], axis=1)
            done = carry_ref[...] + dkv_new[:BLOCK]
            dkv_ref[...] = done.astype(BF16)
            dbkv_ref[0:1, :] += jnp.sum(done, axis=0, keepdims=True)
            carry_ref[...] = dkv_new[BLOCK:]

            cb = z_ref[:, d:2 * d].astype(F32)
            cc = z_ref[:, 2 * d:3 * d].astype(F32)
            cx = z_ref[:, 3 * d:4 * d].astype(F32)
            sg = jax.nn.sigmoid(z_ref[:, 5 * d:6 * d].astype(F32))
            p_in = cc * cx
            prev = jnp.where(n > 0, ccp_ref[...].astype(F32) * cxp_ref[...].astype(F32), 0.0)
            p_m1 = _shift_down(p_in, 1, prev)
            p_m2 = _shift_down(p_in, 2, prev)
            w0, w1, w2 = cw_ref[0:1, :], cw_ref[1:2, :], cw_ref[2:3, :]
            cconv = w0 * p_m2 + w1 * p_m1 + w2 * p_in
            dconv = dm * sg
            dgc = dm * (cb * cconv) * sg * (1.0 - sg)
            dcb = dconv * cconv
            dcc_t = dconv * cb
            nxt = jnp.where(n < nb - 1, dmn_ref[...].astype(F32) * jax.nn.sigmoid(gcn_ref[...].astype(F32))
                            * cbn_ref[...].astype(F32), 0.0)
            dpin = w2 * dcc_t + w1 * _shift_up(dcc_t, 1, nxt) + w0 * _shift_up(dcc_t, 2, nxt)
            for seg, val in ((1, dcb), (2, dpin * cx), (3, dpin * cc), (5, dgc)):
                dz_ref[:, seg * d:(seg + 1) * d] = val.astype(BF16)
                db_ref[0:1, seg * d:(seg + 1) * d] += jnp.sum(val, axis=0, keepdims=True)
            dcw_ref[0:1, :] += jnp.sum(dcc_t * p_m2, axis=0, keepdims=True)
            dcw_ref[1:2, :] += jnp.sum(dcc_t * p_m1, axis=0, keepdims=True)
            dcw_ref[2:3, :] += jnp.sum(dcc_t * p_in, axis=0, keepdims=True)

        @pl.when(n == nb)
        def _():
            done = carry_ref[...]
            dkv_ref[...] = done.astype(BF16)
            dbkv_ref[0:1, :] += jnp.sum(done, axis=0, keepdims=True)

    def cur(n):
        return jnp.minimum(n, nb - 1)

    blk = pl.BlockSpec((BLOCK, d), lambda n: (cur(n), 0))
    return pl.pallas_call(
        body, name="mixer_bwd", grid=(nb + 1,),
        in_specs=[pl.BlockSpec((BLOCK, zw), lambda n: (cur(n), 0)),
                  pl.BlockSpec((BLOCK, kvw2), lambda n: (jnp.maximum(cur(n) - 1, 0), kv_col)),
                  pl.BlockSpec((SUBLANES_BF16, d), lambda n: (jnp.maximum(cur(n) * halo - 1, 0), 2)),
                  pl.BlockSpec((SUBLANES_BF16, d), lambda n: (jnp.maximum(cur(n) * halo - 1, 0), 3)),
                  pl.BlockSpec((SUBLANES_BF16, d), lambda n: (jnp.minimum((cur(n) + 1) * halo, last_halo), 1)),
                  pl.BlockSpec((SUBLANES_BF16, d), lambda n: (jnp.minimum((cur(n) + 1) * halo, last_halo), 5)),
                  blk,
                  pl.BlockSpec((SUBLANES_BF16, d), lambda n: (jnp.minimum((cur(n) + 1) * halo, last_halo), 0)),
                  blk, SMEM_SPEC, pl.BlockSpec((3, d), lambda n: (0, 0))],
        out_specs=[pl.BlockSpec((BLOCK, 6 * d), lambda n: (cur(n), 0)),
                   pl.BlockSpec((BLOCK, kvw2), lambda n: (jnp.maximum(n - 1, 0), 0)),
                   pl.BlockSpec((8, 6 * d), lambda n: (0, 0)), pl.BlockSpec((8, kvw2), lambda n: (0, 0)),
                   pl.BlockSpec((8, d), lambda n: (0, 0)), pl.BlockSpec((8, LANES), lambda n: (0, 0))],
        out_shape=[SDS((t, 6 * d), BF16), SDS((t, kvw2), BF16), SDS((8, 6 * d), F32), SDS((8, kvw2), F32),
                   SDS((8, d), F32), SDS((8, LANES), F32)],
        scratch_shapes=[pltpu.VMEM((BLOCK, kvw2), F32)],
        compiler_params=_params("arbitrary"))(z, z, z, z, z, z, dmerged, dmerged, attn, sinks, conv_w)


def _in_proj_bwd(dzm, dkv, w, x, dx1, g_mix, sc1, tm):
    t, d = x.shape
    kvw2 = dkv.shape[1]
    nk = dzm.shape[1] // d
    kv_col = nk * d // kvw2

    def body(a_ref, akv_ref, w_ref, wkv_ref, x_ref, dx1_ref, g_ref, sc_ref, gx_ref, st_ref, acc_ref):
        i, k = pl.program_id(0), pl.program_id(1)

        @pl.when((i == 0) & (k == 0))
        def _():
            st_ref[...] = jnp.zeros_like(st_ref)

        part = lax.dot_general(a_ref[...], w_ref[...], (((1,), (1,)), ((), ())), preferred_element_type=F32)

        @pl.when(k == 0)
        def _():
            acc_ref[...] = part + lax.dot_general(akv_ref[...], wkv_ref[...], (((1,), (1,)), ((), ())),
                                                  preferred_element_type=F32)

        @pl.when(k > 0)
        def _():
            acc_ref[...] += part

        @pl.when(k == nk - 1)
        def _():
            dh = acc_ref[...]
            xx = x_ref[...]
            r = _rms(xx)
            xn = xx * r
            g = g_ref[...]
            dn = dh * (1.0 + sc_ref[...])
            u = dn * g
            gx_ref[...] = dx1_ref[...] + r * (u - xn * jnp.mean(u * xn, axis=-1, keepdims=True))
            st_ref[0:1, :] += jnp.sum(dh, axis=0, keepdims=True)
            st_ref[1:2, :] += jnp.sum(dh * (xn * g), axis=0, keepdims=True)
            st_ref[2:3, :] += jnp.sum(dn * xn, axis=0, keepdims=True)

    row = pl.BlockSpec((tm, d), lambda i, k: (i, 0))
    vec = pl.BlockSpec((1, d), lambda i, k: (0, 0))
    return pl.pallas_call(
        body, name="in_proj_bwd", grid=(t // tm, nk),
        in_specs=[pl.BlockSpec((tm, d), lambda i, k: (i, k)), pl.BlockSpec((tm, kvw2), lambda i, k: (i, 0)),
                  pl.BlockSpec((d, d), lambda i, k: (0, k)), pl.BlockSpec((d, kvw2), lambda i, k: (0, kv_col)),
                  row, row, vec, vec],
        out_specs=[row, pl.BlockSpec((8, d), lambda i, k: (0, 0))],
        out_shape=[SDS((t, d), F32), SDS((8, d), F32)],
        scratch_shapes=[pltpu.VMEM((tm, d), F32)],
        compiler_params=_params("arbitrary", "arbitrary"))(dzm, dkv, w, w, x, dx1, g_mix, sc1)


def _permute_cols(a, d, kvw2):
    return jnp.concatenate([a[..., :d], a[..., d + kvw2:], a[..., d:d + kvw2]], axis=-1)


def _unpermute_cols(a, d, kvw2):
    n = a.shape[-1]
    return jnp.concatenate([a[..., :d], a[..., n - kvw2:], a[..., d:n - kvw2]], axis=-1)


def _to_lanes(v, rows=None):
    flat = v.reshape(-1)
    need = -(-flat.shape[0] // LANES)
    need = -(-need // 8) * 8 if rows is None else rows
    return jnp.pad(flat, (0, need * LANES - flat.shape[0])).reshape(need, LANES)


def kernel(x, c, w_ada, b_ada, g_mix, w_in, b_in, sinks, conv_w, w_out, g_ffn, w_ffn_in, w_ffn_out, g_final, loss_target, m_w_ada, m_b_ada, m_g_mix, m_w_in, m_b_in, m_sinks, m_conv_w, m_w_out, m_g_ffn, m_w_ffn_in, m_w_ffn_out, m_g_final, v_w_ada, v_b_ada, v_g_mix, v_w_in, v_b_in, v_sinks, v_conv_w, v_w_out, v_g_ffn, v_w_ffn_in, v_w_ffn_out, v_g_final):
    xs, tgt = x[0], loss_target[0]
    t, d = xs.shape
    zw = w_in.shape[2] * N_CHIP
    kvw2 = zw - 6 * d
    ff = w_ffn_out.shape[1] * N_CHIP
    n_mod = w_ada.shape[2] * N_CHIP // d
    mod_sh = w_ada.shape[2]
    cw_sh = conv_w.shape[2]
    assert d % (8 * LANES) == 0 and kvw2 == 2 * LANES and t % 512 == 0 and n_mod == 6
    xi, yi, ci = _mesh_pos()
    j_me = 2 * xi + yi
    b_me = 4 * xi + 2 * yi + ci
    pos = jnp.stack([ci, j_me]).astype(jnp.int32)
    tm = 512

    pack1 = jnp.concatenate([c.reshape(d // LANES, LANES), conv_w[0].reshape(-1, LANES)], axis=0)
    pack1 = jnp.pad(pack1, ((0, 16 - pack1.shape[0]), (0, 0)))
    g1 = _all_gather_small(pack1, "gather_c")
    c_all = g1[:, :d // LANES, :].reshape(N_DEV, d)
    cw_rows = 3 * cw_sh // LANES
    conv_w_full = jnp.concatenate(
        [g1[2 * j, d // LANES:d // LANES + cw_rows, :].reshape(3, cw_sh) for j in range(N_CHIP)], axis=1)
    b_ada_sh = lax.dynamic_slice(b_ada, (0, j_me * mod_sh), (1, mod_sh))
    mod_all = _all_gather_small(_ada_fwd(c_all, w_ada[0], b_ada_sh), "gather_mod")
    mod = jnp.concatenate([lax.dynamic_index_in_dim(mod_all[2 * j], b_me, 0, keepdims=True) for j in range(N_CHIP)],
                          axis=1)
    sh1, sc1, ga1, sh2, sc2, ga2 = [mod[:, k * d:(k + 1) * d] for k in range(6)]

    gathered = _gather_weights([_cast_into_block(pos, w_in[0], "cast_w_in"), _cast_into_block(pos, w_out[0], "cast_w_out"),
                                _cast_into_block(pos, w_ffn_in[0], "cast_w_ffn_in"),
                                _cast_into_block(pos, w_ffn_out[0], "cast_w_ffn_out")])
    w_in_p = _permute_cols(gathered[0].transpose(1, 0, 2).reshape(d, zw), d, kvw2)
    b_in_p = _permute_cols(b_in, d, kvw2)
    w_out_f = gathered[1].reshape(d, d)
    w_ffn_in_f = gathered[2].transpose(1, 0, 2).reshape(d, 2 * ff)
    w_ffn_out_f = gathered[3].reshape(ff, d)

    h1 = _prenorm(xs, g_mix, sc1, sh1, tm)
    z = _in_proj(h1, w_in_p, b_in_p, tm, zw // 5)
    attn, merged = _mixer_fwd(z, sinks, conv_w_full, d)
    y1, x1, h2 = _out_proj_fwd(merged, w_out_f, xs, ga1, g_ffn, sc2, sh2, tm)
    gu, act = _ffn_in_fwd(h2, w_ffn_in_f, ff, tm, ff // 2)
    dx2, dy2, st_loss = _ffn_out_loss(act, w_ffn_out_f, x1, tgt, ga2, g_final.reshape(1, d), tm)

    dgu = _ffn_out_bwd(dy2, w_ffn_out_f, gu, tm, ff // 2)
    tk = 512
    dw_ffn_out = _wgrad(
        act, dy2, pl.BlockSpec((tk, ff // 2), lambda m, k: (k, m)), pl.BlockSpec((tk, d), lambda m, k: (k, 0)),
        pl.BlockSpec((ff // 2, d), lambda m, k: (m, 0)), SDS((ff, d), F32), (2, t // tk), "wgrad_ffn_out")
    dx1, dy1, st_ffn = _ffn_in_bwd(dgu, w_ffn_in_f, x1, dx2, y1, g_ffn, sc2, ga1, tm, ff // 2)
    dw_ffn_in = _wgrad(
        h2, dgu, pl.BlockSpec((tk, d), lambda n, k: (k, 0)),
        pl.BlockSpec((None, tk, ff // 2), lambda n, k: (n // 2, k, n % 2)),
        pl.BlockSpec((None, d, ff // 2), lambda n, k: (n, 0, 0)), SDS((N_CHIP, d, ff // 2), F32),
        (N_CHIP, t // tk), "wgrad_ffn_in")
    dmerged = _out_proj_bwd(dy1, w_out_f, tm)
    dw_out = _wgrad(
        merged, dy1, pl.BlockSpec((tk, d), lambda m, k: (k, 0)), pl.BlockSpec((tk, d), lambda m, k: (k, 0)),
        pl.BlockSpec((d, d), lambda m, k: (0, 0)), SDS((d, d), F32), (1, t // tk), "wgrad_out")
    dzm, dkv, db_main, db_kv, dcw, dsk = _mixer_bwd(z, dmerged, attn, sinks, conv_w_full, d)
    grad_x, st_in = _in_proj_bwd(dzm, dkv, w_in_p, xs, dx1, g_mix, sc1, tm)
    dw_in_main = _wgrad(
        h1, dzm, pl.BlockSpec((tk, d), lambda n, k: (k, 0)), pl.BlockSpec((tk, d), lambda n, k: (k, n)),
        pl.BlockSpec((d, d), lambda n, k: (0, n)), SDS((d, 6 * d), F32), (6, t // tk), "wgrad_in_main")
    dw_in_kv = _wgrad(
        h1, dkv, pl.BlockSpec((tk, d), lambda n, k: (k, 0)), pl.BlockSpec((tk, kvw2), lambda n, k: (k, 0)),
        pl.BlockSpec((d, kvw2), lambda n, k: (0, 0)), SDS((d, kvw2), F32), (1, t // tk), "wgrad_in_kv")
    dw_in = _unpermute_cols(jnp.concatenate([dw_in_main, dw_in_kv], axis=1), d, kvw2)
    dw_in = dw_in.reshape(d, N_CHIP, zw // N_CHIP).transpose(1, 0, 2)

    full = [dw_in, dw_out.reshape(N_CHIP, d // N_CHIP, d), dw_ffn_in, dw_ffn_out.reshape(N_CHIP, ff // N_CHIP, d)]
    names = ["w_in", "w_out", "w_ffn_in", "w_ffn_out"]
    from_sibling = _pair_send(full)
    chip32, chip16 = zip(*[_pair_add(pos, g, s, "pair_add_" + nm) for g, s, nm in zip(full, from_sibling, names)])
    from_chips = _chip_send(list(chip16))
    halves = [_chip_add(pos, p, r, "chip_add_" + nm) for p, r, nm in zip(chip32, from_chips, names)]
    g_w_in, g_w_out, g_w_ffn_in, g_w_ffn_out = _pair_exchange(halves)

    dmod = jnp.concatenate([st_in[0:1], st_in[1:2], st_ffn[3:4], st_ffn[0:1], st_ffn[1:2], st_loss[0:1]], axis=1)
    db_in = _unpermute_cols(jnp.concatenate([db_main[0:1], db_kv[0:1]], axis=1), d, kvw2)
    seg = [dmod, st_in[2:3], db_in, dsk[0:1], dcw[0:3].reshape(1, 3 * d), st_ffn[2:3], st_loss[1:2],
           st_loss[2:3, :LANES]]
    sizes = [s.shape[1] for s in seg]
    pack2 = _to_lanes(jnp.concatenate(seg, axis=1))
    tot = _pack_sum(_all_gather_small(pack2, "gather_small_grads")).reshape(-1)
    offs = [sum(sizes[:k]) for k in range(len(sizes))]
    gb_ada, gg_mix, gb_in, gsinks, gcw, gg_ffn, gg_final, loss_v = [tot[o:o + s] for o, s in zip(offs, sizes)]
    loss = loss_v[0]
    gsinks = gsinks[:sinks.shape[1]]
    gcw_sh = lax.dynamic_slice(gcw.reshape(3, d), (0, j_me * cw_sh), (3, cw_sh))

    dmod_all = _all_gather_small(_to_lanes(dmod), "gather_dmod")[:, :n_mod * d // LANES, :].reshape(N_DEV, n_mod * d)
    g_w_ada = _ada_wgrad(c_all, lax.dynamic_slice(dmod_all, (0, j_me * mod_sh), (N_DEV, mod_sh)))

    out_g, out_d, out_m, out_v = {}, {}, {}, {}
    big = {"w_ada": (w_ada, g_w_ada, m_w_ada, v_w_ada), "w_in": (w_in, g_w_in, m_w_in, v_w_in),
           "w_out": (w_out, g_w_out, m_w_out, v_w_out), "w_ffn_in": (w_ffn_in, g_w_ffn_in, m_w_ffn_in, v_w_ffn_in),
           "w_ffn_out": (w_ffn_out, g_w_ffn_out, m_w_ffn_out, v_w_ffn_out)}
    for nm, (w, g, m, v) in big.items():
        dl, nm_, nv_ = _adamw(w[0], g, m[0], v[0], "adamw_" + nm)
        out_g[nm], out_d[nm], out_m[nm], out_v[nm] = g[None], dl[None], nm_[None], nv_[None]
    small = {"b_ada": (b_ada, gb_ada, m_b_ada, v_b_ada), "g_mix": (g_mix, gg_mix, m_g_mix, v_g_mix),
             "b_in": (b_in, gb_in, m_b_in, v_b_in), "sinks": (sinks, gsinks, m_sinks, v_sinks),
             "conv_w": (conv_w, gcw_sh, m_conv_w, v_conv_w), "g_ffn": (g_ffn, gg_ffn, m_g_ffn, v_g_ffn),
             "g_final": (g_final, gg_final, m_g_final, v_g_final)}
    s_sizes = [w.size for w, _, _, _ in small.values()]
    s_rows = -(-sum(s_sizes) // LANES // 8) * 8

    def s_pack(k):
        return _to_lanes(jnp.concatenate([tup[k].reshape(-1) for tup in small.values()]), s_rows)

    s_out = _adamw(s_pack(0), s_pack(1), s_pack(2), s_pack(3), "adamw_small")
    s_off = 0
    for (nm, (w, g, _, _)), sz in zip(small.items(), s_sizes):
        out_g[nm] = g.reshape(w.shape)
        out_d[nm], out_m[nm], out_v[nm] = [o.reshape(-1)[s_off:s_off + sz].reshape(w.shape) for o in s_out]
        s_off += sz

    order = ["w_ada", "b_ada", "g_mix", "w_in", "b_in", "sinks", "conv_w", "w_out", "g_ffn", "w_ffn_in", "w_ffn_out",
             "g_final"]
    return (loss, grad_x[None], *[out_g[k] for k in order], *[out_d[k] for k in order],
            *[out_m[k] for k in order], *[out_v[k] for k in order])
```

```python
import functools

import jax
import jax.numpy as jnp
from jax import lax
from jax.experimental import pallas as pl
from jax.experimental.pallas import tpu as pltpu

F32 = jnp.float32
BF16 = jnp.bfloat16
EPS = 1e-6
HEAD_DIM = 64
GROUP = 8
BLOCK = 128
LANES = 128
SUBLANES_BF16 = 16
N_DEV = 8
N_CHIP = 4
VMEM_LIMIT = 56 * 1024 * 1024
MESH = pl.DeviceIdType.MESH

ADAM_LR = 0.001
ADAM_B1 = 0.9
ADAM_B2 = 0.999
ADAM_EPS = 1e-08
ADAM_WD = 0.01
ADAM_STEP = 10

SDS = jax.ShapeDtypeStruct
ANY = pl.BlockSpec(memory_space=pl.ANY)
VMEM_SPEC = pl.BlockSpec(memory_space=pltpu.VMEM)
SMEM_SPEC = pl.BlockSpec(memory_space=pltpu.SMEM)


def _params(*sem):
    return pltpu.CompilerParams(dimension_semantics=sem, vmem_limit_bytes=VMEM_LIMIT)


def _mesh_pos():
    return lax.axis_index("x"), lax.axis_index("y"), lax.axis_index("c")


def _row_tile(rows, cols, itemsize=4, budget=1 << 20, mult=8):
    best = None
    for t in range(mult, rows + 1, mult):
        if rows % t == 0 and t * cols * itemsize <= budget:
            best = t
    if best is None:
        best = rows
    return best


def _all_gather_small(v, name):
    rows, cols = v.shape

    def body(v_ref, out_ref, send_sems, recv_sems, local_sem):
        x, y, c = _mesh_pos()
        me = 4 * x + 2 * y + c
        mine = pltpu.make_async_copy(v_ref, out_ref.at[me], local_sem)
        mine.start()
        peers = []
        for k in range(1, N_DEV):
            px = 1 - x if k & 4 else x
            py = 1 - y if k & 2 else y
            pc = 1 - c if k & 1 else c
            peers.append((px, py, pc))

        def copy(k, block):
            return pltpu.make_async_remote_copy(
                src_ref=v_ref, dst_ref=out_ref.at[block], send_sem=send_sems.at[k], recv_sem=recv_sems.at[k],
                device_id=peers[k], device_id_type=MESH)

        sends = [copy(k, me) for k in range(N_DEV - 1)]
        for cp in sends:
            cp.start()
        for k, (px, py, pc) in enumerate(peers):
            copy(k, 4 * px + 2 * py + pc).wait_recv()
        for cp in sends:
            cp.wait_send()
        mine.wait()

    return pl.pallas_call(
        body, name=name,
        out_shape=SDS((N_DEV, rows, cols), v.dtype),
        in_specs=[VMEM_SPEC], out_specs=VMEM_SPEC,
        scratch_shapes=[pltpu.SemaphoreType.DMA((N_DEV - 1,)), pltpu.SemaphoreType.DMA((N_DEV - 1,)),
                        pltpu.SemaphoreType.DMA],
    )(v)


def _other_chips(x, y):
    return [(1 - x, y), (x, 1 - y), (1 - x, 1 - y)]


def _gather_weights(bufs):
    n_w = len(bufs)

    def body(*refs):
        outs = refs[n_w:2 * n_w]
        send_sems, recv_sems, fsend_sems, frecv_sems = refs[2 * n_w:]
        x, y, c = _mesh_pos()
        j_me = 2 * x + y
        chips = _other_chips(x, y)
        sibling = (x, y, 1 - c)

        def half_rows(w, which):
            half = outs[w].shape[1] // 2
            return pl.ds(pl.multiple_of(which * half, SUBLANES_BF16), half)

        def copy(w, p, block, rows, over_ici):
            sems = (send_sems, recv_sems) if over_ici else (fsend_sems, frecv_sems)
            return pltpu.make_async_remote_copy(
                src_ref=outs[w].at[block, rows], dst_ref=outs[w].at[block, rows],
                send_sem=sems[0].at[w * 3 + p], recv_sem=sems[1].at[w * 3 + p],
                device_id=(*chips[p], c) if over_ici else sibling, device_id_type=MESH)

        def block_of(p):
            return 2 * chips[p][0] + chips[p][1]

        sends = [copy(w, p, j_me, half_rows(w, c), True) for w in range(n_w) for p in range(3)]
        for cp in sends:
            cp.start()
        forwards = []
        for w in range(n_w):
            for p in range(3):
                copy(w, p, block_of(p), half_rows(w, c), True).wait_recv()
                fw = copy(w, p, block_of(p), half_rows(w, c), False)
                fw.start()
                forwards.append(fw)
        for w in range(n_w):
            for p in range(3):
                copy(w, p, block_of(p), half_rows(w, 1 - c), False).wait_recv()
        for cp in sends + forwards:
            cp.wait_send()

    return pl.pallas_call(
        body, name="gather_weights",
        out_shape=[SDS(b.shape, b.dtype) for b in bufs],
        in_specs=[ANY] * n_w, out_specs=[ANY] * n_w,
        input_output_aliases={w: w for w in range(n_w)},
        scratch_shapes=[pltpu.SemaphoreType.DMA((3 * n_w,)), pltpu.SemaphoreType.DMA((3 * n_w,)),
                        pltpu.SemaphoreType.DMA((3 * n_w,)), pltpu.SemaphoreType.DMA((3 * n_w,))],
    )(*bufs)


class _Exchange:
    def __init__(self, operands, out_shape, in_place, n_sems, copies):
        self.operands, self.out_shape, self.in_place, self.n_sems, self.copies = (
            list(operands), list(out_shape), in_place, n_sems, copies)

    def sems(self):
        return [pltpu.SemaphoreType.DMA((self.n_sems,)), pltpu.SemaphoreType.DMA((self.n_sems,))]


def _x_gather_ici(bufs):
    def copies(ins, outs, send_sems, recv_sems):
        x, y, c = _mesh_pos()
        chips = _other_chips(x, y)
        out = []
        for w in range(len(outs)):
            half = outs[w].shape[1] // 2
            rows = pl.ds(pl.multiple_of(c * half, SUBLANES_BF16), half)
            for p in range(3):
                out.append(pltpu.make_async_remote_copy(
                    src_ref=outs[w].at[2 * x + y, rows], dst_ref=outs[w].at[2 * x + y, rows],
                    send_sem=send_sems.at[w * 3 + p], recv_sem=recv_sems.at[w * 3 + p],
                    device_id=(*chips[p], c), device_id_type=MESH))
        return out

    return _Exchange(bufs, [SDS(b.shape, b.dtype) for b in bufs], True, 3 * len(bufs), copies)


def _x_gather_d2d(bufs):
    def copies(ins, outs, send_sems, recv_sems):
        x, y, c = _mesh_pos()
        chips = _other_chips(x, y)
        out = []
        for w in range(len(outs)):
            half = outs[w].shape[1] // 2
            rows = pl.ds(pl.multiple_of(c * half, SUBLANES_BF16), half)
            for p in range(3):
                block = 2 * chips[p][0] + chips[p][1]
                out.append(pltpu.make_async_remote_copy(
                    src_ref=outs[w].at[block, rows], dst_ref=outs[w].at[block, rows],
                    send_sem=send_sems.at[w * 3 + p], recv_sem=recv_sems.at[w * 3 + p],
                    device_id=(x, y, 1 - c), device_id_type=MESH))
        return out

    return _Exchange(bufs, [SDS(b.shape, b.dtype) for b in bufs], True, 3 * len(bufs), copies)


def _x_pair_send(grads):
    def copies(ins, outs, send_sems, recv_sems):
        x, y, c = _mesh_pos()
        out = []
        for w in range(len(ins)):
            half = ins[w].shape[1] // 2
            rows = pl.ds(pl.multiple_of((1 - c) * half, 8), half)
            out.append(pltpu.make_async_remote_copy(
                src_ref=ins[w].at[:, rows, :], dst_ref=outs[w], send_sem=send_sems.at[w], recv_sem=recv_sems.at[w],
                device_id=(x, y, 1 - c), device_id_type=MESH))
        return out

    return _Exchange(grads, [SDS((N_CHIP, g.shape[1] // 2, g.shape[2]), g.dtype) for g in grads], False,
                     len(grads), copies)


def _x_chip_send(partials):
    def copies(ins, outs, send_sems, recv_sems):
        x, y, c = _mesh_pos()
        chips = _other_chips(x, y)
        out = []
        for w in range(len(ins)):
            for p in range(3):
                out.append(pltpu.make_async_remote_copy(
                    src_ref=ins[w].at[2 * chips[p][0] + chips[p][1]], dst_ref=outs[w].at[p],
                    send_sem=send_sems.at[w * 3 + p], recv_sem=recv_sems.at[w * 3 + p],
                    device_id=(*chips[p], c), device_id_type=MESH))
        return out

    return _Exchange(partials, [SDS((3,) + p.shape[1:], p.dtype) for p in partials], False, 3 * len(partials),
                     copies)


def _x_pair_exchange(fulls):
    def copies(ins, outs, send_sems, recv_sems):
        x, y, c = _mesh_pos()
        out = []
        for w in range(len(outs)):
            half = outs[w].shape[0] // 2
            rows = pl.ds(pl.multiple_of(c * half, 8), half)
            out.append(pltpu.make_async_remote_copy(
                src_ref=outs[w].at[rows], dst_ref=outs[w].at[rows], send_sem=send_sems.at[w],
                recv_sem=recv_sems.at[w], device_id=(x, y, 1 - c), device_id_type=MESH))
        return out

    return _Exchange(fulls, [SDS(f.shape, f.dtype) for f in fulls], True, len(fulls), copies)


def _pallas(body, *, name, grid, in_specs, out_specs, out_shape, args, scratch=(), sem=None, ride=None):
    single = not isinstance(out_specs, (list, tuple))
    out_specs_l = [out_specs] if single else list(out_specs)
    out_shape_l = [out_shape] if single else list(out_shape)
    n_in, n_out, n_scr = len(in_specs), len(out_specs_l), len(scratch)
    if ride is None:
        res = pl.pallas_call(body, name=name, grid=grid, in_specs=list(in_specs), out_specs=out_specs,
                             out_shape=out_shape, scratch_shapes=list(scratch), compiler_params=_params(*sem))(*args)
        return res, None
    n_x, n_xo = len(ride.operands), len(ride.out_shape)

    def full_body(*refs):
        ins, x_ins = refs[:n_in], refs[n_in:n_in + n_x]
        outs = refs[n_in + n_x:n_in + n_x + n_out]
        x_outs = refs[n_in + n_x + n_out:n_in + n_x + n_out + n_xo]
        rest = refs[n_in + n_x + n_out + n_xo:]
        scr, (send_sems, recv_sems) = rest[:n_scr], rest[n_scr:]
        first = functools.reduce(jnp.logical_and, [pl.program_id(a) == 0 for a in range(len(grid))])
        last = functools.reduce(jnp.logical_and, [pl.program_id(a) == grid[a] - 1 for a in range(len(grid))])

        @pl.when(first)
        def _():
            for cp in ride.copies(x_ins, x_outs, send_sems, recv_sems):
                cp.start()

        body(*ins, *outs, *scr)

        @pl.when(last)
        def _():
            for cp in ride.copies(x_ins, x_outs, send_sems, recv_sems):
                cp.wait()

    res = pl.pallas_call(
        full_body, name=name, grid=grid, in_specs=list(in_specs) + [ANY] * n_x,
        out_specs=out_specs_l + [ANY] * n_xo, out_shape=out_shape_l + ride.out_shape,
        input_output_aliases={n_in + k: n_out + k for k in range(n_x)} if ride.in_place else {},
        scratch_shapes=list(scratch) + ride.sems(),
        compiler_params=_params(*(["arbitrary"] * len(grid))))(*args, *ride.operands)
    own = res[0] if single else list(res[:n_out])
    return own, list(res[n_out:])


def _exchange(ride, name):
    n_x, n_xo = len(ride.operands), len(ride.out_shape)

    def body(*refs):
        x_ins, x_outs = refs[:n_x], refs[n_x:n_x + n_xo]
        send_sems, recv_sems = refs[n_x + n_xo:]
        copies = ride.copies(x_ins, x_outs, send_sems, recv_sems)
        for cp in copies:
            cp.start()
        for cp in copies:
            cp.wait()

    return pl.pallas_call(
        body, name=name, in_specs=[ANY] * n_x, out_specs=[ANY] * n_xo, out_shape=ride.out_shape,
        input_output_aliases={k: k for k in range(n_x)} if ride.in_place else {},
        scratch_shapes=ride.sems())(*ride.operands)


def _cast_into_block(pos, w, name):
    rows, cols = w.shape
    tr = _row_tile(rows, cols, mult=SUBLANES_BF16)

    def body(pos_ref, w_ref, o_ref):
        del pos_ref
        o_ref[...] = w_ref[...].astype(BF16)

    return pl.pallas_call(
        body, name=name,
        grid_spec=pltpu.PrefetchScalarGridSpec(
            num_scalar_prefetch=1, grid=(rows // tr,),
            in_specs=[pl.BlockSpec((tr, cols), lambda i, pos_ref: (i, 0))],
            out_specs=pl.BlockSpec((None, tr, cols), lambda i, pos_ref: (pos_ref[1], i, 0))),
        out_shape=SDS((N_CHIP, rows, cols), BF16), compiler_params=_params("parallel"))(pos, w)


def _pair_add(pos, grad, from_sibling, name):
    _, rows, cols = grad.shape
    half = rows // 2
    tr = _row_tile(half, cols, mult=SUBLANES_BF16)
    nblk = half // tr

    def body(pos_ref, g_ref, s_ref, o32_ref, o16_ref):
        del pos_ref
        s = g_ref[...] + s_ref[...]
        o32_ref[...] = s
        o16_ref[...] = s.astype(BF16)

    spec = pl.BlockSpec((None, tr, cols), lambda j, i, pos_ref: (j, i, 0))
    return pl.pallas_call(
        body, name=name,
        grid_spec=pltpu.PrefetchScalarGridSpec(
            num_scalar_prefetch=1, grid=(N_CHIP, nblk),
            in_specs=[pl.BlockSpec((None, tr, cols), lambda j, i, pos_ref: (j, pos_ref[0] * nblk + i, 0)), spec],
            out_specs=[spec, spec]),
        out_shape=[SDS((N_CHIP, half, cols), F32), SDS((N_CHIP, half, cols), BF16)],
        compiler_params=_params("parallel", "parallel"),
    )(pos, grad, from_sibling)


def _chip_add(pos, partial32, from_chips, name):
    _, half, cols = partial32.shape
    tr = _row_tile(half, cols, mult=SUBLANES_BF16)

    def body(pos_ref, p_ref, r_ref, o_ref):
        del pos_ref
        acc = p_ref[...]
        for p in range(3):
            acc = acc + r_ref[p].astype(F32)
        o_ref[...] = acc

    return pl.pallas_call(
        body, name=name,
        grid_spec=pltpu.PrefetchScalarGridSpec(
            num_scalar_prefetch=1, grid=(half // tr,),
            in_specs=[pl.BlockSpec((None, tr, cols), lambda i, pos_ref: (pos_ref[1], i, 0)),
                      pl.BlockSpec((3, tr, cols), lambda i, pos_ref: (0, i, 0))],
            out_specs=pl.BlockSpec((tr, cols), lambda i, pos_ref: (pos_ref[0] * (half // tr) + i, 0))),
        out_shape=SDS((2 * half, cols), F32),
        compiler_params=_params("parallel"),
    )(pos, partial32, from_chips)


def _adamw(w, g, m, v, name):
    rows, cols = w.shape
    tr = _row_tile(rows, cols, budget=1 << 19)

    def body(w_ref, g_ref, m_ref, v_ref, d_ref, nm_ref, nv_ref):
        gg = g_ref[...]
        nm = ADAM_B1 * m_ref[...] + (1.0 - ADAM_B1) * gg
        nv = ADAM_B2 * v_ref[...] + (1.0 - ADAM_B2) * (gg * gg)
        m_hat = nm / (1.0 - ADAM_B1 ** ADAM_STEP)
        v_hat = nv / (1.0 - ADAM_B2 ** ADAM_STEP)
        d_ref[...] = -ADAM_LR * (m_hat / (jnp.sqrt(v_hat) + ADAM_EPS) + ADAM_WD * w_ref[...])
        nm_ref[...] = nm
        nv_ref[...] = nv

    spec = pl.BlockSpec((tr, cols), lambda i: (i, 0))
    return pl.pallas_call(body, name=name, grid=(rows // tr,), in_specs=[spec] * 4, out_specs=[spec] * 3,
                          out_shape=[SDS((rows, cols), F32)] * 3, compiler_params=_params("parallel"))(w, g, m, v)


def _pack_sum(gathered):
    _, rows, cols = gathered.shape

    def body(g_ref, o_ref):
        acc = g_ref[0]
        for d in range(1, N_DEV):
            acc = acc + g_ref[d]
        o_ref[...] = acc

    return pl.pallas_call(body, name="pack_sum", in_specs=[VMEM_SPEC], out_specs=VMEM_SPEC,
                          out_shape=SDS((rows, cols), F32))(gathered)


def _ada_fwd(c_all, w_sh, b_sh):
    d, n = w_sh.shape
    tn = 512

    def body(c_ref, w_ref, b_ref, o_ref):
        cc = c_ref[...]
        s = (cc * jax.nn.sigmoid(cc)).astype(BF16)
        o_ref[...] = jnp.dot(s, w_ref[...].astype(BF16), preferred_element_type=F32) + b_ref[...]

    return pl.pallas_call(
        body, name="ada_fwd", grid=(n // tn,),
        in_specs=[pl.BlockSpec((N_DEV, d), lambda j: (0, 0)), pl.BlockSpec((d, tn), lambda j: (0, j)),
                  pl.BlockSpec((1, tn), lambda j: (0, j))],
        out_specs=pl.BlockSpec((N_DEV, tn), lambda j: (0, j)),
        out_shape=SDS((N_DEV, n), F32), compiler_params=_params("parallel"))(c_all, w_sh, b_sh)


def _ada_wgrad(c_all, dmod_sh):
    d = c_all.shape[1]
    n = dmod_sh.shape[1]
    tn = 512

    def body(c_ref, g_ref, o_ref):
        cc = c_ref[...]
        s = cc * jax.nn.sigmoid(cc)
        o_ref[...] = lax.dot_general(s, g_ref[...], (((0,), (0,)), ((), ())), preferred_element_type=F32,
                                     precision=lax.Precision.HIGHEST)

    return pl.pallas_call(
        body, name="ada_wgrad", grid=(n // tn,),
        in_specs=[pl.BlockSpec((N_DEV, d), lambda j: (0, 0)), pl.BlockSpec((N_DEV, tn), lambda j: (0, j))],
        out_specs=pl.BlockSpec((d, tn), lambda j: (0, j)),
        out_shape=SDS((d, n), F32), compiler_params=_params("parallel"))(c_all, dmod_sh)


def _rms(xf):
    return lax.rsqrt(jnp.mean(xf * xf, axis=-1, keepdims=True) + EPS)


def _prenorm(x, g, sc, sh, tm):
    t, d = x.shape

    def body(x_ref, g_ref, sc_ref, sh_ref, h_ref):
        xf = x_ref[...]
        h_ref[...] = ((xf * _rms(xf) * g_ref[...]) * (1.0 + sc_ref[...]) + sh_ref[...]).astype(BF16)

    row = pl.BlockSpec((tm, d), lambda i: (i, 0))
    vec = pl.BlockSpec((1, d), lambda i: (0, 0))
    return pl.pallas_call(body, name="prenorm", grid=(t // tm,), in_specs=[row, vec, vec, vec], out_specs=row,
                          out_shape=SDS((t, d), BF16), compiler_params=_params("parallel"))(x, g, sc, sh)


def _in_proj(h, w, b, tm, tn, ride=None):
    t, d = h.shape
    n = w.shape[1]

    def body(h_ref, w_ref, b_ref, z_ref):
        z_ref[...] = (jnp.dot(h_ref[...], w_ref[...], preferred_element_type=F32) + b_ref[...]).astype(BF16)

    return _pallas(
        body, name="in_proj", grid=(n // tn, t // tm),
        in_specs=[pl.BlockSpec((tm, d), lambda j, i: (i, 0)), pl.BlockSpec((d, tn), lambda j, i: (0, j)),
                  pl.BlockSpec((1, tn), lambda j, i: (0, j))],
        out_specs=pl.BlockSpec((tm, tn), lambda j, i: (i, j)),
        out_shape=SDS((t, n), BF16), args=(h, w, b), sem=("parallel", "parallel"), ride=ride)


def _attn_masks():
    rows = 4 * BLOCK
    r = lax.broadcasted_iota(jnp.int32, (rows, 2 * BLOCK), 0) & (BLOCK - 1)
    col = lax.broadcasted_iota(jnp.int32, (rows, 2 * BLOCK), 1)
    return (col > r) & (col <= r + BLOCK), col


def _kv_variants(kv, n_kv_w):
    assert n_kv_w == LANES
    kb, vb = kv[:, :LANES], kv[:, LANES:]
    kr, vr = pltpu.roll(kb, HEAD_DIM, 1), pltpu.roll(vb, HEAD_DIM, 1)
    lane = lax.broadcasted_iota(jnp.int32, kb.shape, 1)
    lo = lane < HEAD_DIM
    zero = jnp.zeros_like(kb)
    k_eff = [[None, None], [None, None]]
    v_eff = [[None, None], [None, None]]
    for h in range(2):
        for e in range(2):
            ksrc, vsrc = (kb, vb) if e == h else (kr, vr)
            keep = lo if e == 0 else jnp.logical_not(lo)
            k_eff[h][e] = jnp.where(keep, ksrc, zero)
            v_eff[h][e] = jnp.where(keep, vsrc, zero)
    return k_eff, v_eff


def _sink_column(sinks_ref, h, e):
    rowblk = lax.broadcasted_iota(jnp.int32, (4 * BLOCK, 1), 0) // BLOCK
    col = jnp.zeros((4 * BLOCK, 1), F32)
    for j in range(4):
        col = jnp.where(rowblk == j, sinks_ref[0, GROUP * h + 2 * j + e], col)
    return col


def _softmax_sink(s, valid, sink):
    s = jnp.where(valid, s, -jnp.inf)
    m = jnp.maximum(jnp.max(s, axis=-1, keepdims=True), sink)
    p = jnp.exp(s - m)
    psink = jnp.exp(sink - m)
    den = jnp.sum(p, axis=-1, keepdims=True) + psink
    inv = 1.0 / den
    return p * inv, psink * inv


def _shift_down(a, s, prev):
    rows = a.shape[0]
    out = pltpu.roll(a, s, 0)
    row = lax.broadcasted_iota(jnp.int32, a.shape, 0)
    for t in range(s):
        out = jnp.where(row == t, prev[SUBLANES_BF16 - s + t:SUBLANES_BF16 - s + t + 1, :], out)
    del rows
    return out


def _shift_up(a, s, nxt):
    rows = a.shape[0]
    out = pltpu.roll(a, rows - s, 0)
    row = lax.broadcasted_iota(jnp.int32, a.shape, 0)
    for t in range(s):
        out = jnp.where(row == rows - s + t, nxt[t:t + 1, :], out)
    return out


def _stack_pairs(ref, h):
    return jnp.concatenate([ref[:, (4 * h + j) * LANES:(4 * h + j + 1) * LANES] for j in range(4)], axis=0)


def _mixer_fwd(z, sinks, conv_w, d):
    t, zw = z.shape
    kvw2 = zw - 6 * d
    nb = t // BLOCK
    kv_col = 6 * d // kvw2
    halo = BLOCK // SUBLANES_BF16

    def body(z_ref, kvp_ref, ccp_ref, cxp_ref, sinks_ref, cw_ref, attn_ref, merged_ref):
        n = pl.program_id(0)
        kv = jnp.concatenate([kvp_ref[...], z_ref[:, 6 * d:]], axis=0)
        k_eff, v_eff = _kv_variants(kv, kvw2 // 2)
        band, col = _attn_masks()
        valid = band & ((n > 0) | (col >= BLOCK))
        for h in range(2):
            q4 = _stack_pairs(z_ref, h)
            o4 = jnp.zeros((4 * BLOCK, LANES), F32)
            for e in range(2):
                s = lax.dot_general(q4, k_eff[h][e], (((1,), (1,)), ((), ())), preferred_element_type=F32)
                p, _ = _softmax_sink(s * (HEAD_DIM ** -0.5), valid, _sink_column(sinks_ref, h, e))
                o4 = o4 + jnp.dot(p.astype(BF16), v_eff[h][e], preferred_element_type=F32)
            for j in range(4):
                attn_ref[:, (4 * h + j) * LANES:(4 * h + j + 1) * LANES] = o4[j * BLOCK:(j + 1) * BLOCK].astype(BF16)
        cb = z_ref[:, d:2 * d].astype(F32)
        p_in = z_ref[:, 2 * d:3 * d].astype(F32) * z_ref[:, 3 * d:4 * d].astype(F32)
        prev = jnp.where(n > 0, ccp_ref[...].astype(F32) * cxp_ref[...].astype(F32), 0.0)
        cconv = (cw_ref[0:1, :] * _shift_down(p_in, 2, prev) + cw_ref[1:2, :] * _shift_down(p_in, 1, prev)
                 + cw_ref[2:3, :] * p_in)
        sa = jax.nn.sigmoid(z_ref[:, 4 * d:5 * d].astype(F32))
        sg = jax.nn.sigmoid(z_ref[:, 5 * d:6 * d].astype(F32))
        merged_ref[...] = (sa * attn_ref[...].astype(F32) + sg * (cb * cconv)).astype(BF16)

    blk = pl.BlockSpec((BLOCK, d), lambda n: (n, 0))
    return pl.pallas_call(
        body, name="mixer_fwd", grid=(nb,),
        in_specs=[pl.BlockSpec((BLOCK, zw), lambda n: (n, 0)),
                  pl.BlockSpec((BLOCK, kvw2), lambda n: (jnp.maximum(n - 1, 0), kv_col)),
                  pl.BlockSpec((SUBLANES_BF16, d), lambda n: (jnp.maximum(n * halo - 1, 0), 2)),
                  pl.BlockSpec((SUBLANES_BF16, d), lambda n: (jnp.maximum(n * halo - 1, 0), 3)),
                  SMEM_SPEC, pl.BlockSpec((3, d), lambda n: (0, 0))],
        out_specs=[blk, blk],
        out_shape=[SDS((t, d), BF16), SDS((t, d), BF16)],
        compiler_params=_params("parallel"))(z, z, z, z, sinks, conv_w)


def _out_proj_fwd(merged, w_out, x, ga1, g_ffn, sc2, sh2, tm):
    t, d = x.shape

    def body(m_ref, w_ref, x_ref, ga_ref, g_ref, sc_ref, sh_ref, y_ref, x1_ref, h_ref):
        y = jnp.dot(m_ref[...], w_ref[...], preferred_element_type=F32)
        x1 = x_ref[...] + ga_ref[...] * y
        y_ref[...] = y.astype(BF16)
        x1_ref[...] = x1
        h_ref[...] = ((x1 * _rms(x1) * g_ref[...]) * (1.0 + sc_ref[...]) + sh_ref[...]).astype(BF16)

    row = pl.BlockSpec((tm, d), lambda i: (i, 0))
    vec = pl.BlockSpec((1, d), lambda i: (0, 0))
    return pl.pallas_call(
        body, name="out_proj_fwd", grid=(t // tm,),
        in_specs=[row, pl.BlockSpec((d, d), lambda i: (0, 0)), row, vec, vec, vec, vec],
        out_specs=[row, row, row],
        out_shape=[SDS((t, d), BF16), SDS((t, d), F32), SDS((t, d), BF16)],
        compiler_params=_params("parallel"))(merged, w_out, x, ga1, g_ffn, sc2, sh2)


def _ffn_in_fwd(h2, w, ff, tm, tn):
    t, d = h2.shape
    nj = ff // tn

    def body(h_ref, wg_ref, wu_ref, gu_ref, act_ref):
        hh = h_ref[...]
        g = jnp.dot(hh, wg_ref[...], preferred_element_type=F32)
        u = jnp.dot(hh, wu_ref[...], preferred_element_type=F32)
        gu_ref[0] = g.astype(BF16)
        gu_ref[1] = u.astype(BF16)
        act_ref[...] = ((g * jax.nn.sigmoid(g)) * u).astype(BF16)

    return pl.pallas_call(
        body, name="ffn_in_fwd", grid=(nj, t // tm),
        in_specs=[pl.BlockSpec((tm, d), lambda j, i: (i, 0)), pl.BlockSpec((d, tn), lambda j, i: (0, j)),
                  pl.BlockSpec((d, tn), lambda j, i: (0, j + nj))],
        out_specs=[pl.BlockSpec((2, tm, tn), lambda j, i: (0, i, j)), pl.BlockSpec((tm, tn), lambda j, i: (i, j))],
        out_shape=[SDS((2, t, ff), BF16), SDS((t, ff), BF16)],
        compiler_params=_params("parallel", "parallel"))(h2, w, w)


def _ffn_out_loss(act, w, x1, target, ga2, g_final, tm):
    t, d = x1.shape
    ff = act.shape[1]

    def body(a_ref, w_ref, x1_ref, tg_ref, ga_ref, gf_ref, dx2_ref, dy2_ref, st_ref):
        @pl.when(pl.program_id(0) == 0)
        def _():
            st_ref[...] = jnp.zeros_like(st_ref)

        y2 = jnp.dot(a_ref[...], w_ref[...], preferred_element_type=F32)
        x2 = x1_ref[...] + ga_ref[...] * y2
        r = _rms(x2)
        yn = x2 * r
        err = yn * gf_ref[...] - tg_ref[...]
        loss = 0.5 * jnp.sum(jnp.mean(err * err, axis=-1, keepdims=True), axis=0, keepdims=True)
        dy = err * (1.0 / d)
        u = dy * gf_ref[...]
        dx2 = r * (u - yn * jnp.mean(u * yn, axis=-1, keepdims=True))
        dx2_ref[...] = dx2
        dy2_ref[...] = (ga_ref[...] * dx2).astype(BF16)
        st_ref[0:1, :] += jnp.sum(dx2 * y2, axis=0, keepdims=True)
        st_ref[1:2, :] += jnp.sum(dy * yn, axis=0, keepdims=True)
        st_ref[2:3, :] += jnp.broadcast_to(loss, (1, d))

    row = pl.BlockSpec((tm, d), lambda i: (i, 0))
    vec = pl.BlockSpec((1, d), lambda i: (0, 0))
    return pl.pallas_call(
        body, name="ffn_out_loss", grid=(t // tm,),
        in_specs=[pl.BlockSpec((tm, ff), lambda i: (i, 0)), pl.BlockSpec((ff, d), lambda i: (0, 0)), row, row,
                  vec, vec],
        out_specs=[row, row, pl.BlockSpec((8, d), lambda i: (0, 0))],
        out_shape=[SDS((t, d), F32), SDS((t, d), BF16), SDS((8, d), F32)],
        compiler_params=_params("arbitrary"))(act, w, x1, target, ga2, g_final)


def _ffn_out_bwd(dy2, w, gu, tm, tn):
    t, d = dy2.shape
    ff = w.shape[0]

    def body(dy_ref, w_ref, gu_ref, o_ref):
        dact = lax.dot_general(dy_ref[...], w_ref[...], (((1,), (1,)), ((), ())), preferred_element_type=F32)
        g = gu_ref[0].astype(F32)
        u = gu_ref[1].astype(F32)
        sg = jax.nn.sigmoid(g)
        o_ref[0] = (dact * u * (sg * (1.0 + g * (1.0 - sg)))).astype(BF16)
        o_ref[1] = (dact * (g * sg)).astype(BF16)

    gu_spec = pl.BlockSpec((2, tm, tn), lambda j, i: (0, i, j))
    return pl.pallas_call(
        body, name="ffn_out_bwd", grid=(ff // tn, t // tm),
        in_specs=[pl.BlockSpec((tm, d), lambda j, i: (i, 0)), pl.BlockSpec((tn, d), lambda j, i: (j, 0)), gu_spec],
        out_specs=gu_spec, out_shape=SDS((2, t, ff), BF16),
        compiler_params=_params("parallel", "parallel"))(dy2, w, gu)


def _wgrad(a, b, a_spec, b_spec, out_spec, out_shape, grid, name, ride=None):
    def body(a_ref, b_ref, o_ref):
        @pl.when(pl.program_id(len(grid) - 1) == 0)
        def _():
            o_ref[...] = jnp.zeros_like(o_ref)

        o_ref[...] += lax.dot_general(a_ref[...], b_ref[...], (((0,), (0,)), ((), ())), preferred_element_type=F32)

    return _pallas(
        body, name=name, grid=grid, in_specs=[a_spec, b_spec], out_specs=out_spec, out_shape=out_shape, args=(a, b),
        sem=["parallel"] * (len(grid) - 1) + ["arbitrary"], ride=ride)


def _ffn_in_bwd(dgu, w, x1, dx2, y1, g_ffn, sc2, ga1, tm, tk):
    t, d = x1.shape
    ff = dgu.shape[2]
    del tk
    nt = (((1,), (1,)), ((), ()))

    def body(a_ref, w_ref, x1_ref, dx2_ref, y1_ref, g_ref, sc_ref, ga_ref, dx1_ref, dy1_ref, st_ref):
        @pl.when(pl.program_id(0) == 0)
        def _():
            st_ref[...] = jnp.zeros_like(st_ref)

        dh = (lax.dot_general(a_ref[0], w_ref[:, :ff], nt, preferred_element_type=F32)
              + lax.dot_general(a_ref[1], w_ref[:, ff:], nt, preferred_element_type=F32))
        x1 = x1_ref[...]
        r = _rms(x1)
        xn = x1 * r
        g = g_ref[...]
        dn = dh * (1.0 + sc_ref[...])
        u = dn * g
        dx1 = dx2_ref[...] + r * (u - xn * jnp.mean(u * xn, axis=-1, keepdims=True))
        dx1_ref[...] = dx1
        dy1_ref[...] = (ga_ref[...] * dx1).astype(BF16)
        st_ref[0:1, :] += jnp.sum(dh, axis=0, keepdims=True)
        st_ref[1:2, :] += jnp.sum(dh * (xn * g), axis=0, keepdims=True)
        st_ref[2:3, :] += jnp.sum(dn * xn, axis=0, keepdims=True)
        st_ref[3:4, :] += jnp.sum(dx1 * y1_ref[...].astype(F32), axis=0, keepdims=True)

    row = pl.BlockSpec((tm, d), lambda i: (i, 0))
    vec = pl.BlockSpec((1, d), lambda i: (0, 0))
    return pl.pallas_call(
        body, name="ffn_in_bwd", grid=(t // tm,),
        in_specs=[pl.BlockSpec((2, tm, ff), lambda i: (0, i, 0)),
                  pl.BlockSpec((d, 2 * ff), lambda i: (0, 0), pipeline_mode=pl.Buffered(1)),
                  row, row, row, vec, vec, vec],
        out_specs=[row, row, pl.BlockSpec((8, d), lambda i: (0, 0))],
        out_shape=[SDS((t, d), F32), SDS((t, d), BF16), SDS((8, d), F32)],
        compiler_params=_params("arbitrary"))(dgu, w, x1, dx2, y1, g_ffn, sc2, ga1)


def _out_proj_bwd(dy1, w_out, tm, ride=None):
    t, d = dy1.shape

    def body(dy_ref, w_ref, o_ref):
        o_ref[...] = lax.dot_general(dy_ref[...], w_ref[...], (((1,), (1,)), ((), ())),
                                     preferred_element_type=F32).astype(BF16)

    row = pl.BlockSpec((tm, d), lambda i: (i, 0))
    return _pallas(body, name="out_proj_bwd", grid=(t // tm,),
                   in_specs=[row, pl.BlockSpec((d, d), lambda i: (0, 0))], out_specs=row,
                   out_shape=SDS((t, d), BF16), args=(dy1, w_out), sem=("parallel",), ride=ride)


def _mixer_bwd(z, dmerged, attn, sinks, conv_w, d, ride=None):
    t, zw = z.shape
    kvw2 = zw - 6 * d
    nb = t // BLOCK
    kv_col = 6 * d // kvw2
    halo = BLOCK // SUBLANES_BF16
    last_halo = t // SUBLANES_BF16 - 1
    scale = HEAD_DIM ** -0.5

    def body(z_ref, kvp_ref, ccp_ref, cxp_ref, cbn_ref, gcn_ref, dm_ref, dmn_ref, attn_ref, sinks_ref, cw_ref,
             dz_ref, dkv_ref, db_ref, dbkv_ref, dcw_ref, dsk_ref, carry_ref):
        n = pl.program_id(0)

        @pl.when(n == 0)
        def _():
            carry_ref[...] = jnp.zeros_like(carry_ref)
            db_ref[...] = jnp.zeros_like(db_ref)
            dbkv_ref[...] = jnp.zeros_like(dbkv_ref)
            dcw_ref[...] = jnp.zeros_like(dcw_ref)
            dsk_ref[...] = jnp.zeros_like(dsk_ref)

        @pl.when(n < nb)
        def _():
            dm = dm_ref[...].astype(F32)
            sa = jax.nn.sigmoid(z_ref[:, 4 * d:5 * d].astype(F32))
            att = attn_ref[...].astype(F32)
            dz_ref[:, 4 * d:5 * d] = (dm * att * sa * (1.0 - sa)).astype(BF16)
            db_ref[0:1, 4 * d:5 * d] += jnp.sum(dm * att * sa * (1.0 - sa), axis=0, keepdims=True)
            dattn = (dm * sa).astype(BF16)

            kv = jnp.concatenate([kvp_ref[...], z_ref[:, 6 * d:]], axis=0)
            k_eff, v_eff = _kv_variants(kv, kvw2 // 2)
            band, col = _attn_masks()
            valid = band & ((n > 0) | (col >= BLOCK))
            lane_lo = lax.broadcasted_iota(jnp.int32, (2 * BLOCK, LANES), 1) < HEAD_DIM
            sink_lane = lax.broadcasted_iota(jnp.int32, (1, LANES), 1)
            rowblk = lax.broadcasted_iota(jnp.int32, (4 * BLOCK, 1), 0) // BLOCK
            dk_acc = [jnp.zeros((2 * BLOCK, LANES), F32), jnp.zeros((2 * BLOCK, LANES), F32)]
            dv_acc = [jnp.zeros((2 * BLOCK, LANES), F32), jnp.zeros((2 * BLOCK, LANES), F32)]
            dsink = jnp.zeros((1, LANES), F32)
            for h in range(2):
                q4 = _stack_pairs(z_ref, h)
                do4 = jnp.concatenate([dattn[:, (4 * h + j) * LANES:(4 * h + j + 1) * LANES] for j in range(4)],
                                      axis=0)
                dq4 = jnp.zeros((4 * BLOCK, LANES), F32)
                for e in range(2):
                    s = lax.dot_general(q4, k_eff[h][e], (((1,), (1,)), ((), ())), preferred_element_type=F32)
                    p, psink = _softmax_sink(s * scale, valid, _sink_column(sinks_ref, h, e))
                    dp = lax.dot_general(do4, v_eff[h][e], (((1,), (1,)), ((), ())), preferred_element_type=F32)
                    delta = jnp.sum(p * dp, axis=-1, keepdims=True)
                    ds = (p * (dp - delta) * scale).astype(BF16)
                    dq4 = dq4 + jnp.dot(ds, k_eff[h][e], preferred_element_type=F32)
                    dk = lax.dot_general(ds, q4, (((0,), (0,)), ((), ())), preferred_element_type=F32)
                    dv = lax.dot_general(p.astype(BF16), do4, (((0,), (0,)), ((), ())), preferred_element_type=F32)
                    keep = lane_lo if e == 0 else jnp.logical_not(lane_lo)
                    slot = 0 if e == h else 1
                    dk_acc[slot] = dk_acc[slot] + jnp.where(keep, dk, 0.0)
                    dv_acc[slot] = dv_acc[slot] + jnp.where(keep, dv, 0.0)
                    dsk = -(psink * delta)
                    for j in range(4):
                        tot = jnp.sum(jnp.where(rowblk == j, dsk, 0.0), axis=0, keepdims=True)
                        dsink = dsink + jnp.where(sink_lane == GROUP * h + 2 * j + e, tot, 0.0)
                for j in range(4):
                    cols = slice((4 * h + j) * LANES, (4 * h + j + 1) * LANES)
                    dqj = dq4[j * BLOCK:(j + 1) * BLOCK]
                    dz_ref[:, cols] = dqj.astype(BF16)
                    db_ref[0:1, cols] += jnp.sum(dqj, axis=0, keepdims=True)
            dsk_ref[0:1, :] += dsink
            dkv_new = jnp.concatenate([dk_acc[0] + pltpu.roll(dk_acc[1], HEAD_DIM, 1),
                                       dv_acc[0] + pltpu.roll(dv_acc[1], HEAD_DIM, 1)], axis=1)
            done = carry_ref[...] + dkv_new[:BLOCK]
            dkv_ref[...] = done.astype(BF16)
            dbkv_ref[0:1, :] += jnp.sum(done, axis=0, keepdims=True)
            carry_ref[...] = dkv_new[BLOCK:]

            cb = z_ref[:, d:2 * d].astype(F32)
            cc = z_ref[:, 2 * d:3 * d].astype(F32)
            cx = z_ref[:, 3 * d:4 * d].astype(F32)
            sg = jax.nn.sigmoid(z_ref[:, 5 * d:6 * d].astype(F32))
            p_in = cc * cx
            prev = jnp.where(n > 0, ccp_ref[...].astype(F32) * cxp_ref[...].astype(F32), 0.0)
            p_m1 = _shift_down(p_in, 1, prev)
            p_m2 = _shift_down(p_in, 2, prev)
            w0, w1, w2 = cw_ref[0:1, :], cw_ref[1:2, :], cw_ref[2:3, :]
            cconv = w0 * p_m2 + w1 * p_m1 + w2 * p_in
            dconv = dm * sg
            dgc = dm * (cb * cconv) * sg * (1.0 - sg)
            dcb = dconv * cconv
            dcc_t = dconv * cb
            nxt = jnp.where(n < nb - 1, dmn_ref[...].astype(F32) * jax.nn.sigmoid(gcn_ref[...].astype(F32))
                            * cbn_ref[...].astype(F32), 0.0)
            dpin = w2 * dcc_t + w1 * _shift_up(dcc_t, 1, nxt) + w0 * _shift_up(dcc_t, 2, nxt)
            for seg, val in ((1, dcb), (2, dpin * cx), (3, dpin * cc), (5, dgc)):
                dz_ref[:, seg * d:(seg + 1) * d] = val.astype(BF16)
                db_ref[0:1, seg * d:(seg + 1) * d] += jnp.sum(val, axis=0, keepdims=True)
            dcw_ref[0:1, :] += jnp.sum(dcc_t * p_m2, axis=0, keepdims=True)
            dcw_ref[1:2, :] += jnp.sum(dcc_t * p_m1, axis=0, keepdims=True)
            dcw_ref[2:3, :] += jnp.sum(dcc_t * p_in, axis=0, keepdims=True)

        @pl.when(n == nb)
        def _():
            done = carry_ref[...]
            dkv_ref[...] = done.astype(BF16)
            dbkv_ref[0:1, :] += jnp.sum(done, axis=0, keepdims=True)

    def cur(n):
        return jnp.minimum(n, nb - 1)

    blk = pl.BlockSpec((BLOCK, d), lambda n: (cur(n), 0))
    return _pallas(
        body, name="mixer_bwd", grid=(nb + 1,), ride=ride, sem=("arbitrary",),
        args=(z, z, z, z, z, z, dmerged, dmerged, attn, sinks, conv_w),
        in_specs=[pl.BlockSpec((BLOCK, zw), lambda n: (cur(n), 0)),
                  pl.BlockSpec((BLOCK, kvw2), lambda n: (jnp.maximum(cur(n) - 1, 0), kv_col)),
                  pl.BlockSpec((SUBLANES_BF16, d), lambda n: (jnp.maximum(cur(n) * halo - 1, 0), 2)),
                  pl.BlockSpec((SUBLANES_BF16, d), lambda n: (jnp.maximum(cur(n) * halo - 1, 0), 3)),
                  pl.BlockSpec((SUBLANES_BF16, d), lambda n: (jnp.minimum((cur(n) + 1) * halo, last_halo), 1)),
                  pl.BlockSpec((SUBLANES_BF16, d), lambda n: (jnp.minimum((cur(n) + 1) * halo, last_halo), 5)),
                  blk,
                  pl.BlockSpec((SUBLANES_BF16, d), lambda n: (jnp.minimum((cur(n) + 1) * halo, last_halo), 0)),
                  blk, SMEM_SPEC, pl.BlockSpec((3, d), lambda n: (0, 0))],
        out_specs=[pl.BlockSpec((BLOCK, 6 * d), lambda n: (cur(n), 0)),
                   pl.BlockSpec((BLOCK, kvw2), lambda n: (jnp.maximum(n - 1, 0), 0)),
                   pl.BlockSpec((8, 6 * d), lambda n: (0, 0)), pl.BlockSpec((8, kvw2), lambda n: (0, 0)),
                   pl.BlockSpec((8, d), lambda n: (0, 0)), pl.BlockSpec((8, LANES), lambda n: (0, 0))],
        out_shape=[SDS((t, 6 * d), BF16), SDS((t, kvw2), BF16), SDS((8, 6 * d), F32), SDS((8, kvw2), F32),
                   SDS((8, d), F32), SDS((8, LANES), F32)],
        scratch=[pltpu.VMEM((BLOCK, kvw2), F32)])


def _in_proj_bwd(dzm, dkv, w, x, dx1, g_mix, sc1, tm):
    t, d = x.shape
    kvw2 = dkv.shape[1]
    wm = dzm.shape[1]
    nt = (((1,), (1,)), ((), ()))

    def body(a_ref, akv_ref, w_ref, x_ref, dx1_ref, g_ref, sc_ref, gx_ref, st_ref):
        @pl.when(pl.program_id(0) == 0)
        def _():
            st_ref[...] = jnp.zeros_like(st_ref)

        dh = (lax.dot_general(a_ref[...], w_ref[:, :wm], nt, preferred_element_type=F32)
              + lax.dot_general(akv_ref[...], w_ref[:, wm:], nt, preferred_element_type=F32))
        xx = x_ref[...]
        r = _rms(xx)
        xn = xx * r
        g = g_ref[...]
        dn = dh * (1.0 + sc_ref[...])
        u = dn * g
        gx_ref[...] = dx1_ref[...] + r * (u - xn * jnp.mean(u * xn, axis=-1, keepdims=True))
        st_ref[0:1, :] += jnp.sum(dh, axis=0, keepdims=True)
        st_ref[1:2, :] += jnp.sum(dh * (xn * g), axis=0, keepdims=True)
        st_ref[2:3, :] += jnp.sum(dn * xn, axis=0, keepdims=True)

    row = pl.BlockSpec((tm, d), lambda i: (i, 0))
    vec = pl.BlockSpec((1, d), lambda i: (0, 0))
    return pl.pallas_call(
        body, name="in_proj_bwd", grid=(t // tm,),
        in_specs=[pl.BlockSpec((tm, wm), lambda i: (i, 0)), pl.BlockSpec((tm, kvw2), lambda i: (i, 0)),
                  pl.BlockSpec((d, wm + kvw2), lambda i: (0, 0), pipeline_mode=pl.Buffered(1)),
                  row, row, vec, vec],
        out_specs=[row, pl.BlockSpec((8, d), lambda i: (0, 0))],
        out_shape=[SDS((t, d), F32), SDS((8, d), F32)],
        compiler_params=_params("arbitrary"))(dzm, dkv, w, x, dx1, g_mix, sc1)


def _permute_cols(a, d, kvw2):
    return jnp.concatenate([a[..., :d], a[..., d + kvw2:], a[..., d:d + kvw2]], axis=-1)


def _unpermute_cols(a, d, kvw2):
    n = a.shape[-1]
    return jnp.concatenate([a[..., :d], a[..., n - kvw2:], a[..., d:n - kvw2]], axis=-1)


def _to_lanes(v, rows=None):
    flat = v.reshape(-1)
    need = -(-flat.shape[0] // LANES)
    need = -(-need // 8) * 8 if rows is None else rows
    return jnp.pad(flat, (0, need * LANES - flat.shape[0])).reshape(need, LANES)


def kernel(x, c, w_ada, b_ada, g_mix, w_in, b_in, sinks, conv_w, w_out, g_ffn, w_ffn_in, w_ffn_out, g_final, loss_target, m_w_ada, m_b_ada, m_g_mix, m_w_in, m_b_in, m_sinks, m_conv_w, m_w_out, m_g_ffn, m_w_ffn_in, m_w_ffn_out, m_g_final, v_w_ada, v_b_ada, v_g_mix, v_w_in, v_b_in, v_sinks, v_conv_w, v_w_out, v_g_ffn, v_w_ffn_in, v_w_ffn_out, v_g_final):
    xs, tgt = x[0], loss_target[0]
    t, d = xs.shape
    zw = w_in.shape[2] * N_CHIP
    kvw2 = zw - 6 * d
    ff = w_ffn_out.shape[1] * N_CHIP
    n_mod = w_ada.shape[2] * N_CHIP // d
    mod_sh = w_ada.shape[2]
    cw_sh = conv_w.shape[2]
    assert d % (8 * LANES) == 0 and kvw2 == 2 * LANES and t % 512 == 0 and n_mod == 6
    xi, yi, ci = _mesh_pos()
    j_me = 2 * xi + yi
    b_me = 4 * xi + 2 * yi + ci
    pos = jnp.stack([ci, j_me]).astype(jnp.int32)
    tm = 512

    pack1 = jnp.concatenate([c.reshape(d // LANES, LANES), conv_w[0].reshape(-1, LANES)], axis=0)
    pack1 = jnp.pad(pack1, ((0, 16 - pack1.shape[0]), (0, 0)))
    g1 = _all_gather_small(pack1, "gather_c")
    c_all = g1[:, :d // LANES, :].reshape(N_DEV, d)
    cw_rows = 3 * cw_sh // LANES
    conv_w_full = jnp.concatenate(
        [g1[2 * j, d // LANES:d // LANES + cw_rows, :].reshape(3, cw_sh) for j in range(N_CHIP)], axis=1)
    b_ada_sh = lax.dynamic_slice(b_ada, (0, j_me * mod_sh), (1, mod_sh))
    mod_all = _all_gather_small(_ada_fwd(c_all, w_ada[0], b_ada_sh), "gather_mod")
    mod = jnp.concatenate([lax.dynamic_index_in_dim(mod_all[2 * j], b_me, 0, keepdims=True) for j in range(N_CHIP)],
                          axis=1)
    sh1, sc1, ga1, sh2, sc2, ga2 = [mod[:, k * d:(k + 1) * d] for k in range(6)]

    (w_in_g,) = _gather_weights([_cast_into_block(pos, w_in[0], "cast_w_in")])
    later = [_cast_into_block(pos, w_out[0], "cast_w_out"), _cast_into_block(pos, w_ffn_in[0], "cast_w_ffn_in"),
             _cast_into_block(pos, w_ffn_out[0], "cast_w_ffn_out")]
    w_in_p = _permute_cols(w_in_g.transpose(1, 0, 2).reshape(d, zw), d, kvw2)
    b_in_p = _permute_cols(b_in, d, kvw2)

    h1 = _prenorm(xs, g_mix, sc1, sh1, tm)
    z, later = _in_proj(h1, w_in_p, b_in_p, tm, zw // 5, ride=_x_gather_ici(later))
    later = _exchange(_x_gather_d2d(later), "gather_forward")
    w_out_f = later[0].reshape(d, d)
    w_ffn_in_f = later[1].transpose(1, 0, 2).reshape(d, 2 * ff)
    w_ffn_out_f = later[2].reshape(ff, d)
    attn, merged = _mixer_fwd(z, sinks, conv_w_full, d)
    y1, x1, h2 = _out_proj_fwd(merged, w_out_f, xs, ga1, g_ffn, sc2, sh2, tm)
    gu, act = _ffn_in_fwd(h2, w_ffn_in_f, ff, tm, ff // 2)
    dx2, dy2, st_loss = _ffn_out_loss(act, w_ffn_out_f, x1, tgt, ga2, g_final.reshape(1, d), tm)

    dgu = _ffn_out_bwd(dy2, w_ffn_out_f, gu, tm, ff // 2)
    tk = 512
    dw_ffn_out, _ = _wgrad(
        act, dy2, pl.BlockSpec((tk, ff // 2), lambda m, k: (k, m)), pl.BlockSpec((tk, d), lambda m, k: (k, 0)),
        pl.BlockSpec((ff // 2, d), lambda m, k: (m, 0)), SDS((ff, d), F32), (2, t // tk), "wgrad_ffn_out")
    dx1, dy1, st_ffn = _ffn_in_bwd(dgu, w_ffn_in_f, x1, dx2, y1, g_ffn, sc2, ga1, tm, ff // 2)
    dw_ffn_in, _ = _wgrad(
        h2, dgu, pl.BlockSpec((tk, d), lambda n, k: (k, 0)),
        pl.BlockSpec((None, tk, ff // 2), lambda n, k: (n // 2, k, n % 2)),
        pl.BlockSpec((None, d, ff // 2), lambda n, k: (n, 0, 0)), SDS((N_CHIP, d, ff // 2), F32),
        (N_CHIP, t // tk), "wgrad_ffn_in")
    dw_out, _ = _wgrad(
        merged, dy1, pl.BlockSpec((tk, d), lambda m, k: (k, 0)), pl.BlockSpec((tk, d), lambda m, k: (k, 0)),
        pl.BlockSpec((d, d), lambda m, k: (0, 0)), SDS((d, d), F32), (1, t // tk), "wgrad_out")

    early = [dw_out.reshape(N_CHIP, d // N_CHIP, d), dw_ffn_in, dw_ffn_out.reshape(N_CHIP, ff // N_CHIP, d)]
    early_names = ["w_out", "w_ffn_in", "w_ffn_out"]
    dmerged, from_sibling = _out_proj_bwd(dy1, w_out_f, tm, ride=_x_pair_send(early))
    chip32, chip16 = zip(*[_pair_add(pos, g, s, "pair_add_" + nm)
                           for g, s, nm in zip(early, from_sibling, early_names)])
    (dzm, dkv, db_main, db_kv, dcw, dsk), from_chips = _mixer_bwd(z, dmerged, attn, sinks, conv_w_full, d,
                                                                  ride=_x_chip_send(list(chip16)))
    fulls = [_chip_add(pos, p, r, "chip_add_" + nm) for p, r, nm in zip(chip32, from_chips, early_names)]
    grad_x, st_in = _in_proj_bwd(dzm, dkv, w_in_p, xs, dx1, g_mix, sc1, tm)
    dw_in_main, (g_w_out, g_w_ffn_in, g_w_ffn_out) = _wgrad(
        h1, dzm, pl.BlockSpec((tk, d), lambda n, k: (k, 0)), pl.BlockSpec((tk, d), lambda n, k: (k, n)),
        pl.BlockSpec((d, d), lambda n, k: (0, n)), SDS((d, 6 * d), F32), (6, t // tk), "wgrad_in_main",
        ride=_x_pair_exchange(fulls))
    dw_in_kv, _ = _wgrad(
        h1, dkv, pl.BlockSpec((tk, d), lambda n, k: (k, 0)), pl.BlockSpec((tk, kvw2), lambda n, k: (k, 0)),
        pl.BlockSpec((d, kvw2), lambda n, k: (0, 0)), SDS((d, kvw2), F32), (1, t // tk), "wgrad_in_kv")
    dw_in = _unpermute_cols(jnp.concatenate([dw_in_main, dw_in_kv], axis=1), d, kvw2)
    dw_in = dw_in.reshape(d, N_CHIP, zw // N_CHIP).transpose(1, 0, 2)

    (from_sibling,) = _exchange(_x_pair_send([dw_in]), "pair_send")
    chip32, chip16 = _pair_add(pos, dw_in, from_sibling, "pair_add_w_in")
    (from_chips,) = _exchange(_x_chip_send([chip16]), "chip_send")
    (g_w_in,) = _exchange(_x_pair_exchange([_chip_add(pos, chip32, from_chips, "chip_add_w_in")]), "pair_exchange")

    dmod = jnp.concatenate([st_in[0:1], st_in[1:2], st_ffn[3:4], st_ffn[0:1], st_ffn[1:2], st_loss[0:1]], axis=1)
    db_in = _unpermute_cols(jnp.concatenate([db_main[0:1], db_kv[0:1]], axis=1), d, kvw2)
    seg = [dmod, st_in[2:3], db_in, dsk[0:1], dcw[0:3].reshape(1, 3 * d), st_ffn[2:3], st_loss[1:2],
           st_loss[2:3, :LANES]]
    sizes = [s.shape[1] for s in seg]
    pack2 = _to_lanes(jnp.concatenate(seg, axis=1))
    packs = _all_gather_small(pack2, "gather_small_grads")
    tot = _pack_sum(packs).reshape(-1)
    offs = [sum(sizes[:k]) for k in range(len(sizes))]
    gb_ada, gg_mix, gb_in, gsinks, gcw, gg_ffn, gg_final, loss_v = [tot[o:o + s] for o, s in zip(offs, sizes)]
    loss = loss_v[0]
    gsinks = gsinks[:sinks.shape[1]]
    gcw_sh = lax.dynamic_slice(gcw.reshape(3, d), (0, j_me * cw_sh), (3, cw_sh))

    dmod_all = packs[:, :n_mod * d // LANES, :].reshape(N_DEV, n_mod * d)
    g_w_ada = _ada_wgrad(c_all, lax.dynamic_slice(dmod_all, (0, j_me * mod_sh), (N_DEV, mod_sh)))

    out_g, out_d, out_m, out_v = {}, {}, {}, {}
    big = {"w_ada": (w_ada, g_w_ada, m_w_ada, v_w_ada), "w_in": (w_in, g_w_in, m_w_in, v_w_in),
           "w_out": (w_out, g_w_out, m_w_out, v_w_out), "w_ffn_in": (w_ffn_in, g_w_ffn_in, m_w_ffn_in, v_w_ffn_in),
           "w_ffn_out": (w_ffn_out, g_w_ffn_out, m_w_ffn_out, v_w_ffn_out)}
    for nm, (w, g, m, v) in big.items():
        dl, nm_, nv_ = _adamw(w[0], g, m[0], v[0], "adamw_" + nm)
        out_g[nm], out_d[nm], out_m[nm], out_v[nm] = g[None], dl[None], nm_[None], nv_[None]
    small = {"b_ada": (b_ada, gb_ada, m_b_ada, v_b_ada), "g_mix": (g_mix, gg_mix, m_g_mix, v_g_mix),
             "b_in": (b_in, gb_in, m_b_in, v_b_in), "sinks": (sinks, gsinks, m_sinks, v_sinks),
             "conv_w": (conv_w, gcw_sh, m_conv_w, v_conv_w), "g_ffn": (g_ffn, gg_ffn, m_g_ffn, v_g_ffn),
             "g_final": (g_final, gg_final, m_g_final, v_g_final)}
    s_sizes = [w.size for w, _, _, _ in small.values()]
    s_rows = -(-sum(s_sizes) // LANES // 8) * 8

    def s_pack(k):
        return _to_lanes(jnp.concatenate([tup[k].reshape(-1) for tup in small.values()]), s_rows)

    s_out = _adamw(s_pack(0), s_pack(1), s_pack(2), s_pack(3), "adamw_small")
    s_off = 0
    for (nm, (w, g, _, _)), sz in zip(small.items(), s_sizes):
        out_g[nm] = g.reshape(w.shape)
        out_d[nm], out_m[nm], out_v[nm] = [o.reshape(-1)[s_off:s_off + sz].reshape(w.shape) for o in s_out]
        s_off += sz

    order = ["w_ada", "b_ada", "g_mix", "w_in", "b_in", "sinks", "conv_w", "w_out", "g_ffn", "w_ffn_in", "w_ffn_out",
             "g_final"]
    return (loss, grad_x[None], *[out_g[k] for k in order], *[out_d[k] for k in order],
            *[out_m[k] for k in order], *[out_v[k] for k in order])
```

```python
import functools

import jax
import jax.numpy as jnp
from jax import lax
from jax.experimental import pallas as pl
from jax.experimental.pallas import tpu as pltpu

F32 = jnp.float32
BF16 = jnp.bfloat16
EPS = 1e-6
HEAD_DIM = 64
GROUP = 8
BLOCK = 128
LANES = 128
SUBLANES_BF16 = 16
N_DEV = 8
N_CHIP = 4
VMEM_LIMIT = 56 * 1024 * 1024
MESH = pl.DeviceIdType.MESH

ADAM_LR = 0.001
ADAM_B1 = 0.9
ADAM_B2 = 0.999
ADAM_EPS = 1e-08
ADAM_WD = 0.01
ADAM_STEP = 10

SDS = jax.ShapeDtypeStruct
ANY = pl.BlockSpec(memory_space=pl.ANY)
VMEM_SPEC = pl.BlockSpec(memory_space=pltpu.VMEM)
SMEM_SPEC = pl.BlockSpec(memory_space=pltpu.SMEM)


def _params(*sem):
    return pltpu.CompilerParams(dimension_semantics=sem, vmem_limit_bytes=VMEM_LIMIT)


def _mesh_pos():
    return lax.axis_index("x"), lax.axis_index("y"), lax.axis_index("c")


def _row_tile(rows, cols, itemsize=4, budget=1 << 20, mult=8):
    best = None
    for t in range(mult, rows + 1, mult):
        if rows % t == 0 and t * cols * itemsize <= budget:
            best = t
    if best is None:
        best = rows
    return best


def _all_gather_small(v, name):
    rows, cols = v.shape

    def body(v_ref, out_ref, send_sems, recv_sems, local_sem):
        x, y, c = _mesh_pos()
        me = 4 * x + 2 * y + c
        mine = pltpu.make_async_copy(v_ref, out_ref.at[me], local_sem)
        mine.start()
        peers = []
        for k in range(1, N_DEV):
            px = 1 - x if k & 4 else x
            py = 1 - y if k & 2 else y
            pc = 1 - c if k & 1 else c
            peers.append((px, py, pc))

        def copy(k, block):
            return pltpu.make_async_remote_copy(
                src_ref=v_ref, dst_ref=out_ref.at[block], send_sem=send_sems.at[k], recv_sem=recv_sems.at[k],
                device_id=peers[k], device_id_type=MESH)

        sends = [copy(k, me) for k in range(N_DEV - 1)]
        for cp in sends:
            cp.start()
        for k, (px, py, pc) in enumerate(peers):
            copy(k, 4 * px + 2 * py + pc).wait_recv()
        for cp in sends:
            cp.wait_send()
        mine.wait()

    return pl.pallas_call(
        body, name=name,
        out_shape=SDS((N_DEV, rows, cols), v.dtype),
        in_specs=[VMEM_SPEC], out_specs=VMEM_SPEC,
        scratch_shapes=[pltpu.SemaphoreType.DMA((N_DEV - 1,)), pltpu.SemaphoreType.DMA((N_DEV - 1,)),
                        pltpu.SemaphoreType.DMA],
    )(v)


def _other_chips(x, y):
    return [(1 - x, y), (x, 1 - y), (1 - x, 1 - y)]


def _gather_weights(bufs):
    n_w = len(bufs)

    def body(*refs):
        outs = refs[n_w:2 * n_w]
        send_sems, recv_sems, fsend_sems, frecv_sems = refs[2 * n_w:]
        x, y, c = _mesh_pos()
        j_me = 2 * x + y
        chips = _other_chips(x, y)
        sibling = (x, y, 1 - c)

        def half_rows(w, which):
            half = outs[w].shape[1] // 2
            return pl.ds(pl.multiple_of(which * half, SUBLANES_BF16), half)

        def copy(w, p, block, rows, over_ici):
            sems = (send_sems, recv_sems) if over_ici else (fsend_sems, frecv_sems)
            return pltpu.make_async_remote_copy(
                src_ref=outs[w].at[block, rows], dst_ref=outs[w].at[block, rows],
                send_sem=sems[0].at[w * 3 + p], recv_sem=sems[1].at[w * 3 + p],
                device_id=(*chips[p], c) if over_ici else sibling, device_id_type=MESH)

        def block_of(p):
            return 2 * chips[p][0] + chips[p][1]

        sends = [copy(w, p, j_me, half_rows(w, c), True) for w in range(n_w) for p in range(3)]
        for cp in sends:
            cp.start()
        forwards = []
        for w in range(n_w):
            for p in range(3):
                copy(w, p, block_of(p), half_rows(w, c), True).wait_recv()
                fw = copy(w, p, block_of(p), half_rows(w, c), False)
                fw.start()
                forwards.append(fw)
        for w in range(n_w):
            for p in range(3):
                copy(w, p, block_of(p), half_rows(w, 1 - c), False).wait_recv()
        for cp in sends + forwards:
            cp.wait_send()

    return pl.pallas_call(
        body, name="gather_weights",
        out_shape=[SDS(b.shape, b.dtype) for b in bufs],
        in_specs=[ANY] * n_w, out_specs=[ANY] * n_w,
        input_output_aliases={w: w for w in range(n_w)},
        scratch_shapes=[pltpu.SemaphoreType.DMA((3 * n_w,)), pltpu.SemaphoreType.DMA((3 * n_w,)),
                        pltpu.SemaphoreType.DMA((3 * n_w,)), pltpu.SemaphoreType.DMA((3 * n_w,))],
    )(*bufs)


class _Exchange:
    def __init__(self, operands, out_shape, in_place, n_sems, copies):
        self.operands, self.out_shape, self.in_place, self.n_sems, self.copies = (
            list(operands), list(out_shape), in_place, n_sems, copies)

    def sems(self):
        return [pltpu.SemaphoreType.DMA((self.n_sems,)), pltpu.SemaphoreType.DMA((self.n_sems,))]


def _x_gather_ici(bufs):
    def copies(ins, outs, send_sems, recv_sems):
        x, y, c = _mesh_pos()
        chips = _other_chips(x, y)
        out = []
        for w in range(len(outs)):
            half = outs[w].shape[1] // 2
            rows = pl.ds(pl.multiple_of(c * half, SUBLANES_BF16), half)
            for p in range(3):
                out.append(pltpu.make_async_remote_copy(
                    src_ref=outs[w].at[2 * x + y, rows], dst_ref=outs[w].at[2 * x + y, rows],
                    send_sem=send_sems.at[w * 3 + p], recv_sem=recv_sems.at[w * 3 + p],
                    device_id=(*chips[p], c), device_id_type=MESH))
        return out

    return _Exchange(bufs, [SDS(b.shape, b.dtype) for b in bufs], True, 3 * len(bufs), copies)


def _x_gather_d2d(bufs):
    def copies(ins, outs, send_sems, recv_sems):
        x, y, c = _mesh_pos()
        chips = _other_chips(x, y)
        out = []
        for w in range(len(outs)):
            half = outs[w].shape[1] // 2
            rows = pl.ds(pl.multiple_of(c * half, SUBLANES_BF16), half)
            for p in range(3):
                block = 2 * chips[p][0] + chips[p][1]
                out.append(pltpu.make_async_remote_copy(
                    src_ref=outs[w].at[block, rows], dst_ref=outs[w].at[block, rows],
                    send_sem=send_sems.at[w * 3 + p], recv_sem=recv_sems.at[w * 3 + p],
                    device_id=(x, y, 1 - c), device_id_type=MESH))
        return out

    return _Exchange(bufs, [SDS(b.shape, b.dtype) for b in bufs], True, 3 * len(bufs), copies)


def _x_pair_send(grads):
    def copies(ins, outs, send_sems, recv_sems):
        x, y, c = _mesh_pos()
        out = []
        for w in range(len(ins)):
            half = ins[w].shape[1] // 2
            rows = pl.ds(pl.multiple_of((1 - c) * half, 8), half)
            out.append(pltpu.make_async_remote_copy(
                src_ref=ins[w].at[:, rows, :], dst_ref=outs[w], send_sem=send_sems.at[w], recv_sem=recv_sems.at[w],
                device_id=(x, y, 1 - c), device_id_type=MESH))
        return out

    return _Exchange(grads, [SDS((N_CHIP, g.shape[1] // 2, g.shape[2]), g.dtype) for g in grads], False,
                     len(grads), copies)


def _x_chip_send(partials):
    def copies(ins, outs, send_sems, recv_sems):
        x, y, c = _mesh_pos()
        chips = _other_chips(x, y)
        out = []
        for w in range(len(ins)):
            for p in range(3):
                out.append(pltpu.make_async_remote_copy(
                    src_ref=ins[w].at[2 * chips[p][0] + chips[p][1]], dst_ref=outs[w].at[p],
                    send_sem=send_sems.at[w * 3 + p], recv_sem=recv_sems.at[w * 3 + p],
                    device_id=(*chips[p], c), device_id_type=MESH))
        return out

    return _Exchange(partials, [SDS((3,) + p.shape[1:], p.dtype) for p in partials], False, 3 * len(partials),
                     copies)


def _x_pair_exchange(fulls):
    def copies(ins, outs, send_sems, recv_sems):
        x, y, c = _mesh_pos()
        out = []
        for w in range(len(outs)):
            half = outs[w].shape[0] // 2
            rows = pl.ds(pl.multiple_of(c * half, 8), half)
            out.append(pltpu.make_async_remote_copy(
                src_ref=outs[w].at[rows], dst_ref=outs[w].at[rows], send_sem=send_sems.at[w],
                recv_sem=recv_sems.at[w], device_id=(x, y, 1 - c), device_id_type=MESH))
        return out

    return _Exchange(fulls, [SDS(f.shape, f.dtype) for f in fulls], True, len(fulls), copies)


def _pallas(body, *, name, grid, in_specs, out_specs, out_shape, args, scratch=(), sem=None, ride=None):
    single = not isinstance(out_specs, (list, tuple))
    out_specs_l = [out_specs] if single else list(out_specs)
    out_shape_l = [out_shape] if single else list(out_shape)
    n_in, n_out, n_scr = len(in_specs), len(out_specs_l), len(scratch)
    if ride is None:
        res = pl.pallas_call(body, name=name, grid=grid, in_specs=list(in_specs), out_specs=out_specs,
                             out_shape=out_shape, scratch_shapes=list(scratch), compiler_params=_params(*sem))(*args)
        return res, None
    n_x, n_xo = len(ride.operands), len(ride.out_shape)

    def full_body(*refs):
        ins, x_ins = refs[:n_in], refs[n_in:n_in + n_x]
        outs = refs[n_in + n_x:n_in + n_x + n_out]
        x_outs = refs[n_in + n_x + n_out:n_in + n_x + n_out + n_xo]
        rest = refs[n_in + n_x + n_out + n_xo:]
        scr, (send_sems, recv_sems) = rest[:n_scr], rest[n_scr:]
        first = functools.reduce(jnp.logical_and, [pl.program_id(a) == 0 for a in range(len(grid))])
        last = functools.reduce(jnp.logical_and, [pl.program_id(a) == grid[a] - 1 for a in range(len(grid))])

        @pl.when(first)
        def _():
            for cp in ride.copies(x_ins, x_outs, send_sems, recv_sems):
                cp.start()

        body(*ins, *outs, *scr)

        @pl.when(last)
        def _():
            for cp in ride.copies(x_ins, x_outs, send_sems, recv_sems):
                cp.wait()

    res = pl.pallas_call(
        full_body, name=name, grid=grid, in_specs=list(in_specs) + [ANY] * n_x,
        out_specs=out_specs_l + [ANY] * n_xo, out_shape=out_shape_l + ride.out_shape,
        input_output_aliases={n_in + k: n_out + k for k in range(n_x)} if ride.in_place else {},
        scratch_shapes=list(scratch) + ride.sems(),
        compiler_params=_params(*(["arbitrary"] * len(grid))))(*args, *ride.operands)
    own = res[0] if single else list(res[:n_out])
    return own, list(res[n_out:])


def _exchange(ride, name):
    n_x, n_xo = len(ride.operands), len(ride.out_shape)

    def body(*refs):
        x_ins, x_outs = refs[:n_x], refs[n_x:n_x + n_xo]
        send_sems, recv_sems = refs[n_x + n_xo:]
        copies = ride.copies(x_ins, x_outs, send_sems, recv_sems)
        for cp in copies:
            cp.start()
        for cp in copies:
            cp.wait()

    return pl.pallas_call(
        body, name=name, in_specs=[ANY] * n_x, out_specs=[ANY] * n_xo, out_shape=ride.out_shape,
        input_output_aliases={k: k for k in range(n_x)} if ride.in_place else {},
        scratch_shapes=ride.sems())(*ride.operands)


def _cast_into_block(pos, w, name):
    rows, cols = w.shape
    tr = _row_tile(rows, cols, mult=SUBLANES_BF16)

    def body(pos_ref, w_ref, o_ref):
        del pos_ref
        o_ref[...] = w_ref[...].astype(BF16)

    return pl.pallas_call(
        body, name=name,
        grid_spec=pltpu.PrefetchScalarGridSpec(
            num_scalar_prefetch=1, grid=(rows // tr,),
            in_specs=[pl.BlockSpec((tr, cols), lambda i, pos_ref: (i, 0))],
            out_specs=pl.BlockSpec((None, tr, cols), lambda i, pos_ref: (pos_ref[1], i, 0))),
        out_shape=SDS((N_CHIP, rows, cols), BF16), compiler_params=_params("parallel"))(pos, w)


def _pair_add(pos, grad, from_sibling, name):
    _, rows, cols = grad.shape
    half = rows // 2
    tr = _row_tile(half, cols, mult=SUBLANES_BF16)
    nblk = half // tr

    def body(pos_ref, g_ref, s_ref, o32_ref, o16_ref):
        del pos_ref
        s = g_ref[...] + s_ref[...]
        o32_ref[...] = s
        o16_ref[...] = s.astype(BF16)

    spec = pl.BlockSpec((None, tr, cols), lambda j, i, pos_ref: (j, i, 0))
    return pl.pallas_call(
        body, name=name,
        grid_spec=pltpu.PrefetchScalarGridSpec(
            num_scalar_prefetch=1, grid=(N_CHIP, nblk),
            in_specs=[pl.BlockSpec((None, tr, cols), lambda j, i, pos_ref: (j, pos_ref[0] * nblk + i, 0)), spec],
            out_specs=[spec, spec]),
        out_shape=[SDS((N_CHIP, half, cols), F32), SDS((N_CHIP, half, cols), BF16)],
        compiler_params=_params("parallel", "parallel"),
    )(pos, grad, from_sibling)


def _chip_add(pos, partial32, from_chips, name):
    _, half, cols = partial32.shape
    tr = _row_tile(half, cols, mult=SUBLANES_BF16)

    def body(pos_ref, p_ref, r_ref, o_ref):
        del pos_ref
        acc = p_ref[...]
        for p in range(3):
            acc = acc + r_ref[p].astype(F32)
        o_ref[...] = acc

    return pl.pallas_call(
        body, name=name,
        grid_spec=pltpu.PrefetchScalarGridSpec(
            num_scalar_prefetch=1, grid=(half // tr,),
            in_specs=[pl.BlockSpec((None, tr, cols), lambda i, pos_ref: (pos_ref[1], i, 0)),
                      pl.BlockSpec((3, tr, cols), lambda i, pos_ref: (0, i, 0))],
            out_specs=pl.BlockSpec((tr, cols), lambda i, pos_ref: (pos_ref[0] * (half // tr) + i, 0))),
        out_shape=SDS((2 * half, cols), F32),
        compiler_params=_params("parallel"),
    )(pos, partial32, from_chips)


def _adamw(w, g, m, v, name):
    rows, cols = w.shape
    tr = _row_tile(rows, cols, budget=1 << 19)

    def body(w_ref, g_ref, m_ref, v_ref, go_ref, d_ref, nm_ref, nv_ref):
        gg = g_ref[...]
        go_ref[...] = gg
        nm = ADAM_B1 * m_ref[...] + (1.0 - ADAM_B1) * gg
        nv = ADAM_B2 * v_ref[...] + (1.0 - ADAM_B2) * (gg * gg)
        m_hat = nm / (1.0 - ADAM_B1 ** ADAM_STEP)
        v_hat = nv / (1.0 - ADAM_B2 ** ADAM_STEP)
        d_ref[...] = -ADAM_LR * (m_hat / (jnp.sqrt(v_hat) + ADAM_EPS) + ADAM_WD * w_ref[...])
        nm_ref[...] = nm
        nv_ref[...] = nv

    spec = pl.BlockSpec((tr, cols), lambda i: (i, 0))
    return pl.pallas_call(body, name=name, grid=(rows // tr,), in_specs=[spec] * 4, out_specs=[spec] * 4,
                          out_shape=[SDS((rows, cols), F32)] * 4, compiler_params=_params("parallel"))(w, g, m, v)


def _pack_sum(gathered):
    _, rows, cols = gathered.shape

    def body(g_ref, o_ref):
        acc = g_ref[0]
        for d in range(1, N_DEV):
            acc = acc + g_ref[d]
        o_ref[...] = acc

    return pl.pallas_call(body, name="pack_sum", in_specs=[VMEM_SPEC], out_specs=VMEM_SPEC,
                          out_shape=SDS((rows, cols), F32))(gathered)


def _ada_fwd(c_all, w_sh, b_sh):
    d, n = w_sh.shape
    tn = 512

    def body(c_ref, w_ref, b_ref, o_ref):
        cc = c_ref[...]
        s = (cc * jax.nn.sigmoid(cc)).astype(BF16)
        o_ref[...] = jnp.dot(s, w_ref[...].astype(BF16), preferred_element_type=F32) + b_ref[...]

    return pl.pallas_call(
        body, name="ada_fwd", grid=(n // tn,),
        in_specs=[pl.BlockSpec((N_DEV, d), lambda j: (0, 0)), pl.BlockSpec((d, tn), lambda j: (0, j)),
                  pl.BlockSpec((1, tn), lambda j: (0, j))],
        out_specs=pl.BlockSpec((N_DEV, tn), lambda j: (0, j)),
        out_shape=SDS((N_DEV, n), F32), compiler_params=_params("parallel"))(c_all, w_sh, b_sh)


def _ada_wgrad(c_all, dmod_sh):
    d = c_all.shape[1]
    n = dmod_sh.shape[1]
    tn = 512

    def body(c_ref, g_ref, o_ref):
        cc = c_ref[...]
        s = cc * jax.nn.sigmoid(cc)
        o_ref[...] = lax.dot_general(s, g_ref[...], (((0,), (0,)), ((), ())), preferred_element_type=F32,
                                     precision=lax.Precision.HIGHEST)

    return pl.pallas_call(
        body, name="ada_wgrad", grid=(n // tn,),
        in_specs=[pl.BlockSpec((N_DEV, d), lambda j: (0, 0)), pl.BlockSpec((N_DEV, tn), lambda j: (0, j))],
        out_specs=pl.BlockSpec((d, tn), lambda j: (0, j)),
        out_shape=SDS((d, n), F32), compiler_params=_params("parallel"))(c_all, dmod_sh)


def _rms(xf):
    return lax.rsqrt(jnp.mean(xf * xf, axis=-1, keepdims=True) + EPS)


def _prenorm(x, g, sc, sh, tm):
    t, d = x.shape

    def body(x_ref, g_ref, sc_ref, sh_ref, h_ref):
        xf = x_ref[...]
        h_ref[...] = ((xf * _rms(xf) * g_ref[...]) * (1.0 + sc_ref[...]) + sh_ref[...]).astype(BF16)

    row = pl.BlockSpec((tm, d), lambda i: (i, 0))
    vec = pl.BlockSpec((1, d), lambda i: (0, 0))
    return pl.pallas_call(body, name="prenorm", grid=(t // tm,), in_specs=[row, vec, vec, vec], out_specs=row,
                          out_shape=SDS((t, d), BF16), compiler_params=_params("parallel"))(x, g, sc, sh)


def _in_proj(h, w, b, tm, tn, ride=None):
    t, d = h.shape
    n = w.shape[1]

    def body(h_ref, w_ref, b_ref, z_ref):
        z_ref[...] = (jnp.dot(h_ref[...], w_ref[...], preferred_element_type=F32) + b_ref[...]).astype(BF16)

    return _pallas(
        body, name="in_proj", grid=(n // tn, t // tm),
        in_specs=[pl.BlockSpec((tm, d), lambda j, i: (i, 0)), pl.BlockSpec((d, tn), lambda j, i: (0, j)),
                  pl.BlockSpec((1, tn), lambda j, i: (0, j))],
        out_specs=pl.BlockSpec((tm, tn), lambda j, i: (i, j)),
        out_shape=SDS((t, n), BF16), args=(h, w, b), sem=("parallel", "parallel"), ride=ride)


def _attn_masks():
    rows = 4 * BLOCK
    r = lax.broadcasted_iota(jnp.int32, (rows, 2 * BLOCK), 0) & (BLOCK - 1)
    col = lax.broadcasted_iota(jnp.int32, (rows, 2 * BLOCK), 1)
    return (col > r) & (col <= r + BLOCK), col


def _kv_variants(kv, n_kv_w):
    assert n_kv_w == LANES
    kb, vb = kv[:, :LANES] * (HEAD_DIM ** -0.5), kv[:, LANES:]
    kr, vr = pltpu.roll(kb, HEAD_DIM, 1), pltpu.roll(vb, HEAD_DIM, 1)
    lane = lax.broadcasted_iota(jnp.int32, kb.shape, 1)
    lo = lane < HEAD_DIM
    zero = jnp.zeros_like(kb)
    k_eff = [[None, None], [None, None]]
    v_eff = [[None, None], [None, None]]
    for h in range(2):
        for e in range(2):
            ksrc, vsrc = (kb, vb) if e == h else (kr, vr)
            keep = lo if e == 0 else jnp.logical_not(lo)
            k_eff[h][e] = jnp.where(keep, ksrc, zero)
            v_eff[h][e] = jnp.where(keep, vsrc, zero)
    return k_eff, v_eff


def _sink_column(sinks_ref, h, e):
    rowblk = lax.broadcasted_iota(jnp.int32, (4 * BLOCK, 1), 0) // BLOCK
    col = jnp.zeros((4 * BLOCK, 1), F32)
    for j in range(4):
        col = jnp.where(rowblk == j, sinks_ref[0, GROUP * h + 2 * j + e], col)
    return col


def _softmax_sink(s, valid, sink):
    s = jnp.where(valid, s, -jnp.inf)
    m = jnp.maximum(jnp.max(s, axis=-1, keepdims=True), sink)
    p = jnp.exp(s - m)
    psink = jnp.exp(sink - m)
    den = jnp.sum(p, axis=-1, keepdims=True) + psink
    inv = 1.0 / den
    return p * inv, psink * inv


def _shift_down(a, s, prev):
    rows = a.shape[0]
    out = pltpu.roll(a, s, 0)
    row = lax.broadcasted_iota(jnp.int32, a.shape, 0)
    for t in range(s):
        out = jnp.where(row == t, prev[SUBLANES_BF16 - s + t:SUBLANES_BF16 - s + t + 1, :], out)
    del rows
    return out


def _shift_up(a, s, nxt):
    rows = a.shape[0]
    out = pltpu.roll(a, rows - s, 0)
    row = lax.broadcasted_iota(jnp.int32, a.shape, 0)
    for t in range(s):
        out = jnp.where(row == rows - s + t, nxt[t:t + 1, :], out)
    return out


def _stack_pairs(ref, h):
    return jnp.concatenate([ref[:, (4 * h + j) * LANES:(4 * h + j + 1) * LANES] for j in range(4)], axis=0)


def _mixer_fwd(z, sinks, conv_w, d):
    t, zw = z.shape
    kvw2 = zw - 6 * d
    nb = t // BLOCK
    kv_col = 6 * d // kvw2
    halo = BLOCK // SUBLANES_BF16

    def body(z_ref, kvp_ref, ccp_ref, cxp_ref, sinks_ref, cw_ref, attn_ref, merged_ref):
        n = pl.program_id(0)
        kv = jnp.concatenate([kvp_ref[...], z_ref[:, 6 * d:]], axis=0)
        k_eff, v_eff = _kv_variants(kv, kvw2 // 2)
        band, col = _attn_masks()
        valid = band & ((n > 0) | (col >= BLOCK))
        for h in range(2):
            q4 = _stack_pairs(z_ref, h)
            o4 = jnp.zeros((4 * BLOCK, LANES), F32)
            for e in range(2):
                s = lax.dot_general(q4, k_eff[h][e], (((1,), (1,)), ((), ())), preferred_element_type=F32)
                p, _ = _softmax_sink(s, valid, _sink_column(sinks_ref, h, e))
                o4 = o4 + jnp.dot(p.astype(BF16), v_eff[h][e], preferred_element_type=F32)
            for j in range(4):
                attn_ref[:, (4 * h + j) * LANES:(4 * h + j + 1) * LANES] = o4[j * BLOCK:(j + 1) * BLOCK].astype(BF16)
        cb = z_ref[:, d:2 * d].astype(F32)
        p_in = z_ref[:, 2 * d:3 * d].astype(F32) * z_ref[:, 3 * d:4 * d].astype(F32)
        prev = jnp.where(n > 0, ccp_ref[...].astype(F32) * cxp_ref[...].astype(F32), 0.0)
        cconv = (cw_ref[0:1, :] * _shift_down(p_in, 2, prev) + cw_ref[1:2, :] * _shift_down(p_in, 1, prev)
                 + cw_ref[2:3, :] * p_in)
        sa = jax.nn.sigmoid(z_ref[:, 4 * d:5 * d].astype(F32))
        sg = jax.nn.sigmoid(z_ref[:, 5 * d:6 * d].astype(F32))
        merged_ref[...] = (sa * attn_ref[...].astype(F32) + sg * (cb * cconv)).astype(BF16)

    blk = pl.BlockSpec((BLOCK, d), lambda n: (n, 0))
    return pl.pallas_call(
        body, name="mixer_fwd", grid=(nb,),
        in_specs=[pl.BlockSpec((BLOCK, zw), lambda n: (n, 0)),
                  pl.BlockSpec((BLOCK, kvw2), lambda n: (jnp.maximum(n - 1, 0), kv_col)),
                  pl.BlockSpec((SUBLANES_BF16, d), lambda n: (jnp.maximum(n * halo - 1, 0), 2)),
                  pl.BlockSpec((SUBLANES_BF16, d), lambda n: (jnp.maximum(n * halo - 1, 0), 3)),
                  SMEM_SPEC, pl.BlockSpec((3, d), lambda n: (0, 0))],
        out_specs=[blk, blk],
        out_shape=[SDS((t, d), BF16), SDS((t, d), BF16)],
        compiler_params=_params("parallel"))(z, z, z, z, sinks, conv_w)


def _out_proj_fwd(merged, w_out, x, ga1, g_ffn, sc2, sh2, tm):
    t, d = x.shape

    def body(m_ref, w_ref, x_ref, ga_ref, g_ref, sc_ref, sh_ref, y_ref, x1_ref, h_ref):
        y = jnp.dot(m_ref[...], w_ref[...], preferred_element_type=F32)
        x1 = x_ref[...] + ga_ref[...] * y
        y_ref[...] = y.astype(BF16)
        x1_ref[...] = x1
        h_ref[...] = ((x1 * _rms(x1) * g_ref[...]) * (1.0 + sc_ref[...]) + sh_ref[...]).astype(BF16)

    row = pl.BlockSpec((tm, d), lambda i: (i, 0))
    vec = pl.BlockSpec((1, d), lambda i: (0, 0))
    return pl.pallas_call(
        body, name="out_proj_fwd", grid=(t // tm,),
        in_specs=[row, pl.BlockSpec((d, d), lambda i: (0, 0)), row, vec, vec, vec, vec],
        out_specs=[row, row, row],
        out_shape=[SDS((t, d), BF16), SDS((t, d), F32), SDS((t, d), BF16)],
        compiler_params=_params("parallel"))(merged, w_out, x, ga1, g_ffn, sc2, sh2)


def _ffn_in_fwd(h2, w, ff, tm, tn):
    t, d = h2.shape
    nj = ff // tn

    def body(h_ref, wg_ref, wu_ref, gu_ref, act_ref):
        hh = h_ref[...]
        g = jnp.dot(hh, wg_ref[...], preferred_element_type=F32)
        u = jnp.dot(hh, wu_ref[...], preferred_element_type=F32)
        gu_ref[0] = g.astype(BF16)
        gu_ref[1] = u.astype(BF16)
        act_ref[...] = ((g * jax.nn.sigmoid(g)) * u).astype(BF16)

    return pl.pallas_call(
        body, name="ffn_in_fwd", grid=(nj, t // tm),
        in_specs=[pl.BlockSpec((tm, d), lambda j, i: (i, 0)), pl.BlockSpec((d, tn), lambda j, i: (0, j)),
                  pl.BlockSpec((d, tn), lambda j, i: (0, j + nj))],
        out_specs=[pl.BlockSpec((2, tm, tn), lambda j, i: (0, i, j)), pl.BlockSpec((tm, tn), lambda j, i: (i, j))],
        out_shape=[SDS((2, t, ff), BF16), SDS((t, ff), BF16)],
        compiler_params=_params("parallel", "parallel"))(h2, w, w)


def _ffn_out_loss(act, w, x1, target, ga2, g_final, tm):
    t, d = x1.shape
    ff = act.shape[1]

    def body(a_ref, w_ref, x1_ref, tg_ref, ga_ref, gf_ref, dx2_ref, dy2_ref, st_ref):
        @pl.when(pl.program_id(0) == 0)
        def _():
            st_ref[...] = jnp.zeros_like(st_ref)

        y2 = jnp.dot(a_ref[...], w_ref[...], preferred_element_type=F32)
        x2 = x1_ref[...] + ga_ref[...] * y2
        r = _rms(x2)
        yn = x2 * r
        err = yn * gf_ref[...] - tg_ref[...]
        loss = 0.5 * jnp.sum(jnp.mean(err * err, axis=-1, keepdims=True), axis=0, keepdims=True)
        dy = err * (1.0 / d)
        u = dy * gf_ref[...]
        dx2 = r * (u - yn * jnp.mean(u * yn, axis=-1, keepdims=True))
        dx2_ref[...] = dx2
        dy2_ref[...] = (ga_ref[...] * dx2).astype(BF16)
        st_ref[0:1, :] += jnp.sum(dx2 * y2, axis=0, keepdims=True)
        st_ref[1:2, :] += jnp.sum(dy * yn, axis=0, keepdims=True)
        st_ref[2:3, :] += jnp.broadcast_to(loss, (1, d))

    row = pl.BlockSpec((tm, d), lambda i: (i, 0))
    vec = pl.BlockSpec((1, d), lambda i: (0, 0))
    return pl.pallas_call(
        body, name="ffn_out_loss", grid=(t // tm,),
        in_specs=[pl.BlockSpec((tm, ff), lambda i: (i, 0)), pl.BlockSpec((ff, d), lambda i: (0, 0)), row, row,
                  vec, vec],
        out_specs=[row, row, pl.BlockSpec((8, d), lambda i: (0, 0))],
        out_shape=[SDS((t, d), F32), SDS((t, d), BF16), SDS((8, d), F32)],
        compiler_params=_params("arbitrary"))(act, w, x1, target, ga2, g_final)


def _ffn_out_bwd(dy2, w, gu, tm, tn):
    t, d = dy2.shape
    ff = w.shape[0]

    def body(dy_ref, w_ref, gu_ref, o_ref):
        dact = lax.dot_general(dy_ref[...], w_ref[...], (((1,), (1,)), ((), ())), preferred_element_type=F32)
        g = gu_ref[0].astype(F32)
        u = gu_ref[1].astype(F32)
        sg = jax.nn.sigmoid(g)
        o_ref[0] = (dact * u * (sg * (1.0 + g * (1.0 - sg)))).astype(BF16)
        o_ref[1] = (dact * (g * sg)).astype(BF16)

    gu_spec = pl.BlockSpec((2, tm, tn), lambda j, i: (0, i, j))
    return pl.pallas_call(
        body, name="ffn_out_bwd", grid=(ff // tn, t // tm),
        in_specs=[pl.BlockSpec((tm, d), lambda j, i: (i, 0)), pl.BlockSpec((tn, d), lambda j, i: (j, 0)), gu_spec],
        out_specs=gu_spec, out_shape=SDS((2, t, ff), BF16),
        compiler_params=_params("parallel", "parallel"))(dy2, w, gu)


def _wgrad(a, b, a_spec, b_spec, out_spec, out_shape, grid, name, ride=None):
    def body(a_ref, b_ref, o_ref):
        @pl.when(pl.program_id(len(grid) - 1) == 0)
        def _():
            o_ref[...] = jnp.zeros_like(o_ref)

        o_ref[...] += lax.dot_general(a_ref[...], b_ref[...], (((0,), (0,)), ((), ())), preferred_element_type=F32)

    return _pallas(
        body, name=name, grid=grid, in_specs=[a_spec, b_spec], out_specs=out_spec, out_shape=out_shape, args=(a, b),
        sem=["parallel"] * (len(grid) - 1) + ["arbitrary"], ride=ride)


def _ffn_in_bwd(dgu, w, x1, dx2, y1, g_ffn, sc2, ga1, tm, tk):
    t, d = x1.shape
    ff = dgu.shape[2]
    del tk
    nt = (((1,), (1,)), ((), ()))

    def body(a_ref, w_ref, x1_ref, dx2_ref, y1_ref, g_ref, sc_ref, ga_ref, dx1_ref, dy1_ref, st_ref):
        @pl.when(pl.program_id(0) == 0)
        def _():
            st_ref[...] = jnp.zeros_like(st_ref)

        dh = (lax.dot_general(a_ref[0], w_ref[:, :ff], nt, preferred_element_type=F32)
              + lax.dot_general(a_ref[1], w_ref[:, ff:], nt, preferred_element_type=F32))
        x1 = x1_ref[...]
        r = _rms(x1)
        xn = x1 * r
        g = g_ref[...]
        dn = dh * (1.0 + sc_ref[...])
        u = dn * g
        dx1 = dx2_ref[...] + r * (u - xn * jnp.mean(u * xn, axis=-1, keepdims=True))
        dx1_ref[...] = dx1
        dy1_ref[...] = (ga_ref[...] * dx1).astype(BF16)
        st_ref[0:1, :] += jnp.sum(dh, axis=0, keepdims=True)
        st_ref[1:2, :] += jnp.sum(dh * (xn * g), axis=0, keepdims=True)
        st_ref[2:3, :] += jnp.sum(dn * xn, axis=0, keepdims=True)
        st_ref[3:4, :] += jnp.sum(dx1 * y1_ref[...].astype(F32), axis=0, keepdims=True)

    row = pl.BlockSpec((tm, d), lambda i: (i, 0))
    vec = pl.BlockSpec((1, d), lambda i: (0, 0))
    return pl.pallas_call(
        body, name="ffn_in_bwd", grid=(t // tm,),
        in_specs=[pl.BlockSpec((2, tm, ff), lambda i: (0, i, 0)),
                  pl.BlockSpec((d, 2 * ff), lambda i: (0, 0), pipeline_mode=pl.Buffered(1)),
                  row, row, row, vec, vec, vec],
        out_specs=[row, row, pl.BlockSpec((8, d), lambda i: (0, 0))],
        out_shape=[SDS((t, d), F32), SDS((t, d), BF16), SDS((8, d), F32)],
        compiler_params=_params("arbitrary"))(dgu, w, x1, dx2, y1, g_ffn, sc2, ga1)


def _out_proj_bwd(dy1, w_out, tm, ride=None):
    t, d = dy1.shape

    def body(dy_ref, w_ref, o_ref):
        o_ref[...] = lax.dot_general(dy_ref[...], w_ref[...], (((1,), (1,)), ((), ())),
                                     preferred_element_type=F32).astype(BF16)

    row = pl.BlockSpec((tm, d), lambda i: (i, 0))
    return _pallas(body, name="out_proj_bwd", grid=(t // tm,),
                   in_specs=[row, pl.BlockSpec((d, d), lambda i: (0, 0))], out_specs=row,
                   out_shape=SDS((t, d), BF16), args=(dy1, w_out), sem=("parallel",), ride=ride)


def _mixer_bwd(z, dmerged, attn, sinks, conv_w, d, ride=None):
    t, zw = z.shape
    kvw2 = zw - 6 * d
    nb = t // BLOCK
    kv_col = 6 * d // kvw2
    halo = BLOCK // SUBLANES_BF16
    last_halo = t // SUBLANES_BF16 - 1
    scale = HEAD_DIM ** -0.5

    def body(z_ref, kvp_ref, ccp_ref, cxp_ref, cbn_ref, gcn_ref, dm_ref, dmn_ref, attn_ref, sinks_ref, cw_ref,
             dz_ref, dkv_ref, db_ref, dbkv_ref, dcw_ref, dsk_ref, carry_ref):
        n = pl.program_id(0)

        @pl.when(n == 0)
        def _():
            carry_ref[...] = jnp.zeros_like(carry_ref)
            db_ref[...] = jnp.zeros_like(db_ref)
            dbkv_ref[...] = jnp.zeros_like(dbkv_ref)
            dcw_ref[...] = jnp.zeros_like(dcw_ref)
            dsk_ref[...] = jnp.zeros_like(dsk_ref)

        @pl.when(n < nb)
        def _():
            dm = dm_ref[...].astype(F32)
            sa = jax.nn.sigmoid(z_ref[:, 4 * d:5 * d].astype(F32))
            att = attn_ref[...].astype(F32)
            dz_ref[:, 4 * d:5 * d] = (dm * att * sa * (1.0 - sa)).astype(BF16)
            db_ref[0:1, 4 * d:5 * d] += jnp.sum(dm * att * sa * (1.0 - sa), axis=0, keepdims=True)
            dattn = (dm * sa).astype(BF16)

            kv = jnp.concatenate([kvp_ref[...], z_ref[:, 6 * d:]], axis=0)
            k_eff, v_eff = _kv_variants(kv, kvw2 // 2)
            band, col = _attn_masks()
            valid = band & ((n > 0) | (col >= BLOCK))
            lane_lo = lax.broadcasted_iota(jnp.int32, (2 * BLOCK, LANES), 1) < HEAD_DIM
            sink_lane = lax.broadcasted_iota(jnp.int32, (1, LANES), 1)
            rowblk = lax.broadcasted_iota(jnp.int32, (4 * BLOCK, 1), 0) // BLOCK
            dk_acc = [jnp.zeros((2 * BLOCK, LANES), F32), jnp.zeros((2 * BLOCK, LANES), F32)]
            dv_acc = [jnp.zeros((2 * BLOCK, LANES), F32), jnp.zeros((2 * BLOCK, LANES), F32)]
            dsink = jnp.zeros((1, LANES), F32)
            for h in range(2):
                q4 = _stack_pairs(z_ref, h)
                do4 = jnp.concatenate([dattn[:, (4 * h + j) * LANES:(4 * h + j + 1) * LANES] for j in range(4)],
                                      axis=0)
                dq4 = jnp.zeros((4 * BLOCK, LANES), F32)
                for e in range(2):
                    s = lax.dot_general(q4, k_eff[h][e], (((1,), (1,)), ((), ())), preferred_element_type=F32)
                    p, psink = _softmax_sink(s, valid, _sink_column(sinks_ref, h, e))
                    dp = lax.dot_general(do4, v_eff[h][e], (((1,), (1,)), ((), ())), preferred_element_type=F32)
                    delta = jnp.sum(p * dp, axis=-1, keepdims=True)
                    ds = (p * (dp - delta)).astype(BF16)
                    dq4 = dq4 + jnp.dot(ds, k_eff[h][e], preferred_element_type=F32)
                    dk = lax.dot_general(ds, q4, (((0,), (0,)), ((), ())), preferred_element_type=F32)
                    dv = lax.dot_general(p.astype(BF16), do4, (((0,), (0,)), ((), ())), preferred_element_type=F32)
                    keep = lane_lo if e == 0 else jnp.logical_not(lane_lo)
                    slot = 0 if e == h else 1
                    dk_acc[slot] = dk_acc[slot] + jnp.where(keep, dk, 0.0)
                    dv_acc[slot] = dv_acc[slot] + jnp.where(keep, dv, 0.0)
                    dsk = -(psink * delta)
                    for j in range(4):
                        tot = jnp.sum(jnp.where(rowblk == j, dsk, 0.0), axis=0, keepdims=True)
                        dsink = dsink + jnp.where(sink_lane == GROUP * h + 2 * j + e, tot, 0.0)
                for j in range(4):
                    cols = slice((4 * h + j) * LANES, (4 * h + j + 1) * LANES)
                    dqj = dq4[j * BLOCK:(j + 1) * BLOCK]
                    dz_ref[:, cols] = dqj.astype(BF16)
                    db_ref[0:1, cols] += jnp.sum(dqj, axis=0, keepdims=True)
            dsk_ref[0:1, :] += dsink
            dkv_new = jnp.concatenate([(dk_acc[0] + pltpu.roll(dk_acc[1], HEAD_DIM, 1)) * scale,
                                       dv_acc[0] + pltpu.roll(dv_acc[1], HEAD_DIM, 1)], axis=1)
            done = carry_ref[...] + dkv_new[:BLOCK]
            dkv_ref[...] = done.astype(BF16)
            dbkv_ref[0:1, :] += jnp.sum(done, axis=0, keepdims=True)
            carry_ref[...] = dkv_new[BLOCK:]

            cb = z_ref[:, d:2 * d].astype(F32)
            cc = z_ref[:, 2 * d:3 * d].astype(F32)
            cx = z_ref[:, 3 * d:4 * d].astype(F32)
            sg = jax.nn.sigmoid(z_ref[:, 5 * d:6 * d].astype(F32))
            p_in = cc * cx
            prev = jnp.where(n > 0, ccp_ref[...].astype(F32) * cxp_ref[...].astype(F32), 0.0)
            p_m1 = _shift_down(p_in, 1, prev)
            p_m2 = _shift_down(p_in, 2, prev)
            w0, w1, w2 = cw_ref[0:1, :], cw_ref[1:2, :], cw_ref[2:3, :]
            cconv = w0 * p_m2 + w1 * p_m1 + w2 * p_in
            dconv = dm * sg
            dgc = dm * (cb * cconv) * sg * (1.0 - sg)
            dcb = dconv * cconv
            dcc_t = dconv * cb
            nxt = jnp.where(n < nb - 1, dmn_ref[...].astype(F32) * jax.nn.sigmoid(gcn_ref[...].astype(F32))
                            * cbn_ref[...].astype(F32), 0.0)
            dpin = w2 * dcc_t + w1 * _shift_up(dcc_t, 1, nxt) + w0 * _shift_up(dcc_t, 2, nxt)
            for seg, val in ((1, dcb), (2, dpin * cx), (3, dpin * cc), (5, dgc)):
                dz_ref[:, seg * d:(seg + 1) * d] = val.astype(BF16)
                db_ref[0:1, seg * d:(seg + 1) * d] += jnp.sum(val, axis=0, keepdims=True)
            dcw_ref[0:1, :] += jnp.sum(dcc_t * p_m2, axis=0, keepdims=True)
            dcw_ref[1:2, :] += jnp.sum(dcc_t * p_m1, axis=0, keepdims=True)
            dcw_ref[2:3, :] += jnp.sum(dcc_t * p_in, axis=0, keepdims=True)

        @pl.when(n == nb)
        def _():
            done = carry_ref[...]
            dkv_ref[...] = done.astype(BF16)
            dbkv_ref[0:1, :] += jnp.sum(done, axis=0, keepdims=True)

    def cur(n):
        return jnp.minimum(n, nb - 1)

    blk = pl.BlockSpec((BLOCK, d), lambda n: (cur(n), 0))
    return _pallas(
        body, name="mixer_bwd", grid=(nb + 1,), ride=ride, sem=("arbitrary",),
        args=(z, z, z, z, z, z, dmerged, dmerged, attn, sinks, conv_w),
        in_specs=[pl.BlockSpec((BLOCK, zw), lambda n: (cur(n), 0)),
                  pl.BlockSpec((BLOCK, kvw2), lambda n: (jnp.maximum(cur(n) - 1, 0), kv_col)),
                  pl.BlockSpec((SUBLANES_BF16, d), lambda n: (jnp.maximum(cur(n) * halo - 1, 0), 2)),
                  pl.BlockSpec((SUBLANES_BF16, d), lambda n: (jnp.maximum(cur(n) * halo - 1, 0), 3)),
                  pl.BlockSpec((SUBLANES_BF16, d), lambda n: (jnp.minimum((cur(n) + 1) * halo, last_halo), 1)),
                  pl.BlockSpec((SUBLANES_BF16, d), lambda n: (jnp.minimum((cur(n) + 1) * halo, last_halo), 5)),
                  blk,
                  pl.BlockSpec((SUBLANES_BF16, d), lambda n: (jnp.minimum((cur(n) + 1) * halo, last_halo), 0)),
                  blk, SMEM_SPEC, pl.BlockSpec((3, d), lambda n: (0, 0))],
        out_specs=[pl.BlockSpec((BLOCK, 6 * d), lambda n: (cur(n), 0)),
                   pl.BlockSpec((BLOCK, kvw2), lambda n: (jnp.maximum(n - 1, 0), 0)),
                   pl.BlockSpec((8, 6 * d), lambda n: (0, 0)), pl.BlockSpec((8, kvw2), lambda n: (0, 0)),
                   pl.BlockSpec((8, d), lambda n: (0, 0)), pl.BlockSpec((8, LANES), lambda n: (0, 0))],
        out_shape=[SDS((t, 6 * d), BF16), SDS((t, kvw2), BF16), SDS((8, 6 * d), F32), SDS((8, kvw2), F32),
                   SDS((8, d), F32), SDS((8, LANES), F32)],
        scratch=[pltpu.VMEM((BLOCK, kvw2), F32)])


def _in_proj_bwd(dzm, dkv, w, x, dx1, g_mix, sc1, tm, ride=None):
    t, d = x.shape
    kvw2 = dkv.shape[1]
    wm = dzm.shape[1]
    nt = (((1,), (1,)), ((), ()))

    def body(a_ref, akv_ref, w_ref, x_ref, dx1_ref, g_ref, sc_ref, gx_ref, st_ref):
        @pl.when(pl.program_id(0) == 0)
        def _():
            st_ref[...] = jnp.zeros_like(st_ref)

        dh = (lax.dot_general(a_ref[...], w_ref[:, :wm], nt, preferred_element_type=F32)
              + lax.dot_general(akv_ref[...], w_ref[:, wm:], nt, preferred_element_type=F32))
        xx = x_ref[...]
        r = _rms(xx)
        xn = xx * r
        g = g_ref[...]
        dn = dh * (1.0 + sc_ref[...])
        u = dn * g
        gx_ref[...] = dx1_ref[...] + r * (u - xn * jnp.mean(u * xn, axis=-1, keepdims=True))
        st_ref[0:1, :] += jnp.sum(dh, axis=0, keepdims=True)
        st_ref[1:2, :] += jnp.sum(dh * (xn * g), axis=0, keepdims=True)
        st_ref[2:3, :] += jnp.sum(dn * xn, axis=0, keepdims=True)

    row = pl.BlockSpec((tm, d), lambda i: (i, 0))
    vec = pl.BlockSpec((1, d), lambda i: (0, 0))
    return _pallas(
        body, name="in_proj_bwd", grid=(t // tm,),
        in_specs=[pl.BlockSpec((tm, wm), lambda i: (i, 0)), pl.BlockSpec((tm, kvw2), lambda i: (i, 0)),
                  pl.BlockSpec((d, wm + kvw2), lambda i: (0, 0), pipeline_mode=pl.Buffered(1)),
                  row, row, vec, vec],
        out_specs=[row, pl.BlockSpec((8, d), lambda i: (0, 0))],
        out_shape=[SDS((t, d), F32), SDS((8, d), F32)],
        args=(dzm, dkv, w, x, dx1, g_mix, sc1), sem=("arbitrary",), ride=ride)


def _permute_cols(a, d, kvw2):
    return jnp.concatenate([a[..., :d], a[..., d + kvw2:], a[..., d:d + kvw2]], axis=-1)


def _unpermute_cols(a, d, kvw2):
    n = a.shape[-1]
    return jnp.concatenate([a[..., :d], a[..., n - kvw2:], a[..., d:n - kvw2]], axis=-1)


def _to_lanes(v, rows=None):
    flat = v.reshape(-1)
    need = -(-flat.shape[0] // LANES)
    need = -(-need // 8) * 8 if rows is None else rows
    return jnp.pad(flat, (0, need * LANES - flat.shape[0])).reshape(need, LANES)


def kernel(x, c, w_ada, b_ada, g_mix, w_in, b_in, sinks, conv_w, w_out, g_ffn, w_ffn_in, w_ffn_out, g_final, loss_target, m_w_ada, m_b_ada, m_g_mix, m_w_in, m_b_in, m_sinks, m_conv_w, m_w_out, m_g_ffn, m_w_ffn_in, m_w_ffn_out, m_g_final, v_w_ada, v_b_ada, v_g_mix, v_w_in, v_b_in, v_sinks, v_conv_w, v_w_out, v_g_ffn, v_w_ffn_in, v_w_ffn_out, v_g_final):
    xs, tgt = x[0], loss_target[0]
    t, d = xs.shape
    zw = w_in.shape[2] * N_CHIP
    kvw2 = zw - 6 * d
    ff = w_ffn_out.shape[1] * N_CHIP
    n_mod = w_ada.shape[2] * N_CHIP // d
    mod_sh = w_ada.shape[2]
    cw_sh = conv_w.shape[2]
    assert d % (8 * LANES) == 0 and kvw2 == 2 * LANES and t % 512 == 0 and n_mod == 6
    xi, yi, ci = _mesh_pos()
    j_me = 2 * xi + yi
    b_me = 4 * xi + 2 * yi + ci
    pos = jnp.stack([ci, j_me]).astype(jnp.int32)
    tm = 512

    pack1 = jnp.concatenate([c.reshape(d // LANES, LANES), conv_w[0].reshape(-1, LANES)], axis=0)
    pack1 = jnp.pad(pack1, ((0, 16 - pack1.shape[0]), (0, 0)))
    g1 = _all_gather_small(pack1, "gather_c")
    c_all = g1[:, :d // LANES, :].reshape(N_DEV, d)
    cw_rows = 3 * cw_sh // LANES
    conv_w_full = jnp.concatenate(
        [g1[2 * j, d // LANES:d // LANES + cw_rows, :].reshape(3, cw_sh) for j in range(N_CHIP)], axis=1)
    b_ada_sh = lax.dynamic_slice(b_ada, (0, j_me * mod_sh), (1, mod_sh))
    mod_all = _all_gather_small(_ada_fwd(c_all, w_ada[0], b_ada_sh), "gather_mod")
    mod = jnp.concatenate([lax.dynamic_index_in_dim(mod_all[2 * j], b_me, 0, keepdims=True) for j in range(N_CHIP)],
                          axis=1)
    sh1, sc1, ga1, sh2, sc2, ga2 = [mod[:, k * d:(k + 1) * d] for k in range(6)]

    (w_in_g,) = _gather_weights([_cast_into_block(pos, w_in[0], "cast_w_in")])
    later = [_cast_into_block(pos, w_out[0], "cast_w_out"), _cast_into_block(pos, w_ffn_in[0], "cast_w_ffn_in"),
             _cast_into_block(pos, w_ffn_out[0], "cast_w_ffn_out")]
    w_in_p = _permute_cols(w_in_g.transpose(1, 0, 2).reshape(d, zw), d, kvw2)
    b_in_p = _permute_cols(b_in, d, kvw2)

    h1 = _prenorm(xs, g_mix, sc1, sh1, tm)
    z, later = _in_proj(h1, w_in_p, b_in_p, min(t, 1024), zw // 5, ride=_x_gather_ici(later))
    later = _exchange(_x_gather_d2d(later), "gather_forward")
    w_out_f = later[0].reshape(d, d)
    w_ffn_in_f = later[1].transpose(1, 0, 2).reshape(d, 2 * ff)
    w_ffn_out_f = later[2].reshape(ff, d)
    attn, merged = _mixer_fwd(z, sinks, conv_w_full, d)
    y1, x1, h2 = _out_proj_fwd(merged, w_out_f, xs, ga1, g_ffn, sc2, sh2, tm)
    gu, act = _ffn_in_fwd(h2, w_ffn_in_f, ff, tm, ff // 2)
    dx2, dy2, st_loss = _ffn_out_loss(act, w_ffn_out_f, x1, tgt, ga2, g_final.reshape(1, d), tm)

    dgu = _ffn_out_bwd(dy2, w_ffn_out_f, gu, tm, ff // 2)
    tk = min(t, 2048)
    dw_ffn_out, _ = _wgrad(
        act, dy2, pl.BlockSpec((tk, ff // 2), lambda m, k: (k, m)), pl.BlockSpec((tk, d), lambda m, k: (k, 0)),
        pl.BlockSpec((ff // 2, d), lambda m, k: (m, 0)), SDS((ff, d), F32), (2, t // tk), "wgrad_ffn_out")
    dx1, dy1, st_ffn = _ffn_in_bwd(dgu, w_ffn_in_f, x1, dx2, y1, g_ffn, sc2, ga1, tm, ff // 2)
    dw_ffn_in, _ = _wgrad(
        h2, dgu, pl.BlockSpec((tk, d), lambda n, k: (k, 0)),
        pl.BlockSpec((None, tk, ff // 2), lambda n, k: (n // 2, k, n % 2)),
        pl.BlockSpec((None, d, ff // 2), lambda n, k: (n, 0, 0)), SDS((N_CHIP, d, ff // 2), F32),
        (N_CHIP, t // tk), "wgrad_ffn_in")
    dw_out, _ = _wgrad(
        merged, dy1, pl.BlockSpec((tk, d), lambda m, k: (k, 0)), pl.BlockSpec((tk, d), lambda m, k: (k, 0)),
        pl.BlockSpec((d, d), lambda m, k: (0, 0)), SDS((d, d), F32), (1, t // tk), "wgrad_out")

    early = [dw_out.reshape(N_CHIP, d // N_CHIP, d), dw_ffn_in, dw_ffn_out.reshape(N_CHIP, ff // N_CHIP, d)]
    early_names = ["w_out", "w_ffn_in", "w_ffn_out"]
    dmerged, from_sibling = _out_proj_bwd(dy1, w_out_f, tm, ride=_x_pair_send(early))
    chip32, chip16 = zip(*[_pair_add(pos, g, s, "pair_add_" + nm)
                           for g, s, nm in zip(early, from_sibling, early_names)])
    (dzm, dkv, db_main, db_kv, dcw, dsk), from_chips = _mixer_bwd(z, dmerged, attn, sinks, conv_w_full, d,
                                                                  ride=_x_chip_send(list(chip16)))
    fulls = [_chip_add(pos, p, r, "chip_add_" + nm) for p, r, nm in zip(chip32, from_chips, early_names)]
    dw_in_main, (g_w_out, g_w_ffn_in, g_w_ffn_out) = _wgrad(
        h1, dzm, pl.BlockSpec((tk, d), lambda n, k: (k, 0)), pl.BlockSpec((tk, d), lambda n, k: (k, n)),
        pl.BlockSpec((d, d), lambda n, k: (0, n)), SDS((d, 6 * d), F32), (6, t // tk), "wgrad_in_main",
        ride=_x_pair_exchange(fulls))
    dw_in_kv, _ = _wgrad(
        h1, dkv, pl.BlockSpec((tk, d), lambda n, k: (k, 0)), pl.BlockSpec((tk, kvw2), lambda n, k: (k, 0)),
        pl.BlockSpec((d, kvw2), lambda n, k: (0, 0)), SDS((d, kvw2), F32), (1, t // tk), "wgrad_in_kv")
    dw_in = _unpermute_cols(jnp.concatenate([dw_in_main, dw_in_kv], axis=1), d, kvw2)
    dw_in = dw_in.reshape(d, N_CHIP, zw // N_CHIP).transpose(1, 0, 2)

    (from_sibling,) = _exchange(_x_pair_send([dw_in]), "pair_send")
    chip32, chip16 = _pair_add(pos, dw_in, from_sibling, "pair_add_w_in")
    (grad_x, st_in), (from_chips,) = _in_proj_bwd(dzm, dkv, w_in_p, xs, dx1, g_mix, sc1, tm,
                                                  ride=_x_chip_send([chip16]))
    (g_w_in,) = _exchange(_x_pair_exchange([_chip_add(pos, chip32, from_chips, "chip_add_w_in")]), "pair_exchange")

    dmod = jnp.concatenate([st_in[0:1], st_in[1:2], st_ffn[3:4], st_ffn[0:1], st_ffn[1:2], st_loss[0:1]], axis=1)
    db_in = _unpermute_cols(jnp.concatenate([db_main[0:1], db_kv[0:1]], axis=1), d, kvw2)
    seg = [dmod, st_in[2:3], db_in, dsk[0:1], dcw[0:3].reshape(1, 3 * d), st_ffn[2:3], st_loss[1:2],
           st_loss[2:3, :LANES]]
    sizes = [s.shape[1] for s in seg]
    pack2 = _to_lanes(jnp.concatenate(seg, axis=1))
    packs = _all_gather_small(pack2, "gather_small_grads")
    tot = _pack_sum(packs).reshape(-1)
    offs = [sum(sizes[:k]) for k in range(len(sizes))]
    gb_ada, gg_mix, gb_in, gsinks, gcw, gg_ffn, gg_final, loss_v = [tot[o:o + s] for o, s in zip(offs, sizes)]
    loss = loss_v[0]
    gsinks = gsinks[:sinks.shape[1]]
    gcw_sh = lax.dynamic_slice(gcw.reshape(3, d), (0, j_me * cw_sh), (3, cw_sh))

    dmod_all = packs[:, :n_mod * d // LANES, :].reshape(N_DEV, n_mod * d)
    g_w_ada = _ada_wgrad(c_all, lax.dynamic_slice(dmod_all, (0, j_me * mod_sh), (N_DEV, mod_sh)))

    out_g, out_d, out_m, out_v = {}, {}, {}, {}
    big = {"w_ada": (w_ada, g_w_ada, m_w_ada, v_w_ada), "w_in": (w_in, g_w_in, m_w_in, v_w_in),
           "w_out": (w_out, g_w_out, m_w_out, v_w_out), "w_ffn_in": (w_ffn_in, g_w_ffn_in, m_w_ffn_in, v_w_ffn_in),
           "w_ffn_out": (w_ffn_out, g_w_ffn_out, m_w_ffn_out, v_w_ffn_out)}
    for nm, (w, g, m, v) in big.items():
        go, dl, nm_, nv_ = _adamw(w[0], g, m[0], v[0], "adamw_" + nm)
        out_g[nm], out_d[nm], out_m[nm], out_v[nm] = go[None], dl[None], nm_[None], nv_[None]
    small = {"b_ada": (b_ada, gb_ada, m_b_ada, v_b_ada), "g_mix": (g_mix, gg_mix, m_g_mix, v_g_mix),
             "b_in": (b_in, gb_in, m_b_in, v_b_in), "sinks": (sinks, gsinks, m_sinks, v_sinks),
             "conv_w": (conv_w, gcw_sh, m_conv_w, v_conv_w), "g_ffn": (g_ffn, gg_ffn, m_g_ffn, v_g_ffn),
             "g_final": (g_final, gg_final, m_g_final, v_g_final)}
    s_sizes = [w.size for w, _, _, _ in small.values()]
    s_rows = -(-sum(s_sizes) // LANES // 8) * 8

    def s_pack(k):
        return _to_lanes(jnp.concatenate([tup[k].reshape(-1) for tup in small.values()]), s_rows)

    s_out = _adamw(s_pack(0), s_pack(1), s_pack(2), s_pack(3), "adamw_small")
    s_off = 0
    for (nm, (w, g, _, _)), sz in zip(small.items(), s_sizes):
        out_g[nm] = g.reshape(w.shape)
        out_d[nm], out_m[nm], out_v[nm] = [o.reshape(-1)[s_off:s_off + sz].reshape(w.shape) for o in s_out[1:]]
        s_off += sz

    order = ["w_ada", "b_ada", "g_mix", "w_in", "b_in", "sinks", "conv_w", "w_out", "g_ffn", "w_ffn_in", "w_ffn_out",
             "g_final"]
    return (loss, grad_x[None], *[out_g[k] for k in order], *[out_d[k] for k in order],
            *[out_m[k] for k in order], *[out_v[k] for k in order])
```

```python
import functools

import jax
import jax.numpy as jnp
from jax import lax
from jax.experimental import pallas as pl
from jax.experimental.pallas import tpu as pltpu

F32 = jnp.float32
BF16 = jnp.bfloat16
EPS = 1e-6
HEAD_DIM = 64
GROUP = 8
BLOCK = 128
LANES = 128
SUBLANES_BF16 = 16
N_DEV = 8
N_CHIP = 4
VMEM_LIMIT = 56 * 1024 * 1024
MESH = pl.DeviceIdType.MESH

ADAM_LR = 0.001
ADAM_B1 = 0.9
ADAM_B2 = 0.999
ADAM_EPS = 1e-08
ADAM_WD = 0.01
ADAM_STEP = 10

SDS = jax.ShapeDtypeStruct
ANY = pl.BlockSpec(memory_space=pl.ANY)
VMEM_SPEC = pl.BlockSpec(memory_space=pltpu.VMEM)
SMEM_SPEC = pl.BlockSpec(memory_space=pltpu.SMEM)


def _params(*sem):
    return pltpu.CompilerParams(dimension_semantics=sem, vmem_limit_bytes=VMEM_LIMIT)


def _mesh_pos():
    return lax.axis_index("x"), lax.axis_index("y"), lax.axis_index("c")


def _row_tile(rows, cols, itemsize=4, budget=1 << 20, mult=8):
    best = None
    for t in range(mult, rows + 1, mult):
        if rows % t == 0 and t * cols * itemsize <= budget:
            best = t
    if best is None:
        best = rows
    return best


def _all_gather_small(v, name):
    rows, cols = v.shape

    def body(v_ref, out_ref, send_sems, recv_sems, local_sem):
        x, y, c = _mesh_pos()
        me = 4 * x + 2 * y + c
        mine = pltpu.make_async_copy(v_ref, out_ref.at[me], local_sem)
        mine.start()
        peers = []
        for k in range(1, N_DEV):
            px = 1 - x if k & 4 else x
            py = 1 - y if k & 2 else y
            pc = 1 - c if k & 1 else c
            peers.append((px, py, pc))

        def copy(k, block):
            return pltpu.make_async_remote_copy(
                src_ref=v_ref, dst_ref=out_ref.at[block], send_sem=send_sems.at[k], recv_sem=recv_sems.at[k],
                device_id=peers[k], device_id_type=MESH)

        sends = [copy(k, me) for k in range(N_DEV - 1)]
        for cp in sends:
            cp.start()
        for k, (px, py, pc) in enumerate(peers):
            copy(k, 4 * px + 2 * py + pc).wait_recv()
        for cp in sends:
            cp.wait_send()
        mine.wait()

    return pl.pallas_call(
        body, name=name,
        out_shape=SDS((N_DEV, rows, cols), v.dtype),
        in_specs=[VMEM_SPEC], out_specs=VMEM_SPEC,
        scratch_shapes=[pltpu.SemaphoreType.DMA((N_DEV - 1,)), pltpu.SemaphoreType.DMA((N_DEV - 1,)),
                        pltpu.SemaphoreType.DMA],
    )(v)


def _other_chips(x, y):
    return [(1 - x, y), (x, 1 - y), (1 - x, 1 - y)]


def _gather_weights(bufs):
    n_w = len(bufs)

    def body(*refs):
        outs = refs[n_w:2 * n_w]
        send_sems, recv_sems, fsend_sems, frecv_sems = refs[2 * n_w:]
        x, y, c = _mesh_pos()
        j_me = 2 * x + y
        chips = _other_chips(x, y)
        sibling = (x, y, 1 - c)

        def half_rows(w, which):
            half = outs[w].shape[1] // 2
            return pl.ds(pl.multiple_of(which * half, SUBLANES_BF16), half)

        def copy(w, p, block, rows, over_ici):
            sems = (send_sems, recv_sems) if over_ici else (fsend_sems, frecv_sems)
            return pltpu.make_async_remote_copy(
                src_ref=outs[w].at[block, rows], dst_ref=outs[w].at[block, rows],
                send_sem=sems[0].at[w * 3 + p], recv_sem=sems[1].at[w * 3 + p],
                device_id=(*chips[p], c) if over_ici else sibling, device_id_type=MESH)

        def block_of(p):
            return 2 * chips[p][0] + chips[p][1]

        sends = [copy(w, p, j_me, half_rows(w, c), True) for w in range(n_w) for p in range(3)]
        for cp in sends:
            cp.start()
        forwards = []
        for w in range(n_w):
            for p in range(3):
                copy(w, p, block_of(p), half_rows(w, c), True).wait_recv()
                fw = copy(w, p, block_of(p), half_rows(w, c), False)
                fw.start()
                forwards.append(fw)
        for w in range(n_w):
            for p in range(3):
                copy(w, p, block_of(p), half_rows(w, 1 - c), False).wait_recv()
        for cp in sends + forwards:
            cp.wait_send()

    return pl.pallas_call(
        body, name="gather_weights",
        out_shape=[SDS(b.shape, b.dtype) for b in bufs],
        in_specs=[ANY] * n_w, out_specs=[ANY] * n_w,
        input_output_aliases={w: w for w in range(n_w)},
        scratch_shapes=[pltpu.SemaphoreType.DMA((3 * n_w,)), pltpu.SemaphoreType.DMA((3 * n_w,)),
                        pltpu.SemaphoreType.DMA((3 * n_w,)), pltpu.SemaphoreType.DMA((3 * n_w,))],
    )(*bufs)


class _Exchange:
    def __init__(self, operands, out_shape, in_place, n_sems, copies):
        self.operands, self.out_shape, self.in_place, self.n_sems, self.copies = (
            list(operands), list(out_shape), in_place, n_sems, copies)

    def sems(self):
        return [pltpu.SemaphoreType.DMA((self.n_sems,)), pltpu.SemaphoreType.DMA((self.n_sems,))]


def _x_gather_ici(bufs):
    def copies(ins, outs, send_sems, recv_sems):
        x, y, c = _mesh_pos()
        chips = _other_chips(x, y)
        out = []
        for w in range(len(outs)):
            half = outs[w].shape[1] // 2
            rows = pl.ds(pl.multiple_of(c * half, SUBLANES_BF16), half)
            for p in range(3):
                out.append(pltpu.make_async_remote_copy(
                    src_ref=outs[w].at[2 * x + y, rows], dst_ref=outs[w].at[2 * x + y, rows],
                    send_sem=send_sems.at[w * 3 + p], recv_sem=recv_sems.at[w * 3 + p],
                    device_id=(*chips[p], c), device_id_type=MESH))
        return out

    return _Exchange(bufs, [SDS(b.shape, b.dtype) for b in bufs], True, 3 * len(bufs), copies)


def _x_gather_d2d(bufs):
    def copies(ins, outs, send_sems, recv_sems):
        x, y, c = _mesh_pos()
        chips = _other_chips(x, y)
        out = []
        for w in range(len(outs)):
            half = outs[w].shape[1] // 2
            rows = pl.ds(pl.multiple_of(c * half, SUBLANES_BF16), half)
            for p in range(3):
                block = 2 * chips[p][0] + chips[p][1]
                out.append(pltpu.make_async_remote_copy(
                    src_ref=outs[w].at[block, rows], dst_ref=outs[w].at[block, rows],
                    send_sem=send_sems.at[w * 3 + p], recv_sem=recv_sems.at[w * 3 + p],
                    device_id=(x, y, 1 - c), device_id_type=MESH))
        return out

    return _Exchange(bufs, [SDS(b.shape, b.dtype) for b in bufs], True, 3 * len(bufs), copies)


def _x_pair_send(grads):
    def copies(ins, outs, send_sems, recv_sems):
        x, y, c = _mesh_pos()
        out = []
        for w in range(len(ins)):
            half = ins[w].shape[1] // 2
            rows = pl.ds(pl.multiple_of((1 - c) * half, 8), half)
            out.append(pltpu.make_async_remote_copy(
                src_ref=ins[w].at[:, rows, :], dst_ref=outs[w], send_sem=send_sems.at[w], recv_sem=recv_sems.at[w],
                device_id=(x, y, 1 - c), device_id_type=MESH))
        return out

    return _Exchange(grads, [SDS((N_CHIP, g.shape[1] // 2, g.shape[2]), g.dtype) for g in grads], False,
                     len(grads), copies)


def _x_chip_send(partials):
    def copies(ins, outs, send_sems, recv_sems):
        x, y, c = _mesh_pos()
        chips = _other_chips(x, y)
        out = []
        for w in range(len(ins)):
            for p in range(3):
                out.append(pltpu.make_async_remote_copy(
                    src_ref=ins[w].at[2 * chips[p][0] + chips[p][1]], dst_ref=outs[w].at[p],
                    send_sem=send_sems.at[w * 3 + p], recv_sem=recv_sems.at[w * 3 + p],
                    device_id=(*chips[p], c), device_id_type=MESH))
        return out

    return _Exchange(partials, [SDS((3,) + p.shape[1:], p.dtype) for p in partials], False, 3 * len(partials),
                     copies)


def _x_pair_exchange(fulls):
    def copies(ins, outs, send_sems, recv_sems):
        x, y, c = _mesh_pos()
        out = []
        for w in range(len(outs)):
            half = outs[w].shape[0] // 2
            rows = pl.ds(pl.multiple_of(c * half, 8), half)
            out.append(pltpu.make_async_remote_copy(
                src_ref=outs[w].at[rows], dst_ref=outs[w].at[rows], send_sem=send_sems.at[w],
                recv_sem=recv_sems.at[w], device_id=(x, y, 1 - c), device_id_type=MESH))
        return out

    return _Exchange(fulls, [SDS(f.shape, f.dtype) for f in fulls], True, len(fulls), copies)


def _pallas(body, *, name, grid, in_specs, out_specs, out_shape, args, scratch=(), sem=None, ride=None):
    single = not isinstance(out_specs, (list, tuple))
    out_specs_l = [out_specs] if single else list(out_specs)
    out_shape_l = [out_shape] if single else list(out_shape)
    n_in, n_out, n_scr = len(in_specs), len(out_specs_l), len(scratch)
    if ride is None:
        res = pl.pallas_call(body, name=name, grid=grid, in_specs=list(in_specs), out_specs=out_specs,
                             out_shape=out_shape, scratch_shapes=list(scratch), compiler_params=_params(*sem))(*args)
        return res, None
    n_x, n_xo = len(ride.operands), len(ride.out_shape)

    def full_body(*refs):
        ins, x_ins = refs[:n_in], refs[n_in:n_in + n_x]
        outs = refs[n_in + n_x:n_in + n_x + n_out]
        x_outs = refs[n_in + n_x + n_out:n_in + n_x + n_out + n_xo]
        rest = refs[n_in + n_x + n_out + n_xo:]
        scr, (send_sems, recv_sems) = rest[:n_scr], rest[n_scr:]
        first = functools.reduce(jnp.logical_and, [pl.program_id(a) == 0 for a in range(len(grid))])
        last = functools.reduce(jnp.logical_and, [pl.program_id(a) == grid[a] - 1 for a in range(len(grid))])

        @pl.when(first)
        def _():
            for cp in ride.copies(x_ins, x_outs, send_sems, recv_sems):
                cp.start()

        body(*ins, *outs, *scr)

        @pl.when(last)
        def _():
            for cp in ride.copies(x_ins, x_outs, send_sems, recv_sems):
                cp.wait()

    res = pl.pallas_call(
        full_body, name=name, grid=grid, in_specs=list(in_specs) + [ANY] * n_x,
        out_specs=out_specs_l + [ANY] * n_xo, out_shape=out_shape_l + ride.out_shape,
        input_output_aliases={n_in + k: n_out + k for k in range(n_x)} if ride.in_place else {},
        scratch_shapes=list(scratch) + ride.sems(),
        compiler_params=_params(*(["arbitrary"] * len(grid))))(*args, *ride.operands)
    own = res[0] if single else list(res[:n_out])
    return own, list(res[n_out:])


def _exchange(ride, name):
    n_x, n_xo = len(ride.operands), len(ride.out_shape)

    def body(*refs):
        x_ins, x_outs = refs[:n_x], refs[n_x:n_x + n_xo]
        send_sems, recv_sems = refs[n_x + n_xo:]
        copies = ride.copies(x_ins, x_outs, send_sems, recv_sems)
        for cp in copies:
            cp.start()
        for cp in copies:
            cp.wait()

    return pl.pallas_call(
        body, name=name, in_specs=[ANY] * n_x, out_specs=[ANY] * n_xo, out_shape=ride.out_shape,
        input_output_aliases={k: k for k in range(n_x)} if ride.in_place else {},
        scratch_shapes=ride.sems())(*ride.operands)


def _cast_into_block(pos, w, name):
    rows, cols = w.shape
    tr = _row_tile(rows, cols, mult=SUBLANES_BF16)

    def body(pos_ref, w_ref, o_ref):
        del pos_ref
        o_ref[...] = w_ref[...].astype(BF16)

    return pl.pallas_call(
        body, name=name,
        grid_spec=pltpu.PrefetchScalarGridSpec(
            num_scalar_prefetch=1, grid=(rows // tr,),
            in_specs=[pl.BlockSpec((tr, cols), lambda i, pos_ref: (i, 0))],
            out_specs=pl.BlockSpec((None, tr, cols), lambda i, pos_ref: (pos_ref[1], i, 0))),
        out_shape=SDS((N_CHIP, rows, cols), BF16), compiler_params=_params("parallel"))(pos, w)


def _pair_add(pos, grad, from_sibling, name):
    _, rows, cols = grad.shape
    half = rows // 2
    tr = _row_tile(half, cols, mult=SUBLANES_BF16)
    nblk = half // tr

    def body(pos_ref, g_ref, s_ref, o32_ref, o16_ref):
        del pos_ref
        s = g_ref[...] + s_ref[...]
        o32_ref[...] = s
        o16_ref[...] = s.astype(BF16)

    spec = pl.BlockSpec((None, tr, cols), lambda j, i, pos_ref: (j, i, 0))
    return pl.pallas_call(
        body, name=name,
        grid_spec=pltpu.PrefetchScalarGridSpec(
            num_scalar_prefetch=1, grid=(N_CHIP, nblk),
            in_specs=[pl.BlockSpec((None, tr, cols), lambda j, i, pos_ref: (j, pos_ref[0] * nblk + i, 0)), spec],
            out_specs=[spec, spec]),
        out_shape=[SDS((N_CHIP, half, cols), F32), SDS((N_CHIP, half, cols), BF16)],
        compiler_params=_params("parallel", "parallel"),
    )(pos, grad, from_sibling)


def _chip_add(pos, partial32, from_chips, name):
    _, half, cols = partial32.shape
    tr = _row_tile(half, cols, mult=SUBLANES_BF16)

    def body(pos_ref, p_ref, r_ref, o_ref):
        del pos_ref
        acc = p_ref[...]
        for p in range(3):
            acc = acc + r_ref[p].astype(F32)
        o_ref[...] = acc

    return pl.pallas_call(
        body, name=name,
        grid_spec=pltpu.PrefetchScalarGridSpec(
            num_scalar_prefetch=1, grid=(half // tr,),
            in_specs=[pl.BlockSpec((None, tr, cols), lambda i, pos_ref: (pos_ref[1], i, 0)),
                      pl.BlockSpec((3, tr, cols), lambda i, pos_ref: (0, i, 0))],
            out_specs=pl.BlockSpec((tr, cols), lambda i, pos_ref: (pos_ref[0] * (half // tr) + i, 0))),
        out_shape=SDS((2 * half, cols), F32),
        compiler_params=_params("parallel"),
    )(pos, partial32, from_chips)


def _adamw(w, g, m, v, name):
    rows, cols = w.shape
    tr = _row_tile(rows, cols, budget=1 << 19)

    def body(w_ref, g_ref, m_ref, v_ref, go_ref, d_ref, nm_ref, nv_ref):
        gg = g_ref[...]
        go_ref[...] = gg
        nm = ADAM_B1 * m_ref[...] + (1.0 - ADAM_B1) * gg
        nv = ADAM_B2 * v_ref[...] + (1.0 - ADAM_B2) * (gg * gg)
        m_hat = nm / (1.0 - ADAM_B1 ** ADAM_STEP)
        v_hat = nv / (1.0 - ADAM_B2 ** ADAM_STEP)
        d_ref[...] = -ADAM_LR * (m_hat / (jnp.sqrt(v_hat) + ADAM_EPS) + ADAM_WD * w_ref[...])
        nm_ref[...] = nm
        nv_ref[...] = nv

    spec = pl.BlockSpec((tr, cols), lambda i: (i, 0))
    return pl.pallas_call(body, name=name, grid=(rows // tr,), in_specs=[spec] * 4, out_specs=[spec] * 4,
                          out_shape=[SDS((rows, cols), F32)] * 4, compiler_params=_params("parallel"))(w, g, m, v)


def _pack_sum(gathered):
    _, rows, cols = gathered.shape

    def body(g_ref, o_ref):
        acc = g_ref[0]
        for d in range(1, N_DEV):
            acc = acc + g_ref[d]
        o_ref[...] = acc

    return pl.pallas_call(body, name="pack_sum", in_specs=[VMEM_SPEC], out_specs=VMEM_SPEC,
                          out_shape=SDS((rows, cols), F32))(gathered)


def _ada_fwd(c_all, w_sh, b_sh):
    d, n = w_sh.shape
    tn = 512

    def body(c_ref, w_ref, b_ref, o_ref):
        cc = c_ref[...]
        s = (cc * jax.nn.sigmoid(cc)).astype(BF16)
        o_ref[...] = jnp.dot(s, w_ref[...].astype(BF16), preferred_element_type=F32) + b_ref[...]

    return pl.pallas_call(
        body, name="ada_fwd", grid=(n // tn,),
        in_specs=[pl.BlockSpec((N_DEV, d), lambda j: (0, 0)), pl.BlockSpec((d, tn), lambda j: (0, j)),
                  pl.BlockSpec((1, tn), lambda j: (0, j))],
        out_specs=pl.BlockSpec((N_DEV, tn), lambda j: (0, j)),
        out_shape=SDS((N_DEV, n), F32), compiler_params=_params("parallel"))(c_all, w_sh, b_sh)


def _ada_wgrad(c_all, dmod_sh):
    d = c_all.shape[1]
    n = dmod_sh.shape[1]
    tn = 512

    def body(c_ref, g_ref, o_ref):
        cc = c_ref[...]
        s = cc * jax.nn.sigmoid(cc)
        o_ref[...] = lax.dot_general(s, g_ref[...], (((0,), (0,)), ((), ())), preferred_element_type=F32,
                                     precision=lax.Precision.HIGHEST)

    return pl.pallas_call(
        body, name="ada_wgrad", grid=(n // tn,),
        in_specs=[pl.BlockSpec((N_DEV, d), lambda j: (0, 0)), pl.BlockSpec((N_DEV, tn), lambda j: (0, j))],
        out_specs=pl.BlockSpec((d, tn), lambda j: (0, j)),
        out_shape=SDS((d, n), F32), compiler_params=_params("parallel"))(c_all, dmod_sh)


def _rms(xf):
    return lax.rsqrt(jnp.mean(xf * xf, axis=-1, keepdims=True) + EPS)


def _prenorm(x, g, sc, sh, tm):
    t, d = x.shape

    def body(x_ref, g_ref, sc_ref, sh_ref, h_ref):
        xf = x_ref[...]
        h_ref[...] = ((xf * _rms(xf) * g_ref[...]) * (1.0 + sc_ref[...]) + sh_ref[...]).astype(BF16)

    row = pl.BlockSpec((tm, d), lambda i: (i, 0))
    vec = pl.BlockSpec((1, d), lambda i: (0, 0))
    return pl.pallas_call(body, name="prenorm", grid=(t // tm,), in_specs=[row, vec, vec, vec], out_specs=row,
                          out_shape=SDS((t, d), BF16), compiler_params=_params("parallel"))(x, g, sc, sh)


def _in_proj(h, wt, b, tm, tn, ride=None):
    t, d = h.shape
    n = wt.shape[0]

    def body(h_ref, w_ref, b_ref, z_ref):
        acc = lax.dot_general(h_ref[...], w_ref[...], (((1,), (1,)), ((), ())), preferred_element_type=F32)
        z_ref[...] = (acc + b_ref[...]).astype(BF16)

    return _pallas(
        body, name="in_proj", grid=(n // tn, t // tm),
        in_specs=[pl.BlockSpec((tm, d), lambda j, i: (i, 0)), pl.BlockSpec((tn, d), lambda j, i: (j, 0)),
                  pl.BlockSpec((1, tn), lambda j, i: (0, j))],
        out_specs=pl.BlockSpec((tm, tn), lambda j, i: (i, j)),
        out_shape=SDS((t, n), BF16), args=(h, wt, b), sem=("parallel", "parallel"), ride=ride)


def _segments(d, kvw2):
    o = d + kvw2
    names = ("cb", "cc", "cx", "ga", "gc")
    seg = {nm: slice(o + k * d, o + (k + 1) * d) for k, nm in enumerate(names)}
    seg["q"], seg["kv"] = slice(0, d), slice(d, o)
    return seg


def _attn_masks():
    rows = 4 * BLOCK
    r = lax.broadcasted_iota(jnp.int32, (rows, 2 * BLOCK), 0) & (BLOCK - 1)
    col = lax.broadcasted_iota(jnp.int32, (rows, 2 * BLOCK), 1)
    return (col > r) & (col <= r + BLOCK), col


def _kv_variants(kv, n_kv_w):
    assert n_kv_w == LANES
    kb, vb = kv[:, :LANES] * (HEAD_DIM ** -0.5), kv[:, LANES:]
    kr, vr = pltpu.roll(kb, HEAD_DIM, 1), pltpu.roll(vb, HEAD_DIM, 1)
    lane = lax.broadcasted_iota(jnp.int32, kb.shape, 1)
    lo = lane < HEAD_DIM
    zero = jnp.zeros_like(kb)
    k_eff = [[None, None], [None, None]]
    v_eff = [[None, None], [None, None]]
    for h in range(2):
        for e in range(2):
            ksrc, vsrc = (kb, vb) if e == h else (kr, vr)
            keep = lo if e == 0 else jnp.logical_not(lo)
            k_eff[h][e] = jnp.where(keep, ksrc, zero)
            v_eff[h][e] = jnp.where(keep, vsrc, zero)
    return k_eff, v_eff


def _sink_column(sinks_ref, h, e):
    rowblk = lax.broadcasted_iota(jnp.int32, (4 * BLOCK, 1), 0) // BLOCK
    col = jnp.zeros((4 * BLOCK, 1), F32)
    for j in range(4):
        col = jnp.where(rowblk == j, sinks_ref[0, GROUP * h + 2 * j + e], col)
    return col


def _softmax_sink(s, valid, sink):
    s = jnp.where(valid, s, -jnp.inf)
    m = jnp.maximum(jnp.max(s, axis=-1, keepdims=True), sink)
    p = jnp.exp(s - m)
    psink = jnp.exp(sink - m)
    den = jnp.sum(p, axis=-1, keepdims=True) + psink
    inv = 1.0 / den
    return p * inv, psink * inv


def _shift_down(a, s, prev):
    rows = a.shape[0]
    out = pltpu.roll(a, s, 0)
    row = lax.broadcasted_iota(jnp.int32, a.shape, 0)
    for t in range(s):
        out = jnp.where(row == t, prev[SUBLANES_BF16 - s + t:SUBLANES_BF16 - s + t + 1, :], out)
    del rows
    return out


def _shift_up(a, s, nxt):
    rows = a.shape[0]
    out = pltpu.roll(a, rows - s, 0)
    row = lax.broadcasted_iota(jnp.int32, a.shape, 0)
    for t in range(s):
        out = jnp.where(row == rows - s + t, nxt[t:t + 1, :], out)
    return out


def _stack_pairs(ref, h):
    return jnp.concatenate([ref[:, (4 * h + j) * LANES:(4 * h + j + 1) * LANES] for j in range(4)], axis=0)


def _mixer_fwd(z, sinks, conv_w, d):
    t, zw = z.shape
    kvw2 = zw - 6 * d
    nb = t // BLOCK
    halo = BLOCK // SUBLANES_BF16
    seg = _segments(d, kvw2)

    def body(z_ref, kvp_ref, prev_ref, sinks_ref, cw_ref, attn_ref, merged_ref):
        n = pl.program_id(0)
        kv = jnp.concatenate([kvp_ref[...], z_ref[:, seg["kv"]]], axis=0)
        k_eff, v_eff = _kv_variants(kv, kvw2 // 2)
        band, col = _attn_masks()
        valid = band & ((n > 0) | (col >= BLOCK))
        for h in range(2):
            q4 = _stack_pairs(z_ref, h)
            o4 = jnp.zeros((4 * BLOCK, LANES), F32)
            for e in range(2):
                s = lax.dot_general(q4, k_eff[h][e], (((1,), (1,)), ((), ())), preferred_element_type=F32)
                p, _ = _softmax_sink(s, valid, _sink_column(sinks_ref, h, e))
                o4 = o4 + jnp.dot(p.astype(BF16), v_eff[h][e], preferred_element_type=F32)
            for j in range(4):
                attn_ref[:, (4 * h + j) * LANES:(4 * h + j + 1) * LANES] = o4[j * BLOCK:(j + 1) * BLOCK].astype(BF16)
        cb = z_ref[:, seg["cb"]].astype(F32)
        p_in = z_ref[:, seg["cc"]].astype(F32) * z_ref[:, seg["cx"]].astype(F32)
        prev = jnp.where(n > 0, prev_ref[:, seg["cc"]].astype(F32) * prev_ref[:, seg["cx"]].astype(F32), 0.0)
        cconv = (cw_ref[0:1, :] * _shift_down(p_in, 2, prev) + cw_ref[1:2, :] * _shift_down(p_in, 1, prev)
                 + cw_ref[2:3, :] * p_in)
        sa = jax.nn.sigmoid(z_ref[:, seg["ga"]].astype(F32))
        sg = jax.nn.sigmoid(z_ref[:, seg["gc"]].astype(F32))
        merged_ref[...] = (sa * attn_ref[...].astype(F32) + sg * (cb * cconv)).astype(BF16)

    blk = pl.BlockSpec((BLOCK, d), lambda n: (n, 0))
    return pl.pallas_call(
        body, name="mixer_fwd", grid=(nb,),
        in_specs=[pl.BlockSpec((BLOCK, zw), lambda n: (n, 0)),
                  pl.BlockSpec((BLOCK, kvw2), lambda n: (jnp.maximum(n - 1, 0), d // kvw2)),
                  pl.BlockSpec((SUBLANES_BF16, zw), lambda n: (jnp.maximum(n * halo - 1, 0), 0)),
                  SMEM_SPEC, pl.BlockSpec((3, d), lambda n: (0, 0))],
        out_specs=[blk, blk],
        out_shape=[SDS((t, d), BF16), SDS((t, d), BF16)],
        compiler_params=_params("parallel"))(z, z, z, sinks, conv_w)


def _out_proj_fwd(merged, w_out, x, ga1, g_ffn, sc2, sh2, tm):
    t, d = x.shape

    def body(m_ref, w_ref, x_ref, ga_ref, g_ref, sc_ref, sh_ref, y_ref, x1_ref, h_ref):
        y = jnp.dot(m_ref[...], w_ref[...], preferred_element_type=F32)
        x1 = x_ref[...] + ga_ref[...] * y
        y_ref[...] = y.astype(BF16)
        x1_ref[...] = x1
        h_ref[...] = ((x1 * _rms(x1) * g_ref[...]) * (1.0 + sc_ref[...]) + sh_ref[...]).astype(BF16)

    row = pl.BlockSpec((tm, d), lambda i: (i, 0))
    vec = pl.BlockSpec((1, d), lambda i: (0, 0))
    return pl.pallas_call(
        body, name="out_proj_fwd", grid=(t // tm,),
        in_specs=[row, pl.BlockSpec((d, d), lambda i: (0, 0)), row, vec, vec, vec, vec],
        out_specs=[row, row, row],
        out_shape=[SDS((t, d), BF16), SDS((t, d), F32), SDS((t, d), BF16)],
        compiler_params=_params("parallel"))(merged, w_out, x, ga1, g_ffn, sc2, sh2)


def _ffn_in_fwd(h2, w, ff, tm, tn):
    t, d = h2.shape
    nj = ff // tn
    assert w.shape == (2 * nj, d, tn)

    def body(h_ref, wg_ref, wu_ref, gu_ref, act_ref):
        hh = h_ref[...]
        g = jnp.dot(hh, wg_ref[...], preferred_element_type=F32)
        u = jnp.dot(hh, wu_ref[...], preferred_element_type=F32)
        gu_ref[0] = g.astype(BF16)
        gu_ref[1] = u.astype(BF16)
        act_ref[...] = ((g * jax.nn.sigmoid(g)) * u).astype(BF16)

    return pl.pallas_call(
        body, name="ffn_in_fwd", grid=(nj, t // tm),
        in_specs=[pl.BlockSpec((tm, d), lambda j, i: (i, 0)), pl.BlockSpec((None, d, tn), lambda j, i: (j, 0, 0)),
                  pl.BlockSpec((None, d, tn), lambda j, i: (j + nj, 0, 0))],
        out_specs=[pl.BlockSpec((2, tm, tn), lambda j, i: (0, i, j)), pl.BlockSpec((tm, tn), lambda j, i: (i, j))],
        out_shape=[SDS((2, t, ff), BF16), SDS((t, ff), BF16)],
        compiler_params=_params("parallel", "parallel"))(h2, w, w)


def _ffn_out_loss(act, w, x1, target, ga2, g_final, tm):
    t, d = x1.shape
    ff = act.shape[1]

    def body(a_ref, w_ref, x1_ref, tg_ref, ga_ref, gf_ref, dx2_ref, dy2_ref, st_ref):
        @pl.when(pl.program_id(0) == 0)
        def _():
            st_ref[...] = jnp.zeros_like(st_ref)

        y2 = jnp.dot(a_ref[...], w_ref[...], preferred_element_type=F32)
        x2 = x1_ref[...] + ga_ref[...] * y2
        r = _rms(x2)
        yn = x2 * r
        err = yn * gf_ref[...] - tg_ref[...]
        loss = 0.5 * jnp.sum(jnp.mean(err * err, axis=-1, keepdims=True), axis=0, keepdims=True)
        dy = err * (1.0 / d)
        u = dy * gf_ref[...]
        dx2 = r * (u - yn * jnp.mean(u * yn, axis=-1, keepdims=True))
        dx2_ref[...] = dx2
        dy2_ref[...] = (ga_ref[...] * dx2).astype(BF16)
        st_ref[0:1, :] += jnp.sum(dx2 * y2, axis=0, keepdims=True)
        st_ref[1:2, :] += jnp.sum(dy * yn, axis=0, keepdims=True)
        st_ref[2:3, :] += jnp.broadcast_to(loss, (1, d))

    row = pl.BlockSpec((tm, d), lambda i: (i, 0))
    vec = pl.BlockSpec((1, d), lambda i: (0, 0))
    return pl.pallas_call(
        body, name="ffn_out_loss", grid=(t // tm,),
        in_specs=[pl.BlockSpec((tm, ff), lambda i: (i, 0)), pl.BlockSpec((ff, d), lambda i: (0, 0)), row, row,
                  vec, vec],
        out_specs=[row, row, pl.BlockSpec((8, d), lambda i: (0, 0))],
        out_shape=[SDS((t, d), F32), SDS((t, d), BF16), SDS((8, d), F32)],
        compiler_params=_params("arbitrary"))(act, w, x1, target, ga2, g_final)


def _ffn_out_bwd(dy2, w, gu, tm, tn):
    t, d = dy2.shape
    ff = w.shape[0]

    def body(dy_ref, w_ref, gu_ref, o_ref):
        dact = lax.dot_general(dy_ref[...], w_ref[...], (((1,), (1,)), ((), ())), preferred_element_type=F32)
        g = gu_ref[0].astype(F32)
        u = gu_ref[1].astype(F32)
        sg = jax.nn.sigmoid(g)
        o_ref[0] = (dact * u * (sg * (1.0 + g * (1.0 - sg)))).astype(BF16)
        o_ref[1] = (dact * (g * sg)).astype(BF16)

    gu_spec = pl.BlockSpec((2, tm, tn), lambda j, i: (0, i, j))
    return pl.pallas_call(
        body, name="ffn_out_bwd", grid=(ff // tn, t // tm),
        in_specs=[pl.BlockSpec((tm, d), lambda j, i: (i, 0)), pl.BlockSpec((tn, d), lambda j, i: (j, 0)), gu_spec],
        out_specs=gu_spec, out_shape=SDS((2, t, ff), BF16),
        compiler_params=_params("parallel", "parallel"))(dy2, w, gu)


def _wgrad(a, b, a_spec, b_spec, out_spec, out_shape, grid, name, ride=None):
    def body(a_ref, b_ref, o_ref):
        @pl.when(pl.program_id(len(grid) - 1) == 0)
        def _():
            o_ref[...] = jnp.zeros_like(o_ref)

        o_ref[...] += lax.dot_general(a_ref[...], b_ref[...], (((0,), (0,)), ((), ())), preferred_element_type=F32)

    return _pallas(
        body, name=name, grid=grid, in_specs=[a_spec, b_spec], out_specs=out_spec, out_shape=out_shape, args=(a, b),
        sem=["parallel"] * (len(grid) - 1) + ["arbitrary"], ride=ride)


def _ffn_in_bwd(dgu, w, x1, dx2, y1, g_ffn, sc2, ga1, tm, tk):
    t, d = x1.shape
    ff = dgu.shape[2]
    n_sh, _, sw = w.shape
    per = ff // sw
    del tk
    nt = (((1,), (1,)), ((), ()))

    def body(a_ref, w_ref, x1_ref, dx2_ref, y1_ref, g_ref, sc_ref, ga_ref, dx1_ref, dy1_ref, st_ref):
        @pl.when(pl.program_id(0) == 0)
        def _():
            st_ref[...] = jnp.zeros_like(st_ref)

        dh = None
        for j in range(n_sh):
            part = lax.dot_general(a_ref[j // per, :, (j % per) * sw:(j % per + 1) * sw], w_ref[j], nt,
                                   preferred_element_type=F32)
            dh = part if dh is None else dh + part
        x1 = x1_ref[...]
        r = _rms(x1)
        xn = x1 * r
        g = g_ref[...]
        dn = dh * (1.0 + sc_ref[...])
        u = dn * g
        dx1 = dx2_ref[...] + r * (u - xn * jnp.mean(u * xn, axis=-1, keepdims=True))
        dx1_ref[...] = dx1
        dy1_ref[...] = (ga_ref[...] * dx1).astype(BF16)
        st_ref[0:1, :] += jnp.sum(dh, axis=0, keepdims=True)
        st_ref[1:2, :] += jnp.sum(dh * (xn * g), axis=0, keepdims=True)
        st_ref[2:3, :] += jnp.sum(dn * xn, axis=0, keepdims=True)
        st_ref[3:4, :] += jnp.sum(dx1 * y1_ref[...].astype(F32), axis=0, keepdims=True)

    row = pl.BlockSpec((tm, d), lambda i: (i, 0))
    vec = pl.BlockSpec((1, d), lambda i: (0, 0))
    return pl.pallas_call(
        body, name="ffn_in_bwd", grid=(t // tm,),
        in_specs=[pl.BlockSpec((2, tm, ff), lambda i: (0, i, 0)),
                  pl.BlockSpec((n_sh, d, sw), lambda i: (0, 0, 0), pipeline_mode=pl.Buffered(1)),
                  row, row, row, vec, vec, vec],
        out_specs=[row, row, pl.BlockSpec((8, d), lambda i: (0, 0))],
        out_shape=[SDS((t, d), F32), SDS((t, d), BF16), SDS((8, d), F32)],
        compiler_params=_params("arbitrary"))(dgu, w, x1, dx2, y1, g_ffn, sc2, ga1)


def _out_proj_bwd(dy1, w_out, tm, ride=None):
    t, d = dy1.shape

    def body(dy_ref, w_ref, o_ref):
        o_ref[...] = lax.dot_general(dy_ref[...], w_ref[...], (((1,), (1,)), ((), ())),
                                     preferred_element_type=F32).astype(BF16)

    row = pl.BlockSpec((tm, d), lambda i: (i, 0))
    return _pallas(body, name="out_proj_bwd", grid=(t // tm,),
                   in_specs=[row, pl.BlockSpec((d, d), lambda i: (0, 0))], out_specs=row,
                   out_shape=SDS((t, d), BF16), args=(dy1, w_out), sem=("parallel",), ride=ride)


def _mixer_bwd(z, dmerged, attn, sinks, conv_w, d, ride=None):
    t, zw = z.shape
    kvw2 = zw - 6 * d
    nb = t // BLOCK
    halo = BLOCK // SUBLANES_BF16
    last_halo = t // SUBLANES_BF16 - 1
    scale = HEAD_DIM ** -0.5
    seg = _segments(d, kvw2)

    def body(z_ref, kvp_ref, prev_ref, next_ref, dm_ref, dmn_ref, attn_ref, sinks_ref, cw_ref,
             dz_ref, dkv_ref, db_ref, dbkv_ref, dcw_ref, dsk_ref, carry_ref):
        n = pl.program_id(0)

        @pl.when(n == 0)
        def _():
            carry_ref[...] = jnp.zeros_like(carry_ref)
            db_ref[...] = jnp.zeros_like(db_ref)
            dbkv_ref[...] = jnp.zeros_like(dbkv_ref)
            dcw_ref[...] = jnp.zeros_like(dcw_ref)
            dsk_ref[...] = jnp.zeros_like(dsk_ref)

        @pl.when(n < nb)
        def _():
            dm = dm_ref[...].astype(F32)
            sa = jax.nn.sigmoid(z_ref[:, seg["ga"]].astype(F32))
            dga = dm * attn_ref[...].astype(F32) * sa * (1.0 - sa)
            dz_ref[:, seg["ga"]] = dga.astype(BF16)
            db_ref[0:1, seg["ga"]] += jnp.sum(dga, axis=0, keepdims=True)
            dattn = (dm * sa).astype(BF16)

            kv = jnp.concatenate([kvp_ref[...], z_ref[:, seg["kv"]]], axis=0)
            k_eff, v_eff = _kv_variants(kv, kvw2 // 2)
            band, col = _attn_masks()
            valid = band & ((n > 0) | (col >= BLOCK))
            lane_lo = lax.broadcasted_iota(jnp.int32, (2 * BLOCK, LANES), 1) < HEAD_DIM
            sink_lane = lax.broadcasted_iota(jnp.int32, (1, LANES), 1)
            rowblk = lax.broadcasted_iota(jnp.int32, (4 * BLOCK, 1), 0) // BLOCK
            dk_acc = [jnp.zeros((2 * BLOCK, LANES), F32), jnp.zeros((2 * BLOCK, LANES), F32)]
            dv_acc = [jnp.zeros((2 * BLOCK, LANES), F32), jnp.zeros((2 * BLOCK, LANES), F32)]
            dsink = jnp.zeros((1, LANES), F32)
            for h in range(2):
                q4 = _stack_pairs(z_ref, h)
                do4 = jnp.concatenate([dattn[:, (4 * h + j) * LANES:(4 * h + j + 1) * LANES] for j in range(4)],
                                      axis=0)
                dq4 = jnp.zeros((4 * BLOCK, LANES), F32)
                for e in range(2):
                    s = lax.dot_general(q4, k_eff[h][e], (((1,), (1,)), ((), ())), preferred_element_type=F32)
                    p, psink = _softmax_sink(s, valid, _sink_column(sinks_ref, h, e))
                    dp = lax.dot_general(do4, v_eff[h][e], (((1,), (1,)), ((), ())), preferred_element_type=F32)
                    delta = jnp.sum(p * dp, axis=-1, keepdims=True)
                    ds = (p * (dp - delta)).astype(BF16)
                    dq4 = dq4 + jnp.dot(ds, k_eff[h][e], preferred_element_type=F32)
                    dk = lax.dot_general(ds, q4, (((0,), (0,)), ((), ())), preferred_element_type=F32)
                    dv = lax.dot_general(p.astype(BF16), do4, (((0,), (0,)), ((), ())), preferred_element_type=F32)
                    keep = lane_lo if e == 0 else jnp.logical_not(lane_lo)
                    slot = 0 if e == h else 1
                    dk_acc[slot] = dk_acc[slot] + jnp.where(keep, dk, 0.0)
                    dv_acc[slot] = dv_acc[slot] + jnp.where(keep, dv, 0.0)
                    dsk = -(psink * delta)
                    for j in range(4):
                        tot = jnp.sum(jnp.where(rowblk == j, dsk, 0.0), axis=0, keepdims=True)
                        dsink = dsink + jnp.where(sink_lane == GROUP * h + 2 * j + e, tot, 0.0)
                for j in range(4):
                    cols = slice((4 * h + j) * LANES, (4 * h + j + 1) * LANES)
                    dqj = dq4[j * BLOCK:(j + 1) * BLOCK]
                    dz_ref[:, cols] = dqj.astype(BF16)
                    db_ref[0:1, cols] += jnp.sum(dqj, axis=0, keepdims=True)
            dsk_ref[0:1, :] += dsink
            dkv_new = jnp.concatenate([(dk_acc[0] + pltpu.roll(dk_acc[1], HEAD_DIM, 1)) * scale,
                                       dv_acc[0] + pltpu.roll(dv_acc[1], HEAD_DIM, 1)], axis=1)
            done = carry_ref[...] + dkv_new[:BLOCK]
            dkv_ref[...] = done.astype(BF16)
            dbkv_ref[0:1, :] += jnp.sum(done, axis=0, keepdims=True)
            carry_ref[...] = dkv_new[BLOCK:]

            cb = z_ref[:, seg["cb"]].astype(F32)
            cc = z_ref[:, seg["cc"]].astype(F32)
            cx = z_ref[:, seg["cx"]].astype(F32)
            sg = jax.nn.sigmoid(z_ref[:, seg["gc"]].astype(F32))
            p_in = cc * cx
            prev = jnp.where(n > 0, prev_ref[:, seg["cc"]].astype(F32) * prev_ref[:, seg["cx"]].astype(F32), 0.0)
            p_m1 = _shift_down(p_in, 1, prev)
            p_m2 = _shift_down(p_in, 2, prev)
            w0, w1, w2 = cw_ref[0:1, :], cw_ref[1:2, :], cw_ref[2:3, :]
            cconv = w0 * p_m2 + w1 * p_m1 + w2 * p_in
            dconv = dm * sg
            dgc = dm * (cb * cconv) * sg * (1.0 - sg)
            dcb = dconv * cconv
            dcc_t = dconv * cb
            nxt = jnp.where(n < nb - 1, dmn_ref[...].astype(F32) * jax.nn.sigmoid(next_ref[:, seg["gc"]].astype(F32))
                            * next_ref[:, seg["cb"]].astype(F32), 0.0)
            dpin = w2 * dcc_t + w1 * _shift_up(dcc_t, 1, nxt) + w0 * _shift_up(dcc_t, 2, nxt)
            for nm, val in (("cb", dcb), ("cc", dpin * cx), ("cx", dpin * cc), ("gc", dgc)):
                dz_ref[:, seg[nm]] = val.astype(BF16)
                db_ref[0:1, seg[nm]] += jnp.sum(val, axis=0, keepdims=True)
            dcw_ref[0:1, :] += jnp.sum(dcc_t * p_m2, axis=0, keepdims=True)
            dcw_ref[1:2, :] += jnp.sum(dcc_t * p_m1, axis=0, keepdims=True)
            dcw_ref[2:3, :] += jnp.sum(dcc_t * p_in, axis=0, keepdims=True)

        @pl.when(n == nb)
        def _():
            done = carry_ref[...]
            dkv_ref[...] = done.astype(BF16)
            dbkv_ref[0:1, :] += jnp.sum(done, axis=0, keepdims=True)

    def cur(n):
        return jnp.minimum(n, nb - 1)

    blk = pl.BlockSpec((BLOCK, d), lambda n: (cur(n), 0))
    return _pallas(
        body, name="mixer_bwd", grid=(nb + 1,), ride=ride, sem=("arbitrary",),
        args=(z, z, z, z, dmerged, dmerged, attn, sinks, conv_w),
        in_specs=[pl.BlockSpec((BLOCK, zw), lambda n: (cur(n), 0)),
                  pl.BlockSpec((BLOCK, kvw2), lambda n: (jnp.maximum(cur(n) - 1, 0), d // kvw2)),
                  pl.BlockSpec((SUBLANES_BF16, zw), lambda n: (jnp.maximum(cur(n) * halo - 1, 0), 0)),
                  pl.BlockSpec((SUBLANES_BF16, zw), lambda n: (jnp.minimum((cur(n) + 1) * halo, last_halo), 0)),
                  blk,
                  pl.BlockSpec((SUBLANES_BF16, d), lambda n: (jnp.minimum((cur(n) + 1) * halo, last_halo), 0)),
                  blk, SMEM_SPEC, pl.BlockSpec((3, d), lambda n: (0, 0))],
        out_specs=[pl.BlockSpec((BLOCK, zw), lambda n: (cur(n), 0)),
                   pl.BlockSpec((BLOCK, kvw2), lambda n: (jnp.maximum(n - 1, 0), 0)),
                   pl.BlockSpec((8, zw), lambda n: (0, 0)), pl.BlockSpec((8, kvw2), lambda n: (0, 0)),
                   pl.BlockSpec((8, d), lambda n: (0, 0)), pl.BlockSpec((8, LANES), lambda n: (0, 0))],
        out_shape=[SDS((t, zw), BF16), SDS((t, kvw2), BF16), SDS((8, zw), F32), SDS((8, kvw2), F32),
                   SDS((8, d), F32), SDS((8, LANES), F32)],
        scratch=[pltpu.VMEM((BLOCK, kvw2), F32)])


def _wgrad_in(dz, dkv, h1, tk, ride=None):
    t, zw = dz.shape
    d = h1.shape[1]
    kvw2 = dkv.shape[1]
    blk = d + kvw2
    assert zw % blk == 0
    tn = (((0,), (0,)), ((), ()))

    def body(a_ref, akv_ref, h_ref, o_ref):
        n, k = pl.program_id(0), pl.program_id(1)

        @pl.when(k == 0)
        def _():
            o_ref[...] = jnp.zeros_like(o_ref)

        @pl.when(n == 0)
        def _():
            o_ref[:d, :] += lax.dot_general(a_ref[:, :d], h_ref[...], tn, preferred_element_type=F32)
            o_ref[d:, :] += lax.dot_general(akv_ref[...], h_ref[...], tn, preferred_element_type=F32)

        @pl.when(n > 0)
        def _():
            o_ref[...] += lax.dot_general(a_ref[...], h_ref[...], tn, preferred_element_type=F32)

    return _pallas(
        body, name="wgrad_in", grid=(zw // blk, t // tk),
        in_specs=[pl.BlockSpec((tk, blk), lambda n, k: (k, n)), pl.BlockSpec((tk, kvw2), lambda n, k: (k, 0)),
                  pl.BlockSpec((tk, d), lambda n, k: (k, 0))],
        out_specs=pl.BlockSpec((blk, d), lambda n, k: (n, 0)), out_shape=SDS((zw, d), F32),
        args=(dz, dkv, h1), sem=("parallel", "arbitrary"), ride=ride)


def _in_proj_bwd(dz, dkv, wt, x, dx1, g_mix, sc1, tm, ride=None):
    t, d = x.shape
    zw = dz.shape[1]
    kvw2 = dkv.shape[1]
    rest = d + kvw2

    def body(a_ref, akv_ref, w_ref, x_ref, dx1_ref, g_ref, sc_ref, gx_ref, st_ref):
        @pl.when(pl.program_id(0) == 0)
        def _():
            st_ref[...] = jnp.zeros_like(st_ref)

        dh = (jnp.dot(a_ref[:, :d], w_ref[:d, :], preferred_element_type=F32)
              + jnp.dot(akv_ref[...], w_ref[d:rest, :], preferred_element_type=F32)
              + jnp.dot(a_ref[:, rest:], w_ref[rest:, :], preferred_element_type=F32))
        xx = x_ref[...]
        r = _rms(xx)
        xn = xx * r
        g = g_ref[...]
        dn = dh * (1.0 + sc_ref[...])
        u = dn * g
        gx_ref[...] = dx1_ref[...] + r * (u - xn * jnp.mean(u * xn, axis=-1, keepdims=True))
        st_ref[0:1, :] += jnp.sum(dh, axis=0, keepdims=True)
        st_ref[1:2, :] += jnp.sum(dh * (xn * g), axis=0, keepdims=True)
        st_ref[2:3, :] += jnp.sum(dn * xn, axis=0, keepdims=True)

    row = pl.BlockSpec((tm, d), lambda i: (i, 0))
    vec = pl.BlockSpec((1, d), lambda i: (0, 0))
    return _pallas(
        body, name="in_proj_bwd", grid=(t // tm,),
        in_specs=[pl.BlockSpec((tm, zw), lambda i: (i, 0)), pl.BlockSpec((tm, kvw2), lambda i: (i, 0)),
                  pl.BlockSpec((zw, d), lambda i: (0, 0), pipeline_mode=pl.Buffered(1)),
                  row, row, vec, vec],
        out_specs=[row, pl.BlockSpec((8, d), lambda i: (0, 0))],
        out_shape=[SDS((t, d), F32), SDS((8, d), F32)],
        args=(dz, dkv, wt, x, dx1, g_mix, sc1), sem=("arbitrary",), ride=ride)


def _to_lanes(v, rows=None):
    flat = v.reshape(-1)
    need = -(-flat.shape[0] // LANES)
    need = -(-need // 8) * 8 if rows is None else rows
    return jnp.pad(flat, (0, need * LANES - flat.shape[0])).reshape(need, LANES)


def kernel(x, c, w_ada, b_ada, g_mix, w_in, b_in, sinks, conv_w, w_out, g_ffn, w_ffn_in, w_ffn_out, g_final, loss_target, m_w_ada, m_b_ada, m_g_mix, m_w_in, m_b_in, m_sinks, m_conv_w, m_w_out, m_g_ffn, m_w_ffn_in, m_w_ffn_out, m_g_final, v_w_ada, v_b_ada, v_g_mix, v_w_in, v_b_in, v_sinks, v_conv_w, v_w_out, v_g_ffn, v_w_ffn_in, v_w_ffn_out, v_g_final):
    xs, tgt = x[0], loss_target[0]
    t, d = xs.shape
    zw = w_in.shape[2] * N_CHIP
    kvw2 = zw - 6 * d
    ff = w_ffn_out.shape[1] * N_CHIP
    n_mod = w_ada.shape[2] * N_CHIP // d
    mod_sh = w_ada.shape[2]
    cw_sh = conv_w.shape[2]
    assert d % (8 * LANES) == 0 and kvw2 == 2 * LANES and t % 512 == 0 and n_mod == 6
    xi, yi, ci = _mesh_pos()
    j_me = 2 * xi + yi
    b_me = 4 * xi + 2 * yi + ci
    pos = jnp.stack([ci, j_me]).astype(jnp.int32)
    tm = 512

    pack1 = jnp.concatenate([c.reshape(d // LANES, LANES), conv_w[0].reshape(-1, LANES)], axis=0)
    pack1 = jnp.pad(pack1, ((0, 16 - pack1.shape[0]), (0, 0)))
    g1 = _all_gather_small(pack1, "gather_c")
    c_all = g1[:, :d // LANES, :].reshape(N_DEV, d)
    cw_rows = 3 * cw_sh // LANES
    conv_w_full = jnp.concatenate(
        [g1[2 * j, d // LANES:d // LANES + cw_rows, :].reshape(3, cw_sh) for j in range(N_CHIP)], axis=1)
    b_ada_sh = lax.dynamic_slice(b_ada, (0, j_me * mod_sh), (1, mod_sh))
    mod_all = _all_gather_small(_ada_fwd(c_all, w_ada[0], b_ada_sh), "gather_mod")
    mod = jnp.concatenate([lax.dynamic_index_in_dim(mod_all[2 * j], b_me, 0, keepdims=True) for j in range(N_CHIP)],
                          axis=1)
    sh1, sc1, ga1, sh2, sc2, ga2 = [mod[:, k * d:(k + 1) * d] for k in range(6)]

    w_in_t, m_w_in_t, v_w_in_t = w_in[0].T, m_w_in[0].T, v_w_in[0].T
    (w_in_g,) = _gather_weights([_cast_into_block(pos, w_in_t, "cast_w_in")])
    w_in_tf = w_in_g.reshape(zw, d)
    later = [_cast_into_block(pos, w_out[0], "cast_w_out"), _cast_into_block(pos, w_ffn_in[0], "cast_w_ffn_in"),
             _cast_into_block(pos, w_ffn_out[0], "cast_w_ffn_out")]

    h1 = _prenorm(xs, g_mix, sc1, sh1, tm)
    z, later = _in_proj(h1, w_in_tf, b_in, min(t, 1024), zw // 5, ride=_x_gather_ici(later))
    later = _exchange(_x_gather_d2d(later), "gather_forward")
    w_out_f = later[0].reshape(d, d)
    w_ffn_in_f = later[1]
    w_ffn_out_f = later[2].reshape(ff, d)
    attn, merged = _mixer_fwd(z, sinks, conv_w_full, d)
    y1, x1, h2 = _out_proj_fwd(merged, w_out_f, xs, ga1, g_ffn, sc2, sh2, tm)
    gu, act = _ffn_in_fwd(h2, w_ffn_in_f, ff, tm, ff // 2)
    dx2, dy2, st_loss = _ffn_out_loss(act, w_ffn_out_f, x1, tgt, ga2, g_final.reshape(1, d), tm)

    dgu = _ffn_out_bwd(dy2, w_ffn_out_f, gu, tm, ff // 2)
    tk = min(t, 2048)
    dw_ffn_out, _ = _wgrad(
        act, dy2, pl.BlockSpec((tk, ff // 2), lambda m, k: (k, m)), pl.BlockSpec((tk, d), lambda m, k: (k, 0)),
        pl.BlockSpec((ff // 2, d), lambda m, k: (m, 0)), SDS((ff, d), F32), (2, t // tk), "wgrad_ffn_out")
    dx1, dy1, st_ffn = _ffn_in_bwd(dgu, w_ffn_in_f, x1, dx2, y1, g_ffn, sc2, ga1, tm, ff // 2)
    dw_ffn_in, _ = _wgrad(
        h2, dgu, pl.BlockSpec((tk, d), lambda n, k: (k, 0)),
        pl.BlockSpec((None, tk, ff // 2), lambda n, k: (n // 2, k, n % 2)),
        pl.BlockSpec((None, d, ff // 2), lambda n, k: (n, 0, 0)), SDS((N_CHIP, d, ff // 2), F32),
        (N_CHIP, t // tk), "wgrad_ffn_in")
    dw_out, _ = _wgrad(
        merged, dy1, pl.BlockSpec((tk, d), lambda m, k: (k, 0)), pl.BlockSpec((tk, d), lambda m, k: (k, 0)),
        pl.BlockSpec((d, d), lambda m, k: (0, 0)), SDS((d, d), F32), (1, t // tk), "wgrad_out")

    early = [dw_out.reshape(N_CHIP, d // N_CHIP, d), dw_ffn_in, dw_ffn_out.reshape(N_CHIP, ff // N_CHIP, d)]
    early_names = ["w_out", "w_ffn_in", "w_ffn_out"]
    dmerged, from_sibling = _out_proj_bwd(dy1, w_out_f, tm, ride=_x_pair_send(early))
    chip32, chip16 = zip(*[_pair_add(pos, g, s, "pair_add_" + nm)
                           for g, s, nm in zip(early, from_sibling, early_names)])
    (dz, dkv, db_z, db_kv, dcw, dsk), from_chips = _mixer_bwd(z, dmerged, attn, sinks, conv_w_full, d,
                                                              ride=_x_chip_send(list(chip16)))
    fulls = [_chip_add(pos, p, r, "chip_add_" + nm) for p, r, nm in zip(chip32, from_chips, early_names)]
    dw_in_t, (g_w_out, g_w_ffn_in, g_w_ffn_out) = _wgrad_in(dz, dkv, h1, tk, ride=_x_pair_exchange(fulls))
    dw_in_t = dw_in_t.reshape(N_CHIP, zw // N_CHIP, d)

    (from_sibling,) = _exchange(_x_pair_send([dw_in_t]), "pair_send")
    chip32, chip16 = _pair_add(pos, dw_in_t, from_sibling, "pair_add_w_in")
    (grad_x, st_in), (from_chips,) = _in_proj_bwd(dz, dkv, w_in_tf, xs, dx1, g_mix, sc1, tm,
                                                  ride=_x_chip_send([chip16]))
    (g_w_in_t,) = _exchange(_x_pair_exchange([_chip_add(pos, chip32, from_chips, "chip_add_w_in")]), "pair_exchange")

    dmod = jnp.concatenate([st_in[0:1], st_in[1:2], st_ffn[3:4], st_ffn[0:1], st_ffn[1:2], st_loss[0:1]], axis=1)
    db_in = jnp.concatenate([db_z[0:1, :d], db_kv[0:1], db_z[0:1, d + kvw2:]], axis=1)
    seg = [dmod, st_in[2:3], db_in, dsk[0:1], dcw[0:3].reshape(1, 3 * d), st_ffn[2:3], st_loss[1:2],
           st_loss[2:3, :LANES]]
    sizes = [s.shape[1] for s in seg]
    pack2 = _to_lanes(jnp.concatenate(seg, axis=1))
    packs = _all_gather_small(pack2, "gather_small_grads")
    tot = _pack_sum(packs).reshape(-1)
    offs = [sum(sizes[:k]) for k in range(len(sizes))]
    gb_ada, gg_mix, gb_in, gsinks, gcw, gg_ffn, gg_final, loss_v = [tot[o:o + s] for o, s in zip(offs, sizes)]
    loss = loss_v[0]
    gsinks = gsinks[:sinks.shape[1]]
    gcw_sh = lax.dynamic_slice(gcw.reshape(3, d), (0, j_me * cw_sh), (3, cw_sh))

    dmod_all = packs[:, :n_mod * d // LANES, :].reshape(N_DEV, n_mod * d)
    g_w_ada = _ada_wgrad(c_all, lax.dynamic_slice(dmod_all, (0, j_me * mod_sh), (N_DEV, mod_sh)))

    out_g, out_d, out_m, out_v = {}, {}, {}, {}
    big = {"w_ada": (w_ada[0], g_w_ada, m_w_ada[0], v_w_ada[0]),
           "w_out": (w_out[0], g_w_out, m_w_out[0], v_w_out[0]),
           "w_ffn_in": (w_ffn_in[0], g_w_ffn_in, m_w_ffn_in[0], v_w_ffn_in[0]),
           "w_ffn_out": (w_ffn_out[0], g_w_ffn_out, m_w_ffn_out[0], v_w_ffn_out[0])}
    for nm, (w, g, m, v) in big.items():
        out_g[nm], out_d[nm], out_m[nm], out_v[nm] = [o[None] for o in _adamw(w, g, m, v, "adamw_" + nm)]
    out_g["w_in"], out_d["w_in"], out_m["w_in"], out_v["w_in"] = [
        o.T[None] for o in _adamw(w_in_t, g_w_in_t, m_w_in_t, v_w_in_t, "adamw_w_in")]
    small = {"b_ada": (b_ada, gb_ada, m_b_ada, v_b_ada), "g_mix": (g_mix, gg_mix, m_g_mix, v_g_mix),
             "b_in": (b_in, gb_in, m_b_in, v_b_in), "sinks": (sinks, gsinks, m_sinks, v_sinks),
             "conv_w": (conv_w, gcw_sh, m_conv_w, v_conv_w), "g_ffn": (g_ffn, gg_ffn, m_g_ffn, v_g_ffn),
             "g_final": (g_final, gg_final, m_g_final, v_g_final)}
    s_sizes = [w.size for w, _, _, _ in small.values()]
    s_rows = -(-sum(s_sizes) // LANES // 8) * 8

    def s_pack(k):
        return _to_lanes(jnp.concatenate([tup[k].reshape(-1) for tup in small.values()]), s_rows)

    s_out = _adamw(s_pack(0), s_pack(1), s_pack(2), s_pack(3), "adamw_small")
    s_off = 0
    for (nm, (w, g, _, _)), sz in zip(small.items(), s_sizes):
        out_g[nm] = g.reshape(w.shape)
        out_d[nm], out_m[nm], out_v[nm] = [o.reshape(-1)[s_off:s_off + sz].reshape(w.shape) for o in s_out[1:]]
        s_off += sz

    order = ["w_ada", "b_ada", "g_mix", "w_in", "b_in", "sinks", "conv_w", "w_out", "g_ffn", "w_ffn_in", "w_ffn_out",
             "g_final"]
    return (loss, grad_x[None], *[out_g[k] for k in order], *[out_d[k] for k in order],
            *[out_m[k] for k in order], *[out_v[k] for k in order])
```

```python
import functools

import jax
import jax.numpy as jnp
from jax import lax
from jax.experimental import pallas as pl
from jax.experimental.pallas import tpu as pltpu

F32 = jnp.float32
BF16 = jnp.bfloat16
EPS = 1e-6
HEAD_DIM = 64
GROUP = 8
BLOCK = 128
LANES = 128
SUBLANES_BF16 = 16
N_DEV = 8
N_CHIP = 4
VMEM_LIMIT = 56 * 1024 * 1024
MESH = pl.DeviceIdType.MESH

ADAM_LR = 0.001
ADAM_B1 = 0.9
ADAM_B2 = 0.999
ADAM_EPS = 1e-08
ADAM_WD = 0.01
ADAM_STEP = 10

SDS = jax.ShapeDtypeStruct
ANY = pl.BlockSpec(memory_space=pl.ANY)
VMEM_SPEC = pl.BlockSpec(memory_space=pltpu.VMEM)
SMEM_SPEC = pl.BlockSpec(memory_space=pltpu.SMEM)


def _params(*sem):
    return pltpu.CompilerParams(dimension_semantics=sem, vmem_limit_bytes=VMEM_LIMIT)


def _mesh_pos():
    return lax.axis_index("x"), lax.axis_index("y"), lax.axis_index("c")


def _row_tile(rows, cols, itemsize=4, budget=1 << 20, mult=8):
    best = None
    for t in range(mult, rows + 1, mult):
        if rows % t == 0 and t * cols * itemsize <= budget:
            best = t
    if best is None:
        best = rows
    return best


def _all_gather_small(v, name):
    rows, cols = v.shape

    def body(v_ref, out_ref, send_sems, recv_sems, local_sem):
        x, y, c = _mesh_pos()
        me = 4 * x + 2 * y + c
        mine = pltpu.make_async_copy(v_ref, out_ref.at[me], local_sem)
        mine.start()
        peers = []
        for k in range(1, N_DEV):
            px = 1 - x if k & 4 else x
            py = 1 - y if k & 2 else y
            pc = 1 - c if k & 1 else c
            peers.append((px, py, pc))

        def copy(k, block):
            return pltpu.make_async_remote_copy(
                src_ref=v_ref, dst_ref=out_ref.at[block], send_sem=send_sems.at[k], recv_sem=recv_sems.at[k],
                device_id=peers[k], device_id_type=MESH)

        sends = [copy(k, me) for k in range(N_DEV - 1)]
        for cp in sends:
            cp.start()
        for k, (px, py, pc) in enumerate(peers):
            copy(k, 4 * px + 2 * py + pc).wait_recv()
        for cp in sends:
            cp.wait_send()
        mine.wait()

    return pl.pallas_call(
        body, name=name,
        out_shape=SDS((N_DEV, rows, cols), v.dtype),
        in_specs=[VMEM_SPEC], out_specs=VMEM_SPEC,
        scratch_shapes=[pltpu.SemaphoreType.DMA((N_DEV - 1,)), pltpu.SemaphoreType.DMA((N_DEV - 1,)),
                        pltpu.SemaphoreType.DMA],
    )(v)


def _other_chips(x, y):
    return [(1 - x, y), (x, 1 - y), (1 - x, 1 - y)]


def _gather_weights(bufs):
    n_w = len(bufs)

    def body(*refs):
        outs = refs[n_w:2 * n_w]
        send_sems, recv_sems, fsend_sems, frecv_sems = refs[2 * n_w:]
        x, y, c = _mesh_pos()
        j_me = 2 * x + y
        chips = _other_chips(x, y)
        sibling = (x, y, 1 - c)

        def half_rows(w, which):
            half = outs[w].shape[1] // 2
            return pl.ds(pl.multiple_of(which * half, SUBLANES_BF16), half)

        def copy(w, p, block, rows, over_ici):
            sems = (send_sems, recv_sems) if over_ici else (fsend_sems, frecv_sems)
            return pltpu.make_async_remote_copy(
                src_ref=outs[w].at[block, rows], dst_ref=outs[w].at[block, rows],
                send_sem=sems[0].at[w * 3 + p], recv_sem=sems[1].at[w * 3 + p],
                device_id=(*chips[p], c) if over_ici else sibling, device_id_type=MESH)

        def block_of(p):
            return 2 * chips[p][0] + chips[p][1]

        sends = [copy(w, p, j_me, half_rows(w, c), True) for w in range(n_w) for p in range(3)]
        for cp in sends:
            cp.start()
        forwards = []
        for w in range(n_w):
            for p in range(3):
                copy(w, p, block_of(p), half_rows(w, c), True).wait_recv()
                fw = copy(w, p, block_of(p), half_rows(w, c), False)
                fw.start()
                forwards.append(fw)
        for w in range(n_w):
            for p in range(3):
                copy(w, p, block_of(p), half_rows(w, 1 - c), False).wait_recv()
        for cp in sends + forwards:
            cp.wait_send()

    return pl.pallas_call(
        body, name="gather_weights",
        out_shape=[SDS(b.shape, b.dtype) for b in bufs],
        in_specs=[ANY] * n_w, out_specs=[ANY] * n_w,
        input_output_aliases={w: w for w in range(n_w)},
        scratch_shapes=[pltpu.SemaphoreType.DMA((3 * n_w,)), pltpu.SemaphoreType.DMA((3 * n_w,)),
                        pltpu.SemaphoreType.DMA((3 * n_w,)), pltpu.SemaphoreType.DMA((3 * n_w,))],
    )(*bufs)


class _Exchange:
    def __init__(self, operands, out_shape, in_place, n_sems, copies):
        self.operands, self.out_shape, self.in_place, self.n_sems, self.copies = (
            list(operands), list(out_shape), in_place, n_sems, copies)

    def sems(self):
        return [pltpu.SemaphoreType.DMA((self.n_sems,)), pltpu.SemaphoreType.DMA((self.n_sems,))]


def _x_gather_ici(bufs):
    def copies(ins, outs, send_sems, recv_sems):
        x, y, c = _mesh_pos()
        chips = _other_chips(x, y)
        out = []
        for w in range(len(outs)):
            half = outs[w].shape[1] // 2
            rows = pl.ds(pl.multiple_of(c * half, SUBLANES_BF16), half)
            for p in range(3):
                out.append(pltpu.make_async_remote_copy(
                    src_ref=outs[w].at[2 * x + y, rows], dst_ref=outs[w].at[2 * x + y, rows],
                    send_sem=send_sems.at[w * 3 + p], recv_sem=recv_sems.at[w * 3 + p],
                    device_id=(*chips[p], c), device_id_type=MESH))
        return out

    return _Exchange(bufs, [SDS(b.shape, b.dtype) for b in bufs], True, 3 * len(bufs), copies)


def _x_gather_d2d(bufs):
    def copies(ins, outs, send_sems, recv_sems):
        x, y, c = _mesh_pos()
        chips = _other_chips(x, y)
        out = []
        for w in range(len(outs)):
            half = outs[w].shape[1] // 2
            rows = pl.ds(pl.multiple_of(c * half, SUBLANES_BF16), half)
            for p in range(3):
                block = 2 * chips[p][0] + chips[p][1]
                out.append(pltpu.make_async_remote_copy(
                    src_ref=outs[w].at[block, rows], dst_ref=outs[w].at[block, rows],
                    send_sem=send_sems.at[w * 3 + p], recv_sem=recv_sems.at[w * 3 + p],
                    device_id=(x, y, 1 - c), device_id_type=MESH))
        return out

    return _Exchange(bufs, [SDS(b.shape, b.dtype) for b in bufs], True, 3 * len(bufs), copies)


def _x_pair_send(grads):
    def copies(ins, outs, send_sems, recv_sems):
        x, y, c = _mesh_pos()
        out = []
        for w in range(len(ins)):
            half = ins[w].shape[1] // 2
            rows = pl.ds(pl.multiple_of((1 - c) * half, 8), half)
            out.append(pltpu.make_async_remote_copy(
                src_ref=ins[w].at[:, rows, :], dst_ref=outs[w], send_sem=send_sems.at[w], recv_sem=recv_sems.at[w],
                device_id=(x, y, 1 - c), device_id_type=MESH))
        return out

    return _Exchange(grads, [SDS((N_CHIP, g.shape[1] // 2, g.shape[2]), g.dtype) for g in grads], False,
                     len(grads), copies)


def _x_chip_send(partials):
    def copies(ins, outs, send_sems, recv_sems):
        x, y, c = _mesh_pos()
        chips = _other_chips(x, y)
        out = []
        for w in range(len(ins)):
            for p in range(3):
                out.append(pltpu.make_async_remote_copy(
                    src_ref=ins[w].at[2 * chips[p][0] + chips[p][1]], dst_ref=outs[w].at[p],
                    send_sem=send_sems.at[w * 3 + p], recv_sem=recv_sems.at[w * 3 + p],
                    device_id=(*chips[p], c), device_id_type=MESH))
        return out

    return _Exchange(partials, [SDS((3,) + p.shape[1:], p.dtype) for p in partials], False, 3 * len(partials),
                     copies)


def _x_pair_exchange(fulls):
    def copies(ins, outs, send_sems, recv_sems):
        x, y, c = _mesh_pos()
        out = []
        for w in range(len(outs)):
            half = outs[w].shape[0] // 2
            rows = pl.ds(pl.multiple_of(c * half, 8), half)
            out.append(pltpu.make_async_remote_copy(
                src_ref=outs[w].at[rows], dst_ref=outs[w].at[rows], send_sem=send_sems.at[w],
                recv_sem=recv_sems.at[w], device_id=(x, y, 1 - c), device_id_type=MESH))
        return out

    return _Exchange(fulls, [SDS(f.shape, f.dtype) for f in fulls], True, len(fulls), copies)


def _pallas(body, *, name, grid, in_specs, out_specs, out_shape, args, scratch=(), sem=None, ride=None):
    single = not isinstance(out_specs, (list, tuple))
    out_specs_l = [out_specs] if single else list(out_specs)
    out_shape_l = [out_shape] if single else list(out_shape)
    n_in, n_out, n_scr = len(in_specs), len(out_specs_l), len(scratch)
    if ride is None:
        res = pl.pallas_call(body, name=name, grid=grid, in_specs=list(in_specs), out_specs=out_specs,
                             out_shape=out_shape, scratch_shapes=list(scratch), compiler_params=_params(*sem))(*args)
        return res, None
    n_x, n_xo = len(ride.operands), len(ride.out_shape)

    def full_body(*refs):
        ins, x_ins = refs[:n_in], refs[n_in:n_in + n_x]
        outs = refs[n_in + n_x:n_in + n_x + n_out]
        x_outs = refs[n_in + n_x + n_out:n_in + n_x + n_out + n_xo]
        rest = refs[n_in + n_x + n_out + n_xo:]
        scr, (send_sems, recv_sems) = rest[:n_scr], rest[n_scr:]
        first = functools.reduce(jnp.logical_and, [pl.program_id(a) == 0 for a in range(len(grid))])
        last = functools.reduce(jnp.logical_and, [pl.program_id(a) == grid[a] - 1 for a in range(len(grid))])

        @pl.when(first)
        def _():
            for cp in ride.copies(x_ins, x_outs, send_sems, recv_sems):
                cp.start()

        body(*ins, *outs, *scr)

        @pl.when(last)
        def _():
            for cp in ride.copies(x_ins, x_outs, send_sems, recv_sems):
                cp.wait()

    res = pl.pallas_call(
        full_body, name=name, grid=grid, in_specs=list(in_specs) + [ANY] * n_x,
        out_specs=out_specs_l + [ANY] * n_xo, out_shape=out_shape_l + ride.out_shape,
        input_output_aliases={n_in + k: n_out + k for k in range(n_x)} if ride.in_place else {},
        scratch_shapes=list(scratch) + ride.sems(),
        compiler_params=_params(*(["arbitrary"] * len(grid))))(*args, *ride.operands)
    own = res[0] if single else list(res[:n_out])
    return own, list(res[n_out:])


def _exchange(ride, name):
    n_x, n_xo = len(ride.operands), len(ride.out_shape)

    def body(*refs):
        x_ins, x_outs = refs[:n_x], refs[n_x:n_x + n_xo]
        send_sems, recv_sems = refs[n_x + n_xo:]
        copies = ride.copies(x_ins, x_outs, send_sems, recv_sems)
        for cp in copies:
            cp.start()
        for cp in copies:
            cp.wait()

    return pl.pallas_call(
        body, name=name, in_specs=[ANY] * n_x, out_specs=[ANY] * n_xo, out_shape=ride.out_shape,
        input_output_aliases={k: k for k in range(n_x)} if ride.in_place else {},
        scratch_shapes=ride.sems())(*ride.operands)


def _cast_into_block(pos, w, name):
    rows, cols = w.shape
    tr = _row_tile(rows, cols, mult=SUBLANES_BF16)

    def body(pos_ref, w_ref, o_ref):
        del pos_ref
        o_ref[...] = w_ref[...].astype(BF16)

    return pl.pallas_call(
        body, name=name,
        grid_spec=pltpu.PrefetchScalarGridSpec(
            num_scalar_prefetch=1, grid=(rows // tr,),
            in_specs=[pl.BlockSpec((tr, cols), lambda i, pos_ref: (i, 0))],
            out_specs=pl.BlockSpec((None, tr, cols), lambda i, pos_ref: (pos_ref[1], i, 0))),
        out_shape=SDS((N_CHIP, rows, cols), BF16), compiler_params=_params("parallel"))(pos, w)


def _pair_add(pos, grad, from_sibling, name):
    _, rows, cols = grad.shape
    half = rows // 2
    tr = _row_tile(half, cols, mult=SUBLANES_BF16)
    nblk = half // tr

    def body(pos_ref, g_ref, s_ref, o32_ref, o16_ref):
        del pos_ref
        s = g_ref[...] + s_ref[...]
        o32_ref[...] = s
        o16_ref[...] = s.astype(BF16)

    spec = pl.BlockSpec((None, tr, cols), lambda j, i, pos_ref: (j, i, 0))
    return pl.pallas_call(
        body, name=name,
        grid_spec=pltpu.PrefetchScalarGridSpec(
            num_scalar_prefetch=1, grid=(N_CHIP, nblk),
            in_specs=[pl.BlockSpec((None, tr, cols), lambda j, i, pos_ref: (j, pos_ref[0] * nblk + i, 0)), spec],
            out_specs=[spec, spec]),
        out_shape=[SDS((N_CHIP, half, cols), F32), SDS((N_CHIP, half, cols), BF16)],
        compiler_params=_params("parallel", "parallel"),
    )(pos, grad, from_sibling)


def _chip_add(pos, partial32, from_chips, name):
    _, half, cols = partial32.shape
    tr = _row_tile(half, cols, mult=SUBLANES_BF16)

    def body(pos_ref, p_ref, r_ref, o_ref):
        del pos_ref
        acc = p_ref[...]
        for p in range(3):
            acc = acc + r_ref[p].astype(F32)
        o_ref[...] = acc

    return pl.pallas_call(
        body, name=name,
        grid_spec=pltpu.PrefetchScalarGridSpec(
            num_scalar_prefetch=1, grid=(half // tr,),
            in_specs=[pl.BlockSpec((None, tr, cols), lambda i, pos_ref: (pos_ref[1], i, 0)),
                      pl.BlockSpec((3, tr, cols), lambda i, pos_ref: (0, i, 0))],
            out_specs=pl.BlockSpec((tr, cols), lambda i, pos_ref: (pos_ref[0] * (half // tr) + i, 0))),
        out_shape=SDS((2 * half, cols), F32),
        compiler_params=_params("parallel"),
    )(pos, partial32, from_chips)


def _adamw(w, g, m, v, name):
    rows, cols = w.shape
    tr = _row_tile(rows, cols, budget=1 << 19)

    def body(w_ref, g_ref, m_ref, v_ref, go_ref, d_ref, nm_ref, nv_ref):
        gg = g_ref[...]
        go_ref[...] = gg
        nm = ADAM_B1 * m_ref[...] + (1.0 - ADAM_B1) * gg
        nv = ADAM_B2 * v_ref[...] + (1.0 - ADAM_B2) * (gg * gg)
        m_hat = nm / (1.0 - ADAM_B1 ** ADAM_STEP)
        v_hat = nv / (1.0 - ADAM_B2 ** ADAM_STEP)
        d_ref[...] = -ADAM_LR * (m_hat / (jnp.sqrt(v_hat) + ADAM_EPS) + ADAM_WD * w_ref[...])
        nm_ref[...] = nm
        nv_ref[...] = nv

    spec = pl.BlockSpec((tr, cols), lambda i: (i, 0))
    return pl.pallas_call(body, name=name, grid=(rows // tr,), in_specs=[spec] * 4, out_specs=[spec] * 4,
                          out_shape=[SDS((rows, cols), F32)] * 4, compiler_params=_params("parallel"))(w, g, m, v)


def _pack_sum(gathered):
    _, rows, cols = gathered.shape

    def body(g_ref, o_ref):
        acc = g_ref[0]
        for d in range(1, N_DEV):
            acc = acc + g_ref[d]
        o_ref[...] = acc

    return pl.pallas_call(body, name="pack_sum", in_specs=[VMEM_SPEC], out_specs=VMEM_SPEC,
                          out_shape=SDS((rows, cols), F32))(gathered)


def _ada_fwd(c_all, w_sh, b_sh):
    d, n = w_sh.shape
    tn = 512

    def body(c_ref, w_ref, b_ref, o_ref):
        cc = c_ref[...]
        s = (cc * jax.nn.sigmoid(cc)).astype(BF16)
        o_ref[...] = jnp.dot(s, w_ref[...].astype(BF16), preferred_element_type=F32) + b_ref[...]

    return pl.pallas_call(
        body, name="ada_fwd", grid=(n // tn,),
        in_specs=[pl.BlockSpec((N_DEV, d), lambda j: (0, 0)), pl.BlockSpec((d, tn), lambda j: (0, j)),
                  pl.BlockSpec((1, tn), lambda j: (0, j))],
        out_specs=pl.BlockSpec((N_DEV, tn), lambda j: (0, j)),
        out_shape=SDS((N_DEV, n), F32), compiler_params=_params("parallel"))(c_all, w_sh, b_sh)


def _ada_wgrad(c_all, dmod_sh):
    d = c_all.shape[1]
    n = dmod_sh.shape[1]
    tn = 512

    def body(c_ref, g_ref, o_ref):
        cc = c_ref[...]
        s = cc * jax.nn.sigmoid(cc)
        o_ref[...] = lax.dot_general(s, g_ref[...], (((0,), (0,)), ((), ())), preferred_element_type=F32,
                                     precision=lax.Precision.HIGHEST)

    return pl.pallas_call(
        body, name="ada_wgrad", grid=(n // tn,),
        in_specs=[pl.BlockSpec((N_DEV, d), lambda j: (0, 0)), pl.BlockSpec((N_DEV, tn), lambda j: (0, j))],
        out_specs=pl.BlockSpec((d, tn), lambda j: (0, j)),
        out_shape=SDS((d, n), F32), compiler_params=_params("parallel"))(c_all, dmod_sh)


def _rms(xf):
    return lax.rsqrt(jnp.mean(xf * xf, axis=-1, keepdims=True) + EPS)


def _prenorm(x, g, sc, sh, tm):
    t, d = x.shape

    def body(x_ref, g_ref, sc_ref, sh_ref, h_ref):
        xf = x_ref[...]
        h_ref[...] = ((xf * _rms(xf) * g_ref[...]) * (1.0 + sc_ref[...]) + sh_ref[...]).astype(BF16)

    row = pl.BlockSpec((tm, d), lambda i: (i, 0))
    vec = pl.BlockSpec((1, d), lambda i: (0, 0))
    return pl.pallas_call(body, name="prenorm", grid=(t // tm,), in_specs=[row, vec, vec, vec], out_specs=row,
                          out_shape=SDS((t, d), BF16), compiler_params=_params("parallel"))(x, g, sc, sh)


def _in_proj(h, wt, b, tm, tn, ride=None):
    t, d = h.shape
    n = wt.shape[0]

    def body(h_ref, w_ref, b_ref, z_ref):
        acc = lax.dot_general(h_ref[...], w_ref[...], (((1,), (1,)), ((), ())), preferred_element_type=F32)
        z_ref[...] = (acc + b_ref[...]).astype(BF16)

    return _pallas(
        body, name="in_proj", grid=(n // tn, t // tm),
        in_specs=[pl.BlockSpec((tm, d), lambda j, i: (i, 0)), pl.BlockSpec((tn, d), lambda j, i: (j, 0)),
                  pl.BlockSpec((1, tn), lambda j, i: (0, j))],
        out_specs=pl.BlockSpec((tm, tn), lambda j, i: (i, j)),
        out_shape=SDS((t, n), BF16), args=(h, wt, b), sem=("parallel", "parallel"), ride=ride)


def _segments(d, kvw2):
    o = d + kvw2
    names = ("cb", "cc", "cx", "ga", "gc")
    seg = {nm: slice(o + k * d, o + (k + 1) * d) for k, nm in enumerate(names)}
    seg["q"], seg["kv"] = slice(0, d), slice(d, o)
    return seg


def _attn_masks():
    rows = 4 * BLOCK
    r = lax.broadcasted_iota(jnp.int32, (rows, 2 * BLOCK), 0) & (BLOCK - 1)
    col = lax.broadcasted_iota(jnp.int32, (rows, 2 * BLOCK), 1)
    return (col > r) & (col <= r + BLOCK), col


def _kv_variants(kv, n_kv_w):
    assert n_kv_w == LANES
    kb, vb = kv[:, :LANES] * (HEAD_DIM ** -0.5), kv[:, LANES:]
    kr, vr = pltpu.roll(kb, HEAD_DIM, 1), pltpu.roll(vb, HEAD_DIM, 1)
    lane = lax.broadcasted_iota(jnp.int32, kb.shape, 1)
    lo = lane < HEAD_DIM
    zero = jnp.zeros_like(kb)
    k_eff = [[None, None], [None, None]]
    v_eff = [[None, None], [None, None]]
    for h in range(2):
        for e in range(2):
            ksrc, vsrc = (kb, vb) if e == h else (kr, vr)
            keep = lo if e == 0 else jnp.logical_not(lo)
            k_eff[h][e] = jnp.where(keep, ksrc, zero)
            v_eff[h][e] = jnp.where(keep, vsrc, zero)
    return k_eff, v_eff


def _sink_column(sinks_ref, h, e):
    rowblk = lax.broadcasted_iota(jnp.int32, (4 * BLOCK, 1), 0) // BLOCK
    col = jnp.zeros((4 * BLOCK, 1), F32)
    for j in range(4):
        col = jnp.where(rowblk == j, sinks_ref[0, GROUP * h + 2 * j + e], col)
    return col


def _softmax_sink(s, valid, sink):
    s = jnp.where(valid, s, -jnp.inf)
    m = jnp.maximum(jnp.max(s, axis=-1, keepdims=True), sink)
    p = jnp.exp(s - m)
    psink = jnp.exp(sink - m)
    den = jnp.sum(p, axis=-1, keepdims=True) + psink
    inv = 1.0 / den
    return p * inv, psink * inv


def _shift_down(a, s, prev):
    rows = a.shape[0]
    out = pltpu.roll(a, s, 0)
    row = lax.broadcasted_iota(jnp.int32, a.shape, 0)
    for t in range(s):
        out = jnp.where(row == t, prev[SUBLANES_BF16 - s + t:SUBLANES_BF16 - s + t + 1, :], out)
    del rows
    return out


def _shift_up(a, s, nxt):
    rows = a.shape[0]
    out = pltpu.roll(a, rows - s, 0)
    row = lax.broadcasted_iota(jnp.int32, a.shape, 0)
    for t in range(s):
        out = jnp.where(row == rows - s + t, nxt[t:t + 1, :], out)
    return out


def _stack_pairs(ref, h):
    return jnp.concatenate([ref[:, (4 * h + j) * LANES:(4 * h + j + 1) * LANES] for j in range(4)], axis=0)


def _mixer_fwd(z, sinks, conv_w, d):
    t, zw = z.shape
    kvw2 = zw - 6 * d
    nb = t // BLOCK
    halo = BLOCK // SUBLANES_BF16
    seg = _segments(d, kvw2)

    def body(z_ref, kvp_ref, prev_ref, sinks_ref, cw_ref, attn_ref, merged_ref):
        n = pl.program_id(0)
        kv = jnp.concatenate([kvp_ref[...], z_ref[:, seg["kv"]]], axis=0)
        k_eff, v_eff = _kv_variants(kv, kvw2 // 2)
        band, col = _attn_masks()
        valid = band & ((n > 0) | (col >= BLOCK))
        for h in range(2):
            q4 = _stack_pairs(z_ref, h)
            o4 = jnp.zeros((4 * BLOCK, LANES), F32)
            for e in range(2):
                s = lax.dot_general(q4, k_eff[h][e], (((1,), (1,)), ((), ())), preferred_element_type=F32)
                p, _ = _softmax_sink(s, valid, _sink_column(sinks_ref, h, e))
                o4 = o4 + jnp.dot(p.astype(BF16), v_eff[h][e], preferred_element_type=F32)
            for j in range(4):
                attn_ref[:, (4 * h + j) * LANES:(4 * h + j + 1) * LANES] = o4[j * BLOCK:(j + 1) * BLOCK].astype(BF16)
        cb = z_ref[:, seg["cb"]].astype(F32)
        p_in = z_ref[:, seg["cc"]].astype(F32) * z_ref[:, seg["cx"]].astype(F32)
        prev = jnp.where(n > 0, prev_ref[:, seg["cc"]].astype(F32) * prev_ref[:, seg["cx"]].astype(F32), 0.0)
        cconv = (cw_ref[0:1, :] * _shift_down(p_in, 2, prev) + cw_ref[1:2, :] * _shift_down(p_in, 1, prev)
                 + cw_ref[2:3, :] * p_in)
        sa = jax.nn.sigmoid(z_ref[:, seg["ga"]].astype(F32))
        sg = jax.nn.sigmoid(z_ref[:, seg["gc"]].astype(F32))
        merged_ref[...] = (sa * attn_ref[...].astype(F32) + sg * (cb * cconv)).astype(BF16)

    blk = pl.BlockSpec((BLOCK, d), lambda n: (n, 0))
    return pl.pallas_call(
        body, name="mixer_fwd", grid=(nb,),
        in_specs=[pl.BlockSpec((BLOCK, zw), lambda n: (n, 0)),
                  pl.BlockSpec((BLOCK, kvw2), lambda n: (jnp.maximum(n - 1, 0), d // kvw2)),
                  pl.BlockSpec((SUBLANES_BF16, zw), lambda n: (jnp.maximum(n * halo - 1, 0), 0)),
                  SMEM_SPEC, pl.BlockSpec((3, d), lambda n: (0, 0))],
        out_specs=[blk, blk],
        out_shape=[SDS((t, d), BF16), SDS((t, d), BF16)],
        compiler_params=_params("parallel"))(z, z, z, sinks, conv_w)


def _out_proj_fwd(merged, w_out, x, ga1, g_ffn, sc2, sh2, tm):
    t, d = x.shape

    def body(m_ref, w_ref, x_ref, ga_ref, g_ref, sc_ref, sh_ref, y_ref, x1_ref, h_ref):
        y = jnp.dot(m_ref[...], w_ref[...], preferred_element_type=F32)
        x1 = x_ref[...] + ga_ref[...] * y
        y_ref[...] = y.astype(BF16)
        x1_ref[...] = x1
        h_ref[...] = ((x1 * _rms(x1) * g_ref[...]) * (1.0 + sc_ref[...]) + sh_ref[...]).astype(BF16)

    row = pl.BlockSpec((tm, d), lambda i: (i, 0))
    vec = pl.BlockSpec((1, d), lambda i: (0, 0))
    return pl.pallas_call(
        body, name="out_proj_fwd", grid=(t // tm,),
        in_specs=[row, pl.BlockSpec((d, d), lambda i: (0, 0)), row, vec, vec, vec, vec],
        out_specs=[row, row, row],
        out_shape=[SDS((t, d), BF16), SDS((t, d), F32), SDS((t, d), BF16)],
        compiler_params=_params("parallel"))(merged, w_out, x, ga1, g_ffn, sc2, sh2)


def _ffn_in_fwd(h2, w, ff, tm, tn):
    t, d = h2.shape
    nj = ff // tn
    assert w.shape == (2 * nj, d, tn)

    def body(h_ref, wg_ref, wu_ref, gu_ref, act_ref):
        hh = h_ref[...]
        g = jnp.dot(hh, wg_ref[...], preferred_element_type=F32)
        u = jnp.dot(hh, wu_ref[...], preferred_element_type=F32)
        gu_ref[0] = g.astype(BF16)
        gu_ref[1] = u.astype(BF16)
        act_ref[...] = ((g * jax.nn.sigmoid(g)) * u).astype(BF16)

    return pl.pallas_call(
        body, name="ffn_in_fwd", grid=(nj, t // tm),
        in_specs=[pl.BlockSpec((tm, d), lambda j, i: (i, 0)), pl.BlockSpec((None, d, tn), lambda j, i: (j, 0, 0)),
                  pl.BlockSpec((None, d, tn), lambda j, i: (j + nj, 0, 0))],
        out_specs=[pl.BlockSpec((2, tm, tn), lambda j, i: (0, i, j)), pl.BlockSpec((tm, tn), lambda j, i: (i, j))],
        out_shape=[SDS((2, t, ff), BF16), SDS((t, ff), BF16)],
        compiler_params=_params("parallel", "parallel"))(h2, w, w)


def _ffn_out_loss(act, w, x1, target, ga2, g_final, tm):
    t, d = x1.shape
    ff = act.shape[1]

    def body(a_ref, w_ref, x1_ref, tg_ref, ga_ref, gf_ref, dx2_ref, dy2_ref, st_ref):
        @pl.when(pl.program_id(0) == 0)
        def _():
            st_ref[...] = jnp.zeros_like(st_ref)

        y2 = jnp.dot(a_ref[...], w_ref[...], preferred_element_type=F32)
        x2 = x1_ref[...] + ga_ref[...] * y2
        r = _rms(x2)
        yn = x2 * r
        err = yn * gf_ref[...] - tg_ref[...]
        loss = 0.5 * jnp.sum(jnp.mean(err * err, axis=-1, keepdims=True), axis=0, keepdims=True)
        dy = err * (1.0 / d)
        u = dy * gf_ref[...]
        dx2 = r * (u - yn * jnp.mean(u * yn, axis=-1, keepdims=True))
        dx2_ref[...] = dx2
        dy2_ref[...] = (ga_ref[...] * dx2).astype(BF16)
        st_ref[0:1, :] += jnp.sum(dx2 * y2, axis=0, keepdims=True)
        st_ref[1:2, :] += jnp.sum(dy * yn, axis=0, keepdims=True)
        st_ref[2:3, :] += jnp.broadcast_to(loss, (1, d))

    row = pl.BlockSpec((tm, d), lambda i: (i, 0))
    vec = pl.BlockSpec((1, d), lambda i: (0, 0))
    return pl.pallas_call(
        body, name="ffn_out_loss", grid=(t // tm,),
        in_specs=[pl.BlockSpec((tm, ff), lambda i: (i, 0)), pl.BlockSpec((ff, d), lambda i: (0, 0)), row, row,
                  vec, vec],
        out_specs=[row, row, pl.BlockSpec((8, d), lambda i: (0, 0))],
        out_shape=[SDS((t, d), F32), SDS((t, d), BF16), SDS((8, d), F32)],
        compiler_params=_params("arbitrary"))(act, w, x1, target, ga2, g_final)


def _ffn_out_bwd(dy2, w, gu, tm, tn):
    t, d = dy2.shape
    ff = w.shape[0]

    def body(dy_ref, w_ref, gu_ref, o_ref):
        dy = dy_ref[...]
        for lo in range(0, tn, 3 * LANES):
            cols = slice(lo, min(lo + 3 * LANES, tn))
            dact = lax.dot_general(dy, w_ref[cols, :], (((1,), (1,)), ((), ())), preferred_element_type=F32)
            g = gu_ref[0, :, cols].astype(F32)
            u = gu_ref[1, :, cols].astype(F32)
            sg = jax.nn.sigmoid(g)
            a = dact * sg
            du = a * g
            o_ref[0, :, cols] = (u * (a + du * (1.0 - sg))).astype(BF16)
            o_ref[1, :, cols] = du.astype(BF16)

    gu_spec = pl.BlockSpec((2, tm, tn), lambda j, i: (0, i, j))
    return pl.pallas_call(
        body, name="ffn_out_bwd", grid=(ff // tn, t // tm),
        in_specs=[pl.BlockSpec((tm, d), lambda j, i: (i, 0)), pl.BlockSpec((tn, d), lambda j, i: (j, 0)), gu_spec],
        out_specs=gu_spec, out_shape=SDS((2, t, ff), BF16),
        compiler_params=_params("parallel", "parallel"))(dy2, w, gu)


def _wgrad(a, b, a_spec, b_spec, out_spec, out_shape, grid, name, ride=None):
    def body(a_ref, b_ref, o_ref):
        @pl.when(pl.program_id(len(grid) - 1) == 0)
        def _():
            o_ref[...] = jnp.zeros_like(o_ref)

        o_ref[...] += lax.dot_general(a_ref[...], b_ref[...], (((0,), (0,)), ((), ())), preferred_element_type=F32)

    return _pallas(
        body, name=name, grid=grid, in_specs=[a_spec, b_spec], out_specs=out_spec, out_shape=out_shape, args=(a, b),
        sem=["parallel"] * (len(grid) - 1) + ["arbitrary"], ride=ride)


def _ffn_in_bwd(dgu, w, x1, dx2, y1, g_ffn, sc2, ga1, tm, tk):
    t, d = x1.shape
    ff = dgu.shape[2]
    n_sh, _, sw = w.shape
    per = ff // sw
    del tk
    nt = (((1,), (1,)), ((), ()))

    def body(a_ref, w_ref, x1_ref, dx2_ref, y1_ref, g_ref, sc_ref, ga_ref, dx1_ref, dy1_ref, st_ref):
        @pl.when(pl.program_id(0) == 0)
        def _():
            st_ref[...] = jnp.zeros_like(st_ref)

        dh = None
        for j in range(n_sh):
            part = lax.dot_general(a_ref[j // per, :, (j % per) * sw:(j % per + 1) * sw], w_ref[j], nt,
                                   preferred_element_type=F32)
            dh = part if dh is None else dh + part
        x1 = x1_ref[...]
        r = _rms(x1)
        xn = x1 * r
        g = g_ref[...]
        dn = dh * (1.0 + sc_ref[...])
        u = dn * g
        dx1 = dx2_ref[...] + r * (u - xn * jnp.mean(u * xn, axis=-1, keepdims=True))
        dx1_ref[...] = dx1
        dy1_ref[...] = (ga_ref[...] * dx1).astype(BF16)
        st_ref[0:1, :] += jnp.sum(dh, axis=0, keepdims=True)
        st_ref[1:2, :] += jnp.sum(dh * (xn * g), axis=0, keepdims=True)
        st_ref[2:3, :] += jnp.sum(dn * xn, axis=0, keepdims=True)
        st_ref[3:4, :] += jnp.sum(dx1 * y1_ref[...].astype(F32), axis=0, keepdims=True)

    row = pl.BlockSpec((tm, d), lambda i: (i, 0))
    vec = pl.BlockSpec((1, d), lambda i: (0, 0))
    return pl.pallas_call(
        body, name="ffn_in_bwd", grid=(t // tm,),
        in_specs=[pl.BlockSpec((2, tm, ff), lambda i: (0, i, 0)),
                  pl.BlockSpec((n_sh, d, sw), lambda i: (0, 0, 0), pipeline_mode=pl.Buffered(1)),
                  row, row, row, vec, vec, vec],
        out_specs=[row, row, pl.BlockSpec((8, d), lambda i: (0, 0))],
        out_shape=[SDS((t, d), F32), SDS((t, d), BF16), SDS((8, d), F32)],
        compiler_params=_params("arbitrary"))(dgu, w, x1, dx2, y1, g_ffn, sc2, ga1)


def _out_proj_bwd(dy1, w_out, tm, ride=None):
    t, d = dy1.shape

    def body(dy_ref, w_ref, o_ref):
        o_ref[...] = lax.dot_general(dy_ref[...], w_ref[...], (((1,), (1,)), ((), ())),
                                     preferred_element_type=F32).astype(BF16)

    row = pl.BlockSpec((tm, d), lambda i: (i, 0))
    return _pallas(body, name="out_proj_bwd", grid=(t // tm,),
                   in_specs=[row, pl.BlockSpec((d, d), lambda i: (0, 0))], out_specs=row,
                   out_shape=SDS((t, d), BF16), args=(dy1, w_out), sem=("parallel",), ride=ride)


def _mixer_bwd(z, dmerged, attn, sinks, conv_w, d, ride=None):
    t, zw = z.shape
    kvw2 = zw - 6 * d
    nb = t // BLOCK
    halo = BLOCK // SUBLANES_BF16
    last_halo = t // SUBLANES_BF16 - 1
    scale = HEAD_DIM ** -0.5
    seg = _segments(d, kvw2)

    def body(z_ref, kvp_ref, prev_ref, next_ref, dm_ref, dmn_ref, attn_ref, sinks_ref, cw_ref,
             dz_ref, dkv_ref, db_ref, dbkv_ref, dcw_ref, dsk_ref, carry_ref):
        n = pl.program_id(0)

        @pl.when(n == 0)
        def _():
            carry_ref[...] = jnp.zeros_like(carry_ref)
            db_ref[...] = jnp.zeros_like(db_ref)
            dbkv_ref[...] = jnp.zeros_like(dbkv_ref)
            dcw_ref[...] = jnp.zeros_like(dcw_ref)
            dsk_ref[...] = jnp.zeros_like(dsk_ref)

        @pl.when(n < nb)
        def _():
            dm = dm_ref[...].astype(F32)
            sa = jax.nn.sigmoid(z_ref[:, seg["ga"]].astype(F32))
            dga = dm * attn_ref[...].astype(F32) * sa * (1.0 - sa)
            dz_ref[:, seg["ga"]] = dga.astype(BF16)
            db_ref[0:1, seg["ga"]] += jnp.sum(dga, axis=0, keepdims=True)
            dattn = (dm * sa).astype(BF16)

            kv = jnp.concatenate([kvp_ref[...], z_ref[:, seg["kv"]]], axis=0)
            k_eff, v_eff = _kv_variants(kv, kvw2 // 2)
            band, col = _attn_masks()
            valid = band & ((n > 0) | (col >= BLOCK))
            lane_lo = lax.broadcasted_iota(jnp.int32, (2 * BLOCK, LANES), 1) < HEAD_DIM
            sink_lane = lax.broadcasted_iota(jnp.int32, (1, LANES), 1)
            rowblk = lax.broadcasted_iota(jnp.int32, (4 * BLOCK, 1), 0) // BLOCK
            dk_acc = [jnp.zeros((2 * BLOCK, LANES), F32), jnp.zeros((2 * BLOCK, LANES), F32)]
            dv_acc = [jnp.zeros((2 * BLOCK, LANES), F32), jnp.zeros((2 * BLOCK, LANES), F32)]
            dsink = jnp.zeros((1, LANES), F32)
            for h in range(2):
                q4 = _stack_pairs(z_ref, h)
                do4 = jnp.concatenate([dattn[:, (4 * h + j) * LANES:(4 * h + j + 1) * LANES] for j in range(4)],
                                      axis=0)
                dq4 = jnp.zeros((4 * BLOCK, LANES), F32)
                for e in range(2):
                    s = lax.dot_general(q4, k_eff[h][e], (((1,), (1,)), ((), ())), preferred_element_type=F32)
                    p, psink = _softmax_sink(s, valid, _sink_column(sinks_ref, h, e))
                    dp = lax.dot_general(do4, v_eff[h][e], (((1,), (1,)), ((), ())), preferred_element_type=F32)
                    delta = jnp.sum(p * dp, axis=-1, keepdims=True)
                    ds = (p * (dp - delta)).astype(BF16)
                    dq4 = dq4 + jnp.dot(ds, k_eff[h][e], preferred_element_type=F32)
                    dk = lax.dot_general(q4, ds, (((0,), (0,)), ((), ())), preferred_element_type=F32).T
                    dv = lax.dot_general(do4, p.astype(BF16), (((0,), (0,)), ((), ())), preferred_element_type=F32).T
                    keep = lane_lo if e == 0 else jnp.logical_not(lane_lo)
                    slot = 0 if e == h else 1
                    dk_acc[slot] = dk_acc[slot] + jnp.where(keep, dk, 0.0)
                    dv_acc[slot] = dv_acc[slot] + jnp.where(keep, dv, 0.0)
                    dsk = -(psink * delta)
                    for j in range(4):
                        tot = jnp.sum(jnp.where(rowblk == j, dsk, 0.0), axis=0, keepdims=True)
                        dsink = dsink + jnp.where(sink_lane == GROUP * h + 2 * j + e, tot, 0.0)
                for j in range(4):
                    cols = slice((4 * h + j) * LANES, (4 * h + j + 1) * LANES)
                    dqj = dq4[j * BLOCK:(j + 1) * BLOCK]
                    dz_ref[:, cols] = dqj.astype(BF16)
                    db_ref[0:1, cols] += jnp.sum(dqj, axis=0, keepdims=True)
            dsk_ref[0:1, :] += dsink
            dkv_new = jnp.concatenate([(dk_acc[0] + pltpu.roll(dk_acc[1], HEAD_DIM, 1)) * scale,
                                       dv_acc[0] + pltpu.roll(dv_acc[1], HEAD_DIM, 1)], axis=1)
            done = carry_ref[...] + dkv_new[:BLOCK]
            dkv_ref[...] = done.astype(BF16)
            dbkv_ref[0:1, :] += jnp.sum(done, axis=0, keepdims=True)
            carry_ref[...] = dkv_new[BLOCK:]

            cb = z_ref[:, seg["cb"]].astype(F32)
            cc = z_ref[:, seg["cc"]].astype(F32)
            cx = z_ref[:, seg["cx"]].astype(F32)
            sg = jax.nn.sigmoid(z_ref[:, seg["gc"]].astype(F32))
            p_in = cc * cx
            prev = jnp.where(n > 0, prev_ref[:, seg["cc"]].astype(F32) * prev_ref[:, seg["cx"]].astype(F32), 0.0)
            p_m1 = _shift_down(p_in, 1, prev)
            p_m2 = _shift_down(p_in, 2, prev)
            w0, w1, w2 = cw_ref[0:1, :], cw_ref[1:2, :], cw_ref[2:3, :]
            cconv = w0 * p_m2 + w1 * p_m1 + w2 * p_in
            dconv = dm * sg
            dgc = dm * (cb * cconv) * sg * (1.0 - sg)
            dcb = dconv * cconv
            dcc_t = dconv * cb
            nxt = jnp.where(n < nb - 1, dmn_ref[...].astype(F32) * jax.nn.sigmoid(next_ref[:, seg["gc"]].astype(F32))
                            * next_ref[:, seg["cb"]].astype(F32), 0.0)
            dpin = w2 * dcc_t + w1 * _shift_up(dcc_t, 1, nxt) + w0 * _shift_up(dcc_t, 2, nxt)
            for nm, val in (("cb", dcb), ("cc", dpin * cx), ("cx", dpin * cc), ("gc", dgc)):
                dz_ref[:, seg[nm]] = val.astype(BF16)
                db_ref[0:1, seg[nm]] += jnp.sum(val, axis=0, keepdims=True)
            dcw_ref[0:1, :] += jnp.sum(dcc_t * p_m2, axis=0, keepdims=True)
            dcw_ref[1:2, :] += jnp.sum(dcc_t * p_m1, axis=0, keepdims=True)
            dcw_ref[2:3, :] += jnp.sum(dcc_t * p_in, axis=0, keepdims=True)

        @pl.when(n == nb)
        def _():
            done = carry_ref[...]
            dkv_ref[...] = done.astype(BF16)
            dbkv_ref[0:1, :] += jnp.sum(done, axis=0, keepdims=True)

    def cur(n):
        return jnp.minimum(n, nb - 1)

    blk = pl.BlockSpec((BLOCK, d), lambda n: (cur(n), 0))
    return _pallas(
        body, name="mixer_bwd", grid=(nb + 1,), ride=ride, sem=("arbitrary",),
        args=(z, z, z, z, dmerged, dmerged, attn, sinks, conv_w),
        in_specs=[pl.BlockSpec((BLOCK, zw), lambda n: (cur(n), 0)),
                  pl.BlockSpec((BLOCK, kvw2), lambda n: (jnp.maximum(cur(n) - 1, 0), d // kvw2)),
                  pl.BlockSpec((SUBLANES_BF16, zw), lambda n: (jnp.maximum(cur(n) * halo - 1, 0), 0)),
                  pl.BlockSpec((SUBLANES_BF16, zw), lambda n: (jnp.minimum((cur(n) + 1) * halo, last_halo), 0)),
                  blk,
                  pl.BlockSpec((SUBLANES_BF16, d), lambda n: (jnp.minimum((cur(n) + 1) * halo, last_halo), 0)),
                  blk, SMEM_SPEC, pl.BlockSpec((3, d), lambda n: (0, 0))],
        out_specs=[pl.BlockSpec((BLOCK, zw), lambda n: (cur(n), 0)),
                   pl.BlockSpec((BLOCK, kvw2), lambda n: (jnp.maximum(n - 1, 0), 0)),
                   pl.BlockSpec((8, zw), lambda n: (0, 0)), pl.BlockSpec((8, kvw2), lambda n: (0, 0)),
                   pl.BlockSpec((8, d), lambda n: (0, 0)), pl.BlockSpec((8, LANES), lambda n: (0, 0))],
        out_shape=[SDS((t, zw), BF16), SDS((t, kvw2), BF16), SDS((8, zw), F32), SDS((8, kvw2), F32),
                   SDS((8, d), F32), SDS((8, LANES), F32)],
        scratch=[pltpu.VMEM((BLOCK, kvw2), F32)])


def _wgrad_in(dz, dkv, h1, tk, ride=None):
    t, zw = dz.shape
    d = h1.shape[1]
    kvw2 = dkv.shape[1]
    blk = d + kvw2
    assert zw % blk == 0
    tn = (((0,), (0,)), ((), ()))

    def body(a_ref, akv_ref, h_ref, o_ref):
        n, k = pl.program_id(0), pl.program_id(1)

        @pl.when(k == 0)
        def _():
            o_ref[...] = jnp.zeros_like(o_ref)

        @pl.when(n == 0)
        def _():
            o_ref[:d, :] += lax.dot_general(a_ref[:, :d], h_ref[...], tn, preferred_element_type=F32)
            o_ref[d:, :] += lax.dot_general(akv_ref[...], h_ref[...], tn, preferred_element_type=F32)

        @pl.when(n > 0)
        def _():
            o_ref[...] += lax.dot_general(a_ref[...], h_ref[...], tn, preferred_element_type=F32)

    return _pallas(
        body, name="wgrad_in", grid=(zw // blk, t // tk),
        in_specs=[pl.BlockSpec((tk, blk), lambda n, k: (k, n)), pl.BlockSpec((tk, kvw2), lambda n, k: (k, 0)),
                  pl.BlockSpec((tk, d), lambda n, k: (k, 0))],
        out_specs=pl.BlockSpec((blk, d), lambda n, k: (n, 0)), out_shape=SDS((zw, d), F32),
        args=(dz, dkv, h1), sem=("parallel", "arbitrary"), ride=ride)


def _in_proj_bwd(dz, dkv, wt, x, dx1, g_mix, sc1, tm, ride=None):
    t, d = x.shape
    zw = dz.shape[1]
    kvw2 = dkv.shape[1]
    rest = d + kvw2

    def body(a_ref, akv_ref, w_ref, x_ref, dx1_ref, g_ref, sc_ref, gx_ref, st_ref):
        @pl.when(pl.program_id(0) == 0)
        def _():
            st_ref[...] = jnp.zeros_like(st_ref)

        dh = (jnp.dot(a_ref[:, :d], w_ref[:d, :], preferred_element_type=F32)
              + jnp.dot(akv_ref[...], w_ref[d:rest, :], preferred_element_type=F32)
              + jnp.dot(a_ref[:, rest:], w_ref[rest:, :], preferred_element_type=F32))
        xx = x_ref[...]
        r = _rms(xx)
        xn = xx * r
        g = g_ref[...]
        dn = dh * (1.0 + sc_ref[...])
        u = dn * g
        gx_ref[...] = dx1_ref[...] + r * (u - xn * jnp.mean(u * xn, axis=-1, keepdims=True))
        st_ref[0:1, :] += jnp.sum(dh, axis=0, keepdims=True)
        st_ref[1:2, :] += jnp.sum(dh * (xn * g), axis=0, keepdims=True)
        st_ref[2:3, :] += jnp.sum(dn * xn, axis=0, keepdims=True)

    row = pl.BlockSpec((tm, d), lambda i: (i, 0))
    vec = pl.BlockSpec((1, d), lambda i: (0, 0))
    return _pallas(
        body, name="in_proj_bwd", grid=(t // tm,),
        in_specs=[pl.BlockSpec((tm, zw), lambda i: (i, 0)), pl.BlockSpec((tm, kvw2), lambda i: (i, 0)),
                  pl.BlockSpec((zw, d), lambda i: (0, 0), pipeline_mode=pl.Buffered(1)),
                  row, row, vec, vec],
        out_specs=[row, pl.BlockSpec((8, d), lambda i: (0, 0))],
        out_shape=[SDS((t, d), F32), SDS((8, d), F32)],
        args=(dz, dkv, wt, x, dx1, g_mix, sc1), sem=("arbitrary",), ride=ride)


def _to_lanes(v, rows=None):
    flat = v.reshape(-1)
    need = -(-flat.shape[0] // LANES)
    need = -(-need // 8) * 8 if rows is None else rows
    return jnp.pad(flat, (0, need * LANES - flat.shape[0])).reshape(need, LANES)


def kernel(x, c, w_ada, b_ada, g_mix, w_in, b_in, sinks, conv_w, w_out, g_ffn, w_ffn_in, w_ffn_out, g_final, loss_target, m_w_ada, m_b_ada, m_g_mix, m_w_in, m_b_in, m_sinks, m_conv_w, m_w_out, m_g_ffn, m_w_ffn_in, m_w_ffn_out, m_g_final, v_w_ada, v_b_ada, v_g_mix, v_w_in, v_b_in, v_sinks, v_conv_w, v_w_out, v_g_ffn, v_w_ffn_in, v_w_ffn_out, v_g_final):
    xs, tgt = x[0], loss_target[0]
    t, d = xs.shape
    zw = w_in.shape[2] * N_CHIP
    kvw2 = zw - 6 * d
    ff = w_ffn_out.shape[1] * N_CHIP
    n_mod = w_ada.shape[2] * N_CHIP // d
    mod_sh = w_ada.shape[2]
    cw_sh = conv_w.shape[2]
    assert d % (8 * LANES) == 0 and kvw2 == 2 * LANES and t % 512 == 0 and n_mod == 6
    xi, yi, ci = _mesh_pos()
    j_me = 2 * xi + yi
    b_me = 4 * xi + 2 * yi + ci
    pos = jnp.stack([ci, j_me]).astype(jnp.int32)
    tm = 512

    pack1 = jnp.concatenate([c.reshape(d // LANES, LANES), conv_w[0].reshape(-1, LANES)], axis=0)
    pack1 = jnp.pad(pack1, ((0, 16 - pack1.shape[0]), (0, 0)))
    g1 = _all_gather_small(pack1, "gather_c")
    c_all = g1[:, :d // LANES, :].reshape(N_DEV, d)
    cw_rows = 3 * cw_sh // LANES
    conv_w_full = jnp.concatenate(
        [g1[2 * j, d // LANES:d // LANES + cw_rows, :].reshape(3, cw_sh) for j in range(N_CHIP)], axis=1)
    b_ada_sh = lax.dynamic_slice(b_ada, (0, j_me * mod_sh), (1, mod_sh))
    mod_all = _all_gather_small(_ada_fwd(c_all, w_ada[0], b_ada_sh), "gather_mod")
    mod = jnp.concatenate([lax.dynamic_index_in_dim(mod_all[2 * j], b_me, 0, keepdims=True) for j in range(N_CHIP)],
                          axis=1)
    sh1, sc1, ga1, sh2, sc2, ga2 = [mod[:, k * d:(k + 1) * d] for k in range(6)]

    w_in_t, m_w_in_t, v_w_in_t = w_in[0].T, m_w_in[0].T, v_w_in[0].T
    (w_in_g,) = _gather_weights([_cast_into_block(pos, w_in_t, "cast_w_in")])
    w_in_tf = w_in_g.reshape(zw, d)
    later = [_cast_into_block(pos, w_out[0], "cast_w_out"), _cast_into_block(pos, w_ffn_in[0], "cast_w_ffn_in"),
             _cast_into_block(pos, w_ffn_out[0], "cast_w_ffn_out")]

    h1 = _prenorm(xs, g_mix, sc1, sh1, tm)
    z, later = _in_proj(h1, w_in_tf, b_in, min(t, 1024), zw // 5, ride=_x_gather_ici(later))
    later = _exchange(_x_gather_d2d(later), "gather_forward")
    w_out_f = later[0].reshape(d, d)
    w_ffn_in_f = later[1]
    w_ffn_out_f = later[2].reshape(ff, d)
    attn, merged = _mixer_fwd(z, sinks, conv_w_full, d)
    tml = min(t, 1024)
    y1, x1, h2 = _out_proj_fwd(merged, w_out_f, xs, ga1, g_ffn, sc2, sh2, tml)
    gu, act = _ffn_in_fwd(h2, w_ffn_in_f, ff, tml, ff // 2)
    dx2, dy2, st_loss = _ffn_out_loss(act, w_ffn_out_f, x1, tgt, ga2, g_final.reshape(1, d), tm)

    dgu = _ffn_out_bwd(dy2, w_ffn_out_f, gu, tml, ff // 2)
    tk = min(t, 2048)
    dw_ffn_out, _ = _wgrad(
        act, dy2, pl.BlockSpec((tk, ff // 2), lambda m, k: (k, m)), pl.BlockSpec((tk, d), lambda m, k: (k, 0)),
        pl.BlockSpec((ff // 2, d), lambda m, k: (m, 0)), SDS((ff, d), F32), (2, t // tk), "wgrad_ffn_out")
    dx1, dy1, st_ffn = _ffn_in_bwd(dgu, w_ffn_in_f, x1, dx2, y1, g_ffn, sc2, ga1, tm, ff // 2)
    dw_ffn_in, _ = _wgrad(
        h2, dgu, pl.BlockSpec((tk, d), lambda n, k: (k, 0)),
        pl.BlockSpec((None, tk, ff // 2), lambda n, k: (n // 2, k, n % 2)),
        pl.BlockSpec((None, d, ff // 2), lambda n, k: (n, 0, 0)), SDS((N_CHIP, d, ff // 2), F32),
        (N_CHIP, t // tk), "wgrad_ffn_in")
    dw_out, _ = _wgrad(
        merged, dy1, pl.BlockSpec((tk, d), lambda m, k: (k, 0)), pl.BlockSpec((tk, d), lambda m, k: (k, 0)),
        pl.BlockSpec((d, d), lambda m, k: (0, 0)), SDS((d, d), F32), (1, t // tk), "wgrad_out")

    early = [dw_out.reshape(N_CHIP, d // N_CHIP, d), dw_ffn_in, dw_ffn_out.reshape(N_CHIP, ff // N_CHIP, d)]
    early_names = ["w_out", "w_ffn_in", "w_ffn_out"]
    dmerged, from_sibling = _out_proj_bwd(dy1, w_out_f, tml, ride=_x_pair_send(early))
    chip32, chip16 = zip(*[_pair_add(pos, g, s, "pair_add_" + nm)
                           for g, s, nm in zip(early, from_sibling, early_names)])
    (dz, dkv, db_z, db_kv, dcw, dsk), from_chips = _mixer_bwd(z, dmerged, attn, sinks, conv_w_full, d,
                                                              ride=_x_chip_send(list(chip16)))
    fulls = [_chip_add(pos, p, r, "chip_add_" + nm) for p, r, nm in zip(chip32, from_chips, early_names)]
    dw_in_t, (g_w_out, g_w_ffn_in, g_w_ffn_out) = _wgrad_in(dz, dkv, h1, tk, ride=_x_pair_exchange(fulls))
    dw_in_t = dw_in_t.reshape(N_CHIP, zw // N_CHIP, d)

    (from_sibling,) = _exchange(_x_pair_send([dw_in_t]), "pair_send")
    chip32, chip16 = _pair_add(pos, dw_in_t, from_sibling, "pair_add_w_in")
    (grad_x, st_in), (from_chips,) = _in_proj_bwd(dz, dkv, w_in_tf, xs, dx1, g_mix, sc1, tm,
                                                  ride=_x_chip_send([chip16]))
    (g_w_in_t,) = _exchange(_x_pair_exchange([_chip_add(pos, chip32, from_chips, "chip_add_w_in")]), "pair_exchange")

    dmod = jnp.concatenate([st_in[0:1], st_in[1:2], st_ffn[3:4], st_ffn[0:1], st_ffn[1:2], st_loss[0:1]], axis=1)
    db_in = jnp.concatenate([db_z[0:1, :d], db_kv[0:1], db_z[0:1, d + kvw2:]], axis=1)
    seg = [dmod, st_in[2:3], db_in, dsk[0:1], dcw[0:3].reshape(1, 3 * d), st_ffn[2:3], st_loss[1:2],
           st_loss[2:3, :LANES]]
    sizes = [s.shape[1] for s in seg]
    pack2 = _to_lanes(jnp.concatenate(seg, axis=1))
    packs = _all_gather_small(pack2, "gather_small_grads")
    tot = _pack_sum(packs).reshape(-1)
    offs = [sum(sizes[:k]) for k in range(len(sizes))]
    gb_ada, gg_mix, gb_in, gsinks, gcw, gg_ffn, gg_final, loss_v = [tot[o:o + s] for o, s in zip(offs, sizes)]
    loss = loss_v[0]
    gsinks = gsinks[:sinks.shape[1]]
    gcw_sh = lax.dynamic_slice(gcw.reshape(3, d), (0, j_me * cw_sh), (3, cw_sh))

    dmod_all = packs[:, :n_mod * d // LANES, :].reshape(N_DEV, n_mod * d)
    g_w_ada = _ada_wgrad(c_all, lax.dynamic_slice(dmod_all, (0, j_me * mod_sh), (N_DEV, mod_sh)))

    out_g, out_d, out_m, out_v = {}, {}, {}, {}
    big = {"w_ada": (w_ada[0], g_w_ada, m_w_ada[0], v_w_ada[0]),
           "w_out": (w_out[0], g_w_out, m_w_out[0], v_w_out[0]),
           "w_ffn_in": (w_ffn_in[0], g_w_ffn_in, m_w_ffn_in[0], v_w_ffn_in[0]),
           "w_ffn_out": (w_ffn_out[0], g_w_ffn_out, m_w_ffn_out[0], v_w_ffn_out[0])}
    for nm, (w, g, m, v) in big.items():
        out_g[nm], out_d[nm], out_m[nm], out_v[nm] = [o[None] for o in _adamw(w, g, m, v, "adamw_" + nm)]
    out_g["w_in"], out_d["w_in"], out_m["w_in"], out_v["w_in"] = [
        o.T[None] for o in _adamw(w_in_t, g_w_in_t, m_w_in_t, v_w_in_t, "adamw_w_in")]
    small = {"b_ada": (b_ada, gb_ada, m_b_ada, v_b_ada), "g_mix": (g_mix, gg_mix, m_g_mix, v_g_mix),
             "b_in": (b_in, gb_in, m_b_in, v_b_in), "sinks": (sinks, gsinks, m_sinks, v_sinks),
             "conv_w": (conv_w, gcw_sh, m_conv_w, v_conv_w), "g_ffn": (g_ffn, gg_ffn, m_g_ffn, v_g_ffn),
             "g_final": (g_final, gg_final, m_g_final, v_g_final)}
    s_sizes = [w.size for w, _, _, _ in small.values()]
    s_rows = -(-sum(s_sizes) // LANES // 8) * 8

    def s_pack(k):
        return _to_lanes(jnp.concatenate([tup[k].reshape(-1) for tup in small.values()]), s_rows)

    s_out = _adamw(s_pack(0), s_pack(1), s_pack(2), s_pack(3), "adamw_small")
    s_off = 0
    for (nm, (w, g, _, _)), sz in zip(small.items(), s_sizes):
        out_g[nm] = g.reshape(w.shape)
        out_d[nm], out_m[nm], out_v[nm] = [o.reshape(-1)[s_off:s_off + sz].reshape(w.shape) for o in s_out[1:]]
        s_off += sz

    order = ["w_ada", "b_ada", "g_mix", "w_in", "b_in", "sinks", "conv_w", "w_out", "g_ffn", "w_ffn_in", "w_ffn_out",
             "g_final"]
    return (loss, grad_x[None], *[out_g[k] for k in order], *[out_d[k] for k in order],
            *[out_m[k] for k in order], *[out_v[k] for k in order])
```

```python
import functools

import jax
import jax.numpy as jnp
from jax import lax
from jax.experimental import pallas as pl
from jax.experimental.pallas import tpu as pltpu

F32 = jnp.float32
BF16 = jnp.bfloat16
EPS = 1e-6
HEAD_DIM = 64
GROUP = 8
BLOCK = 128
LANES = 128
SUBLANES_BF16 = 16
N_DEV = 8
N_CHIP = 4
VMEM_LIMIT = 56 * 1024 * 1024
MESH = pl.DeviceIdType.MESH

ADAM_LR = 0.001
ADAM_B1 = 0.9
ADAM_B2 = 0.999
ADAM_EPS = 1e-08
ADAM_WD = 0.01
ADAM_STEP = 10

SDS = jax.ShapeDtypeStruct
ANY = pl.BlockSpec(memory_space=pl.ANY)
VMEM_SPEC = pl.BlockSpec(memory_space=pltpu.VMEM)
SMEM_SPEC = pl.BlockSpec(memory_space=pltpu.SMEM)


def _params(*sem):
    return pltpu.CompilerParams(dimension_semantics=sem, vmem_limit_bytes=VMEM_LIMIT)


def _mesh_pos():
    return lax.axis_index("x"), lax.axis_index("y"), lax.axis_index("c")


def _row_tile(rows, cols, itemsize=4, budget=1 << 20, mult=8):
    best = None
    for t in range(mult, rows + 1, mult):
        if rows % t == 0 and t * cols * itemsize <= budget:
            best = t
    if best is None:
        best = rows
    return best


def _all_gather_small(v, name):
    rows, cols = v.shape

    def body(v_ref, out_ref, send_sems, recv_sems, local_sem):
        x, y, c = _mesh_pos()
        me = 4 * x + 2 * y + c
        mine = pltpu.make_async_copy(v_ref, out_ref.at[me], local_sem)
        mine.start()
        peers = []
        for k in range(1, N_DEV):
            px = 1 - x if k & 4 else x
            py = 1 - y if k & 2 else y
            pc = 1 - c if k & 1 else c
            peers.append((px, py, pc))

        def copy(k, block):
            return pltpu.make_async_remote_copy(
                src_ref=v_ref, dst_ref=out_ref.at[block], send_sem=send_sems.at[k], recv_sem=recv_sems.at[k],
                device_id=peers[k], device_id_type=MESH)

        sends = [copy(k, me) for k in range(N_DEV - 1)]
        for cp in sends:
            cp.start()
        for k, (px, py, pc) in enumerate(peers):
            copy(k, 4 * px + 2 * py + pc).wait_recv()
        for cp in sends:
            cp.wait_send()
        mine.wait()

    return pl.pallas_call(
        body, name=name,
        out_shape=SDS((N_DEV, rows, cols), v.dtype),
        in_specs=[VMEM_SPEC], out_specs=VMEM_SPEC,
        scratch_shapes=[pltpu.SemaphoreType.DMA((N_DEV - 1,)), pltpu.SemaphoreType.DMA((N_DEV - 1,)),
                        pltpu.SemaphoreType.DMA],
    )(v)


def _other_chips(x, y):
    return [(1 - x, y), (x, 1 - y), (1 - x, 1 - y)]


def _gather_weights(bufs):
    n_w = len(bufs)

    def body(*refs):
        outs = refs[n_w:2 * n_w]
        send_sems, recv_sems, fsend_sems, frecv_sems = refs[2 * n_w:]
        x, y, c = _mesh_pos()
        j_me = 2 * x + y
        chips = _other_chips(x, y)
        sibling = (x, y, 1 - c)

        def half_rows(w, which):
            half = outs[w].shape[1] // 2
            return pl.ds(pl.multiple_of(which * half, SUBLANES_BF16), half)

        def copy(w, p, block, rows, over_ici):
            sems = (send_sems, recv_sems) if over_ici else (fsend_sems, frecv_sems)
            return pltpu.make_async_remote_copy(
                src_ref=outs[w].at[block, rows], dst_ref=outs[w].at[block, rows],
                send_sem=sems[0].at[w * 3 + p], recv_sem=sems[1].at[w * 3 + p],
                device_id=(*chips[p], c) if over_ici else sibling, device_id_type=MESH)

        def block_of(p):
            return 2 * chips[p][0] + chips[p][1]

        sends = [copy(w, p, j_me, half_rows(w, c), True) for w in range(n_w) for p in range(3)]
        for cp in sends:
            cp.start()
        forwards = []
        for w in range(n_w):
            for p in range(3):
                copy(w, p, block_of(p), half_rows(w, c), True).wait_recv()
                fw = copy(w, p, block_of(p), half_rows(w, c), False)
                fw.start()
                forwards.append(fw)
        for w in range(n_w):
            for p in range(3):
                copy(w, p, block_of(p), half_rows(w, 1 - c), False).wait_recv()
        for cp in sends + forwards:
            cp.wait_send()

    return pl.pallas_call(
        body, name="gather_weights",
        out_shape=[SDS(b.shape, b.dtype) for b in bufs],
        in_specs=[ANY] * n_w, out_specs=[ANY] * n_w,
        input_output_aliases={w: w for w in range(n_w)},
        scratch_shapes=[pltpu.SemaphoreType.DMA((3 * n_w,)), pltpu.SemaphoreType.DMA((3 * n_w,)),
                        pltpu.SemaphoreType.DMA((3 * n_w,)), pltpu.SemaphoreType.DMA((3 * n_w,))],
    )(*bufs)


class _Exchange:
    def __init__(self, operands, out_shape, in_place, n_sems, copies):
        self.operands, self.out_shape, self.in_place, self.n_sems, self.copies = (
            list(operands), list(out_shape), in_place, n_sems, copies)

    def sems(self):
        return [pltpu.SemaphoreType.DMA((self.n_sems,)), pltpu.SemaphoreType.DMA((self.n_sems,))]


def _x_gather_ici(bufs):
    def copies(ins, outs, send_sems, recv_sems):
        x, y, c = _mesh_pos()
        chips = _other_chips(x, y)
        out = []
        for w in range(len(outs)):
            half = outs[w].shape[1] // 2
            rows = pl.ds(pl.multiple_of(c * half, SUBLANES_BF16), half)
            for p in range(3):
                out.append(pltpu.make_async_remote_copy(
                    src_ref=outs[w].at[2 * x + y, rows], dst_ref=outs[w].at[2 * x + y, rows],
                    send_sem=send_sems.at[w * 3 + p], recv_sem=recv_sems.at[w * 3 + p],
                    device_id=(*chips[p], c), device_id_type=MESH))
        return out

    return _Exchange(bufs, [SDS(b.shape, b.dtype) for b in bufs], True, 3 * len(bufs), copies)


def _x_gather_d2d(bufs):
    def copies(ins, outs, send_sems, recv_sems):
        x, y, c = _mesh_pos()
        chips = _other_chips(x, y)
        out = []
        for w in range(len(outs)):
            half = outs[w].shape[1] // 2
            rows = pl.ds(pl.multiple_of(c * half, SUBLANES_BF16), half)
            for p in range(3):
                block = 2 * chips[p][0] + chips[p][1]
                out.append(pltpu.make_async_remote_copy(
                    src_ref=outs[w].at[block, rows], dst_ref=outs[w].at[block, rows],
                    send_sem=send_sems.at[w * 3 + p], recv_sem=recv_sems.at[w * 3 + p],
                    device_id=(x, y, 1 - c), device_id_type=MESH))
        return out

    return _Exchange(bufs, [SDS(b.shape, b.dtype) for b in bufs], True, 3 * len(bufs), copies)


def _x_pair_send(grads):
    def copies(ins, outs, send_sems, recv_sems):
        x, y, c = _mesh_pos()
        out = []
        for w in range(len(ins)):
            half = ins[w].shape[1] // 2
            rows = pl.ds(pl.multiple_of((1 - c) * half, 8), half)
            out.append(pltpu.make_async_remote_copy(
                src_ref=ins[w].at[:, rows, :], dst_ref=outs[w], send_sem=send_sems.at[w], recv_sem=recv_sems.at[w],
                device_id=(x, y, 1 - c), device_id_type=MESH))
        return out

    return _Exchange(grads, [SDS((N_CHIP, g.shape[1] // 2, g.shape[2]), g.dtype) for g in grads], False,
                     len(grads), copies)


def _x_chip_send(partials):
    def copies(ins, outs, send_sems, recv_sems):
        x, y, c = _mesh_pos()
        chips = _other_chips(x, y)
        out = []
        for w in range(len(ins)):
            for p in range(3):
                out.append(pltpu.make_async_remote_copy(
                    src_ref=ins[w].at[2 * chips[p][0] + chips[p][1]], dst_ref=outs[w].at[p],
                    send_sem=send_sems.at[w * 3 + p], recv_sem=recv_sems.at[w * 3 + p],
                    device_id=(*chips[p], c), device_id_type=MESH))
        return out

    return _Exchange(partials, [SDS((3,) + p.shape[1:], p.dtype) for p in partials], False, 3 * len(partials),
                     copies)


def _x_pair_exchange(fulls):
    def copies(ins, outs, send_sems, recv_sems):
        x, y, c = _mesh_pos()
        out = []
        for w in range(len(outs)):
            half = outs[w].shape[0] // 2
            rows = pl.ds(pl.multiple_of(c * half, 8), half)
            out.append(pltpu.make_async_remote_copy(
                src_ref=outs[w].at[rows], dst_ref=outs[w].at[rows], send_sem=send_sems.at[w],
                recv_sem=recv_sems.at[w], device_id=(x, y, 1 - c), device_id_type=MESH))
        return out

    return _Exchange(fulls, [SDS(f.shape, f.dtype) for f in fulls], True, len(fulls), copies)


def _pallas(body, *, name, grid, in_specs, out_specs, out_shape, args, scratch=(), sem=None, ride=None):
    single = not isinstance(out_specs, (list, tuple))
    out_specs_l = [out_specs] if single else list(out_specs)
    out_shape_l = [out_shape] if single else list(out_shape)
    n_in, n_out, n_scr = len(in_specs), len(out_specs_l), len(scratch)
    if ride is None:
        res = pl.pallas_call(body, name=name, grid=grid, in_specs=list(in_specs), out_specs=out_specs,
                             out_shape=out_shape, scratch_shapes=list(scratch), compiler_params=_params(*sem))(*args)
        return res, None
    n_x, n_xo = len(ride.operands), len(ride.out_shape)

    def full_body(*refs):
        ins, x_ins = refs[:n_in], refs[n_in:n_in + n_x]
        outs = refs[n_in + n_x:n_in + n_x + n_out]
        x_outs = refs[n_in + n_x + n_out:n_in + n_x + n_out + n_xo]
        rest = refs[n_in + n_x + n_out + n_xo:]
        scr, (send_sems, recv_sems) = rest[:n_scr], rest[n_scr:]
        first = functools.reduce(jnp.logical_and, [pl.program_id(a) == 0 for a in range(len(grid))])
        last = functools.reduce(jnp.logical_and, [pl.program_id(a) == grid[a] - 1 for a in range(len(grid))])

        @pl.when(first)
        def _():
            for cp in ride.copies(x_ins, x_outs, send_sems, recv_sems):
                cp.start()

        body(*ins, *outs, *scr)

        @pl.when(last)
        def _():
            for cp in ride.copies(x_ins, x_outs, send_sems, recv_sems):
                cp.wait()

    res = pl.pallas_call(
        full_body, name=name, grid=grid, in_specs=list(in_specs) + [ANY] * n_x,
        out_specs=out_specs_l + [ANY] * n_xo, out_shape=out_shape_l + ride.out_shape,
        input_output_aliases={n_in + k: n_out + k for k in range(n_x)} if ride.in_place else {},
        scratch_shapes=list(scratch) + ride.sems(),
        compiler_params=_params(*(["arbitrary"] * len(grid))))(*args, *ride.operands)
    own = res[0] if single else list(res[:n_out])
    return own, list(res[n_out:])


def _exchange(ride, name):
    n_x, n_xo = len(ride.operands), len(ride.out_shape)

    def body(*refs):
        x_ins, x_outs = refs[:n_x], refs[n_x:n_x + n_xo]
        send_sems, recv_sems = refs[n_x + n_xo:]
        copies = ride.copies(x_ins, x_outs, send_sems, recv_sems)
        for cp in copies:
            cp.start()
        for cp in copies:
            cp.wait()

    return pl.pallas_call(
        body, name=name, in_specs=[ANY] * n_x, out_specs=[ANY] * n_xo, out_shape=ride.out_shape,
        input_output_aliases={k: k for k in range(n_x)} if ride.in_place else {},
        scratch_shapes=ride.sems())(*ride.operands)


def _cast_into_block(pos, w, name):
    rows, cols = w.shape
    tr = _row_tile(rows, cols, mult=SUBLANES_BF16)

    def body(pos_ref, w_ref, o_ref):
        del pos_ref
        o_ref[...] = w_ref[...].astype(BF16)

    return pl.pallas_call(
        body, name=name,
        grid_spec=pltpu.PrefetchScalarGridSpec(
            num_scalar_prefetch=1, grid=(rows // tr,),
            in_specs=[pl.BlockSpec((tr, cols), lambda i, pos_ref: (i, 0))],
            out_specs=pl.BlockSpec((None, tr, cols), lambda i, pos_ref: (pos_ref[1], i, 0))),
        out_shape=SDS((N_CHIP, rows, cols), BF16), compiler_params=_params("parallel"))(pos, w)


def _pair_add(pos, grad, from_sibling, name):
    _, rows, cols = grad.shape
    half = rows // 2
    tr = _row_tile(half, cols, mult=SUBLANES_BF16)
    nblk = half // tr

    def body(pos_ref, g_ref, s_ref, o32_ref, o16_ref):
        del pos_ref
        s = g_ref[...] + s_ref[...]
        o32_ref[...] = s
        o16_ref[...] = s.astype(BF16)

    spec = pl.BlockSpec((None, tr, cols), lambda j, i, pos_ref: (j, i, 0))
    return pl.pallas_call(
        body, name=name,
        grid_spec=pltpu.PrefetchScalarGridSpec(
            num_scalar_prefetch=1, grid=(N_CHIP, nblk),
            in_specs=[pl.BlockSpec((None, tr, cols), lambda j, i, pos_ref: (j, pos_ref[0] * nblk + i, 0)), spec],
            out_specs=[spec, spec]),
        out_shape=[SDS((N_CHIP, half, cols), F32), SDS((N_CHIP, half, cols), BF16)],
        compiler_params=_params("parallel", "parallel"),
    )(pos, grad, from_sibling)


def _chip_add(pos, partial32, from_chips, name):
    _, half, cols = partial32.shape
    tr = _row_tile(half, cols, mult=SUBLANES_BF16)

    def body(pos_ref, p_ref, r_ref, o_ref):
        del pos_ref
        acc = p_ref[...]
        for p in range(3):
            acc = acc + r_ref[p].astype(F32)
        o_ref[...] = acc

    return pl.pallas_call(
        body, name=name,
        grid_spec=pltpu.PrefetchScalarGridSpec(
            num_scalar_prefetch=1, grid=(half // tr,),
            in_specs=[pl.BlockSpec((None, tr, cols), lambda i, pos_ref: (pos_ref[1], i, 0)),
                      pl.BlockSpec((3, tr, cols), lambda i, pos_ref: (0, i, 0))],
            out_specs=pl.BlockSpec((tr, cols), lambda i, pos_ref: (pos_ref[0] * (half // tr) + i, 0))),
        out_shape=SDS((2 * half, cols), F32),
        compiler_params=_params("parallel"),
    )(pos, partial32, from_chips)


def _adamw(w, g, m, v, name):
    rows, cols = w.shape
    tr = _row_tile(rows, cols)

    def body(w_ref, g_ref, m_ref, v_ref, go_ref, d_ref, nm_ref, nv_ref):
        gg = g_ref[...]
        go_ref[...] = gg
        nm = ADAM_B1 * m_ref[...] + (1.0 - ADAM_B1) * gg
        nv = ADAM_B2 * v_ref[...] + (1.0 - ADAM_B2) * (gg * gg)
        m_hat = nm / (1.0 - ADAM_B1 ** ADAM_STEP)
        v_hat = nv / (1.0 - ADAM_B2 ** ADAM_STEP)
        d_ref[...] = -ADAM_LR * (m_hat / (jnp.sqrt(v_hat) + ADAM_EPS) + ADAM_WD * w_ref[...])
        nm_ref[...] = nm
        nv_ref[...] = nv

    spec = pl.BlockSpec((tr, cols), lambda i: (i, 0))
    return pl.pallas_call(body, name=name, grid=(rows // tr,), in_specs=[spec] * 4, out_specs=[spec] * 4,
                          out_shape=[SDS((rows, cols), F32)] * 4, compiler_params=_params("parallel"))(w, g, m, v)


def _pack_sum(gathered):
    _, rows, cols = gathered.shape

    def body(g_ref, o_ref):
        acc = g_ref[0]
        for d in range(1, N_DEV):
            acc = acc + g_ref[d]
        o_ref[...] = acc

    return pl.pallas_call(body, name="pack_sum", in_specs=[VMEM_SPEC], out_specs=VMEM_SPEC,
                          out_shape=SDS((rows, cols), F32))(gathered)


def _ada_fwd(c_all, w_sh, b_sh):
    d, n = w_sh.shape
    tn = 512

    def body(c_ref, w_ref, b_ref, o_ref):
        cc = c_ref[...]
        s = (cc * jax.nn.sigmoid(cc)).astype(BF16)
        o_ref[...] = jnp.dot(s, w_ref[...].astype(BF16), preferred_element_type=F32) + b_ref[...]

    return pl.pallas_call(
        body, name="ada_fwd", grid=(n // tn,),
        in_specs=[pl.BlockSpec((N_DEV, d), lambda j: (0, 0)), pl.BlockSpec((d, tn), lambda j: (0, j)),
                  pl.BlockSpec((1, tn), lambda j: (0, j))],
        out_specs=pl.BlockSpec((N_DEV, tn), lambda j: (0, j)),
        out_shape=SDS((N_DEV, n), F32), compiler_params=_params("parallel"))(c_all, w_sh, b_sh)


def _ada_wgrad(c_all, dmod_sh):
    d = c_all.shape[1]
    n = dmod_sh.shape[1]
    tn = 512

    def body(c_ref, g_ref, o_ref):
        cc = c_ref[...]
        s = cc * jax.nn.sigmoid(cc)
        o_ref[...] = lax.dot_general(s, g_ref[...], (((0,), (0,)), ((), ())), preferred_element_type=F32,
                                     precision=lax.Precision.HIGHEST)

    return pl.pallas_call(
        body, name="ada_wgrad", grid=(n // tn,),
        in_specs=[pl.BlockSpec((N_DEV, d), lambda j: (0, 0)), pl.BlockSpec((N_DEV, tn), lambda j: (0, j))],
        out_specs=pl.BlockSpec((d, tn), lambda j: (0, j)),
        out_shape=SDS((d, n), F32), compiler_params=_params("parallel"))(c_all, dmod_sh)


def _rms(xf):
    return lax.rsqrt(jnp.mean(xf * xf, axis=-1, keepdims=True) + EPS)


def _in_proj(x, g, sc, sh, wt, b, tm, tn, ride=None):
    t, d = x.shape
    n = wt.shape[0]

    def body(x_ref, g_ref, sc_ref, sh_ref, w_ref, b_ref, z_ref, h_ref):
        @pl.when(pl.program_id(1) == 0)
        def _():
            xf = x_ref[...]
            h_ref[...] = ((xf * _rms(xf) * g_ref[...]) * (1.0 + sc_ref[...]) + sh_ref[...]).astype(BF16)

        acc = lax.dot_general(h_ref[...], w_ref[...], (((1,), (1,)), ((), ())), preferred_element_type=F32)
        z_ref[...] = (acc + b_ref[...]).astype(BF16)

    row = pl.BlockSpec((tm, d), lambda i, j: (i, 0))
    vec = pl.BlockSpec((1, d), lambda i, j: (0, 0))
    return _pallas(
        body, name="in_proj", grid=(t // tm, n // tn),
        in_specs=[row, vec, vec, vec, pl.BlockSpec((tn, d), lambda i, j: (j, 0)),
                  pl.BlockSpec((1, tn), lambda i, j: (0, j))],
        out_specs=[pl.BlockSpec((tm, tn), lambda i, j: (i, j)), row],
        out_shape=[SDS((t, n), BF16), SDS((t, d), BF16)], args=(x, g, sc, sh, wt, b),
        sem=("parallel", "arbitrary"), ride=ride)


def _segments(d, kvw2):
    o = d + kvw2
    names = ("cb", "cc", "cx", "ga", "gc")
    seg = {nm: slice(o + k * d, o + (k + 1) * d) for k, nm in enumerate(names)}
    seg["q"], seg["kv"] = slice(0, d), slice(d, o)
    return seg


def _attn_masks():
    rows = 4 * BLOCK
    r = lax.broadcasted_iota(jnp.int32, (rows, 2 * BLOCK), 0) & (BLOCK - 1)
    col = lax.broadcasted_iota(jnp.int32, (rows, 2 * BLOCK), 1)
    return (col > r) & (col <= r + BLOCK), col


def _kv_variants(kv, n_kv_w):
    assert n_kv_w == LANES
    kb, vb = kv[:, :LANES] * (HEAD_DIM ** -0.5), kv[:, LANES:]
    kr, vr = pltpu.roll(kb, HEAD_DIM, 1), pltpu.roll(vb, HEAD_DIM, 1)
    lane = lax.broadcasted_iota(jnp.int32, kb.shape, 1)
    lo = lane < HEAD_DIM
    zero = jnp.zeros_like(kb)
    k_eff = [[None, None], [None, None]]
    v_eff = [[None, None], [None, None]]
    for h in range(2):
        for e in range(2):
            ksrc, vsrc = (kb, vb) if e == h else (kr, vr)
            keep = lo if e == 0 else jnp.logical_not(lo)
            k_eff[h][e] = jnp.where(keep, ksrc, zero)
            v_eff[h][e] = jnp.where(keep, vsrc, zero)
    return k_eff, v_eff


def _sink_column(sinks_ref, h, e):
    rowblk = lax.broadcasted_iota(jnp.int32, (4 * BLOCK, 1), 0) // BLOCK
    col = jnp.zeros((4 * BLOCK, 1), F32)
    for j in range(4):
        col = jnp.where(rowblk == j, sinks_ref[0, GROUP * h + 2 * j + e], col)
    return col


def _softmax_sink(s, valid, sink):
    s = jnp.where(valid, s, -jnp.inf)
    m = jnp.maximum(jnp.max(s, axis=-1, keepdims=True), sink)
    p = jnp.exp(s - m)
    psink = jnp.exp(sink - m)
    den = jnp.sum(p, axis=-1, keepdims=True) + psink
    inv = 1.0 / den
    return p * inv, psink * inv


def _shift_down(a, s, prev):
    rows = a.shape[0]
    out = pltpu.roll(a, s, 0)
    row = lax.broadcasted_iota(jnp.int32, a.shape, 0)
    for t in range(s):
        out = jnp.where(row == t, prev[SUBLANES_BF16 - s + t:SUBLANES_BF16 - s + t + 1, :], out)
    del rows
    return out


def _shift_up(a, s, nxt):
    rows = a.shape[0]
    out = pltpu.roll(a, rows - s, 0)
    row = lax.broadcasted_iota(jnp.int32, a.shape, 0)
    for t in range(s):
        out = jnp.where(row == rows - s + t, nxt[t:t + 1, :], out)
    return out


def _stack_pairs(ref, h):
    return jnp.concatenate([ref[:, (4 * h + j) * LANES:(4 * h + j + 1) * LANES] for j in range(4)], axis=0)


def _mixer_fwd(z, sinks, conv_w, d, ride=None):
    t, zw = z.shape
    kvw2 = zw - 6 * d
    nb = t // BLOCK
    halo = BLOCK // SUBLANES_BF16
    seg = _segments(d, kvw2)

    def body(z_ref, kvp_ref, prev_ref, sinks_ref, cw_ref, attn_ref, merged_ref):
        n = pl.program_id(0)
        kv = jnp.concatenate([kvp_ref[...], z_ref[:, seg["kv"]]], axis=0)
        k_eff, v_eff = _kv_variants(kv, kvw2 // 2)
        band, col = _attn_masks()
        valid = band & ((n > 0) | (col >= BLOCK))
        for h in range(2):
            q4 = _stack_pairs(z_ref, h)
            o4 = jnp.zeros((4 * BLOCK, LANES), F32)
            for e in range(2):
                s = lax.dot_general(q4, k_eff[h][e], (((1,), (1,)), ((), ())), preferred_element_type=F32)
                p, _ = _softmax_sink(s, valid, _sink_column(sinks_ref, h, e))
                o4 = o4 + jnp.dot(p.astype(BF16), v_eff[h][e], preferred_element_type=F32)
            for j in range(4):
                attn_ref[:, (4 * h + j) * LANES:(4 * h + j + 1) * LANES] = o4[j * BLOCK:(j + 1) * BLOCK].astype(BF16)
        cb = z_ref[:, seg["cb"]].astype(F32)
        p_in = z_ref[:, seg["cc"]].astype(F32) * z_ref[:, seg["cx"]].astype(F32)
        prev = jnp.where(n > 0, prev_ref[:, seg["cc"]].astype(F32) * prev_ref[:, seg["cx"]].astype(F32), 0.0)
        cconv = (cw_ref[0:1, :] * _shift_down(p_in, 2, prev) + cw_ref[1:2, :] * _shift_down(p_in, 1, prev)
                 + cw_ref[2:3, :] * p_in)
        sa = jax.nn.sigmoid(z_ref[:, seg["ga"]].astype(F32))
        sg = jax.nn.sigmoid(z_ref[:, seg["gc"]].astype(F32))
        merged_ref[...] = (sa * attn_ref[...].astype(F32) + sg * (cb * cconv)).astype(BF16)

    blk = pl.BlockSpec((BLOCK, d), lambda n: (n, 0))
    return _pallas(
        body, name="mixer_fwd", grid=(nb,),
        in_specs=[pl.BlockSpec((BLOCK, zw), lambda n: (n, 0)),
                  pl.BlockSpec((BLOCK, kvw2), lambda n: (jnp.maximum(n - 1, 0), d // kvw2)),
                  pl.BlockSpec((SUBLANES_BF16, zw), lambda n: (jnp.maximum(n * halo - 1, 0), 0)),
                  SMEM_SPEC, pl.BlockSpec((3, d), lambda n: (0, 0))],
        out_specs=[blk, blk],
        out_shape=[SDS((t, d), BF16), SDS((t, d), BF16)],
        args=(z, z, z, sinks, conv_w), sem=("parallel",), ride=ride)


def _out_proj_fwd(merged, w_out, x, ga1, g_ffn, sc2, sh2, tm):
    t, d = x.shape

    def body(m_ref, w_ref, x_ref, ga_ref, g_ref, sc_ref, sh_ref, y_ref, x1_ref, h_ref):
        y = jnp.dot(m_ref[...], w_ref[...], preferred_element_type=F32)
        x1 = x_ref[...] + ga_ref[...] * y
        y_ref[...] = y.astype(BF16)
        x1_ref[...] = x1
        h_ref[...] = ((x1 * _rms(x1) * g_ref[...]) * (1.0 + sc_ref[...]) + sh_ref[...]).astype(BF16)

    row = pl.BlockSpec((tm, d), lambda i: (i, 0))
    vec = pl.BlockSpec((1, d), lambda i: (0, 0))
    return pl.pallas_call(
        body, name="out_proj_fwd", grid=(t // tm,),
        in_specs=[row, pl.BlockSpec((d, d), lambda i: (0, 0)), row, vec, vec, vec, vec],
        out_specs=[row, row, row],
        out_shape=[SDS((t, d), BF16), SDS((t, d), F32), SDS((t, d), BF16)],
        compiler_params=_params("parallel"))(merged, w_out, x, ga1, g_ffn, sc2, sh2)


def _ffn_in_fwd(h2, w, ff, tm, tn):
    t, d = h2.shape
    nj = ff // tn
    assert w.shape == (2 * nj, d, tn)

    def body(h_ref, wg_ref, wu_ref, gu_ref, act_ref):
        hh = h_ref[...]
        g = jnp.dot(hh, wg_ref[...], preferred_element_type=F32)
        u = jnp.dot(hh, wu_ref[...], preferred_element_type=F32)
        gu_ref[0] = g.astype(BF16)
        gu_ref[1] = u.astype(BF16)
        act_ref[...] = ((g * jax.nn.sigmoid(g)) * u).astype(BF16)

    return pl.pallas_call(
        body, name="ffn_in_fwd", grid=(nj, t // tm),
        in_specs=[pl.BlockSpec((tm, d), lambda j, i: (i, 0)), pl.BlockSpec((None, d, tn), lambda j, i: (j, 0, 0)),
                  pl.BlockSpec((None, d, tn), lambda j, i: (j + nj, 0, 0))],
        out_specs=[pl.BlockSpec((2, tm, tn), lambda j, i: (0, i, j)), pl.BlockSpec((tm, tn), lambda j, i: (i, j))],
        out_shape=[SDS((2, t, ff), BF16), SDS((t, ff), BF16)],
        compiler_params=_params("parallel", "parallel"))(h2, w, w)


def _ffn_out_loss(act, w, x1, target, ga2, g_final, tm):
    t, d = x1.shape
    ff = act.shape[1]

    def body(a_ref, w_ref, x1_ref, tg_ref, ga_ref, gf_ref, dx2_ref, dy2_ref, st_ref):
        @pl.when(pl.program_id(0) == 0)
        def _():
            st_ref[...] = jnp.zeros_like(st_ref)

        halves = [slice(k * (tm // 2), (k + 1) * (tm // 2)) for k in range(2)]
        y2s = [jnp.dot(a_ref[rows, :], w_ref[...], preferred_element_type=F32) for rows in halves]
        for rows, y2 in zip(halves, y2s):
            x2 = x1_ref[rows, :] + ga_ref[...] * y2
            r = _rms(x2)
            yn = x2 * r
            err = yn * gf_ref[...] - tg_ref[rows, :]
            loss = 0.5 * jnp.sum(jnp.mean(err * err, axis=-1, keepdims=True), axis=0, keepdims=True)
            dy = err * (1.0 / d)
            u = dy * gf_ref[...]
            dx2 = r * (u - yn * jnp.mean(u * yn, axis=-1, keepdims=True))
            dx2_ref[rows, :] = dx2
            dy2_ref[rows, :] = (ga_ref[...] * dx2).astype(BF16)
            st_ref[0:1, :] += jnp.sum(dx2 * y2, axis=0, keepdims=True)
            st_ref[1:2, :] += jnp.sum(dy * yn, axis=0, keepdims=True)
            st_ref[2:3, :] += jnp.broadcast_to(loss, (1, d))

    row = pl.BlockSpec((tm, d), lambda i: (i, 0))
    vec = pl.BlockSpec((1, d), lambda i: (0, 0))
    return pl.pallas_call(
        body, name="ffn_out_loss", grid=(t // tm,),
        in_specs=[pl.BlockSpec((tm, ff), lambda i: (i, 0)),
                  pl.BlockSpec((ff, d), lambda i: (0, 0), pipeline_mode=pl.Buffered(1)), row, row, vec, vec],
        out_specs=[row, row, pl.BlockSpec((8, d), lambda i: (0, 0))],
        out_shape=[SDS((t, d), F32), SDS((t, d), BF16), SDS((8, d), F32)],
        compiler_params=_params("arbitrary"))(act, w, x1, target, ga2, g_final)


def _ffn_out_bwd(dy2, w, gu, tm, tn):
    t, d = dy2.shape
    ff = w.shape[0]

    def body(dy_ref, w_ref, gu_ref, o_ref):
        dy = dy_ref[...]
        for lo in range(0, tn, 3 * LANES):
            cols = slice(lo, min(lo + 3 * LANES, tn))
            dact = lax.dot_general(dy, w_ref[cols, :], (((1,), (1,)), ((), ())), preferred_element_type=F32)
            g = gu_ref[0, :, cols].astype(F32)
            u = gu_ref[1, :, cols].astype(F32)
            sg = jax.nn.sigmoid(g)
            a = dact * sg
            du = a * g
            o_ref[0, :, cols] = (u * (a + du * (1.0 - sg))).astype(BF16)
            o_ref[1, :, cols] = du.astype(BF16)

    gu_spec = pl.BlockSpec((2, tm, tn), lambda j, i: (0, i, j))
    return pl.pallas_call(
        body, name="ffn_out_bwd", grid=(ff // tn, t // tm),
        in_specs=[pl.BlockSpec((tm, d), lambda j, i: (i, 0)), pl.BlockSpec((tn, d), lambda j, i: (j, 0)), gu_spec],
        out_specs=gu_spec, out_shape=SDS((2, t, ff), BF16),
        compiler_params=_params("parallel", "parallel"))(dy2, w, gu)


def _wgrad(a, b, a_spec, b_spec, out_spec, out_shape, grid, name, ride=None):
    def body(a_ref, b_ref, o_ref):
        @pl.when(pl.program_id(len(grid) - 1) == 0)
        def _():
            o_ref[...] = jnp.zeros_like(o_ref)

        o_ref[...] += lax.dot_general(a_ref[...], b_ref[...], (((0,), (0,)), ((), ())), preferred_element_type=F32)

    return _pallas(
        body, name=name, grid=grid, in_specs=[a_spec, b_spec], out_specs=out_spec, out_shape=out_shape, args=(a, b),
        sem=["parallel"] * (len(grid) - 1) + ["arbitrary"], ride=ride)


def _ffn_in_bwd(dgu, w, x1, dx2, y1, g_ffn, sc2, ga1, tm, tk):
    t, d = x1.shape
    ff = dgu.shape[2]
    n_sh, _, sw = w.shape
    per = ff // sw
    del tk
    nt = (((1,), (1,)), ((), ()))

    def body(a_ref, w_ref, x1_ref, dx2_ref, y1_ref, g_ref, sc_ref, ga_ref, dx1_ref, dy1_ref, st_ref):
        @pl.when(pl.program_id(0) == 0)
        def _():
            st_ref[...] = jnp.zeros_like(st_ref)

        dh = None
        for j in range(n_sh):
            part = lax.dot_general(a_ref[j // per, :, (j % per) * sw:(j % per + 1) * sw], w_ref[j], nt,
                                   preferred_element_type=F32)
            dh = part if dh is None else dh + part
        x1 = x1_ref[...]
        r = _rms(x1)
        xn = x1 * r
        g = g_ref[...]
        dn = dh * (1.0 + sc_ref[...])
        u = dn * g
        dx1 = dx2_ref[...] + r * (u - xn * jnp.mean(u * xn, axis=-1, keepdims=True))
        dx1_ref[...] = dx1
        dy1_ref[...] = (ga_ref[...] * dx1).astype(BF16)
        st_ref[0:1, :] += jnp.sum(dh, axis=0, keepdims=True)
        st_ref[1:2, :] += jnp.sum(dh * (xn * g), axis=0, keepdims=True)
        st_ref[2:3, :] += jnp.sum(dn * xn, axis=0, keepdims=True)
        st_ref[3:4, :] += jnp.sum(dx1 * y1_ref[...].astype(F32), axis=0, keepdims=True)

    row = pl.BlockSpec((tm, d), lambda i: (i, 0))
    vec = pl.BlockSpec((1, d), lambda i: (0, 0))
    return pl.pallas_call(
        body, name="ffn_in_bwd", grid=(t // tm,),
        in_specs=[pl.BlockSpec((2, tm, ff), lambda i: (0, i, 0)),
                  pl.BlockSpec((n_sh, d, sw), lambda i: (0, 0, 0), pipeline_mode=pl.Buffered(1)),
                  row, row, row, vec, vec, vec],
        out_specs=[row, row, pl.BlockSpec((8, d), lambda i: (0, 0))],
        out_shape=[SDS((t, d), F32), SDS((t, d), BF16), SDS((8, d), F32)],
        compiler_params=_params("arbitrary"))(dgu, w, x1, dx2, y1, g_ffn, sc2, ga1)


def _out_proj_bwd(dy1, w_out, tm, ride=None):
    t, d = dy1.shape

    def body(dy_ref, w_ref, o_ref):
        o_ref[...] = lax.dot_general(dy_ref[...], w_ref[...], (((1,), (1,)), ((), ())),
                                     preferred_element_type=F32).astype(BF16)

    row = pl.BlockSpec((tm, d), lambda i: (i, 0))
    return _pallas(body, name="out_proj_bwd", grid=(t // tm,),
                   in_specs=[row, pl.BlockSpec((d, d), lambda i: (0, 0))], out_specs=row,
                   out_shape=SDS((t, d), BF16), args=(dy1, w_out), sem=("parallel",), ride=ride)


def _mixer_bwd(z, dmerged, attn, sinks, conv_w, d, ride=None):
    t, zw = z.shape
    kvw2 = zw - 6 * d
    nb = t // BLOCK
    halo = BLOCK // SUBLANES_BF16
    last_halo = t // SUBLANES_BF16 - 1
    scale = HEAD_DIM ** -0.5
    seg = _segments(d, kvw2)

    def body(z_ref, kvp_ref, prev_ref, next_ref, dm_ref, dmn_ref, attn_ref, sinks_ref, cw_ref,
             dz_ref, dkv_ref, db_ref, dbkv_ref, dcw_ref, dsk_ref, carry_ref):
        n = pl.program_id(0)

        @pl.when(n == 0)
        def _():
            carry_ref[...] = jnp.zeros_like(carry_ref)
            db_ref[...] = jnp.zeros_like(db_ref)
            dbkv_ref[...] = jnp.zeros_like(dbkv_ref)
            dcw_ref[...] = jnp.zeros_like(dcw_ref)
            dsk_ref[...] = jnp.zeros_like(dsk_ref)

        @pl.when(n < nb)
        def _():
            dm = dm_ref[...].astype(F32)
            sa = jax.nn.sigmoid(z_ref[:, seg["ga"]].astype(F32))
            dga = dm * attn_ref[...].astype(F32) * sa * (1.0 - sa)
            dz_ref[:, seg["ga"]] = dga.astype(BF16)
            db_ref[0:1, seg["ga"]] += jnp.sum(dga, axis=0, keepdims=True)
            dattn = (dm * sa).astype(BF16)

            kv = jnp.concatenate([kvp_ref[...], z_ref[:, seg["kv"]]], axis=0)
            k_eff, v_eff = _kv_variants(kv, kvw2 // 2)
            band, col = _attn_masks()
            valid = band & ((n > 0) | (col >= BLOCK))
            lane_lo = lax.broadcasted_iota(jnp.int32, (2 * BLOCK, LANES), 1) < HEAD_DIM
            sink_lane = lax.broadcasted_iota(jnp.int32, (1, LANES), 1)
            rowblk = lax.broadcasted_iota(jnp.int32, (4 * BLOCK, 1), 0) // BLOCK
            dk_acc = [jnp.zeros((2 * BLOCK, LANES), F32), jnp.zeros((2 * BLOCK, LANES), F32)]
            dv_acc = [jnp.zeros((2 * BLOCK, LANES), F32), jnp.zeros((2 * BLOCK, LANES), F32)]
            dsink = jnp.zeros((1, LANES), F32)
            for h in range(2):
                q4 = _stack_pairs(z_ref, h)
                do4 = jnp.concatenate([dattn[:, (4 * h + j) * LANES:(4 * h + j + 1) * LANES] for j in range(4)],
                                      axis=0)
                dq4 = jnp.zeros((4 * BLOCK, LANES), F32)
                for e in range(2):
                    s = lax.dot_general(q4, k_eff[h][e], (((1,), (1,)), ((), ())), preferred_element_type=F32)
                    p, psink = _softmax_sink(s, valid, _sink_column(sinks_ref, h, e))
                    dp = lax.dot_general(do4, v_eff[h][e], (((1,), (1,)), ((), ())), preferred_element_type=F32)
                    delta = jnp.sum(p * dp, axis=-1, keepdims=True)
                    ds = (p * (dp - delta)).astype(BF16)
                    dq4 = dq4 + jnp.dot(ds, k_eff[h][e], preferred_element_type=F32)
                    dk = lax.dot_general(q4, ds, (((0,), (0,)), ((), ())), preferred_element_type=F32).T
                    dv = lax.dot_general(do4, p.astype(BF16), (((0,), (0,)), ((), ())), preferred_element_type=F32).T
                    keep = lane_lo if e == 0 else jnp.logical_not(lane_lo)
                    slot = 0 if e == h else 1
                    dk_acc[slot] = dk_acc[slot] + jnp.where(keep, dk, 0.0)
                    dv_acc[slot] = dv_acc[slot] + jnp.where(keep, dv, 0.0)
                    dsk = -(psink * delta)
                    for j in range(4):
                        tot = jnp.sum(jnp.where(rowblk == j, dsk, 0.0), axis=0, keepdims=True)
                        dsink = dsink + jnp.where(sink_lane == GROUP * h + 2 * j + e, tot, 0.0)
                for j in range(4):
                    cols = slice((4 * h + j) * LANES, (4 * h + j + 1) * LANES)
                    dqj = dq4[j * BLOCK:(j + 1) * BLOCK]
                    dz_ref[:, cols] = dqj.astype(BF16)
                    db_ref[0:1, cols] += jnp.sum(dqj, axis=0, keepdims=True)
            dsk_ref[0:1, :] += dsink
            dkv_new = jnp.concatenate([(dk_acc[0] + pltpu.roll(dk_acc[1], HEAD_DIM, 1)) * scale,
                                       dv_acc[0] + pltpu.roll(dv_acc[1], HEAD_DIM, 1)], axis=1)
            done = carry_ref[...] + dkv_new[:BLOCK]
            dkv_ref[...] = done.astype(BF16)
            dbkv_ref[0:1, :] += jnp.sum(done, axis=0, keepdims=True)
            carry_ref[...] = dkv_new[BLOCK:]

            cb = z_ref[:, seg["cb"]].astype(F32)
            cc = z_ref[:, seg["cc"]].astype(F32)
            cx = z_ref[:, seg["cx"]].astype(F32)
            sg = jax.nn.sigmoid(z_ref[:, seg["gc"]].astype(F32))
            p_in = cc * cx
            prev = jnp.where(n > 0, prev_ref[:, seg["cc"]].astype(F32) * prev_ref[:, seg["cx"]].astype(F32), 0.0)
            p_m1 = _shift_down(p_in, 1, prev)
            p_m2 = _shift_down(p_in, 2, prev)
            w0, w1, w2 = cw_ref[0:1, :], cw_ref[1:2, :], cw_ref[2:3, :]
            cconv = w0 * p_m2 + w1 * p_m1 + w2 * p_in
            dconv = dm * sg
            dgc = dm * (cb * cconv) * sg * (1.0 - sg)
            dcb = dconv * cconv
            dcc_t = dconv * cb
            nxt = jnp.where(n < nb - 1, dmn_ref[...].astype(F32) * jax.nn.sigmoid(next_ref[:, seg["gc"]].astype(F32))
                            * next_ref[:, seg["cb"]].astype(F32), 0.0)
            dpin = w2 * dcc_t + w1 * _shift_up(dcc_t, 1, nxt) + w0 * _shift_up(dcc_t, 2, nxt)
            for nm, val in (("cb", dcb), ("cc", dpin * cx), ("cx", dpin * cc), ("gc", dgc)):
                dz_ref[:, seg[nm]] = val.astype(BF16)
                db_ref[0:1, seg[nm]] += jnp.sum(val, axis=0, keepdims=True)
            dcw_ref[0:1, :] += jnp.sum(dcc_t * p_m2, axis=0, keepdims=True)
            dcw_ref[1:2, :] += jnp.sum(dcc_t * p_m1, axis=0, keepdims=True)
            dcw_ref[2:3, :] += jnp.sum(dcc_t * p_in, axis=0, keepdims=True)

        @pl.when(n == nb)
        def _():
            done = carry_ref[...]
            dkv_ref[...] = done.astype(BF16)
            dbkv_ref[0:1, :] += jnp.sum(done, axis=0, keepdims=True)

    def cur(n):
        return jnp.minimum(n, nb - 1)

    blk = pl.BlockSpec((BLOCK, d), lambda n: (cur(n), 0))
    return _pallas(
        body, name="mixer_bwd", grid=(nb + 1,), ride=ride, sem=("arbitrary",),
        args=(z, z, z, z, dmerged, dmerged, attn, sinks, conv_w),
        in_specs=[pl.BlockSpec((BLOCK, zw), lambda n: (cur(n), 0)),
                  pl.BlockSpec((BLOCK, kvw2), lambda n: (jnp.maximum(cur(n) - 1, 0), d // kvw2)),
                  pl.BlockSpec((SUBLANES_BF16, zw), lambda n: (jnp.maximum(cur(n) * halo - 1, 0), 0)),
                  pl.BlockSpec((SUBLANES_BF16, zw), lambda n: (jnp.minimum((cur(n) + 1) * halo, last_halo), 0)),
                  blk,
                  pl.BlockSpec((SUBLANES_BF16, d), lambda n: (jnp.minimum((cur(n) + 1) * halo, last_halo), 0)),
                  blk, SMEM_SPEC, pl.BlockSpec((3, d), lambda n: (0, 0))],
        out_specs=[pl.BlockSpec((BLOCK, zw), lambda n: (cur(n), 0)),
                   pl.BlockSpec((BLOCK, kvw2), lambda n: (jnp.maximum(n - 1, 0), 0)),
                   pl.BlockSpec((8, zw), lambda n: (0, 0)), pl.BlockSpec((8, kvw2), lambda n: (0, 0)),
                   pl.BlockSpec((8, d), lambda n: (0, 0)), pl.BlockSpec((8, LANES), lambda n: (0, 0))],
        out_shape=[SDS((t, zw), BF16), SDS((t, kvw2), BF16), SDS((8, zw), F32), SDS((8, kvw2), F32),
                   SDS((8, d), F32), SDS((8, LANES), F32)],
        scratch=[pltpu.VMEM((BLOCK, kvw2), F32)])


def _wgrad_in(dz, dkv, h1, tk, ride=None):
    t, zw = dz.shape
    d = h1.shape[1]
    kvw2 = dkv.shape[1]
    blk = d + kvw2
    assert zw % blk == 0
    tn = (((0,), (0,)), ((), ()))

    def body(a_ref, akv_ref, h_ref, o_ref):
        n, k = pl.program_id(0), pl.program_id(1)

        @pl.when(k == 0)
        def _():
            o_ref[...] = jnp.zeros_like(o_ref)

        @pl.when(n == 0)
        def _():
            o_ref[:d, :] += lax.dot_general(a_ref[:, :d], h_ref[...], tn, preferred_element_type=F32)
            o_ref[d:, :] += lax.dot_general(akv_ref[...], h_ref[...], tn, preferred_element_type=F32)

        @pl.when(n > 0)
        def _():
            o_ref[...] += lax.dot_general(a_ref[...], h_ref[...], tn, preferred_element_type=F32)

    return _pallas(
        body, name="wgrad_in", grid=(zw // blk, t // tk),
        in_specs=[pl.BlockSpec((tk, blk), lambda n, k: (k, n)), pl.BlockSpec((tk, kvw2), lambda n, k: (k, 0)),
                  pl.BlockSpec((tk, d), lambda n, k: (k, 0))],
        out_specs=pl.BlockSpec((blk, d), lambda n, k: (n, 0)), out_shape=SDS((zw, d), F32),
        args=(dz, dkv, h1), sem=("parallel", "arbitrary"), ride=ride)


def _in_proj_bwd(dz, dkv, wt, x, dx1, g_mix, sc1, tm, ride=None):
    t, d = x.shape
    zw = dz.shape[1]
    kvw2 = dkv.shape[1]
    rest = d + kvw2

    def body(a_ref, akv_ref, w_ref, x_ref, dx1_ref, g_ref, sc_ref, gx_ref, st_ref):
        @pl.when(pl.program_id(0) == 0)
        def _():
            st_ref[...] = jnp.zeros_like(st_ref)

        dh = (jnp.dot(a_ref[:, :d], w_ref[:d, :], preferred_element_type=F32)
              + jnp.dot(akv_ref[...], w_ref[d:rest, :], preferred_element_type=F32)
              + jnp.dot(a_ref[:, rest:], w_ref[rest:, :], preferred_element_type=F32))
        xx = x_ref[...]
        r = _rms(xx)
        xn = xx * r
        g = g_ref[...]
        dn = dh * (1.0 + sc_ref[...])
        u = dn * g
        gx_ref[...] = dx1_ref[...] + r * (u - xn * jnp.mean(u * xn, axis=-1, keepdims=True))
        st_ref[0:1, :] += jnp.sum(dh, axis=0, keepdims=True)
        st_ref[1:2, :] += jnp.sum(dh * (xn * g), axis=0, keepdims=True)
        st_ref[2:3, :] += jnp.sum(dn * xn, axis=0, keepdims=True)

    row = pl.BlockSpec((tm, d), lambda i: (i, 0))
    vec = pl.BlockSpec((1, d), lambda i: (0, 0))
    return _pallas(
        body, name="in_proj_bwd", grid=(t // tm,),
        in_specs=[pl.BlockSpec((tm, zw), lambda i: (i, 0)), pl.BlockSpec((tm, kvw2), lambda i: (i, 0)),
                  pl.BlockSpec((zw, d), lambda i: (0, 0), pipeline_mode=pl.Buffered(1)),
                  row, row, vec, vec],
        out_specs=[row, pl.BlockSpec((8, d), lambda i: (0, 0))],
        out_shape=[SDS((t, d), F32), SDS((8, d), F32)],
        args=(dz, dkv, wt, x, dx1, g_mix, sc1), sem=("arbitrary",), ride=ride)


def _to_lanes(v, rows=None):
    flat = v.reshape(-1)
    need = -(-flat.shape[0] // LANES)
    need = -(-need // 8) * 8 if rows is None else rows
    return jnp.pad(flat, (0, need * LANES - flat.shape[0])).reshape(need, LANES)


def kernel(x, c, w_ada, b_ada, g_mix, w_in, b_in, sinks, conv_w, w_out, g_ffn, w_ffn_in, w_ffn_out, g_final, loss_target, m_w_ada, m_b_ada, m_g_mix, m_w_in, m_b_in, m_sinks, m_conv_w, m_w_out, m_g_ffn, m_w_ffn_in, m_w_ffn_out, m_g_final, v_w_ada, v_b_ada, v_g_mix, v_w_in, v_b_in, v_sinks, v_conv_w, v_w_out, v_g_ffn, v_w_ffn_in, v_w_ffn_out, v_g_final):
    xs, tgt = x[0], loss_target[0]
    t, d = xs.shape
    zw = w_in.shape[2] * N_CHIP
    kvw2 = zw - 6 * d
    ff = w_ffn_out.shape[1] * N_CHIP
    n_mod = w_ada.shape[2] * N_CHIP // d
    mod_sh = w_ada.shape[2]
    cw_sh = conv_w.shape[2]
    assert d % (8 * LANES) == 0 and kvw2 == 2 * LANES and t % 512 == 0 and n_mod == 6
    xi, yi, ci = _mesh_pos()
    j_me = 2 * xi + yi
    b_me = 4 * xi + 2 * yi + ci
    pos = jnp.stack([ci, j_me]).astype(jnp.int32)
    tm = 512

    pack1 = jnp.concatenate([c.reshape(d // LANES, LANES), conv_w[0].reshape(-1, LANES)], axis=0)
    pack1 = jnp.pad(pack1, ((0, 16 - pack1.shape[0]), (0, 0)))
    g1 = _all_gather_small(pack1, "gather_c")
    c_all = g1[:, :d // LANES, :].reshape(N_DEV, d)
    cw_rows = 3 * cw_sh // LANES
    conv_w_full = jnp.concatenate(
        [g1[2 * j, d // LANES:d // LANES + cw_rows, :].reshape(3, cw_sh) for j in range(N_CHIP)], axis=1)
    b_ada_sh = lax.dynamic_slice(b_ada, (0, j_me * mod_sh), (1, mod_sh))
    mod_all = _all_gather_small(_ada_fwd(c_all, w_ada[0], b_ada_sh), "gather_mod")
    mod = jnp.concatenate([lax.dynamic_index_in_dim(mod_all[2 * j], b_me, 0, keepdims=True) for j in range(N_CHIP)],
                          axis=1)
    sh1, sc1, ga1, sh2, sc2, ga2 = [mod[:, k * d:(k + 1) * d] for k in range(6)]

    w_in_t, m_w_in_t, v_w_in_t = w_in[0].T, m_w_in[0].T, v_w_in[0].T
    (w_in_g,) = _gather_weights([_cast_into_block(pos, w_in_t, "cast_w_in")])
    w_in_tf = w_in_g.reshape(zw, d)
    later = [_cast_into_block(pos, w_out[0], "cast_w_out"), _cast_into_block(pos, w_ffn_in[0], "cast_w_ffn_in"),
             _cast_into_block(pos, w_ffn_out[0], "cast_w_ffn_out")]

    (z, h1), later = _in_proj(xs, g_mix, sc1, sh1, w_in_tf, b_in, min(t, 1024), zw // 5, ride=_x_gather_ici(later))
    (attn, merged), later = _mixer_fwd(z, sinks, conv_w_full, d, ride=_x_gather_d2d(later))
    w_out_f = later[0].reshape(d, d)
    w_ffn_in_f = later[1]
    w_ffn_out_f = later[2].reshape(ff, d)
    tml = min(t, 1024)
    y1, x1, h2 = _out_proj_fwd(merged, w_out_f, xs, ga1, g_ffn, sc2, sh2, tml)
    gu, act = _ffn_in_fwd(h2, w_ffn_in_f, ff, tml, ff // 2)
    dx2, dy2, st_loss = _ffn_out_loss(act, w_ffn_out_f, x1, tgt, ga2, g_final.reshape(1, d), tml)

    dgu = _ffn_out_bwd(dy2, w_ffn_out_f, gu, tml, ff // 2)
    tk = min(t, 2048)
    dw_ffn_out, _ = _wgrad(
        act, dy2, pl.BlockSpec((tk, ff // 2), lambda m, k: (k, m)), pl.BlockSpec((tk, d), lambda m, k: (k, 0)),
        pl.BlockSpec((ff // 2, d), lambda m, k: (m, 0)), SDS((ff, d), F32), (2, t // tk), "wgrad_ffn_out")
    dx1, dy1, st_ffn = _ffn_in_bwd(dgu, w_ffn_in_f, x1, dx2, y1, g_ffn, sc2, ga1, tm, ff // 2)
    dw_ffn_in, _ = _wgrad(
        h2, dgu, pl.BlockSpec((tk, d), lambda n, k: (k, 0)),
        pl.BlockSpec((None, tk, ff // 2), lambda n, k: (n // 2, k, n % 2)),
        pl.BlockSpec((None, d, ff // 2), lambda n, k: (n, 0, 0)), SDS((N_CHIP, d, ff // 2), F32),
        (N_CHIP, t // tk), "wgrad_ffn_in")
    dw_out, _ = _wgrad(
        merged, dy1, pl.BlockSpec((tk, d), lambda m, k: (k, 0)), pl.BlockSpec((tk, d), lambda m, k: (k, 0)),
        pl.BlockSpec((d, d), lambda m, k: (0, 0)), SDS((d, d), F32), (1, t // tk), "wgrad_out")

    early = [dw_out.reshape(N_CHIP, d // N_CHIP, d), dw_ffn_in, dw_ffn_out.reshape(N_CHIP, ff // N_CHIP, d)]
    early_names = ["w_out", "w_ffn_in", "w_ffn_out"]
    dmerged, from_sibling = _out_proj_bwd(dy1, w_out_f, tml, ride=_x_pair_send(early))
    chip32, chip16 = zip(*[_pair_add(pos, g, s, "pair_add_" + nm)
                           for g, s, nm in zip(early, from_sibling, early_names)])
    (dz, dkv, db_z, db_kv, dcw, dsk), from_chips = _mixer_bwd(z, dmerged, attn, sinks, conv_w_full, d,
                                                              ride=_x_chip_send(list(chip16)))
    fulls = [_chip_add(pos, p, r, "chip_add_" + nm) for p, r, nm in zip(chip32, from_chips, early_names)]
    dw_in_t, (g_w_out, g_w_ffn_in, g_w_ffn_out) = _wgrad_in(dz, dkv, h1, tk, ride=_x_pair_exchange(fulls))
    dw_in_t = dw_in_t.reshape(N_CHIP, zw // N_CHIP, d)

    (from_sibling,) = _exchange(_x_pair_send([dw_in_t]), "pair_send")
    chip32, chip16 = _pair_add(pos, dw_in_t, from_sibling, "pair_add_w_in")
    (grad_x, st_in), (from_chips,) = _in_proj_bwd(dz, dkv, w_in_tf, xs, dx1, g_mix, sc1, tm,
                                                  ride=_x_chip_send([chip16]))
    (g_w_in_t,) = _exchange(_x_pair_exchange([_chip_add(pos, chip32, from_chips, "chip_add_w_in")]), "pair_exchange")

    dmod = jnp.concatenate([st_in[0:1], st_in[1:2], st_ffn[3:4], st_ffn[0:1], st_ffn[1:2], st_loss[0:1]], axis=1)
    db_in = jnp.concatenate([db_z[0:1, :d], db_kv[0:1], db_z[0:1, d + kvw2:]], axis=1)
    seg = [dmod, st_in[2:3], db_in, dsk[0:1], dcw[0:3].reshape(1, 3 * d), st_ffn[2:3], st_loss[1:2],
           st_loss[2:3, :LANES]]
    sizes = [s.shape[1] for s in seg]
    pack2 = _to_lanes(jnp.concatenate(seg, axis=1))
    packs = _all_gather_small(pack2, "gather_small_grads")
    tot = _pack_sum(packs).reshape(-1)
    offs = [sum(sizes[:k]) for k in range(len(sizes))]
    gb_ada, gg_mix, gb_in, gsinks, gcw, gg_ffn, gg_final, loss_v = [tot[o:o + s] for o, s in zip(offs, sizes)]
    loss = loss_v[0]
    gsinks = gsinks[:sinks.shape[1]]
    gcw_sh = lax.dynamic_slice(gcw.reshape(3, d), (0, j_me * cw_sh), (3, cw_sh))

    dmod_all = packs[:, :n_mod * d // LANES, :].reshape(N_DEV, n_mod * d)
    g_w_ada = _ada_wgrad(c_all, lax.dynamic_slice(dmod_all, (0, j_me * mod_sh), (N_DEV, mod_sh)))

    out_g, out_d, out_m, out_v = {}, {}, {}, {}
    big = {"w_ada": (w_ada[0], g_w_ada, m_w_ada[0], v_w_ada[0]),
           "w_out": (w_out[0], g_w_out, m_w_out[0], v_w_out[0]),
           "w_ffn_in": (w_ffn_in[0], g_w_ffn_in, m_w_ffn_in[0], v_w_ffn_in[0]),
           "w_ffn_out": (w_ffn_out[0], g_w_ffn_out, m_w_ffn_out[0], v_w_ffn_out[0])}
    for nm, (w, g, m, v) in big.items():
        out_g[nm], out_d[nm], out_m[nm], out_v[nm] = [o[None] for o in _adamw(w, g, m, v, "adamw_" + nm)]
    out_g["w_in"], out_d["w_in"], out_m["w_in"], out_v["w_in"] = [
        o.T[None] for o in _adamw(w_in_t, g_w_in_t, m_w_in_t, v_w_in_t, "adamw_w_in")]
    small = {"b_ada": (b_ada, gb_ada, m_b_ada, v_b_ada), "g_mix": (g_mix, gg_mix, m_g_mix, v_g_mix),
             "b_in": (b_in, gb_in, m_b_in, v_b_in), "sinks": (sinks, gsinks, m_sinks, v_sinks),
             "conv_w": (conv_w, gcw_sh, m_conv_w, v_conv_w), "g_ffn": (g_ffn, gg_ffn, m_g_ffn, v_g_ffn),
             "g_final": (g_final, gg_final, m_g_final, v_g_final)}
    s_sizes = [w.size for w, _, _, _ in small.values()]
    s_rows = -(-sum(s_sizes) // LANES // 8) * 8

    def s_pack(k):
        return _to_lanes(jnp.concatenate([tup[k].reshape(-1) for tup in small.values()]), s_rows)

    s_out = _adamw(s_pack(0), s_pack(1), s_pack(2), s_pack(3), "adamw_small")
    s_off = 0
    for (nm, (w, g, _, _)), sz in zip(small.items(), s_sizes):
        out_g[nm] = g.reshape(w.shape)
        out_d[nm], out_m[nm], out_v[nm] = [o.reshape(-1)[s_off:s_off + sz].reshape(w.shape) for o in s_out[1:]]
        s_off += sz

    order = ["w_ada", "b_ada", "g_mix", "w_in", "b_in", "sinks", "conv_w", "w_out", "g_ffn", "w_ffn_in", "w_ffn_out",
             "g_final"]
    return (loss, grad_x[None], *[out_g[k] for k in order], *[out_d[k] for k in order],
            *[out_m[k] for k in order], *[out_v[k] for k in order])
```

```python
import functools

import jax
import jax.numpy as jnp
from jax import lax
from jax.experimental import pallas as pl
from jax.experimental.pallas import tpu as pltpu

F32 = jnp.float32
BF16 = jnp.bfloat16
EPS = 1e-6
HEAD_DIM = 64
GROUP = 8
BLOCK = 128
LANES = 128
SUBLANES_BF16 = 16
N_DEV = 8
N_CHIP = 4
VMEM_LIMIT = 56 * 1024 * 1024
MESH = pl.DeviceIdType.MESH

ADAM_LR = 0.001
ADAM_B1 = 0.9
ADAM_B2 = 0.999
ADAM_EPS = 1e-08
ADAM_WD = 0.01
ADAM_STEP = 10

SDS = jax.ShapeDtypeStruct
ANY = pl.BlockSpec(memory_space=pl.ANY)
VMEM_SPEC = pl.BlockSpec(memory_space=pltpu.VMEM)
SMEM_SPEC = pl.BlockSpec(memory_space=pltpu.SMEM)


def _params(*sem):
    return pltpu.CompilerParams(dimension_semantics=sem, vmem_limit_bytes=VMEM_LIMIT)


def _mesh_pos():
    return lax.axis_index("x"), lax.axis_index("y"), lax.axis_index("c")


def _row_tile(rows, cols, itemsize=4, budget=1 << 20, mult=8):
    best = None
    for t in range(mult, rows + 1, mult):
        if rows % t == 0 and t * cols * itemsize <= budget:
            best = t
    if best is None:
        best = rows
    return best


def _all_gather_small(v, name):
    rows, cols = v.shape

    def body(v_ref, out_ref, send_sems, recv_sems, local_sem):
        x, y, c = _mesh_pos()
        me = 4 * x + 2 * y + c
        mine = pltpu.make_async_copy(v_ref, out_ref.at[me], local_sem)
        mine.start()
        peers = []
        for k in range(1, N_DEV):
            px = 1 - x if k & 4 else x
            py = 1 - y if k & 2 else y
            pc = 1 - c if k & 1 else c
            peers.append((px, py, pc))

        def copy(k, block):
            return pltpu.make_async_remote_copy(
                src_ref=v_ref, dst_ref=out_ref.at[block], send_sem=send_sems.at[k], recv_sem=recv_sems.at[k],
                device_id=peers[k], device_id_type=MESH)

        sends = [copy(k, me) for k in range(N_DEV - 1)]
        for cp in sends:
            cp.start()
        for k, (px, py, pc) in enumerate(peers):
            copy(k, 4 * px + 2 * py + pc).wait_recv()
        for cp in sends:
            cp.wait_send()
        mine.wait()

    return pl.pallas_call(
        body, name=name,
        out_shape=SDS((N_DEV, rows, cols), v.dtype),
        in_specs=[VMEM_SPEC], out_specs=VMEM_SPEC,
        scratch_shapes=[pltpu.SemaphoreType.DMA((N_DEV - 1,)), pltpu.SemaphoreType.DMA((N_DEV - 1,)),
                        pltpu.SemaphoreType.DMA],
    )(v)


def _other_chips(x, y):
    return [(1 - x, y), (x, 1 - y), (1 - x, 1 - y)]


def _gather_weights(bufs):
    n_w = len(bufs)

    def body(*refs):
        outs = refs[n_w:2 * n_w]
        send_sems, recv_sems, fsend_sems, frecv_sems = refs[2 * n_w:]
        x, y, c = _mesh_pos()
        j_me = 2 * x + y
        chips = _other_chips(x, y)
        sibling = (x, y, 1 - c)

        def half_rows(w, which):
            half = outs[w].shape[1] // 2
            return pl.ds(pl.multiple_of(which * half, SUBLANES_BF16), half)

        def copy(w, p, block, rows, over_ici):
            sems = (send_sems, recv_sems) if over_ici else (fsend_sems, frecv_sems)
            return pltpu.make_async_remote_copy(
                src_ref=outs[w].at[block, rows], dst_ref=outs[w].at[block, rows],
                send_sem=sems[0].at[w * 3 + p], recv_sem=sems[1].at[w * 3 + p],
                device_id=(*chips[p], c) if over_ici else sibling, device_id_type=MESH)

        def block_of(p):
            return 2 * chips[p][0] + chips[p][1]

        sends = [copy(w, p, j_me, half_rows(w, c), True) for w in range(n_w) for p in range(3)]
        for cp in sends:
            cp.start()
        forwards = []
        for w in range(n_w):
            for p in range(3):
                copy(w, p, block_of(p), half_rows(w, c), True).wait_recv()
                fw = copy(w, p, block_of(p), half_rows(w, c), False)
                fw.start()
                forwards.append(fw)
        for w in range(n_w):
            for p in range(3):
                copy(w, p, block_of(p), half_rows(w, 1 - c), False).wait_recv()
        for cp in sends + forwards:
            cp.wait_send()

    return pl.pallas_call(
        body, name="gather_weights",
        out_shape=[SDS(b.shape, b.dtype) for b in bufs],
        in_specs=[ANY] * n_w, out_specs=[ANY] * n_w,
        input_output_aliases={w: w for w in range(n_w)},
        scratch_shapes=[pltpu.SemaphoreType.DMA((3 * n_w,)), pltpu.SemaphoreType.DMA((3 * n_w,)),
                        pltpu.SemaphoreType.DMA((3 * n_w,)), pltpu.SemaphoreType.DMA((3 * n_w,))],
    )(*bufs)


class _Exchange:
    def __init__(self, operands, out_shape, in_place, n_sems, copies):
        self.operands, self.out_shape, self.in_place, self.n_sems, self.copies = (
            list(operands), list(out_shape), in_place, n_sems, copies)

    def sems(self):
        return [pltpu.SemaphoreType.DMA((self.n_sems,)), pltpu.SemaphoreType.DMA((self.n_sems,))]


def _x_gather_ici(bufs):
    def copies(ins, outs, send_sems, recv_sems):
        x, y, c = _mesh_pos()
        chips = _other_chips(x, y)
        out = []
        for w in range(len(outs)):
            half = outs[w].shape[1] // 2
            rows = pl.ds(pl.multiple_of(c * half, SUBLANES_BF16), half)
            for p in range(3):
                out.append(pltpu.make_async_remote_copy(
                    src_ref=outs[w].at[2 * x + y, rows], dst_ref=outs[w].at[2 * x + y, rows],
                    send_sem=send_sems.at[w * 3 + p], recv_sem=recv_sems.at[w * 3 + p],
                    device_id=(*chips[p], c), device_id_type=MESH))
        return out

    return _Exchange(bufs, [SDS(b.shape, b.dtype) for b in bufs], True, 3 * len(bufs), copies)


def _x_gather_d2d(bufs):
    def copies(ins, outs, send_sems, recv_sems):
        x, y, c = _mesh_pos()
        chips = _other_chips(x, y)
        out = []
        for w in range(len(outs)):
            half = outs[w].shape[1] // 2
            rows = pl.ds(pl.multiple_of(c * half, SUBLANES_BF16), half)
            for p in range(3):
                block = 2 * chips[p][0] + chips[p][1]
                out.append(pltpu.make_async_remote_copy(
                    src_ref=outs[w].at[block, rows], dst_ref=outs[w].at[block, rows],
                    send_sem=send_sems.at[w * 3 + p], recv_sem=recv_sems.at[w * 3 + p],
                    device_id=(x, y, 1 - c), device_id_type=MESH))
        return out

    return _Exchange(bufs, [SDS(b.shape, b.dtype) for b in bufs], True, 3 * len(bufs), copies)


N_REMOTE = 6


def _x_reduce(grads32, grads16):
    n_w = len(grads32)

    def copies(ins, outs, send_sems, recv_sems):
        g32, g16 = ins[:n_w], ins[n_w:]
        from_sib, from_far = outs[:n_w], outs[n_w:]
        x, y, c = _mesh_pos()
        chips = _other_chips(x, y)
        out = []
        for w in range(n_w):
            half = g32[w].shape[1] // 2
            k0 = w * (N_REMOTE + 1)
            out.append(pltpu.make_async_remote_copy(
                src_ref=g32[w].at[2 * x + y, pl.ds(pl.multiple_of((1 - c) * half, SUBLANES_BF16), half), :],
                dst_ref=from_sib[w], send_sem=send_sems.at[k0], recv_sem=recv_sems.at[k0],
                device_id=(x, y, 1 - c), device_id_type=MESH))
            for p in range(3):
                for f in range(2):
                    tc = c if f == 0 else 1 - c
                    k = 2 * p + f
                    out.append(pltpu.make_async_remote_copy(
                        src_ref=g16[w].at[2 * chips[p][0] + chips[p][1],
                                          pl.ds(pl.multiple_of(tc * half, SUBLANES_BF16), half), :],
                        dst_ref=from_far[w].at[k], send_sem=send_sems.at[k0 + 1 + k], recv_sem=recv_sems.at[k0 + 1 + k],
                        device_id=(*chips[p], tc), device_id_type=MESH))
        return out

    shapes = ([SDS((g.shape[1] // 2, g.shape[2]), g.dtype) for g in grads32]
              + [SDS((N_REMOTE, g.shape[1] // 2, g.shape[2]), g.dtype) for g in grads16])
    return _Exchange(list(grads32) + list(grads16), shapes, False, (N_REMOTE + 1) * n_w, copies)


def _x_pair_exchange(fulls):
    def copies(ins, outs, send_sems, recv_sems):
        x, y, c = _mesh_pos()
        out = []
        for w in range(len(outs)):
            half = outs[w].shape[0] // 2
            rows = pl.ds(pl.multiple_of(c * half, 8), half)
            out.append(pltpu.make_async_remote_copy(
                src_ref=outs[w].at[rows], dst_ref=outs[w].at[rows], send_sem=send_sems.at[w],
                recv_sem=recv_sems.at[w], device_id=(x, y, 1 - c), device_id_type=MESH))
        return out

    return _Exchange(fulls, [SDS(f.shape, f.dtype) for f in fulls], True, len(fulls), copies)


def _pallas(body, *, name, grid, in_specs, out_specs, out_shape, args, scratch=(), sem=None, ride=None):
    single = not isinstance(out_specs, (list, tuple))
    out_specs_l = [out_specs] if single else list(out_specs)
    out_shape_l = [out_shape] if single else list(out_shape)
    n_in, n_out, n_scr = len(in_specs), len(out_specs_l), len(scratch)
    if ride is None:
        res = pl.pallas_call(body, name=name, grid=grid, in_specs=list(in_specs), out_specs=out_specs,
                             out_shape=out_shape, scratch_shapes=list(scratch), compiler_params=_params(*sem))(*args)
        return res, None
    n_x, n_xo = len(ride.operands), len(ride.out_shape)

    def full_body(*refs):
        ins, x_ins = refs[:n_in], refs[n_in:n_in + n_x]
        outs = refs[n_in + n_x:n_in + n_x + n_out]
        x_outs = refs[n_in + n_x + n_out:n_in + n_x + n_out + n_xo]
        rest = refs[n_in + n_x + n_out + n_xo:]
        scr, (send_sems, recv_sems) = rest[:n_scr], rest[n_scr:]
        first = functools.reduce(jnp.logical_and, [pl.program_id(a) == 0 for a in range(len(grid))])
        last = functools.reduce(jnp.logical_and, [pl.program_id(a) == grid[a] - 1 for a in range(len(grid))])

        @pl.when(first)
        def _():
            for cp in ride.copies(x_ins, x_outs, send_sems, recv_sems):
                cp.start()

        body(*ins, *outs, *scr)

        @pl.when(last)
        def _():
            for cp in ride.copies(x_ins, x_outs, send_sems, recv_sems):
                cp.wait()

    res = pl.pallas_call(
        full_body, name=name, grid=grid, in_specs=list(in_specs) + [ANY] * n_x,
        out_specs=out_specs_l + [ANY] * n_xo, out_shape=out_shape_l + ride.out_shape,
        input_output_aliases={n_in + k: n_out + k for k in range(n_x)} if ride.in_place else {},
        scratch_shapes=list(scratch) + ride.sems(),
        compiler_params=_params(*(["arbitrary"] * len(grid))))(*args, *ride.operands)
    own = res[0] if single else list(res[:n_out])
    return own, list(res[n_out:])


def _exchange(ride, name):
    n_x, n_xo = len(ride.operands), len(ride.out_shape)

    def body(*refs):
        x_ins, x_outs = refs[:n_x], refs[n_x:n_x + n_xo]
        send_sems, recv_sems = refs[n_x + n_xo:]
        copies = ride.copies(x_ins, x_outs, send_sems, recv_sems)
        for cp in copies:
            cp.start()
        for cp in copies:
            cp.wait()

    return pl.pallas_call(
        body, name=name, in_specs=[ANY] * n_x, out_specs=[ANY] * n_xo, out_shape=ride.out_shape,
        input_output_aliases={k: k for k in range(n_x)} if ride.in_place else {},
        scratch_shapes=ride.sems())(*ride.operands)


def _cast_into_block(pos, w, name):
    rows, cols = w.shape
    tr = _row_tile(rows, cols, mult=SUBLANES_BF16)

    def body(pos_ref, w_ref, o_ref):
        del pos_ref
        o_ref[...] = w_ref[...].astype(BF16)

    return pl.pallas_call(
        body, name=name,
        grid_spec=pltpu.PrefetchScalarGridSpec(
            num_scalar_prefetch=1, grid=(rows // tr,),
            in_specs=[pl.BlockSpec((tr, cols), lambda i, pos_ref: (i, 0))],
            out_specs=pl.BlockSpec((None, tr, cols), lambda i, pos_ref: (pos_ref[1], i, 0))),
        out_shape=SDS((N_CHIP, rows, cols), BF16), compiler_params=_params("parallel"))(pos, w)


def _sum_terms(pos, grad, from_sib, from_far, name):
    _, rows, cols = grad.shape
    half = rows // 2
    tr = _row_tile(half, cols, mult=SUBLANES_BF16)
    nblk = half // tr

    def body(pos_ref, g_ref, s_ref, r_ref, o_ref):
        del pos_ref
        acc = g_ref[...] + s_ref[...]
        for k in range(N_REMOTE):
            acc = acc + r_ref[k].astype(F32)
        o_ref[...] = acc

    return pl.pallas_call(
        body, name=name,
        grid_spec=pltpu.PrefetchScalarGridSpec(
            num_scalar_prefetch=1, grid=(nblk,),
            in_specs=[pl.BlockSpec((None, tr, cols), lambda i, pos_ref: (pos_ref[1], pos_ref[0] * nblk + i, 0)),
                      pl.BlockSpec((tr, cols), lambda i, pos_ref: (i, 0)),
                      pl.BlockSpec((N_REMOTE, tr, cols), lambda i, pos_ref: (0, i, 0))],
            out_specs=pl.BlockSpec((tr, cols), lambda i, pos_ref: (pos_ref[0] * nblk + i, 0))),
        out_shape=SDS((rows, cols), F32),
        compiler_params=_params("parallel"),
    )(pos, grad, from_sib, from_far)


def _adamw(w, g, m, v, name):
    rows, cols = w.shape
    tr = _row_tile(rows, cols)

    def body(w_ref, g_ref, m_ref, v_ref, go_ref, d_ref, nm_ref, nv_ref):
        gg = g_ref[...]
        go_ref[...] = gg
        nm = ADAM_B1 * m_ref[...] + (1.0 - ADAM_B1) * gg
        nv = ADAM_B2 * v_ref[...] + (1.0 - ADAM_B2) * (gg * gg)
        m_hat = nm / (1.0 - ADAM_B1 ** ADAM_STEP)
        v_hat = nv / (1.0 - ADAM_B2 ** ADAM_STEP)
        d_ref[...] = -ADAM_LR * (m_hat / (jnp.sqrt(v_hat) + ADAM_EPS) + ADAM_WD * w_ref[...])
        nm_ref[...] = nm
        nv_ref[...] = nv

    spec = pl.BlockSpec((tr, cols), lambda i: (i, 0))
    return pl.pallas_call(body, name=name, grid=(rows // tr,), in_specs=[spec] * 4, out_specs=[spec] * 4,
                          out_shape=[SDS((rows, cols), F32)] * 4, compiler_params=_params("parallel"))(w, g, m, v)


def _pack_sum(gathered):
    _, rows, cols = gathered.shape

    def body(g_ref, o_ref):
        acc = g_ref[0]
        for d in range(1, N_DEV):
            acc = acc + g_ref[d]
        o_ref[...] = acc

    return pl.pallas_call(body, name="pack_sum", in_specs=[VMEM_SPEC], out_specs=VMEM_SPEC,
                          out_shape=SDS((rows, cols), F32))(gathered)


def _ada_fwd(c_all, w_sh, b_sh):
    d, n = w_sh.shape
    tn = 512

    def body(c_ref, w_ref, b_ref, o_ref):
        cc = c_ref[...]
        s = (cc * jax.nn.sigmoid(cc)).astype(BF16)
        o_ref[...] = jnp.dot(s, w_ref[...].astype(BF16), preferred_element_type=F32) + b_ref[...]

    return pl.pallas_call(
        body, name="ada_fwd", grid=(n // tn,),
        in_specs=[pl.BlockSpec((N_DEV, d), lambda j: (0, 0)), pl.BlockSpec((d, tn), lambda j: (0, j)),
                  pl.BlockSpec((1, tn), lambda j: (0, j))],
        out_specs=pl.BlockSpec((N_DEV, tn), lambda j: (0, j)),
        out_shape=SDS((N_DEV, n), F32), compiler_params=_params("parallel"))(c_all, w_sh, b_sh)


def _ada_wgrad(c_all, dmod_sh):
    d = c_all.shape[1]
    n = dmod_sh.shape[1]
    tn = 512

    def body(c_ref, g_ref, o_ref):
        cc = c_ref[...]
        s = cc * jax.nn.sigmoid(cc)
        o_ref[...] = lax.dot_general(s, g_ref[...], (((0,), (0,)), ((), ())), preferred_element_type=F32,
                                     precision=lax.Precision.HIGHEST)

    return pl.pallas_call(
        body, name="ada_wgrad", grid=(n // tn,),
        in_specs=[pl.BlockSpec((N_DEV, d), lambda j: (0, 0)), pl.BlockSpec((N_DEV, tn), lambda j: (0, j))],
        out_specs=pl.BlockSpec((d, tn), lambda j: (0, j)),
        out_shape=SDS((d, n), F32), compiler_params=_params("parallel"))(c_all, dmod_sh)


def _rms(xf):
    return lax.rsqrt(jnp.mean(xf * xf, axis=-1, keepdims=True) + EPS)


def _in_proj(x, g, sc, sh, wt, b, tm, tn, ride=None):
    t, d = x.shape
    n = wt.shape[0]

    def body(x_ref, g_ref, sc_ref, sh_ref, w_ref, b_ref, z_ref, h_ref):
        @pl.when(pl.program_id(1) == 0)
        def _():
            xf = x_ref[...]
            h_ref[...] = ((xf * _rms(xf) * g_ref[...]) * (1.0 + sc_ref[...]) + sh_ref[...]).astype(BF16)

        acc = lax.dot_general(h_ref[...], w_ref[...], (((1,), (1,)), ((), ())), preferred_element_type=F32)
        z_ref[...] = (acc + b_ref[...]).astype(BF16)

    row = pl.BlockSpec((tm, d), lambda i, j: (i, 0))
    vec = pl.BlockSpec((1, d), lambda i, j: (0, 0))
    return _pallas(
        body, name="in_proj", grid=(t // tm, n // tn),
        in_specs=[row, vec, vec, vec, pl.BlockSpec((tn, d), lambda i, j: (j, 0)),
                  pl.BlockSpec((1, tn), lambda i, j: (0, j))],
        out_specs=[pl.BlockSpec((tm, tn), lambda i, j: (i, j)), row],
        out_shape=[SDS((t, n), BF16), SDS((t, d), BF16)], args=(x, g, sc, sh, wt, b),
        sem=("parallel", "arbitrary"), ride=ride)


def _segments(d, kvw2):
    o = d + kvw2
    names = ("cb", "cc", "cx", "ga", "gc")
    seg = {nm: slice(o + k * d, o + (k + 1) * d) for k, nm in enumerate(names)}
    seg["q"], seg["kv"] = slice(0, d), slice(d, o)
    return seg


def _attn_masks():
    rows = 4 * BLOCK
    r = lax.broadcasted_iota(jnp.int32, (rows, 2 * BLOCK), 0) & (BLOCK - 1)
    col = lax.broadcasted_iota(jnp.int32, (rows, 2 * BLOCK), 1)
    return (col > r) & (col <= r + BLOCK), col


def _kv_variants(kv, n_kv_w):
    assert n_kv_w == LANES
    kb, vb = kv[:, :LANES] * (HEAD_DIM ** -0.5), kv[:, LANES:]
    kr, vr = pltpu.roll(kb, HEAD_DIM, 1), pltpu.roll(vb, HEAD_DIM, 1)
    lane = lax.broadcasted_iota(jnp.int32, kb.shape, 1)
    lo = lane < HEAD_DIM
    zero = jnp.zeros_like(kb)
    k_eff = [[None, None], [None, None]]
    v_eff = [[None, None], [None, None]]
    for h in range(2):
        for e in range(2):
            ksrc, vsrc = (kb, vb) if e == h else (kr, vr)
            keep = lo if e == 0 else jnp.logical_not(lo)
            k_eff[h][e] = jnp.where(keep, ksrc, zero)
            v_eff[h][e] = jnp.where(keep, vsrc, zero)
    return k_eff, v_eff


def _sink_column(sinks_ref, h, e):
    rowblk = lax.broadcasted_iota(jnp.int32, (4 * BLOCK, 1), 0) // BLOCK
    col = jnp.zeros((4 * BLOCK, 1), F32)
    for j in range(4):
        col = jnp.where(rowblk == j, sinks_ref[0, GROUP * h + 2 * j + e], col)
    return col


def _softmax_sink(s, valid, sink):
    s = jnp.where(valid, s, -jnp.inf)
    m = jnp.maximum(jnp.max(s, axis=-1, keepdims=True), sink)
    p = jnp.exp(s - m)
    psink = jnp.exp(sink - m)
    den = jnp.sum(p, axis=-1, keepdims=True) + psink
    inv = 1.0 / den
    return p * inv, psink * inv


def _shift_down(a, s, prev):
    rows = a.shape[0]
    out = pltpu.roll(a, s, 0)
    row = lax.broadcasted_iota(jnp.int32, a.shape, 0)
    for t in range(s):
        out = jnp.where(row == t, prev[SUBLANES_BF16 - s + t:SUBLANES_BF16 - s + t + 1, :], out)
    del rows
    return out


def _shift_up(a, s, nxt):
    rows = a.shape[0]
    out = pltpu.roll(a, rows - s, 0)
    row = lax.broadcasted_iota(jnp.int32, a.shape, 0)
    for t in range(s):
        out = jnp.where(row == rows - s + t, nxt[t:t + 1, :], out)
    return out


def _stack_pairs(ref, h):
    return jnp.concatenate([ref[:, (4 * h + j) * LANES:(4 * h + j + 1) * LANES] for j in range(4)], axis=0)


def _mixer_fwd(z, sinks, conv_w, d, ride=None):
    t, zw = z.shape
    kvw2 = zw - 6 * d
    nb = t // BLOCK
    halo = BLOCK // SUBLANES_BF16
    seg = _segments(d, kvw2)

    def body(z_ref, kvp_ref, prev_ref, sinks_ref, cw_ref, attn_ref, merged_ref):
        n = pl.program_id(0)
        kv = jnp.concatenate([kvp_ref[...], z_ref[:, seg["kv"]]], axis=0)
        k_eff, v_eff = _kv_variants(kv, kvw2 // 2)
        band, col = _attn_masks()
        valid = band & ((n > 0) | (col >= BLOCK))
        for h in range(2):
            q4 = _stack_pairs(z_ref, h)
            o4 = jnp.zeros((4 * BLOCK, LANES), F32)
            for e in range(2):
                s = lax.dot_general(q4, k_eff[h][e], (((1,), (1,)), ((), ())), preferred_element_type=F32)
                p, _ = _softmax_sink(s, valid, _sink_column(sinks_ref, h, e))
                o4 = o4 + jnp.dot(p.astype(BF16), v_eff[h][e], preferred_element_type=F32)
            for j in range(4):
                attn_ref[:, (4 * h + j) * LANES:(4 * h + j + 1) * LANES] = o4[j * BLOCK:(j + 1) * BLOCK].astype(BF16)
        cb = z_ref[:, seg["cb"]].astype(F32)
        p_in = z_ref[:, seg["cc"]].astype(F32) * z_ref[:, seg["cx"]].astype(F32)
        prev = jnp.where(n > 0, prev_ref[:, seg["cc"]].astype(F32) * prev_ref[:, seg["cx"]].astype(F32), 0.0)
        cconv = (cw_ref[0:1, :] * _shift_down(p_in, 2, prev) + cw_ref[1:2, :] * _shift_down(p_in, 1, prev)
                 + cw_ref[2:3, :] * p_in)
        sa = jax.nn.sigmoid(z_ref[:, seg["ga"]].astype(F32))
        sg = jax.nn.sigmoid(z_ref[:, seg["gc"]].astype(F32))
        merged_ref[...] = (sa * attn_ref[...].astype(F32) + sg * (cb * cconv)).astype(BF16)

    blk = pl.BlockSpec((BLOCK, d), lambda n: (n, 0))
    return _pallas(
        body, name="mixer_fwd", grid=(nb,),
        in_specs=[pl.BlockSpec((BLOCK, zw), lambda n: (n, 0)),
                  pl.BlockSpec((BLOCK, kvw2), lambda n: (jnp.maximum(n - 1, 0), d // kvw2)),
                  pl.BlockSpec((SUBLANES_BF16, zw), lambda n: (jnp.maximum(n * halo - 1, 0), 0)),
                  SMEM_SPEC, pl.BlockSpec((3, d), lambda n: (0, 0))],
        out_specs=[blk, blk],
        out_shape=[SDS((t, d), BF16), SDS((t, d), BF16)],
        args=(z, z, z, sinks, conv_w), sem=("parallel",), ride=ride)


def _out_proj_fwd(merged, w_out, x, ga1, g_ffn, sc2, sh2, tm):
    t, d = x.shape

    def body(m_ref, w_ref, x_ref, ga_ref, g_ref, sc_ref, sh_ref, y_ref, x1_ref, h_ref):
        y = jnp.dot(m_ref[...], w_ref[...], preferred_element_type=F32)
        x1 = x_ref[...] + ga_ref[...] * y
        y_ref[...] = y.astype(BF16)
        x1_ref[...] = x1
        h_ref[...] = ((x1 * _rms(x1) * g_ref[...]) * (1.0 + sc_ref[...]) + sh_ref[...]).astype(BF16)

    row = pl.BlockSpec((tm, d), lambda i: (i, 0))
    vec = pl.BlockSpec((1, d), lambda i: (0, 0))
    return pl.pallas_call(
        body, name="out_proj_fwd", grid=(t // tm,),
        in_specs=[row, pl.BlockSpec((d, d), lambda i: (0, 0)), row, vec, vec, vec, vec],
        out_specs=[row, row, row],
        out_shape=[SDS((t, d), BF16), SDS((t, d), F32), SDS((t, d), BF16)],
        compiler_params=_params("parallel"))(merged, w_out, x, ga1, g_ffn, sc2, sh2)


def _ffn_in_fwd(h2, w, ff, tm, tn):
    t, d = h2.shape
    nj = ff // tn
    assert w.shape == (2 * nj, d, tn)

    def body(h_ref, wg_ref, wu_ref, gu_ref, act_ref):
        hh = h_ref[...]
        g = jnp.dot(hh, wg_ref[...], preferred_element_type=F32)
        u = jnp.dot(hh, wu_ref[...], preferred_element_type=F32)
        gu_ref[0] = g.astype(BF16)
        gu_ref[1] = u.astype(BF16)
        act_ref[...] = ((g * jax.nn.sigmoid(g)) * u).astype(BF16)

    return pl.pallas_call(
        body, name="ffn_in_fwd", grid=(nj, t // tm),
        in_specs=[pl.BlockSpec((tm, d), lambda j, i: (i, 0)), pl.BlockSpec((None, d, tn), lambda j, i: (j, 0, 0)),
                  pl.BlockSpec((None, d, tn), lambda j, i: (j + nj, 0, 0))],
        out_specs=[pl.BlockSpec((2, tm, tn), lambda j, i: (0, i, j)), pl.BlockSpec((tm, tn), lambda j, i: (i, j))],
        out_shape=[SDS((2, t, ff), BF16), SDS((t, ff), BF16)],
        compiler_params=_params("parallel", "parallel"))(h2, w, w)


def _ffn_out_loss(act, w, x1, target, ga2, g_final, tm):
    t, d = x1.shape
    ff = act.shape[1]

    def body(a_ref, w_ref, x1_ref, tg_ref, ga_ref, gf_ref, dx2_ref, dy2_ref, st_ref):
        @pl.when(pl.program_id(0) == 0)
        def _():
            st_ref[...] = jnp.zeros_like(st_ref)

        halves = [slice(k * (tm // 2), (k + 1) * (tm // 2)) for k in range(2)]
        y2s = [jnp.dot(a_ref[rows, :], w_ref[...], preferred_element_type=F32) for rows in halves]
        for rows, y2 in zip(halves, y2s):
            x2 = x1_ref[rows, :] + ga_ref[...] * y2
            r = _rms(x2)
            yn = x2 * r
            err = yn * gf_ref[...] - tg_ref[rows, :]
            loss = 0.5 * jnp.sum(jnp.mean(err * err, axis=-1, keepdims=True), axis=0, keepdims=True)
            dy = err * (1.0 / d)
            u = dy * gf_ref[...]
            dx2 = r * (u - yn * jnp.mean(u * yn, axis=-1, keepdims=True))
            dx2_ref[rows, :] = dx2
            dy2_ref[rows, :] = (ga_ref[...] * dx2).astype(BF16)
            st_ref[0:1, :] += jnp.sum(dx2 * y2, axis=0, keepdims=True)
            st_ref[1:2, :] += jnp.sum(dy * yn, axis=0, keepdims=True)
            st_ref[2:3, :] += jnp.broadcast_to(loss, (1, d))

    row = pl.BlockSpec((tm, d), lambda i: (i, 0))
    vec = pl.BlockSpec((1, d), lambda i: (0, 0))
    return pl.pallas_call(
        body, name="ffn_out_loss", grid=(t // tm,),
        in_specs=[pl.BlockSpec((tm, ff), lambda i: (i, 0)),
                  pl.BlockSpec((ff, d), lambda i: (0, 0), pipeline_mode=pl.Buffered(1)), row, row, vec, vec],
        out_specs=[row, row, pl.BlockSpec((8, d), lambda i: (0, 0))],
        out_shape=[SDS((t, d), F32), SDS((t, d), BF16), SDS((8, d), F32)],
        compiler_params=_params("arbitrary"))(act, w, x1, target, ga2, g_final)


def _ffn_out_bwd(dy2, w, gu, tm, tn):
    t, d = dy2.shape
    ff = w.shape[0]

    def body(dy_ref, w_ref, gu_ref, o_ref):
        dy = dy_ref[...]
        for lo in range(0, tn, 3 * LANES):
            cols = slice(lo, min(lo + 3 * LANES, tn))
            dact = lax.dot_general(dy, w_ref[cols, :], (((1,), (1,)), ((), ())), preferred_element_type=F32)
            g = gu_ref[0, :, cols].astype(F32)
            u = gu_ref[1, :, cols].astype(F32)
            sg = jax.nn.sigmoid(g)
            a = dact * sg
            du = a * g
            o_ref[0, :, cols] = (u * (a + du * (1.0 - sg))).astype(BF16)
            o_ref[1, :, cols] = du.astype(BF16)

    gu_spec = pl.BlockSpec((2, tm, tn), lambda j, i: (0, i, j))
    return pl.pallas_call(
        body, name="ffn_out_bwd", grid=(ff // tn, t // tm),
        in_specs=[pl.BlockSpec((tm, d), lambda j, i: (i, 0)), pl.BlockSpec((tn, d), lambda j, i: (j, 0)), gu_spec],
        out_specs=gu_spec, out_shape=SDS((2, t, ff), BF16),
        compiler_params=_params("parallel", "parallel"))(dy2, w, gu)


def _wgrad(a, b, a_spec, b_spec, out_spec, out_shape, grid, name, ride=None):
    def body(a_ref, b_ref, o_ref, o16_ref):
        k = pl.program_id(len(grid) - 1)

        @pl.when(k == 0)
        def _():
            o_ref[...] = jnp.zeros_like(o_ref)

        o_ref[...] += lax.dot_general(a_ref[...], b_ref[...], (((0,), (0,)), ((), ())), preferred_element_type=F32)

        @pl.when(k == grid[-1] - 1)
        def _():
            o16_ref[...] = o_ref[...].astype(BF16)

    return _pallas(
        body, name=name, grid=grid, in_specs=[a_spec, b_spec], out_specs=[out_spec, out_spec],
        out_shape=[out_shape, SDS(out_shape.shape, BF16)], args=(a, b),
        sem=["parallel"] * (len(grid) - 1) + ["arbitrary"], ride=ride)


def _ffn_in_bwd(dgu, w, x1, dx2, y1, g_ffn, sc2, ga1, tm, tk):
    t, d = x1.shape
    ff = dgu.shape[2]
    n_sh, _, sw = w.shape
    per = ff // sw
    del tk
    nt = (((1,), (1,)), ((), ()))

    def body(a_ref, w_ref, x1_ref, dx2_ref, y1_ref, g_ref, sc_ref, ga_ref, dx1_ref, dy1_ref, st_ref):
        @pl.when(pl.program_id(0) == 0)
        def _():
            st_ref[...] = jnp.zeros_like(st_ref)

        dh = None
        for j in range(n_sh):
            part = lax.dot_general(a_ref[j // per, :, (j % per) * sw:(j % per + 1) * sw], w_ref[j], nt,
                                   preferred_element_type=F32)
            dh = part if dh is None else dh + part
        x1 = x1_ref[...]
        r = _rms(x1)
        xn = x1 * r
        g = g_ref[...]
        dn = dh * (1.0 + sc_ref[...])
        u = dn * g
        dx1 = dx2_ref[...] + r * (u - xn * jnp.mean(u * xn, axis=-1, keepdims=True))
        dx1_ref[...] = dx1
        dy1_ref[...] = (ga_ref[...] * dx1).astype(BF16)
        st_ref[0:1, :] += jnp.sum(dh, axis=0, keepdims=True)
        st_ref[1:2, :] += jnp.sum(dh * (xn * g), axis=0, keepdims=True)
        st_ref[2:3, :] += jnp.sum(dn * xn, axis=0, keepdims=True)
        st_ref[3:4, :] += jnp.sum(dx1 * y1_ref[...].astype(F32), axis=0, keepdims=True)

    row = pl.BlockSpec((tm, d), lambda i: (i, 0))
    vec = pl.BlockSpec((1, d), lambda i: (0, 0))
    return pl.pallas_call(
        body, name="ffn_in_bwd", grid=(t // tm,),
        in_specs=[pl.BlockSpec((2, tm, ff), lambda i: (0, i, 0)),
                  pl.BlockSpec((n_sh, d, sw), lambda i: (0, 0, 0), pipeline_mode=pl.Buffered(1)),
                  row, row, row, vec, vec, vec],
        out_specs=[row, row, pl.BlockSpec((8, d), lambda i: (0, 0))],
        out_shape=[SDS((t, d), F32), SDS((t, d), BF16), SDS((8, d), F32)],
        compiler_params=_params("arbitrary"))(dgu, w, x1, dx2, y1, g_ffn, sc2, ga1)


def _out_proj_bwd(dy1, w_out, tm, ride=None):
    t, d = dy1.shape

    def body(dy_ref, w_ref, o_ref):
        o_ref[...] = lax.dot_general(dy_ref[...], w_ref[...], (((1,), (1,)), ((), ())),
                                     preferred_element_type=F32).astype(BF16)

    row = pl.BlockSpec((tm, d), lambda i: (i, 0))
    return _pallas(body, name="out_proj_bwd", grid=(t // tm,),
                   in_specs=[row, pl.BlockSpec((d, d), lambda i: (0, 0))], out_specs=row,
                   out_shape=SDS((t, d), BF16), args=(dy1, w_out), sem=("parallel",), ride=ride)


def _mixer_bwd(z, dmerged, attn, sinks, conv_w, d, ride=None):
    t, zw = z.shape
    kvw2 = zw - 6 * d
    nb = t // BLOCK
    halo = BLOCK // SUBLANES_BF16
    last_halo = t // SUBLANES_BF16 - 1
    scale = HEAD_DIM ** -0.5
    seg = _segments(d, kvw2)

    def body(z_ref, kvp_ref, prev_ref, next_ref, dm_ref, dmn_ref, attn_ref, sinks_ref, cw_ref,
             dz_ref, dkv_ref, db_ref, dbkv_ref, dcw_ref, dsk_ref, carry_ref):
        n = pl.program_id(0)

        @pl.when(n == 0)
        def _():
            carry_ref[...] = jnp.zeros_like(carry_ref)
            db_ref[...] = jnp.zeros_like(db_ref)
            dbkv_ref[...] = jnp.zeros_like(dbkv_ref)
            dcw_ref[...] = jnp.zeros_like(dcw_ref)
            dsk_ref[...] = jnp.zeros_like(dsk_ref)

        @pl.when(n < nb)
        def _():
            dm = dm_ref[...].astype(F32)
            sa = jax.nn.sigmoid(z_ref[:, seg["ga"]].astype(F32))
            dga = dm * attn_ref[...].astype(F32) * sa * (1.0 - sa)
            dz_ref[:, seg["ga"]] = dga.astype(BF16)
            db_ref[0:1, seg["ga"]] += jnp.sum(dga, axis=0, keepdims=True)
            dattn = (dm * sa).astype(BF16)

            kv = jnp.concatenate([kvp_ref[...], z_ref[:, seg["kv"]]], axis=0)
            k_eff, v_eff = _kv_variants(kv, kvw2 // 2)
            band, col = _attn_masks()
            valid = band & ((n > 0) | (col >= BLOCK))
            lane_lo = lax.broadcasted_iota(jnp.int32, (2 * BLOCK, LANES), 1) < HEAD_DIM
            sink_lane = lax.broadcasted_iota(jnp.int32, (1, LANES), 1)
            rowblk = lax.broadcasted_iota(jnp.int32, (4 * BLOCK, 1), 0) // BLOCK
            dk_acc = [jnp.zeros((2 * BLOCK, LANES), F32), jnp.zeros((2 * BLOCK, LANES), F32)]
            dv_acc = [jnp.zeros((2 * BLOCK, LANES), F32), jnp.zeros((2 * BLOCK, LANES), F32)]
            dsink = jnp.zeros((1, LANES), F32)
            for h in range(2):
                q4 = _stack_pairs(z_ref, h)
                do4 = jnp.concatenate([dattn[:, (4 * h + j) * LANES:(4 * h + j + 1) * LANES] for j in range(4)],
                                      axis=0)
                dq4 = jnp.zeros((4 * BLOCK, LANES), F32)
                for e in range(2):
                    s = lax.dot_general(q4, k_eff[h][e], (((1,), (1,)), ((), ())), preferred_element_type=F32)
                    p, psink = _softmax_sink(s, valid, _sink_column(sinks_ref, h, e))
                    dp = lax.dot_general(do4, v_eff[h][e], (((1,), (1,)), ((), ())), preferred_element_type=F32)
                    delta = jnp.sum(p * dp, axis=-1, keepdims=True)
                    ds = (p * (dp - delta)).astype(BF16)
                    dq4 = dq4 + jnp.dot(ds, k_eff[h][e], preferred_element_type=F32)
                    dk = lax.dot_general(q4, ds, (((0,), (0,)), ((), ())), preferred_element_type=F32).T
                    dv = lax.dot_general(do4, p.astype(BF16), (((0,), (0,)), ((), ())), preferred_element_type=F32).T
                    keep = lane_lo if e == 0 else jnp.logical_not(lane_lo)
                    slot = 0 if e == h else 1
                    dk_acc[slot] = dk_acc[slot] + jnp.where(keep, dk, 0.0)
                    dv_acc[slot] = dv_acc[slot] + jnp.where(keep, dv, 0.0)
                    dsk = -(psink * delta)
                    for j in range(4):
                        tot = jnp.sum(jnp.where(rowblk == j, dsk, 0.0), axis=0, keepdims=True)
                        dsink = dsink + jnp.where(sink_lane == GROUP * h + 2 * j + e, tot, 0.0)
                for j in range(4):
                    cols = slice((4 * h + j) * LANES, (4 * h + j + 1) * LANES)
                    dqj = dq4[j * BLOCK:(j + 1) * BLOCK]
                    dz_ref[:, cols] = dqj.astype(BF16)
                    db_ref[0:1, cols] += jnp.sum(dqj, axis=0, keepdims=True)
            dsk_ref[0:1, :] += dsink
            dkv_new = jnp.concatenate([(dk_acc[0] + pltpu.roll(dk_acc[1], HEAD_DIM, 1)) * scale,
                                       dv_acc[0] + pltpu.roll(dv_acc[1], HEAD_DIM, 1)], axis=1)
            done = carry_ref[...] + dkv_new[:BLOCK]
            dkv_ref[...] = done.astype(BF16)
            dbkv_ref[0:1, :] += jnp.sum(done, axis=0, keepdims=True)
            carry_ref[...] = dkv_new[BLOCK:]

            cb = z_ref[:, seg["cb"]].astype(F32)
            cc = z_ref[:, seg["cc"]].astype(F32)
            cx = z_ref[:, seg["cx"]].astype(F32)
            sg = jax.nn.sigmoid(z_ref[:, seg["gc"]].astype(F32))
            p_in = cc * cx
            prev = jnp.where(n > 0, prev_ref[:, seg["cc"]].astype(F32) * prev_ref[:, seg["cx"]].astype(F32), 0.0)
            p_m1 = _shift_down(p_in, 1, prev)
            p_m2 = _shift_down(p_in, 2, prev)
            w0, w1, w2 = cw_ref[0:1, :], cw_ref[1:2, :], cw_ref[2:3, :]
            cconv = w0 * p_m2 + w1 * p_m1 + w2 * p_in
            dconv = dm * sg
            dgc = dm * (cb * cconv) * sg * (1.0 - sg)
            dcb = dconv * cconv
            dcc_t = dconv * cb
            nxt = jnp.where(n < nb - 1, dmn_ref[...].astype(F32) * jax.nn.sigmoid(next_ref[:, seg["gc"]].astype(F32))
                            * next_ref[:, seg["cb"]].astype(F32), 0.0)
            dpin = w2 * dcc_t + w1 * _shift_up(dcc_t, 1, nxt) + w0 * _shift_up(dcc_t, 2, nxt)
            for nm, val in (("cb", dcb), ("cc", dpin * cx), ("cx", dpin * cc), ("gc", dgc)):
                dz_ref[:, seg[nm]] = val.astype(BF16)
                db_ref[0:1, seg[nm]] += jnp.sum(val, axis=0, keepdims=True)
            dcw_ref[0:1, :] += jnp.sum(dcc_t * p_m2, axis=0, keepdims=True)
            dcw_ref[1:2, :] += jnp.sum(dcc_t * p_m1, axis=0, keepdims=True)
            dcw_ref[2:3, :] += jnp.sum(dcc_t * p_in, axis=0, keepdims=True)

        @pl.when(n == nb)
        def _():
            done = carry_ref[...]
            dkv_ref[...] = done.astype(BF16)
            dbkv_ref[0:1, :] += jnp.sum(done, axis=0, keepdims=True)

    def cur(n):
        return jnp.minimum(n, nb - 1)

    blk = pl.BlockSpec((BLOCK, d), lambda n: (cur(n), 0))
    return _pallas(
        body, name="mixer_bwd", grid=(nb + 1,), ride=ride, sem=("arbitrary",),
        args=(z, z, z, z, dmerged, dmerged, attn, sinks, conv_w),
        in_specs=[pl.BlockSpec((BLOCK, zw), lambda n: (cur(n), 0)),
                  pl.BlockSpec((BLOCK, kvw2), lambda n: (jnp.maximum(cur(n) - 1, 0), d // kvw2)),
                  pl.BlockSpec((SUBLANES_BF16, zw), lambda n: (jnp.maximum(cur(n) * halo - 1, 0), 0)),
                  pl.BlockSpec((SUBLANES_BF16, zw), lambda n: (jnp.minimum((cur(n) + 1) * halo, last_halo), 0)),
                  blk,
                  pl.BlockSpec((SUBLANES_BF16, d), lambda n: (jnp.minimum((cur(n) + 1) * halo, last_halo), 0)),
                  blk, SMEM_SPEC, pl.BlockSpec((3, d), lambda n: (0, 0))],
        out_specs=[pl.BlockSpec((BLOCK, zw), lambda n: (cur(n), 0)),
                   pl.BlockSpec((BLOCK, kvw2), lambda n: (jnp.maximum(n - 1, 0), 0)),
                   pl.BlockSpec((8, zw), lambda n: (0, 0)), pl.BlockSpec((8, kvw2), lambda n: (0, 0)),
                   pl.BlockSpec((8, d), lambda n: (0, 0)), pl.BlockSpec((8, LANES), lambda n: (0, 0))],
        out_shape=[SDS((t, zw), BF16), SDS((t, kvw2), BF16), SDS((8, zw), F32), SDS((8, kvw2), F32),
                   SDS((8, d), F32), SDS((8, LANES), F32)],
        scratch=[pltpu.VMEM((BLOCK, kvw2), F32)])


def _wgrad_in(dz, dkv, h1, tk, ride=None):
    t, zw = dz.shape
    d = h1.shape[1]
    kvw2 = dkv.shape[1]
    blk = d + kvw2
    assert zw % blk == 0
    tn = (((0,), (0,)), ((), ()))

    def body(a_ref, akv_ref, h_ref, o_ref, o16_ref):
        n, k = pl.program_id(0), pl.program_id(1)

        @pl.when(k == 0)
        def _():
            o_ref[...] = jnp.zeros_like(o_ref)

        @pl.when(n == 0)
        def _():
            o_ref[:d, :] += lax.dot_general(a_ref[:, :d], h_ref[...], tn, preferred_element_type=F32)
            o_ref[d:, :] += lax.dot_general(akv_ref[...], h_ref[...], tn, preferred_element_type=F32)

        @pl.when(n > 0)
        def _():
            o_ref[...] += lax.dot_general(a_ref[...], h_ref[...], tn, preferred_element_type=F32)

        @pl.when(k == t // tk - 1)
        def _():
            o16_ref[...] = o_ref[...].astype(BF16)

    out_spec = pl.BlockSpec((blk, d), lambda n, k: (n, 0))
    return _pallas(
        body, name="wgrad_in", grid=(zw // blk, t // tk),
        in_specs=[pl.BlockSpec((tk, blk), lambda n, k: (k, n)), pl.BlockSpec((tk, kvw2), lambda n, k: (k, 0)),
                  pl.BlockSpec((tk, d), lambda n, k: (k, 0))],
        out_specs=[out_spec, out_spec], out_shape=[SDS((zw, d), F32), SDS((zw, d), BF16)],
        args=(dz, dkv, h1), sem=("parallel", "arbitrary"), ride=ride)


def _in_proj_bwd(dz, dkv, wt, x, dx1, g_mix, sc1, tm, ride=None):
    t, d = x.shape
    zw = dz.shape[1]
    kvw2 = dkv.shape[1]
    rest = d + kvw2

    def body(a_ref, akv_ref, w_ref, x_ref, dx1_ref, g_ref, sc_ref, gx_ref, st_ref):
        @pl.when(pl.program_id(0) == 0)
        def _():
            st_ref[...] = jnp.zeros_like(st_ref)

        dh = (jnp.dot(a_ref[:, :d], w_ref[:d, :], preferred_element_type=F32)
              + jnp.dot(akv_ref[...], w_ref[d:rest, :], preferred_element_type=F32)
              + jnp.dot(a_ref[:, rest:], w_ref[rest:, :], preferred_element_type=F32))
        xx = x_ref[...]
        r = _rms(xx)
        xn = xx * r
        g = g_ref[...]
        dn = dh * (1.0 + sc_ref[...])
        u = dn * g
        gx_ref[...] = dx1_ref[...] + r * (u - xn * jnp.mean(u * xn, axis=-1, keepdims=True))
        st_ref[0:1, :] += jnp.sum(dh, axis=0, keepdims=True)
        st_ref[1:2, :] += jnp.sum(dh * (xn * g), axis=0, keepdims=True)
        st_ref[2:3, :] += jnp.sum(dn * xn, axis=0, keepdims=True)

    row = pl.BlockSpec((tm, d), lambda i: (i, 0))
    vec = pl.BlockSpec((1, d), lambda i: (0, 0))
    return _pallas(
        body, name="in_proj_bwd", grid=(t // tm,),
        in_specs=[pl.BlockSpec((tm, zw), lambda i: (i, 0)), pl.BlockSpec((tm, kvw2), lambda i: (i, 0)),
                  pl.BlockSpec((zw, d), lambda i: (0, 0), pipeline_mode=pl.Buffered(1)),
                  row, row, vec, vec],
        out_specs=[row, pl.BlockSpec((8, d), lambda i: (0, 0))],
        out_shape=[SDS((t, d), F32), SDS((8, d), F32)],
        args=(dz, dkv, wt, x, dx1, g_mix, sc1), sem=("arbitrary",), ride=ride)


def _to_lanes(v, rows=None):
    flat = v.reshape(-1)
    need = -(-flat.shape[0] // LANES)
    need = -(-need // 8) * 8 if rows is None else rows
    return jnp.pad(flat, (0, need * LANES - flat.shape[0])).reshape(need, LANES)


def kernel(x, c, w_ada, b_ada, g_mix, w_in, b_in, sinks, conv_w, w_out, g_ffn, w_ffn_in, w_ffn_out, g_final, loss_target, m_w_ada, m_b_ada, m_g_mix, m_w_in, m_b_in, m_sinks, m_conv_w, m_w_out, m_g_ffn, m_w_ffn_in, m_w_ffn_out, m_g_final, v_w_ada, v_b_ada, v_g_mix, v_w_in, v_b_in, v_sinks, v_conv_w, v_w_out, v_g_ffn, v_w_ffn_in, v_w_ffn_out, v_g_final):
    xs, tgt = x[0], loss_target[0]
    t, d = xs.shape
    zw = w_in.shape[2] * N_CHIP
    kvw2 = zw - 6 * d
    ff = w_ffn_out.shape[1] * N_CHIP
    n_mod = w_ada.shape[2] * N_CHIP // d
    mod_sh = w_ada.shape[2]
    cw_sh = conv_w.shape[2]
    assert d % (8 * LANES) == 0 and kvw2 == 2 * LANES and t % 512 == 0 and n_mod == 6
    xi, yi, ci = _mesh_pos()
    j_me = 2 * xi + yi
    b_me = 4 * xi + 2 * yi + ci
    pos = jnp.stack([ci, j_me]).astype(jnp.int32)
    tm = 512

    pack1 = jnp.concatenate([c.reshape(d // LANES, LANES), conv_w[0].reshape(-1, LANES)], axis=0)
    pack1 = jnp.pad(pack1, ((0, 16 - pack1.shape[0]), (0, 0)))
    g1 = _all_gather_small(pack1, "gather_c")
    c_all = g1[:, :d // LANES, :].reshape(N_DEV, d)
    cw_rows = 3 * cw_sh // LANES
    conv_w_full = jnp.concatenate(
        [g1[2 * j, d // LANES:d // LANES + cw_rows, :].reshape(3, cw_sh) for j in range(N_CHIP)], axis=1)
    b_ada_sh = lax.dynamic_slice(b_ada, (0, j_me * mod_sh), (1, mod_sh))
    mod_all = _all_gather_small(_ada_fwd(c_all, w_ada[0], b_ada_sh), "gather_mod")
    mod = jnp.concatenate([lax.dynamic_index_in_dim(mod_all[2 * j], b_me, 0, keepdims=True) for j in range(N_CHIP)],
                          axis=1)
    sh1, sc1, ga1, sh2, sc2, ga2 = [mod[:, k * d:(k + 1) * d] for k in range(6)]

    w_in_t, m_w_in_t, v_w_in_t = w_in[0].T, m_w_in[0].T, v_w_in[0].T
    (w_in_g,) = _gather_weights([_cast_into_block(pos, w_in_t, "cast_w_in")])
    w_in_tf = w_in_g.reshape(zw, d)
    later = [_cast_into_block(pos, w_out[0], "cast_w_out"), _cast_into_block(pos, w_ffn_in[0], "cast_w_ffn_in"),
             _cast_into_block(pos, w_ffn_out[0], "cast_w_ffn_out")]

    (z, h1), later = _in_proj(xs, g_mix, sc1, sh1, w_in_tf, b_in, min(t, 1024), zw // 5, ride=_x_gather_ici(later))
    (attn, merged), later = _mixer_fwd(z, sinks, conv_w_full, d, ride=_x_gather_d2d(later))
    w_out_f = later[0].reshape(d, d)
    w_ffn_in_f = later[1]
    w_ffn_out_f = later[2].reshape(ff, d)
    tml = min(t, 1024)
    y1, x1, h2 = _out_proj_fwd(merged, w_out_f, xs, ga1, g_ffn, sc2, sh2, tml)
    gu, act = _ffn_in_fwd(h2, w_ffn_in_f, ff, tml, ff // 2)
    dx2, dy2, st_loss = _ffn_out_loss(act, w_ffn_out_f, x1, tgt, ga2, g_final.reshape(1, d), tml)

    dgu = _ffn_out_bwd(dy2, w_ffn_out_f, gu, tml, ff // 2)
    tk = min(t, 2048)
    dw_ffn_out, _ = _wgrad(
        act, dy2, pl.BlockSpec((tk, ff // 2), lambda m, k: (k, m)), pl.BlockSpec((tk, d), lambda m, k: (k, 0)),
        pl.BlockSpec((ff // 2, d), lambda m, k: (m, 0)), SDS((ff, d), F32), (2, t // tk), "wgrad_ffn_out")
    dx1, dy1, st_ffn = _ffn_in_bwd(dgu, w_ffn_in_f, x1, dx2, y1, g_ffn, sc2, ga1, tm, ff // 2)
    dw_ffn_in, _ = _wgrad(
        h2, dgu, pl.BlockSpec((tk, d), lambda n, k: (k, 0)),
        pl.BlockSpec((None, tk, ff // 2), lambda n, k: (n // 2, k, n % 2)),
        pl.BlockSpec((None, d, ff // 2), lambda n, k: (n, 0, 0)), SDS((N_CHIP, d, ff // 2), F32),
        (N_CHIP, t // tk), "wgrad_ffn_in")
    dw_out, _ = _wgrad(
        merged, dy1, pl.BlockSpec((tk, d), lambda m, k: (k, 0)), pl.BlockSpec((tk, d), lambda m, k: (k, 0)),
        pl.BlockSpec((d, d), lambda m, k: (0, 0)), SDS((d, d), F32), (1, t // tk), "wgrad_out")

    early = [[g.reshape(N_CHIP, -1, g.shape[-1]) for g in pair] for pair in (dw_out, dw_ffn_in, dw_ffn_out)]
    early_names = ["w_out", "w_ffn_in", "w_ffn_out"]
    dmerged, _ = _out_proj_bwd(dy1, w_out_f, tml)
    (dz, dkv, db_z, db_kv, dcw, dsk), terms = _mixer_bwd(
        z, dmerged, attn, sinks, conv_w_full, d, ride=_x_reduce([e[0] for e in early], [e[1] for e in early]))
    fulls = [_sum_terms(pos, e[0], s, r, "sum_terms_" + nm)
             for e, s, r, nm in zip(early, terms[:3], terms[3:], early_names)]
    dw_in_t, (g_w_out, g_w_ffn_in, g_w_ffn_out) = _wgrad_in(dz, dkv, h1, tk, ride=_x_pair_exchange(fulls))
    dw_in_t = [g.reshape(N_CHIP, zw // N_CHIP, d) for g in dw_in_t]

    (grad_x, st_in), (from_sib, from_far) = _in_proj_bwd(dz, dkv, w_in_tf, xs, dx1, g_mix, sc1, tm,
                                                         ride=_x_reduce([dw_in_t[0]], [dw_in_t[1]]))
    (g_w_in_t,) = _exchange(_x_pair_exchange([_sum_terms(pos, dw_in_t[0], from_sib, from_far, "sum_terms_w_in")]),
                            "pair_exchange")

    dmod = jnp.concatenate([st_in[0:1], st_in[1:2], st_ffn[3:4], st_ffn[0:1], st_ffn[1:2], st_loss[0:1]], axis=1)
    db_in = jnp.concatenate([db_z[0:1, :d], db_kv[0:1], db_z[0:1, d + kvw2:]], axis=1)
    seg = [dmod, st_in[2:3], db_in, dsk[0:1], dcw[0:3].reshape(1, 3 * d), st_ffn[2:3], st_loss[1:2],
           st_loss[2:3, :LANES]]
    sizes = [s.shape[1] for s in seg]
    pack2 = _to_lanes(jnp.concatenate(seg, axis=1))
    packs = _all_gather_small(pack2, "gather_small_grads")
    tot = _pack_sum(packs).reshape(-1)
    offs = [sum(sizes[:k]) for k in range(len(sizes))]
    gb_ada, gg_mix, gb_in, gsinks, gcw, gg_ffn, gg_final, loss_v = [tot[o:o + s] for o, s in zip(offs, sizes)]
    loss = loss_v[0]
    gsinks = gsinks[:sinks.shape[1]]
    gcw_sh = lax.dynamic_slice(gcw.reshape(3, d), (0, j_me * cw_sh), (3, cw_sh))

    dmod_all = packs[:, :n_mod * d // LANES, :].reshape(N_DEV, n_mod * d)
    g_w_ada = _ada_wgrad(c_all, lax.dynamic_slice(dmod_all, (0, j_me * mod_sh), (N_DEV, mod_sh)))

    out_g, out_d, out_m, out_v = {}, {}, {}, {}
    big = {"w_ada": (w_ada[0], g_w_ada, m_w_ada[0], v_w_ada[0]),
           "w_out": (w_out[0], g_w_out, m_w_out[0], v_w_out[0]),
           "w_ffn_in": (w_ffn_in[0], g_w_ffn_in, m_w_ffn_in[0], v_w_ffn_in[0]),
           "w_ffn_out": (w_ffn_out[0], g_w_ffn_out, m_w_ffn_out[0], v_w_ffn_out[0])}
    for nm, (w, g, m, v) in big.items():
        out_g[nm], out_d[nm], out_m[nm], out_v[nm] = [o[None] for o in _adamw(w, g, m, v, "adamw_" + nm)]
    out_g["w_in"], out_d["w_in"], out_m["w_in"], out_v["w_in"] = [
        o.T[None] for o in _adamw(w_in_t, g_w_in_t, m_w_in_t, v_w_in_t, "adamw_w_in")]
    small = {"b_ada": (b_ada, gb_ada, m_b_ada, v_b_ada), "g_mix": (g_mix, gg_mix, m_g_mix, v_g_mix),
             "b_in": (b_in, gb_in, m_b_in, v_b_in), "sinks": (sinks, gsinks, m_sinks, v_sinks),
             "conv_w": (conv_w, gcw_sh, m_conv_w, v_conv_w), "g_ffn": (g_ffn, gg_ffn, m_g_ffn, v_g_ffn),
             "g_final": (g_final, gg_final, m_g_final, v_g_final)}
    s_sizes = [w.size for w, _, _, _ in small.values()]
    s_rows = -(-sum(s_sizes) // LANES // 8) * 8

    def s_pack(k):
        return _to_lanes(jnp.concatenate([tup[k].reshape(-1) for tup in small.values()]), s_rows)

    s_out = _adamw(s_pack(0), s_pack(1), s_pack(2), s_pack(3), "adamw_small")
    s_off = 0
    for (nm, (w, g, _, _)), sz in zip(small.items(), s_sizes):
        out_g[nm] = g.reshape(w.shape)
        out_d[nm], out_m[nm], out_v[nm] = [o.reshape(-1)[s_off:s_off + sz].reshape(w.shape) for o in s_out[1:]]
        s_off += sz

    order = ["w_ada", "b_ada", "g_mix", "w_in", "b_in", "sinks", "conv_w", "w_out", "g_ffn", "w_ffn_in", "w_ffn_out",
             "g_final"]
    return (loss, grad_x[None], *[out_g[k] for k in order], *[out_d[k] for k in order],
            *[out_m[k] for k in order], *[out_v[k] for k in order])
```

```python
import functools

import jax
import jax.numpy as jnp
from jax import lax
from jax.experimental import pallas as pl
from jax.experimental.pallas import tpu as pltpu

F32 = jnp.float32
BF16 = jnp.bfloat16
EPS = 1e-6
HEAD_DIM = 64
GROUP = 8
BLOCK = 128
LANES = 128
SUBLANES_BF16 = 16
N_DEV = 8
N_CHIP = 4
VMEM_LIMIT = 56 * 1024 * 1024
MESH = pl.DeviceIdType.MESH

ADAM_LR = 0.001
ADAM_B1 = 0.9
ADAM_B2 = 0.999
ADAM_EPS = 1e-08
ADAM_WD = 0.01
ADAM_STEP = 10

SDS = jax.ShapeDtypeStruct
ANY = pl.BlockSpec(memory_space=pl.ANY)
VMEM_SPEC = pl.BlockSpec(memory_space=pltpu.VMEM)
SMEM_SPEC = pl.BlockSpec(memory_space=pltpu.SMEM)


def _params(*sem):
    return pltpu.CompilerParams(dimension_semantics=sem, vmem_limit_bytes=VMEM_LIMIT)


def _mesh_pos():
    return lax.axis_index("x"), lax.axis_index("y"), lax.axis_index("c")


def _row_tile(rows, cols, itemsize=4, budget=1 << 20, mult=8):
    best = None
    for t in range(mult, rows + 1, mult):
        if rows % t == 0 and t * cols * itemsize <= budget:
            best = t
    if best is None:
        best = rows
    return best


def _all_gather_small(v, name):
    rows, cols = v.shape

    def body(v_ref, out_ref, send_sems, recv_sems, local_sem):
        x, y, c = _mesh_pos()
        me = 4 * x + 2 * y + c
        mine = pltpu.make_async_copy(v_ref, out_ref.at[me], local_sem)
        mine.start()
        peers = []
        for k in range(1, N_DEV):
            px = 1 - x if k & 4 else x
            py = 1 - y if k & 2 else y
            pc = 1 - c if k & 1 else c
            peers.append((px, py, pc))

        def copy(k, block):
            return pltpu.make_async_remote_copy(
                src_ref=v_ref, dst_ref=out_ref.at[block], send_sem=send_sems.at[k], recv_sem=recv_sems.at[k],
                device_id=peers[k], device_id_type=MESH)

        sends = [copy(k, me) for k in range(N_DEV - 1)]
        for cp in sends:
            cp.start()
        for k, (px, py, pc) in enumerate(peers):
            copy(k, 4 * px + 2 * py + pc).wait_recv()
        for cp in sends:
            cp.wait_send()
        mine.wait()

    return pl.pallas_call(
        body, name=name,
        out_shape=SDS((N_DEV, rows, cols), v.dtype),
        in_specs=[VMEM_SPEC], out_specs=VMEM_SPEC,
        scratch_shapes=[pltpu.SemaphoreType.DMA((N_DEV - 1,)), pltpu.SemaphoreType.DMA((N_DEV - 1,)),
                        pltpu.SemaphoreType.DMA],
    )(v)


def _other_chips(x, y):
    return [(1 - x, y), (x, 1 - y), (1 - x, 1 - y)]


def _gather_weights(bufs):
    n_w = len(bufs)

    def body(*refs):
        outs = refs[n_w:2 * n_w]
        send_sems, recv_sems, fsend_sems, frecv_sems = refs[2 * n_w:]
        x, y, c = _mesh_pos()
        j_me = 2 * x + y
        chips = _other_chips(x, y)
        sibling = (x, y, 1 - c)

        def half_rows(w, which):
            half = outs[w].shape[1] // 2
            return pl.ds(pl.multiple_of(which * half, SUBLANES_BF16), half)

        def copy(w, p, block, rows, over_ici):
            sems = (send_sems, recv_sems) if over_ici else (fsend_sems, frecv_sems)
            return pltpu.make_async_remote_copy(
                src_ref=outs[w].at[block, rows], dst_ref=outs[w].at[block, rows],
                send_sem=sems[0].at[w * 3 + p], recv_sem=sems[1].at[w * 3 + p],
                device_id=(*chips[p], c) if over_ici else sibling, device_id_type=MESH)

        def block_of(p):
            return 2 * chips[p][0] + chips[p][1]

        sends = [copy(w, p, j_me, half_rows(w, c), True) for w in range(n_w) for p in range(3)]
        for cp in sends:
            cp.start()
        forwards = []
        for w in range(n_w):
            for p in range(3):
                copy(w, p, block_of(p), half_rows(w, c), True).wait_recv()
                fw = copy(w, p, block_of(p), half_rows(w, c), False)
                fw.start()
                forwards.append(fw)
        for w in range(n_w):
            for p in range(3):
                copy(w, p, block_of(p), half_rows(w, 1 - c), False).wait_recv()
        for cp in sends + forwards:
            cp.wait_send()

    return pl.pallas_call(
        body, name="gather_weights",
        out_shape=[SDS(b.shape, b.dtype) for b in bufs],
        in_specs=[ANY] * n_w, out_specs=[ANY] * n_w,
        input_output_aliases={w: w for w in range(n_w)},
        scratch_shapes=[pltpu.SemaphoreType.DMA((3 * n_w,)), pltpu.SemaphoreType.DMA((3 * n_w,)),
                        pltpu.SemaphoreType.DMA((3 * n_w,)), pltpu.SemaphoreType.DMA((3 * n_w,))],
    )(*bufs)


class _Exchange:
    def __init__(self, operands, out_shape, in_place, n_sems, copies):
        self.operands, self.out_shape, self.in_place, self.n_sems, self.copies = (
            list(operands), list(out_shape), in_place, n_sems, copies)

    def sems(self):
        return [pltpu.SemaphoreType.DMA((self.n_sems,)), pltpu.SemaphoreType.DMA((self.n_sems,))]


def _x_gather_ici(bufs):
    def copies(ins, outs, send_sems, recv_sems):
        x, y, c = _mesh_pos()
        chips = _other_chips(x, y)
        out = []
        for w in range(len(outs)):
            half = outs[w].shape[1] // 2
            rows = pl.ds(pl.multiple_of(c * half, SUBLANES_BF16), half)
            for p in range(3):
                out.append(pltpu.make_async_remote_copy(
                    src_ref=outs[w].at[2 * x + y, rows], dst_ref=outs[w].at[2 * x + y, rows],
                    send_sem=send_sems.at[w * 3 + p], recv_sem=recv_sems.at[w * 3 + p],
                    device_id=(*chips[p], c), device_id_type=MESH))
        return out

    return _Exchange(bufs, [SDS(b.shape, b.dtype) for b in bufs], True, 3 * len(bufs), copies)


def _x_gather_d2d(bufs):
    def copies(ins, outs, send_sems, recv_sems):
        x, y, c = _mesh_pos()
        chips = _other_chips(x, y)
        out = []
        for w in range(len(outs)):
            half = outs[w].shape[1] // 2
            rows = pl.ds(pl.multiple_of(c * half, SUBLANES_BF16), half)
            for p in range(3):
                block = 2 * chips[p][0] + chips[p][1]
                out.append(pltpu.make_async_remote_copy(
                    src_ref=outs[w].at[block, rows], dst_ref=outs[w].at[block, rows],
                    send_sem=send_sems.at[w * 3 + p], recv_sem=recv_sems.at[w * 3 + p],
                    device_id=(x, y, 1 - c), device_id_type=MESH))
        return out

    return _Exchange(bufs, [SDS(b.shape, b.dtype) for b in bufs], True, 3 * len(bufs), copies)


N_REMOTE = 6


def _x_reduce(grads32, grads16):
    n_w = len(grads32)

    def copies(ins, outs, send_sems, recv_sems):
        g32, g16 = ins[:n_w], ins[n_w:]
        from_sib, from_far = outs[:n_w], outs[n_w:]
        x, y, c = _mesh_pos()
        chips = _other_chips(x, y)
        out = []
        for w in range(n_w):
            half = g32[w].shape[1] // 2
            k0 = w * (N_REMOTE + 1)
            out.append(pltpu.make_async_remote_copy(
                src_ref=g32[w].at[2 * x + y, pl.ds(pl.multiple_of((1 - c) * half, SUBLANES_BF16), half), :],
                dst_ref=from_sib[w], send_sem=send_sems.at[k0], recv_sem=recv_sems.at[k0],
                device_id=(x, y, 1 - c), device_id_type=MESH))
            for p in range(3):
                for f in range(2):
                    tc = c if f == 0 else 1 - c
                    k = 2 * p + f
                    out.append(pltpu.make_async_remote_copy(
                        src_ref=g16[w].at[2 * chips[p][0] + chips[p][1],
                                          pl.ds(pl.multiple_of(tc * half, SUBLANES_BF16), half), :],
                        dst_ref=from_far[w].at[k], send_sem=send_sems.at[k0 + 1 + k], recv_sem=recv_sems.at[k0 + 1 + k],
                        device_id=(*chips[p], tc), device_id_type=MESH))
        return out

    shapes = ([SDS((g.shape[1] // 2, g.shape[2]), g.dtype) for g in grads32]
              + [SDS((N_REMOTE, g.shape[1] // 2, g.shape[2]), g.dtype) for g in grads16])
    return _Exchange(list(grads32) + list(grads16), shapes, False, (N_REMOTE + 1) * n_w, copies)


def _x_pair_exchange(fulls):
    def copies(ins, outs, send_sems, recv_sems):
        x, y, c = _mesh_pos()
        out = []
        for w in range(len(outs)):
            half = outs[w].shape[0] // 2
            rows = pl.ds(pl.multiple_of(c * half, 8), half)
            out.append(pltpu.make_async_remote_copy(
                src_ref=outs[w].at[rows], dst_ref=outs[w].at[rows], send_sem=send_sems.at[w],
                recv_sem=recv_sems.at[w], device_id=(x, y, 1 - c), device_id_type=MESH))
        return out

    return _Exchange(fulls, [SDS(f.shape, f.dtype) for f in fulls], True, len(fulls), copies)


def _pallas(body, *, name, grid, in_specs, out_specs, out_shape, args, scratch=(), sem=None, ride=None):
    single = not isinstance(out_specs, (list, tuple))
    out_specs_l = [out_specs] if single else list(out_specs)
    out_shape_l = [out_shape] if single else list(out_shape)
    n_in, n_out, n_scr = len(in_specs), len(out_specs_l), len(scratch)
    if ride is None:
        res = pl.pallas_call(body, name=name, grid=grid, in_specs=list(in_specs), out_specs=out_specs,
                             out_shape=out_shape, scratch_shapes=list(scratch), compiler_params=_params(*sem))(*args)
        return res, None
    n_x, n_xo = len(ride.operands), len(ride.out_shape)

    def full_body(*refs):
        ins, x_ins = refs[:n_in], refs[n_in:n_in + n_x]
        outs = refs[n_in + n_x:n_in + n_x + n_out]
        x_outs = refs[n_in + n_x + n_out:n_in + n_x + n_out + n_xo]
        rest = refs[n_in + n_x + n_out + n_xo:]
        scr, (send_sems, recv_sems) = rest[:n_scr], rest[n_scr:]
        first = functools.reduce(jnp.logical_and, [pl.program_id(a) == 0 for a in range(len(grid))])
        last = functools.reduce(jnp.logical_and, [pl.program_id(a) == grid[a] - 1 for a in range(len(grid))])

        @pl.when(first)
        def _():
            for cp in ride.copies(x_ins, x_outs, send_sems, recv_sems):
                cp.start()

        body(*ins, *outs, *scr)

        @pl.when(last)
        def _():
            for cp in ride.copies(x_ins, x_outs, send_sems, recv_sems):
                cp.wait()

    res = pl.pallas_call(
        full_body, name=name, grid=grid, in_specs=list(in_specs) + [ANY] * n_x,
        out_specs=out_specs_l + [ANY] * n_xo, out_shape=out_shape_l + ride.out_shape,
        input_output_aliases={n_in + k: n_out + k for k in range(n_x)} if ride.in_place else {},
        scratch_shapes=list(scratch) + ride.sems(),
        compiler_params=_params(*(["arbitrary"] * len(grid))))(*args, *ride.operands)
    own = res[0] if single else list(res[:n_out])
    return own, list(res[n_out:])


def _exchange(ride, name):
    n_x, n_xo = len(ride.operands), len(ride.out_shape)

    def body(*refs):
        x_ins, x_outs = refs[:n_x], refs[n_x:n_x + n_xo]
        send_sems, recv_sems = refs[n_x + n_xo:]
        copies = ride.copies(x_ins, x_outs, send_sems, recv_sems)
        for cp in copies:
            cp.start()
        for cp in copies:
            cp.wait()

    return pl.pallas_call(
        body, name=name, in_specs=[ANY] * n_x, out_specs=[ANY] * n_xo, out_shape=ride.out_shape,
        input_output_aliases={k: k for k in range(n_x)} if ride.in_place else {},
        scratch_shapes=ride.sems())(*ride.operands)


def _cast_into_block(pos, w, name):
    rows, cols = w.shape
    tr = _row_tile(rows, cols, mult=SUBLANES_BF16)

    def body(pos_ref, w_ref, o_ref):
        del pos_ref
        o_ref[...] = w_ref[...].astype(BF16)

    return pl.pallas_call(
        body, name=name,
        grid_spec=pltpu.PrefetchScalarGridSpec(
            num_scalar_prefetch=1, grid=(rows // tr,),
            in_specs=[pl.BlockSpec((tr, cols), lambda i, pos_ref: (i, 0))],
            out_specs=pl.BlockSpec((None, tr, cols), lambda i, pos_ref: (pos_ref[1], i, 0))),
        out_shape=SDS((N_CHIP, rows, cols), BF16), compiler_params=_params("parallel"))(pos, w)


def _sum_terms(pos, grad, from_sib, from_far, name):
    _, rows, cols = grad.shape
    half = rows // 2
    tr = _row_tile(half, cols, mult=SUBLANES_BF16)
    nblk = half // tr

    def body(pos_ref, g_ref, s_ref, r_ref, o_ref):
        del pos_ref
        acc = g_ref[...] + s_ref[...]
        for k in range(N_REMOTE):
            acc = acc + r_ref[k].astype(F32)
        o_ref[...] = acc

    return pl.pallas_call(
        body, name=name,
        grid_spec=pltpu.PrefetchScalarGridSpec(
            num_scalar_prefetch=1, grid=(nblk,),
            in_specs=[pl.BlockSpec((None, tr, cols), lambda i, pos_ref: (pos_ref[1], pos_ref[0] * nblk + i, 0)),
                      pl.BlockSpec((tr, cols), lambda i, pos_ref: (i, 0)),
                      pl.BlockSpec((N_REMOTE, tr, cols), lambda i, pos_ref: (0, i, 0))],
            out_specs=pl.BlockSpec((tr, cols), lambda i, pos_ref: (pos_ref[0] * nblk + i, 0))),
        out_shape=SDS((rows, cols), F32),
        compiler_params=_params("parallel"),
    )(pos, grad, from_sib, from_far)


def _adamw(w, g, m, v, name):
    rows, cols = w.shape
    tr = _row_tile(rows, cols)

    def body(w_ref, g_ref, m_ref, v_ref, go_ref, d_ref, nm_ref, nv_ref):
        gg = g_ref[...]
        go_ref[...] = gg
        nm = ADAM_B1 * m_ref[...] + (1.0 - ADAM_B1) * gg
        nv = ADAM_B2 * v_ref[...] + (1.0 - ADAM_B2) * (gg * gg)
        m_hat = nm / (1.0 - ADAM_B1 ** ADAM_STEP)
        v_hat = nv / (1.0 - ADAM_B2 ** ADAM_STEP)
        d_ref[...] = -ADAM_LR * (m_hat / (jnp.sqrt(v_hat) + ADAM_EPS) + ADAM_WD * w_ref[...])
        nm_ref[...] = nm
        nv_ref[...] = nv

    spec = pl.BlockSpec((tr, cols), lambda i: (i, 0))
    return pl.pallas_call(body, name=name, grid=(rows // tr,), in_specs=[spec] * 4, out_specs=[spec] * 4,
                          out_shape=[SDS((rows, cols), F32)] * 4, compiler_params=_params("parallel"))(w, g, m, v)


def _pack_sum(gathered):
    _, rows, cols = gathered.shape

    def body(g_ref, o_ref):
        acc = g_ref[0]
        for d in range(1, N_DEV):
            acc = acc + g_ref[d]
        o_ref[...] = acc

    return pl.pallas_call(body, name="pack_sum", in_specs=[VMEM_SPEC], out_specs=VMEM_SPEC,
                          out_shape=SDS((rows, cols), F32))(gathered)


def _ada_fwd(c_all, w_sh, b_sh):
    d, n = w_sh.shape
    tn = 512

    def body(c_ref, w_ref, b_ref, o_ref):
        cc = c_ref[...]
        s = (cc * jax.nn.sigmoid(cc)).astype(BF16)
        o_ref[...] = jnp.dot(s, w_ref[...].astype(BF16), preferred_element_type=F32) + b_ref[...]

    return pl.pallas_call(
        body, name="ada_fwd", grid=(n // tn,),
        in_specs=[pl.BlockSpec((N_DEV, d), lambda j: (0, 0)), pl.BlockSpec((d, tn), lambda j: (0, j)),
                  pl.BlockSpec((1, tn), lambda j: (0, j))],
        out_specs=pl.BlockSpec((N_DEV, tn), lambda j: (0, j)),
        out_shape=SDS((N_DEV, n), F32), compiler_params=_params("parallel"))(c_all, w_sh, b_sh)


def _ada_wgrad(c_all, dmod_sh):
    d = c_all.shape[1]
    n = dmod_sh.shape[1]
    tn = 512

    def body(c_ref, g_ref, o_ref):
        cc = c_ref[...]
        s = cc * jax.nn.sigmoid(cc)
        o_ref[...] = lax.dot_general(s, g_ref[...], (((0,), (0,)), ((), ())), preferred_element_type=F32,
                                     precision=lax.Precision.HIGHEST)

    return pl.pallas_call(
        body, name="ada_wgrad", grid=(n // tn,),
        in_specs=[pl.BlockSpec((N_DEV, d), lambda j: (0, 0)), pl.BlockSpec((N_DEV, tn), lambda j: (0, j))],
        out_specs=pl.BlockSpec((d, tn), lambda j: (0, j)),
        out_shape=SDS((d, n), F32), compiler_params=_params("parallel"))(c_all, dmod_sh)


def _rms(xf):
    return lax.rsqrt(jnp.mean(xf * xf, axis=-1, keepdims=True) + EPS)


def _in_proj(x, g, sc, sh, wt, b, tm, tn, ride=None):
    t, d = x.shape
    n = wt.shape[0]

    def body(x_ref, g_ref, sc_ref, sh_ref, w_ref, b_ref, z_ref, h_ref):
        @pl.when(pl.program_id(1) == 0)
        def _():
            xf = x_ref[...]
            h_ref[...] = ((xf * _rms(xf) * g_ref[...]) * (1.0 + sc_ref[...]) + sh_ref[...]).astype(BF16)

        acc = lax.dot_general(h_ref[...], w_ref[...], (((1,), (1,)), ((), ())), preferred_element_type=F32)
        z_ref[...] = (acc + b_ref[...]).astype(BF16)

    row = pl.BlockSpec((tm, d), lambda i, j: (i, 0))
    vec = pl.BlockSpec((1, d), lambda i, j: (0, 0))
    return _pallas(
        body, name="in_proj", grid=(t // tm, n // tn),
        in_specs=[row, vec, vec, vec, pl.BlockSpec((tn, d), lambda i, j: (j, 0)),
                  pl.BlockSpec((1, tn), lambda i, j: (0, j))],
        out_specs=[pl.BlockSpec((tm, tn), lambda i, j: (i, j)), row],
        out_shape=[SDS((t, n), BF16), SDS((t, d), BF16)], args=(x, g, sc, sh, wt, b),
        sem=("parallel", "arbitrary"), ride=ride)


def _segments(d, kvw2):
    o = d + kvw2
    names = ("cb", "cc", "cx", "ga", "gc")
    seg = {nm: slice(o + k * d, o + (k + 1) * d) for k, nm in enumerate(names)}
    seg["q"], seg["kv"] = slice(0, d), slice(d, o)
    return seg


def _attn_masks():
    rows = 4 * BLOCK
    r = lax.broadcasted_iota(jnp.int32, (rows, 2 * BLOCK), 0) & (BLOCK - 1)
    col = lax.broadcasted_iota(jnp.int32, (rows, 2 * BLOCK), 1)
    return (col > r) & (col <= r + BLOCK), col


def _kv_variants(kv, n_kv_w):
    assert n_kv_w == LANES
    kb, vb = kv[:, :LANES] * (HEAD_DIM ** -0.5), kv[:, LANES:]
    kr, vr = pltpu.roll(kb, HEAD_DIM, 1), pltpu.roll(vb, HEAD_DIM, 1)
    lane = lax.broadcasted_iota(jnp.int32, kb.shape, 1)
    lo = lane < HEAD_DIM
    zero = jnp.zeros_like(kb)
    k_eff = [[None, None], [None, None]]
    v_eff = [[None, None], [None, None]]
    for h in range(2):
        for e in range(2):
            ksrc, vsrc = (kb, vb) if e == h else (kr, vr)
            keep = lo if e == 0 else jnp.logical_not(lo)
            k_eff[h][e] = jnp.where(keep, ksrc, zero)
            v_eff[h][e] = jnp.where(keep, vsrc, zero)
    return k_eff, v_eff


def _sink_column(sinks_ref, h, e):
    rowblk = lax.broadcasted_iota(jnp.int32, (4 * BLOCK, 1), 0) // BLOCK
    col = jnp.zeros((4 * BLOCK, 1), F32)
    for j in range(4):
        col = jnp.where(rowblk == j, sinks_ref[0, GROUP * h + 2 * j + e], col)
    return col


def _softmax_sink(s, valid, sink):
    s = jnp.where(valid, s, -jnp.inf)
    m = jnp.maximum(jnp.max(s, axis=-1, keepdims=True), sink)
    p = jnp.exp(s - m)
    psink = jnp.exp(sink - m)
    den = jnp.sum(p, axis=-1, keepdims=True) + psink
    inv = 1.0 / den
    return p * inv, psink * inv


def _shift_down(a, s, prev):
    rows = a.shape[0]
    out = pltpu.roll(a, s, 0)
    row = lax.broadcasted_iota(jnp.int32, a.shape, 0)
    for t in range(s):
        out = jnp.where(row == t, prev[SUBLANES_BF16 - s + t:SUBLANES_BF16 - s + t + 1, :], out)
    del rows
    return out


def _shift_up(a, s, nxt):
    rows = a.shape[0]
    out = pltpu.roll(a, rows - s, 0)
    row = lax.broadcasted_iota(jnp.int32, a.shape, 0)
    for t in range(s):
        out = jnp.where(row == rows - s + t, nxt[t:t + 1, :], out)
    return out


def _stack_pairs(ref, h, rows=slice(None)):
    return jnp.concatenate([ref[rows, (4 * h + j) * LANES:(4 * h + j + 1) * LANES] for j in range(4)], axis=0)


FWD_BLOCKS = 4


def _mixer_fwd(z, sinks, conv_w, d, ride=None):
    t, zw = z.shape
    kvw2 = zw - 6 * d
    tq = FWD_BLOCKS * BLOCK
    halo = tq // SUBLANES_BF16
    seg = _segments(d, kvw2)

    def body(z_ref, kvp_ref, prev_ref, sinks_ref, cw_ref, attn_ref, merged_ref):
        n = pl.program_id(0)
        band, col = _attn_masks()
        for b in range(FWD_BLOCKS):
            rows = slice(b * BLOCK, (b + 1) * BLOCK)
            before = slice((b - 1) * BLOCK, b * BLOCK)
            kv_prev = kvp_ref[...] if b == 0 else z_ref[before, seg["kv"]]
            kv = jnp.concatenate([kv_prev, z_ref[rows, seg["kv"]]], axis=0)
            k_eff, v_eff = _kv_variants(kv, kvw2 // 2)
            valid = band & ((n > 0) | (col >= BLOCK)) if b == 0 else band
            for h in range(2):
                q4 = _stack_pairs(z_ref, h, rows)
                o4 = jnp.zeros((4 * BLOCK, LANES), F32)
                for e in range(2):
                    s = lax.dot_general(q4, k_eff[h][e], (((1,), (1,)), ((), ())), preferred_element_type=F32)
                    p, _ = _softmax_sink(s, valid, _sink_column(sinks_ref, h, e))
                    o4 = o4 + jnp.dot(p.astype(BF16), v_eff[h][e], preferred_element_type=F32)
                for j in range(4):
                    attn_ref[rows, (4 * h + j) * LANES:(4 * h + j + 1) * LANES] = (
                        o4[j * BLOCK:(j + 1) * BLOCK].astype(BF16))
            cb = z_ref[rows, seg["cb"]].astype(F32)
            p_in = z_ref[rows, seg["cc"]].astype(F32) * z_ref[rows, seg["cx"]].astype(F32)
            if b == 0:
                prev = jnp.where(n > 0, prev_ref[:, seg["cc"]].astype(F32) * prev_ref[:, seg["cx"]].astype(F32), 0.0)
            else:
                tail = slice(b * BLOCK - SUBLANES_BF16, b * BLOCK)
                prev = z_ref[tail, seg["cc"]].astype(F32) * z_ref[tail, seg["cx"]].astype(F32)
            cconv = (cw_ref[0:1, :] * _shift_down(p_in, 2, prev) + cw_ref[1:2, :] * _shift_down(p_in, 1, prev)
                     + cw_ref[2:3, :] * p_in)
            sa = jax.nn.sigmoid(z_ref[rows, seg["ga"]].astype(F32))
            sg = jax.nn.sigmoid(z_ref[rows, seg["gc"]].astype(F32))
            merged_ref[rows, :] = (sa * attn_ref[rows, :].astype(F32) + sg * (cb * cconv)).astype(BF16)

    blk = pl.BlockSpec((tq, d), lambda n: (n, 0))
    return _pallas(
        body, name="mixer_fwd", grid=(t // tq,),
        in_specs=[pl.BlockSpec((tq, zw), lambda n: (n, 0)),
                  pl.BlockSpec((BLOCK, kvw2), lambda n: (jnp.maximum(n * FWD_BLOCKS - 1, 0), d // kvw2)),
                  pl.BlockSpec((SUBLANES_BF16, zw), lambda n: (jnp.maximum(n * halo - 1, 0), 0)),
                  SMEM_SPEC, pl.BlockSpec((3, d), lambda n: (0, 0))],
        out_specs=[blk, blk],
        out_shape=[SDS((t, d), BF16), SDS((t, d), BF16)],
        args=(z, z, z, sinks, conv_w), sem=("parallel",), ride=ride)


def _out_proj_fwd(merged, w_out, x, ga1, g_ffn, sc2, sh2, tm):
    t, d = x.shape

    def body(m_ref, w_ref, x_ref, ga_ref, g_ref, sc_ref, sh_ref, y_ref, x1_ref, h_ref):
        y = jnp.dot(m_ref[...], w_ref[...], preferred_element_type=F32)
        x1 = x_ref[...] + ga_ref[...] * y
        y_ref[...] = y.astype(BF16)
        x1_ref[...] = x1
        h_ref[...] = ((x1 * _rms(x1) * g_ref[...]) * (1.0 + sc_ref[...]) + sh_ref[...]).astype(BF16)

    row = pl.BlockSpec((tm, d), lambda i: (i, 0))
    vec = pl.BlockSpec((1, d), lambda i: (0, 0))
    return pl.pallas_call(
        body, name="out_proj_fwd", grid=(t // tm,),
        in_specs=[row, pl.BlockSpec((d, d), lambda i: (0, 0)), row, vec, vec, vec, vec],
        out_specs=[row, row, row],
        out_shape=[SDS((t, d), BF16), SDS((t, d), F32), SDS((t, d), BF16)],
        compiler_params=_params("parallel"))(merged, w_out, x, ga1, g_ffn, sc2, sh2)


def _ffn_in_fwd(h2, w, ff, tm, tn):
    t, d = h2.shape
    nj = ff // tn
    assert w.shape == (2 * nj, d, tn)

    def body(h_ref, wg_ref, wu_ref, gu_ref, act_ref):
        hh = h_ref[...]
        g = jnp.dot(hh, wg_ref[...], preferred_element_type=F32)
        u = jnp.dot(hh, wu_ref[...], preferred_element_type=F32)
        gu_ref[0] = g.astype(BF16)
        gu_ref[1] = u.astype(BF16)
        act_ref[...] = ((g * jax.nn.sigmoid(g)) * u).astype(BF16)

    return pl.pallas_call(
        body, name="ffn_in_fwd", grid=(nj, t // tm),
        in_specs=[pl.BlockSpec((tm, d), lambda j, i: (i, 0)), pl.BlockSpec((None, d, tn), lambda j, i: (j, 0, 0)),
                  pl.BlockSpec((None, d, tn), lambda j, i: (j + nj, 0, 0))],
        out_specs=[pl.BlockSpec((2, tm, tn), lambda j, i: (0, i, j)), pl.BlockSpec((tm, tn), lambda j, i: (i, j))],
        out_shape=[SDS((2, t, ff), BF16), SDS((t, ff), BF16)],
        compiler_params=_params("parallel", "parallel"))(h2, w, w)


def _ffn_out_loss(act, w, x1, target, ga2, g_final, tm):
    t, d = x1.shape
    ff = act.shape[1]

    def body(a_ref, w_ref, x1_ref, tg_ref, ga_ref, gf_ref, dx2_ref, dy2_ref, st_ref):
        @pl.when(pl.program_id(0) == 0)
        def _():
            st_ref[...] = jnp.zeros_like(st_ref)

        halves = [slice(k * (tm // 2), (k + 1) * (tm // 2)) for k in range(2)]
        y2s = [jnp.dot(a_ref[rows, :], w_ref[...], preferred_element_type=F32) for rows in halves]
        for rows, y2 in zip(halves, y2s):
            x2 = x1_ref[rows, :] + ga_ref[...] * y2
            r = _rms(x2)
            yn = x2 * r
            err = yn * gf_ref[...] - tg_ref[rows, :]
            loss = 0.5 * jnp.sum(jnp.mean(err * err, axis=-1, keepdims=True), axis=0, keepdims=True)
            dy = err * (1.0 / d)
            u = dy * gf_ref[...]
            dx2 = r * (u - yn * jnp.mean(u * yn, axis=-1, keepdims=True))
            dx2_ref[rows, :] = dx2
            dy2_ref[rows, :] = (ga_ref[...] * dx2).astype(BF16)
            st_ref[0:1, :] += jnp.sum(dx2 * y2, axis=0, keepdims=True)
            st_ref[1:2, :] += jnp.sum(dy * yn, axis=0, keepdims=True)
            st_ref[2:3, :] += jnp.broadcast_to(loss, (1, d))

    row = pl.BlockSpec((tm, d), lambda i: (i, 0))
    vec = pl.BlockSpec((1, d), lambda i: (0, 0))
    return pl.pallas_call(
        body, name="ffn_out_loss", grid=(t // tm,),
        in_specs=[pl.BlockSpec((tm, ff), lambda i: (i, 0)),
                  pl.BlockSpec((ff, d), lambda i: (0, 0), pipeline_mode=pl.Buffered(1)), row, row, vec, vec],
        out_specs=[row, row, pl.BlockSpec((8, d), lambda i: (0, 0))],
        out_shape=[SDS((t, d), F32), SDS((t, d), BF16), SDS((8, d), F32)],
        compiler_params=_params("arbitrary"))(act, w, x1, target, ga2, g_final)


def _ffn_out_bwd(dy2, w, gu, tm, tn):
    t, d = dy2.shape
    ff = w.shape[0]

    def body(dy_ref, w_ref, gu_ref, o_ref):
        dy = dy_ref[...]
        for lo in range(0, tn, 3 * LANES):
            cols = slice(lo, min(lo + 3 * LANES, tn))
            dact = lax.dot_general(dy, w_ref[cols, :], (((1,), (1,)), ((), ())), preferred_element_type=F32)
            g = gu_ref[0, :, cols].astype(F32)
            u = gu_ref[1, :, cols].astype(F32)
            sg = jax.nn.sigmoid(g)
            a = dact * sg
            du = a * g
            o_ref[0, :, cols] = (u * (a + du * (1.0 - sg))).astype(BF16)
            o_ref[1, :, cols] = du.astype(BF16)

    gu_spec = pl.BlockSpec((2, tm, tn), lambda j, i: (0, i, j))
    return pl.pallas_call(
        body, name="ffn_out_bwd", grid=(ff // tn, t // tm),
        in_specs=[pl.BlockSpec((tm, d), lambda j, i: (i, 0)), pl.BlockSpec((tn, d), lambda j, i: (j, 0)), gu_spec],
        out_specs=gu_spec, out_shape=SDS((2, t, ff), BF16),
        compiler_params=_params("parallel", "parallel"))(dy2, w, gu)


def _wgrad(a, b, a_spec, b_spec, out_spec, out_shape, grid, name, ride=None):
    def body(a_ref, b_ref, o_ref, o16_ref):
        k = pl.program_id(len(grid) - 1)

        @pl.when(k == 0)
        def _():
            o_ref[...] = jnp.zeros_like(o_ref)

        o_ref[...] += lax.dot_general(a_ref[...], b_ref[...], (((0,), (0,)), ((), ())), preferred_element_type=F32)

        @pl.when(k == grid[-1] - 1)
        def _():
            o16_ref[...] = o_ref[...].astype(BF16)

    return _pallas(
        body, name=name, grid=grid, in_specs=[a_spec, b_spec], out_specs=[out_spec, out_spec],
        out_shape=[out_shape, SDS(out_shape.shape, BF16)], args=(a, b),
        sem=["parallel"] * (len(grid) - 1) + ["arbitrary"], ride=ride)


def _ffn_in_bwd(dgu, w, x1, dx2, y1, g_ffn, sc2, ga1, tm, tk):
    t, d = x1.shape
    ff = dgu.shape[2]
    n_sh, _, sw = w.shape
    per = ff // sw
    del tk
    nt = (((1,), (1,)), ((), ()))

    def body(a_ref, w_ref, x1_ref, dx2_ref, y1_ref, g_ref, sc_ref, ga_ref, dx1_ref, dy1_ref, st_ref):
        @pl.when(pl.program_id(0) == 0)
        def _():
            st_ref[...] = jnp.zeros_like(st_ref)

        dh = None
        for j in range(n_sh):
            part = lax.dot_general(a_ref[j // per, :, (j % per) * sw:(j % per + 1) * sw], w_ref[j], nt,
                                   preferred_element_type=F32)
            dh = part if dh is None else dh + part
        x1 = x1_ref[...]
        r = _rms(x1)
        xn = x1 * r
        g = g_ref[...]
        dn = dh * (1.0 + sc_ref[...])
        u = dn * g
        dx1 = dx2_ref[...] + r * (u - xn * jnp.mean(u * xn, axis=-1, keepdims=True))
        dx1_ref[...] = dx1
        dy1_ref[...] = (ga_ref[...] * dx1).astype(BF16)
        st_ref[0:1, :] += jnp.sum(dh, axis=0, keepdims=True)
        st_ref[1:2, :] += jnp.sum(dh * (xn * g), axis=0, keepdims=True)
        st_ref[2:3, :] += jnp.sum(dn * xn, axis=0, keepdims=True)
        st_ref[3:4, :] += jnp.sum(dx1 * y1_ref[...].astype(F32), axis=0, keepdims=True)

    row = pl.BlockSpec((tm, d), lambda i: (i, 0))
    vec = pl.BlockSpec((1, d), lambda i: (0, 0))
    return pl.pallas_call(
        body, name="ffn_in_bwd", grid=(t // tm,),
        in_specs=[pl.BlockSpec((2, tm, ff), lambda i: (0, i, 0)),
                  pl.BlockSpec((n_sh, d, sw), lambda i: (0, 0, 0), pipeline_mode=pl.Buffered(1)),
                  row, row, row, vec, vec, vec],
        out_specs=[row, row, pl.BlockSpec((8, d), lambda i: (0, 0))],
        out_shape=[SDS((t, d), F32), SDS((t, d), BF16), SDS((8, d), F32)],
        compiler_params=_params("arbitrary"))(dgu, w, x1, dx2, y1, g_ffn, sc2, ga1)


def _out_proj_bwd(dy1, w_out, tm, ride=None):
    t, d = dy1.shape

    def body(dy_ref, w_ref, o_ref):
        o_ref[...] = lax.dot_general(dy_ref[...], w_ref[...], (((1,), (1,)), ((), ())),
                                     preferred_element_type=F32).astype(BF16)

    row = pl.BlockSpec((tm, d), lambda i: (i, 0))
    return _pallas(body, name="out_proj_bwd", grid=(t // tm,),
                   in_specs=[row, pl.BlockSpec((d, d), lambda i: (0, 0))], out_specs=row,
                   out_shape=SDS((t, d), BF16), args=(dy1, w_out), sem=("parallel",), ride=ride)


BWD_BLOCKS = 2


def _mixer_bwd(z, dmerged, attn, sinks, conv_w, d, ride=None):
    t, zw = z.shape
    kvw2 = zw - 6 * d
    tq = BWD_BLOCKS * BLOCK
    steps = t // tq
    halo = tq // SUBLANES_BF16
    last_halo = t // SUBLANES_BF16 - 1
    scale = HEAD_DIM ** -0.5
    seg = _segments(d, kvw2)

    def body(z_ref, kvp_ref, prev_ref, next_ref, dm_ref, dmn_ref, attn_ref, sinks_ref, cw_ref,
             dz_ref, dkv_ref, db_ref, dbkv_ref, dcw_ref, dsk_ref, carry_ref):
        n = pl.program_id(0)

        @pl.when(n == 0)
        def _():
            carry_ref[...] = jnp.zeros_like(carry_ref)
            db_ref[...] = jnp.zeros_like(db_ref)
            dbkv_ref[...] = jnp.zeros_like(dbkv_ref)
            dcw_ref[...] = jnp.zeros_like(dcw_ref)
            dsk_ref[...] = jnp.zeros_like(dsk_ref)

        def one_block(b, pending):
            rows = slice(b * BLOCK, (b + 1) * BLOCK)
            before = slice((b - 1) * BLOCK, b * BLOCK)
            dm = dm_ref[rows, :].astype(F32)
            sa = jax.nn.sigmoid(z_ref[rows, seg["ga"]].astype(F32))
            dga = dm * attn_ref[rows, :].astype(F32) * sa * (1.0 - sa)
            dz_ref[rows, seg["ga"]] = dga.astype(BF16)
            db_ref[0:1, seg["ga"]] += jnp.sum(dga, axis=0, keepdims=True)
            dattn = (dm * sa).astype(BF16)

            kv_prev = kvp_ref[...] if b == 0 else z_ref[before, seg["kv"]]
            kv = jnp.concatenate([kv_prev, z_ref[rows, seg["kv"]]], axis=0)
            k_eff, v_eff = _kv_variants(kv, kvw2 // 2)
            band, col = _attn_masks()
            valid = band & ((n > 0) | (col >= BLOCK)) if b == 0 else band
            lane_lo = lax.broadcasted_iota(jnp.int32, (2 * BLOCK, LANES), 1) < HEAD_DIM
            sink_lane = lax.broadcasted_iota(jnp.int32, (1, LANES), 1)
            rowblk = lax.broadcasted_iota(jnp.int32, (4 * BLOCK, 1), 0) // BLOCK
            dk_acc = [jnp.zeros((2 * BLOCK, LANES), F32), jnp.zeros((2 * BLOCK, LANES), F32)]
            dv_acc = [jnp.zeros((2 * BLOCK, LANES), F32), jnp.zeros((2 * BLOCK, LANES), F32)]
            dsink = jnp.zeros((1, LANES), F32)
            for h in range(2):
                q4 = _stack_pairs(z_ref, h, rows)
                do4 = jnp.concatenate([dattn[:, (4 * h + j) * LANES:(4 * h + j + 1) * LANES] for j in range(4)],
                                      axis=0)
                dq4 = jnp.zeros((4 * BLOCK, LANES), F32)
                for e in range(2):
                    s = lax.dot_general(q4, k_eff[h][e], (((1,), (1,)), ((), ())), preferred_element_type=F32)
                    p, psink = _softmax_sink(s, valid, _sink_column(sinks_ref, h, e))
                    dp = lax.dot_general(do4, v_eff[h][e], (((1,), (1,)), ((), ())), preferred_element_type=F32)
                    delta = jnp.sum(p * dp, axis=-1, keepdims=True)
                    ds = (p * (dp - delta)).astype(BF16)
                    dq4 = dq4 + jnp.dot(ds, k_eff[h][e], preferred_element_type=F32)
                    dk = lax.dot_general(q4, ds, (((0,), (0,)), ((), ())), preferred_element_type=F32).T
                    dv = lax.dot_general(do4, p.astype(BF16), (((0,), (0,)), ((), ())), preferred_element_type=F32).T
                    keep = lane_lo if e == 0 else jnp.logical_not(lane_lo)
                    slot = 0 if e == h else 1
                    dk_acc[slot] = dk_acc[slot] + jnp.where(keep, dk, 0.0)
                    dv_acc[slot] = dv_acc[slot] + jnp.where(keep, dv, 0.0)
                    dsk = -(psink * delta)
                    for j in range(4):
                        tot = jnp.sum(jnp.where(rowblk == j, dsk, 0.0), axis=0, keepdims=True)
                        dsink = dsink + jnp.where(sink_lane == GROUP * h + 2 * j + e, tot, 0.0)
                for j in range(4):
                    cols = slice((4 * h + j) * LANES, (4 * h + j + 1) * LANES)
                    dqj = dq4[j * BLOCK:(j + 1) * BLOCK]
                    dz_ref[rows, cols] = dqj.astype(BF16)
                    db_ref[0:1, cols] += jnp.sum(dqj, axis=0, keepdims=True)
            dsk_ref[0:1, :] += dsink
            dkv_new = jnp.concatenate([(dk_acc[0] + pltpu.roll(dk_acc[1], HEAD_DIM, 1)) * scale,
                                       dv_acc[0] + pltpu.roll(dv_acc[1], HEAD_DIM, 1)], axis=1)
            done = pending + dkv_new[:BLOCK]
            dkv_ref[rows, :] = done.astype(BF16)
            dbkv_ref[0:1, :] += jnp.sum(done, axis=0, keepdims=True)

            cb = z_ref[rows, seg["cb"]].astype(F32)
            cc = z_ref[rows, seg["cc"]].astype(F32)
            cx = z_ref[rows, seg["cx"]].astype(F32)
            sg = jax.nn.sigmoid(z_ref[rows, seg["gc"]].astype(F32))
            p_in = cc * cx
            if b == 0:
                prev = jnp.where(n > 0, prev_ref[:, seg["cc"]].astype(F32) * prev_ref[:, seg["cx"]].astype(F32), 0.0)
            else:
                tail = slice(b * BLOCK - SUBLANES_BF16, b * BLOCK)
                prev = z_ref[tail, seg["cc"]].astype(F32) * z_ref[tail, seg["cx"]].astype(F32)
            p_m1 = _shift_down(p_in, 1, prev)
            p_m2 = _shift_down(p_in, 2, prev)
            w0, w1, w2 = cw_ref[0:1, :], cw_ref[1:2, :], cw_ref[2:3, :]
            cconv = w0 * p_m2 + w1 * p_m1 + w2 * p_in
            dconv = dm * sg
            dgc = dm * (cb * cconv) * sg * (1.0 - sg)
            dcb = dconv * cconv
            dcc_t = dconv * cb
            if b == BWD_BLOCKS - 1:
                nxt = jnp.where(n < steps - 1,
                                dmn_ref[...].astype(F32) * jax.nn.sigmoid(next_ref[:, seg["gc"]].astype(F32))
                                * next_ref[:, seg["cb"]].astype(F32), 0.0)
            else:
                head = slice((b + 1) * BLOCK, (b + 1) * BLOCK + SUBLANES_BF16)
                nxt = (dm_ref[head, :].astype(F32) * jax.nn.sigmoid(z_ref[head, seg["gc"]].astype(F32))
                       * z_ref[head, seg["cb"]].astype(F32))
            dpin = w2 * dcc_t + w1 * _shift_up(dcc_t, 1, nxt) + w0 * _shift_up(dcc_t, 2, nxt)
            for nm, val in (("cb", dcb), ("cc", dpin * cx), ("cx", dpin * cc), ("gc", dgc)):
                dz_ref[rows, seg[nm]] = val.astype(BF16)
                db_ref[0:1, seg[nm]] += jnp.sum(val, axis=0, keepdims=True)
            dcw_ref[0:1, :] += jnp.sum(dcc_t * p_m2, axis=0, keepdims=True)
            dcw_ref[1:2, :] += jnp.sum(dcc_t * p_m1, axis=0, keepdims=True)
            dcw_ref[2:3, :] += jnp.sum(dcc_t * p_in, axis=0, keepdims=True)
            return dkv_new[BLOCK:]

        @pl.when(n < steps)
        def _():
            pending = carry_ref[...]
            for b in range(BWD_BLOCKS):
                pending = one_block(b, pending)
            carry_ref[...] = pending

        @pl.when(n == steps)
        def _():
            done = carry_ref[...]
            dkv_ref[:BLOCK, :] = done.astype(BF16)
            dkv_ref[BLOCK:, :] = jnp.zeros((tq - BLOCK, kvw2), BF16)
            dbkv_ref[0:1, :] += jnp.sum(done, axis=0, keepdims=True)

    def cur(n):
        return jnp.minimum(n, steps - 1)

    def after(n):
        return jnp.minimum((cur(n) + 1) * halo, last_halo)

    blk = pl.BlockSpec((tq, d), lambda n: (cur(n), 0))
    return _pallas(
        body, name="mixer_bwd", grid=(steps + 1,), ride=ride, sem=("arbitrary",),
        args=(z, z, z, z, dmerged, dmerged, attn, sinks, conv_w),
        in_specs=[pl.BlockSpec((tq, zw), lambda n: (cur(n), 0)),
                  pl.BlockSpec((BLOCK, kvw2), lambda n: (jnp.maximum(cur(n) * BWD_BLOCKS - 1, 0), d // kvw2)),
                  pl.BlockSpec((SUBLANES_BF16, zw), lambda n: (jnp.maximum(cur(n) * halo - 1, 0), 0)),
                  pl.BlockSpec((SUBLANES_BF16, zw), lambda n: (after(n), 0)),
                  blk,
                  pl.BlockSpec((SUBLANES_BF16, d), lambda n: (after(n), 0)),
                  blk, SMEM_SPEC, pl.BlockSpec((3, d), lambda n: (0, 0))],
        out_specs=[pl.BlockSpec((tq, zw), lambda n: (cur(n), 0)),
                   pl.BlockSpec((tq, kvw2), lambda n: (n, 0)),
                   pl.BlockSpec((8, zw), lambda n: (0, 0)), pl.BlockSpec((8, kvw2), lambda n: (0, 0)),
                   pl.BlockSpec((8, d), lambda n: (0, 0)), pl.BlockSpec((8, LANES), lambda n: (0, 0))],
        out_shape=[SDS((t, zw), BF16), SDS((t + tq, kvw2), BF16), SDS((8, zw), F32), SDS((8, kvw2), F32),
                   SDS((8, d), F32), SDS((8, LANES), F32)],
        scratch=[pltpu.VMEM((BLOCK, kvw2), F32)])


def _wgrad_in(dz, dkv, h1, tk, ride=None):
    t, zw = dz.shape
    d = h1.shape[1]
    kvw2 = dkv.shape[1]
    blk = d + kvw2
    assert zw % blk == 0
    tn = (((0,), (0,)), ((), ()))

    def body(a_ref, akv_ref, h_ref, o_ref, o16_ref):
        n, k = pl.program_id(0), pl.program_id(1)

        @pl.when(k == 0)
        def _():
            o_ref[...] = jnp.zeros_like(o_ref)

        @pl.when(n == 0)
        def _():
            o_ref[:d, :] += lax.dot_general(a_ref[:, :d], h_ref[...], tn, preferred_element_type=F32)
            o_ref[d:, :] += lax.dot_general(akv_ref[...], h_ref[...], tn, preferred_element_type=F32)

        @pl.when(n > 0)
        def _():
            o_ref[...] += lax.dot_general(a_ref[...], h_ref[...], tn, preferred_element_type=F32)

        @pl.when(k == t // tk - 1)
        def _():
            o16_ref[...] = o_ref[...].astype(BF16)

    out_spec = pl.BlockSpec((blk, d), lambda n, k: (n, 0))
    return _pallas(
        body, name="wgrad_in", grid=(zw // blk, t // tk),
        in_specs=[pl.BlockSpec((tk, blk), lambda n, k: (k, n)), pl.BlockSpec((tk, kvw2), lambda n, k: (k, 0)),
                  pl.BlockSpec((tk, d), lambda n, k: (k, 0))],
        out_specs=[out_spec, out_spec], out_shape=[SDS((zw, d), F32), SDS((zw, d), BF16)],
        args=(dz, dkv, h1), sem=("parallel", "arbitrary"), ride=ride)


def _in_proj_bwd(dz, dkv, wt, x, dx1, g_mix, sc1, tm, ride=None):
    t, d = x.shape
    zw = dz.shape[1]
    kvw2 = dkv.shape[1]
    rest = d + kvw2

    def body(a_ref, akv_ref, w_ref, x_ref, dx1_ref, g_ref, sc_ref, gx_ref, st_ref):
        @pl.when(pl.program_id(0) == 0)
        def _():
            st_ref[...] = jnp.zeros_like(st_ref)

        dh = (jnp.dot(a_ref[:, :d], w_ref[:d, :], preferred_element_type=F32)
              + jnp.dot(akv_ref[...], w_ref[d:rest, :], preferred_element_type=F32)
              + jnp.dot(a_ref[:, rest:], w_ref[rest:, :], preferred_element_type=F32))
        xx = x_ref[...]
        r = _rms(xx)
        xn = xx * r
        g = g_ref[...]
        dn = dh * (1.0 + sc_ref[...])
        u = dn * g
        gx_ref[...] = dx1_ref[...] + r * (u - xn * jnp.mean(u * xn, axis=-1, keepdims=True))
        st_ref[0:1, :] += jnp.sum(dh, axis=0, keepdims=True)
        st_ref[1:2, :] += jnp.sum(dh * (xn * g), axis=0, keepdims=True)
        st_ref[2:3, :] += jnp.sum(dn * xn, axis=0, keepdims=True)

    row = pl.BlockSpec((tm, d), lambda i: (i, 0))
    vec = pl.BlockSpec((1, d), lambda i: (0, 0))
    return _pallas(
        body, name="in_proj_bwd", grid=(t // tm,),
        in_specs=[pl.BlockSpec((tm, zw), lambda i: (i, 0)), pl.BlockSpec((tm, kvw2), lambda i: (i, 0)),
                  pl.BlockSpec((zw, d), lambda i: (0, 0), pipeline_mode=pl.Buffered(1)),
                  row, row, vec, vec],
        out_specs=[row, pl.BlockSpec((8, d), lambda i: (0, 0))],
        out_shape=[SDS((t, d), F32), SDS((8, d), F32)],
        args=(dz, dkv, wt, x, dx1, g_mix, sc1), sem=("arbitrary",), ride=ride)


def _to_lanes(v, rows=None):
    flat = v.reshape(-1)
    need = -(-flat.shape[0] // LANES)
    need = -(-need // 8) * 8 if rows is None else rows
    return jnp.pad(flat, (0, need * LANES - flat.shape[0])).reshape(need, LANES)


def kernel(x, c, w_ada, b_ada, g_mix, w_in, b_in, sinks, conv_w, w_out, g_ffn, w_ffn_in, w_ffn_out, g_final, loss_target, m_w_ada, m_b_ada, m_g_mix, m_w_in, m_b_in, m_sinks, m_conv_w, m_w_out, m_g_ffn, m_w_ffn_in, m_w_ffn_out, m_g_final, v_w_ada, v_b_ada, v_g_mix, v_w_in, v_b_in, v_sinks, v_conv_w, v_w_out, v_g_ffn, v_w_ffn_in, v_w_ffn_out, v_g_final):
    xs, tgt = x[0], loss_target[0]
    t, d = xs.shape
    zw = w_in.shape[2] * N_CHIP
    kvw2 = zw - 6 * d
    ff = w_ffn_out.shape[1] * N_CHIP
    n_mod = w_ada.shape[2] * N_CHIP // d
    mod_sh = w_ada.shape[2]
    cw_sh = conv_w.shape[2]
    assert d % (8 * LANES) == 0 and kvw2 == 2 * LANES and t % 512 == 0 and n_mod == 6
    xi, yi, ci = _mesh_pos()
    j_me = 2 * xi + yi
    b_me = 4 * xi + 2 * yi + ci
    pos = jnp.stack([ci, j_me]).astype(jnp.int32)
    tm = 512

    pack1 = jnp.concatenate([c.reshape(d // LANES, LANES), conv_w[0].reshape(-1, LANES)], axis=0)
    pack1 = jnp.pad(pack1, ((0, 16 - pack1.shape[0]), (0, 0)))
    g1 = _all_gather_small(pack1, "gather_c")
    c_all = g1[:, :d // LANES, :].reshape(N_DEV, d)
    cw_rows = 3 * cw_sh // LANES
    conv_w_full = jnp.concatenate(
        [g1[2 * j, d // LANES:d // LANES + cw_rows, :].reshape(3, cw_sh) for j in range(N_CHIP)], axis=1)
    b_ada_sh = lax.dynamic_slice(b_ada, (0, j_me * mod_sh), (1, mod_sh))
    mod_all = _all_gather_small(_ada_fwd(c_all, w_ada[0], b_ada_sh), "gather_mod")
    mod = jnp.concatenate([lax.dynamic_index_in_dim(mod_all[2 * j], b_me, 0, keepdims=True) for j in range(N_CHIP)],
                          axis=1)
    sh1, sc1, ga1, sh2, sc2, ga2 = [mod[:, k * d:(k + 1) * d] for k in range(6)]

    w_in_t, m_w_in_t, v_w_in_t = w_in[0].T, m_w_in[0].T, v_w_in[0].T
    (w_in_g,) = _gather_weights([_cast_into_block(pos, w_in_t, "cast_w_in")])
    w_in_tf = w_in_g.reshape(zw, d)
    later = [_cast_into_block(pos, w_out[0], "cast_w_out"), _cast_into_block(pos, w_ffn_in[0], "cast_w_ffn_in"),
             _cast_into_block(pos, w_ffn_out[0], "cast_w_ffn_out")]

    (z, h1), later = _in_proj(xs, g_mix, sc1, sh1, w_in_tf, b_in, min(t, 1024), zw // 5, ride=_x_gather_ici(later))
    (attn, merged), later = _mixer_fwd(z, sinks, conv_w_full, d, ride=_x_gather_d2d(later))
    w_out_f = later[0].reshape(d, d)
    w_ffn_in_f = later[1]
    w_ffn_out_f = later[2].reshape(ff, d)
    tml = min(t, 1024)
    y1, x1, h2 = _out_proj_fwd(merged, w_out_f, xs, ga1, g_ffn, sc2, sh2, tml)
    gu, act = _ffn_in_fwd(h2, w_ffn_in_f, ff, tml, ff // 2)
    dx2, dy2, st_loss = _ffn_out_loss(act, w_ffn_out_f, x1, tgt, ga2, g_final.reshape(1, d), tml)

    dgu = _ffn_out_bwd(dy2, w_ffn_out_f, gu, tml, ff // 2)
    tk = min(t, 2048)
    dw_ffn_out, _ = _wgrad(
        act, dy2, pl.BlockSpec((tk, ff // 2), lambda m, k: (k, m)), pl.BlockSpec((tk, d), lambda m, k: (k, 0)),
        pl.BlockSpec((ff // 2, d), lambda m, k: (m, 0)), SDS((ff, d), F32), (2, t // tk), "wgrad_ffn_out")
    dx1, dy1, st_ffn = _ffn_in_bwd(dgu, w_ffn_in_f, x1, dx2, y1, g_ffn, sc2, ga1, tm, ff // 2)
    dw_ffn_in, _ = _wgrad(
        h2, dgu, pl.BlockSpec((tk, d), lambda n, k: (k, 0)),
        pl.BlockSpec((None, tk, ff // 2), lambda n, k: (n // 2, k, n % 2)),
        pl.BlockSpec((None, d, ff // 2), lambda n, k: (n, 0, 0)), SDS((N_CHIP, d, ff // 2), F32),
        (N_CHIP, t // tk), "wgrad_ffn_in")
    dw_out, _ = _wgrad(
        merged, dy1, pl.BlockSpec((tk, d), lambda m, k: (k, 0)), pl.BlockSpec((tk, d), lambda m, k: (k, 0)),
        pl.BlockSpec((d, d), lambda m, k: (0, 0)), SDS((d, d), F32), (1, t // tk), "wgrad_out")

    early = [[g.reshape(N_CHIP, -1, g.shape[-1]) for g in pair] for pair in (dw_out, dw_ffn_in, dw_ffn_out)]
    early_names = ["w_out", "w_ffn_in", "w_ffn_out"]
    dmerged, _ = _out_proj_bwd(dy1, w_out_f, tml)
    (dz, dkv_shifted, db_z, db_kv, dcw, dsk), terms = _mixer_bwd(
        z, dmerged, attn, sinks, conv_w_full, d, ride=_x_reduce([e[0] for e in early], [e[1] for e in early]))
    dkv = dkv_shifted[BLOCK:BLOCK + t]
    fulls = [_sum_terms(pos, e[0], s, r, "sum_terms_" + nm)
             for e, s, r, nm in zip(early, terms[:3], terms[3:], early_names)]
    dw_in_t, (g_w_out, g_w_ffn_in, g_w_ffn_out) = _wgrad_in(dz, dkv, h1, tk, ride=_x_pair_exchange(fulls))
    dw_in_t = [g.reshape(N_CHIP, zw // N_CHIP, d) for g in dw_in_t]

    (grad_x, st_in), (from_sib, from_far) = _in_proj_bwd(dz, dkv, w_in_tf, xs, dx1, g_mix, sc1, tm,
                                                         ride=_x_reduce([dw_in_t[0]], [dw_in_t[1]]))
    (g_w_in_t,) = _exchange(_x_pair_exchange([_sum_terms(pos, dw_in_t[0], from_sib, from_far, "sum_terms_w_in")]),
                            "pair_exchange")

    dmod = jnp.concatenate([st_in[0:1], st_in[1:2], st_ffn[3:4], st_ffn[0:1], st_ffn[1:2], st_loss[0:1]], axis=1)
    db_in = jnp.concatenate([db_z[0:1, :d], db_kv[0:1], db_z[0:1, d + kvw2:]], axis=1)
    seg = [dmod, st_in[2:3], db_in, dsk[0:1], dcw[0:3].reshape(1, 3 * d), st_ffn[2:3], st_loss[1:2],
           st_loss[2:3, :LANES]]
    sizes = [s.shape[1] for s in seg]
    pack2 = _to_lanes(jnp.concatenate(seg, axis=1))
    packs = _all_gather_small(pack2, "gather_small_grads")
    tot = _pack_sum(packs).reshape(-1)
    offs = [sum(sizes[:k]) for k in range(len(sizes))]
    gb_ada, gg_mix, gb_in, gsinks, gcw, gg_ffn, gg_final, loss_v = [tot[o:o + s] for o, s in zip(offs, sizes)]
    loss = loss_v[0]
    gsinks = gsinks[:sinks.shape[1]]
    gcw_sh = lax.dynamic_slice(gcw.reshape(3, d), (0, j_me * cw_sh), (3, cw_sh))

    dmod_all = packs[:, :n_mod * d // LANES, :].reshape(N_DEV, n_mod * d)
    g_w_ada = _ada_wgrad(c_all, lax.dynamic_slice(dmod_all, (0, j_me * mod_sh), (N_DEV, mod_sh)))

    out_g, out_d, out_m, out_v = {}, {}, {}, {}
    big = {"w_ada": (w_ada[0], g_w_ada, m_w_ada[0], v_w_ada[0]),
           "w_out": (w_out[0], g_w_out, m_w_out[0], v_w_out[0]),
           "w_ffn_in": (w_ffn_in[0], g_w_ffn_in, m_w_ffn_in[0], v_w_ffn_in[0]),
           "w_ffn_out": (w_ffn_out[0], g_w_ffn_out, m_w_ffn_out[0], v_w_ffn_out[0])}
    for nm, (w, g, m, v) in big.items():
        out_g[nm], out_d[nm], out_m[nm], out_v[nm] = [o[None] for o in _adamw(w, g, m, v, "adamw_" + nm)]
    out_g["w_in"], out_d["w_in"], out_m["w_in"], out_v["w_in"] = [
        o.T[None] for o in _adamw(w_in_t, g_w_in_t, m_w_in_t, v_w_in_t, "adamw_w_in")]
    small = {"b_ada": (b_ada, gb_ada, m_b_ada, v_b_ada), "g_mix": (g_mix, gg_mix, m_g_mix, v_g_mix),
             "b_in": (b_in, gb_in, m_b_in, v_b_in), "sinks": (sinks, gsinks, m_sinks, v_sinks),
             "conv_w": (conv_w, gcw_sh, m_conv_w, v_conv_w), "g_ffn": (g_ffn, gg_ffn, m_g_ffn, v_g_ffn),
             "g_final": (g_final, gg_final, m_g_final, v_g_final)}
    s_sizes = [w.size for w, _, _, _ in small.values()]
    s_rows = -(-sum(s_sizes) // LANES // 8) * 8

    def s_pack(k):
        return _to_lanes(jnp.concatenate([tup[k].reshape(-1) for tup in small.values()]), s_rows)

    s_out = _adamw(s_pack(0), s_pack(1), s_pack(2), s_pack(3), "adamw_small")
    s_off = 0
    for (nm, (w, g, _, _)), sz in zip(small.items(), s_sizes):
        out_g[nm] = g.reshape(w.shape)
        out_d[nm], out_m[nm], out_v[nm] = [o.reshape(-1)[s_off:s_off + sz].reshape(w.shape) for o in s_out[1:]]
        s_off += sz

    order = ["w_ada", "b_ada", "g_mix", "w_in", "b_in", "sinks", "conv_w", "w_out", "g_ffn", "w_ffn_in", "w_ffn_out",
             "g_final"]
    return (loss, grad_x[None], *[out_g[k] for k in order], *[out_d[k] for k in order],
            *[out_m[k] for k in order], *[out_v[k] for k in order])
```

```python
import functools

import jax
import jax.numpy as jnp
from jax import lax
from jax.experimental import pallas as pl
from jax.experimental.pallas import tpu as pltpu

F32 = jnp.float32
BF16 = jnp.bfloat16
EPS = 1e-6
HEAD_DIM = 64
GROUP = 8
BLOCK = 128
LANES = 128
SUBLANES_BF16 = 16
N_DEV = 8
N_CHIP = 4
VMEM_LIMIT = 56 * 1024 * 1024
MESH = pl.DeviceIdType.MESH

ADAM_LR = 0.001
ADAM_B1 = 0.9
ADAM_B2 = 0.999
ADAM_EPS = 1e-08
ADAM_WD = 0.01
ADAM_STEP = 10

SDS = jax.ShapeDtypeStruct
ANY = pl.BlockSpec(memory_space=pl.ANY)
VMEM_SPEC = pl.BlockSpec(memory_space=pltpu.VMEM)
SMEM_SPEC = pl.BlockSpec(memory_space=pltpu.SMEM)


def _params(*sem):
    return pltpu.CompilerParams(dimension_semantics=sem, vmem_limit_bytes=VMEM_LIMIT)


def _mesh_pos():
    return lax.axis_index("x"), lax.axis_index("y"), lax.axis_index("c")


def _row_tile(rows, cols, itemsize=4, budget=1 << 20, mult=8):
    best = None
    for t in range(mult, rows + 1, mult):
        if rows % t == 0 and t * cols * itemsize <= budget:
            best = t
    if best is None:
        best = rows
    return best


def _gather_all(v_ref, out_ref, send_sems, recv_sems, local_sem):
    x, y, c = _mesh_pos()
    me = 4 * x + 2 * y + c
    mine = pltpu.make_async_copy(v_ref, out_ref.at[me], local_sem)
    mine.start()
    peers = []
    for k in range(1, N_DEV):
        px = 1 - x if k & 4 else x
        py = 1 - y if k & 2 else y
        pc = 1 - c if k & 1 else c
        peers.append((px, py, pc))

    def copy(k, block):
        return pltpu.make_async_remote_copy(
            src_ref=v_ref, dst_ref=out_ref.at[block], send_sem=send_sems.at[k], recv_sem=recv_sems.at[k],
            device_id=peers[k], device_id_type=MESH)

    sends = [copy(k, me) for k in range(N_DEV - 1)]
    for cp in sends:
        cp.start()
    for k, (px, py, pc) in enumerate(peers):
        copy(k, 4 * px + 2 * py + pc).wait_recv()
    for cp in sends:
        cp.wait_send()
    mine.wait()


def _small_sems():
    return [pltpu.SemaphoreType.DMA((N_DEV - 1,)), pltpu.SemaphoreType.DMA((N_DEV - 1,)), pltpu.SemaphoreType.DMA]


def _all_gather_small(v, name):
    return pl.pallas_call(
        functools.partial(_gather_all), name=name, out_shape=SDS((N_DEV,) + v.shape, v.dtype),
        in_specs=[VMEM_SPEC], out_specs=VMEM_SPEC, scratch_shapes=_small_sems())(v)


def _other_chips(x, y):
    return [(1 - x, y), (x, 1 - y), (1 - x, 1 - y)]


def _startup(pack, w_ada_sh, b_ada_sh, w_buf):
    d, n = w_ada_sh.shape
    kc = d // LANES

    def body(pack_ref, wa_hbm, ba_ref, w_in_unused, packs_ref, mod_ref, w_ref, wa_scr, mod_scr,
             s1, r1, l1, s2, r2, l2, send_sems, recv_sems, fsend_sems, frecv_sems, wa_sem):
        del w_in_unused
        x, y, c = _mesh_pos()
        j_me = 2 * x + y
        chips = _other_chips(x, y)
        half = w_ref.shape[1] // 2

        def rows_of(which):
            return pl.ds(pl.multiple_of(which * half, SUBLANES_BF16), half)

        def copy(p, block, rows, over_ici):
            sems = (send_sems, recv_sems) if over_ici else (fsend_sems, frecv_sems)
            return pltpu.make_async_remote_copy(
                src_ref=w_ref.at[block, rows], dst_ref=w_ref.at[block, rows], send_sem=sems[0].at[p],
                recv_sem=sems[1].at[p], device_id=(*chips[p], c) if over_ici else (x, y, 1 - c), device_id_type=MESH)

        def block_of(p):
            return 2 * chips[p][0] + chips[p][1]

        sends = [copy(p, j_me, rows_of(c), True) for p in range(3)]
        for cp in sends:
            cp.start()
        load_wa = pltpu.make_async_copy(wa_hbm, wa_scr, wa_sem)
        load_wa.start()
        _gather_all(pack_ref, packs_ref, s1, r1, l1)
        load_wa.wait()
        acc = jnp.zeros((N_DEV, n), F32)
        for k in range(kc):
            ck = packs_ref[:, k, :]
            sk = (ck * jax.nn.sigmoid(ck)).astype(BF16)
            acc = acc + jnp.dot(sk, wa_scr[k * LANES:(k + 1) * LANES, :].astype(BF16), preferred_element_type=F32)
        mod_scr[...] = acc + ba_ref[...]
        _gather_all(mod_scr, mod_ref, s2, r2, l2)
        forwards = []
        for p in range(3):
            copy(p, block_of(p), rows_of(c), True).wait_recv()
            fw = copy(p, block_of(p), rows_of(c), False)
            fw.start()
            forwards.append(fw)
        for p in range(3):
            copy(p, block_of(p), rows_of(1 - c), False).wait_recv()
        for cp in sends + forwards:
            cp.wait_send()

    return pl.pallas_call(
        body, name="startup",
        out_shape=[SDS((N_DEV,) + pack.shape, F32), SDS((N_DEV, N_DEV, n), F32), SDS(w_buf.shape, w_buf.dtype)],
        in_specs=[VMEM_SPEC, ANY, VMEM_SPEC, ANY], out_specs=[VMEM_SPEC, VMEM_SPEC, ANY],
        input_output_aliases={3: 2},
        scratch_shapes=[pltpu.VMEM((d, n), F32), pltpu.VMEM((N_DEV, n), F32)] + _small_sems() + _small_sems()
        + [pltpu.SemaphoreType.DMA((3,))] * 4 + [pltpu.SemaphoreType.DMA],
        compiler_params=pltpu.CompilerParams(vmem_limit_bytes=VMEM_LIMIT),
    )(pack, w_ada_sh, b_ada_sh, w_buf)


class _Exchange:
    def __init__(self, operands, out_shape, in_place, n_sems, copies):
        self.operands, self.out_shape, self.in_place, self.n_sems, self.copies = (
            list(operands), list(out_shape), in_place, n_sems, copies)

    def sems(self):
        return [pltpu.SemaphoreType.DMA((self.n_sems,)), pltpu.SemaphoreType.DMA((self.n_sems,))]


def _x_gather_ici(bufs):
    def copies(ins, outs, send_sems, recv_sems):
        x, y, c = _mesh_pos()
        chips = _other_chips(x, y)
        out = []
        for w in range(len(outs)):
            half = outs[w].shape[1] // 2
            rows = pl.ds(pl.multiple_of(c * half, SUBLANES_BF16), half)
            for p in range(3):
                out.append(pltpu.make_async_remote_copy(
                    src_ref=outs[w].at[2 * x + y, rows], dst_ref=outs[w].at[2 * x + y, rows],
                    send_sem=send_sems.at[w * 3 + p], recv_sem=recv_sems.at[w * 3 + p],
                    device_id=(*chips[p], c), device_id_type=MESH))
        return out

    return _Exchange(bufs, [SDS(b.shape, b.dtype) for b in bufs], True, 3 * len(bufs), copies)


def _x_gather_d2d(bufs):
    def copies(ins, outs, send_sems, recv_sems):
        x, y, c = _mesh_pos()
        chips = _other_chips(x, y)
        out = []
        for w in range(len(outs)):
            half = outs[w].shape[1] // 2
            rows = pl.ds(pl.multiple_of(c * half, SUBLANES_BF16), half)
            for p in range(3):
                block = 2 * chips[p][0] + chips[p][1]
                out.append(pltpu.make_async_remote_copy(
                    src_ref=outs[w].at[block, rows], dst_ref=outs[w].at[block, rows],
                    send_sem=send_sems.at[w * 3 + p], recv_sem=recv_sems.at[w * 3 + p],
                    device_id=(x, y, 1 - c), device_id_type=MESH))
        return out

    return _Exchange(bufs, [SDS(b.shape, b.dtype) for b in bufs], True, 3 * len(bufs), copies)


N_REMOTE = 6


def _x_reduce(grads32, grads16):
    n_w = len(grads32)

    def copies(ins, outs, send_sems, recv_sems):
        g32, g16 = ins[:n_w], ins[n_w:]
        from_sib, from_far = outs[:n_w], outs[n_w:]
        x, y, c = _mesh_pos()
        chips = _other_chips(x, y)
        out = []
        for w in range(n_w):
            half = g32[w].shape[1] // 2
            k0 = w * (N_REMOTE + 1)
            out.append(pltpu.make_async_remote_copy(
                src_ref=g32[w].at[2 * x + y, pl.ds(pl.multiple_of((1 - c) * half, SUBLANES_BF16), half), :],
                dst_ref=from_sib[w], send_sem=send_sems.at[k0], recv_sem=recv_sems.at[k0],
                device_id=(x, y, 1 - c), device_id_type=MESH))
            for p in range(3):
                for f in range(2):
                    tc = c if f == 0 else 1 - c
                    k = 2 * p + f
                    out.append(pltpu.make_async_remote_copy(
                        src_ref=g16[w].at[2 * chips[p][0] + chips[p][1],
                                          pl.ds(pl.multiple_of(tc * half, SUBLANES_BF16), half), :],
                        dst_ref=from_far[w].at[k], send_sem=send_sems.at[k0 + 1 + k], recv_sem=recv_sems.at[k0 + 1 + k],
                        device_id=(*chips[p], tc), device_id_type=MESH))
        return out

    shapes = ([SDS((g.shape[1] // 2, g.shape[2]), g.dtype) for g in grads32]
              + [SDS((N_REMOTE, g.shape[1] // 2, g.shape[2]), g.dtype) for g in grads16])
    return _Exchange(list(grads32) + list(grads16), shapes, False, (N_REMOTE + 1) * n_w, copies)


def _x_pair_exchange(fulls):
    def copies(ins, outs, send_sems, recv_sems):
        x, y, c = _mesh_pos()
        out = []
        for w in range(len(outs)):
            half = outs[w].shape[0] // 2
            rows = pl.ds(pl.multiple_of(c * half, 8), half)
            out.append(pltpu.make_async_remote_copy(
                src_ref=outs[w].at[rows], dst_ref=outs[w].at[rows], send_sem=send_sems.at[w],
                recv_sem=recv_sems.at[w], device_id=(x, y, 1 - c), device_id_type=MESH))
        return out

    return _Exchange(fulls, [SDS(f.shape, f.dtype) for f in fulls], True, len(fulls), copies)


def _pallas(body, *, name, grid, in_specs, out_specs, out_shape, args, scratch=(), sem=None, ride=None):
    single = not isinstance(out_specs, (list, tuple))
    out_specs_l = [out_specs] if single else list(out_specs)
    out_shape_l = [out_shape] if single else list(out_shape)
    n_in, n_out, n_scr = len(in_specs), len(out_specs_l), len(scratch)
    if ride is None:
        res = pl.pallas_call(body, name=name, grid=grid, in_specs=list(in_specs), out_specs=out_specs,
                             out_shape=out_shape, scratch_shapes=list(scratch), compiler_params=_params(*sem))(*args)
        return res, None
    n_x, n_xo = len(ride.operands), len(ride.out_shape)

    def full_body(*refs):
        ins, x_ins = refs[:n_in], refs[n_in:n_in + n_x]
        outs = refs[n_in + n_x:n_in + n_x + n_out]
        x_outs = refs[n_in + n_x + n_out:n_in + n_x + n_out + n_xo]
        rest = refs[n_in + n_x + n_out + n_xo:]
        scr, (send_sems, recv_sems) = rest[:n_scr], rest[n_scr:]
        first = functools.reduce(jnp.logical_and, [pl.program_id(a) == 0 for a in range(len(grid))])
        last = functools.reduce(jnp.logical_and, [pl.program_id(a) == grid[a] - 1 for a in range(len(grid))])

        @pl.when(first)
        def _():
            for cp in ride.copies(x_ins, x_outs, send_sems, recv_sems):
                cp.start()

        body(*ins, *outs, *scr)

        @pl.when(last)
        def _():
            for cp in ride.copies(x_ins, x_outs, send_sems, recv_sems):
                cp.wait()

    res = pl.pallas_call(
        full_body, name=name, grid=grid, in_specs=list(in_specs) + [ANY] * n_x,
        out_specs=out_specs_l + [ANY] * n_xo, out_shape=out_shape_l + ride.out_shape,
        input_output_aliases={n_in + k: n_out + k for k in range(n_x)} if ride.in_place else {},
        scratch_shapes=list(scratch) + ride.sems(),
        compiler_params=_params(*(["arbitrary"] * len(grid))))(*args, *ride.operands)
    own = res[0] if single else list(res[:n_out])
    return own, list(res[n_out:])


def _exchange(ride, name):
    n_x, n_xo = len(ride.operands), len(ride.out_shape)

    def body(*refs):
        x_ins, x_outs = refs[:n_x], refs[n_x:n_x + n_xo]
        send_sems, recv_sems = refs[n_x + n_xo:]
        copies = ride.copies(x_ins, x_outs, send_sems, recv_sems)
        for cp in copies:
            cp.start()
        for cp in copies:
            cp.wait()

    return pl.pallas_call(
        body, name=name, in_specs=[ANY] * n_x, out_specs=[ANY] * n_xo, out_shape=ride.out_shape,
        input_output_aliases={k: k for k in range(n_x)} if ride.in_place else {},
        scratch_shapes=ride.sems())(*ride.operands)


def _cast_into_block(pos, w, name):
    rows, cols = w.shape
    tr = _row_tile(rows, cols, mult=SUBLANES_BF16)

    def body(pos_ref, w_ref, o_ref):
        del pos_ref
        o_ref[...] = w_ref[...].astype(BF16)

    return pl.pallas_call(
        body, name=name,
        grid_spec=pltpu.PrefetchScalarGridSpec(
            num_scalar_prefetch=1, grid=(rows // tr,),
            in_specs=[pl.BlockSpec((tr, cols), lambda i, pos_ref: (i, 0))],
            out_specs=pl.BlockSpec((None, tr, cols), lambda i, pos_ref: (pos_ref[1], i, 0))),
        out_shape=SDS((N_CHIP, rows, cols), BF16), compiler_params=_params("parallel"))(pos, w)


def _sum_terms(pos, grad, from_sib, from_far, name):
    _, rows, cols = grad.shape
    half = rows // 2
    tr = _row_tile(half, cols, mult=SUBLANES_BF16)
    nblk = half // tr

    def body(pos_ref, g_ref, s_ref, r_ref, o_ref):
        del pos_ref
        acc = g_ref[...] + s_ref[...]
        for k in range(N_REMOTE):
            acc = acc + r_ref[k].astype(F32)
        o_ref[...] = acc

    return pl.pallas_call(
        body, name=name,
        grid_spec=pltpu.PrefetchScalarGridSpec(
            num_scalar_prefetch=1, grid=(nblk,),
            in_specs=[pl.BlockSpec((None, tr, cols), lambda i, pos_ref: (pos_ref[1], pos_ref[0] * nblk + i, 0)),
                      pl.BlockSpec((tr, cols), lambda i, pos_ref: (i, 0)),
                      pl.BlockSpec((N_REMOTE, tr, cols), lambda i, pos_ref: (0, i, 0))],
            out_specs=pl.BlockSpec((tr, cols), lambda i, pos_ref: (pos_ref[0] * nblk + i, 0))),
        out_shape=SDS((rows, cols), F32),
        compiler_params=_params("parallel"),
    )(pos, grad, from_sib, from_far)


def _adamw(w, g, m, v, name):
    rows, cols = w.shape
    tr = _row_tile(rows, cols)

    def body(w_ref, g_ref, m_ref, v_ref, go_ref, d_ref, nm_ref, nv_ref):
        gg = g_ref[...]
        go_ref[...] = gg
        nm = ADAM_B1 * m_ref[...] + (1.0 - ADAM_B1) * gg
        nv = ADAM_B2 * v_ref[...] + (1.0 - ADAM_B2) * (gg * gg)
        m_hat = nm / (1.0 - ADAM_B1 ** ADAM_STEP)
        v_hat = nv / (1.0 - ADAM_B2 ** ADAM_STEP)
        d_ref[...] = -ADAM_LR * (m_hat / (jnp.sqrt(v_hat) + ADAM_EPS) + ADAM_WD * w_ref[...])
        nm_ref[...] = nm
        nv_ref[...] = nv

    spec = pl.BlockSpec((tr, cols), lambda i: (i, 0))
    return pl.pallas_call(body, name=name, grid=(rows // tr,), in_specs=[spec] * 4, out_specs=[spec] * 4,
                          out_shape=[SDS((rows, cols), F32)] * 4, compiler_params=_params("parallel"))(w, g, m, v)


def _pack_sum(gathered):
    _, rows, cols = gathered.shape

    def body(g_ref, o_ref):
        acc = g_ref[0]
        for d in range(1, N_DEV):
            acc = acc + g_ref[d]
        o_ref[...] = acc

    return pl.pallas_call(body, name="pack_sum", in_specs=[VMEM_SPEC], out_specs=VMEM_SPEC,
                          out_shape=SDS((rows, cols), F32))(gathered)


def _ada_wgrad(c_all, dmod_sh):
    d = c_all.shape[1]
    n = dmod_sh.shape[1]
    tn = 512

    def body(c_ref, g_ref, o_ref):
        cc = c_ref[...]
        s = cc * jax.nn.sigmoid(cc)
        o_ref[...] = lax.dot_general(s, g_ref[...], (((0,), (0,)), ((), ())), preferred_element_type=F32,
                                     precision=lax.Precision.HIGHEST)

    return pl.pallas_call(
        body, name="ada_wgrad", grid=(n // tn,),
        in_specs=[pl.BlockSpec((N_DEV, d), lambda j: (0, 0)), pl.BlockSpec((N_DEV, tn), lambda j: (0, j))],
        out_specs=pl.BlockSpec((d, tn), lambda j: (0, j)),
        out_shape=SDS((d, n), F32), compiler_params=_params("parallel"))(c_all, dmod_sh)


def _rms(xf):
    return lax.rsqrt(jnp.mean(xf * xf, axis=-1, keepdims=True) + EPS)


def _in_proj(x, g, sc, sh, wt, b, tm, tn, ride=None):
    t, d = x.shape
    n = wt.shape[0]

    def body(x_ref, g_ref, sc_ref, sh_ref, w_ref, b_ref, z_ref, h_ref):
        @pl.when(pl.program_id(1) == 0)
        def _():
            xf = x_ref[...]
            h_ref[...] = ((xf * _rms(xf) * g_ref[...]) * (1.0 + sc_ref[...]) + sh_ref[...]).astype(BF16)

        acc = lax.dot_general(h_ref[...], w_ref[...], (((1,), (1,)), ((), ())), preferred_element_type=F32)
        z_ref[...] = (acc + b_ref[...]).astype(BF16)

    row = pl.BlockSpec((tm, d), lambda i, j: (i, 0))
    vec = pl.BlockSpec((1, d), lambda i, j: (0, 0))
    nt, nj = t // tm, n // tn
    x_spec = pl.BlockSpec((tm, d), lambda i, j: (jnp.minimum(jnp.where(j == nj - 1, i + 1, i), nt - 1), 0))
    return _pallas(
        body, name="in_proj", grid=(nt, nj),
        in_specs=[x_spec, vec, vec, vec, pl.BlockSpec((tn, d), lambda i, j: (j, 0)),
                  pl.BlockSpec((1, tn), lambda i, j: (0, j))],
        out_specs=[pl.BlockSpec((tm, tn), lambda i, j: (i, j)), row],
        out_shape=[SDS((t, n), BF16), SDS((t, d), BF16)], args=(x, g, sc, sh, wt, b),
        sem=("parallel", "arbitrary"), ride=ride)


def _segments(d, kvw2):
    o = d + kvw2
    names = ("cb", "cc", "cx", "ga", "gc")
    seg = {nm: slice(o + k * d, o + (k + 1) * d) for k, nm in enumerate(names)}
    seg["q"], seg["kv"] = slice(0, d), slice(d, o)
    return seg


def _attn_masks():
    rows = 4 * BLOCK
    r = lax.broadcasted_iota(jnp.int32, (rows, 2 * BLOCK), 0) & (BLOCK - 1)
    col = lax.broadcasted_iota(jnp.int32, (rows, 2 * BLOCK), 1)
    return (col > r) & (col <= r + BLOCK), col


def _kv_variants(kv, n_kv_w):
    assert n_kv_w == LANES
    kb, vb = kv[:, :LANES] * (HEAD_DIM ** -0.5), kv[:, LANES:]
    kr, vr = pltpu.roll(kb, HEAD_DIM, 1), pltpu.roll(vb, HEAD_DIM, 1)
    lane = lax.broadcasted_iota(jnp.int32, kb.shape, 1)
    lo = lane < HEAD_DIM
    zero = jnp.zeros_like(kb)
    k_eff = [[None, None], [None, None]]
    v_eff = [[None, None], [None, None]]
    for h in range(2):
        for e in range(2):
            ksrc, vsrc = (kb, vb) if e == h else (kr, vr)
            keep = lo if e == 0 else jnp.logical_not(lo)
            k_eff[h][e] = jnp.where(keep, ksrc, zero)
            v_eff[h][e] = jnp.where(keep, vsrc, zero)
    return k_eff, v_eff


def _sink_column(sinks_ref, h, e):
    rowblk = lax.broadcasted_iota(jnp.int32, (4 * BLOCK, 1), 0) // BLOCK
    col = jnp.zeros((4 * BLOCK, 1), F32)
    for j in range(4):
        col = jnp.where(rowblk == j, sinks_ref[0, GROUP * h + 2 * j + e], col)
    return col


def _softmax_sink(s, valid, sink):
    s = jnp.where(valid, s, -jnp.inf)
    m = jnp.maximum(jnp.max(s, axis=-1, keepdims=True), sink)
    p = jnp.exp(s - m)
    psink = jnp.exp(sink - m)
    den = jnp.sum(p, axis=-1, keepdims=True) + psink
    inv = 1.0 / den
    return p * inv, psink * inv


def _shift_down(a, s, prev):
    rows = a.shape[0]
    out = pltpu.roll(a, s, 0)
    row = lax.broadcasted_iota(jnp.int32, a.shape, 0)
    for t in range(s):
        out = jnp.where(row == t, prev[SUBLANES_BF16 - s + t:SUBLANES_BF16 - s + t + 1, :], out)
    del rows
    return out


def _shift_up(a, s, nxt):
    rows = a.shape[0]
    out = pltpu.roll(a, rows - s, 0)
    row = lax.broadcasted_iota(jnp.int32, a.shape, 0)
    for t in range(s):
        out = jnp.where(row == rows - s + t, nxt[t:t + 1, :], out)
    return out


def _stack_pairs(ref, h, rows=slice(None)):
    return jnp.concatenate([ref[rows, (4 * h + j) * LANES:(4 * h + j + 1) * LANES] for j in range(4)], axis=0)


FWD_BLOCKS = 4


def _mixer_fwd(z, sinks, conv_w, d, ride=None):
    t, zw = z.shape
    kvw2 = zw - 6 * d
    tq = FWD_BLOCKS * BLOCK
    halo = tq // SUBLANES_BF16
    seg = _segments(d, kvw2)

    def body(z_ref, kvp_ref, prev_ref, sinks_ref, cw_ref, attn_ref, merged_ref):
        n = pl.program_id(0)
        band, col = _attn_masks()
        for b in range(FWD_BLOCKS):
            rows = slice(b * BLOCK, (b + 1) * BLOCK)
            before = slice((b - 1) * BLOCK, b * BLOCK)
            kv_prev = kvp_ref[...] if b == 0 else z_ref[before, seg["kv"]]
            kv = jnp.concatenate([kv_prev, z_ref[rows, seg["kv"]]], axis=0)
            k_eff, v_eff = _kv_variants(kv, kvw2 // 2)
            valid = band & ((n > 0) | (col >= BLOCK)) if b == 0 else band
            for h in range(2):
                q4 = _stack_pairs(z_ref, h, rows)
                o4 = jnp.zeros((4 * BLOCK, LANES), F32)
                for e in range(2):
                    s = lax.dot_general(q4, k_eff[h][e], (((1,), (1,)), ((), ())), preferred_element_type=F32)
                    p, _ = _softmax_sink(s, valid, _sink_column(sinks_ref, h, e))
                    o4 = o4 + jnp.dot(p.astype(BF16), v_eff[h][e], preferred_element_type=F32)
                for j in range(4):
                    attn_ref[rows, (4 * h + j) * LANES:(4 * h + j + 1) * LANES] = (
                        o4[j * BLOCK:(j + 1) * BLOCK].astype(BF16))
            cb = z_ref[rows, seg["cb"]].astype(F32)
            p_in = z_ref[rows, seg["cc"]].astype(F32) * z_ref[rows, seg["cx"]].astype(F32)
            if b == 0:
                prev = jnp.where(n > 0, prev_ref[:, seg["cc"]].astype(F32) * prev_ref[:, seg["cx"]].astype(F32), 0.0)
            else:
                tail = slice(b * BLOCK - SUBLANES_BF16, b * BLOCK)
                prev = z_ref[tail, seg["cc"]].astype(F32) * z_ref[tail, seg["cx"]].astype(F32)
            cconv = (cw_ref[0:1, :] * _shift_down(p_in, 2, prev) + cw_ref[1:2, :] * _shift_down(p_in, 1, prev)
                     + cw_ref[2:3, :] * p_in)
            sa = jax.nn.sigmoid(z_ref[rows, seg["ga"]].astype(F32))
            sg = jax.nn.sigmoid(z_ref[rows, seg["gc"]].astype(F32))
            merged_ref[rows, :] = (sa * attn_ref[rows, :].astype(F32) + sg * (cb * cconv)).astype(BF16)

    blk = pl.BlockSpec((tq, d), lambda n: (n, 0))
    return _pallas(
        body, name="mixer_fwd", grid=(t // tq,),
        in_specs=[pl.BlockSpec((tq, zw), lambda n: (n, 0)),
                  pl.BlockSpec((BLOCK, kvw2), lambda n: (jnp.maximum(n * FWD_BLOCKS - 1, 0), d // kvw2)),
                  pl.BlockSpec((SUBLANES_BF16, zw), lambda n: (jnp.maximum(n * halo - 1, 0), 0)),
                  SMEM_SPEC, pl.BlockSpec((3, d), lambda n: (0, 0))],
        out_specs=[blk, blk],
        out_shape=[SDS((t, d), BF16), SDS((t, d), BF16)],
        args=(z, z, z, sinks, conv_w), sem=("parallel",), ride=ride)


def _out_proj_fwd(merged, w_out, x, ga1, g_ffn, sc2, sh2, tm):
    t, d = x.shape

    def body(m_ref, w_ref, x_ref, ga_ref, g_ref, sc_ref, sh_ref, y_ref, x1_ref, h_ref):
        y = jnp.dot(m_ref[...], w_ref[...], preferred_element_type=F32)
        x1 = x_ref[...] + ga_ref[...] * y
        y_ref[...] = y.astype(BF16)
        x1_ref[...] = x1
        h_ref[...] = ((x1 * _rms(x1) * g_ref[...]) * (1.0 + sc_ref[...]) + sh_ref[...]).astype(BF16)

    row = pl.BlockSpec((tm, d), lambda i: (i, 0))
    vec = pl.BlockSpec((1, d), lambda i: (0, 0))
    return pl.pallas_call(
        body, name="out_proj_fwd", grid=(t // tm,),
        in_specs=[row, pl.BlockSpec((d, d), lambda i: (0, 0)), row, vec, vec, vec, vec],
        out_specs=[row, row, row],
        out_shape=[SDS((t, d), BF16), SDS((t, d), F32), SDS((t, d), BF16)],
        compiler_params=_params("parallel"))(merged, w_out, x, ga1, g_ffn, sc2, sh2)


def _ffn_in_fwd(h2, w, ff, tm, tn):
    t, d = h2.shape
    nj = ff // tn
    assert w.shape == (2 * nj, d, tn)

    def body(h_ref, wg_ref, wu_ref, gu_ref, act_ref):
        hh = h_ref[...]
        g = jnp.dot(hh, wg_ref[...], preferred_element_type=F32)
        u = jnp.dot(hh, wu_ref[...], preferred_element_type=F32)
        gu_ref[0] = g.astype(BF16)
        gu_ref[1] = u.astype(BF16)
        act_ref[...] = ((g * jax.nn.sigmoid(g)) * u).astype(BF16)

    return pl.pallas_call(
        body, name="ffn_in_fwd", grid=(nj, t // tm),
        in_specs=[pl.BlockSpec((tm, d), lambda j, i: (i, 0)), pl.BlockSpec((None, d, tn), lambda j, i: (j, 0, 0)),
                  pl.BlockSpec((None, d, tn), lambda j, i: (j + nj, 0, 0))],
        out_specs=[pl.BlockSpec((2, tm, tn), lambda j, i: (0, i, j)), pl.BlockSpec((tm, tn), lambda j, i: (i, j))],
        out_shape=[SDS((2, t, ff), BF16), SDS((t, ff), BF16)],
        compiler_params=_params("parallel", "parallel"))(h2, w, w)


def _ffn_out_loss(act, w, x1, target, ga2, g_final, tm):
    t, d = x1.shape
    ff = act.shape[1]

    def body(a_ref, w_ref, x1_ref, tg_ref, ga_ref, gf_ref, dx2_ref, dy2_ref, st_ref):
        @pl.when(pl.program_id(0) == 0)
        def _():
            st_ref[...] = jnp.zeros_like(st_ref)

        halves = [slice(k * (tm // 2), (k + 1) * (tm // 2)) for k in range(2)]
        y2s = [jnp.dot(a_ref[rows, :], w_ref[...], preferred_element_type=F32) for rows in halves]
        for rows, y2 in zip(halves, y2s):
            x2 = x1_ref[rows, :] + ga_ref[...] * y2
            r = _rms(x2)
            yn = x2 * r
            err = yn * gf_ref[...] - tg_ref[rows, :]
            loss = 0.5 * jnp.sum(jnp.mean(err * err, axis=-1, keepdims=True), axis=0, keepdims=True)
            dy = err * (1.0 / d)
            u = dy * gf_ref[...]
            dx2 = r * (u - yn * jnp.mean(u * yn, axis=-1, keepdims=True))
            dx2_ref[rows, :] = dx2
            dy2_ref[rows, :] = (ga_ref[...] * dx2).astype(BF16)
            st_ref[0:1, :] += jnp.sum(dx2 * y2, axis=0, keepdims=True)
            st_ref[1:2, :] += jnp.sum(dy * yn, axis=0, keepdims=True)
            st_ref[2:3, :] += jnp.broadcast_to(loss, (1, d))

    row = pl.BlockSpec((tm, d), lambda i: (i, 0))
    vec = pl.BlockSpec((1, d), lambda i: (0, 0))
    return pl.pallas_call(
        body, name="ffn_out_loss", grid=(t // tm,),
        in_specs=[pl.BlockSpec((tm, ff), lambda i: (i, 0)),
                  pl.BlockSpec((ff, d), lambda i: (0, 0), pipeline_mode=pl.Buffered(1)), row, row, vec, vec],
        out_specs=[row, row, pl.BlockSpec((8, d), lambda i: (0, 0))],
        out_shape=[SDS((t, d), F32), SDS((t, d), BF16), SDS((8, d), F32)],
        compiler_params=_params("arbitrary"))(act, w, x1, target, ga2, g_final)


def _ffn_out_bwd(dy2, w, gu, tm, tn):
    t, d = dy2.shape
    ff = w.shape[0]

    def body(dy_ref, w_ref, gu_ref, o_ref):
        dy = dy_ref[...]
        for lo in range(0, tn, 3 * LANES):
            cols = slice(lo, min(lo + 3 * LANES, tn))
            dact = lax.dot_general(dy, w_ref[cols, :], (((1,), (1,)), ((), ())), preferred_element_type=F32)
            g = gu_ref[0, :, cols].astype(F32)
            u = gu_ref[1, :, cols].astype(F32)
            sg = jax.nn.sigmoid(g)
            a = dact * sg
            du = a * g
            o_ref[0, :, cols] = (u * (a + du * (1.0 - sg))).astype(BF16)
            o_ref[1, :, cols] = du.astype(BF16)

    gu_spec = pl.BlockSpec((2, tm, tn), lambda j, i: (0, i, j))
    return pl.pallas_call(
        body, name="ffn_out_bwd", grid=(ff // tn, t // tm),
        in_specs=[pl.BlockSpec((tm, d), lambda j, i: (i, 0)), pl.BlockSpec((tn, d), lambda j, i: (j, 0)), gu_spec],
        out_specs=gu_spec, out_shape=SDS((2, t, ff), BF16),
        compiler_params=_params("parallel", "parallel"))(dy2, w, gu)


def _wgrad(a, b, a_spec, b_spec, out_spec, out_shape, grid, name, ride=None):
    def body(a_ref, b_ref, o_ref, o16_ref):
        k = pl.program_id(len(grid) - 1)

        @pl.when(k == 0)
        def _():
            o_ref[...] = jnp.zeros_like(o_ref)

        o_ref[...] += lax.dot_general(a_ref[...], b_ref[...], (((0,), (0,)), ((), ())), preferred_element_type=F32)

        @pl.when(k == grid[-1] - 1)
        def _():
            o16_ref[...] = o_ref[...].astype(BF16)

    return _pallas(
        body, name=name, grid=grid, in_specs=[a_spec, b_spec], out_specs=[out_spec, out_spec],
        out_shape=[out_shape, SDS(out_shape.shape, BF16)], args=(a, b),
        sem=["parallel"] * (len(grid) - 1) + ["arbitrary"], ride=ride)


def _ffn_in_bwd(dgu, w, x1, dx2, y1, g_ffn, sc2, ga1, tm):
    t, d = x1.shape
    ff = dgu.shape[2]
    n_sh, _, sw = w.shape
    per = ff // sw
    nt = (((1,), (1,)), ((), ()))

    def body(a_ref, w_ref, x1_ref, dx2_ref, y1_ref, g_ref, sc_ref, ga_ref, dx1_ref, dy1_ref, st_ref):
        @pl.when(pl.program_id(0) == 0)
        def _():
            st_ref[...] = jnp.zeros_like(st_ref)

        dh = None
        for j in range(n_sh):
            part = lax.dot_general(a_ref[j // per, :, (j % per) * sw:(j % per + 1) * sw], w_ref[j], nt,
                                   preferred_element_type=F32)
            dh = part if dh is None else dh + part
        x1 = x1_ref[...]
        r = _rms(x1)
        xn = x1 * r
        g = g_ref[...]
        dn = dh * (1.0 + sc_ref[...])
        u = dn * g
        dx1 = dx2_ref[...] + r * (u - xn * jnp.mean(u * xn, axis=-1, keepdims=True))
        dx1_ref[...] = dx1
        dy1_ref[...] = (ga_ref[...] * dx1).astype(BF16)
        st_ref[0:1, :] += jnp.sum(dh, axis=0, keepdims=True)
        st_ref[1:2, :] += jnp.sum(dh * (xn * g), axis=0, keepdims=True)
        st_ref[2:3, :] += jnp.sum(dn * xn, axis=0, keepdims=True)
        st_ref[3:4, :] += jnp.sum(dx1 * y1_ref[...].astype(F32), axis=0, keepdims=True)

    row = pl.BlockSpec((tm, d), lambda i: (i, 0))
    vec = pl.BlockSpec((1, d), lambda i: (0, 0))
    return pl.pallas_call(
        body, name="ffn_in_bwd", grid=(t // tm,),
        in_specs=[pl.BlockSpec((2, tm, ff), lambda i: (0, i, 0)),
                  pl.BlockSpec((n_sh, d, sw), lambda i: (0, 0, 0), pipeline_mode=pl.Buffered(1)),
                  row, row, row, vec, vec, vec],
        out_specs=[row, row, pl.BlockSpec((8, d), lambda i: (0, 0))],
        out_shape=[SDS((t, d), F32), SDS((t, d), BF16), SDS((8, d), F32)],
        compiler_params=_params("arbitrary"))(dgu, w, x1, dx2, y1, g_ffn, sc2, ga1)


def _out_proj_bwd(dy1, w_out, tm, ride=None):
    t, d = dy1.shape

    def body(dy_ref, w_ref, o_ref):
        o_ref[...] = lax.dot_general(dy_ref[...], w_ref[...], (((1,), (1,)), ((), ())),
                                     preferred_element_type=F32).astype(BF16)

    row = pl.BlockSpec((tm, d), lambda i: (i, 0))
    return _pallas(body, name="out_proj_bwd", grid=(t // tm,),
                   in_specs=[row, pl.BlockSpec((d, d), lambda i: (0, 0))], out_specs=row,
                   out_shape=SDS((t, d), BF16), args=(dy1, w_out), sem=("parallel",), ride=ride)


BWD_BLOCKS = 2


def _mixer_bwd(z, dmerged, attn, sinks, conv_w, d, ride=None):
    t, zw = z.shape
    kvw2 = zw - 6 * d
    tq = BWD_BLOCKS * BLOCK
    steps = t // tq
    halo = tq // SUBLANES_BF16
    last_halo = t // SUBLANES_BF16 - 1
    scale = HEAD_DIM ** -0.5
    seg = _segments(d, kvw2)

    def body(z_ref, kvp_ref, prev_ref, next_ref, dm_ref, dmn_ref, attn_ref, sinks_ref, cw_ref,
             dz_ref, dkv_ref, db_ref, dbkv_ref, dcw_ref, dsk_ref, carry_ref):
        n = pl.program_id(0)

        @pl.when(n == 0)
        def _():
            carry_ref[...] = jnp.zeros_like(carry_ref)
            db_ref[...] = jnp.zeros_like(db_ref)
            dbkv_ref[...] = jnp.zeros_like(dbkv_ref)
            dcw_ref[...] = jnp.zeros_like(dcw_ref)
            dsk_ref[...] = jnp.zeros_like(dsk_ref)

        def one_block(b, pending):
            rows = slice(b * BLOCK, (b + 1) * BLOCK)
            before = slice((b - 1) * BLOCK, b * BLOCK)
            dm = dm_ref[rows, :].astype(F32)
            sa = jax.nn.sigmoid(z_ref[rows, seg["ga"]].astype(F32))
            dga = dm * attn_ref[rows, :].astype(F32) * sa * (1.0 - sa)
            dz_ref[rows, seg["ga"]] = dga.astype(BF16)
            db_ref[0:1, seg["ga"]] += jnp.sum(dga, axis=0, keepdims=True)
            dattn = (dm * sa).astype(BF16)

            kv_prev = kvp_ref[...] if b == 0 else z_ref[before, seg["kv"]]
            kv = jnp.concatenate([kv_prev, z_ref[rows, seg["kv"]]], axis=0)
            k_eff, v_eff = _kv_variants(kv, kvw2 // 2)
            band, col = _attn_masks()
            valid = band & ((n > 0) | (col >= BLOCK)) if b == 0 else band
            lane_lo = lax.broadcasted_iota(jnp.int32, (2 * BLOCK, LANES), 1) < HEAD_DIM
            sink_lane = lax.broadcasted_iota(jnp.int32, (1, LANES), 1)
            rowblk = lax.broadcasted_iota(jnp.int32, (4 * BLOCK, 1), 0) // BLOCK
            dk_acc = [jnp.zeros((2 * BLOCK, LANES), F32), jnp.zeros((2 * BLOCK, LANES), F32)]
            dv_acc = [jnp.zeros((2 * BLOCK, LANES), F32), jnp.zeros((2 * BLOCK, LANES), F32)]
            dsink = jnp.zeros((1, LANES), F32)
            for h in range(2):
                q4 = _stack_pairs(z_ref, h, rows)
                do4 = jnp.concatenate([dattn[:, (4 * h + j) * LANES:(4 * h + j + 1) * LANES] for j in range(4)],
                                      axis=0)
                dq4 = jnp.zeros((4 * BLOCK, LANES), F32)
                for e in range(2):
                    s = lax.dot_general(q4, k_eff[h][e], (((1,), (1,)), ((), ())), preferred_element_type=F32)
                    p, psink = _softmax_sink(s, valid, _sink_column(sinks_ref, h, e))
                    dp = lax.dot_general(do4, v_eff[h][e], (((1,), (1,)), ((), ())), preferred_element_type=F32)
                    delta = jnp.sum(p * dp, axis=-1, keepdims=True)
                    ds = (p * (dp - delta)).astype(BF16)
                    dq4 = dq4 + jnp.dot(ds, k_eff[h][e], preferred_element_type=F32)
                    dk = lax.dot_general(q4, ds, (((0,), (0,)), ((), ())), preferred_element_type=F32).T
                    dv = lax.dot_general(do4, p.astype(BF16), (((0,), (0,)), ((), ())), preferred_element_type=F32).T
                    keep = lane_lo if e == 0 else jnp.logical_not(lane_lo)
                    slot = 0 if e == h else 1
                    dk_acc[slot] = dk_acc[slot] + jnp.where(keep, dk, 0.0)
                    dv_acc[slot] = dv_acc[slot] + jnp.where(keep, dv, 0.0)
                    dsk = -(psink * delta)
                    for j in range(4):
                        tot = jnp.sum(jnp.where(rowblk == j, dsk, 0.0), axis=0, keepdims=True)
                        dsink = dsink + jnp.where(sink_lane == GROUP * h + 2 * j + e, tot, 0.0)
                for j in range(4):
                    cols = slice((4 * h + j) * LANES, (4 * h + j + 1) * LANES)
                    dqj = dq4[j * BLOCK:(j + 1) * BLOCK]
                    dz_ref[rows, cols] = dqj.astype(BF16)
                    db_ref[0:1, cols] += jnp.sum(dqj, axis=0, keepdims=True)
            dsk_ref[0:1, :] += dsink
            dkv_new = jnp.concatenate([(dk_acc[0] + pltpu.roll(dk_acc[1], HEAD_DIM, 1)) * scale,
                                       dv_acc[0] + pltpu.roll(dv_acc[1], HEAD_DIM, 1)], axis=1)
            done = pending + dkv_new[:BLOCK]
            dkv_ref[rows, :] = done.astype(BF16)
            dbkv_ref[0:1, :] += jnp.sum(done, axis=0, keepdims=True)

            cb = z_ref[rows, seg["cb"]].astype(F32)
            cc = z_ref[rows, seg["cc"]].astype(F32)
            cx = z_ref[rows, seg["cx"]].astype(F32)
            sg = jax.nn.sigmoid(z_ref[rows, seg["gc"]].astype(F32))
            p_in = cc * cx
            if b == 0:
                prev = jnp.where(n > 0, prev_ref[:, seg["cc"]].astype(F32) * prev_ref[:, seg["cx"]].astype(F32), 0.0)
            else:
                tail = slice(b * BLOCK - SUBLANES_BF16, b * BLOCK)
                prev = z_ref[tail, seg["cc"]].astype(F32) * z_ref[tail, seg["cx"]].astype(F32)
            p_m1 = _shift_down(p_in, 1, prev)
            p_m2 = _shift_down(p_in, 2, prev)
            w0, w1, w2 = cw_ref[0:1, :], cw_ref[1:2, :], cw_ref[2:3, :]
            cconv = w0 * p_m2 + w1 * p_m1 + w2 * p_in
            dconv = dm * sg
            dgc = dm * (cb * cconv) * sg * (1.0 - sg)
            dcb = dconv * cconv
            dcc_t = dconv * cb
            if b == BWD_BLOCKS - 1:
                nxt = jnp.where(n < steps - 1,
                                dmn_ref[...].astype(F32) * jax.nn.sigmoid(next_ref[:, seg["gc"]].astype(F32))
                                * next_ref[:, seg["cb"]].astype(F32), 0.0)
            else:
                head = slice((b + 1) * BLOCK, (b + 1) * BLOCK + SUBLANES_BF16)
                nxt = (dm_ref[head, :].astype(F32) * jax.nn.sigmoid(z_ref[head, seg["gc"]].astype(F32))
                       * z_ref[head, seg["cb"]].astype(F32))
            dpin = w2 * dcc_t + w1 * _shift_up(dcc_t, 1, nxt) + w0 * _shift_up(dcc_t, 2, nxt)
            for nm, val in (("cb", dcb), ("cc", dpin * cx), ("cx", dpin * cc), ("gc", dgc)):
                dz_ref[rows, seg[nm]] = val.astype(BF16)
                db_ref[0:1, seg[nm]] += jnp.sum(val, axis=0, keepdims=True)
            dcw_ref[0:1, :] += jnp.sum(dcc_t * p_m2, axis=0, keepdims=True)
            dcw_ref[1:2, :] += jnp.sum(dcc_t * p_m1, axis=0, keepdims=True)
            dcw_ref[2:3, :] += jnp.sum(dcc_t * p_in, axis=0, keepdims=True)
            return dkv_new[BLOCK:]

        @pl.when(n < steps)
        def _():
            pending = carry_ref[...]
            for b in range(BWD_BLOCKS):
                pending = one_block(b, pending)
            carry_ref[...] = pending

        @pl.when(n == steps)
        def _():
            done = carry_ref[...]
            dkv_ref[:BLOCK, :] = done.astype(BF16)
            dkv_ref[BLOCK:, :] = jnp.zeros((tq - BLOCK, kvw2), BF16)
            dbkv_ref[0:1, :] += jnp.sum(done, axis=0, keepdims=True)

    def cur(n):
        return jnp.minimum(n, steps - 1)

    def after(n):
        return jnp.minimum((cur(n) + 1) * halo, last_halo)

    blk = pl.BlockSpec((tq, d), lambda n: (cur(n), 0))
    return _pallas(
        body, name="mixer_bwd", grid=(steps + 1,), ride=ride, sem=("arbitrary",),
        args=(z, z, z, z, dmerged, dmerged, attn, sinks, conv_w),
        in_specs=[pl.BlockSpec((tq, zw), lambda n: (cur(n), 0)),
                  pl.BlockSpec((BLOCK, kvw2), lambda n: (jnp.maximum(cur(n) * BWD_BLOCKS - 1, 0), d // kvw2)),
                  pl.BlockSpec((SUBLANES_BF16, zw), lambda n: (jnp.maximum(cur(n) * halo - 1, 0), 0)),
                  pl.BlockSpec((SUBLANES_BF16, zw), lambda n: (after(n), 0)),
                  blk,
                  pl.BlockSpec((SUBLANES_BF16, d), lambda n: (after(n), 0)),
                  blk, SMEM_SPEC, pl.BlockSpec((3, d), lambda n: (0, 0))],
        out_specs=[pl.BlockSpec((tq, zw), lambda n: (cur(n), 0)),
                   pl.BlockSpec((tq, kvw2), lambda n: (n, 0)),
                   pl.BlockSpec((8, zw), lambda n: (0, 0)), pl.BlockSpec((8, kvw2), lambda n: (0, 0)),
                   pl.BlockSpec((8, d), lambda n: (0, 0)), pl.BlockSpec((8, LANES), lambda n: (0, 0))],
        out_shape=[SDS((t, zw), BF16), SDS((t + tq, kvw2), BF16), SDS((8, zw), F32), SDS((8, kvw2), F32),
                   SDS((8, d), F32), SDS((8, LANES), F32)],
        scratch=[pltpu.VMEM((BLOCK, kvw2), F32)])


def _wgrad_in(dz, dkv, h1, tk, ride=None):
    t, zw = dz.shape
    d = h1.shape[1]
    kvw2 = dkv.shape[1]
    blk = d + kvw2
    assert zw % blk == 0
    tn = (((0,), (0,)), ((), ()))

    def body(a_ref, akv_ref, h_ref, o_ref, o16_ref):
        n, k = pl.program_id(0), pl.program_id(1)

        @pl.when(k == 0)
        def _():
            o_ref[...] = jnp.zeros_like(o_ref)

        @pl.when(n == 0)
        def _():
            o_ref[:d, :] += lax.dot_general(a_ref[:, :d], h_ref[...], tn, preferred_element_type=F32)
            o_ref[d:, :] += lax.dot_general(akv_ref[...], h_ref[...], tn, preferred_element_type=F32)

        @pl.when(n > 0)
        def _():
            o_ref[...] += lax.dot_general(a_ref[...], h_ref[...], tn, preferred_element_type=F32)

        @pl.when(k == t // tk - 1)
        def _():
            o16_ref[...] = o_ref[...].astype(BF16)

    out_spec = pl.BlockSpec((blk, d), lambda n, k: (n, 0))
    return _pallas(
        body, name="wgrad_in", grid=(zw // blk, t // tk),
        in_specs=[pl.BlockSpec((tk, blk), lambda n, k: (k, n)), pl.BlockSpec((tk, kvw2), lambda n, k: (k, 0)),
                  pl.BlockSpec((tk, d), lambda n, k: (k, 0))],
        out_specs=[out_spec, out_spec], out_shape=[SDS((zw, d), F32), SDS((zw, d), BF16)],
        args=(dz, dkv, h1), sem=("parallel", "arbitrary"), ride=ride)


def _in_proj_bwd(dz, dkv, wt, x, dx1, g_mix, sc1, tm, ride=None):
    t, d = x.shape
    zw = dz.shape[1]
    kvw2 = dkv.shape[1]
    rest = d + kvw2

    def body(a_ref, akv_ref, w_ref, x_ref, dx1_ref, g_ref, sc_ref, gx_ref, st_ref):
        @pl.when(pl.program_id(0) == 0)
        def _():
            st_ref[...] = jnp.zeros_like(st_ref)

        dh = (jnp.dot(a_ref[:, :d], w_ref[:d, :], preferred_element_type=F32)
              + jnp.dot(akv_ref[...], w_ref[d:rest, :], preferred_element_type=F32)
              + jnp.dot(a_ref[:, rest:], w_ref[rest:, :], preferred_element_type=F32))
        xx = x_ref[...]
        r = _rms(xx)
        xn = xx * r
        g = g_ref[...]
        dn = dh * (1.0 + sc_ref[...])
        u = dn * g
        gx_ref[...] = dx1_ref[...] + r * (u - xn * jnp.mean(u * xn, axis=-1, keepdims=True))
        st_ref[0:1, :] += jnp.sum(dh, axis=0, keepdims=True)
        st_ref[1:2, :] += jnp.sum(dh * (xn * g), axis=0, keepdims=True)
        st_ref[2:3, :] += jnp.sum(dn * xn, axis=0, keepdims=True)

    row = pl.BlockSpec((tm, d), lambda i: (i, 0))
    vec = pl.BlockSpec((1, d), lambda i: (0, 0))
    return _pallas(
        body, name="in_proj_bwd", grid=(t // tm,),
        in_specs=[pl.BlockSpec((tm, zw), lambda i: (i, 0)), pl.BlockSpec((tm, kvw2), lambda i: (i, 0)),
                  pl.BlockSpec((zw, d), lambda i: (0, 0), pipeline_mode=pl.Buffered(1)),
                  row, row, vec, vec],
        out_specs=[row, pl.BlockSpec((8, d), lambda i: (0, 0))],
        out_shape=[SDS((t, d), F32), SDS((8, d), F32)],
        args=(dz, dkv, wt, x, dx1, g_mix, sc1), sem=("arbitrary",), ride=ride)


def _to_lanes(v, rows=None):
    flat = v.reshape(-1)
    need = -(-flat.shape[0] // LANES)
    need = -(-need // 8) * 8 if rows is None else rows
    return jnp.pad(flat, (0, need * LANES - flat.shape[0])).reshape(need, LANES)


def kernel(x, c, w_ada, b_ada, g_mix, w_in, b_in, sinks, conv_w, w_out, g_ffn, w_ffn_in, w_ffn_out, g_final, loss_target, m_w_ada, m_b_ada, m_g_mix, m_w_in, m_b_in, m_sinks, m_conv_w, m_w_out, m_g_ffn, m_w_ffn_in, m_w_ffn_out, m_g_final, v_w_ada, v_b_ada, v_g_mix, v_w_in, v_b_in, v_sinks, v_conv_w, v_w_out, v_g_ffn, v_w_ffn_in, v_w_ffn_out, v_g_final):
    xs, tgt = x[0], loss_target[0]
    t, d = xs.shape
    zw = w_in.shape[2] * N_CHIP
    kvw2 = zw - 6 * d
    ff = w_ffn_out.shape[1] * N_CHIP
    n_mod = w_ada.shape[2] * N_CHIP // d
    mod_sh = w_ada.shape[2]
    cw_sh = conv_w.shape[2]
    assert d % (8 * LANES) == 0 and kvw2 == 2 * LANES and t % 512 == 0 and n_mod == 6
    xi, yi, ci = _mesh_pos()
    j_me = 2 * xi + yi
    b_me = 4 * xi + 2 * yi + ci
    pos = jnp.stack([ci, j_me]).astype(jnp.int32)
    tm = 512

    w_in_t, m_w_in_t, v_w_in_t = w_in[0].T, m_w_in[0].T, v_w_in[0].T
    assert d == 8 * LANES
    pack1 = jnp.concatenate([c.reshape(d // LANES, LANES), conv_w[0].reshape(-1, LANES)], axis=0)
    pack1 = jnp.pad(pack1, ((0, 16 - pack1.shape[0]), (0, 0)))
    b_ada_sh = lax.dynamic_slice(b_ada, (0, j_me * mod_sh), (1, mod_sh))
    g1, mod_all, w_in_g = _startup(pack1, w_ada[0], b_ada_sh, _cast_into_block(pos, w_in_t, "cast_w_in"))
    c_all = g1[:, :d // LANES, :].reshape(N_DEV, d)
    cw_rows = 3 * cw_sh // LANES
    conv_w_full = jnp.concatenate(
        [g1[2 * j, d // LANES:d // LANES + cw_rows, :].reshape(3, cw_sh) for j in range(N_CHIP)], axis=1)
    mod = jnp.concatenate([lax.dynamic_index_in_dim(mod_all[2 * j], b_me, 0, keepdims=True) for j in range(N_CHIP)],
                          axis=1)
    sh1, sc1, ga1, sh2, sc2, ga2 = [mod[:, k * d:(k + 1) * d] for k in range(6)]
    w_in_tf = w_in_g.reshape(zw, d)
    later = [_cast_into_block(pos, w_out[0], "cast_w_out"), _cast_into_block(pos, w_ffn_in[0], "cast_w_ffn_in"),
             _cast_into_block(pos, w_ffn_out[0], "cast_w_ffn_out")]

    (z, h1), later = _in_proj(xs, g_mix, sc1, sh1, w_in_tf, b_in, min(t, 1024), zw // 5, ride=_x_gather_ici(later))
    (attn, merged), later = _mixer_fwd(z, sinks, conv_w_full, d, ride=_x_gather_d2d(later))
    w_out_f = later[0].reshape(d, d)
    w_ffn_in_f = later[1]
    w_ffn_out_f = later[2].reshape(ff, d)
    tml = min(t, 1024)
    y1, x1, h2 = _out_proj_fwd(merged, w_out_f, xs, ga1, g_ffn, sc2, sh2, tml)
    gu, act = _ffn_in_fwd(h2, w_ffn_in_f, ff, tml, ff // 2)
    dx2, dy2, st_loss = _ffn_out_loss(act, w_ffn_out_f, x1, tgt, ga2, g_final.reshape(1, d), tml)

    dgu = _ffn_out_bwd(dy2, w_ffn_out_f, gu, tml, ff // 2)
    tk = min(t, 2048)
    dw_ffn_out, _ = _wgrad(
        act, dy2, pl.BlockSpec((tk, ff // 2), lambda m, k: (k, m)), pl.BlockSpec((tk, d), lambda m, k: (k, 0)),
        pl.BlockSpec((ff // 2, d), lambda m, k: (m, 0)), SDS((ff, d), F32), (2, t // tk), "wgrad_ffn_out")
    dx1, dy1, st_ffn = _ffn_in_bwd(dgu, w_ffn_in_f, x1, dx2, y1, g_ffn, sc2, ga1, tm)
    dw_ffn_in, _ = _wgrad(
        h2, dgu, pl.BlockSpec((tk, d), lambda n, k: (k, 0)),
        pl.BlockSpec((None, tk, ff // 2), lambda n, k: (n // 2, k, n % 2)),
        pl.BlockSpec((None, d, ff // 2), lambda n, k: (n, 0, 0)), SDS((N_CHIP, d, ff // 2), F32),
        (N_CHIP, t // tk), "wgrad_ffn_in")
    dw_out, _ = _wgrad(
        merged, dy1, pl.BlockSpec((tk, d), lambda m, k: (k, 0)), pl.BlockSpec((tk, d), lambda m, k: (k, 0)),
        pl.BlockSpec((d, d), lambda m, k: (0, 0)), SDS((d, d), F32), (1, t // tk), "wgrad_out")

    early = [[g.reshape(N_CHIP, -1, g.shape[-1]) for g in pair] for pair in (dw_out, dw_ffn_in, dw_ffn_out)]
    early_names = ["w_out", "w_ffn_in", "w_ffn_out"]
    dmerged, _ = _out_proj_bwd(dy1, w_out_f, tml)
    (dz, dkv_shifted, db_z, db_kv, dcw, dsk), terms = _mixer_bwd(
        z, dmerged, attn, sinks, conv_w_full, d, ride=_x_reduce([e[0] for e in early], [e[1] for e in early]))
    dkv = dkv_shifted[BLOCK:BLOCK + t]
    fulls = [_sum_terms(pos, e[0], s, r, "sum_terms_" + nm)
             for e, s, r, nm in zip(early, terms[:3], terms[3:], early_names)]
    dw_in_t, (g_w_out, g_w_ffn_in, g_w_ffn_out) = _wgrad_in(dz, dkv, h1, tk, ride=_x_pair_exchange(fulls))
    dw_in_t = [g.reshape(N_CHIP, zw // N_CHIP, d) for g in dw_in_t]

    (grad_x, st_in), (from_sib, from_far) = _in_proj_bwd(dz, dkv, w_in_tf, xs, dx1, g_mix, sc1, tm,
                                                         ride=_x_reduce([dw_in_t[0]], [dw_in_t[1]]))
    (g_w_in_t,) = _exchange(_x_pair_exchange([_sum_terms(pos, dw_in_t[0], from_sib, from_far, "sum_terms_w_in")]),
                            "pair_exchange")

    dmod = jnp.concatenate([st_in[0:1], st_in[1:2], st_ffn[3:4], st_ffn[0:1], st_ffn[1:2], st_loss[0:1]], axis=1)
    db_in = jnp.concatenate([db_z[0:1, :d], db_kv[0:1], db_z[0:1, d + kvw2:]], axis=1)
    seg = [dmod, st_in[2:3], db_in, dsk[0:1], dcw[0:3].reshape(1, 3 * d), st_ffn[2:3], st_loss[1:2],
           st_loss[2:3, :LANES]]
    sizes = [s.shape[1] for s in seg]
    pack2 = _to_lanes(jnp.concatenate(seg, axis=1))
    packs = _all_gather_small(pack2, "gather_small_grads")
    tot = _pack_sum(packs).reshape(-1)
    offs = [sum(sizes[:k]) for k in range(len(sizes))]
    gb_ada, gg_mix, gb_in, gsinks, gcw, gg_ffn, gg_final, loss_v = [tot[o:o + s] for o, s in zip(offs, sizes)]
    loss = loss_v[0]
    gsinks = gsinks[:sinks.shape[1]]
    gcw_sh = lax.dynamic_slice(gcw.reshape(3, d), (0, j_me * cw_sh), (3, cw_sh))

    dmod_all = packs[:, :n_mod * d // LANES, :].reshape(N_DEV, n_mod * d)
    g_w_ada = _ada_wgrad(c_all, lax.dynamic_slice(dmod_all, (0, j_me * mod_sh), (N_DEV, mod_sh)))

    out_g, out_d, out_m, out_v = {}, {}, {}, {}
    big = {"w_ada": (w_ada[0], g_w_ada, m_w_ada[0], v_w_ada[0]),
           "w_out": (w_out[0], g_w_out, m_w_out[0], v_w_out[0]),
           "w_ffn_in": (w_ffn_in[0], g_w_ffn_in, m_w_ffn_in[0], v_w_ffn_in[0]),
           "w_ffn_out": (w_ffn_out[0], g_w_ffn_out, m_w_ffn_out[0], v_w_ffn_out[0])}
    for nm, (w, g, m, v) in big.items():
        out_g[nm], out_d[nm], out_m[nm], out_v[nm] = [o[None] for o in _adamw(w, g, m, v, "adamw_" + nm)]
    out_g["w_in"], out_d["w_in"], out_m["w_in"], out_v["w_in"] = [
        o.T[None] for o in _adamw(w_in_t, g_w_in_t, m_w_in_t, v_w_in_t, "adamw_w_in")]
    small = {"b_ada": (b_ada, gb_ada, m_b_ada, v_b_ada), "g_mix": (g_mix, gg_mix, m_g_mix, v_g_mix),
             "b_in": (b_in, gb_in, m_b_in, v_b_in), "sinks": (sinks, gsinks, m_sinks, v_sinks),
             "conv_w": (conv_w, gcw_sh, m_conv_w, v_conv_w), "g_ffn": (g_ffn, gg_ffn, m_g_ffn, v_g_ffn),
             "g_final": (g_final, gg_final, m_g_final, v_g_final)}
    s_sizes = [w.size for w, _, _, _ in small.values()]
    s_rows = -(-sum(s_sizes) // LANES // 8) * 8

    def s_pack(k):
        return _to_lanes(jnp.concatenate([tup[k].reshape(-1) for tup in small.values()]), s_rows)

    s_out = _adamw(s_pack(0), s_pack(1), s_pack(2), s_pack(3), "adamw_small")
    s_off = 0
    for (nm, (w, g, _, _)), sz in zip(small.items(), s_sizes):
        out_g[nm] = g.reshape(w.shape)
        out_d[nm], out_m[nm], out_v[nm] = [o.reshape(-1)[s_off:s_off + sz].reshape(w.shape) for o in s_out[1:]]
        s_off += sz

    order = ["w_ada", "b_ada", "g_mix", "w_in", "b_in", "sinks", "conv_w", "w_out", "g_ffn", "w_ffn_in", "w_ffn_out",
             "g_final"]
    return (loss, grad_x[None], *[out_g[k] for k in order], *[out_d[k] for k in order],
            *[out_m[k] for k in order], *[out_v[k] for k in order])
```

```python
import functools

import jax
import jax.numpy as jnp
from jax import lax
from jax.experimental import pallas as pl
from jax.experimental.pallas import tpu as pltpu

F32 = jnp.float32
BF16 = jnp.bfloat16
EPS = 1e-6
HEAD_DIM = 64
GROUP = 8
BLOCK = 128
LANES = 128
SUBLANES_BF16 = 16
N_DEV = 8
N_CHIP = 4
VMEM_LIMIT = 56 * 1024 * 1024
MESH = pl.DeviceIdType.MESH

ADAM_LR = 0.001
ADAM_B1 = 0.9
ADAM_B2 = 0.999
ADAM_EPS = 1e-08
ADAM_WD = 0.01
ADAM_STEP = 10

SDS = jax.ShapeDtypeStruct
ANY = pl.BlockSpec(memory_space=pl.ANY)
VMEM_SPEC = pl.BlockSpec(memory_space=pltpu.VMEM)
SMEM_SPEC = pl.BlockSpec(memory_space=pltpu.SMEM)


def _params(*sem):
    return pltpu.CompilerParams(dimension_semantics=sem, vmem_limit_bytes=VMEM_LIMIT)


def _mesh_pos():
    return lax.axis_index("x"), lax.axis_index("y"), lax.axis_index("c")


def _row_tile(rows, cols, itemsize=4, budget=1 << 20, mult=8):
    best = None
    for t in range(mult, rows + 1, mult):
        if rows % t == 0 and t * cols * itemsize <= budget:
            best = t
    if best is None:
        best = rows
    return best


def _gather_all(v_ref, out_ref, send_sems, recv_sems, local_sem):
    x, y, c = _mesh_pos()
    me = 4 * x + 2 * y + c
    mine = pltpu.make_async_copy(v_ref, out_ref.at[me], local_sem)
    mine.start()
    peers = []
    for k in range(1, N_DEV):
        px = 1 - x if k & 4 else x
        py = 1 - y if k & 2 else y
        pc = 1 - c if k & 1 else c
        peers.append((px, py, pc))

    def copy(k, block):
        return pltpu.make_async_remote_copy(
            src_ref=v_ref, dst_ref=out_ref.at[block], send_sem=send_sems.at[k], recv_sem=recv_sems.at[k],
            device_id=peers[k], device_id_type=MESH)

    sends = [copy(k, me) for k in range(N_DEV - 1)]
    for cp in sends:
        cp.start()
    for k, (px, py, pc) in enumerate(peers):
        copy(k, 4 * px + 2 * py + pc).wait_recv()
    for cp in sends:
        cp.wait_send()
    mine.wait()


def _small_sems():
    return [pltpu.SemaphoreType.DMA((N_DEV - 1,)), pltpu.SemaphoreType.DMA((N_DEV - 1,)), pltpu.SemaphoreType.DMA]


def _all_gather_small(v, name):
    return pl.pallas_call(
        functools.partial(_gather_all), name=name, out_shape=SDS((N_DEV,) + v.shape, v.dtype),
        in_specs=[VMEM_SPEC], out_specs=VMEM_SPEC, scratch_shapes=_small_sems())(v)


def _other_chips(x, y):
    return [(1 - x, y), (x, 1 - y), (1 - x, 1 - y)]


def _startup(pack, w_ada_sh, b_ada_sh, w_buf):
    d, n = w_ada_sh.shape
    kc = d // LANES

    def body(pack_ref, wa_hbm, ba_ref, w_in_unused, packs_ref, mod_ref, w_ref, wa_scr, mod_scr,
             s1, r1, l1, s2, r2, l2, send_sems, recv_sems, fsend_sems, frecv_sems, relay_send, relay_recv, wa_sem):
        del w_in_unused
        x, y, c = _mesh_pos()
        j_me = 2 * x + y
        chips = _other_chips(x, y)
        half = w_ref.shape[1] // 2

        def rows_of(which):
            return pl.ds(pl.multiple_of(which * half, SUBLANES_BF16), half)

        def copy(p, block, rows, over_ici):
            sems = (send_sems, recv_sems) if over_ici else (fsend_sems, frecv_sems)
            return pltpu.make_async_remote_copy(
                src_ref=w_ref.at[block, rows], dst_ref=w_ref.at[block, rows], send_sem=sems[0].at[p],
                recv_sem=sems[1].at[p], device_id=(*chips[p], c) if over_ici else (x, y, 1 - c), device_id_type=MESH)

        def block_of(p):
            return 2 * chips[p][0] + chips[p][1]

        def relay(q, block):
            rows = pl.ds(pl.multiple_of(c * half + q * (half // 2), SUBLANES_BF16), half // 2)
            return pltpu.make_async_remote_copy(
                src_ref=w_ref.at[block, rows], dst_ref=w_ref.at[block, rows], send_sem=relay_send.at[q],
                recv_sem=relay_recv.at[q], device_id=(*chips[1 - q], c), device_id_type=MESH)

        load_wa = pltpu.make_async_copy(wa_hbm, wa_scr, wa_sem)
        load_wa.start()
        _gather_all(pack_ref, packs_ref, s1, r1, l1)
        sends = [copy(p, j_me, rows_of(c), True) for p in range(2)]
        for cp in sends:
            cp.start()
        load_wa.wait()
        acc = jnp.zeros((N_DEV, n), F32)
        for k in range(kc):
            ck = packs_ref[:, k, :]
            sk = (ck * jax.nn.sigmoid(ck)).astype(BF16)
            acc = acc + jnp.dot(sk, wa_scr[k * LANES:(k + 1) * LANES, :].astype(BF16), preferred_element_type=F32)
        mod_scr[...] = acc + ba_ref[...]
        _gather_all(mod_scr, mod_ref, s2, r2, l2)
        passed = []
        for q in range(2):
            copy(q, block_of(q), rows_of(c), True).wait_recv()
            for cp in (relay(q, block_of(q)), copy(q, block_of(q), rows_of(c), False)):
                cp.start()
                passed.append(cp)
        for q in range(2):
            relay(q, block_of(2)).wait_recv()
        fw = copy(2, block_of(2), rows_of(c), False)
        fw.start()
        for p in range(3):
            copy(p, block_of(p), rows_of(1 - c), False).wait_recv()
        for cp in sends + passed + [fw]:
            cp.wait_send()

    return pl.pallas_call(
        body, name="startup",
        out_shape=[SDS((N_DEV,) + pack.shape, F32), SDS((N_DEV, N_DEV, n), F32), SDS(w_buf.shape, w_buf.dtype)],
        in_specs=[VMEM_SPEC, ANY, VMEM_SPEC, ANY], out_specs=[VMEM_SPEC, VMEM_SPEC, ANY],
        input_output_aliases={3: 2},
        scratch_shapes=[pltpu.VMEM((d, n), F32), pltpu.VMEM((N_DEV, n), F32)] + _small_sems() + _small_sems()
        + [pltpu.SemaphoreType.DMA((3,))] * 4 + [pltpu.SemaphoreType.DMA((2,))] * 2 + [pltpu.SemaphoreType.DMA],
        compiler_params=pltpu.CompilerParams(vmem_limit_bytes=VMEM_LIMIT),
    )(pack, w_ada_sh, b_ada_sh, w_buf)


class _Exchange:
    def __init__(self, operands, out_shape, in_place, n_sems, copies):
        self.operands, self.out_shape, self.in_place, self.n_sems, self.copies = (
            list(operands), list(out_shape), in_place, n_sems, copies)

    def sems(self):
        return [pltpu.SemaphoreType.DMA((self.n_sems,)), pltpu.SemaphoreType.DMA((self.n_sems,))]


def _x_gather_ici(bufs):
    def copies(ins, outs, send_sems, recv_sems):
        x, y, c = _mesh_pos()
        chips = _other_chips(x, y)
        out = []
        for w in range(len(outs)):
            half = outs[w].shape[1] // 2
            rows = pl.ds(pl.multiple_of(c * half, SUBLANES_BF16), half)
            for p in range(3):
                out.append(pltpu.make_async_remote_copy(
                    src_ref=outs[w].at[2 * x + y, rows], dst_ref=outs[w].at[2 * x + y, rows],
                    send_sem=send_sems.at[w * 3 + p], recv_sem=recv_sems.at[w * 3 + p],
                    device_id=(*chips[p], c), device_id_type=MESH))
        return out

    return _Exchange(bufs, [SDS(b.shape, b.dtype) for b in bufs], True, 3 * len(bufs), copies)


def _x_gather_d2d(bufs):
    def copies(ins, outs, send_sems, recv_sems):
        x, y, c = _mesh_pos()
        chips = _other_chips(x, y)
        out = []
        for w in range(len(outs)):
            half = outs[w].shape[1] // 2
            rows = pl.ds(pl.multiple_of(c * half, SUBLANES_BF16), half)
            for p in range(3):
                block = 2 * chips[p][0] + chips[p][1]
                out.append(pltpu.make_async_remote_copy(
                    src_ref=outs[w].at[block, rows], dst_ref=outs[w].at[block, rows],
                    send_sem=send_sems.at[w * 3 + p], recv_sem=recv_sems.at[w * 3 + p],
                    device_id=(x, y, 1 - c), device_id_type=MESH))
        return out

    return _Exchange(bufs, [SDS(b.shape, b.dtype) for b in bufs], True, 3 * len(bufs), copies)


N_REMOTE = 6


def _x_reduce(grads32, grads16):
    n_w = len(grads32)

    def copies(ins, outs, send_sems, recv_sems):
        g32, g16 = ins[:n_w], ins[n_w:]
        from_sib, from_far = outs[:n_w], outs[n_w:]
        x, y, c = _mesh_pos()
        chips = _other_chips(x, y)
        out = []
        for w in range(n_w):
            half = g32[w].shape[1] // 2
            k0 = w * (N_REMOTE + 1)
            out.append(pltpu.make_async_remote_copy(
                src_ref=g32[w].at[2 * x + y, pl.ds(pl.multiple_of((1 - c) * half, SUBLANES_BF16), half), :],
                dst_ref=from_sib[w], send_sem=send_sems.at[k0], recv_sem=recv_sems.at[k0],
                device_id=(x, y, 1 - c), device_id_type=MESH))
            for p in range(3):
                for f in range(2):
                    tc = c if f == 0 else 1 - c
                    k = 2 * p + f
                    out.append(pltpu.make_async_remote_copy(
                        src_ref=g16[w].at[2 * chips[p][0] + chips[p][1],
                                          pl.ds(pl.multiple_of(tc * half, SUBLANES_BF16), half), :],
                        dst_ref=from_far[w].at[k], send_sem=send_sems.at[k0 + 1 + k], recv_sem=recv_sems.at[k0 + 1 + k],
                        device_id=(*chips[p], tc), device_id_type=MESH))
        return out

    shapes = ([SDS((g.shape[1] // 2, g.shape[2]), g.dtype) for g in grads32]
              + [SDS((N_REMOTE, g.shape[1] // 2, g.shape[2]), g.dtype) for g in grads16])
    return _Exchange(list(grads32) + list(grads16), shapes, False, (N_REMOTE + 1) * n_w, copies)


def _x_pair_exchange(fulls):
    def copies(ins, outs, send_sems, recv_sems):
        x, y, c = _mesh_pos()
        out = []
        for w in range(len(outs)):
            half = outs[w].shape[0] // 2
            rows = pl.ds(pl.multiple_of(c * half, 8), half)
            out.append(pltpu.make_async_remote_copy(
                src_ref=outs[w].at[rows], dst_ref=outs[w].at[rows], send_sem=send_sems.at[w],
                recv_sem=recv_sems.at[w], device_id=(x, y, 1 - c), device_id_type=MESH))
        return out

    return _Exchange(fulls, [SDS(f.shape, f.dtype) for f in fulls], True, len(fulls), copies)


def _pallas(body, *, name, grid, in_specs, out_specs, out_shape, args, scratch=(), sem=None, ride=None):
    single = not isinstance(out_specs, (list, tuple))
    out_specs_l = [out_specs] if single else list(out_specs)
    out_shape_l = [out_shape] if single else list(out_shape)
    n_in, n_out, n_scr = len(in_specs), len(out_specs_l), len(scratch)
    if ride is None:
        res = pl.pallas_call(body, name=name, grid=grid, in_specs=list(in_specs), out_specs=out_specs,
                             out_shape=out_shape, scratch_shapes=list(scratch), compiler_params=_params(*sem))(*args)
        return res, None
    n_x, n_xo = len(ride.operands), len(ride.out_shape)

    def full_body(*refs):
        ins, x_ins = refs[:n_in], refs[n_in:n_in + n_x]
        outs = refs[n_in + n_x:n_in + n_x + n_out]
        x_outs = refs[n_in + n_x + n_out:n_in + n_x + n_out + n_xo]
        rest = refs[n_in + n_x + n_out + n_xo:]
        scr, (send_sems, recv_sems) = rest[:n_scr], rest[n_scr:]
        first = functools.reduce(jnp.logical_and, [pl.program_id(a) == 0 for a in range(len(grid))])
        last = functools.reduce(jnp.logical_and, [pl.program_id(a) == grid[a] - 1 for a in range(len(grid))])

        @pl.when(first)
        def _():
            for cp in ride.copies(x_ins, x_outs, send_sems, recv_sems):
                cp.start()

        body(*ins, *outs, *scr)

        @pl.when(last)
        def _():
            for cp in ride.copies(x_ins, x_outs, send_sems, recv_sems):
                cp.wait()

    res = pl.pallas_call(
        full_body, name=name, grid=grid, in_specs=list(in_specs) + [ANY] * n_x,
        out_specs=out_specs_l + [ANY] * n_xo, out_shape=out_shape_l + ride.out_shape,
        input_output_aliases={n_in + k: n_out + k for k in range(n_x)} if ride.in_place else {},
        scratch_shapes=list(scratch) + ride.sems(),
        compiler_params=_params(*(["arbitrary"] * len(grid))))(*args, *ride.operands)
    own = res[0] if single else list(res[:n_out])
    return own, list(res[n_out:])


def _exchange(ride, name):
    n_x, n_xo = len(ride.operands), len(ride.out_shape)

    def body(*refs):
        x_ins, x_outs = refs[:n_x], refs[n_x:n_x + n_xo]
        send_sems, recv_sems = refs[n_x + n_xo:]
        copies = ride.copies(x_ins, x_outs, send_sems, recv_sems)
        for cp in copies:
            cp.start()
        for cp in copies:
            cp.wait()

    return pl.pallas_call(
        body, name=name, in_specs=[ANY] * n_x, out_specs=[ANY] * n_xo, out_shape=ride.out_shape,
        input_output_aliases={k: k for k in range(n_x)} if ride.in_place else {},
        scratch_shapes=ride.sems())(*ride.operands)


def _cast_into_block(pos, w, name):
    rows, cols = w.shape
    tr = _row_tile(rows, cols, mult=SUBLANES_BF16)

    def body(pos_ref, w_ref, o_ref):
        del pos_ref
        o_ref[...] = w_ref[...].astype(BF16)

    return pl.pallas_call(
        body, name=name,
        grid_spec=pltpu.PrefetchScalarGridSpec(
            num_scalar_prefetch=1, grid=(rows // tr,),
            in_specs=[pl.BlockSpec((tr, cols), lambda i, pos_ref: (i, 0))],
            out_specs=pl.BlockSpec((None, tr, cols), lambda i, pos_ref: (pos_ref[1], i, 0))),
        out_shape=SDS((N_CHIP, rows, cols), BF16), compiler_params=_params("parallel"))(pos, w)


def _sum_terms(pos, grad, from_sib, from_far, name):
    _, rows, cols = grad.shape
    half = rows // 2
    tr = _row_tile(half, cols, mult=SUBLANES_BF16)
    nblk = half // tr

    def body(pos_ref, g_ref, s_ref, r_ref, o_ref):
        del pos_ref
        acc = g_ref[...] + s_ref[...]
        for k in range(N_REMOTE):
            acc = acc + r_ref[k].astype(F32)
        o_ref[...] = acc

    return pl.pallas_call(
        body, name=name,
        grid_spec=pltpu.PrefetchScalarGridSpec(
            num_scalar_prefetch=1, grid=(nblk,),
            in_specs=[pl.BlockSpec((None, tr, cols), lambda i, pos_ref: (pos_ref[1], pos_ref[0] * nblk + i, 0)),
                      pl.BlockSpec((tr, cols), lambda i, pos_ref: (i, 0)),
                      pl.BlockSpec((N_REMOTE, tr, cols), lambda i, pos_ref: (0, i, 0))],
            out_specs=pl.BlockSpec((tr, cols), lambda i, pos_ref: (pos_ref[0] * nblk + i, 0))),
        out_shape=SDS((rows, cols), F32),
        compiler_params=_params("parallel"),
    )(pos, grad, from_sib, from_far)


def _adamw(w, g, m, v, name):
    rows, cols = w.shape
    tr = _row_tile(rows, cols)

    def body(w_ref, g_ref, m_ref, v_ref, go_ref, d_ref, nm_ref, nv_ref):
        gg = g_ref[...]
        go_ref[...] = gg
        nm = ADAM_B1 * m_ref[...] + (1.0 - ADAM_B1) * gg
        nv = ADAM_B2 * v_ref[...] + (1.0 - ADAM_B2) * (gg * gg)
        m_hat = nm / (1.0 - ADAM_B1 ** ADAM_STEP)
        v_hat = nv / (1.0 - ADAM_B2 ** ADAM_STEP)
        d_ref[...] = -ADAM_LR * (m_hat / (jnp.sqrt(v_hat) + ADAM_EPS) + ADAM_WD * w_ref[...])
        nm_ref[...] = nm
        nv_ref[...] = nv

    spec = pl.BlockSpec((tr, cols), lambda i: (i, 0))
    return pl.pallas_call(body, name=name, grid=(rows // tr,), in_specs=[spec] * 4, out_specs=[spec] * 4,
                          out_shape=[SDS((rows, cols), F32)] * 4, compiler_params=_params("parallel"))(w, g, m, v)


def _pack_sum(gathered):
    _, rows, cols = gathered.shape

    def body(g_ref, o_ref):
        acc = g_ref[0]
        for d in range(1, N_DEV):
            acc = acc + g_ref[d]
        o_ref[...] = acc

    return pl.pallas_call(body, name="pack_sum", in_specs=[VMEM_SPEC], out_specs=VMEM_SPEC,
                          out_shape=SDS((rows, cols), F32))(gathered)


def _ada_wgrad(c_all, dmod_sh):
    d = c_all.shape[1]
    n = dmod_sh.shape[1]
    tn = 512

    def body(c_ref, g_ref, o_ref):
        cc = c_ref[...]
        s = cc * jax.nn.sigmoid(cc)
        o_ref[...] = lax.dot_general(s, g_ref[...], (((0,), (0,)), ((), ())), preferred_element_type=F32,
                                     precision=lax.Precision.HIGHEST)

    return pl.pallas_call(
        body, name="ada_wgrad", grid=(n // tn,),
        in_specs=[pl.BlockSpec((N_DEV, d), lambda j: (0, 0)), pl.BlockSpec((N_DEV, tn), lambda j: (0, j))],
        out_specs=pl.BlockSpec((d, tn), lambda j: (0, j)),
        out_shape=SDS((d, n), F32), compiler_params=_params("parallel"))(c_all, dmod_sh)


def _rms(xf):
    return lax.rsqrt(jnp.mean(xf * xf, axis=-1, keepdims=True) + EPS)


def _in_proj(x, g, sc, sh, wt, b, tm, tn, ride=None):
    t, d = x.shape
    n = wt.shape[0]

    def body(x_ref, g_ref, sc_ref, sh_ref, w_ref, b_ref, z_ref, h_ref):
        @pl.when(pl.program_id(1) == 0)
        def _():
            xf = x_ref[...]
            h_ref[...] = ((xf * _rms(xf) * g_ref[...]) * (1.0 + sc_ref[...]) + sh_ref[...]).astype(BF16)

        acc = lax.dot_general(h_ref[...], w_ref[...], (((1,), (1,)), ((), ())), preferred_element_type=F32)
        z_ref[...] = (acc + b_ref[...]).astype(BF16)

    row = pl.BlockSpec((tm, d), lambda i, j: (i, 0))
    vec = pl.BlockSpec((1, d), lambda i, j: (0, 0))
    return _pallas(
        body, name="in_proj", grid=(t // tm, n // tn),
        in_specs=[row, vec, vec, vec, pl.BlockSpec((tn, d), lambda i, j: (j, 0)),
                  pl.BlockSpec((1, tn), lambda i, j: (0, j))],
        out_specs=[pl.BlockSpec((tm, tn), lambda i, j: (i, j)), row],
        out_shape=[SDS((t, n), BF16), SDS((t, d), BF16)], args=(x, g, sc, sh, wt, b),
        sem=("parallel", "arbitrary"), ride=ride)


def _segments(d, kvw2):
    o = d + kvw2
    names = ("cb", "cc", "cx", "ga", "gc")
    seg = {nm: slice(o + k * d, o + (k + 1) * d) for k, nm in enumerate(names)}
    seg["q"], seg["kv"] = slice(0, d), slice(d, o)
    return seg


def _attn_masks():
    rows = 4 * BLOCK
    r = lax.broadcasted_iota(jnp.int32, (rows, 2 * BLOCK), 0) & (BLOCK - 1)
    col = lax.broadcasted_iota(jnp.int32, (rows, 2 * BLOCK), 1)
    return (col > r) & (col <= r + BLOCK), col


def _kv_variants(kv, n_kv_w):
    assert n_kv_w == LANES
    kb, vb = kv[:, :LANES] * (HEAD_DIM ** -0.5), kv[:, LANES:]
    kr, vr = pltpu.roll(kb, HEAD_DIM, 1), pltpu.roll(vb, HEAD_DIM, 1)
    lane = lax.broadcasted_iota(jnp.int32, kb.shape, 1)
    lo = lane < HEAD_DIM
    zero = jnp.zeros_like(kb)
    k_eff = [[None, None], [None, None]]
    v_eff = [[None, None], [None, None]]
    for h in range(2):
        for e in range(2):
            ksrc, vsrc = (kb, vb) if e == h else (kr, vr)
            keep = lo if e == 0 else jnp.logical_not(lo)
            k_eff[h][e] = jnp.where(keep, ksrc, zero)
            v_eff[h][e] = jnp.where(keep, vsrc, zero)
    return k_eff, v_eff


def _sink_column(sinks_ref, h, e):
    rowblk = lax.broadcasted_iota(jnp.int32, (4 * BLOCK, 1), 0) // BLOCK
    col = jnp.zeros((4 * BLOCK, 1), F32)
    for j in range(4):
        col = jnp.where(rowblk == j, sinks_ref[0, GROUP * h + 2 * j + e], col)
    return col


def _softmax_sink(s, valid, sink):
    s = jnp.where(valid, s, -jnp.inf)
    m = jnp.maximum(jnp.max(s, axis=-1, keepdims=True), sink)
    p = jnp.exp(s - m)
    psink = jnp.exp(sink - m)
    den = jnp.sum(p, axis=-1, keepdims=True) + psink
    inv = 1.0 / den
    return p * inv, psink * inv


def _shift_down(a, s, prev):
    rows = a.shape[0]
    out = pltpu.roll(a, s, 0)
    row = lax.broadcasted_iota(jnp.int32, a.shape, 0)
    for t in range(s):
        out = jnp.where(row == t, prev[SUBLANES_BF16 - s + t:SUBLANES_BF16 - s + t + 1, :], out)
    del rows
    return out


def _shift_up(a, s, nxt):
    rows = a.shape[0]
    out = pltpu.roll(a, rows - s, 0)
    row = lax.broadcasted_iota(jnp.int32, a.shape, 0)
    for t in range(s):
        out = jnp.where(row == rows - s + t, nxt[t:t + 1, :], out)
    return out


def _stack_pairs(ref, h, rows=slice(None)):
    return jnp.concatenate([ref[rows, (4 * h + j) * LANES:(4 * h + j + 1) * LANES] for j in range(4)], axis=0)


FWD_BLOCKS = 4


def _mixer_fwd(z, sinks, conv_w, d, ride=None):
    t, zw = z.shape
    kvw2 = zw - 6 * d
    tq = FWD_BLOCKS * BLOCK
    halo = tq // SUBLANES_BF16
    seg = _segments(d, kvw2)

    def body(z_ref, kvp_ref, prev_ref, sinks_ref, cw_ref, attn_ref, merged_ref):
        n = pl.program_id(0)
        band, col = _attn_masks()
        for b in range(FWD_BLOCKS):
            rows = slice(b * BLOCK, (b + 1) * BLOCK)
            before = slice((b - 1) * BLOCK, b * BLOCK)
            kv_prev = kvp_ref[...] if b == 0 else z_ref[before, seg["kv"]]
            kv = jnp.concatenate([kv_prev, z_ref[rows, seg["kv"]]], axis=0)
            k_eff, v_eff = _kv_variants(kv, kvw2 // 2)
            valid = band & ((n > 0) | (col >= BLOCK)) if b == 0 else band
            for h in range(2):
                q4 = _stack_pairs(z_ref, h, rows)
                o4 = jnp.zeros((4 * BLOCK, LANES), F32)
                for e in range(2):
                    s = lax.dot_general(q4, k_eff[h][e], (((1,), (1,)), ((), ())), preferred_element_type=F32)
                    p, _ = _softmax_sink(s, valid, _sink_column(sinks_ref, h, e))
                    o4 = o4 + jnp.dot(p.astype(BF16), v_eff[h][e], preferred_element_type=F32)
                for j in range(4):
                    attn_ref[rows, (4 * h + j) * LANES:(4 * h + j + 1) * LANES] = (
                        o4[j * BLOCK:(j + 1) * BLOCK].astype(BF16))
            cb = z_ref[rows, seg["cb"]].astype(F32)
            p_in = z_ref[rows, seg["cc"]].astype(F32) * z_ref[rows, seg["cx"]].astype(F32)
            if b == 0:
                prev = jnp.where(n > 0, prev_ref[:, seg["cc"]].astype(F32) * prev_ref[:, seg["cx"]].astype(F32), 0.0)
            else:
                tail = slice(b * BLOCK - SUBLANES_BF16, b * BLOCK)
                prev = z_ref[tail, seg["cc"]].astype(F32) * z_ref[tail, seg["cx"]].astype(F32)
            cconv = (cw_ref[0:1, :] * _shift_down(p_in, 2, prev) + cw_ref[1:2, :] * _shift_down(p_in, 1, prev)
                     + cw_ref[2:3, :] * p_in)
            sa = jax.nn.sigmoid(z_ref[rows, seg["ga"]].astype(F32))
            sg = jax.nn.sigmoid(z_ref[rows, seg["gc"]].astype(F32))
            merged_ref[rows, :] = (sa * attn_ref[rows, :].astype(F32) + sg * (cb * cconv)).astype(BF16)

    blk = pl.BlockSpec((tq, d), lambda n: (n, 0))
    return _pallas(
        body, name="mixer_fwd", grid=(t // tq,),
        in_specs=[pl.BlockSpec((tq, zw), lambda n: (n, 0)),
                  pl.BlockSpec((BLOCK, kvw2), lambda n: (jnp.maximum(n * FWD_BLOCKS - 1, 0), d // kvw2)),
                  pl.BlockSpec((SUBLANES_BF16, zw), lambda n: (jnp.maximum(n * halo - 1, 0), 0)),
                  SMEM_SPEC, pl.BlockSpec((3, d), lambda n: (0, 0))],
        out_specs=[blk, blk],
        out_shape=[SDS((t, d), BF16), SDS((t, d), BF16)],
        args=(z, z, z, sinks, conv_w), sem=("parallel",), ride=ride)


def _out_proj_fwd(merged, w_out, x, ga1, g_ffn, sc2, sh2, tm):
    t, d = x.shape

    def body(m_ref, w_ref, x_ref, ga_ref, g_ref, sc_ref, sh_ref, y_ref, x1_ref, h_ref):
        y = jnp.dot(m_ref[...], w_ref[...], preferred_element_type=F32)
        x1 = x_ref[...] + ga_ref[...] * y
        y_ref[...] = y.astype(BF16)
        x1_ref[...] = x1
        h_ref[...] = ((x1 * _rms(x1) * g_ref[...]) * (1.0 + sc_ref[...]) + sh_ref[...]).astype(BF16)

    row = pl.BlockSpec((tm, d), lambda i: (i, 0))
    vec = pl.BlockSpec((1, d), lambda i: (0, 0))
    return pl.pallas_call(
        body, name="out_proj_fwd", grid=(t // tm,),
        in_specs=[row, pl.BlockSpec((d, d), lambda i: (0, 0)), row, vec, vec, vec, vec],
        out_specs=[row, row, row],
        out_shape=[SDS((t, d), BF16), SDS((t, d), F32), SDS((t, d), BF16)],
        compiler_params=_params("parallel"))(merged, w_out, x, ga1, g_ffn, sc2, sh2)


def _ffn_in_fwd(h2, w, ff, tm, tn):
    t, d = h2.shape
    nj = ff // tn
    assert w.shape == (2 * nj, d, tn)

    def body(h_ref, wg_ref, wu_ref, gu_ref, act_ref):
        hh = h_ref[...]
        g = jnp.dot(hh, wg_ref[...], preferred_element_type=F32)
        u = jnp.dot(hh, wu_ref[...], preferred_element_type=F32)
        gu_ref[0] = g.astype(BF16)
        gu_ref[1] = u.astype(BF16)
        act_ref[...] = ((g * jax.nn.sigmoid(g)) * u).astype(BF16)

    return pl.pallas_call(
        body, name="ffn_in_fwd", grid=(nj, t // tm),
        in_specs=[pl.BlockSpec((tm, d), lambda j, i: (i, 0)), pl.BlockSpec((None, d, tn), lambda j, i: (j, 0, 0)),
                  pl.BlockSpec((None, d, tn), lambda j, i: (j + nj, 0, 0))],
        out_specs=[pl.BlockSpec((2, tm, tn), lambda j, i: (0, i, j)), pl.BlockSpec((tm, tn), lambda j, i: (i, j))],
        out_shape=[SDS((2, t, ff), BF16), SDS((t, ff), BF16)],
        compiler_params=_params("parallel", "parallel"))(h2, w, w)


def _ffn_out_loss(act, w, x1, target, ga2, g_final, tm):
    t, d = x1.shape
    ff = act.shape[1]

    def body(a_ref, w_ref, x1_ref, tg_ref, ga_ref, gf_ref, dx2_ref, dy2_ref, st_ref):
        @pl.when(pl.program_id(0) == 0)
        def _():
            st_ref[...] = jnp.zeros_like(st_ref)

        halves = [slice(k * (tm // 2), (k + 1) * (tm // 2)) for k in range(2)]
        y2s = [jnp.dot(a_ref[rows, :], w_ref[...], preferred_element_type=F32) for rows in halves]
        for rows, y2 in zip(halves, y2s):
            x2 = x1_ref[rows, :] + ga_ref[...] * y2
            r = _rms(x2)
            yn = x2 * r
            err = yn * gf_ref[...] - tg_ref[rows, :]
            loss = 0.5 * jnp.sum(jnp.mean(err * err, axis=-1, keepdims=True), axis=0, keepdims=True)
            dy = err * (1.0 / d)
            u = dy * gf_ref[...]
            dx2 = r * (u - yn * jnp.mean(u * yn, axis=-1, keepdims=True))
            dx2_ref[rows, :] = dx2
            dy2_ref[rows, :] = (ga_ref[...] * dx2).astype(BF16)
            st_ref[0:1, :] += jnp.sum(dx2 * y2, axis=0, keepdims=True)
            st_ref[1:2, :] += jnp.sum(dy * yn, axis=0, keepdims=True)
            st_ref[2:3, :] += jnp.broadcast_to(loss, (1, d))

    row = pl.BlockSpec((tm, d), lambda i: (i, 0))
    vec = pl.BlockSpec((1, d), lambda i: (0, 0))
    return pl.pallas_call(
        body, name="ffn_out_loss", grid=(t // tm,),
        in_specs=[pl.BlockSpec((tm, ff), lambda i: (i, 0)),
                  pl.BlockSpec((ff, d), lambda i: (0, 0), pipeline_mode=pl.Buffered(1)), row, row, vec, vec],
        out_specs=[row, row, pl.BlockSpec((8, d), lambda i: (0, 0))],
        out_shape=[SDS((t, d), F32), SDS((t, d), BF16), SDS((8, d), F32)],
        compiler_params=_params("arbitrary"))(act, w, x1, target, ga2, g_final)


def _ffn_out_bwd(dy2, w, gu, tm, tn):
    t, d = dy2.shape
    ff = w.shape[0]

    def body(dy_ref, w_ref, gu_ref, o_ref):
        dy = dy_ref[...]
        for lo in range(0, tn, 3 * LANES):
            cols = slice(lo, min(lo + 3 * LANES, tn))
            dact = lax.dot_general(dy, w_ref[cols, :], (((1,), (1,)), ((), ())), preferred_element_type=F32)
            g = gu_ref[0, :, cols].astype(F32)
            u = gu_ref[1, :, cols].astype(F32)
            sg = jax.nn.sigmoid(g)
            a = dact * sg
            du = a * g
            o_ref[0, :, cols] = (u * (a + du * (1.0 - sg))).astype(BF16)
            o_ref[1, :, cols] = du.astype(BF16)

    gu_spec = pl.BlockSpec((2, tm, tn), lambda j, i: (0, i, j))
    return pl.pallas_call(
        body, name="ffn_out_bwd", grid=(ff // tn, t // tm),
        in_specs=[pl.BlockSpec((tm, d), lambda j, i: (i, 0)), pl.BlockSpec((tn, d), lambda j, i: (j, 0)), gu_spec],
        out_specs=gu_spec, out_shape=SDS((2, t, ff), BF16),
        compiler_params=_params("parallel", "parallel"))(dy2, w, gu)


def _wgrad(a, b, a_spec, b_spec, out_spec, out_shape, grid, name, ride=None):
    def body(a_ref, b_ref, o_ref, o16_ref):
        k = pl.program_id(len(grid) - 1)

        @pl.when(k == 0)
        def _():
            o_ref[...] = jnp.zeros_like(o_ref)

        o_ref[...] += lax.dot_general(a_ref[...], b_ref[...], (((0,), (0,)), ((), ())), preferred_element_type=F32)

        @pl.when(k == grid[-1] - 1)
        def _():
            o16_ref[...] = o_ref[...].astype(BF16)

    return _pallas(
        body, name=name, grid=grid, in_specs=[a_spec, b_spec], out_specs=[out_spec, out_spec],
        out_shape=[out_shape, SDS(out_shape.shape, BF16)], args=(a, b),
        sem=["parallel"] * (len(grid) - 1) + ["arbitrary"], ride=ride)


def _ffn_in_bwd(dgu, w, x1, dx2, y1, g_ffn, sc2, ga1, tm):
    t, d = x1.shape
    ff = dgu.shape[2]
    n_sh, _, sw = w.shape
    per = ff // sw
    nt = (((1,), (1,)), ((), ()))

    def body(a_ref, w_ref, x1_ref, dx2_ref, y1_ref, g_ref, sc_ref, ga_ref, dx1_ref, dy1_ref, st_ref):
        @pl.when(pl.program_id(0) == 0)
        def _():
            st_ref[...] = jnp.zeros_like(st_ref)

        dh = None
        for j in range(n_sh):
            part = lax.dot_general(a_ref[j // per, :, (j % per) * sw:(j % per + 1) * sw], w_ref[j], nt,
                                   preferred_element_type=F32)
            dh = part if dh is None else dh + part
        x1 = x1_ref[...]
        r = _rms(x1)
        xn = x1 * r
        g = g_ref[...]
        dn = dh * (1.0 + sc_ref[...])
        u = dn * g
        dx1 = dx2_ref[...] + r * (u - xn * jnp.mean(u * xn, axis=-1, keepdims=True))
        dx1_ref[...] = dx1
        dy1_ref[...] = (ga_ref[...] * dx1).astype(BF16)
        st_ref[0:1, :] += jnp.sum(dh, axis=0, keepdims=True)
        st_ref[1:2, :] += jnp.sum(dh * (xn * g), axis=0, keepdims=True)
        st_ref[2:3, :] += jnp.sum(dn * xn, axis=0, keepdims=True)
        st_ref[3:4, :] += jnp.sum(dx1 * y1_ref[...].astype(F32), axis=0, keepdims=True)

    row = pl.BlockSpec((tm, d), lambda i: (i, 0))
    vec = pl.BlockSpec((1, d), lambda i: (0, 0))
    return pl.pallas_call(
        body, name="ffn_in_bwd", grid=(t // tm,),
        in_specs=[pl.BlockSpec((2, tm, ff), lambda i: (0, i, 0)),
                  pl.BlockSpec((n_sh, d, sw), lambda i: (0, 0, 0), pipeline_mode=pl.Buffered(1)),
                  row, row, row, vec, vec, vec],
        out_specs=[row, row, pl.BlockSpec((8, d), lambda i: (0, 0))],
        out_shape=[SDS((t, d), F32), SDS((t, d), BF16), SDS((8, d), F32)],
        compiler_params=_params("arbitrary"))(dgu, w, x1, dx2, y1, g_ffn, sc2, ga1)


def _out_proj_bwd(dy1, w_out, tm, ride=None):
    t, d = dy1.shape

    def body(dy_ref, w_ref, o_ref):
        o_ref[...] = lax.dot_general(dy_ref[...], w_ref[...], (((1,), (1,)), ((), ())),
                                     preferred_element_type=F32).astype(BF16)

    row = pl.BlockSpec((tm, d), lambda i: (i, 0))
    return _pallas(body, name="out_proj_bwd", grid=(t // tm,),
                   in_specs=[row, pl.BlockSpec((d, d), lambda i: (0, 0))], out_specs=row,
                   out_shape=SDS((t, d), BF16), args=(dy1, w_out), sem=("parallel",), ride=ride)


BWD_BLOCKS = 2


def _mixer_bwd(z, dmerged, attn, sinks, conv_w, d, ride=None):
    t, zw = z.shape
    kvw2 = zw - 6 * d
    tq = BWD_BLOCKS * BLOCK
    steps = t // tq
    halo = tq // SUBLANES_BF16
    last_halo = t // SUBLANES_BF16 - 1
    scale = HEAD_DIM ** -0.5
    seg = _segments(d, kvw2)

    def body(z_ref, kvp_ref, prev_ref, next_ref, dm_ref, dmn_ref, attn_ref, sinks_ref, cw_ref,
             dz_ref, dkv_ref, db_ref, dbkv_ref, dcw_ref, dsk_ref, carry_ref):
        n = pl.program_id(0)

        @pl.when(n == 0)
        def _():
            carry_ref[...] = jnp.zeros_like(carry_ref)
            db_ref[...] = jnp.zeros_like(db_ref)
            dbkv_ref[...] = jnp.zeros_like(dbkv_ref)
            dcw_ref[...] = jnp.zeros_like(dcw_ref)
            dsk_ref[...] = jnp.zeros_like(dsk_ref)

        def one_block(b, pending):
            rows = slice(b * BLOCK, (b + 1) * BLOCK)
            before = slice((b - 1) * BLOCK, b * BLOCK)
            dm = dm_ref[rows, :].astype(F32)
            sa = jax.nn.sigmoid(z_ref[rows, seg["ga"]].astype(F32))
            dga = dm * attn_ref[rows, :].astype(F32) * sa * (1.0 - sa)
            dz_ref[rows, seg["ga"]] = dga.astype(BF16)
            db_ref[0:1, seg["ga"]] += jnp.sum(dga, axis=0, keepdims=True)
            dattn = (dm * sa).astype(BF16)

            kv_prev = kvp_ref[...] if b == 0 else z_ref[before, seg["kv"]]
            kv = jnp.concatenate([kv_prev, z_ref[rows, seg["kv"]]], axis=0)
            k_eff, v_eff = _kv_variants(kv, kvw2 // 2)
            band, col = _attn_masks()
            valid = band & ((n > 0) | (col >= BLOCK)) if b == 0 else band
            lane_lo = lax.broadcasted_iota(jnp.int32, (2 * BLOCK, LANES), 1) < HEAD_DIM
            sink_lane = lax.broadcasted_iota(jnp.int32, (1, LANES), 1)
            rowblk = lax.broadcasted_iota(jnp.int32, (4 * BLOCK, 1), 0) // BLOCK
            dk_acc = [jnp.zeros((2 * BLOCK, LANES), F32), jnp.zeros((2 * BLOCK, LANES), F32)]
            dv_acc = [jnp.zeros((2 * BLOCK, LANES), F32), jnp.zeros((2 * BLOCK, LANES), F32)]
            dsink = jnp.zeros((1, LANES), F32)
            for h in range(2):
                q4 = _stack_pairs(z_ref, h, rows)
                do4 = jnp.concatenate([dattn[:, (4 * h + j) * LANES:(4 * h + j + 1) * LANES] for j in range(4)],
                                      axis=0)
                dq4 = jnp.zeros((4 * BLOCK, LANES), F32)
                for e in range(2):
                    s = lax.dot_general(q4, k_eff[h][e], (((1,), (1,)), ((), ())), preferred_element_type=F32)
                    p, psink = _softmax_sink(s, valid, _sink_column(sinks_ref, h, e))
                    dp = lax.dot_general(do4, v_eff[h][e], (((1,), (1,)), ((), ())), preferred_element_type=F32)
                    delta = jnp.sum(p * dp, axis=-1, keepdims=True)
                    ds = (p * (dp - delta)).astype(BF16)
                    dq4 = dq4 + jnp.dot(ds, k_eff[h][e], preferred_element_type=F32)
                    dk = lax.dot_general(q4, ds, (((0,), (0,)), ((), ())), preferred_element_type=F32).T
                    dv = lax.dot_general(do4, p.astype(BF16), (((0,), (0,)), ((), ())), preferred_element_type=F32).T
                    keep = lane_lo if e == 0 else jnp.logical_not(lane_lo)
                    slot = 0 if e == h else 1
                    dk_acc[slot] = dk_acc[slot] + jnp.where(keep, dk, 0.0)
                    dv_acc[slot] = dv_acc[slot] + jnp.where(keep, dv, 0.0)
                    dsk = -(psink * delta)
                    for j in range(4):
                        tot = jnp.sum(jnp.where(rowblk == j, dsk, 0.0), axis=0, keepdims=True)
                        dsink = dsink + jnp.where(sink_lane == GROUP * h + 2 * j + e, tot, 0.0)
                for j in range(4):
                    cols = slice((4 * h + j) * LANES, (4 * h + j + 1) * LANES)
                    dqj = dq4[j * BLOCK:(j + 1) * BLOCK]
                    dz_ref[rows, cols] = dqj.astype(BF16)
                    db_ref[0:1, cols] += jnp.sum(dqj, axis=0, keepdims=True)
            dsk_ref[0:1, :] += dsink
            dkv_new = jnp.concatenate([(dk_acc[0] + pltpu.roll(dk_acc[1], HEAD_DIM, 1)) * scale,
                                       dv_acc[0] + pltpu.roll(dv_acc[1], HEAD_DIM, 1)], axis=1)
            done = pending + dkv_new[:BLOCK]
            dkv_ref[rows, :] = done.astype(BF16)
            dbkv_ref[0:1, :] += jnp.sum(done, axis=0, keepdims=True)

            cb = z_ref[rows, seg["cb"]].astype(F32)
            cc = z_ref[rows, seg["cc"]].astype(F32)
            cx = z_ref[rows, seg["cx"]].astype(F32)
            sg = jax.nn.sigmoid(z_ref[rows, seg["gc"]].astype(F32))
            p_in = cc * cx
            if b == 0:
                prev = jnp.where(n > 0, prev_ref[:, seg["cc"]].astype(F32) * prev_ref[:, seg["cx"]].astype(F32), 0.0)
            else:
                tail = slice(b * BLOCK - SUBLANES_BF16, b * BLOCK)
                prev = z_ref[tail, seg["cc"]].astype(F32) * z_ref[tail, seg["cx"]].astype(F32)
            p_m1 = _shift_down(p_in, 1, prev)
            p_m2 = _shift_down(p_in, 2, prev)
            w0, w1, w2 = cw_ref[0:1, :], cw_ref[1:2, :], cw_ref[2:3, :]
            cconv = w0 * p_m2 + w1 * p_m1 + w2 * p_in
            dconv = dm * sg
            dgc = dm * (cb * cconv) * sg * (1.0 - sg)
            dcb = dconv * cconv
            dcc_t = dconv * cb
            if b == BWD_BLOCKS - 1:
                nxt = jnp.where(n < steps - 1,
                                dmn_ref[...].astype(F32) * jax.nn.sigmoid(next_ref[:, seg["gc"]].astype(F32))
                                * next_ref[:, seg["cb"]].astype(F32), 0.0)
            else:
                head = slice((b + 1) * BLOCK, (b + 1) * BLOCK + SUBLANES_BF16)
                nxt = (dm_ref[head, :].astype(F32) * jax.nn.sigmoid(z_ref[head, seg["gc"]].astype(F32))
                       * z_ref[head, seg["cb"]].astype(F32))
            dpin = w2 * dcc_t + w1 * _shift_up(dcc_t, 1, nxt) + w0 * _shift_up(dcc_t, 2, nxt)
            for nm, val in (("cb", dcb), ("cc", dpin * cx), ("cx", dpin * cc), ("gc", dgc)):
                dz_ref[rows, seg[nm]] = val.astype(BF16)
                db_ref[0:1, seg[nm]] += jnp.sum(val, axis=0, keepdims=True)
            dcw_ref[0:1, :] += jnp.sum(dcc_t * p_m2, axis=0, keepdims=True)
            dcw_ref[1:2, :] += jnp.sum(dcc_t * p_m1, axis=0, keepdims=True)
            dcw_ref[2:3, :] += jnp.sum(dcc_t * p_in, axis=0, keepdims=True)
            return dkv_new[BLOCK:]

        @pl.when(n < steps)
        def _():
            pending = carry_ref[...]
            for b in range(BWD_BLOCKS):
                pending = one_block(b, pending)
            carry_ref[...] = pending

        @pl.when(n == steps)
        def _():
            done = carry_ref[...]
            dkv_ref[:BLOCK, :] = done.astype(BF16)
            dkv_ref[BLOCK:, :] = jnp.zeros((tq - BLOCK, kvw2), BF16)
            dbkv_ref[0:1, :] += jnp.sum(done, axis=0, keepdims=True)

    def cur(n):
        return jnp.minimum(n, steps - 1)

    def after(n):
        return jnp.minimum((cur(n) + 1) * halo, last_halo)

    blk = pl.BlockSpec((tq, d), lambda n: (cur(n), 0))
    return _pallas(
        body, name="mixer_bwd", grid=(steps + 1,), ride=ride, sem=("arbitrary",),
        args=(z, z, z, z, dmerged, dmerged, attn, sinks, conv_w),
        in_specs=[pl.BlockSpec((tq, zw), lambda n: (cur(n), 0)),
                  pl.BlockSpec((BLOCK, kvw2), lambda n: (jnp.maximum(cur(n) * BWD_BLOCKS - 1, 0), d // kvw2)),
                  pl.BlockSpec((SUBLANES_BF16, zw), lambda n: (jnp.maximum(cur(n) * halo - 1, 0), 0)),
                  pl.BlockSpec((SUBLANES_BF16, zw), lambda n: (after(n), 0)),
                  blk,
                  pl.BlockSpec((SUBLANES_BF16, d), lambda n: (after(n), 0)),
                  blk, SMEM_SPEC, pl.BlockSpec((3, d), lambda n: (0, 0))],
        out_specs=[pl.BlockSpec((tq, zw), lambda n: (cur(n), 0)),
                   pl.BlockSpec((tq, kvw2), lambda n: (n, 0)),
                   pl.BlockSpec((8, zw), lambda n: (0, 0)), pl.BlockSpec((8, kvw2), lambda n: (0, 0)),
                   pl.BlockSpec((8, d), lambda n: (0, 0)), pl.BlockSpec((8, LANES), lambda n: (0, 0))],
        out_shape=[SDS((t, zw), BF16), SDS((t + tq, kvw2), BF16), SDS((8, zw), F32), SDS((8, kvw2), F32),
                   SDS((8, d), F32), SDS((8, LANES), F32)],
        scratch=[pltpu.VMEM((BLOCK, kvw2), F32)])


def _wgrad_in(dz, dkv, h1, tk, ride=None):
    t, zw = dz.shape
    d = h1.shape[1]
    kvw2 = dkv.shape[1]
    blk = d + kvw2
    assert zw % blk == 0
    tn = (((0,), (0,)), ((), ()))

    def body(a_ref, akv_ref, h_ref, o_ref, o16_ref):
        n, k = pl.program_id(0), pl.program_id(1)

        @pl.when(k == 0)
        def _():
            o_ref[...] = jnp.zeros_like(o_ref)

        @pl.when(n == 0)
        def _():
            o_ref[:d, :] += lax.dot_general(a_ref[:, :d], h_ref[...], tn, preferred_element_type=F32)
            o_ref[d:, :] += lax.dot_general(akv_ref[...], h_ref[...], tn, preferred_element_type=F32)

        @pl.when(n > 0)
        def _():
            o_ref[...] += lax.dot_general(a_ref[...], h_ref[...], tn, preferred_element_type=F32)

        @pl.when(k == t // tk - 1)
        def _():
            o16_ref[...] = o_ref[...].astype(BF16)

    out_spec = pl.BlockSpec((blk, d), lambda n, k: (n, 0))
    return _pallas(
        body, name="wgrad_in", grid=(zw // blk, t // tk),
        in_specs=[pl.BlockSpec((tk, blk), lambda n, k: (k, n)), pl.BlockSpec((tk, kvw2), lambda n, k: (k, 0)),
                  pl.BlockSpec((tk, d), lambda n, k: (k, 0))],
        out_specs=[out_spec, out_spec], out_shape=[SDS((zw, d), F32), SDS((zw, d), BF16)],
        args=(dz, dkv, h1), sem=("parallel", "arbitrary"), ride=ride)


def _in_proj_bwd(dz, dkv, wt, x, dx1, g_mix, sc1, tm, ride=None):
    t, d = x.shape
    zw = dz.shape[1]
    kvw2 = dkv.shape[1]
    rest = d + kvw2

    def body(a_ref, akv_ref, w_ref, x_ref, dx1_ref, g_ref, sc_ref, gx_ref, st_ref):
        @pl.when(pl.program_id(0) == 0)
        def _():
            st_ref[...] = jnp.zeros_like(st_ref)

        dh = (jnp.dot(a_ref[:, :d], w_ref[:d, :], preferred_element_type=F32)
              + jnp.dot(akv_ref[...], w_ref[d:rest, :], preferred_element_type=F32)
              + jnp.dot(a_ref[:, rest:], w_ref[rest:, :], preferred_element_type=F32))
        xx = x_ref[...]
        r = _rms(xx)
        xn = xx * r
        g = g_ref[...]
        dn = dh * (1.0 + sc_ref[...])
        u = dn * g
        gx_ref[...] = dx1_ref[...] + r * (u - xn * jnp.mean(u * xn, axis=-1, keepdims=True))
        st_ref[0:1, :] += jnp.sum(dh, axis=0, keepdims=True)
        st_ref[1:2, :] += jnp.sum(dh * (xn * g), axis=0, keepdims=True)
        st_ref[2:3, :] += jnp.sum(dn * xn, axis=0, keepdims=True)

    row = pl.BlockSpec((tm, d), lambda i: (i, 0))
    vec = pl.BlockSpec((1, d), lambda i: (0, 0))
    return _pallas(
        body, name="in_proj_bwd", grid=(t // tm,),
        in_specs=[pl.BlockSpec((tm, zw), lambda i: (i, 0)), pl.BlockSpec((tm, kvw2), lambda i: (i, 0)),
                  pl.BlockSpec((zw, d), lambda i: (0, 0), pipeline_mode=pl.Buffered(1)),
                  row, row, vec, vec],
        out_specs=[row, pl.BlockSpec((8, d), lambda i: (0, 0))],
        out_shape=[SDS((t, d), F32), SDS((8, d), F32)],
        args=(dz, dkv, wt, x, dx1, g_mix, sc1), sem=("arbitrary",), ride=ride)


def _to_lanes(v, rows=None):
    flat = v.reshape(-1)
    need = -(-flat.shape[0] // LANES)
    need = -(-need // 8) * 8 if rows is None else rows
    return jnp.pad(flat, (0, need * LANES - flat.shape[0])).reshape(need, LANES)


def kernel(x, c, w_ada, b_ada, g_mix, w_in, b_in, sinks, conv_w, w_out, g_ffn, w_ffn_in, w_ffn_out, g_final, loss_target, m_w_ada, m_b_ada, m_g_mix, m_w_in, m_b_in, m_sinks, m_conv_w, m_w_out, m_g_ffn, m_w_ffn_in, m_w_ffn_out, m_g_final, v_w_ada, v_b_ada, v_g_mix, v_w_in, v_b_in, v_sinks, v_conv_w, v_w_out, v_g_ffn, v_w_ffn_in, v_w_ffn_out, v_g_final):
    xs, tgt = x[0], loss_target[0]
    t, d = xs.shape
    zw = w_in.shape[2] * N_CHIP
    kvw2 = zw - 6 * d
    ff = w_ffn_out.shape[1] * N_CHIP
    n_mod = w_ada.shape[2] * N_CHIP // d
    mod_sh = w_ada.shape[2]
    cw_sh = conv_w.shape[2]
    assert d % (8 * LANES) == 0 and kvw2 == 2 * LANES and t % 512 == 0 and n_mod == 6
    xi, yi, ci = _mesh_pos()
    j_me = 2 * xi + yi
    b_me = 4 * xi + 2 * yi + ci
    pos = jnp.stack([ci, j_me]).astype(jnp.int32)
    tm = 512

    w_in_t, m_w_in_t, v_w_in_t = w_in[0].T, m_w_in[0].T, v_w_in[0].T
    assert d == 8 * LANES
    pack1 = jnp.concatenate([c.reshape(d // LANES, LANES), conv_w[0].reshape(-1, LANES)], axis=0)
    pack1 = jnp.pad(pack1, ((0, 16 - pack1.shape[0]), (0, 0)))
    b_ada_sh = lax.dynamic_slice(b_ada, (0, j_me * mod_sh), (1, mod_sh))
    g1, mod_all, w_in_g = _startup(pack1, w_ada[0], b_ada_sh, _cast_into_block(pos, w_in_t, "cast_w_in"))
    c_all = g1[:, :d // LANES, :].reshape(N_DEV, d)
    cw_rows = 3 * cw_sh // LANES
    conv_w_full = jnp.concatenate(
        [g1[2 * j, d // LANES:d // LANES + cw_rows, :].reshape(3, cw_sh) for j in range(N_CHIP)], axis=1)
    mod = jnp.concatenate([lax.dynamic_index_in_dim(mod_all[2 * j], b_me, 0, keepdims=True) for j in range(N_CHIP)],
                          axis=1)
    sh1, sc1, ga1, sh2, sc2, ga2 = [mod[:, k * d:(k + 1) * d] for k in range(6)]
    w_in_tf = w_in_g.reshape(zw, d)
    later = [_cast_into_block(pos, w_out[0], "cast_w_out"), _cast_into_block(pos, w_ffn_in[0], "cast_w_ffn_in"),
             _cast_into_block(pos, w_ffn_out[0], "cast_w_ffn_out")]

    (z, h1), later = _in_proj(xs, g_mix, sc1, sh1, w_in_tf, b_in, min(t, 1024), zw // 5, ride=_x_gather_ici(later))
    (attn, merged), later = _mixer_fwd(z, sinks, conv_w_full, d, ride=_x_gather_d2d(later))
    w_out_f = later[0].reshape(d, d)
    w_ffn_in_f = later[1]
    w_ffn_out_f = later[2].reshape(ff, d)
    tml = min(t, 1024)
    y1, x1, h2 = _out_proj_fwd(merged, w_out_f, xs, ga1, g_ffn, sc2, sh2, tml)
    gu, act = _ffn_in_fwd(h2, w_ffn_in_f, ff, tml, ff // 2)
    dx2, dy2, st_loss = _ffn_out_loss(act, w_ffn_out_f, x1, tgt, ga2, g_final.reshape(1, d), tml)

    dgu = _ffn_out_bwd(dy2, w_ffn_out_f, gu, tml, ff // 2)
    tk = min(t, 2048)
    dw_ffn_out, _ = _wgrad(
        act, dy2, pl.BlockSpec((tk, ff // 2), lambda m, k: (k, m)), pl.BlockSpec((tk, d), lambda m, k: (k, 0)),
        pl.BlockSpec((ff // 2, d), lambda m, k: (m, 0)), SDS((ff, d), F32), (2, t // tk), "wgrad_ffn_out")
    dx1, dy1, st_ffn = _ffn_in_bwd(dgu, w_ffn_in_f, x1, dx2, y1, g_ffn, sc2, ga1, tm)
    dw_ffn_in, _ = _wgrad(
        h2, dgu, pl.BlockSpec((tk, d), lambda n, k: (k, 0)),
        pl.BlockSpec((None, tk, ff // 2), lambda n, k: (n // 2, k, n % 2)),
        pl.BlockSpec((None, d, ff // 2), lambda n, k: (n, 0, 0)), SDS((N_CHIP, d, ff // 2), F32),
        (N_CHIP, t // tk), "wgrad_ffn_in")
    dw_out, _ = _wgrad(
        merged, dy1, pl.BlockSpec((tk, d), lambda m, k: (k, 0)), pl.BlockSpec((tk, d), lambda m, k: (k, 0)),
        pl.BlockSpec((d, d), lambda m, k: (0, 0)), SDS((d, d), F32), (1, t // tk), "wgrad_out")

    early = [[g.reshape(N_CHIP, -1, g.shape[-1]) for g in pair] for pair in (dw_out, dw_ffn_in, dw_ffn_out)]
    early_names = ["w_out", "w_ffn_in", "w_ffn_out"]
    dmerged, _ = _out_proj_bwd(dy1, w_out_f, tml)
    (dz, dkv_shifted, db_z, db_kv, dcw, dsk), terms = _mixer_bwd(
        z, dmerged, attn, sinks, conv_w_full, d, ride=_x_reduce([e[0] for e in early], [e[1] for e in early]))
    dkv = dkv_shifted[BLOCK:BLOCK + t]
    fulls = [_sum_terms(pos, e[0], s, r, "sum_terms_" + nm)
             for e, s, r, nm in zip(early, terms[:3], terms[3:], early_names)]
    dw_in_t, (g_w_out, g_w_ffn_in, g_w_ffn_out) = _wgrad_in(dz, dkv, h1, tk, ride=_x_pair_exchange(fulls))
    dw_in_t = [g.reshape(N_CHIP, zw // N_CHIP, d) for g in dw_in_t]

    (grad_x, st_in), (from_sib, from_far) = _in_proj_bwd(dz, dkv, w_in_tf, xs, dx1, g_mix, sc1, tm,
                                                         ride=_x_reduce([dw_in_t[0]], [dw_in_t[1]]))
    (g_w_in_t,) = _exchange(_x_pair_exchange([_sum_terms(pos, dw_in_t[0], from_sib, from_far, "sum_terms_w_in")]),
                            "pair_exchange")

    dmod = jnp.concatenate([st_in[0:1], st_in[1:2], st_ffn[3:4], st_ffn[0:1], st_ffn[1:2], st_loss[0:1]], axis=1)
    db_in = jnp.concatenate([db_z[0:1, :d], db_kv[0:1], db_z[0:1, d + kvw2:]], axis=1)
    seg = [dmod, st_in[2:3], db_in, dsk[0:1], dcw[0:3].reshape(1, 3 * d), st_ffn[2:3], st_loss[1:2],
           st_loss[2:3, :LANES]]
    sizes = [s.shape[1] for s in seg]
    pack2 = _to_lanes(jnp.concatenate(seg, axis=1))
    packs = _all_gather_small(pack2, "gather_small_grads")
    tot = _pack_sum(packs).reshape(-1)
    offs = [sum(sizes[:k]) for k in range(len(sizes))]
    gb_ada, gg_mix, gb_in, gsinks, gcw, gg_ffn, gg_final, loss_v = [tot[o:o + s] for o, s in zip(offs, sizes)]
    loss = loss_v[0]
    gsinks = gsinks[:sinks.shape[1]]
    gcw_sh = lax.dynamic_slice(gcw.reshape(3, d), (0, j_me * cw_sh), (3, cw_sh))

    dmod_all = packs[:, :n_mod * d // LANES, :].reshape(N_DEV, n_mod * d)
    g_w_ada = _ada_wgrad(c_all, lax.dynamic_slice(dmod_all, (0, j_me * mod_sh), (N_DEV, mod_sh)))

    out_g, out_d, out_m, out_v = {}, {}, {}, {}
    big = {"w_ada": (w_ada[0], g_w_ada, m_w_ada[0], v_w_ada[0]),
           "w_out": (w_out[0], g_w_out, m_w_out[0], v_w_out[0]),
           "w_ffn_in": (w_ffn_in[0], g_w_ffn_in, m_w_ffn_in[0], v_w_ffn_in[0]),
           "w_ffn_out": (w_ffn_out[0], g_w_ffn_out, m_w_ffn_out[0], v_w_ffn_out[0])}
    for nm, (w, g, m, v) in big.items():
        out_g[nm], out_d[nm], out_m[nm], out_v[nm] = [o[None] for o in _adamw(w, g, m, v, "adamw_" + nm)]
    out_g["w_in"], out_d["w_in"], out_m["w_in"], out_v["w_in"] = [
        o.T[None] for o in _adamw(w_in_t, g_w_in_t, m_w_in_t, v_w_in_t, "adamw_w_in")]
    small = {"b_ada": (b_ada, gb_ada, m_b_ada, v_b_ada), "g_mix": (g_mix, gg_mix, m_g_mix, v_g_mix),
             "b_in": (b_in, gb_in, m_b_in, v_b_in), "sinks": (sinks, gsinks, m_sinks, v_sinks),
             "conv_w": (conv_w, gcw_sh, m_conv_w, v_conv_w), "g_ffn": (g_ffn, gg_ffn, m_g_ffn, v_g_ffn),
             "g_final": (g_final, gg_final, m_g_final, v_g_final)}
    s_sizes = [w.size for w, _, _, _ in small.values()]
    s_rows = -(-sum(s_sizes) // LANES // 8) * 8

    def s_pack(k):
        return _to_lanes(jnp.concatenate([tup[k].reshape(-1) for tup in small.values()]), s_rows)

    s_out = _adamw(s_pack(0), s_pack(1), s_pack(2), s_pack(3), "adamw_small")
    s_off = 0
    for (nm, (w, g, _, _)), sz in zip(small.items(), s_sizes):
        out_g[nm] = g.reshape(w.shape)
        out_d[nm], out_m[nm], out_v[nm] = [o.reshape(-1)[s_off:s_off + sz].reshape(w.shape) for o in s_out[1:]]
        s_off += sz

    order = ["w_ada", "b_ada", "g_mix", "w_in", "b_in", "sinks", "conv_w", "w_out", "g_ffn", "w_ffn_in", "w_ffn_out",
             "g_final"]
    return (loss, grad_x[None], *[out_g[k] for k in order], *[out_d[k] for k in order],
            *[out_m[k] for k in order], *[out_v[k] for k in order])
```

```python
import functools

import jax
import jax.numpy as jnp
from jax import lax
from jax.experimental import pallas as pl
from jax.experimental.pallas import tpu as pltpu

F32 = jnp.float32
BF16 = jnp.bfloat16
EPS = 1e-6
HEAD_DIM = 64
GROUP = 8
BLOCK = 128
LANES = 128
SUBLANES_BF16 = 16
N_DEV = 8
N_CHIP = 4
VMEM_LIMIT = 56 * 1024 * 1024
MESH = pl.DeviceIdType.MESH

ADAM_LR = 0.001
ADAM_B1 = 0.9
ADAM_B2 = 0.999
ADAM_EPS = 1e-08
ADAM_WD = 0.01
ADAM_STEP = 10

SDS = jax.ShapeDtypeStruct
ANY = pl.BlockSpec(memory_space=pl.ANY)
VMEM_SPEC = pl.BlockSpec(memory_space=pltpu.VMEM)
SMEM_SPEC = pl.BlockSpec(memory_space=pltpu.SMEM)


def _params(*sem):
    return pltpu.CompilerParams(dimension_semantics=sem, vmem_limit_bytes=VMEM_LIMIT)


def _mesh_pos():
    return lax.axis_index("x"), lax.axis_index("y"), lax.axis_index("c")


def _row_tile(rows, cols, itemsize=4, budget=1 << 20, mult=8):
    best = None
    for t in range(mult, rows + 1, mult):
        if rows % t == 0 and t * cols * itemsize <= budget:
            best = t
    if best is None:
        best = rows
    return best


def _gather_all(v_ref, out_ref, send_sems, recv_sems, local_sem):
    x, y, c = _mesh_pos()
    me = 4 * x + 2 * y + c
    mine = pltpu.make_async_copy(v_ref, out_ref.at[me], local_sem)
    mine.start()
    peers = []
    for k in range(1, N_DEV):
        px = 1 - x if k & 4 else x
        py = 1 - y if k & 2 else y
        pc = 1 - c if k & 1 else c
        peers.append((px, py, pc))

    def copy(k, block):
        return pltpu.make_async_remote_copy(
            src_ref=v_ref, dst_ref=out_ref.at[block], send_sem=send_sems.at[k], recv_sem=recv_sems.at[k],
            device_id=peers[k], device_id_type=MESH)

    sends = [copy(k, me) for k in range(N_DEV - 1)]
    for cp in sends:
        cp.start()
    for k, (px, py, pc) in enumerate(peers):
        copy(k, 4 * px + 2 * py + pc).wait_recv()
    for cp in sends:
        cp.wait_send()
    mine.wait()


def _small_sems():
    return [pltpu.SemaphoreType.DMA((N_DEV - 1,)), pltpu.SemaphoreType.DMA((N_DEV - 1,)), pltpu.SemaphoreType.DMA]


def _tail_exchange(pack, full):
    def body(pack_ref, full_unused, packs_ref, full_ref, s1, r1, l1, send_sem, recv_sem):
        del full_unused
        x, y, c = _mesh_pos()
        half = full_ref.shape[0] // 2
        rows = pl.ds(pl.multiple_of(c * half, 8), half)
        swap = pltpu.make_async_remote_copy(
            src_ref=full_ref.at[rows], dst_ref=full_ref.at[rows], send_sem=send_sem, recv_sem=recv_sem,
            device_id=(x, y, 1 - c), device_id_type=MESH)
        swap.start()
        _gather_all(pack_ref, packs_ref, s1, r1, l1)
        swap.wait()

    return pl.pallas_call(
        body, name="tail_exchange", out_shape=[SDS((N_DEV,) + pack.shape, pack.dtype), SDS(full.shape, full.dtype)],
        in_specs=[VMEM_SPEC, ANY], out_specs=[VMEM_SPEC, ANY], input_output_aliases={1: 1},
        scratch_shapes=_small_sems() + [pltpu.SemaphoreType.DMA, pltpu.SemaphoreType.DMA])(pack, full)


def _other_chips(x, y):
    return [(1 - x, y), (x, 1 - y), (1 - x, 1 - y)]


def _startup(pack, w_ada_sh, b_ada_sh, w_buf):
    d, n = w_ada_sh.shape
    kc = d // LANES

    def body(pack_ref, wa_hbm, ba_ref, w_in_unused, packs_ref, mod_ref, w_ref, wa_scr, mod_scr,
             s1, r1, l1, s2, r2, l2, send_sems, recv_sems, fsend_sems, frecv_sems, relay_send, relay_recv, wa_sem):
        del w_in_unused
        x, y, c = _mesh_pos()
        j_me = 2 * x + y
        chips = _other_chips(x, y)
        half = w_ref.shape[1] // 2

        def rows_of(which):
            return pl.ds(pl.multiple_of(which * half, SUBLANES_BF16), half)

        def copy(p, block, rows, over_ici):
            sems = (send_sems, recv_sems) if over_ici else (fsend_sems, frecv_sems)
            return pltpu.make_async_remote_copy(
                src_ref=w_ref.at[block, rows], dst_ref=w_ref.at[block, rows], send_sem=sems[0].at[p],
                recv_sem=sems[1].at[p], device_id=(*chips[p], c) if over_ici else (x, y, 1 - c), device_id_type=MESH)

        def block_of(p):
            return 2 * chips[p][0] + chips[p][1]

        def relay(q, block):
            rows = pl.ds(pl.multiple_of(c * half + q * (half // 2), SUBLANES_BF16), half // 2)
            return pltpu.make_async_remote_copy(
                src_ref=w_ref.at[block, rows], dst_ref=w_ref.at[block, rows], send_sem=relay_send.at[q],
                recv_sem=relay_recv.at[q], device_id=(*chips[1 - q], c), device_id_type=MESH)

        load_wa = pltpu.make_async_copy(wa_hbm, wa_scr, wa_sem)
        load_wa.start()
        _gather_all(pack_ref, packs_ref, s1, r1, l1)
        sends = [copy(p, j_me, rows_of(c), True) for p in range(2)]
        for cp in sends:
            cp.start()
        load_wa.wait()
        acc = jnp.zeros((N_DEV, n), F32)
        for k in range(kc):
            ck = packs_ref[:, k, :]
            sk = (ck * jax.nn.sigmoid(ck)).astype(BF16)
            acc = acc + jnp.dot(sk, wa_scr[k * LANES:(k + 1) * LANES, :].astype(BF16), preferred_element_type=F32)
        mod_scr[...] = acc + ba_ref[...]
        _gather_all(mod_scr, mod_ref, s2, r2, l2)
        passed = []
        for q in range(2):
            copy(q, block_of(q), rows_of(c), True).wait_recv()
            for cp in (relay(q, block_of(q)), copy(q, block_of(q), rows_of(c), False)):
                cp.start()
                passed.append(cp)
        for q in range(2):
            relay(q, block_of(2)).wait_recv()
        fw = copy(2, block_of(2), rows_of(c), False)
        fw.start()
        for p in range(3):
            copy(p, block_of(p), rows_of(1 - c), False).wait_recv()
        for cp in sends + passed + [fw]:
            cp.wait_send()

    return pl.pallas_call(
        body, name="startup",
        out_shape=[SDS((N_DEV,) + pack.shape, F32), SDS((N_DEV, N_DEV, n), F32), SDS(w_buf.shape, w_buf.dtype)],
        in_specs=[VMEM_SPEC, ANY, VMEM_SPEC, ANY], out_specs=[VMEM_SPEC, VMEM_SPEC, ANY],
        input_output_aliases={3: 2},
        scratch_shapes=[pltpu.VMEM((d, n), F32), pltpu.VMEM((N_DEV, n), F32)] + _small_sems() + _small_sems()
        + [pltpu.SemaphoreType.DMA((3,))] * 4 + [pltpu.SemaphoreType.DMA((2,))] * 2 + [pltpu.SemaphoreType.DMA],
        compiler_params=pltpu.CompilerParams(vmem_limit_bytes=VMEM_LIMIT),
    )(pack, w_ada_sh, b_ada_sh, w_buf)


class _Exchange:
    def __init__(self, operands, out_shape, in_place, n_sems, copies):
        self.operands, self.out_shape, self.in_place, self.n_sems, self.copies = (
            list(operands), list(out_shape), in_place, n_sems, copies)

    def sems(self):
        return [pltpu.SemaphoreType.DMA((self.n_sems,)), pltpu.SemaphoreType.DMA((self.n_sems,))]


def _x_gather_ici(bufs):
    def copies(ins, outs, send_sems, recv_sems):
        x, y, c = _mesh_pos()
        chips = _other_chips(x, y)
        out = []
        for w in range(len(outs)):
            half = outs[w].shape[1] // 2
            rows = pl.ds(pl.multiple_of(c * half, SUBLANES_BF16), half)
            for p in range(3):
                out.append(pltpu.make_async_remote_copy(
                    src_ref=outs[w].at[2 * x + y, rows], dst_ref=outs[w].at[2 * x + y, rows],
                    send_sem=send_sems.at[w * 3 + p], recv_sem=recv_sems.at[w * 3 + p],
                    device_id=(*chips[p], c), device_id_type=MESH))
        return out

    return _Exchange(bufs, [SDS(b.shape, b.dtype) for b in bufs], True, 3 * len(bufs), copies)


def _x_gather_d2d(bufs):
    def copies(ins, outs, send_sems, recv_sems):
        x, y, c = _mesh_pos()
        chips = _other_chips(x, y)
        out = []
        for w in range(len(outs)):
            half = outs[w].shape[1] // 2
            rows = pl.ds(pl.multiple_of(c * half, SUBLANES_BF16), half)
            for p in range(3):
                block = 2 * chips[p][0] + chips[p][1]
                out.append(pltpu.make_async_remote_copy(
                    src_ref=outs[w].at[block, rows], dst_ref=outs[w].at[block, rows],
                    send_sem=send_sems.at[w * 3 + p], recv_sem=recv_sems.at[w * 3 + p],
                    device_id=(x, y, 1 - c), device_id_type=MESH))
        return out

    return _Exchange(bufs, [SDS(b.shape, b.dtype) for b in bufs], True, 3 * len(bufs), copies)


N_REMOTE = 6


def _x_reduce(grads32, grads16):
    n_w = len(grads32)

    def copies(ins, outs, send_sems, recv_sems):
        g32, g16 = ins[:n_w], ins[n_w:]
        from_sib, from_far = outs[:n_w], outs[n_w:]
        x, y, c = _mesh_pos()
        chips = _other_chips(x, y)
        out = []
        for w in range(n_w):
            half = g32[w].shape[1] // 2
            k0 = w * (N_REMOTE + 1)
            out.append(pltpu.make_async_remote_copy(
                src_ref=g32[w].at[2 * x + y, pl.ds(pl.multiple_of((1 - c) * half, SUBLANES_BF16), half), :],
                dst_ref=from_sib[w], send_sem=send_sems.at[k0], recv_sem=recv_sems.at[k0],
                device_id=(x, y, 1 - c), device_id_type=MESH))
            for p in range(3):
                for f in range(2):
                    tc = c if f == 0 else 1 - c
                    k = 2 * p + f
                    out.append(pltpu.make_async_remote_copy(
                        src_ref=g16[w].at[2 * chips[p][0] + chips[p][1],
                                          pl.ds(pl.multiple_of(tc * half, SUBLANES_BF16), half), :],
                        dst_ref=from_far[w].at[k], send_sem=send_sems.at[k0 + 1 + k], recv_sem=recv_sems.at[k0 + 1 + k],
                        device_id=(*chips[p], tc), device_id_type=MESH))
        return out

    shapes = ([SDS((g.shape[1] // 2, g.shape[2]), g.dtype) for g in grads32]
              + [SDS((N_REMOTE, g.shape[1] // 2, g.shape[2]), g.dtype) for g in grads16])
    return _Exchange(list(grads32) + list(grads16), shapes, False, (N_REMOTE + 1) * n_w, copies)


def _x_pair_exchange(fulls):
    def copies(ins, outs, send_sems, recv_sems):
        x, y, c = _mesh_pos()
        out = []
        for w in range(len(outs)):
            half = outs[w].shape[0] // 2
            rows = pl.ds(pl.multiple_of(c * half, 8), half)
            out.append(pltpu.make_async_remote_copy(
                src_ref=outs[w].at[rows], dst_ref=outs[w].at[rows], send_sem=send_sems.at[w],
                recv_sem=recv_sems.at[w], device_id=(x, y, 1 - c), device_id_type=MESH))
        return out

    return _Exchange(fulls, [SDS(f.shape, f.dtype) for f in fulls], True, len(fulls), copies)


def _pallas(body, *, name, grid, in_specs, out_specs, out_shape, args, scratch=(), sem=None, ride=None):
    single = not isinstance(out_specs, (list, tuple))
    out_specs_l = [out_specs] if single else list(out_specs)
    out_shape_l = [out_shape] if single else list(out_shape)
    n_in, n_out, n_scr = len(in_specs), len(out_specs_l), len(scratch)
    if ride is None:
        res = pl.pallas_call(body, name=name, grid=grid, in_specs=list(in_specs), out_specs=out_specs,
                             out_shape=out_shape, scratch_shapes=list(scratch), compiler_params=_params(*sem))(*args)
        return res, None
    n_x, n_xo = len(ride.operands), len(ride.out_shape)

    def full_body(*refs):
        ins, x_ins = refs[:n_in], refs[n_in:n_in + n_x]
        outs = refs[n_in + n_x:n_in + n_x + n_out]
        x_outs = refs[n_in + n_x + n_out:n_in + n_x + n_out + n_xo]
        rest = refs[n_in + n_x + n_out + n_xo:]
        scr, (send_sems, recv_sems) = rest[:n_scr], rest[n_scr:]
        first = functools.reduce(jnp.logical_and, [pl.program_id(a) == 0 for a in range(len(grid))])
        last = functools.reduce(jnp.logical_and, [pl.program_id(a) == grid[a] - 1 for a in range(len(grid))])

        @pl.when(first)
        def _():
            for cp in ride.copies(x_ins, x_outs, send_sems, recv_sems):
                cp.start()

        body(*ins, *outs, *scr)

        @pl.when(last)
        def _():
            for cp in ride.copies(x_ins, x_outs, send_sems, recv_sems):
                cp.wait()

    res = pl.pallas_call(
        full_body, name=name, grid=grid, in_specs=list(in_specs) + [ANY] * n_x,
        out_specs=out_specs_l + [ANY] * n_xo, out_shape=out_shape_l + ride.out_shape,
        input_output_aliases={n_in + k: n_out + k for k in range(n_x)} if ride.in_place else {},
        scratch_shapes=list(scratch) + ride.sems(),
        compiler_params=_params(*(["arbitrary"] * len(grid))))(*args, *ride.operands)
    own = res[0] if single else list(res[:n_out])
    return own, list(res[n_out:])


def _cast_into_block(pos, w, name):
    rows, cols = w.shape
    tr = _row_tile(rows, cols, mult=SUBLANES_BF16)

    def body(pos_ref, w_ref, o_ref):
        del pos_ref
        o_ref[...] = w_ref[...].astype(BF16)

    return pl.pallas_call(
        body, name=name,
        grid_spec=pltpu.PrefetchScalarGridSpec(
            num_scalar_prefetch=1, grid=(rows // tr,),
            in_specs=[pl.BlockSpec((tr, cols), lambda i, pos_ref: (i, 0))],
            out_specs=pl.BlockSpec((None, tr, cols), lambda i, pos_ref: (pos_ref[1], i, 0))),
        out_shape=SDS((N_CHIP, rows, cols), BF16), compiler_params=_params("parallel"))(pos, w)


def _sum_terms(pos, grad, from_sib, from_far, name):
    _, rows, cols = grad.shape
    half = rows // 2
    tr = _row_tile(half, cols, mult=SUBLANES_BF16)
    nblk = half // tr

    def body(pos_ref, g_ref, s_ref, r_ref, o_ref):
        del pos_ref
        acc = g_ref[...] + s_ref[...]
        for k in range(N_REMOTE):
            acc = acc + r_ref[k].astype(F32)
        o_ref[...] = acc

    return pl.pallas_call(
        body, name=name,
        grid_spec=pltpu.PrefetchScalarGridSpec(
            num_scalar_prefetch=1, grid=(nblk,),
            in_specs=[pl.BlockSpec((None, tr, cols), lambda i, pos_ref: (pos_ref[1], pos_ref[0] * nblk + i, 0)),
                      pl.BlockSpec((tr, cols), lambda i, pos_ref: (i, 0)),
                      pl.BlockSpec((N_REMOTE, tr, cols), lambda i, pos_ref: (0, i, 0))],
            out_specs=pl.BlockSpec((tr, cols), lambda i, pos_ref: (pos_ref[0] * nblk + i, 0))),
        out_shape=SDS((rows, cols), F32),
        compiler_params=_params("parallel"),
    )(pos, grad, from_sib, from_far)


def _adamw(w, g, m, v, name):
    rows, cols = w.shape
    tr = _row_tile(rows, cols)

    def body(w_ref, g_ref, m_ref, v_ref, go_ref, d_ref, nm_ref, nv_ref):
        gg = g_ref[...]
        go_ref[...] = gg
        nm = ADAM_B1 * m_ref[...] + (1.0 - ADAM_B1) * gg
        nv = ADAM_B2 * v_ref[...] + (1.0 - ADAM_B2) * (gg * gg)
        m_hat = nm / (1.0 - ADAM_B1 ** ADAM_STEP)
        v_hat = nv / (1.0 - ADAM_B2 ** ADAM_STEP)
        d_ref[...] = -ADAM_LR * (m_hat / (jnp.sqrt(v_hat) + ADAM_EPS) + ADAM_WD * w_ref[...])
        nm_ref[...] = nm
        nv_ref[...] = nv

    spec = pl.BlockSpec((tr, cols), lambda i: (i, 0))
    return pl.pallas_call(body, name=name, grid=(rows // tr,), in_specs=[spec] * 4, out_specs=[spec] * 4,
                          out_shape=[SDS((rows, cols), F32)] * 4, compiler_params=_params("parallel"))(w, g, m, v)


def _pack_sum(gathered):
    _, rows, cols = gathered.shape

    def body(g_ref, o_ref):
        acc = g_ref[0]
        for d in range(1, N_DEV):
            acc = acc + g_ref[d]
        o_ref[...] = acc

    return pl.pallas_call(body, name="pack_sum", in_specs=[VMEM_SPEC], out_specs=VMEM_SPEC,
                          out_shape=SDS((rows, cols), F32))(gathered)


def _ada_wgrad(c_all, dmod_sh):
    d = c_all.shape[1]
    n = dmod_sh.shape[1]
    tn = 512

    def body(c_ref, g_ref, o_ref):
        cc = c_ref[...]
        s = cc * jax.nn.sigmoid(cc)
        o_ref[...] = lax.dot_general(s, g_ref[...], (((0,), (0,)), ((), ())), preferred_element_type=F32,
                                     precision=lax.Precision.HIGHEST)

    return pl.pallas_call(
        body, name="ada_wgrad", grid=(n // tn,),
        in_specs=[pl.BlockSpec((N_DEV, d), lambda j: (0, 0)), pl.BlockSpec((N_DEV, tn), lambda j: (0, j))],
        out_specs=pl.BlockSpec((d, tn), lambda j: (0, j)),
        out_shape=SDS((d, n), F32), compiler_params=_params("parallel"))(c_all, dmod_sh)


def _rms(xf):
    return lax.rsqrt(jnp.mean(xf * xf, axis=-1, keepdims=True) + EPS)


def _in_proj(x, g, sc, sh, wt, b, tm, tn, ride=None):
    t, d = x.shape
    n = wt.shape[0]

    def body(x_ref, g_ref, sc_ref, sh_ref, w_ref, b_ref, z_ref, h_ref):
        @pl.when(pl.program_id(1) == 0)
        def _():
            xf = x_ref[...]
            h_ref[...] = ((xf * _rms(xf) * g_ref[...]) * (1.0 + sc_ref[...]) + sh_ref[...]).astype(BF16)

        acc = lax.dot_general(h_ref[...], w_ref[...], (((1,), (1,)), ((), ())), preferred_element_type=F32)
        z_ref[...] = (acc + b_ref[...]).astype(BF16)

    row = pl.BlockSpec((tm, d), lambda i, j: (i, 0))
    vec = pl.BlockSpec((1, d), lambda i, j: (0, 0))
    return _pallas(
        body, name="in_proj", grid=(t // tm, n // tn),
        in_specs=[row, vec, vec, vec, pl.BlockSpec((tn, d), lambda i, j: (j, 0)),
                  pl.BlockSpec((1, tn), lambda i, j: (0, j))],
        out_specs=[pl.BlockSpec((tm, tn), lambda i, j: (i, j)), row],
        out_shape=[SDS((t, n), BF16), SDS((t, d), BF16)], args=(x, g, sc, sh, wt, b),
        sem=("parallel", "arbitrary"), ride=ride)


def _segments(d, kvw2):
    o = d + kvw2
    names = ("cb", "cc", "cx", "ga", "gc")
    seg = {nm: slice(o + k * d, o + (k + 1) * d) for k, nm in enumerate(names)}
    seg["q"], seg["kv"] = slice(0, d), slice(d, o)
    return seg


def _attn_masks():
    rows = 4 * BLOCK
    r = lax.broadcasted_iota(jnp.int32, (rows, 2 * BLOCK), 0) & (BLOCK - 1)
    col = lax.broadcasted_iota(jnp.int32, (rows, 2 * BLOCK), 1)
    return (col > r) & (col <= r + BLOCK), col


def _kv_variants(kv, n_kv_w):
    assert n_kv_w == LANES
    kb, vb = kv[:, :LANES] * (HEAD_DIM ** -0.5), kv[:, LANES:]
    kr, vr = pltpu.roll(kb, HEAD_DIM, 1), pltpu.roll(vb, HEAD_DIM, 1)
    lane = lax.broadcasted_iota(jnp.int32, kb.shape, 1)
    lo = lane < HEAD_DIM
    zero = jnp.zeros_like(kb)
    k_eff = [[None, None], [None, None]]
    v_eff = [[None, None], [None, None]]
    for h in range(2):
        for e in range(2):
            ksrc, vsrc = (kb, vb) if e == h else (kr, vr)
            keep = lo if e == 0 else jnp.logical_not(lo)
            k_eff[h][e] = jnp.where(keep, ksrc, zero)
            v_eff[h][e] = jnp.where(keep, vsrc, zero)
    return k_eff, v_eff


def _sink_column(sinks_ref, h, e):
    rowblk = lax.broadcasted_iota(jnp.int32, (4 * BLOCK, 1), 0) // BLOCK
    col = jnp.zeros((4 * BLOCK, 1), F32)
    for j in range(4):
        col = jnp.where(rowblk == j, sinks_ref[0, GROUP * h + 2 * j + e], col)
    return col


def _softmax_sink(s, valid, sink):
    s = jnp.where(valid, s, -jnp.inf)
    m = jnp.maximum(jnp.max(s, axis=-1, keepdims=True), sink)
    p = jnp.exp(s - m)
    psink = jnp.exp(sink - m)
    den = jnp.sum(p, axis=-1, keepdims=True) + psink
    inv = 1.0 / den
    return p * inv, psink * inv


def _shift_down(a, s, prev):
    rows = a.shape[0]
    out = pltpu.roll(a, s, 0)
    row = lax.broadcasted_iota(jnp.int32, a.shape, 0)
    for t in range(s):
        out = jnp.where(row == t, prev[SUBLANES_BF16 - s + t:SUBLANES_BF16 - s + t + 1, :], out)
    del rows
    return out


def _shift_up(a, s, nxt):
    rows = a.shape[0]
    out = pltpu.roll(a, rows - s, 0)
    row = lax.broadcasted_iota(jnp.int32, a.shape, 0)
    for t in range(s):
        out = jnp.where(row == rows - s + t, nxt[t:t + 1, :], out)
    return out


def _stack_pairs(ref, h, rows=slice(None)):
    return jnp.concatenate([ref[rows, (4 * h + j) * LANES:(4 * h + j + 1) * LANES] for j in range(4)], axis=0)


FWD_BLOCKS = 4


def _mixer_fwd(z, sinks, conv_w, d, ride=None):
    t, zw = z.shape
    kvw2 = zw - 6 * d
    tq = FWD_BLOCKS * BLOCK
    halo = tq // SUBLANES_BF16
    seg = _segments(d, kvw2)

    def body(z_ref, kvp_ref, prev_ref, sinks_ref, cw_ref, attn_ref, merged_ref):
        n = pl.program_id(0)
        band, col = _attn_masks()
        for b in range(FWD_BLOCKS):
            rows = slice(b * BLOCK, (b + 1) * BLOCK)
            before = slice((b - 1) * BLOCK, b * BLOCK)
            kv_prev = kvp_ref[...] if b == 0 else z_ref[before, seg["kv"]]
            kv = jnp.concatenate([kv_prev, z_ref[rows, seg["kv"]]], axis=0)
            k_eff, v_eff = _kv_variants(kv, kvw2 // 2)
            valid = band & ((n > 0) | (col >= BLOCK)) if b == 0 else band
            for h in range(2):
                q4 = _stack_pairs(z_ref, h, rows)
                o4 = jnp.zeros((4 * BLOCK, LANES), F32)
                for e in range(2):
                    s = lax.dot_general(q4, k_eff[h][e], (((1,), (1,)), ((), ())), preferred_element_type=F32)
                    p, _ = _softmax_sink(s, valid, _sink_column(sinks_ref, h, e))
                    o4 = o4 + jnp.dot(p.astype(BF16), v_eff[h][e], preferred_element_type=F32)
                for j in range(4):
                    attn_ref[rows, (4 * h + j) * LANES:(4 * h + j + 1) * LANES] = (
                        o4[j * BLOCK:(j + 1) * BLOCK].astype(BF16))
            cb = z_ref[rows, seg["cb"]].astype(F32)
            p_in = z_ref[rows, seg["cc"]].astype(F32) * z_ref[rows, seg["cx"]].astype(F32)
            if b == 0:
                prev = jnp.where(n > 0, prev_ref[:, seg["cc"]].astype(F32) * prev_ref[:, seg["cx"]].astype(F32), 0.0)
            else:
                tail = slice(b * BLOCK - SUBLANES_BF16, b * BLOCK)
                prev = z_ref[tail, seg["cc"]].astype(F32) * z_ref[tail, seg["cx"]].astype(F32)
            cconv = (cw_ref[0:1, :] * _shift_down(p_in, 2, prev) + cw_ref[1:2, :] * _shift_down(p_in, 1, prev)
                     + cw_ref[2:3, :] * p_in)
            sa = jax.nn.sigmoid(z_ref[rows, seg["ga"]].astype(F32))
            sg = jax.nn.sigmoid(z_ref[rows, seg["gc"]].astype(F32))
            merged_ref[rows, :] = (sa * attn_ref[rows, :].astype(F32) + sg * (cb * cconv)).astype(BF16)

    blk = pl.BlockSpec((tq, d), lambda n: (n, 0))
    return _pallas(
        body, name="mixer_fwd", grid=(t // tq,),
        in_specs=[pl.BlockSpec((tq, zw), lambda n: (n, 0)),
                  pl.BlockSpec((BLOCK, kvw2), lambda n: (jnp.maximum(n * FWD_BLOCKS - 1, 0), d // kvw2)),
                  pl.BlockSpec((SUBLANES_BF16, zw), lambda n: (jnp.maximum(n * halo - 1, 0), 0)),
                  SMEM_SPEC, pl.BlockSpec((3, d), lambda n: (0, 0))],
        out_specs=[blk, blk],
        out_shape=[SDS((t, d), BF16), SDS((t, d), BF16)],
        args=(z, z, z, sinks, conv_w), sem=("parallel",), ride=ride)


def _out_proj_fwd(merged, w_out, x, ga1, g_ffn, sc2, sh2, tm):
    t, d = x.shape

    def body(m_ref, w_ref, x_ref, ga_ref, g_ref, sc_ref, sh_ref, y_ref, x1_ref, h_ref):
        y = jnp.dot(m_ref[...], w_ref[...], preferred_element_type=F32)
        x1 = x_ref[...] + ga_ref[...] * y
        y_ref[...] = y.astype(BF16)
        x1_ref[...] = x1
        h_ref[...] = ((x1 * _rms(x1) * g_ref[...]) * (1.0 + sc_ref[...]) + sh_ref[...]).astype(BF16)

    row = pl.BlockSpec((tm, d), lambda i: (i, 0))
    vec = pl.BlockSpec((1, d), lambda i: (0, 0))
    return pl.pallas_call(
        body, name="out_proj_fwd", grid=(t // tm,),
        in_specs=[row, pl.BlockSpec((d, d), lambda i: (0, 0)), row, vec, vec, vec, vec],
        out_specs=[row, row, row],
        out_shape=[SDS((t, d), BF16), SDS((t, d), F32), SDS((t, d), BF16)],
        compiler_params=_params("parallel"))(merged, w_out, x, ga1, g_ffn, sc2, sh2)


def _ffn_in_fwd(h2, w, ff, tm, tn):
    t, d = h2.shape
    nj = ff // tn
    assert w.shape == (2 * nj, d, tn)

    def body(h_ref, wg_ref, wu_ref, gu_ref, act_ref):
        hh = h_ref[...]
        g = jnp.dot(hh, wg_ref[...], preferred_element_type=F32)
        u = jnp.dot(hh, wu_ref[...], preferred_element_type=F32)
        gu_ref[0] = g.astype(BF16)
        gu_ref[1] = u.astype(BF16)
        act_ref[...] = ((g * jax.nn.sigmoid(g)) * u).astype(BF16)

    return pl.pallas_call(
        body, name="ffn_in_fwd", grid=(nj, t // tm),
        in_specs=[pl.BlockSpec((tm, d), lambda j, i: (i, 0)), pl.BlockSpec((None, d, tn), lambda j, i: (j, 0, 0)),
                  pl.BlockSpec((None, d, tn), lambda j, i: (j + nj, 0, 0))],
        out_specs=[pl.BlockSpec((2, tm, tn), lambda j, i: (0, i, j)), pl.BlockSpec((tm, tn), lambda j, i: (i, j))],
        out_shape=[SDS((2, t, ff), BF16), SDS((t, ff), BF16)],
        compiler_params=_params("parallel", "parallel"))(h2, w, w)


def _ffn_out_loss(act, w, x1, target, ga2, g_final, tm):
    t, d = x1.shape
    ff = act.shape[1]

    def body(a_ref, w_ref, x1_ref, tg_ref, ga_ref, gf_ref, dx2_ref, dy2_ref, st_ref):
        @pl.when(pl.program_id(0) == 0)
        def _():
            st_ref[...] = jnp.zeros_like(st_ref)

        halves = [slice(k * (tm // 2), (k + 1) * (tm // 2)) for k in range(2)]
        y2s = [jnp.dot(a_ref[rows, :], w_ref[...], preferred_element_type=F32) for rows in halves]
        for rows, y2 in zip(halves, y2s):
            x2 = x1_ref[rows, :] + ga_ref[...] * y2
            r = _rms(x2)
            yn = x2 * r
            err = yn * gf_ref[...] - tg_ref[rows, :]
            loss = 0.5 * jnp.sum(jnp.mean(err * err, axis=-1, keepdims=True), axis=0, keepdims=True)
            dy = err * (1.0 / d)
            u = dy * gf_ref[...]
            dx2 = r * (u - yn * jnp.mean(u * yn, axis=-1, keepdims=True))
            dx2_ref[rows, :] = dx2
            dy2_ref[rows, :] = (ga_ref[...] * dx2).astype(BF16)
            st_ref[0:1, :] += jnp.sum(dx2 * y2, axis=0, keepdims=True)
            st_ref[1:2, :] += jnp.sum(dy * yn, axis=0, keepdims=True)
            st_ref[2:3, :] += jnp.broadcast_to(loss, (1, d))

    row = pl.BlockSpec((tm, d), lambda i: (i, 0))
    vec = pl.BlockSpec((1, d), lambda i: (0, 0))
    return pl.pallas_call(
        body, name="ffn_out_loss", grid=(t // tm,),
        in_specs=[pl.BlockSpec((tm, ff), lambda i: (i, 0)),
                  pl.BlockSpec((ff, d), lambda i: (0, 0), pipeline_mode=pl.Buffered(1)), row, row, vec, vec],
        out_specs=[row, row, pl.BlockSpec((8, d), lambda i: (0, 0))],
        out_shape=[SDS((t, d), F32), SDS((t, d), BF16), SDS((8, d), F32)],
        compiler_params=_params("arbitrary"))(act, w, x1, target, ga2, g_final)


def _ffn_out_bwd(dy2, w, gu, tm, tn):
    t, d = dy2.shape
    ff = w.shape[0]

    def body(dy_ref, w_ref, gu_ref, o_ref):
        dy = dy_ref[...]
        for lo in range(0, tn, 3 * LANES):
            cols = slice(lo, min(lo + 3 * LANES, tn))
            dact = lax.dot_general(dy, w_ref[cols, :], (((1,), (1,)), ((), ())), preferred_element_type=F32)
            g = gu_ref[0, :, cols].astype(F32)
            u = gu_ref[1, :, cols].astype(F32)
            sg = jax.nn.sigmoid(g)
            a = dact * sg
            du = a * g
            o_ref[0, :, cols] = (u * (a + du * (1.0 - sg))).astype(BF16)
            o_ref[1, :, cols] = du.astype(BF16)

    gu_spec = pl.BlockSpec((2, tm, tn), lambda j, i: (0, i, j))
    return pl.pallas_call(
        body, name="ffn_out_bwd", grid=(ff // tn, t // tm),
        in_specs=[pl.BlockSpec((tm, d), lambda j, i: (i, 0)), pl.BlockSpec((tn, d), lambda j, i: (j, 0)), gu_spec],
        out_specs=gu_spec, out_shape=SDS((2, t, ff), BF16),
        compiler_params=_params("parallel", "parallel"))(dy2, w, gu)


def _wgrad(a, b, a_spec, b_spec, out_spec, out_shape, grid, name, ride=None):
    def body(a_ref, b_ref, o_ref, o16_ref):
        k = pl.program_id(len(grid) - 1)

        @pl.when(k == 0)
        def _():
            o_ref[...] = jnp.zeros_like(o_ref)

        o_ref[...] += lax.dot_general(a_ref[...], b_ref[...], (((0,), (0,)), ((), ())), preferred_element_type=F32)

        @pl.when(k == grid[-1] - 1)
        def _():
            o16_ref[...] = o_ref[...].astype(BF16)

    return _pallas(
        body, name=name, grid=grid, in_specs=[a_spec, b_spec], out_specs=[out_spec, out_spec],
        out_shape=[out_shape, SDS(out_shape.shape, BF16)], args=(a, b),
        sem=["parallel"] * (len(grid) - 1) + ["arbitrary"], ride=ride)


def _ffn_in_bwd(dgu, w, x1, dx2, y1, g_ffn, sc2, ga1, tm):
    t, d = x1.shape
    ff = dgu.shape[2]
    n_sh, _, sw = w.shape
    per = ff // sw
    nt = (((1,), (1,)), ((), ()))

    def body(a_ref, w_ref, x1_ref, dx2_ref, y1_ref, g_ref, sc_ref, ga_ref, dx1_ref, dy1_ref, st_ref):
        @pl.when(pl.program_id(0) == 0)
        def _():
            st_ref[...] = jnp.zeros_like(st_ref)

        dh = None
        for j in range(n_sh):
            part = lax.dot_general(a_ref[j // per, :, (j % per) * sw:(j % per + 1) * sw], w_ref[j], nt,
                                   preferred_element_type=F32)
            dh = part if dh is None else dh + part
        x1 = x1_ref[...]
        r = _rms(x1)
        xn = x1 * r
        g = g_ref[...]
        dn = dh * (1.0 + sc_ref[...])
        u = dn * g
        dx1 = dx2_ref[...] + r * (u - xn * jnp.mean(u * xn, axis=-1, keepdims=True))
        dx1_ref[...] = dx1
        dy1_ref[...] = (ga_ref[...] * dx1).astype(BF16)
        st_ref[0:1, :] += jnp.sum(dh, axis=0, keepdims=True)
        st_ref[1:2, :] += jnp.sum(dh * (xn * g), axis=0, keepdims=True)
        st_ref[2:3, :] += jnp.sum(dn * xn, axis=0, keepdims=True)
        st_ref[3:4, :] += jnp.sum(dx1 * y1_ref[...].astype(F32), axis=0, keepdims=True)

    row = pl.BlockSpec((tm, d), lambda i: (i, 0))
    vec = pl.BlockSpec((1, d), lambda i: (0, 0))
    return pl.pallas_call(
        body, name="ffn_in_bwd", grid=(t // tm,),
        in_specs=[pl.BlockSpec((2, tm, ff), lambda i: (0, i, 0)),
                  pl.BlockSpec((n_sh, d, sw), lambda i: (0, 0, 0), pipeline_mode=pl.Buffered(1)),
                  row, row, row, vec, vec, vec],
        out_specs=[row, row, pl.BlockSpec((8, d), lambda i: (0, 0))],
        out_shape=[SDS((t, d), F32), SDS((t, d), BF16), SDS((8, d), F32)],
        compiler_params=_params("arbitrary"))(dgu, w, x1, dx2, y1, g_ffn, sc2, ga1)


def _out_proj_bwd(dy1, w_out, tm, ride=None):
    t, d = dy1.shape

    def body(dy_ref, w_ref, o_ref):
        o_ref[...] = lax.dot_general(dy_ref[...], w_ref[...], (((1,), (1,)), ((), ())),
                                     preferred_element_type=F32).astype(BF16)

    row = pl.BlockSpec((tm, d), lambda i: (i, 0))
    return _pallas(body, name="out_proj_bwd", grid=(t // tm,),
                   in_specs=[row, pl.BlockSpec((d, d), lambda i: (0, 0))], out_specs=row,
                   out_shape=SDS((t, d), BF16), args=(dy1, w_out), sem=("parallel",), ride=ride)


BWD_BLOCKS = 2


def _mixer_bwd(z, dmerged, attn, sinks, conv_w, d, ride=None):
    t, zw = z.shape
    kvw2 = zw - 6 * d
    tq = BWD_BLOCKS * BLOCK
    steps = t // tq
    halo = tq // SUBLANES_BF16
    last_halo = t // SUBLANES_BF16 - 1
    scale = HEAD_DIM ** -0.5
    seg = _segments(d, kvw2)

    def body(z_ref, kvp_ref, prev_ref, next_ref, dm_ref, dmn_ref, attn_ref, sinks_ref, cw_ref,
             dz_ref, dkv_ref, db_ref, dbkv_ref, dcw_ref, dsk_ref, carry_ref):
        n = pl.program_id(0)

        @pl.when(n == 0)
        def _():
            carry_ref[...] = jnp.zeros_like(carry_ref)
            db_ref[...] = jnp.zeros_like(db_ref)
            dbkv_ref[...] = jnp.zeros_like(dbkv_ref)
            dcw_ref[...] = jnp.zeros_like(dcw_ref)
            dsk_ref[...] = jnp.zeros_like(dsk_ref)

        def one_block(b, pending):
            rows = slice(b * BLOCK, (b + 1) * BLOCK)
            before = slice((b - 1) * BLOCK, b * BLOCK)
            dm = dm_ref[rows, :].astype(F32)
            sa = jax.nn.sigmoid(z_ref[rows, seg["ga"]].astype(F32))
            dga = dm * attn_ref[rows, :].astype(F32) * sa * (1.0 - sa)
            dz_ref[rows, seg["ga"]] = dga.astype(BF16)
            db_ref[0:1, seg["ga"]] += jnp.sum(dga, axis=0, keepdims=True)
            dattn = (dm * sa).astype(BF16)

            kv_prev = kvp_ref[...] if b == 0 else z_ref[before, seg["kv"]]
            kv = jnp.concatenate([kv_prev, z_ref[rows, seg["kv"]]], axis=0)
            k_eff, v_eff = _kv_variants(kv, kvw2 // 2)
            band, col = _attn_masks()
            valid = band & ((n > 0) | (col >= BLOCK)) if b == 0 else band
            lane_lo = lax.broadcasted_iota(jnp.int32, (2 * BLOCK, LANES), 1) < HEAD_DIM
            sink_lane = lax.broadcasted_iota(jnp.int32, (1, LANES), 1)
            rowblk = lax.broadcasted_iota(jnp.int32, (4 * BLOCK, 1), 0) // BLOCK
            dk_acc = [jnp.zeros((2 * BLOCK, LANES), F32), jnp.zeros((2 * BLOCK, LANES), F32)]
            dv_acc = [jnp.zeros((2 * BLOCK, LANES), F32), jnp.zeros((2 * BLOCK, LANES), F32)]
            dsink = jnp.zeros((1, LANES), F32)
            for h in range(2):
                q4 = _stack_pairs(z_ref, h, rows)
                do4 = jnp.concatenate([dattn[:, (4 * h + j) * LANES:(4 * h + j + 1) * LANES] for j in range(4)],
                                      axis=0)
                dq4 = jnp.zeros((4 * BLOCK, LANES), F32)
                for e in range(2):
                    s = lax.dot_general(q4, k_eff[h][e], (((1,), (1,)), ((), ())), preferred_element_type=F32)
                    p, psink = _softmax_sink(s, valid, _sink_column(sinks_ref, h, e))
                    dp = lax.dot_general(do4, v_eff[h][e], (((1,), (1,)), ((), ())), preferred_element_type=F32)
                    delta = jnp.sum(p * dp, axis=-1, keepdims=True)
                    ds = (p * (dp - delta)).astype(BF16)
                    dq4 = dq4 + jnp.dot(ds, k_eff[h][e], preferred_element_type=F32)
                    dk = lax.dot_general(q4, ds, (((0,), (0,)), ((), ())), preferred_element_type=F32).T
                    dv = lax.dot_general(do4, p.astype(BF16), (((0,), (0,)), ((), ())), preferred_element_type=F32).T
                    keep = lane_lo if e == 0 else jnp.logical_not(lane_lo)
                    slot = 0 if e == h else 1
                    dk_acc[slot] = dk_acc[slot] + jnp.where(keep, dk, 0.0)
                    dv_acc[slot] = dv_acc[slot] + jnp.where(keep, dv, 0.0)
                    dsk = -(psink * delta)
                    for j in range(4):
                        tot = jnp.sum(jnp.where(rowblk == j, dsk, 0.0), axis=0, keepdims=True)
                        dsink = dsink + jnp.where(sink_lane == GROUP * h + 2 * j + e, tot, 0.0)
                for j in range(4):
                    cols = slice((4 * h + j) * LANES, (4 * h + j + 1) * LANES)
                    dqj = dq4[j * BLOCK:(j + 1) * BLOCK]
                    dz_ref[rows, cols] = dqj.astype(BF16)
                    db_ref[0:1, cols] += jnp.sum(dqj, axis=0, keepdims=True)
            dsk_ref[0:1, :] += dsink
            dkv_new = jnp.concatenate([(dk_acc[0] + pltpu.roll(dk_acc[1], HEAD_DIM, 1)) * scale,
                                       dv_acc[0] + pltpu.roll(dv_acc[1], HEAD_DIM, 1)], axis=1)
            done = pending + dkv_new[:BLOCK]
            dkv_ref[rows, :] = done.astype(BF16)
            dbkv_ref[0:1, :] += jnp.sum(done, axis=0, keepdims=True)

            cb = z_ref[rows, seg["cb"]].astype(F32)
            cc = z_ref[rows, seg["cc"]].astype(F32)
            cx = z_ref[rows, seg["cx"]].astype(F32)
            sg = jax.nn.sigmoid(z_ref[rows, seg["gc"]].astype(F32))
            p_in = cc * cx
            if b == 0:
                prev = jnp.where(n > 0, prev_ref[:, seg["cc"]].astype(F32) * prev_ref[:, seg["cx"]].astype(F32), 0.0)
            else:
                tail = slice(b * BLOCK - SUBLANES_BF16, b * BLOCK)
                prev = z_ref[tail, seg["cc"]].astype(F32) * z_ref[tail, seg["cx"]].astype(F32)
            p_m1 = _shift_down(p_in, 1, prev)
            p_m2 = _shift_down(p_in, 2, prev)
            w0, w1, w2 = cw_ref[0:1, :], cw_ref[1:2, :], cw_ref[2:3, :]
            cconv = w0 * p_m2 + w1 * p_m1 + w2 * p_in
            dconv = dm * sg
            dgc = dm * (cb * cconv) * sg * (1.0 - sg)
            dcb = dconv * cconv
            dcc_t = dconv * cb
            if b == BWD_BLOCKS - 1:
                nxt = jnp.where(n < steps - 1,
                                dmn_ref[...].astype(F32) * jax.nn.sigmoid(next_ref[:, seg["gc"]].astype(F32))
                                * next_ref[:, seg["cb"]].astype(F32), 0.0)
            else:
                head = slice((b + 1) * BLOCK, (b + 1) * BLOCK + SUBLANES_BF16)
                nxt = (dm_ref[head, :].astype(F32) * jax.nn.sigmoid(z_ref[head, seg["gc"]].astype(F32))
                       * z_ref[head, seg["cb"]].astype(F32))
            dpin = w2 * dcc_t + w1 * _shift_up(dcc_t, 1, nxt) + w0 * _shift_up(dcc_t, 2, nxt)
            for nm, val in (("cb", dcb), ("cc", dpin * cx), ("cx", dpin * cc), ("gc", dgc)):
                dz_ref[rows, seg[nm]] = val.astype(BF16)
                db_ref[0:1, seg[nm]] += jnp.sum(val, axis=0, keepdims=True)
            dcw_ref[0:1, :] += jnp.sum(dcc_t * p_m2, axis=0, keepdims=True)
            dcw_ref[1:2, :] += jnp.sum(dcc_t * p_m1, axis=0, keepdims=True)
            dcw_ref[2:3, :] += jnp.sum(dcc_t * p_in, axis=0, keepdims=True)
            return dkv_new[BLOCK:]

        @pl.when(n < steps)
        def _():
            pending = carry_ref[...]
            for b in range(BWD_BLOCKS):
                pending = one_block(b, pending)
            carry_ref[...] = pending

        @pl.when(n == steps)
        def _():
            done = carry_ref[...]
            dkv_ref[:BLOCK, :] = done.astype(BF16)
            dkv_ref[BLOCK:, :] = jnp.zeros((tq - BLOCK, kvw2), BF16)
            dbkv_ref[0:1, :] += jnp.sum(done, axis=0, keepdims=True)

    def cur(n):
        return jnp.minimum(n, steps - 1)

    def after(n):
        return jnp.minimum((cur(n) + 1) * halo, last_halo)

    blk = pl.BlockSpec((tq, d), lambda n: (cur(n), 0))
    return _pallas(
        body, name="mixer_bwd", grid=(steps + 1,), ride=ride, sem=("arbitrary",),
        args=(z, z, z, z, dmerged, dmerged, attn, sinks, conv_w),
        in_specs=[pl.BlockSpec((tq, zw), lambda n: (cur(n), 0)),
                  pl.BlockSpec((BLOCK, kvw2), lambda n: (jnp.maximum(cur(n) * BWD_BLOCKS - 1, 0), d // kvw2)),
                  pl.BlockSpec((SUBLANES_BF16, zw), lambda n: (jnp.maximum(cur(n) * halo - 1, 0), 0)),
                  pl.BlockSpec((SUBLANES_BF16, zw), lambda n: (after(n), 0)),
                  blk,
                  pl.BlockSpec((SUBLANES_BF16, d), lambda n: (after(n), 0)),
                  blk, SMEM_SPEC, pl.BlockSpec((3, d), lambda n: (0, 0))],
        out_specs=[pl.BlockSpec((tq, zw), lambda n: (cur(n), 0)),
                   pl.BlockSpec((tq, kvw2), lambda n: (n, 0)),
                   pl.BlockSpec((8, zw), lambda n: (0, 0)), pl.BlockSpec((8, kvw2), lambda n: (0, 0)),
                   pl.BlockSpec((8, d), lambda n: (0, 0)), pl.BlockSpec((8, LANES), lambda n: (0, 0))],
        out_shape=[SDS((t, zw), BF16), SDS((t + tq, kvw2), BF16), SDS((8, zw), F32), SDS((8, kvw2), F32),
                   SDS((8, d), F32), SDS((8, LANES), F32)],
        scratch=[pltpu.VMEM((BLOCK, kvw2), F32)])


def _wgrad_in(dz, dkv, h1, tk, ride=None):
    t, zw = dz.shape
    d = h1.shape[1]
    kvw2 = dkv.shape[1]
    blk = d + kvw2
    assert zw % blk == 0
    tn = (((0,), (0,)), ((), ()))

    def body(a_ref, akv_ref, h_ref, o_ref, o16_ref):
        n, k = pl.program_id(0), pl.program_id(1)

        @pl.when(k == 0)
        def _():
            o_ref[...] = jnp.zeros_like(o_ref)

        @pl.when(n == 0)
        def _():
            o_ref[:d, :] += lax.dot_general(a_ref[:, :d], h_ref[...], tn, preferred_element_type=F32)
            o_ref[d:, :] += lax.dot_general(akv_ref[...], h_ref[...], tn, preferred_element_type=F32)

        @pl.when(n > 0)
        def _():
            o_ref[...] += lax.dot_general(a_ref[...], h_ref[...], tn, preferred_element_type=F32)

        @pl.when(k == t // tk - 1)
        def _():
            o16_ref[...] = o_ref[...].astype(BF16)

    out_spec = pl.BlockSpec((blk, d), lambda n, k: (n, 0))
    return _pallas(
        body, name="wgrad_in", grid=(zw // blk, t // tk),
        in_specs=[pl.BlockSpec((tk, blk), lambda n, k: (k, n)), pl.BlockSpec((tk, kvw2), lambda n, k: (k, 0)),
                  pl.BlockSpec((tk, d), lambda n, k: (k, 0))],
        out_specs=[out_spec, out_spec], out_shape=[SDS((zw, d), F32), SDS((zw, d), BF16)],
        args=(dz, dkv, h1), sem=("parallel", "arbitrary"), ride=ride)


def _in_proj_bwd(dz, dkv, wt, x, dx1, g_mix, sc1, tm, ride=None):
    t, d = x.shape
    zw = dz.shape[1]
    kvw2 = dkv.shape[1]
    rest = d + kvw2

    def body(a_ref, akv_ref, w_ref, x_ref, dx1_ref, g_ref, sc_ref, gx_ref, st_ref):
        @pl.when(pl.program_id(0) == 0)
        def _():
            st_ref[...] = jnp.zeros_like(st_ref)

        dh = (jnp.dot(a_ref[:, :d], w_ref[:d, :], preferred_element_type=F32)
              + jnp.dot(akv_ref[...], w_ref[d:rest, :], preferred_element_type=F32)
              + jnp.dot(a_ref[:, rest:], w_ref[rest:, :], preferred_element_type=F32))
        xx = x_ref[...]
        r = _rms(xx)
        xn = xx * r
        g = g_ref[...]
        dn = dh * (1.0 + sc_ref[...])
        u = dn * g
        gx_ref[...] = dx1_ref[...] + r * (u - xn * jnp.mean(u * xn, axis=-1, keepdims=True))
        st_ref[0:1, :] += jnp.sum(dh, axis=0, keepdims=True)
        st_ref[1:2, :] += jnp.sum(dh * (xn * g), axis=0, keepdims=True)
        st_ref[2:3, :] += jnp.sum(dn * xn, axis=0, keepdims=True)

    row = pl.BlockSpec((tm, d), lambda i: (i, 0))
    vec = pl.BlockSpec((1, d), lambda i: (0, 0))
    return _pallas(
        body, name="in_proj_bwd", grid=(t // tm,),
        in_specs=[pl.BlockSpec((tm, zw), lambda i: (i, 0)), pl.BlockSpec((tm, kvw2), lambda i: (i, 0)),
                  pl.BlockSpec((zw, d), lambda i: (0, 0), pipeline_mode=pl.Buffered(1)),
                  row, row, vec, vec],
        out_specs=[row, pl.BlockSpec((8, d), lambda i: (0, 0))],
        out_shape=[SDS((t, d), F32), SDS((8, d), F32)],
        args=(dz, dkv, wt, x, dx1, g_mix, sc1), sem=("arbitrary",), ride=ride)


def _to_lanes(v, rows=None):
    flat = v.reshape(-1)
    need = -(-flat.shape[0] // LANES)
    need = -(-need // 8) * 8 if rows is None else rows
    return jnp.pad(flat, (0, need * LANES - flat.shape[0])).reshape(need, LANES)


def kernel(x, c, w_ada, b_ada, g_mix, w_in, b_in, sinks, conv_w, w_out, g_ffn, w_ffn_in, w_ffn_out, g_final, loss_target, m_w_ada, m_b_ada, m_g_mix, m_w_in, m_b_in, m_sinks, m_conv_w, m_w_out, m_g_ffn, m_w_ffn_in, m_w_ffn_out, m_g_final, v_w_ada, v_b_ada, v_g_mix, v_w_in, v_b_in, v_sinks, v_conv_w, v_w_out, v_g_ffn, v_w_ffn_in, v_w_ffn_out, v_g_final):
    xs, tgt = x[0], loss_target[0]
    t, d = xs.shape
    zw = w_in.shape[2] * N_CHIP
    kvw2 = zw - 6 * d
    ff = w_ffn_out.shape[1] * N_CHIP
    n_mod = w_ada.shape[2] * N_CHIP // d
    mod_sh = w_ada.shape[2]
    cw_sh = conv_w.shape[2]
    assert d % (8 * LANES) == 0 and kvw2 == 2 * LANES and t % 512 == 0 and n_mod == 6
    xi, yi, ci = _mesh_pos()
    j_me = 2 * xi + yi
    b_me = 4 * xi + 2 * yi + ci
    pos = jnp.stack([ci, j_me]).astype(jnp.int32)
    tm = 512

    w_in_t, m_w_in_t, v_w_in_t = w_in[0].T, m_w_in[0].T, v_w_in[0].T
    assert d == 8 * LANES
    pack1 = jnp.concatenate([c.reshape(d // LANES, LANES), conv_w[0].reshape(-1, LANES)], axis=0)
    pack1 = jnp.pad(pack1, ((0, 16 - pack1.shape[0]), (0, 0)))
    b_ada_sh = lax.dynamic_slice(b_ada, (0, j_me * mod_sh), (1, mod_sh))
    g1, mod_all, w_in_g = _startup(pack1, w_ada[0], b_ada_sh, _cast_into_block(pos, w_in_t, "cast_w_in"))
    c_all = g1[:, :d // LANES, :].reshape(N_DEV, d)
    cw_rows = 3 * cw_sh // LANES
    conv_w_full = jnp.concatenate(
        [g1[2 * j, d // LANES:d // LANES + cw_rows, :].reshape(3, cw_sh) for j in range(N_CHIP)], axis=1)
    mod = jnp.concatenate([lax.dynamic_index_in_dim(mod_all[2 * j], b_me, 0, keepdims=True) for j in range(N_CHIP)],
                          axis=1)
    sh1, sc1, ga1, sh2, sc2, ga2 = [mod[:, k * d:(k + 1) * d] for k in range(6)]
    w_in_tf = w_in_g.reshape(zw, d)
    later = [_cast_into_block(pos, w_out[0], "cast_w_out"), _cast_into_block(pos, w_ffn_in[0], "cast_w_ffn_in"),
             _cast_into_block(pos, w_ffn_out[0], "cast_w_ffn_out")]

    (z, h1), later = _in_proj(xs, g_mix, sc1, sh1, w_in_tf, b_in, min(t, 1024), zw // 5, ride=_x_gather_ici(later))
    (attn, merged), later = _mixer_fwd(z, sinks, conv_w_full, d, ride=_x_gather_d2d(later))
    w_out_f = later[0].reshape(d, d)
    w_ffn_in_f = later[1]
    w_ffn_out_f = later[2].reshape(ff, d)
    tml = min(t, 1024)
    y1, x1, h2 = _out_proj_fwd(merged, w_out_f, xs, ga1, g_ffn, sc2, sh2, tml)
    gu, act = _ffn_in_fwd(h2, w_ffn_in_f, ff, tml, ff // 2)
    dx2, dy2, st_loss = _ffn_out_loss(act, w_ffn_out_f, x1, tgt, ga2, g_final.reshape(1, d), tml)

    dgu = _ffn_out_bwd(dy2, w_ffn_out_f, gu, tml, ff // 2)
    tk = min(t, 2048)
    dw_ffn_out, _ = _wgrad(
        act, dy2, pl.BlockSpec((tk, ff // 2), lambda m, k: (k, m)), pl.BlockSpec((tk, d), lambda m, k: (k, 0)),
        pl.BlockSpec((ff // 2, d), lambda m, k: (m, 0)), SDS((ff, d), F32), (2, t // tk), "wgrad_ffn_out")
    dx1, dy1, st_ffn = _ffn_in_bwd(dgu, w_ffn_in_f, x1, dx2, y1, g_ffn, sc2, ga1, tm)
    dw_ffn_in, _ = _wgrad(
        h2, dgu, pl.BlockSpec((tk, d), lambda n, k: (k, 0)),
        pl.BlockSpec((None, tk, ff // 2), lambda n, k: (n // 2, k, n % 2)),
        pl.BlockSpec((None, d, ff // 2), lambda n, k: (n, 0, 0)), SDS((N_CHIP, d, ff // 2), F32),
        (N_CHIP, t // tk), "wgrad_ffn_in")
    dw_out, _ = _wgrad(
        merged, dy1, pl.BlockSpec((tk, d), lambda m, k: (k, 0)), pl.BlockSpec((tk, d), lambda m, k: (k, 0)),
        pl.BlockSpec((d, d), lambda m, k: (0, 0)), SDS((d, d), F32), (1, t // tk), "wgrad_out")

    early = [[g.reshape(N_CHIP, -1, g.shape[-1]) for g in pair] for pair in (dw_out, dw_ffn_in, dw_ffn_out)]
    early_names = ["w_out", "w_ffn_in", "w_ffn_out"]
    dmerged, _ = _out_proj_bwd(dy1, w_out_f, tml)
    (dz, dkv_shifted, db_z, db_kv, dcw, dsk), terms = _mixer_bwd(
        z, dmerged, attn, sinks, conv_w_full, d, ride=_x_reduce([e[0] for e in early], [e[1] for e in early]))
    dkv = dkv_shifted[BLOCK:BLOCK + t]
    fulls = [_sum_terms(pos, e[0], s, r, "sum_terms_" + nm)
             for e, s, r, nm in zip(early, terms[:3], terms[3:], early_names)]
    dw_in_t, (g_w_out, g_w_ffn_in, g_w_ffn_out) = _wgrad_in(dz, dkv, h1, tk, ride=_x_pair_exchange(fulls))
    dw_in_t = [g.reshape(N_CHIP, zw // N_CHIP, d) for g in dw_in_t]

    (grad_x, st_in), (from_sib, from_far) = _in_proj_bwd(dz, dkv, w_in_tf, xs, dx1, g_mix, sc1, tm,
                                                         ride=_x_reduce([dw_in_t[0]], [dw_in_t[1]]))
    g_w_in_half = _sum_terms(pos, dw_in_t[0], from_sib, from_far, "sum_terms_w_in")

    dmod = jnp.concatenate([st_in[0:1], st_in[1:2], st_ffn[3:4], st_ffn[0:1], st_ffn[1:2], st_loss[0:1]], axis=1)
    db_in = jnp.concatenate([db_z[0:1, :d], db_kv[0:1], db_z[0:1, d + kvw2:]], axis=1)
    seg = [dmod, st_in[2:3], db_in, dsk[0:1], dcw[0:3].reshape(1, 3 * d), st_ffn[2:3], st_loss[1:2],
           st_loss[2:3, :LANES]]
    sizes = [s.shape[1] for s in seg]
    pack2 = _to_lanes(jnp.concatenate(seg, axis=1))
    packs, g_w_in_t = _tail_exchange(pack2, g_w_in_half)
    tot = _pack_sum(packs).reshape(-1)
    offs = [sum(sizes[:k]) for k in range(len(sizes))]
    gb_ada, gg_mix, gb_in, gsinks, gcw, gg_ffn, gg_final, loss_v = [tot[o:o + s] for o, s in zip(offs, sizes)]
    loss = loss_v[0]
    gsinks = gsinks[:sinks.shape[1]]
    gcw_sh = lax.dynamic_slice(gcw.reshape(3, d), (0, j_me * cw_sh), (3, cw_sh))

    dmod_all = packs[:, :n_mod * d // LANES, :].reshape(N_DEV, n_mod * d)
    g_w_ada = _ada_wgrad(c_all, lax.dynamic_slice(dmod_all, (0, j_me * mod_sh), (N_DEV, mod_sh)))

    out_g, out_d, out_m, out_v = {}, {}, {}, {}
    big = {"w_ada": (w_ada[0], g_w_ada, m_w_ada[0], v_w_ada[0]),
           "w_out": (w_out[0], g_w_out, m_w_out[0], v_w_out[0]),
           "w_ffn_in": (w_ffn_in[0], g_w_ffn_in, m_w_ffn_in[0], v_w_ffn_in[0]),
           "w_ffn_out": (w_ffn_out[0], g_w_ffn_out, m_w_ffn_out[0], v_w_ffn_out[0])}
    for nm, (w, g, m, v) in big.items():
        out_g[nm], out_d[nm], out_m[nm], out_v[nm] = [o[None] for o in _adamw(w, g, m, v, "adamw_" + nm)]
    out_g["w_in"], out_d["w_in"], out_m["w_in"], out_v["w_in"] = [
        o.T[None] for o in _adamw(w_in_t, g_w_in_t, m_w_in_t, v_w_in_t, "adamw_w_in")]
    small = {"b_ada": (b_ada, gb_ada, m_b_ada, v_b_ada), "g_mix": (g_mix, gg_mix, m_g_mix, v_g_mix),
             "b_in": (b_in, gb_in, m_b_in, v_b_in), "sinks": (sinks, gsinks, m_sinks, v_sinks),
             "conv_w": (conv_w, gcw_sh, m_conv_w, v_conv_w), "g_ffn": (g_ffn, gg_ffn, m_g_ffn, v_g_ffn),
             "g_final": (g_final, gg_final, m_g_final, v_g_final)}
    s_sizes = [w.size for w, _, _, _ in small.values()]
    s_rows = -(-sum(s_sizes) // LANES // 8) * 8

    def s_pack(k):
        return _to_lanes(jnp.concatenate([tup[k].reshape(-1) for tup in small.values()]), s_rows)

    s_out = _adamw(s_pack(0), s_pack(1), s_pack(2), s_pack(3), "adamw_small")
    s_off = 0
    for (nm, (w, g, _, _)), sz in zip(small.items(), s_sizes):
        out_g[nm] = g.reshape(w.shape)
        out_d[nm], out_m[nm], out_v[nm] = [o.reshape(-1)[s_off:s_off + sz].reshape(w.shape) for o in s_out[1:]]
        s_off += sz

    order = ["w_ada", "b_ada", "g_mix", "w_in", "b_in", "sinks", "conv_w", "w_out", "g_ffn", "w_ffn_in", "w_ffn_out",
             "g_final"]
    return (loss, grad_x[None], *[out_g[k] for k in order], *[out_d[k] for k in order],
            *[out_m[k] for k in order], *[out_v[k] for k in order])
```

```python
import functools

import jax
import jax.numpy as jnp
from jax import lax
from jax.experimental import pallas as pl
from jax.experimental.pallas import tpu as pltpu

F32 = jnp.float32
BF16 = jnp.bfloat16
EPS = 1e-6
HEAD_DIM = 64
GROUP = 8
BLOCK = 128
LANES = 128
SUBLANES_BF16 = 16
N_DEV = 8
N_CHIP = 4
VMEM_LIMIT = 56 * 1024 * 1024
MESH = pl.DeviceIdType.MESH

ADAM_LR = 0.001
ADAM_B1 = 0.9
ADAM_B2 = 0.999
ADAM_EPS = 1e-08
ADAM_WD = 0.01
ADAM_STEP = 10

SDS = jax.ShapeDtypeStruct
ANY = pl.BlockSpec(memory_space=pl.ANY)
VMEM_SPEC = pl.BlockSpec(memory_space=pltpu.VMEM)
SMEM_SPEC = pl.BlockSpec(memory_space=pltpu.SMEM)


def _params(*sem):
    return pltpu.CompilerParams(dimension_semantics=sem, vmem_limit_bytes=VMEM_LIMIT)


def _vec(v, d):
    arr, k = v if isinstance(v, tuple) else (v, 0)
    return arr, pl.BlockSpec((1, d), lambda *_: (0, k))


def _mesh_pos():
    return lax.axis_index("x"), lax.axis_index("y"), lax.axis_index("c")


def _row_tile(rows, cols, itemsize=4, budget=1 << 20, mult=8):
    best = None
    for t in range(mult, rows + 1, mult):
        if rows % t == 0 and t * cols * itemsize <= budget:
            best = t
    if best is None:
        best = rows
    return best


def _gather_all(v_ref, out_ref, send_sems, recv_sems, local_sem):
    x, y, c = _mesh_pos()
    me = 4 * x + 2 * y + c
    mine = pltpu.make_async_copy(v_ref, out_ref.at[me], local_sem)
    mine.start()
    peers = []
    for k in range(1, N_DEV):
        px = 1 - x if k & 4 else x
        py = 1 - y if k & 2 else y
        pc = 1 - c if k & 1 else c
        peers.append((px, py, pc))

    def copy(k, block):
        return pltpu.make_async_remote_copy(
            src_ref=v_ref, dst_ref=out_ref.at[block], send_sem=send_sems.at[k], recv_sem=recv_sems.at[k],
            device_id=peers[k], device_id_type=MESH)

    sends = [copy(k, me) for k in range(N_DEV - 1)]
    for cp in sends:
        cp.start()
    for k, (px, py, pc) in enumerate(peers):
        copy(k, 4 * px + 2 * py + pc).wait_recv()
    for cp in sends:
        cp.wait_send()
    mine.wait()


def _small_sems():
    return [pltpu.SemaphoreType.DMA((N_DEV - 1,)), pltpu.SemaphoreType.DMA((N_DEV - 1,)), pltpu.SemaphoreType.DMA]


def _tail_exchange(pack, full):
    def body(pack_ref, full_unused, packs_ref, full_ref, s1, r1, l1, send_sem, recv_sem):
        del full_unused
        x, y, c = _mesh_pos()
        half = full_ref.shape[0] // 2
        rows = pl.ds(pl.multiple_of(c * half, 8), half)
        swap = pltpu.make_async_remote_copy(
            src_ref=full_ref.at[rows], dst_ref=full_ref.at[rows], send_sem=send_sem, recv_sem=recv_sem,
            device_id=(x, y, 1 - c), device_id_type=MESH)
        swap.start()
        _gather_all(pack_ref, packs_ref, s1, r1, l1)
        swap.wait()

    return pl.pallas_call(
        body, name="tail_exchange", out_shape=[SDS((N_DEV,) + pack.shape, pack.dtype), SDS(full.shape, full.dtype)],
        in_specs=[VMEM_SPEC, ANY], out_specs=[VMEM_SPEC, ANY], input_output_aliases={1: 1},
        scratch_shapes=_small_sems() + [pltpu.SemaphoreType.DMA, pltpu.SemaphoreType.DMA])(pack, full)


def _other_chips(x, y):
    return [(1 - x, y), (x, 1 - y), (1 - x, 1 - y)]


def _startup(pack, w_ada_sh, b_ada_sh, w_buf):
    d, n = w_ada_sh.shape
    kc = d // LANES

    def body(pack_ref, wa_hbm, ba_ref, w_in_unused, packs_ref, mine_ref, w_ref, wa_scr, mod_scr, mod_ref,
             s1, r1, l1, s2, r2, l2, send_sems, recv_sems, fsend_sems, frecv_sems, relay_send, relay_recv, wa_sem):
        del w_in_unused
        x, y, c = _mesh_pos()
        j_me = 2 * x + y
        chips = _other_chips(x, y)
        half = w_ref.shape[1] // 2

        def rows_of(which):
            return pl.ds(pl.multiple_of(which * half, SUBLANES_BF16), half)

        def copy(p, block, rows, over_ici):
            sems = (send_sems, recv_sems) if over_ici else (fsend_sems, frecv_sems)
            return pltpu.make_async_remote_copy(
                src_ref=w_ref.at[block, rows], dst_ref=w_ref.at[block, rows], send_sem=sems[0].at[p],
                recv_sem=sems[1].at[p], device_id=(*chips[p], c) if over_ici else (x, y, 1 - c), device_id_type=MESH)

        def block_of(p):
            return 2 * chips[p][0] + chips[p][1]

        def relay(q, block):
            rows = pl.ds(pl.multiple_of(c * half + q * (half // 2), SUBLANES_BF16), half // 2)
            return pltpu.make_async_remote_copy(
                src_ref=w_ref.at[block, rows], dst_ref=w_ref.at[block, rows], send_sem=relay_send.at[q],
                recv_sem=relay_recv.at[q], device_id=(*chips[1 - q], c), device_id_type=MESH)

        load_wa = pltpu.make_async_copy(wa_hbm, wa_scr, wa_sem)
        load_wa.start()
        _gather_all(pack_ref, packs_ref, s1, r1, l1)
        sends = [copy(p, j_me, rows_of(c), True) for p in range(2)]
        for cp in sends:
            cp.start()
        load_wa.wait()
        acc = jnp.zeros((N_DEV, n), F32)
        for k in range(kc):
            ck = packs_ref[:, k, :]
            sk = (ck * jax.nn.sigmoid(ck)).astype(BF16)
            acc = acc + jnp.dot(sk, wa_scr[k * LANES:(k + 1) * LANES, :].astype(BF16), preferred_element_type=F32)
        mod_scr[...] = acc + ba_ref[...]
        _gather_all(mod_scr, mod_ref, s2, r2, l2)
        for j in range(N_CHIP):
            mine_ref[:, j * n:(j + 1) * n] = mod_ref[2 * j, pl.ds(4 * x + 2 * y + c, 1), :]
        passed = []
        for q in range(2):
            copy(q, block_of(q), rows_of(c), True).wait_recv()
            for cp in (relay(q, block_of(q)), copy(q, block_of(q), rows_of(c), False)):
                cp.start()
                passed.append(cp)
        for q in range(2):
            relay(q, block_of(2)).wait_recv()
        fw = copy(2, block_of(2), rows_of(c), False)
        fw.start()
        for p in range(3):
            copy(p, block_of(p), rows_of(1 - c), False).wait_recv()
        for cp in sends + passed + [fw]:
            cp.wait_send()

    return pl.pallas_call(
        body, name="startup",
        out_shape=[SDS((N_DEV,) + pack.shape, F32), SDS((1, N_CHIP * n), F32), SDS(w_buf.shape, w_buf.dtype)],
        in_specs=[VMEM_SPEC, ANY, VMEM_SPEC, ANY], out_specs=[VMEM_SPEC, VMEM_SPEC, ANY],
        input_output_aliases={3: 2},
        scratch_shapes=[pltpu.VMEM((d, n), F32), pltpu.VMEM((N_DEV, n), F32), pltpu.VMEM((N_DEV, N_DEV, n), F32)]
        + _small_sems() + _small_sems()
        + [pltpu.SemaphoreType.DMA((3,))] * 4 + [pltpu.SemaphoreType.DMA((2,))] * 2 + [pltpu.SemaphoreType.DMA],
        compiler_params=pltpu.CompilerParams(vmem_limit_bytes=VMEM_LIMIT),
    )(pack, w_ada_sh, b_ada_sh, w_buf)


class _Exchange:
    def __init__(self, operands, out_shape, in_place, n_sems, copies):
        self.operands, self.out_shape, self.in_place, self.n_sems, self.copies = (
            list(operands), list(out_shape), in_place, n_sems, copies)

    def sems(self):
        return [pltpu.SemaphoreType.DMA((self.n_sems,)), pltpu.SemaphoreType.DMA((self.n_sems,))]


def _x_gather_ici(bufs):
    def copies(ins, outs, send_sems, recv_sems):
        x, y, c = _mesh_pos()
        chips = _other_chips(x, y)
        out = []
        for w in range(len(outs)):
            half = outs[w].shape[1] // 2
            rows = pl.ds(pl.multiple_of(c * half, SUBLANES_BF16), half)
            for p in range(3):
                out.append(pltpu.make_async_remote_copy(
                    src_ref=outs[w].at[2 * x + y, rows], dst_ref=outs[w].at[2 * x + y, rows],
                    send_sem=send_sems.at[w * 3 + p], recv_sem=recv_sems.at[w * 3 + p],
                    device_id=(*chips[p], c), device_id_type=MESH))
        return out

    return _Exchange(bufs, [SDS(b.shape, b.dtype) for b in bufs], True, 3 * len(bufs), copies)


def _x_gather_d2d(bufs):
    def copies(ins, outs, send_sems, recv_sems):
        x, y, c = _mesh_pos()
        chips = _other_chips(x, y)
        out = []
        for w in range(len(outs)):
            half = outs[w].shape[1] // 2
            rows = pl.ds(pl.multiple_of(c * half, SUBLANES_BF16), half)
            for p in range(3):
                block = 2 * chips[p][0] + chips[p][1]
                out.append(pltpu.make_async_remote_copy(
                    src_ref=outs[w].at[block, rows], dst_ref=outs[w].at[block, rows],
                    send_sem=send_sems.at[w * 3 + p], recv_sem=recv_sems.at[w * 3 + p],
                    device_id=(x, y, 1 - c), device_id_type=MESH))
        return out

    return _Exchange(bufs, [SDS(b.shape, b.dtype) for b in bufs], True, 3 * len(bufs), copies)


N_REMOTE = 6


def _x_reduce(grads32, grads16):
    n_w = len(grads32)

    def copies(ins, outs, send_sems, recv_sems):
        g32, g16 = ins[:n_w], ins[n_w:]
        from_sib, from_far = outs[:n_w], outs[n_w:]
        x, y, c = _mesh_pos()
        chips = _other_chips(x, y)
        out = []
        for w in range(n_w):
            half = g32[w].shape[1] // 2
            k0 = w * (N_REMOTE + 1)
            out.append(pltpu.make_async_remote_copy(
                src_ref=g32[w].at[2 * x + y, pl.ds(pl.multiple_of((1 - c) * half, SUBLANES_BF16), half), :],
                dst_ref=from_sib[w], send_sem=send_sems.at[k0], recv_sem=recv_sems.at[k0],
                device_id=(x, y, 1 - c), device_id_type=MESH))
            for p in range(3):
                for f in range(2):
                    tc = c if f == 0 else 1 - c
                    k = 2 * p + f
                    out.append(pltpu.make_async_remote_copy(
                        src_ref=g16[w].at[2 * chips[p][0] + chips[p][1],
                                          pl.ds(pl.multiple_of(tc * half, SUBLANES_BF16), half), :],
                        dst_ref=from_far[w].at[k], send_sem=send_sems.at[k0 + 1 + k], recv_sem=recv_sems.at[k0 + 1 + k],
                        device_id=(*chips[p], tc), device_id_type=MESH))
        return out

    shapes = ([SDS((g.shape[1] // 2, g.shape[2]), g.dtype) for g in grads32]
              + [SDS((N_REMOTE, g.shape[1] // 2, g.shape[2]), g.dtype) for g in grads16])
    return _Exchange(list(grads32) + list(grads16), shapes, False, (N_REMOTE + 1) * n_w, copies)


def _x_pair_exchange(fulls):
    def copies(ins, outs, send_sems, recv_sems):
        x, y, c = _mesh_pos()
        out = []
        for w in range(len(outs)):
            half = outs[w].shape[0] // 2
            rows = pl.ds(pl.multiple_of(c * half, 8), half)
            out.append(pltpu.make_async_remote_copy(
                src_ref=outs[w].at[rows], dst_ref=outs[w].at[rows], send_sem=send_sems.at[w],
                recv_sem=recv_sems.at[w], device_id=(x, y, 1 - c), device_id_type=MESH))
        return out

    return _Exchange(fulls, [SDS(f.shape, f.dtype) for f in fulls], True, len(fulls), copies)


def _pallas(body, *, name, grid, in_specs, out_specs, out_shape, args, scratch=(), sem=None, ride=None):
    single = not isinstance(out_specs, (list, tuple))
    out_specs_l = [out_specs] if single else list(out_specs)
    out_shape_l = [out_shape] if single else list(out_shape)
    n_in, n_out, n_scr = len(in_specs), len(out_specs_l), len(scratch)
    if ride is None:
        res = pl.pallas_call(body, name=name, grid=grid, in_specs=list(in_specs), out_specs=out_specs,
                             out_shape=out_shape, scratch_shapes=list(scratch), compiler_params=_params(*sem))(*args)
        return res, None
    n_x, n_xo = len(ride.operands), len(ride.out_shape)

    def full_body(*refs):
        ins, x_ins = refs[:n_in], refs[n_in:n_in + n_x]
        outs = refs[n_in + n_x:n_in + n_x + n_out]
        x_outs = refs[n_in + n_x + n_out:n_in + n_x + n_out + n_xo]
        rest = refs[n_in + n_x + n_out + n_xo:]
        scr, (send_sems, recv_sems) = rest[:n_scr], rest[n_scr:]
        first = functools.reduce(jnp.logical_and, [pl.program_id(a) == 0 for a in range(len(grid))])
        last = functools.reduce(jnp.logical_and, [pl.program_id(a) == grid[a] - 1 for a in range(len(grid))])

        @pl.when(first)
        def _():
            for cp in ride.copies(x_ins, x_outs, send_sems, recv_sems):
                cp.start()

        body(*ins, *outs, *scr)

        @pl.when(last)
        def _():
            for cp in ride.copies(x_ins, x_outs, send_sems, recv_sems):
                cp.wait()

    res = pl.pallas_call(
        full_body, name=name, grid=grid, in_specs=list(in_specs) + [ANY] * n_x,
        out_specs=out_specs_l + [ANY] * n_xo, out_shape=out_shape_l + ride.out_shape,
        input_output_aliases={n_in + k: n_out + k for k in range(n_x)} if ride.in_place else {},
        scratch_shapes=list(scratch) + ride.sems(),
        compiler_params=_params(*(["arbitrary"] * len(grid))))(*args, *ride.operands)
    own = res[0] if single else list(res[:n_out])
    return own, list(res[n_out:])


def _cast_into_block(pos, w, name):
    rows, cols = w.shape
    tr = _row_tile(rows, cols, mult=SUBLANES_BF16)

    def body(pos_ref, w_ref, o_ref):
        del pos_ref
        o_ref[...] = w_ref[...].astype(BF16)

    return pl.pallas_call(
        body, name=name,
        grid_spec=pltpu.PrefetchScalarGridSpec(
            num_scalar_prefetch=1, grid=(rows // tr,),
            in_specs=[pl.BlockSpec((tr, cols), lambda i, pos_ref: (i, 0))],
            out_specs=pl.BlockSpec((None, tr, cols), lambda i, pos_ref: (pos_ref[1], i, 0))),
        out_shape=SDS((N_CHIP, rows, cols), BF16), compiler_params=_params("parallel"))(pos, w)


def _sum_terms(pos, grad, from_sib, from_far, name):
    _, rows, cols = grad.shape
    half = rows // 2
    tr = _row_tile(half, cols, mult=SUBLANES_BF16)
    nblk = half // tr

    def body(pos_ref, g_ref, s_ref, r_ref, o_ref):
        del pos_ref
        acc = g_ref[...] + s_ref[...]
        for k in range(N_REMOTE):
            acc = acc + r_ref[k].astype(F32)
        o_ref[...] = acc

    return pl.pallas_call(
        body, name=name,
        grid_spec=pltpu.PrefetchScalarGridSpec(
            num_scalar_prefetch=1, grid=(nblk,),
            in_specs=[pl.BlockSpec((None, tr, cols), lambda i, pos_ref: (pos_ref[1], pos_ref[0] * nblk + i, 0)),
                      pl.BlockSpec((tr, cols), lambda i, pos_ref: (i, 0)),
                      pl.BlockSpec((N_REMOTE, tr, cols), lambda i, pos_ref: (0, i, 0))],
            out_specs=pl.BlockSpec((tr, cols), lambda i, pos_ref: (pos_ref[0] * nblk + i, 0))),
        out_shape=SDS((rows, cols), F32),
        compiler_params=_params("parallel"),
    )(pos, grad, from_sib, from_far)


def _adamw(w, g, m, v, name):
    rows, cols = w.shape
    tr = _row_tile(rows, cols)

    def body(w_ref, g_ref, m_ref, v_ref, go_ref, d_ref, nm_ref, nv_ref):
        go_ref[...] = g_ref[...]
        _adamw_update(w_ref, g_ref, m_ref, v_ref, d_ref, nm_ref, nv_ref)

    spec = pl.BlockSpec((tr, cols), lambda i: (i, 0))
    return pl.pallas_call(body, name=name, grid=(rows // tr,), in_specs=[spec] * 4, out_specs=[spec] * 4,
                          out_shape=[SDS((rows, cols), F32)] * 4, compiler_params=_params("parallel"))(w, g, m, v)


def _adamw_update(w_ref, g_ref, m_ref, v_ref, d_ref, nm_ref, nv_ref):
    gg = g_ref[...]
    nm = ADAM_B1 * m_ref[...] + (1.0 - ADAM_B1) * gg
    nv = ADAM_B2 * v_ref[...] + (1.0 - ADAM_B2) * (gg * gg)
    m_hat = nm / (1.0 - ADAM_B1 ** ADAM_STEP)
    v_hat = nv / (1.0 - ADAM_B2 ** ADAM_STEP)
    d_ref[...] = -ADAM_LR * (m_hat / (jnp.sqrt(v_hat) + ADAM_EPS) + ADAM_WD * w_ref[...])
    nm_ref[...] = nm
    nv_ref[...] = nv


def _adamw_small(params):
    n_p = len(params)

    def body(*refs):
        ins, outs = refs[:4 * n_p], refs[4 * n_p:]
        for k in range(n_p):
            _adamw_update(*ins[4 * k:4 * k + 4], *outs[3 * k:3 * k + 3])

    flat = [a for tup in params for a in tup]
    res = pl.pallas_call(
        body, name="adamw_small", in_specs=[VMEM_SPEC] * (4 * n_p), out_specs=[VMEM_SPEC] * (3 * n_p),
        out_shape=[SDS(tup[0].shape, F32) for tup in params for _ in range(3)])(*flat)
    return [res[3 * k:3 * k + 3] for k in range(n_p)]


def _pack_sum(gathered):
    _, rows, cols = gathered.shape

    def body(g_ref, o_ref):
        acc = g_ref[0]
        for d in range(1, N_DEV):
            acc = acc + g_ref[d]
        o_ref[...] = acc

    return pl.pallas_call(body, name="pack_sum", in_specs=[VMEM_SPEC], out_specs=VMEM_SPEC,
                          out_shape=SDS((rows, cols), F32))(gathered)


def _ada_wgrad(c_all, dmod_sh):
    d = c_all.shape[1]
    n = dmod_sh.shape[1]
    tn = 512

    def body(c_ref, g_ref, o_ref):
        cc = c_ref[...]
        s = cc * jax.nn.sigmoid(cc)
        o_ref[...] = lax.dot_general(s, g_ref[...], (((0,), (0,)), ((), ())), preferred_element_type=F32,
                                     precision=lax.Precision.HIGHEST)

    return pl.pallas_call(
        body, name="ada_wgrad", grid=(n // tn,),
        in_specs=[pl.BlockSpec((N_DEV, d), lambda j: (0, 0)), pl.BlockSpec((N_DEV, tn), lambda j: (0, j))],
        out_specs=pl.BlockSpec((d, tn), lambda j: (0, j)),
        out_shape=SDS((d, n), F32), compiler_params=_params("parallel"))(c_all, dmod_sh)


def _rms(xf):
    return lax.rsqrt(jnp.mean(xf * xf, axis=-1, keepdims=True) + EPS)


def _in_proj(x, g, sc, sh, wt, b, tm, tn, ride=None):
    t, d = x.shape
    n = wt.shape[0]

    def body(x_ref, g_ref, sc_ref, sh_ref, w_ref, b_ref, z_ref, h_ref):
        @pl.when(pl.program_id(1) == 0)
        def _():
            xf = x_ref[...]
            h_ref[...] = ((xf * _rms(xf) * g_ref[...]) * (1.0 + sc_ref[...]) + sh_ref[...]).astype(BF16)

        acc = lax.dot_general(h_ref[...], w_ref[...], (((1,), (1,)), ((), ())), preferred_element_type=F32)
        z_ref[...] = (acc + b_ref[...]).astype(BF16)

    row = pl.BlockSpec((tm, d), lambda i, j: (i, 0))
    vecs, vec_specs = zip(*[_vec(v, d) for v in (g, sc, sh)])
    return _pallas(
        body, name="in_proj", grid=(t // tm, n // tn),
        in_specs=[row, *vec_specs, pl.BlockSpec((tn, d), lambda i, j: (j, 0)),
                  pl.BlockSpec((1, tn), lambda i, j: (0, j))],
        out_specs=[pl.BlockSpec((tm, tn), lambda i, j: (i, j)), row],
        out_shape=[SDS((t, n), BF16), SDS((t, d), BF16)], args=(x, *vecs, wt, b),
        sem=("parallel", "arbitrary"), ride=ride)


def _segments(d, kvw2):
    o = d + kvw2
    names = ("cb", "cc", "cx", "ga", "gc")
    seg = {nm: slice(o + k * d, o + (k + 1) * d) for k, nm in enumerate(names)}
    seg["q"], seg["kv"] = slice(0, d), slice(d, o)
    return seg


def _attn_masks():
    rows = 4 * BLOCK
    r = lax.broadcasted_iota(jnp.int32, (rows, 2 * BLOCK), 0) & (BLOCK - 1)
    col = lax.broadcasted_iota(jnp.int32, (rows, 2 * BLOCK), 1)
    return (col > r) & (col <= r + BLOCK), col


def _kv_variants(kv, n_kv_w):
    assert n_kv_w == LANES
    kb, vb = kv[:, :LANES] * (HEAD_DIM ** -0.5), kv[:, LANES:]
    kr, vr = pltpu.roll(kb, HEAD_DIM, 1), pltpu.roll(vb, HEAD_DIM, 1)
    lane = lax.broadcasted_iota(jnp.int32, kb.shape, 1)
    lo = lane < HEAD_DIM
    zero = jnp.zeros_like(kb)
    k_eff = [[None, None], [None, None]]
    v_eff = [[None, None], [None, None]]
    for h in range(2):
        for e in range(2):
            ksrc, vsrc = (kb, vb) if e == h else (kr, vr)
            keep = lo if e == 0 else jnp.logical_not(lo)
            k_eff[h][e] = jnp.where(keep, ksrc, zero)
            v_eff[h][e] = jnp.where(keep, vsrc, zero)
    return k_eff, v_eff


def _sink_column(sinks_ref, h, e):
    rowblk = lax.broadcasted_iota(jnp.int32, (4 * BLOCK, 1), 0) // BLOCK
    col = jnp.zeros((4 * BLOCK, 1), F32)
    for j in range(4):
        col = jnp.where(rowblk == j, sinks_ref[0, GROUP * h + 2 * j + e], col)
    return col


def _softmax_sink(s, valid, sink):
    s = jnp.where(valid, s, -jnp.inf)
    m = jnp.maximum(jnp.max(s, axis=-1, keepdims=True), sink)
    p = jnp.exp(s - m)
    psink = jnp.exp(sink - m)
    den = jnp.sum(p, axis=-1, keepdims=True) + psink
    inv = 1.0 / den
    return p * inv, psink * inv


def _shift_down(a, s, prev):
    rows = a.shape[0]
    out = pltpu.roll(a, s, 0)
    row = lax.broadcasted_iota(jnp.int32, a.shape, 0)
    for t in range(s):
        out = jnp.where(row == t, prev[SUBLANES_BF16 - s + t:SUBLANES_BF16 - s + t + 1, :], out)
    del rows
    return out


def _shift_up(a, s, nxt):
    rows = a.shape[0]
    out = pltpu.roll(a, rows - s, 0)
    row = lax.broadcasted_iota(jnp.int32, a.shape, 0)
    for t in range(s):
        out = jnp.where(row == rows - s + t, nxt[t:t + 1, :], out)
    return out


def _stack_pairs(ref, h, rows=slice(None)):
    return jnp.concatenate([ref[rows, (4 * h + j) * LANES:(4 * h + j + 1) * LANES] for j in range(4)], axis=0)


FWD_BLOCKS = 4


def _mixer_fwd(z, sinks, conv_w, d, ride=None):
    t, zw = z.shape
    kvw2 = zw - 6 * d
    tq = FWD_BLOCKS * BLOCK
    halo = tq // SUBLANES_BF16
    seg = _segments(d, kvw2)

    def body(z_ref, kvp_ref, prev_ref, sinks_ref, cw_ref, attn_ref, merged_ref):
        n = pl.program_id(0)
        band, col = _attn_masks()
        for b in range(FWD_BLOCKS):
            rows = slice(b * BLOCK, (b + 1) * BLOCK)
            before = slice((b - 1) * BLOCK, b * BLOCK)
            kv_prev = kvp_ref[...] if b == 0 else z_ref[before, seg["kv"]]
            kv = jnp.concatenate([kv_prev, z_ref[rows, seg["kv"]]], axis=0)
            k_eff, v_eff = _kv_variants(kv, kvw2 // 2)
            valid = band & ((n > 0) | (col >= BLOCK)) if b == 0 else band
            for h in range(2):
                q4 = _stack_pairs(z_ref, h, rows)
                o4 = jnp.zeros((4 * BLOCK, LANES), F32)
                for e in range(2):
                    s = lax.dot_general(q4, k_eff[h][e], (((1,), (1,)), ((), ())), preferred_element_type=F32)
                    p, _ = _softmax_sink(s, valid, _sink_column(sinks_ref, h, e))
                    o4 = o4 + jnp.dot(p.astype(BF16), v_eff[h][e], preferred_element_type=F32)
                for j in range(4):
                    attn_ref[rows, (4 * h + j) * LANES:(4 * h + j + 1) * LANES] = (
                        o4[j * BLOCK:(j + 1) * BLOCK].astype(BF16))
            cb = z_ref[rows, seg["cb"]].astype(F32)
            p_in = z_ref[rows, seg["cc"]].astype(F32) * z_ref[rows, seg["cx"]].astype(F32)
            if b == 0:
                prev = jnp.where(n > 0, prev_ref[:, seg["cc"]].astype(F32) * prev_ref[:, seg["cx"]].astype(F32), 0.0)
            else:
                tail = slice(b * BLOCK - SUBLANES_BF16, b * BLOCK)
                prev = z_ref[tail, seg["cc"]].astype(F32) * z_ref[tail, seg["cx"]].astype(F32)
            cconv = (cw_ref[0:1, :] * _shift_down(p_in, 2, prev) + cw_ref[1:2, :] * _shift_down(p_in, 1, prev)
                     + cw_ref[2:3, :] * p_in)
            sa = jax.nn.sigmoid(z_ref[rows, seg["ga"]].astype(F32))
            sg = jax.nn.sigmoid(z_ref[rows, seg["gc"]].astype(F32))
            merged_ref[rows, :] = (sa * attn_ref[rows, :].astype(F32) + sg * (cb * cconv)).astype(BF16)

    blk = pl.BlockSpec((tq, d), lambda n: (n, 0))
    return _pallas(
        body, name="mixer_fwd", grid=(t // tq,),
        in_specs=[pl.BlockSpec((tq, zw), lambda n: (n, 0)),
                  pl.BlockSpec((BLOCK, kvw2), lambda n: (jnp.maximum(n * FWD_BLOCKS - 1, 0), d // kvw2)),
                  pl.BlockSpec((SUBLANES_BF16, zw), lambda n: (jnp.maximum(n * halo - 1, 0), 0)),
                  SMEM_SPEC, pl.BlockSpec((3, d), lambda n: (0, 0))],
        out_specs=[blk, blk],
        out_shape=[SDS((t, d), BF16), SDS((t, d), BF16)],
        args=(z, z, z, sinks, conv_w), sem=("parallel",), ride=ride)


def _out_proj_fwd(merged, w_out, x, ga1, g_ffn, sc2, sh2, tm):
    t, d = x.shape

    def body(m_ref, w_ref, x_ref, ga_ref, g_ref, sc_ref, sh_ref, y_ref, x1_ref, h_ref):
        y = jnp.dot(m_ref[...], w_ref[...], preferred_element_type=F32)
        x1 = x_ref[...] + ga_ref[...] * y
        y_ref[...] = y.astype(BF16)
        x1_ref[...] = x1
        h_ref[...] = ((x1 * _rms(x1) * g_ref[...]) * (1.0 + sc_ref[...]) + sh_ref[...]).astype(BF16)

    row = pl.BlockSpec((tm, d), lambda i: (i, 0))
    vecs, vec_specs = zip(*[_vec(v, d) for v in (ga1, g_ffn, sc2, sh2)])
    return pl.pallas_call(
        body, name="out_proj_fwd", grid=(t // tm,),
        in_specs=[row, pl.BlockSpec((d, d), lambda i: (0, 0)), row, *vec_specs],
        out_specs=[row, row, row],
        out_shape=[SDS((t, d), BF16), SDS((t, d), F32), SDS((t, d), BF16)],
        compiler_params=_params("parallel"))(merged, w_out, x, *vecs)


def _ffn_in_fwd(h2, w, ff, tm, tn):
    t, d = h2.shape
    nj = ff // tn
    assert w.shape == (2 * nj, d, tn)

    def body(h_ref, wg_ref, wu_ref, gu_ref, act_ref):
        hh = h_ref[...]
        g = jnp.dot(hh, wg_ref[...], preferred_element_type=F32)
        u = jnp.dot(hh, wu_ref[...], preferred_element_type=F32)
        gu_ref[0] = g.astype(BF16)
        gu_ref[1] = u.astype(BF16)
        act_ref[...] = ((g * jax.nn.sigmoid(g)) * u).astype(BF16)

    return pl.pallas_call(
        body, name="ffn_in_fwd", grid=(nj, t // tm),
        in_specs=[pl.BlockSpec((tm, d), lambda j, i: (i, 0)), pl.BlockSpec((None, d, tn), lambda j, i: (j, 0, 0)),
                  pl.BlockSpec((None, d, tn), lambda j, i: (j + nj, 0, 0))],
        out_specs=[pl.BlockSpec((2, tm, tn), lambda j, i: (0, i, j)), pl.BlockSpec((tm, tn), lambda j, i: (i, j))],
        out_shape=[SDS((2, t, ff), BF16), SDS((t, ff), BF16)],
        compiler_params=_params("parallel", "parallel"))(h2, w, w)


def _ffn_out_loss(act, w, x1, target, ga2, g_final, tm):
    t, d = x1.shape
    ff = act.shape[1]

    def body(a_ref, w_ref, x1_ref, tg_ref, ga_ref, gf_ref, dx2_ref, dy2_ref, st_ref):
        @pl.when(pl.program_id(0) == 0)
        def _():
            st_ref[...] = jnp.zeros_like(st_ref)

        halves = [slice(k * (tm // 2), (k + 1) * (tm // 2)) for k in range(2)]
        y2s = [jnp.dot(a_ref[rows, :], w_ref[...], preferred_element_type=F32) for rows in halves]
        for rows, y2 in zip(halves, y2s):
            x2 = x1_ref[rows, :] + ga_ref[...] * y2
            r = _rms(x2)
            yn = x2 * r
            err = yn * gf_ref[...] - tg_ref[rows, :]
            loss = 0.5 * jnp.sum(jnp.mean(err * err, axis=-1, keepdims=True), axis=0, keepdims=True)
            dy = err * (1.0 / d)
            u = dy * gf_ref[...]
            dx2 = r * (u - yn * jnp.mean(u * yn, axis=-1, keepdims=True))
            dx2_ref[rows, :] = dx2
            dy2_ref[rows, :] = (ga_ref[...] * dx2).astype(BF16)
            st_ref[0:1, :] += jnp.sum(dx2 * y2, axis=0, keepdims=True)
            st_ref[1:2, :] += jnp.sum(dy * yn, axis=0, keepdims=True)
            st_ref[2:3, :] += jnp.broadcast_to(loss, (1, d))

    row = pl.BlockSpec((tm, d), lambda i: (i, 0))
    vecs, vec_specs = zip(*[_vec(v, d) for v in (ga2, g_final)])
    return pl.pallas_call(
        body, name="ffn_out_loss", grid=(t // tm,),
        in_specs=[pl.BlockSpec((tm, ff), lambda i: (i, 0)),
                  pl.BlockSpec((ff, d), lambda i: (0, 0), pipeline_mode=pl.Buffered(1)), row, row, *vec_specs],
        out_specs=[row, row, pl.BlockSpec((8, d), lambda i: (0, 0))],
        out_shape=[SDS((t, d), F32), SDS((t, d), BF16), SDS((8, d), F32)],
        compiler_params=_params("arbitrary"))(act, w, x1, target, *vecs)


def _ffn_out_bwd(dy2, w, gu, tm, tn):
    t, d = dy2.shape
    ff = w.shape[0]

    def body(dy_ref, w_ref, gu_ref, o_ref):
        dy = dy_ref[...]
        for lo in range(0, tn, 3 * LANES):
            cols = slice(lo, min(lo + 3 * LANES, tn))
            dact = lax.dot_general(dy, w_ref[cols, :], (((1,), (1,)), ((), ())), preferred_element_type=F32)
            g = gu_ref[0, :, cols].astype(F32)
            u = gu_ref[1, :, cols].astype(F32)
            sg = jax.nn.sigmoid(g)
            a = dact * sg
            du = a * g
            o_ref[0, :, cols] = (u * (a + du * (1.0 - sg))).astype(BF16)
            o_ref[1, :, cols] = du.astype(BF16)

    gu_spec = pl.BlockSpec((2, tm, tn), lambda j, i: (0, i, j))
    return pl.pallas_call(
        body, name="ffn_out_bwd", grid=(ff // tn, t // tm),
        in_specs=[pl.BlockSpec((tm, d), lambda j, i: (i, 0)), pl.BlockSpec((tn, d), lambda j, i: (j, 0)), gu_spec],
        out_specs=gu_spec, out_shape=SDS((2, t, ff), BF16),
        compiler_params=_params("parallel", "parallel"))(dy2, w, gu)


def _wgrad(a, b, a_spec, b_spec, out_spec, out_shape, grid, name, ride=None):
    def body(a_ref, b_ref, o_ref, o16_ref):
        k = pl.program_id(len(grid) - 1)

        @pl.when(k == 0)
        def _():
            o_ref[...] = jnp.zeros_like(o_ref)

        o_ref[...] += lax.dot_general(a_ref[...], b_ref[...], (((0,), (0,)), ((), ())), preferred_element_type=F32)

        @pl.when(k == grid[-1] - 1)
        def _():
            o16_ref[...] = o_ref[...].astype(BF16)

    return _pallas(
        body, name=name, grid=grid, in_specs=[a_spec, b_spec], out_specs=[out_spec, out_spec],
        out_shape=[out_shape, SDS(out_shape.shape, BF16)], args=(a, b),
        sem=["parallel"] * (len(grid) - 1) + ["arbitrary"], ride=ride)


def _ffn_in_bwd(dgu, w, x1, dx2, y1, g_ffn, sc2, ga1, tm):
    t, d = x1.shape
    ff = dgu.shape[2]
    n_sh, _, sw = w.shape
    per = ff // sw
    nt = (((1,), (1,)), ((), ()))

    def body(a_ref, w_ref, x1_ref, dx2_ref, y1_ref, g_ref, sc_ref, ga_ref, dx1_ref, dy1_ref, st_ref):
        @pl.when(pl.program_id(0) == 0)
        def _():
            st_ref[...] = jnp.zeros_like(st_ref)

        dh = None
        for j in range(n_sh):
            part = lax.dot_general(a_ref[j // per, :, (j % per) * sw:(j % per + 1) * sw], w_ref[j], nt,
                                   preferred_element_type=F32)
            dh = part if dh is None else dh + part
        x1 = x1_ref[...]
        r = _rms(x1)
        xn = x1 * r
        g = g_ref[...]
        dn = dh * (1.0 + sc_ref[...])
        u = dn * g
        dx1 = dx2_ref[...] + r * (u - xn * jnp.mean(u * xn, axis=-1, keepdims=True))
        dx1_ref[...] = dx1
        dy1_ref[...] = (ga_ref[...] * dx1).astype(BF16)
        st_ref[0:1, :] += jnp.sum(dh, axis=0, keepdims=True)
        st_ref[1:2, :] += jnp.sum(dh * (xn * g), axis=0, keepdims=True)
        st_ref[2:3, :] += jnp.sum(dn * xn, axis=0, keepdims=True)
        st_ref[3:4, :] += jnp.sum(dx1 * y1_ref[...].astype(F32), axis=0, keepdims=True)

    row = pl.BlockSpec((tm, d), lambda i: (i, 0))
    vecs, vec_specs = zip(*[_vec(v, d) for v in (g_ffn, sc2, ga1)])
    return pl.pallas_call(
        body, name="ffn_in_bwd", grid=(t // tm,),
        in_specs=[pl.BlockSpec((2, tm, ff), lambda i: (0, i, 0)),
                  pl.BlockSpec((n_sh, d, sw), lambda i: (0, 0, 0), pipeline_mode=pl.Buffered(1)),
                  row, row, row, *vec_specs],
        out_specs=[row, row, pl.BlockSpec((8, d), lambda i: (0, 0))],
        out_shape=[SDS((t, d), F32), SDS((t, d), BF16), SDS((8, d), F32)],
        compiler_params=_params("arbitrary"))(dgu, w, x1, dx2, y1, *vecs)


def _out_proj_bwd(dy1, w_out, tm, ride=None):
    t, d = dy1.shape

    def body(dy_ref, w_ref, o_ref):
        o_ref[...] = lax.dot_general(dy_ref[...], w_ref[...], (((1,), (1,)), ((), ())),
                                     preferred_element_type=F32).astype(BF16)

    row = pl.BlockSpec((tm, d), lambda i: (i, 0))
    return _pallas(body, name="out_proj_bwd", grid=(t // tm,),
                   in_specs=[row, pl.BlockSpec((d, d), lambda i: (0, 0))], out_specs=row,
                   out_shape=SDS((t, d), BF16), args=(dy1, w_out), sem=("parallel",), ride=ride)


BWD_BLOCKS = 2


def _mixer_bwd(z, dmerged, attn, sinks, conv_w, d, ride=None):
    t, zw = z.shape
    kvw2 = zw - 6 * d
    tq = BWD_BLOCKS * BLOCK
    steps = t // tq
    halo = tq // SUBLANES_BF16
    last_halo = t // SUBLANES_BF16 - 1
    scale = HEAD_DIM ** -0.5
    seg = _segments(d, kvw2)

    def body(z_ref, kvp_ref, prev_ref, next_ref, dm_ref, dmn_ref, attn_ref, sinks_ref, cw_ref,
             dz_ref, dkv_ref, db_ref, dbkv_ref, dcw_ref, dsk_ref, carry_ref):
        n = pl.program_id(0)

        @pl.when(n == 0)
        def _():
            carry_ref[...] = jnp.zeros_like(carry_ref)
            db_ref[...] = jnp.zeros_like(db_ref)
            dbkv_ref[...] = jnp.zeros_like(dbkv_ref)
            dcw_ref[...] = jnp.zeros_like(dcw_ref)
            dsk_ref[...] = jnp.zeros_like(dsk_ref)

        def one_block(b, pending):
            rows = slice(b * BLOCK, (b + 1) * BLOCK)
            before = slice((b - 1) * BLOCK, b * BLOCK)
            dm = dm_ref[rows, :].astype(F32)
            sa = jax.nn.sigmoid(z_ref[rows, seg["ga"]].astype(F32))
            dga = dm * attn_ref[rows, :].astype(F32) * sa * (1.0 - sa)
            dz_ref[rows, seg["ga"]] = dga.astype(BF16)
            db_ref[0:1, seg["ga"]] += jnp.sum(dga, axis=0, keepdims=True)
            dattn = (dm * sa).astype(BF16)

            kv_prev = kvp_ref[...] if b == 0 else z_ref[before, seg["kv"]]
            kv = jnp.concatenate([kv_prev, z_ref[rows, seg["kv"]]], axis=0)
            k_eff, v_eff = _kv_variants(kv, kvw2 // 2)
            band, col = _attn_masks()
            valid = band & ((n > 0) | (col >= BLOCK)) if b == 0 else band
            lane_lo = lax.broadcasted_iota(jnp.int32, (2 * BLOCK, LANES), 1) < HEAD_DIM
            sink_lane = lax.broadcasted_iota(jnp.int32, (1, LANES), 1)
            rowblk = lax.broadcasted_iota(jnp.int32, (4 * BLOCK, 1), 0) // BLOCK
            dk_acc = [jnp.zeros((2 * BLOCK, LANES), F32), jnp.zeros((2 * BLOCK, LANES), F32)]
            dv_acc = [jnp.zeros((2 * BLOCK, LANES), F32), jnp.zeros((2 * BLOCK, LANES), F32)]
            dsink = jnp.zeros((1, LANES), F32)
            for h in range(2):
                q4 = _stack_pairs(z_ref, h, rows)
                do4 = jnp.concatenate([dattn[:, (4 * h + j) * LANES:(4 * h + j + 1) * LANES] for j in range(4)],
                                      axis=0)
                dq4 = jnp.zeros((4 * BLOCK, LANES), F32)
                for e in range(2):
                    s = lax.dot_general(q4, k_eff[h][e], (((1,), (1,)), ((), ())), preferred_element_type=F32)
                    p, psink = _softmax_sink(s, valid, _sink_column(sinks_ref, h, e))
                    dp = lax.dot_general(do4, v_eff[h][e], (((1,), (1,)), ((), ())), preferred_element_type=F32)
                    delta = jnp.sum(p * dp, axis=-1, keepdims=True)
                    ds = (p * (dp - delta)).astype(BF16)
                    dq4 = dq4 + jnp.dot(ds, k_eff[h][e], preferred_element_type=F32)
                    dk = lax.dot_general(q4, ds, (((0,), (0,)), ((), ())), preferred_element_type=F32).T
                    dv = lax.dot_general(do4, p.astype(BF16), (((0,), (0,)), ((), ())), preferred_element_type=F32).T
                    keep = lane_lo if e == 0 else jnp.logical_not(lane_lo)
                    slot = 0 if e == h else 1
                    dk_acc[slot] = dk_acc[slot] + jnp.where(keep, dk, 0.0)
                    dv_acc[slot] = dv_acc[slot] + jnp.where(keep, dv, 0.0)
                    dsk = -(psink * delta)
                    for j in range(4):
                        tot = jnp.sum(jnp.where(rowblk == j, dsk, 0.0), axis=0, keepdims=True)
                        dsink = dsink + jnp.where(sink_lane == GROUP * h + 2 * j + e, tot, 0.0)
                for j in range(4):
                    cols = slice((4 * h + j) * LANES, (4 * h + j + 1) * LANES)
                    dqj = dq4[j * BLOCK:(j + 1) * BLOCK]
                    dz_ref[rows, cols] = dqj.astype(BF16)
                    db_ref[0:1, cols] += jnp.sum(dqj, axis=0, keepdims=True)
            dsk_ref[0:1, :] += dsink
            dkv_new = jnp.concatenate([(dk_acc[0] + pltpu.roll(dk_acc[1], HEAD_DIM, 1)) * scale,
                                       dv_acc[0] + pltpu.roll(dv_acc[1], HEAD_DIM, 1)], axis=1)
            done = pending + dkv_new[:BLOCK]
            dkv_ref[rows, :] = done.astype(BF16)
            dbkv_ref[0:1, :] += jnp.sum(done, axis=0, keepdims=True)

            cb = z_ref[rows, seg["cb"]].astype(F32)
            cc = z_ref[rows, seg["cc"]].astype(F32)
            cx = z_ref[rows, seg["cx"]].astype(F32)
            sg = jax.nn.sigmoid(z_ref[rows, seg["gc"]].astype(F32))
            p_in = cc * cx
            if b == 0:
                prev = jnp.where(n > 0, prev_ref[:, seg["cc"]].astype(F32) * prev_ref[:, seg["cx"]].astype(F32), 0.0)
            else:
                tail = slice(b * BLOCK - SUBLANES_BF16, b * BLOCK)
                prev = z_ref[tail, seg["cc"]].astype(F32) * z_ref[tail, seg["cx"]].astype(F32)
            p_m1 = _shift_down(p_in, 1, prev)
            p_m2 = _shift_down(p_in, 2, prev)
            w0, w1, w2 = cw_ref[0:1, :], cw_ref[1:2, :], cw_ref[2:3, :]
            cconv = w0 * p_m2 + w1 * p_m1 + w2 * p_in
            dconv = dm * sg
            dgc = dm * (cb * cconv) * sg * (1.0 - sg)
            dcb = dconv * cconv
            dcc_t = dconv * cb
            if b == BWD_BLOCKS - 1:
                nxt = jnp.where(n < steps - 1,
                                dmn_ref[...].astype(F32) * jax.nn.sigmoid(next_ref[:, seg["gc"]].astype(F32))
                                * next_ref[:, seg["cb"]].astype(F32), 0.0)
            else:
                head = slice((b + 1) * BLOCK, (b + 1) * BLOCK + SUBLANES_BF16)
                nxt = (dm_ref[head, :].astype(F32) * jax.nn.sigmoid(z_ref[head, seg["gc"]].astype(F32))
                       * z_ref[head, seg["cb"]].astype(F32))
            dpin = w2 * dcc_t + w1 * _shift_up(dcc_t, 1, nxt) + w0 * _shift_up(dcc_t, 2, nxt)
            for nm, val in (("cb", dcb), ("cc", dpin * cx), ("cx", dpin * cc), ("gc", dgc)):
                dz_ref[rows, seg[nm]] = val.astype(BF16)
                db_ref[0:1, seg[nm]] += jnp.sum(val, axis=0, keepdims=True)
            dcw_ref[0:1, :] += jnp.sum(dcc_t * p_m2, axis=0, keepdims=True)
            dcw_ref[1:2, :] += jnp.sum(dcc_t * p_m1, axis=0, keepdims=True)
            dcw_ref[2:3, :] += jnp.sum(dcc_t * p_in, axis=0, keepdims=True)
            return dkv_new[BLOCK:]

        @pl.when(n < steps)
        def _():
            pending = carry_ref[...]
            for b in range(BWD_BLOCKS):
                pending = one_block(b, pending)
            carry_ref[...] = pending

        @pl.when(n == steps)
        def _():
            done = carry_ref[...]
            dkv_ref[:BLOCK, :] = done.astype(BF16)
            dkv_ref[BLOCK:, :] = jnp.zeros((tq - BLOCK, kvw2), BF16)
            dbkv_ref[0:1, :] += jnp.sum(done, axis=0, keepdims=True)

    def cur(n):
        return jnp.minimum(n, steps - 1)

    def after(n):
        return jnp.minimum((cur(n) + 1) * halo, last_halo)

    blk = pl.BlockSpec((tq, d), lambda n: (cur(n), 0))
    return _pallas(
        body, name="mixer_bwd", grid=(steps + 1,), ride=ride, sem=("arbitrary",),
        args=(z, z, z, z, dmerged, dmerged, attn, sinks, conv_w),
        in_specs=[pl.BlockSpec((tq, zw), lambda n: (cur(n), 0)),
                  pl.BlockSpec((BLOCK, kvw2), lambda n: (jnp.maximum(cur(n) * BWD_BLOCKS - 1, 0), d // kvw2)),
                  pl.BlockSpec((SUBLANES_BF16, zw), lambda n: (jnp.maximum(cur(n) * halo - 1, 0), 0)),
                  pl.BlockSpec((SUBLANES_BF16, zw), lambda n: (after(n), 0)),
                  blk,
                  pl.BlockSpec((SUBLANES_BF16, d), lambda n: (after(n), 0)),
                  blk, SMEM_SPEC, pl.BlockSpec((3, d), lambda n: (0, 0))],
        out_specs=[pl.BlockSpec((tq, zw), lambda n: (cur(n), 0)),
                   pl.BlockSpec((tq, kvw2), lambda n: (n, 0)),
                   pl.BlockSpec((8, zw), lambda n: (0, 0)), pl.BlockSpec((8, kvw2), lambda n: (0, 0)),
                   pl.BlockSpec((8, d), lambda n: (0, 0)), pl.BlockSpec((8, LANES), lambda n: (0, 0))],
        out_shape=[SDS((t, zw), BF16), SDS((t + tq, kvw2), BF16), SDS((8, zw), F32), SDS((8, kvw2), F32),
                   SDS((8, d), F32), SDS((8, LANES), F32)],
        scratch=[pltpu.VMEM((BLOCK, kvw2), F32)])


def _wgrad_in(dz, dkv, h1, tk, ride=None):
    t, zw = dz.shape
    d = h1.shape[1]
    kvw2 = dkv.shape[1]
    blk = d + kvw2
    assert zw % blk == 0
    tn = (((0,), (0,)), ((), ()))

    def body(a_ref, akv_ref, h_ref, o_ref, o16_ref):
        n, k = pl.program_id(0), pl.program_id(1)

        @pl.when(k == 0)
        def _():
            o_ref[...] = jnp.zeros_like(o_ref)

        @pl.when(n == 0)
        def _():
            o_ref[:d, :] += lax.dot_general(a_ref[:, :d], h_ref[...], tn, preferred_element_type=F32)
            o_ref[d:, :] += lax.dot_general(akv_ref[...], h_ref[...], tn, preferred_element_type=F32)

        @pl.when(n > 0)
        def _():
            o_ref[...] += lax.dot_general(a_ref[...], h_ref[...], tn, preferred_element_type=F32)

        @pl.when(k == t // tk - 1)
        def _():
            o16_ref[...] = o_ref[...].astype(BF16)

    out_spec = pl.BlockSpec((blk, d), lambda n, k: (n, 0))
    return _pallas(
        body, name="wgrad_in", grid=(zw // blk, t // tk),
        in_specs=[pl.BlockSpec((tk, blk), lambda n, k: (k, n)), pl.BlockSpec((tk, kvw2), lambda n, k: (k, 0)),
                  pl.BlockSpec((tk, d), lambda n, k: (k, 0))],
        out_specs=[out_spec, out_spec], out_shape=[SDS((zw, d), F32), SDS((zw, d), BF16)],
        args=(dz, dkv, h1), sem=("parallel", "arbitrary"), ride=ride)


def _in_proj_bwd(dz, dkv, wt, x, dx1, g_mix, sc1, tm, ride=None):
    t, d = x.shape
    zw = dz.shape[1]
    kvw2 = dkv.shape[1]
    rest = d + kvw2

    def body(a_ref, akv_ref, w_ref, x_ref, dx1_ref, g_ref, sc_ref, gx_ref, st_ref):
        @pl.when(pl.program_id(0) == 0)
        def _():
            st_ref[...] = jnp.zeros_like(st_ref)

        dh = (jnp.dot(a_ref[:, :d], w_ref[:d, :], preferred_element_type=F32)
              + jnp.dot(akv_ref[...], w_ref[d:rest, :], preferred_element_type=F32)
              + jnp.dot(a_ref[:, rest:], w_ref[rest:, :], preferred_element_type=F32))
        xx = x_ref[...]
        r = _rms(xx)
        xn = xx * r
        g = g_ref[...]
        dn = dh * (1.0 + sc_ref[...])
        u = dn * g
        gx_ref[...] = dx1_ref[...] + r * (u - xn * jnp.mean(u * xn, axis=-1, keepdims=True))
        st_ref[0:1, :] += jnp.sum(dh, axis=0, keepdims=True)
        st_ref[1:2, :] += jnp.sum(dh * (xn * g), axis=0, keepdims=True)
        st_ref[2:3, :] += jnp.sum(dn * xn, axis=0, keepdims=True)

    row = pl.BlockSpec((tm, d), lambda i: (i, 0))
    vecs, vec_specs = zip(*[_vec(v, d) for v in (g_mix, sc1)])
    return _pallas(
        body, name="in_proj_bwd", grid=(t // tm,),
        in_specs=[pl.BlockSpec((tm, zw), lambda i: (i, 0)), pl.BlockSpec((tm, kvw2), lambda i: (i, 0)),
                  pl.BlockSpec((zw, d), lambda i: (0, 0), pipeline_mode=pl.Buffered(1)),
                  row, row, *vec_specs],
        out_specs=[row, pl.BlockSpec((8, d), lambda i: (0, 0))],
        out_shape=[SDS((t, d), F32), SDS((8, d), F32)],
        args=(dz, dkv, wt, x, dx1, *vecs), sem=("arbitrary",), ride=ride)


def _to_lanes(v, rows=None):
    flat = v.reshape(-1)
    need = -(-flat.shape[0] // LANES)
    need = -(-need // 8) * 8 if rows is None else rows
    return jnp.pad(flat, (0, need * LANES - flat.shape[0])).reshape(need, LANES)


def kernel(x, c, w_ada, b_ada, g_mix, w_in, b_in, sinks, conv_w, w_out, g_ffn, w_ffn_in, w_ffn_out, g_final, loss_target, m_w_ada, m_b_ada, m_g_mix, m_w_in, m_b_in, m_sinks, m_conv_w, m_w_out, m_g_ffn, m_w_ffn_in, m_w_ffn_out, m_g_final, v_w_ada, v_b_ada, v_g_mix, v_w_in, v_b_in, v_sinks, v_conv_w, v_w_out, v_g_ffn, v_w_ffn_in, v_w_ffn_out, v_g_final):
    xs, tgt = x[0], loss_target[0]
    t, d = xs.shape
    zw = w_in.shape[2] * N_CHIP
    kvw2 = zw - 6 * d
    ff = w_ffn_out.shape[1] * N_CHIP
    n_mod = w_ada.shape[2] * N_CHIP // d
    mod_sh = w_ada.shape[2]
    cw_sh = conv_w.shape[2]
    assert d % (8 * LANES) == 0 and kvw2 == 2 * LANES and t % 512 == 0 and n_mod == 6
    xi, yi, ci = _mesh_pos()
    j_me = 2 * xi + yi
    b_me = 4 * xi + 2 * yi + ci
    pos = jnp.stack([ci, j_me]).astype(jnp.int32)
    tm = 512

    w_in_t, m_w_in_t, v_w_in_t = w_in[0].T, m_w_in[0].T, v_w_in[0].T
    assert d == 8 * LANES
    pack1 = jnp.concatenate([c.reshape(d // LANES, LANES), conv_w[0].reshape(-1, LANES)], axis=0)
    pack1 = jnp.pad(pack1, ((0, 16 - pack1.shape[0]), (0, 0)))
    b_ada_sh = lax.dynamic_slice(b_ada, (0, j_me * mod_sh), (1, mod_sh))
    g1, mod, w_in_g = _startup(pack1, w_ada[0], b_ada_sh, _cast_into_block(pos, w_in_t, "cast_w_in"))
    c_all = g1[:, :d // LANES, :].reshape(N_DEV, d)
    cw_rows = 3 * cw_sh // LANES
    conv_w_full = jnp.concatenate(
        [g1[2 * j, d // LANES:d // LANES + cw_rows, :].reshape(3, cw_sh) for j in range(N_CHIP)], axis=1)
    sh1, sc1, ga1, sh2, sc2, ga2 = [(mod, k) for k in range(6)]
    w_in_tf = w_in_g.reshape(zw, d)
    later = [_cast_into_block(pos, w_out[0], "cast_w_out"), _cast_into_block(pos, w_ffn_in[0], "cast_w_ffn_in"),
             _cast_into_block(pos, w_ffn_out[0], "cast_w_ffn_out")]

    (z, h1), later = _in_proj(xs, g_mix, sc1, sh1, w_in_tf, b_in, min(t, 1024), zw // 5, ride=_x_gather_ici(later))
    (attn, merged), later = _mixer_fwd(z, sinks, conv_w_full, d, ride=_x_gather_d2d(later))
    w_out_f = later[0].reshape(d, d)
    w_ffn_in_f = later[1]
    w_ffn_out_f = later[2].reshape(ff, d)
    tml = min(t, 1024)
    y1, x1, h2 = _out_proj_fwd(merged, w_out_f, xs, ga1, g_ffn, sc2, sh2, tml)
    gu, act = _ffn_in_fwd(h2, w_ffn_in_f, ff, tml, ff // 2)
    dx2, dy2, st_loss = _ffn_out_loss(act, w_ffn_out_f, x1, tgt, ga2, g_final.reshape(1, d), tml)

    dgu = _ffn_out_bwd(dy2, w_ffn_out_f, gu, tml, ff // 2)
    tk = min(t, 2048)
    dw_ffn_out, _ = _wgrad(
        act, dy2, pl.BlockSpec((tk, ff // 2), lambda m, k: (k, m)), pl.BlockSpec((tk, d), lambda m, k: (k, 0)),
        pl.BlockSpec((ff // 2, d), lambda m, k: (m, 0)), SDS((ff, d), F32), (2, t // tk), "wgrad_ffn_out")
    dx1, dy1, st_ffn = _ffn_in_bwd(dgu, w_ffn_in_f, x1, dx2, y1, g_ffn, sc2, ga1, tm)
    dw_ffn_in, _ = _wgrad(
        h2, dgu, pl.BlockSpec((tk, d), lambda n, k: (k, 0)),
        pl.BlockSpec((None, tk, ff // 2), lambda n, k: (n // 2, k, n % 2)),
        pl.BlockSpec((None, d, ff // 2), lambda n, k: (n, 0, 0)), SDS((N_CHIP, d, ff // 2), F32),
        (N_CHIP, t // tk), "wgrad_ffn_in")
    dw_out, _ = _wgrad(
        merged, dy1, pl.BlockSpec((tk, d), lambda m, k: (k, 0)), pl.BlockSpec((tk, d), lambda m, k: (k, 0)),
        pl.BlockSpec((d, d), lambda m, k: (0, 0)), SDS((d, d), F32), (1, t // tk), "wgrad_out")

    early = [[g.reshape(N_CHIP, -1, g.shape[-1]) for g in pair] for pair in (dw_out, dw_ffn_in, dw_ffn_out)]
    early_names = ["w_out", "w_ffn_in", "w_ffn_out"]
    dmerged, _ = _out_proj_bwd(dy1, w_out_f, tml)
    (dz, dkv_shifted, db_z, db_kv, dcw, dsk), terms = _mixer_bwd(
        z, dmerged, attn, sinks, conv_w_full, d, ride=_x_reduce([e[0] for e in early], [e[1] for e in early]))
    dkv = dkv_shifted[BLOCK:BLOCK + t]
    fulls = [_sum_terms(pos, e[0], s, r, "sum_terms_" + nm)
             for e, s, r, nm in zip(early, terms[:3], terms[3:], early_names)]
    dw_in_t, (g_w_out, g_w_ffn_in, g_w_ffn_out) = _wgrad_in(dz, dkv, h1, tk, ride=_x_pair_exchange(fulls))
    dw_in_t = [g.reshape(N_CHIP, zw // N_CHIP, d) for g in dw_in_t]

    (grad_x, st_in), (from_sib, from_far) = _in_proj_bwd(dz, dkv, w_in_tf, xs, dx1, g_mix, sc1, tm,
                                                         ride=_x_reduce([dw_in_t[0]], [dw_in_t[1]]))
    g_w_in_half = _sum_terms(pos, dw_in_t[0], from_sib, from_far, "sum_terms_w_in")

    dmod = jnp.concatenate([st_in[0:1], st_in[1:2], st_ffn[3:4], st_ffn[0:1], st_ffn[1:2], st_loss[0:1]], axis=1)
    db_in = jnp.concatenate([db_z[0:1, :d], db_kv[0:1], db_z[0:1, d + kvw2:]], axis=1)
    seg = [dmod, st_in[2:3], db_in, dsk[0:1], dcw[0:3].reshape(1, 3 * d), st_ffn[2:3], st_loss[1:2],
           st_loss[2:3, :LANES]]
    sizes = [s.shape[1] for s in seg]
    pack2 = _to_lanes(jnp.concatenate(seg, axis=1))
    packs, g_w_in_t = _tail_exchange(pack2, g_w_in_half)
    tot = _pack_sum(packs).reshape(-1)
    offs = [sum(sizes[:k]) for k in range(len(sizes))]
    gb_ada, gg_mix, gb_in, gsinks, gcw, gg_ffn, gg_final, loss_v = [tot[o:o + s] for o, s in zip(offs, sizes)]
    loss = loss_v[0]
    gsinks = gsinks[:sinks.shape[1]]
    gcw_sh = lax.dynamic_slice(gcw.reshape(3, d), (0, j_me * cw_sh), (3, cw_sh))

    dmod_all = packs[:, :n_mod * d // LANES, :].reshape(N_DEV, n_mod * d)
    g_w_ada = _ada_wgrad(c_all, lax.dynamic_slice(dmod_all, (0, j_me * mod_sh), (N_DEV, mod_sh)))

    out_g, out_d, out_m, out_v = {}, {}, {}, {}
    big = {"w_ada": (w_ada[0], g_w_ada, m_w_ada[0], v_w_ada[0]),
           "w_out": (w_out[0], g_w_out, m_w_out[0], v_w_out[0]),
           "w_ffn_in": (w_ffn_in[0], g_w_ffn_in, m_w_ffn_in[0], v_w_ffn_in[0]),
           "w_ffn_out": (w_ffn_out[0], g_w_ffn_out, m_w_ffn_out[0], v_w_ffn_out[0])}
    for nm, (w, g, m, v) in big.items():
        out_g[nm], out_d[nm], out_m[nm], out_v[nm] = [o[None] for o in _adamw(w, g, m, v, "adamw_" + nm)]
    out_g["w_in"], out_d["w_in"], out_m["w_in"], out_v["w_in"] = [
        o.T[None] for o in _adamw(w_in_t, g_w_in_t, m_w_in_t, v_w_in_t, "adamw_w_in")]
    small = {"b_ada": (b_ada, gb_ada, m_b_ada, v_b_ada), "g_mix": (g_mix, gg_mix, m_g_mix, v_g_mix),
             "b_in": (b_in, gb_in, m_b_in, v_b_in), "sinks": (sinks, gsinks, m_sinks, v_sinks),
             "conv_w": (conv_w, gcw_sh, m_conv_w, v_conv_w), "g_ffn": (g_ffn, gg_ffn, m_g_ffn, v_g_ffn),
             "g_final": (g_final, gg_final, m_g_final, v_g_final)}
    def two_d(a):
        return a.reshape(-1, a.shape[-1])

    s_out = _adamw_small([tuple(two_d(a.reshape(w.shape)) for a in (w, g, m, v)) for w, g, m, v in small.values()])
    for (nm, (w, g, _, _)), res in zip(small.items(), s_out):
        out_g[nm] = g.reshape(w.shape)
        out_d[nm], out_m[nm], out_v[nm] = [o.reshape(w.shape) for o in res]

    order = ["w_ada", "b_ada", "g_mix", "w_in", "b_in", "sinks", "conv_w", "w_out", "g_ffn", "w_ffn_in", "w_ffn_out",
             "g_final"]
    return (loss, grad_x[None], *[out_g[k] for k in order], *[out_d[k] for k in order],
            *[out_m[k] for k in order], *[out_v[k] for k in order])
```

```python
import functools

import jax
import jax.numpy as jnp
from jax import lax
from jax.experimental import pallas as pl
from jax.experimental.pallas import tpu as pltpu

F32 = jnp.float32
BF16 = jnp.bfloat16
EPS = 1e-6
HEAD_DIM = 64
GROUP = 8
BLOCK = 128
LANES = 128
SUBLANES_BF16 = 16
N_DEV = 8
N_CHIP = 4
VMEM_LIMIT = 56 * 1024 * 1024
MESH = pl.DeviceIdType.MESH

ADAM_LR = 0.001
ADAM_B1 = 0.9
ADAM_B2 = 0.999
ADAM_EPS = 1e-08
ADAM_WD = 0.01
ADAM_STEP = 10

SDS = jax.ShapeDtypeStruct
ANY = pl.BlockSpec(memory_space=pl.ANY)
VMEM_SPEC = pl.BlockSpec(memory_space=pltpu.VMEM)
SMEM_SPEC = pl.BlockSpec(memory_space=pltpu.SMEM)


def _params(*sem):
    return pltpu.CompilerParams(dimension_semantics=sem, vmem_limit_bytes=VMEM_LIMIT)


def _vec(v, d):
    arr, k = v if isinstance(v, tuple) else (v, 0)
    return arr, pl.BlockSpec((1, d), lambda *_: (0, k))


def _mesh_pos():
    return lax.axis_index("x"), lax.axis_index("y"), lax.axis_index("c")


def _row_tile(rows, cols, itemsize=4, budget=1 << 20, mult=8):
    best = None
    for t in range(mult, rows + 1, mult):
        if rows % t == 0 and t * cols * itemsize <= budget:
            best = t
    if best is None:
        best = rows
    return best


def _gather_all(v_ref, out_ref, send_sems, recv_sems, local_sem):
    x, y, c = _mesh_pos()
    me = 4 * x + 2 * y + c
    mine = pltpu.make_async_copy(v_ref, out_ref.at[me], local_sem)
    mine.start()
    peers = []
    for k in range(1, N_DEV):
        px = 1 - x if k & 4 else x
        py = 1 - y if k & 2 else y
        pc = 1 - c if k & 1 else c
        peers.append((px, py, pc))

    def copy(k, block):
        return pltpu.make_async_remote_copy(
            src_ref=v_ref, dst_ref=out_ref.at[block], send_sem=send_sems.at[k], recv_sem=recv_sems.at[k],
            device_id=peers[k], device_id_type=MESH)

    sends = [copy(k, me) for k in range(N_DEV - 1)]
    for cp in sends:
        cp.start()
    for k, (px, py, pc) in enumerate(peers):
        copy(k, 4 * px + 2 * py + pc).wait_recv()
    for cp in sends:
        cp.wait_send()
    mine.wait()


def _small_sems():
    return [pltpu.SemaphoreType.DMA((N_DEV - 1,)), pltpu.SemaphoreType.DMA((N_DEV - 1,)), pltpu.SemaphoreType.DMA]


def _tail_exchange(pack, full):
    def body(pack_ref, full_unused, packs_ref, full_ref, s1, r1, l1, send_sem, recv_sem):
        del full_unused
        x, y, c = _mesh_pos()
        half = full_ref.shape[0] // 2
        rows = pl.ds(pl.multiple_of(c * half, 8), half)
        swap = pltpu.make_async_remote_copy(
            src_ref=full_ref.at[rows], dst_ref=full_ref.at[rows], send_sem=send_sem, recv_sem=recv_sem,
            device_id=(x, y, 1 - c), device_id_type=MESH)
        swap.start()
        _gather_all(pack_ref, packs_ref, s1, r1, l1)
        swap.wait()

    return pl.pallas_call(
        body, name="tail_exchange", out_shape=[SDS((N_DEV,) + pack.shape, pack.dtype), SDS(full.shape, full.dtype)],
        in_specs=[VMEM_SPEC, ANY], out_specs=[VMEM_SPEC, ANY], input_output_aliases={1: 1},
        scratch_shapes=_small_sems() + [pltpu.SemaphoreType.DMA, pltpu.SemaphoreType.DMA])(pack, full)


def _other_chips(x, y):
    return [(1 - x, y), (x, 1 - y), (1 - x, 1 - y)]


def _startup(pack, w_ada_sh, b_ada_sh, w_buf):
    d, n = w_ada_sh.shape
    kc = d // LANES

    def body(pack_ref, wa_hbm, ba_ref, w_in_unused, packs_ref, mine_ref, w_ref, wa_scr, mod_scr, mod_ref,
             s1, r1, l1, s2, r2, l2, send_sems, recv_sems, fsend_sems, frecv_sems, relay_send, relay_recv, wa_sem):
        del w_in_unused
        x, y, c = _mesh_pos()
        j_me = 2 * x + y
        chips = _other_chips(x, y)
        half = w_ref.shape[1] // 2

        def rows_of(which):
            return pl.ds(pl.multiple_of(which * half, SUBLANES_BF16), half)

        def copy(p, block, rows, over_ici):
            sems = (send_sems, recv_sems) if over_ici else (fsend_sems, frecv_sems)
            return pltpu.make_async_remote_copy(
                src_ref=w_ref.at[block, rows], dst_ref=w_ref.at[block, rows], send_sem=sems[0].at[p],
                recv_sem=sems[1].at[p], device_id=(*chips[p], c) if over_ici else (x, y, 1 - c), device_id_type=MESH)

        def block_of(p):
            return 2 * chips[p][0] + chips[p][1]

        def relay(q, block):
            rows = pl.ds(pl.multiple_of(c * half + q * (half // 2), SUBLANES_BF16), half // 2)
            return pltpu.make_async_remote_copy(
                src_ref=w_ref.at[block, rows], dst_ref=w_ref.at[block, rows], send_sem=relay_send.at[q],
                recv_sem=relay_recv.at[q], device_id=(*chips[1 - q], c), device_id_type=MESH)

        load_wa = pltpu.make_async_copy(wa_hbm, wa_scr, wa_sem)
        load_wa.start()
        _gather_all(pack_ref, packs_ref, s1, r1, l1)
        sends = [copy(p, j_me, rows_of(c), True) for p in range(2)]
        for cp in sends:
            cp.start()
        load_wa.wait()
        acc = jnp.zeros((N_DEV, n), F32)
        for k in range(kc):
            ck = packs_ref[:, k, :]
            sk = (ck * jax.nn.sigmoid(ck)).astype(BF16)
            acc = acc + jnp.dot(sk, wa_scr[k * LANES:(k + 1) * LANES, :].astype(BF16), preferred_element_type=F32)
        mod_scr[...] = acc + ba_ref[...]
        _gather_all(mod_scr, mod_ref, s2, r2, l2)
        for j in range(N_CHIP):
            mine_ref[:, j * n:(j + 1) * n] = mod_ref[2 * j, pl.ds(4 * x + 2 * y + c, 1), :]
        passed = []
        for q in range(2):
            copy(q, block_of(q), rows_of(c), True).wait_recv()
            for cp in (relay(q, block_of(q)), copy(q, block_of(q), rows_of(c), False)):
                cp.start()
                passed.append(cp)
        for q in range(2):
            relay(q, block_of(2)).wait_recv()
        fw = copy(2, block_of(2), rows_of(c), False)
        fw.start()
        for p in range(3):
            copy(p, block_of(p), rows_of(1 - c), False).wait_recv()
        for cp in sends + passed + [fw]:
            cp.wait_send()

    return pl.pallas_call(
        body, name="startup",
        out_shape=[SDS((N_DEV,) + pack.shape, F32), SDS((1, N_CHIP * n), F32), SDS(w_buf.shape, w_buf.dtype)],
        in_specs=[VMEM_SPEC, ANY, VMEM_SPEC, ANY], out_specs=[VMEM_SPEC, VMEM_SPEC, ANY],
        input_output_aliases={3: 2},
        scratch_shapes=[pltpu.VMEM((d, n), F32), pltpu.VMEM((N_DEV, n), F32), pltpu.VMEM((N_DEV, N_DEV, n), F32)]
        + _small_sems() + _small_sems()
        + [pltpu.SemaphoreType.DMA((3,))] * 4 + [pltpu.SemaphoreType.DMA((2,))] * 2 + [pltpu.SemaphoreType.DMA],
        compiler_params=pltpu.CompilerParams(vmem_limit_bytes=VMEM_LIMIT),
    )(pack, w_ada_sh, b_ada_sh, w_buf)


class _Exchange:
    def __init__(self, operands, out_shape, in_place, n_sems, copies):
        self.operands, self.out_shape, self.in_place, self.n_sems, self.copies = (
            list(operands), list(out_shape), in_place, n_sems, copies)

    def sems(self):
        return [pltpu.SemaphoreType.DMA((self.n_sems,)), pltpu.SemaphoreType.DMA((self.n_sems,))]


def _x_gather_ici(bufs):
    def copies(ins, outs, send_sems, recv_sems):
        x, y, c = _mesh_pos()
        chips = _other_chips(x, y)
        out = []
        for w in range(len(outs)):
            half = outs[w].shape[1] // 2
            rows = pl.ds(pl.multiple_of(c * half, SUBLANES_BF16), half)
            for p in range(3):
                out.append(pltpu.make_async_remote_copy(
                    src_ref=outs[w].at[2 * x + y, rows], dst_ref=outs[w].at[2 * x + y, rows],
                    send_sem=send_sems.at[w * 3 + p], recv_sem=recv_sems.at[w * 3 + p],
                    device_id=(*chips[p], c), device_id_type=MESH))
        return out

    return _Exchange(bufs, [SDS(b.shape, b.dtype) for b in bufs], True, 3 * len(bufs), copies)


def _x_gather_d2d(bufs):
    def copies(ins, outs, send_sems, recv_sems):
        x, y, c = _mesh_pos()
        chips = _other_chips(x, y)
        out = []
        for w in range(len(outs)):
            half = outs[w].shape[1] // 2
            rows = pl.ds(pl.multiple_of(c * half, SUBLANES_BF16), half)
            for p in range(3):
                block = 2 * chips[p][0] + chips[p][1]
                out.append(pltpu.make_async_remote_copy(
                    src_ref=outs[w].at[block, rows], dst_ref=outs[w].at[block, rows],
                    send_sem=send_sems.at[w * 3 + p], recv_sem=recv_sems.at[w * 3 + p],
                    device_id=(x, y, 1 - c), device_id_type=MESH))
        return out

    return _Exchange(bufs, [SDS(b.shape, b.dtype) for b in bufs], True, 3 * len(bufs), copies)


N_REMOTE = 6


def _x_reduce(grads32, grads16):
    n_w = len(grads32)

    def copies(ins, outs, send_sems, recv_sems):
        g32, g16 = ins[:n_w], ins[n_w:]
        from_sib, from_far = outs[:n_w], outs[n_w:]
        x, y, c = _mesh_pos()
        chips = _other_chips(x, y)
        out = []
        for w in range(n_w):
            half = g32[w].shape[1] // 2
            k0 = w * (N_REMOTE + 1)
            out.append(pltpu.make_async_remote_copy(
                src_ref=g32[w].at[2 * x + y, pl.ds(pl.multiple_of((1 - c) * half, SUBLANES_BF16), half), :],
                dst_ref=from_sib[w], send_sem=send_sems.at[k0], recv_sem=recv_sems.at[k0],
                device_id=(x, y, 1 - c), device_id_type=MESH))
            for p in range(3):
                for f in range(2):
                    tc = c if f == 0 else 1 - c
                    k = 2 * p + f
                    out.append(pltpu.make_async_remote_copy(
                        src_ref=g16[w].at[2 * chips[p][0] + chips[p][1],
                                          pl.ds(pl.multiple_of(tc * half, SUBLANES_BF16), half), :],
                        dst_ref=from_far[w].at[k], send_sem=send_sems.at[k0 + 1 + k], recv_sem=recv_sems.at[k0 + 1 + k],
                        device_id=(*chips[p], tc), device_id_type=MESH))
        return out

    shapes = ([SDS((g.shape[1] // 2, g.shape[2]), g.dtype) for g in grads32]
              + [SDS((N_REMOTE, g.shape[1] // 2, g.shape[2]), g.dtype) for g in grads16])
    return _Exchange(list(grads32) + list(grads16), shapes, False, (N_REMOTE + 1) * n_w, copies)


def _x_pair_exchange(fulls):
    def copies(ins, outs, send_sems, recv_sems):
        x, y, c = _mesh_pos()
        out = []
        for w in range(len(outs)):
            half = outs[w].shape[0] // 2
            rows = pl.ds(pl.multiple_of(c * half, 8), half)
            out.append(pltpu.make_async_remote_copy(
                src_ref=outs[w].at[rows], dst_ref=outs[w].at[rows], send_sem=send_sems.at[w],
                recv_sem=recv_sems.at[w], device_id=(x, y, 1 - c), device_id_type=MESH))
        return out

    return _Exchange(fulls, [SDS(f.shape, f.dtype) for f in fulls], True, len(fulls), copies)


def _pallas(body, *, name, grid, in_specs, out_specs, out_shape, args, scratch=(), sem=None, ride=None):
    single = not isinstance(out_specs, (list, tuple))
    out_specs_l = [out_specs] if single else list(out_specs)
    out_shape_l = [out_shape] if single else list(out_shape)
    n_in, n_out, n_scr = len(in_specs), len(out_specs_l), len(scratch)
    if ride is None:
        res = pl.pallas_call(body, name=name, grid=grid, in_specs=list(in_specs), out_specs=out_specs,
                             out_shape=out_shape, scratch_shapes=list(scratch), compiler_params=_params(*sem))(*args)
        return res, None
    n_x, n_xo = len(ride.operands), len(ride.out_shape)

    def full_body(*refs):
        ins, x_ins = refs[:n_in], refs[n_in:n_in + n_x]
        outs = refs[n_in + n_x:n_in + n_x + n_out]
        x_outs = refs[n_in + n_x + n_out:n_in + n_x + n_out + n_xo]
        rest = refs[n_in + n_x + n_out + n_xo:]
        scr, (send_sems, recv_sems) = rest[:n_scr], rest[n_scr:]
        first = functools.reduce(jnp.logical_and, [pl.program_id(a) == 0 for a in range(len(grid))])
        last = functools.reduce(jnp.logical_and, [pl.program_id(a) == grid[a] - 1 for a in range(len(grid))])

        @pl.when(first)
        def _():
            for cp in ride.copies(x_ins, x_outs, send_sems, recv_sems):
                cp.start()

        body(*ins, *outs, *scr)

        @pl.when(last)
        def _():
            for cp in ride.copies(x_ins, x_outs, send_sems, recv_sems):
                cp.wait()

    res = pl.pallas_call(
        full_body, name=name, grid=grid, in_specs=list(in_specs) + [ANY] * n_x,
        out_specs=out_specs_l + [ANY] * n_xo, out_shape=out_shape_l + ride.out_shape,
        input_output_aliases={n_in + k: n_out + k for k in range(n_x)} if ride.in_place else {},
        scratch_shapes=list(scratch) + ride.sems(),
        compiler_params=_params(*(["arbitrary"] * len(grid))))(*args, *ride.operands)
    own = res[0] if single else list(res[:n_out])
    return own, list(res[n_out:])


def _cast_into_block(pos, w, name):
    rows, cols = w.shape
    tr = _row_tile(rows, cols, mult=SUBLANES_BF16)

    def body(pos_ref, w_ref, o_ref):
        del pos_ref
        o_ref[...] = w_ref[...].astype(BF16)

    return pl.pallas_call(
        body, name=name,
        grid_spec=pltpu.PrefetchScalarGridSpec(
            num_scalar_prefetch=1, grid=(rows // tr,),
            in_specs=[pl.BlockSpec((tr, cols), lambda i, pos_ref: (i, 0))],
            out_specs=pl.BlockSpec((None, tr, cols), lambda i, pos_ref: (pos_ref[1], i, 0))),
        out_shape=SDS((N_CHIP, rows, cols), BF16), compiler_params=_params("parallel"))(pos, w)


def _sum_terms(pos, grad, from_sib, from_far, name):
    _, rows, cols = grad.shape
    half = rows // 2
    tr = _row_tile(half, cols, mult=SUBLANES_BF16)
    nblk = half // tr

    def body(pos_ref, g_ref, s_ref, r_ref, o_ref):
        del pos_ref
        acc = g_ref[...] + s_ref[...]
        for k in range(N_REMOTE):
            acc = acc + r_ref[k].astype(F32)
        o_ref[...] = acc

    return pl.pallas_call(
        body, name=name,
        grid_spec=pltpu.PrefetchScalarGridSpec(
            num_scalar_prefetch=1, grid=(nblk,),
            in_specs=[pl.BlockSpec((None, tr, cols), lambda i, pos_ref: (pos_ref[1], pos_ref[0] * nblk + i, 0)),
                      pl.BlockSpec((tr, cols), lambda i, pos_ref: (i, 0)),
                      pl.BlockSpec((N_REMOTE, tr, cols), lambda i, pos_ref: (0, i, 0))],
            out_specs=pl.BlockSpec((tr, cols), lambda i, pos_ref: (pos_ref[0] * nblk + i, 0))),
        out_shape=SDS((rows, cols), F32),
        compiler_params=_params("parallel"),
    )(pos, grad, from_sib, from_far)


def _adamw(w, g, m, v, name):
    rows, cols = w.shape
    tr = _row_tile(rows, cols)

    def body(w_ref, g_ref, m_ref, v_ref, go_ref, d_ref, nm_ref, nv_ref):
        go_ref[...] = g_ref[...]
        _adamw_update(w_ref, g_ref, m_ref, v_ref, d_ref, nm_ref, nv_ref)

    spec = pl.BlockSpec((tr, cols), lambda i: (i, 0))
    return pl.pallas_call(body, name=name, grid=(rows // tr,), in_specs=[spec] * 4, out_specs=[spec] * 4,
                          out_shape=[SDS((rows, cols), F32)] * 4, compiler_params=_params("parallel"))(w, g, m, v)


def _adamw_update(w_ref, g_ref, m_ref, v_ref, d_ref, nm_ref, nv_ref):
    gg = g_ref[...]
    nm = ADAM_B1 * m_ref[...] + (1.0 - ADAM_B1) * gg
    nv = ADAM_B2 * v_ref[...] + (1.0 - ADAM_B2) * (gg * gg)
    m_hat = nm / (1.0 - ADAM_B1 ** ADAM_STEP)
    v_hat = nv / (1.0 - ADAM_B2 ** ADAM_STEP)
    d_ref[...] = -ADAM_LR * (m_hat / (jnp.sqrt(v_hat) + ADAM_EPS) + ADAM_WD * w_ref[...])
    nm_ref[...] = nm
    nv_ref[...] = nv


def _adamw_small(params):
    n_p = len(params)

    def body(*refs):
        ins, outs = refs[:4 * n_p], refs[4 * n_p:]
        for k in range(n_p):
            _adamw_update(*ins[4 * k:4 * k + 4], *outs[3 * k:3 * k + 3])

    flat = [a for tup in params for a in tup]
    res = pl.pallas_call(
        body, name="adamw_small", in_specs=[VMEM_SPEC] * (4 * n_p), out_specs=[VMEM_SPEC] * (3 * n_p),
        out_shape=[SDS(tup[0].shape, F32) for tup in params for _ in range(3)])(*flat)
    return [res[3 * k:3 * k + 3] for k in range(n_p)]


def _pack_sum(gathered):
    _, rows, cols = gathered.shape

    def body(g_ref, o_ref):
        acc = g_ref[0]
        for d in range(1, N_DEV):
            acc = acc + g_ref[d]
        o_ref[...] = acc

    return pl.pallas_call(body, name="pack_sum", in_specs=[VMEM_SPEC], out_specs=VMEM_SPEC,
                          out_shape=SDS((rows, cols), F32))(gathered)


def _ada_wgrad(c_all, dmod_sh):
    d = c_all.shape[1]
    n = dmod_sh.shape[1]
    tn = 512

    def body(c_ref, g_ref, o_ref):
        cc = c_ref[...]
        s = cc * jax.nn.sigmoid(cc)
        o_ref[...] = lax.dot_general(s, g_ref[...], (((0,), (0,)), ((), ())), preferred_element_type=F32,
                                     precision=lax.Precision.HIGHEST)

    return pl.pallas_call(
        body, name="ada_wgrad", grid=(n // tn,),
        in_specs=[pl.BlockSpec((N_DEV, d), lambda j: (0, 0)), pl.BlockSpec((N_DEV, tn), lambda j: (0, j))],
        out_specs=pl.BlockSpec((d, tn), lambda j: (0, j)),
        out_shape=SDS((d, n), F32), compiler_params=_params("parallel"))(c_all, dmod_sh)


def _rms(xf):
    return lax.rsqrt(jnp.mean(xf * xf, axis=-1, keepdims=True) + EPS)


def _in_proj(x, g, sc, sh, wt, b, tm, tn, ride=None):
    t, d = x.shape
    n = wt.shape[0]

    def body(x_ref, g_ref, sc_ref, sh_ref, w_ref, b_ref, z_ref, h_ref):
        @pl.when(pl.program_id(1) == 0)
        def _():
            xf = x_ref[...]
            h_ref[...] = ((xf * _rms(xf) * g_ref[...]) * (1.0 + sc_ref[...]) + sh_ref[...]).astype(BF16)

        acc = lax.dot_general(h_ref[...], w_ref[...], (((1,), (1,)), ((), ())), preferred_element_type=F32)
        z_ref[...] = (acc + b_ref[...]).astype(BF16)

    row = pl.BlockSpec((tm, d), lambda i, j: (i, 0))
    vecs, vec_specs = zip(*[_vec(v, d) for v in (g, sc, sh)])
    return _pallas(
        body, name="in_proj", grid=(t // tm, n // tn),
        in_specs=[row, *vec_specs, pl.BlockSpec((tn, d), lambda i, j: (j, 0)),
                  pl.BlockSpec((1, tn), lambda i, j: (0, j))],
        out_specs=[pl.BlockSpec((tm, tn), lambda i, j: (i, j)), row],
        out_shape=[SDS((t, n), BF16), SDS((t, d), BF16)], args=(x, *vecs, wt, b),
        sem=("parallel", "arbitrary"), ride=ride)


def _segments(d, kvw2):
    o = d + kvw2
    names = ("cb", "cc", "cx", "ga", "gc")
    seg = {nm: slice(o + k * d, o + (k + 1) * d) for k, nm in enumerate(names)}
    seg["q"], seg["kv"] = slice(0, d), slice(d, o)
    return seg


def _attn_masks():
    rows = 4 * BLOCK
    r = lax.broadcasted_iota(jnp.int32, (rows, 2 * BLOCK), 0) & (BLOCK - 1)
    col = lax.broadcasted_iota(jnp.int32, (rows, 2 * BLOCK), 1)
    return (col > r) & (col <= r + BLOCK), col


def _kv_variants(kv, n_kv_w):
    assert n_kv_w == LANES
    kb, vb = kv[:, :LANES] * (HEAD_DIM ** -0.5), kv[:, LANES:]
    kr, vr = pltpu.roll(kb, HEAD_DIM, 1), pltpu.roll(vb, HEAD_DIM, 1)
    lane = lax.broadcasted_iota(jnp.int32, kb.shape, 1)
    lo = lane < HEAD_DIM
    zero = jnp.zeros_like(kb)
    k_eff = [[None, None], [None, None]]
    v_eff = [[None, None], [None, None]]
    for h in range(2):
        for e in range(2):
            ksrc, vsrc = (kb, vb) if e == h else (kr, vr)
            keep = lo if e == 0 else jnp.logical_not(lo)
            k_eff[h][e] = jnp.where(keep, ksrc, zero)
            v_eff[h][e] = jnp.where(keep, vsrc, zero)
    return k_eff, v_eff


def _sink_column(sinks_ref, h, e):
    rowblk = lax.broadcasted_iota(jnp.int32, (4 * BLOCK, 1), 0) // BLOCK
    col = jnp.zeros((4 * BLOCK, 1), F32)
    for j in range(4):
        col = jnp.where(rowblk == j, sinks_ref[0, GROUP * h + 2 * j + e], col)
    return col


def _softmax_sink(s, valid, sink):
    s = jnp.where(valid, s, -jnp.inf)
    m = jnp.maximum(jnp.max(s, axis=-1, keepdims=True), sink)
    p = jnp.exp(s - m)
    psink = jnp.exp(sink - m)
    den = jnp.sum(p, axis=-1, keepdims=True) + psink
    inv = 1.0 / den
    return p * inv, psink * inv


def _shift_down(a, s, prev):
    rows = a.shape[0]
    out = pltpu.roll(a, s, 0)
    row = lax.broadcasted_iota(jnp.int32, a.shape, 0)
    for t in range(s):
        out = jnp.where(row == t, prev[SUBLANES_BF16 - s + t:SUBLANES_BF16 - s + t + 1, :], out)
    del rows
    return out


def _shift_up(a, s, nxt):
    rows = a.shape[0]
    out = pltpu.roll(a, rows - s, 0)
    row = lax.broadcasted_iota(jnp.int32, a.shape, 0)
    for t in range(s):
        out = jnp.where(row == rows - s + t, nxt[t:t + 1, :], out)
    return out


def _stack_pairs(ref, h, rows=slice(None)):
    return jnp.concatenate([ref[rows, (4 * h + j) * LANES:(4 * h + j + 1) * LANES] for j in range(4)], axis=0)


FWD_BLOCKS = 4


def _mixer_fwd(z, sinks, conv_w, d, ride=None):
    t, zw = z.shape
    kvw2 = zw - 6 * d
    tq = FWD_BLOCKS * BLOCK
    halo = tq // SUBLANES_BF16
    seg = _segments(d, kvw2)

    def body(z_ref, kvp_ref, prev_ref, sinks_ref, cw_ref, attn_ref, merged_ref):
        n = pl.program_id(0)
        band, col = _attn_masks()
        for b in range(FWD_BLOCKS):
            rows = slice(b * BLOCK, (b + 1) * BLOCK)
            before = slice((b - 1) * BLOCK, b * BLOCK)
            kv_prev = kvp_ref[...] if b == 0 else z_ref[before, seg["kv"]]
            kv = jnp.concatenate([kv_prev, z_ref[rows, seg["kv"]]], axis=0)
            k_eff, v_eff = _kv_variants(kv, kvw2 // 2)
            valid = band & ((n > 0) | (col >= BLOCK)) if b == 0 else band
            for h in range(2):
                q4 = _stack_pairs(z_ref, h, rows)
                o4 = jnp.zeros((4 * BLOCK, LANES), F32)
                for e in range(2):
                    s = lax.dot_general(q4, k_eff[h][e], (((1,), (1,)), ((), ())), preferred_element_type=F32)
                    p, _ = _softmax_sink(s, valid, _sink_column(sinks_ref, h, e))
                    o4 = o4 + jnp.dot(p.astype(BF16), v_eff[h][e], preferred_element_type=F32)
                for j in range(4):
                    attn_ref[rows, (4 * h + j) * LANES:(4 * h + j + 1) * LANES] = (
                        o4[j * BLOCK:(j + 1) * BLOCK].astype(BF16))
            cb = z_ref[rows, seg["cb"]].astype(F32)
            p_in = z_ref[rows, seg["cc"]].astype(F32) * z_ref[rows, seg["cx"]].astype(F32)
            if b == 0:
                prev = jnp.where(n > 0, prev_ref[:, seg["cc"]].astype(F32) * prev_ref[:, seg["cx"]].astype(F32), 0.0)
            else:
                tail = slice(b * BLOCK - SUBLANES_BF16, b * BLOCK)
                prev = z_ref[tail, seg["cc"]].astype(F32) * z_ref[tail, seg["cx"]].astype(F32)
            cconv = (cw_ref[0:1, :] * _shift_down(p_in, 2, prev) + cw_ref[1:2, :] * _shift_down(p_in, 1, prev)
                     + cw_ref[2:3, :] * p_in)
            sa = jax.nn.sigmoid(z_ref[rows, seg["ga"]].astype(F32))
            sg = jax.nn.sigmoid(z_ref[rows, seg["gc"]].astype(F32))
            merged_ref[rows, :] = (sa * attn_ref[rows, :].astype(F32) + sg * (cb * cconv)).astype(BF16)

    blk = pl.BlockSpec((tq, d), lambda n: (n, 0))
    return _pallas(
        body, name="mixer_fwd", grid=(t // tq,),
        in_specs=[pl.BlockSpec((tq, zw), lambda n: (n, 0)),
                  pl.BlockSpec((BLOCK, kvw2), lambda n: (jnp.maximum(n * FWD_BLOCKS - 1, 0), d // kvw2)),
                  pl.BlockSpec((SUBLANES_BF16, zw), lambda n: (jnp.maximum(n * halo - 1, 0), 0)),
                  SMEM_SPEC, pl.BlockSpec((3, d), lambda n: (0, 0))],
        out_specs=[blk, blk],
        out_shape=[SDS((t, d), BF16), SDS((t, d), BF16)],
        args=(z, z, z, sinks, conv_w), sem=("parallel",), ride=ride)


def _out_proj_fwd(merged, w_out, x, ga1, g_ffn, sc2, sh2, tm):
    t, d = x.shape

    def body(m_ref, w_ref, x_ref, ga_ref, g_ref, sc_ref, sh_ref, y_ref, x1_ref, h_ref):
        y = jnp.dot(m_ref[...], w_ref[...], preferred_element_type=F32)
        x1 = x_ref[...] + ga_ref[...] * y
        y_ref[...] = y.astype(BF16)
        x1_ref[...] = x1
        h_ref[...] = ((x1 * _rms(x1) * g_ref[...]) * (1.0 + sc_ref[...]) + sh_ref[...]).astype(BF16)

    row = pl.BlockSpec((tm, d), lambda i: (i, 0))
    vecs, vec_specs = zip(*[_vec(v, d) for v in (ga1, g_ffn, sc2, sh2)])
    return pl.pallas_call(
        body, name="out_proj_fwd", grid=(t // tm,),
        in_specs=[row, pl.BlockSpec((d, d), lambda i: (0, 0)), row, *vec_specs],
        out_specs=[row, row, row],
        out_shape=[SDS((t, d), BF16), SDS((t, d), F32), SDS((t, d), BF16)],
        compiler_params=_params("parallel"))(merged, w_out, x, *vecs)


def _ffn_in_fwd(h2, w, ff, tm, tn):
    t, d = h2.shape
    nj = ff // tn
    assert w.shape == (2 * nj, d, tn)

    def body(h_ref, wg_ref, wu_ref, gu_ref, act_ref):
        hh = h_ref[...]
        g = jnp.dot(hh, wg_ref[...], preferred_element_type=F32)
        u = jnp.dot(hh, wu_ref[...], preferred_element_type=F32)
        sg = jax.nn.sigmoid(g)
        silu = g * sg
        gu_ref[0] = (u * (sg + silu * (1.0 - sg))).astype(BF16)
        gu_ref[1] = silu.astype(BF16)
        act_ref[...] = (silu * u).astype(BF16)

    return pl.pallas_call(
        body, name="ffn_in_fwd", grid=(nj, t // tm),
        in_specs=[pl.BlockSpec((tm, d), lambda j, i: (i, 0)), pl.BlockSpec((None, d, tn), lambda j, i: (j, 0, 0)),
                  pl.BlockSpec((None, d, tn), lambda j, i: (j + nj, 0, 0))],
        out_specs=[pl.BlockSpec((2, tm, tn), lambda j, i: (0, i, j)), pl.BlockSpec((tm, tn), lambda j, i: (i, j))],
        out_shape=[SDS((2, t, ff), BF16), SDS((t, ff), BF16)],
        compiler_params=_params("parallel", "parallel"))(h2, w, w)


def _ffn_out_loss(act, w, x1, target, ga2, g_final, tm):
    t, d = x1.shape
    ff = act.shape[1]

    def body(a_ref, w_ref, x1_ref, tg_ref, ga_ref, gf_ref, dx2_ref, dy2_ref, st_ref):
        @pl.when(pl.program_id(0) == 0)
        def _():
            st_ref[...] = jnp.zeros_like(st_ref)

        halves = [slice(k * (tm // 2), (k + 1) * (tm // 2)) for k in range(2)]
        y2s = [jnp.dot(a_ref[rows, :], w_ref[...], preferred_element_type=F32) for rows in halves]
        for rows, y2 in zip(halves, y2s):
            x2 = x1_ref[rows, :] + ga_ref[...] * y2
            r = _rms(x2)
            yn = x2 * r
            err = yn * gf_ref[...] - tg_ref[rows, :]
            loss = 0.5 * jnp.sum(jnp.mean(err * err, axis=-1, keepdims=True), axis=0, keepdims=True)
            dy = err * (1.0 / d)
            u = dy * gf_ref[...]
            dx2 = r * (u - yn * jnp.mean(u * yn, axis=-1, keepdims=True))
            dx2_ref[rows, :] = dx2
            dy2_ref[rows, :] = (ga_ref[...] * dx2).astype(BF16)
            st_ref[0:1, :] += jnp.sum(dx2 * y2, axis=0, keepdims=True)
            st_ref[1:2, :] += jnp.sum(dy * yn, axis=0, keepdims=True)
            st_ref[2:3, :] += jnp.broadcast_to(loss, (1, d))

    row = pl.BlockSpec((tm, d), lambda i: (i, 0))
    vecs, vec_specs = zip(*[_vec(v, d) for v in (ga2, g_final)])
    return pl.pallas_call(
        body, name="ffn_out_loss", grid=(t // tm,),
        in_specs=[pl.BlockSpec((tm, ff), lambda i: (i, 0)),
                  pl.BlockSpec((ff, d), lambda i: (0, 0), pipeline_mode=pl.Buffered(1)), row, row, *vec_specs],
        out_specs=[row, row, pl.BlockSpec((8, d), lambda i: (0, 0))],
        out_shape=[SDS((t, d), F32), SDS((t, d), BF16), SDS((8, d), F32)],
        compiler_params=_params("arbitrary"))(act, w, x1, target, *vecs)


def _ffn_out_bwd(dy2, w, gu, tm, tn):
    t, d = dy2.shape
    ff = w.shape[0]

    def body(dy_ref, w_ref, gu_ref, o_ref):
        dy = dy_ref[...]
        for lo in range(0, tn, 3 * LANES):
            cols = slice(lo, min(lo + 3 * LANES, tn))
            dact = lax.dot_general(dy, w_ref[cols, :], (((1,), (1,)), ((), ())), preferred_element_type=F32)
            o_ref[0, :, cols] = (dact * gu_ref[0, :, cols].astype(F32)).astype(BF16)
            o_ref[1, :, cols] = (dact * gu_ref[1, :, cols].astype(F32)).astype(BF16)

    gu_spec = pl.BlockSpec((2, tm, tn), lambda j, i: (0, i, j))
    return pl.pallas_call(
        body, name="ffn_out_bwd", grid=(ff // tn, t // tm),
        in_specs=[pl.BlockSpec((tm, d), lambda j, i: (i, 0)), pl.BlockSpec((tn, d), lambda j, i: (j, 0)), gu_spec],
        out_specs=gu_spec, out_shape=SDS((2, t, ff), BF16),
        compiler_params=_params("parallel", "parallel"))(dy2, w, gu)


def _wgrad(a, b, a_spec, b_spec, out_spec, out_shape, grid, name, ride=None):
    def body(a_ref, b_ref, o_ref, o16_ref):
        k = pl.program_id(len(grid) - 1)

        @pl.when(k == 0)
        def _():
            o_ref[...] = jnp.zeros_like(o_ref)

        o_ref[...] += lax.dot_general(a_ref[...], b_ref[...], (((0,), (0,)), ((), ())), preferred_element_type=F32)

        @pl.when(k == grid[-1] - 1)
        def _():
            o16_ref[...] = o_ref[...].astype(BF16)

    return _pallas(
        body, name=name, grid=grid, in_specs=[a_spec, b_spec], out_specs=[out_spec, out_spec],
        out_shape=[out_shape, SDS(out_shape.shape, BF16)], args=(a, b),
        sem=["parallel"] * (len(grid) - 1) + ["arbitrary"], ride=ride)


def _ffn_in_bwd(dgu, w, x1, dx2, y1, g_ffn, sc2, ga1, tm):
    t, d = x1.shape
    ff = dgu.shape[2]
    n_sh, _, sw = w.shape
    per = ff // sw
    nt = (((1,), (1,)), ((), ()))

    def body(a_ref, w_ref, x1_ref, dx2_ref, y1_ref, g_ref, sc_ref, ga_ref, dx1_ref, dy1_ref, st_ref):
        @pl.when(pl.program_id(0) == 0)
        def _():
            st_ref[...] = jnp.zeros_like(st_ref)

        dh = None
        for j in range(n_sh):
            part = lax.dot_general(a_ref[j // per, :, (j % per) * sw:(j % per + 1) * sw], w_ref[j], nt,
                                   preferred_element_type=F32)
            dh = part if dh is None else dh + part
        x1 = x1_ref[...]
        r = _rms(x1)
        xn = x1 * r
        g = g_ref[...]
        dn = dh * (1.0 + sc_ref[...])
        u = dn * g
        dx1 = dx2_ref[...] + r * (u - xn * jnp.mean(u * xn, axis=-1, keepdims=True))
        dx1_ref[...] = dx1
        dy1_ref[...] = (ga_ref[...] * dx1).astype(BF16)
        st_ref[0:1, :] += jnp.sum(dh, axis=0, keepdims=True)
        st_ref[1:2, :] += jnp.sum(dh * (xn * g), axis=0, keepdims=True)
        st_ref[2:3, :] += jnp.sum(dn * xn, axis=0, keepdims=True)
        st_ref[3:4, :] += jnp.sum(dx1 * y1_ref[...].astype(F32), axis=0, keepdims=True)

    row = pl.BlockSpec((tm, d), lambda i: (i, 0))
    vecs, vec_specs = zip(*[_vec(v, d) for v in (g_ffn, sc2, ga1)])
    return pl.pallas_call(
        body, name="ffn_in_bwd", grid=(t // tm,),
        in_specs=[pl.BlockSpec((2, tm, ff), lambda i: (0, i, 0)),
                  pl.BlockSpec((n_sh, d, sw), lambda i: (0, 0, 0), pipeline_mode=pl.Buffered(1)),
                  row, row, row, *vec_specs],
        out_specs=[row, row, pl.BlockSpec((8, d), lambda i: (0, 0))],
        out_shape=[SDS((t, d), F32), SDS((t, d), BF16), SDS((8, d), F32)],
        compiler_params=_params("arbitrary"))(dgu, w, x1, dx2, y1, *vecs)


def _out_proj_bwd(dy1, w_out, tm, ride=None):
    t, d = dy1.shape

    def body(dy_ref, w_ref, o_ref):
        o_ref[...] = lax.dot_general(dy_ref[...], w_ref[...], (((1,), (1,)), ((), ())),
                                     preferred_element_type=F32).astype(BF16)

    row = pl.BlockSpec((tm, d), lambda i: (i, 0))
    return _pallas(body, name="out_proj_bwd", grid=(t // tm,),
                   in_specs=[row, pl.BlockSpec((d, d), lambda i: (0, 0))], out_specs=row,
                   out_shape=SDS((t, d), BF16), args=(dy1, w_out), sem=("parallel",), ride=ride)


BWD_BLOCKS = 2


def _mixer_bwd(z, dmerged, attn, sinks, conv_w, d, ride=None):
    t, zw = z.shape
    kvw2 = zw - 6 * d
    tq = BWD_BLOCKS * BLOCK
    steps = t // tq
    halo = tq // SUBLANES_BF16
    last_halo = t // SUBLANES_BF16 - 1
    scale = HEAD_DIM ** -0.5
    seg = _segments(d, kvw2)

    def body(z_ref, kvp_ref, prev_ref, next_ref, dm_ref, dmn_ref, attn_ref, sinks_ref, cw_ref,
             dz_ref, dkv_ref, db_ref, dbkv_ref, dcw_ref, dsk_ref, carry_ref):
        n = pl.program_id(0)

        @pl.when(n == 0)
        def _():
            carry_ref[...] = jnp.zeros_like(carry_ref)
            db_ref[...] = jnp.zeros_like(db_ref)
            dbkv_ref[...] = jnp.zeros_like(dbkv_ref)
            dcw_ref[...] = jnp.zeros_like(dcw_ref)
            dsk_ref[...] = jnp.zeros_like(dsk_ref)

        def one_block(b, pending):
            rows = slice(b * BLOCK, (b + 1) * BLOCK)
            before = slice((b - 1) * BLOCK, b * BLOCK)
            dm = dm_ref[rows, :].astype(F32)
            sa = jax.nn.sigmoid(z_ref[rows, seg["ga"]].astype(F32))
            dga = dm * attn_ref[rows, :].astype(F32) * sa * (1.0 - sa)
            dz_ref[rows, seg["ga"]] = dga.astype(BF16)
            db_ref[0:1, seg["ga"]] += jnp.sum(dga, axis=0, keepdims=True)
            dattn = (dm * sa).astype(BF16)

            kv_prev = kvp_ref[...] if b == 0 else z_ref[before, seg["kv"]]
            kv = jnp.concatenate([kv_prev, z_ref[rows, seg["kv"]]], axis=0)
            k_eff, v_eff = _kv_variants(kv, kvw2 // 2)
            band, col = _attn_masks()
            valid = band & ((n > 0) | (col >= BLOCK)) if b == 0 else band
            lane_lo = lax.broadcasted_iota(jnp.int32, (2 * BLOCK, LANES), 1) < HEAD_DIM
            sink_lane = lax.broadcasted_iota(jnp.int32, (1, LANES), 1)
            rowblk = lax.broadcasted_iota(jnp.int32, (4 * BLOCK, 1), 0) // BLOCK
            dk_acc = [jnp.zeros((2 * BLOCK, LANES), F32), jnp.zeros((2 * BLOCK, LANES), F32)]
            dv_acc = [jnp.zeros((2 * BLOCK, LANES), F32), jnp.zeros((2 * BLOCK, LANES), F32)]
            dsink = jnp.zeros((1, LANES), F32)
            for h in range(2):
                q4 = _stack_pairs(z_ref, h, rows)
                do4 = jnp.concatenate([dattn[:, (4 * h + j) * LANES:(4 * h + j + 1) * LANES] for j in range(4)],
                                      axis=0)
                dq4 = jnp.zeros((4 * BLOCK, LANES), F32)
                for e in range(2):
                    s = lax.dot_general(q4, k_eff[h][e], (((1,), (1,)), ((), ())), preferred_element_type=F32)
                    p, psink = _softmax_sink(s, valid, _sink_column(sinks_ref, h, e))
                    dp = lax.dot_general(do4, v_eff[h][e], (((1,), (1,)), ((), ())), preferred_element_type=F32)
                    delta = jnp.sum(p * dp, axis=-1, keepdims=True)
                    ds = (p * (dp - delta)).astype(BF16)
                    dq4 = dq4 + jnp.dot(ds, k_eff[h][e], preferred_element_type=F32)
                    dk = lax.dot_general(q4, ds, (((0,), (0,)), ((), ())), preferred_element_type=F32).T
                    dv = lax.dot_general(do4, p.astype(BF16), (((0,), (0,)), ((), ())), preferred_element_type=F32).T
                    keep = lane_lo if e == 0 else jnp.logical_not(lane_lo)
                    slot = 0 if e == h else 1
                    dk_acc[slot] = dk_acc[slot] + jnp.where(keep, dk, 0.0)
                    dv_acc[slot] = dv_acc[slot] + jnp.where(keep, dv, 0.0)
                    dsk = -(psink * delta)
                    for j in range(4):
                        tot = jnp.sum(jnp.where(rowblk == j, dsk, 0.0), axis=0, keepdims=True)
                        dsink = dsink + jnp.where(sink_lane == GROUP * h + 2 * j + e, tot, 0.0)
                for j in range(4):
                    cols = slice((4 * h + j) * LANES, (4 * h + j + 1) * LANES)
                    dqj = dq4[j * BLOCK:(j + 1) * BLOCK]
                    dz_ref[rows, cols] = dqj.astype(BF16)
                    db_ref[0:1, cols] += jnp.sum(dqj, axis=0, keepdims=True)
            dsk_ref[0:1, :] += dsink
            dkv_new = jnp.concatenate([(dk_acc[0] + pltpu.roll(dk_acc[1], HEAD_DIM, 1)) * scale,
                                       dv_acc[0] + pltpu.roll(dv_acc[1], HEAD_DIM, 1)], axis=1)
            done = pending + dkv_new[:BLOCK]
            dkv_ref[rows, :] = done.astype(BF16)
            dbkv_ref[0:1, :] += jnp.sum(done, axis=0, keepdims=True)

            cb = z_ref[rows, seg["cb"]].astype(F32)
            cc = z_ref[rows, seg["cc"]].astype(F32)
            cx = z_ref[rows, seg["cx"]].astype(F32)
            sg = jax.nn.sigmoid(z_ref[rows, seg["gc"]].astype(F32))
            p_in = cc * cx
            if b == 0:
                prev = jnp.where(n > 0, prev_ref[:, seg["cc"]].astype(F32) * prev_ref[:, seg["cx"]].astype(F32), 0.0)
            else:
                tail = slice(b * BLOCK - SUBLANES_BF16, b * BLOCK)
                prev = z_ref[tail, seg["cc"]].astype(F32) * z_ref[tail, seg["cx"]].astype(F32)
            p_m1 = _shift_down(p_in, 1, prev)
            p_m2 = _shift_down(p_in, 2, prev)
            w0, w1, w2 = cw_ref[0:1, :], cw_ref[1:2, :], cw_ref[2:3, :]
            cconv = w0 * p_m2 + w1 * p_m1 + w2 * p_in
            dconv = dm * sg
            dgc = dm * (cb * cconv) * sg * (1.0 - sg)
            dcb = dconv * cconv
            dcc_t = dconv * cb
            if b == BWD_BLOCKS - 1:
                nxt = jnp.where(n < steps - 1,
                                dmn_ref[...].astype(F32) * jax.nn.sigmoid(next_ref[:, seg["gc"]].astype(F32))
                                * next_ref[:, seg["cb"]].astype(F32), 0.0)
            else:
                head = slice((b + 1) * BLOCK, (b + 1) * BLOCK + SUBLANES_BF16)
                nxt = (dm_ref[head, :].astype(F32) * jax.nn.sigmoid(z_ref[head, seg["gc"]].astype(F32))
                       * z_ref[head, seg["cb"]].astype(F32))
            dpin = w2 * dcc_t + w1 * _shift_up(dcc_t, 1, nxt) + w0 * _shift_up(dcc_t, 2, nxt)
            for nm, val in (("cb", dcb), ("cc", dpin * cx), ("cx", dpin * cc), ("gc", dgc)):
                dz_ref[rows, seg[nm]] = val.astype(BF16)
                db_ref[0:1, seg[nm]] += jnp.sum(val, axis=0, keepdims=True)
            dcw_ref[0:1, :] += jnp.sum(dcc_t * p_m2, axis=0, keepdims=True)
            dcw_ref[1:2, :] += jnp.sum(dcc_t * p_m1, axis=0, keepdims=True)
            dcw_ref[2:3, :] += jnp.sum(dcc_t * p_in, axis=0, keepdims=True)
            return dkv_new[BLOCK:]

        @pl.when(n < steps)
        def _():
            pending = carry_ref[...]
            for b in range(BWD_BLOCKS):
                pending = one_block(b, pending)
            carry_ref[...] = pending

        @pl.when(n == steps)
        def _():
            done = carry_ref[...]
            dkv_ref[:BLOCK, :] = done.astype(BF16)
            dkv_ref[BLOCK:, :] = jnp.zeros((tq - BLOCK, kvw2), BF16)
            dbkv_ref[0:1, :] += jnp.sum(done, axis=0, keepdims=True)

    def cur(n):
        return jnp.minimum(n, steps - 1)

    def after(n):
        return jnp.minimum((cur(n) + 1) * halo, last_halo)

    blk = pl.BlockSpec((tq, d), lambda n: (cur(n), 0))
    return _pallas(
        body, name="mixer_bwd", grid=(steps + 1,), ride=ride, sem=("arbitrary",),
        args=(z, z, z, z, dmerged, dmerged, attn, sinks, conv_w),
        in_specs=[pl.BlockSpec((tq, zw), lambda n: (cur(n), 0)),
                  pl.BlockSpec((BLOCK, kvw2), lambda n: (jnp.maximum(cur(n) * BWD_BLOCKS - 1, 0), d // kvw2)),
                  pl.BlockSpec((SUBLANES_BF16, zw), lambda n: (jnp.maximum(cur(n) * halo - 1, 0), 0)),
                  pl.BlockSpec((SUBLANES_BF16, zw), lambda n: (after(n), 0)),
                  blk,
                  pl.BlockSpec((SUBLANES_BF16, d), lambda n: (after(n), 0)),
                  blk, SMEM_SPEC, pl.BlockSpec((3, d), lambda n: (0, 0))],
        out_specs=[pl.BlockSpec((tq, zw), lambda n: (cur(n), 0)),
                   pl.BlockSpec((tq, kvw2), lambda n: (n, 0)),
                   pl.BlockSpec((8, zw), lambda n: (0, 0)), pl.BlockSpec((8, kvw2), lambda n: (0, 0)),
                   pl.BlockSpec((8, d), lambda n: (0, 0)), pl.BlockSpec((8, LANES), lambda n: (0, 0))],
        out_shape=[SDS((t, zw), BF16), SDS((t + tq, kvw2), BF16), SDS((8, zw), F32), SDS((8, kvw2), F32),
                   SDS((8, d), F32), SDS((8, LANES), F32)],
        scratch=[pltpu.VMEM((BLOCK, kvw2), F32)])


def _wgrad_in(dz, dkv, h1, tk, ride=None):
    t, zw = dz.shape
    d = h1.shape[1]
    kvw2 = dkv.shape[1]
    blk = d + kvw2
    assert zw % blk == 0
    tn = (((0,), (0,)), ((), ()))

    def body(a_ref, akv_ref, h_ref, o_ref, o16_ref):
        n, k = pl.program_id(0), pl.program_id(1)

        @pl.when(k == 0)
        def _():
            o_ref[...] = jnp.zeros_like(o_ref)

        @pl.when(n == 0)
        def _():
            o_ref[:d, :] += lax.dot_general(a_ref[:, :d], h_ref[...], tn, preferred_element_type=F32)
            o_ref[d:, :] += lax.dot_general(akv_ref[...], h_ref[...], tn, preferred_element_type=F32)

        @pl.when(n > 0)
        def _():
            o_ref[...] += lax.dot_general(a_ref[...], h_ref[...], tn, preferred_element_type=F32)

        @pl.when(k == t // tk - 1)
        def _():
            o16_ref[...] = o_ref[...].astype(BF16)

    out_spec = pl.BlockSpec((blk, d), lambda n, k: (n, 0))
    return _pallas(
        body, name="wgrad_in", grid=(zw // blk, t // tk),
        in_specs=[pl.BlockSpec((tk, blk), lambda n, k: (k, n)), pl.BlockSpec((tk, kvw2), lambda n, k: (k, 0)),
                  pl.BlockSpec((tk, d), lambda n, k: (k, 0))],
        out_specs=[out_spec, out_spec], out_shape=[SDS((zw, d), F32), SDS((zw, d), BF16)],
        args=(dz, dkv, h1), sem=("parallel", "arbitrary"), ride=ride)


def _in_proj_bwd(dz, dkv, wt, x, dx1, g_mix, sc1, tm, ride=None):
    t, d = x.shape
    zw = dz.shape[1]
    kvw2 = dkv.shape[1]
    rest = d + kvw2

    def body(a_ref, akv_ref, w_ref, x_ref, dx1_ref, g_ref, sc_ref, gx_ref, st_ref):
        @pl.when(pl.program_id(0) == 0)
        def _():
            st_ref[...] = jnp.zeros_like(st_ref)

        dh = (jnp.dot(a_ref[:, :d], w_ref[:d, :], preferred_element_type=F32)
              + jnp.dot(akv_ref[...], w_ref[d:rest, :], preferred_element_type=F32)
              + jnp.dot(a_ref[:, rest:], w_ref[rest:, :], preferred_element_type=F32))
        xx = x_ref[...]
        r = _rms(xx)
        xn = xx * r
        g = g_ref[...]
        dn = dh * (1.0 + sc_ref[...])
        u = dn * g
        gx_ref[...] = dx1_ref[...] + r * (u - xn * jnp.mean(u * xn, axis=-1, keepdims=True))
        st_ref[0:1, :] += jnp.sum(dh, axis=0, keepdims=True)
        st_ref[1:2, :] += jnp.sum(dh * (xn * g), axis=0, keepdims=True)
        st_ref[2:3, :] += jnp.sum(dn * xn, axis=0, keepdims=True)

    row = pl.BlockSpec((tm, d), lambda i: (i, 0))
    vecs, vec_specs = zip(*[_vec(v, d) for v in (g_mix, sc1)])
    return _pallas(
        body, name="in_proj_bwd", grid=(t // tm,),
        in_specs=[pl.BlockSpec((tm, zw), lambda i: (i, 0)), pl.BlockSpec((tm, kvw2), lambda i: (i, 0)),
                  pl.BlockSpec((zw, d), lambda i: (0, 0), pipeline_mode=pl.Buffered(1)),
                  row, row, *vec_specs],
        out_specs=[row, pl.BlockSpec((8, d), lambda i: (0, 0))],
        out_shape=[SDS((t, d), F32), SDS((8, d), F32)],
        args=(dz, dkv, wt, x, dx1, *vecs), sem=("arbitrary",), ride=ride)


def _to_lanes(v, rows=None):
    flat = v.reshape(-1)
    need = -(-flat.shape[0] // LANES)
    need = -(-need // 8) * 8 if rows is None else rows
    return jnp.pad(flat, (0, need * LANES - flat.shape[0])).reshape(need, LANES)


def kernel(x, c, w_ada, b_ada, g_mix, w_in, b_in, sinks, conv_w, w_out, g_ffn, w_ffn_in, w_ffn_out, g_final, loss_target, m_w_ada, m_b_ada, m_g_mix, m_w_in, m_b_in, m_sinks, m_conv_w, m_w_out, m_g_ffn, m_w_ffn_in, m_w_ffn_out, m_g_final, v_w_ada, v_b_ada, v_g_mix, v_w_in, v_b_in, v_sinks, v_conv_w, v_w_out, v_g_ffn, v_w_ffn_in, v_w_ffn_out, v_g_final):
    xs, tgt = x[0], loss_target[0]
    t, d = xs.shape
    zw = w_in.shape[2] * N_CHIP
    kvw2 = zw - 6 * d
    ff = w_ffn_out.shape[1] * N_CHIP
    n_mod = w_ada.shape[2] * N_CHIP // d
    mod_sh = w_ada.shape[2]
    cw_sh = conv_w.shape[2]
    assert d % (8 * LANES) == 0 and kvw2 == 2 * LANES and t % 512 == 0 and n_mod == 6
    xi, yi, ci = _mesh_pos()
    j_me = 2 * xi + yi
    pos = jnp.stack([ci, j_me]).astype(jnp.int32)
    tm = 512

    w_in_t, m_w_in_t, v_w_in_t = w_in[0].T, m_w_in[0].T, v_w_in[0].T
    assert d == 8 * LANES
    pack1 = jnp.concatenate([c.reshape(d // LANES, LANES), conv_w[0].reshape(-1, LANES)], axis=0)
    pack1 = jnp.pad(pack1, ((0, 16 - pack1.shape[0]), (0, 0)))
    b_ada_sh = lax.dynamic_slice(b_ada, (0, j_me * mod_sh), (1, mod_sh))
    g1, mod, w_in_g = _startup(pack1, w_ada[0], b_ada_sh, _cast_into_block(pos, w_in_t, "cast_w_in"))
    c_all = g1[:, :d // LANES, :].reshape(N_DEV, d)
    cw_rows = 3 * cw_sh // LANES
    conv_w_full = jnp.concatenate(
        [g1[2 * j, d // LANES:d // LANES + cw_rows, :].reshape(3, cw_sh) for j in range(N_CHIP)], axis=1)
    sh1, sc1, ga1, sh2, sc2, ga2 = [(mod, k) for k in range(6)]
    w_in_tf = w_in_g.reshape(zw, d)
    later = [_cast_into_block(pos, w_out[0], "cast_w_out"), _cast_into_block(pos, w_ffn_in[0], "cast_w_ffn_in"),
             _cast_into_block(pos, w_ffn_out[0], "cast_w_ffn_out")]

    (z, h1), later = _in_proj(xs, g_mix, sc1, sh1, w_in_tf, b_in, min(t, 1024), zw // 5, ride=_x_gather_ici(later))
    (attn, merged), later = _mixer_fwd(z, sinks, conv_w_full, d, ride=_x_gather_d2d(later))
    w_out_f = later[0].reshape(d, d)
    w_ffn_in_f = later[1]
    w_ffn_out_f = later[2].reshape(ff, d)
    tml = min(t, 1024)
    y1, x1, h2 = _out_proj_fwd(merged, w_out_f, xs, ga1, g_ffn, sc2, sh2, tml)
    gu, act = _ffn_in_fwd(h2, w_ffn_in_f, ff, tml, ff // 2)
    dx2, dy2, st_loss = _ffn_out_loss(act, w_ffn_out_f, x1, tgt, ga2, g_final.reshape(1, d), tml)

    dgu = _ffn_out_bwd(dy2, w_ffn_out_f, gu, tml, ff // 2)
    tk = min(t, 2048)
    dw_ffn_out, _ = _wgrad(
        act, dy2, pl.BlockSpec((tk, ff // 2), lambda m, k: (k, m)), pl.BlockSpec((tk, d), lambda m, k: (k, 0)),
        pl.BlockSpec((ff // 2, d), lambda m, k: (m, 0)), SDS((ff, d), F32), (2, t // tk), "wgrad_ffn_out")
    dx1, dy1, st_ffn = _ffn_in_bwd(dgu, w_ffn_in_f, x1, dx2, y1, g_ffn, sc2, ga1, tm)
    dw_ffn_in, _ = _wgrad(
        h2, dgu, pl.BlockSpec((tk, d), lambda n, k: (k, 0)),
        pl.BlockSpec((None, tk, ff // 2), lambda n, k: (n // 2, k, n % 2)),
        pl.BlockSpec((None, d, ff // 2), lambda n, k: (n, 0, 0)), SDS((N_CHIP, d, ff // 2), F32),
        (N_CHIP, t // tk), "wgrad_ffn_in")
    dw_out, _ = _wgrad(
        merged, dy1, pl.BlockSpec((tk, d), lambda m, k: (k, 0)), pl.BlockSpec((tk, d), lambda m, k: (k, 0)),
        pl.BlockSpec((d, d), lambda m, k: (0, 0)), SDS((d, d), F32), (1, t // tk), "wgrad_out")

    early = [[g.reshape(N_CHIP, -1, g.shape[-1]) for g in pair] for pair in (dw_out, dw_ffn_in, dw_ffn_out)]
    early_names = ["w_out", "w_ffn_in", "w_ffn_out"]
    dmerged, _ = _out_proj_bwd(dy1, w_out_f, tml)
    (dz, dkv_shifted, db_z, db_kv, dcw, dsk), terms = _mixer_bwd(
        z, dmerged, attn, sinks, conv_w_full, d, ride=_x_reduce([e[0] for e in early], [e[1] for e in early]))
    dkv = dkv_shifted[BLOCK:BLOCK + t]
    fulls = [_sum_terms(pos, e[0], s, r, "sum_terms_" + nm)
             for e, s, r, nm in zip(early, terms[:3], terms[3:], early_names)]
    dw_in_t, (g_w_out, g_w_ffn_in, g_w_ffn_out) = _wgrad_in(dz, dkv, h1, tk, ride=_x_pair_exchange(fulls))
    dw_in_t = [g.reshape(N_CHIP, zw // N_CHIP, d) for g in dw_in_t]

    (grad_x, st_in), (from_sib, from_far) = _in_proj_bwd(dz, dkv, w_in_tf, xs, dx1, g_mix, sc1, tm,
                                                         ride=_x_reduce([dw_in_t[0]], [dw_in_t[1]]))
    g_w_in_half = _sum_terms(pos, dw_in_t[0], from_sib, from_far, "sum_terms_w_in")

    dmod = jnp.concatenate([st_in[0:1], st_in[1:2], st_ffn[3:4], st_ffn[0:1], st_ffn[1:2], st_loss[0:1]], axis=1)
    db_in = jnp.concatenate([db_z[0:1, :d], db_kv[0:1], db_z[0:1, d + kvw2:]], axis=1)
    seg = [dmod, st_in[2:3], db_in, dsk[0:1], dcw[0:3].reshape(1, 3 * d), st_ffn[2:3], st_loss[1:2],
           st_loss[2:3, :LANES]]
    sizes = [s.shape[1] for s in seg]
    pack2 = _to_lanes(jnp.concatenate(seg, axis=1))
    packs, g_w_in_t = _tail_exchange(pack2, g_w_in_half)
    tot = _pack_sum(packs).reshape(-1)
    offs = [sum(sizes[:k]) for k in range(len(sizes))]
    gb_ada, gg_mix, gb_in, gsinks, gcw, gg_ffn, gg_final, loss_v = [tot[o:o + s] for o, s in zip(offs, sizes)]
    loss = loss_v[0]
    gsinks = gsinks[:sinks.shape[1]]
    gcw_sh = lax.dynamic_slice(gcw.reshape(3, d), (0, j_me * cw_sh), (3, cw_sh))

    dmod_all = packs[:, :n_mod * d // LANES, :].reshape(N_DEV, n_mod * d)
    g_w_ada = _ada_wgrad(c_all, lax.dynamic_slice(dmod_all, (0, j_me * mod_sh), (N_DEV, mod_sh)))

    out_g, out_d, out_m, out_v = {}, {}, {}, {}
    big = {"w_ada": (w_ada[0], g_w_ada, m_w_ada[0], v_w_ada[0]),
           "w_out": (w_out[0], g_w_out, m_w_out[0], v_w_out[0]),
           "w_ffn_in": (w_ffn_in[0], g_w_ffn_in, m_w_ffn_in[0], v_w_ffn_in[0]),
           "w_ffn_out": (w_ffn_out[0], g_w_ffn_out, m_w_ffn_out[0], v_w_ffn_out[0])}
    for nm, (w, g, m, v) in big.items():
        out_g[nm], out_d[nm], out_m[nm], out_v[nm] = [o[None] for o in _adamw(w, g, m, v, "adamw_" + nm)]
    out_g["w_in"], out_d["w_in"], out_m["w_in"], out_v["w_in"] = [
        o.T[None] for o in _adamw(w_in_t, g_w_in_t, m_w_in_t, v_w_in_t, "adamw_w_in")]
    small = {"b_ada": (b_ada, gb_ada, m_b_ada, v_b_ada), "g_mix": (g_mix, gg_mix, m_g_mix, v_g_mix),
             "b_in": (b_in, gb_in, m_b_in, v_b_in), "sinks": (sinks, gsinks, m_sinks, v_sinks),
             "conv_w": (conv_w, gcw_sh, m_conv_w, v_conv_w), "g_ffn": (g_ffn, gg_ffn, m_g_ffn, v_g_ffn),
             "g_final": (g_final, gg_final, m_g_final, v_g_final)}
    def two_d(a):
        return a.reshape(-1, a.shape[-1])

    s_out = _adamw_small([tuple(two_d(a.reshape(w.shape)) for a in (w, g, m, v)) for w, g, m, v in small.values()])
    for (nm, (w, g, _, _)), res in zip(small.items(), s_out):
        out_g[nm] = g.reshape(w.shape)
        out_d[nm], out_m[nm], out_v[nm] = [o.reshape(w.shape) for o in res]

    order = ["w_ada", "b_ada", "g_mix", "w_in", "b_in", "sinks", "conv_w", "w_out", "g_ffn", "w_ffn_in", "w_ffn_out",
             "g_final"]
    return (loss, grad_x[None], *[out_g[k] for k in order], *[out_d[k] for k in order],
            *[out_m[k] for k in order], *[out_v[k] for k in order])
```

```python
import functools

import jax
import jax.numpy as jnp
from jax import lax
from jax.experimental import pallas as pl
from jax.experimental.pallas import tpu as pltpu

F32 = jnp.float32
BF16 = jnp.bfloat16
EPS = 1e-6
HEAD_DIM = 64
GROUP = 8
BLOCK = 128
LANES = 128
SUBLANES_BF16 = 16
N_DEV = 8
N_CHIP = 4
VMEM_LIMIT = 56 * 1024 * 1024
MESH = pl.DeviceIdType.MESH

ADAM_LR = 0.001
ADAM_B1 = 0.9
ADAM_B2 = 0.999
ADAM_EPS = 1e-08
ADAM_WD = 0.01
ADAM_STEP = 10

SDS = jax.ShapeDtypeStruct
ANY = pl.BlockSpec(memory_space=pl.ANY)
VMEM_SPEC = pl.BlockSpec(memory_space=pltpu.VMEM)
SMEM_SPEC = pl.BlockSpec(memory_space=pltpu.SMEM)


def _params(*sem):
    return pltpu.CompilerParams(dimension_semantics=sem, vmem_limit_bytes=VMEM_LIMIT)


def _vec(v, d):
    arr, k = v if isinstance(v, tuple) else (v, 0)
    return arr, pl.BlockSpec((1, d), lambda *_: (0, k))


def _mesh_pos():
    return lax.axis_index("x"), lax.axis_index("y"), lax.axis_index("c")


def _row_tile(rows, cols, itemsize=4, budget=1 << 20, mult=8):
    best = None
    for t in range(mult, rows + 1, mult):
        if rows % t == 0 and t * cols * itemsize <= budget:
            best = t
    if best is None:
        best = rows
    return best


def _gather_all(v_ref, out_ref, send_sems, recv_sems, local_sem):
    x, y, c = _mesh_pos()
    me = 4 * x + 2 * y + c
    mine = pltpu.make_async_copy(v_ref, out_ref.at[me], local_sem)
    mine.start()
    peers = []
    for k in range(1, N_DEV):
        px = 1 - x if k & 4 else x
        py = 1 - y if k & 2 else y
        pc = 1 - c if k & 1 else c
        peers.append((px, py, pc))

    def copy(k, block):
        return pltpu.make_async_remote_copy(
            src_ref=v_ref, dst_ref=out_ref.at[block], send_sem=send_sems.at[k], recv_sem=recv_sems.at[k],
            device_id=peers[k], device_id_type=MESH)

    sends = [copy(k, me) for k in range(N_DEV - 1)]
    for cp in sends:
        cp.start()
    for k, (px, py, pc) in enumerate(peers):
        copy(k, 4 * px + 2 * py + pc).wait_recv()
    for cp in sends:
        cp.wait_send()
    mine.wait()


def _small_sems():
    return [pltpu.SemaphoreType.DMA((N_DEV - 1,)), pltpu.SemaphoreType.DMA((N_DEV - 1,)), pltpu.SemaphoreType.DMA]


def _tail_exchange(pack, full):
    def body(pack_ref, full_unused, packs_ref, full_ref, s1, r1, l1, send_sem, recv_sem):
        del full_unused
        x, y, c = _mesh_pos()
        half = full_ref.shape[0] // 2
        rows = pl.ds(pl.multiple_of(c * half, 8), half)
        swap = pltpu.make_async_remote_copy(
            src_ref=full_ref.at[rows], dst_ref=full_ref.at[rows], send_sem=send_sem, recv_sem=recv_sem,
            device_id=(x, y, 1 - c), device_id_type=MESH)
        swap.start()
        _gather_all(pack_ref, packs_ref, s1, r1, l1)
        swap.wait()

    return pl.pallas_call(
        body, name="tail_exchange", out_shape=[SDS((N_DEV,) + pack.shape, pack.dtype), SDS(full.shape, full.dtype)],
        in_specs=[VMEM_SPEC, ANY], out_specs=[VMEM_SPEC, ANY], input_output_aliases={1: 1},
        scratch_shapes=_small_sems() + [pltpu.SemaphoreType.DMA, pltpu.SemaphoreType.DMA])(pack, full)


def _other_chips(x, y):
    return [(1 - x, y), (x, 1 - y), (1 - x, 1 - y)]


def _startup(pack, w_ada_sh, b_ada_sh, w_buf, later):
    d, n = w_ada_sh.shape
    kc = d // LANES
    n_l = len(later)
    chunk_rows = [_row_tile(a.shape[0], a.shape[1], budget=3 << 19, mult=SUBLANES_BF16) for a in later]

    def body(*refs):
        pack_ref, wa_hbm, ba_ref, w_in_unused = refs[:4]
        later_src = refs[4:4 + n_l]
        packs_ref, mine_ref, w_ref = refs[4 + n_l:7 + n_l]
        later_dst = refs[7 + n_l:7 + 2 * n_l]
        wa_scr, mod_scr, mod_ref = refs[7 + 2 * n_l:10 + 2 * n_l]
        f32_bufs = refs[10 + 2 * n_l:10 + 3 * n_l]
        bf16_bufs = refs[10 + 3 * n_l:10 + 4 * n_l]
        (s1, r1, l1, s2, r2, l2, send_sems, recv_sems, fsend_sems, frecv_sems, relay_send, relay_recv, wa_sem,
         cast_sems) = refs[10 + 4 * n_l:]
        del w_in_unused
        x, y, c = _mesh_pos()
        j_me = 2 * x + y
        chips = _other_chips(x, y)
        half = w_ref.shape[1] // 2

        def rows_of(which):
            return pl.ds(pl.multiple_of(which * half, SUBLANES_BF16), half)

        def copy(p, block, rows, over_ici):
            sems = (send_sems, recv_sems) if over_ici else (fsend_sems, frecv_sems)
            return pltpu.make_async_remote_copy(
                src_ref=w_ref.at[block, rows], dst_ref=w_ref.at[block, rows], send_sem=sems[0].at[p],
                recv_sem=sems[1].at[p], device_id=(*chips[p], c) if over_ici else (x, y, 1 - c), device_id_type=MESH)

        def block_of(p):
            return 2 * chips[p][0] + chips[p][1]

        def relay(q, block):
            rows = pl.ds(pl.multiple_of(c * half + q * (half // 2), SUBLANES_BF16), half // 2)
            return pltpu.make_async_remote_copy(
                src_ref=w_ref.at[block, rows], dst_ref=w_ref.at[block, rows], send_sem=relay_send.at[q],
                recv_sem=relay_recv.at[q], device_id=(*chips[1 - q], c), device_id_type=MESH)

        load_wa = pltpu.make_async_copy(wa_hbm, wa_scr, wa_sem)
        load_wa.start()
        _gather_all(pack_ref, packs_ref, s1, r1, l1)
        sends = [copy(p, j_me, rows_of(c), True) for p in range(2)]
        for cp in sends:
            cp.start()
        load_wa.wait()
        acc = jnp.zeros((N_DEV, n), F32)
        for k in range(kc):
            ck = packs_ref[:, k, :]
            sk = (ck * jax.nn.sigmoid(ck)).astype(BF16)
            acc = acc + jnp.dot(sk, wa_scr[k * LANES:(k + 1) * LANES, :].astype(BF16), preferred_element_type=F32)
        mod_scr[...] = acc + ba_ref[...]
        _gather_all(mod_scr, mod_ref, s2, r2, l2)
        for j in range(N_CHIP):
            mine_ref[:, j * n:(j + 1) * n] = mod_ref[2 * j, pl.ds(4 * x + 2 * y + c, 1), :]
        for src, dst, fbuf, bbuf in zip(later_src, later_dst, f32_bufs, bf16_bufs):
            cr = fbuf.shape[0]
            for k in range(src.shape[0] // cr):
                rows = pl.ds(k * cr, cr)
                cin = pltpu.make_async_copy(src.at[rows], fbuf, cast_sems.at[0])
                cin.start()
                cin.wait()
                bbuf[...] = fbuf[...].astype(BF16)
                cout = pltpu.make_async_copy(bbuf, dst.at[j_me, rows], cast_sems.at[1])
                cout.start()
                cout.wait()
        passed = []
        for q in range(2):
            copy(q, block_of(q), rows_of(c), True).wait_recv()
            for cp in (relay(q, block_of(q)), copy(q, block_of(q), rows_of(c), False)):
                cp.start()
                passed.append(cp)
        for q in range(2):
            relay(q, block_of(2)).wait_recv()
        fw = copy(2, block_of(2), rows_of(c), False)
        fw.start()
        for p in range(3):
            copy(p, block_of(p), rows_of(1 - c), False).wait_recv()
        for cp in sends + passed + [fw]:
            cp.wait_send()

    res = pl.pallas_call(
        body, name="startup",
        out_shape=[SDS((N_DEV,) + pack.shape, F32), SDS((1, N_CHIP * n), F32), SDS(w_buf.shape, w_buf.dtype)]
        + [SDS((N_CHIP,) + a.shape, BF16) for a in later],
        in_specs=[VMEM_SPEC, ANY, VMEM_SPEC, ANY] + [ANY] * n_l, out_specs=[VMEM_SPEC, VMEM_SPEC, ANY] + [ANY] * n_l,
        input_output_aliases={3: 2},
        scratch_shapes=[pltpu.VMEM((d, n), F32), pltpu.VMEM((N_DEV, n), F32), pltpu.VMEM((N_DEV, N_DEV, n), F32)]
        + [pltpu.VMEM((cr, a.shape[1]), F32) for cr, a in zip(chunk_rows, later)]
        + [pltpu.VMEM((cr, a.shape[1]), BF16) for cr, a in zip(chunk_rows, later)]
        + _small_sems() + _small_sems()
        + [pltpu.SemaphoreType.DMA((3,))] * 4 + [pltpu.SemaphoreType.DMA((2,))] * 2 + [pltpu.SemaphoreType.DMA]
        + [pltpu.SemaphoreType.DMA((2,))],
        compiler_params=pltpu.CompilerParams(vmem_limit_bytes=VMEM_LIMIT),
    )(pack, w_ada_sh, b_ada_sh, w_buf, *later)
    return res[0], res[1], res[2], list(res[3:])


class _Exchange:
    def __init__(self, operands, out_shape, in_place, n_sems, copies):
        self.operands, self.out_shape, self.in_place, self.n_sems, self.copies = (
            list(operands), list(out_shape), in_place, n_sems, copies)

    def sems(self):
        return [pltpu.SemaphoreType.DMA((self.n_sems,)), pltpu.SemaphoreType.DMA((self.n_sems,))]


def _x_gather_ici(bufs):
    def copies(ins, outs, send_sems, recv_sems):
        x, y, c = _mesh_pos()
        chips = _other_chips(x, y)
        out = []
        for w in range(len(outs)):
            half = outs[w].shape[1] // 2
            rows = pl.ds(pl.multiple_of(c * half, SUBLANES_BF16), half)
            for p in range(3):
                out.append(pltpu.make_async_remote_copy(
                    src_ref=outs[w].at[2 * x + y, rows], dst_ref=outs[w].at[2 * x + y, rows],
                    send_sem=send_sems.at[w * 3 + p], recv_sem=recv_sems.at[w * 3 + p],
                    device_id=(*chips[p], c), device_id_type=MESH))
        return out

    return _Exchange(bufs, [SDS(b.shape, b.dtype) for b in bufs], True, 3 * len(bufs), copies)


def _x_gather_d2d(bufs):
    def copies(ins, outs, send_sems, recv_sems):
        x, y, c = _mesh_pos()
        chips = _other_chips(x, y)
        out = []
        for w in range(len(outs)):
            half = outs[w].shape[1] // 2
            rows = pl.ds(pl.multiple_of(c * half, SUBLANES_BF16), half)
            for p in range(3):
                block = 2 * chips[p][0] + chips[p][1]
                out.append(pltpu.make_async_remote_copy(
                    src_ref=outs[w].at[block, rows], dst_ref=outs[w].at[block, rows],
                    send_sem=send_sems.at[w * 3 + p], recv_sem=recv_sems.at[w * 3 + p],
                    device_id=(x, y, 1 - c), device_id_type=MESH))
        return out

    return _Exchange(bufs, [SDS(b.shape, b.dtype) for b in bufs], True, 3 * len(bufs), copies)


N_REMOTE = 6


def _x_reduce(grads32, grads16):
    n_w = len(grads32)

    def copies(ins, outs, send_sems, recv_sems):
        g32, g16 = ins[:n_w], ins[n_w:]
        from_sib, from_far = outs[:n_w], outs[n_w:]
        x, y, c = _mesh_pos()
        chips = _other_chips(x, y)
        out = []
        for w in range(n_w):
            half = g32[w].shape[1] // 2
            k0 = w * (N_REMOTE + 1)
            out.append(pltpu.make_async_remote_copy(
                src_ref=g32[w].at[2 * x + y, pl.ds(pl.multiple_of((1 - c) * half, SUBLANES_BF16), half), :],
                dst_ref=from_sib[w], send_sem=send_sems.at[k0], recv_sem=recv_sems.at[k0],
                device_id=(x, y, 1 - c), device_id_type=MESH))
            for p in range(3):
                for f in range(2):
                    tc = c if f == 0 else 1 - c
                    k = 2 * p + f
                    out.append(pltpu.make_async_remote_copy(
                        src_ref=g16[w].at[2 * chips[p][0] + chips[p][1],
                                          pl.ds(pl.multiple_of(tc * half, SUBLANES_BF16), half), :],
                        dst_ref=from_far[w].at[k], send_sem=send_sems.at[k0 + 1 + k], recv_sem=recv_sems.at[k0 + 1 + k],
                        device_id=(*chips[p], tc), device_id_type=MESH))
        return out

    shapes = ([SDS((g.shape[1] // 2, g.shape[2]), g.dtype) for g in grads32]
              + [SDS((N_REMOTE, g.shape[1] // 2, g.shape[2]), g.dtype) for g in grads16])
    return _Exchange(list(grads32) + list(grads16), shapes, False, (N_REMOTE + 1) * n_w, copies)


def _x_pair_exchange(fulls):
    def copies(ins, outs, send_sems, recv_sems):
        x, y, c = _mesh_pos()
        out = []
        for w in range(len(outs)):
            half = outs[w].shape[0] // 2
            rows = pl.ds(pl.multiple_of(c * half, 8), half)
            out.append(pltpu.make_async_remote_copy(
                src_ref=outs[w].at[rows], dst_ref=outs[w].at[rows], send_sem=send_sems.at[w],
                recv_sem=recv_sems.at[w], device_id=(x, y, 1 - c), device_id_type=MESH))
        return out

    return _Exchange(fulls, [SDS(f.shape, f.dtype) for f in fulls], True, len(fulls), copies)


def _pallas(body, *, name, grid, in_specs, out_specs, out_shape, args, scratch=(), sem=None, ride=None):
    single = not isinstance(out_specs, (list, tuple))
    out_specs_l = [out_specs] if single else list(out_specs)
    out_shape_l = [out_shape] if single else list(out_shape)
    n_in, n_out, n_scr = len(in_specs), len(out_specs_l), len(scratch)
    if ride is None:
        res = pl.pallas_call(body, name=name, grid=grid, in_specs=list(in_specs), out_specs=out_specs,
                             out_shape=out_shape, scratch_shapes=list(scratch), compiler_params=_params(*sem))(*args)
        return res, None
    n_x, n_xo = len(ride.operands), len(ride.out_shape)

    def full_body(*refs):
        ins, x_ins = refs[:n_in], refs[n_in:n_in + n_x]
        outs = refs[n_in + n_x:n_in + n_x + n_out]
        x_outs = refs[n_in + n_x + n_out:n_in + n_x + n_out + n_xo]
        rest = refs[n_in + n_x + n_out + n_xo:]
        scr, (send_sems, recv_sems) = rest[:n_scr], rest[n_scr:]
        first = functools.reduce(jnp.logical_and, [pl.program_id(a) == 0 for a in range(len(grid))])
        last = functools.reduce(jnp.logical_and, [pl.program_id(a) == grid[a] - 1 for a in range(len(grid))])

        @pl.when(first)
        def _():
            for cp in ride.copies(x_ins, x_outs, send_sems, recv_sems):
                cp.start()

        body(*ins, *outs, *scr)

        @pl.when(last)
        def _():
            for cp in ride.copies(x_ins, x_outs, send_sems, recv_sems):
                cp.wait()

    res = pl.pallas_call(
        full_body, name=name, grid=grid, in_specs=list(in_specs) + [ANY] * n_x,
        out_specs=out_specs_l + [ANY] * n_xo, out_shape=out_shape_l + ride.out_shape,
        input_output_aliases={n_in + k: n_out + k for k in range(n_x)} if ride.in_place else {},
        scratch_shapes=list(scratch) + ride.sems(),
        compiler_params=_params(*(["arbitrary"] * len(grid))))(*args, *ride.operands)
    own = res[0] if single else list(res[:n_out])
    return own, list(res[n_out:])


def _cast_into_block(pos, w, name):
    rows, cols = w.shape
    tr = _row_tile(rows, cols, mult=SUBLANES_BF16)

    def body(pos_ref, w_ref, o_ref):
        del pos_ref
        o_ref[...] = w_ref[...].astype(BF16)

    return pl.pallas_call(
        body, name=name,
        grid_spec=pltpu.PrefetchScalarGridSpec(
            num_scalar_prefetch=1, grid=(rows // tr,),
            in_specs=[pl.BlockSpec((tr, cols), lambda i, pos_ref: (i, 0))],
            out_specs=pl.BlockSpec((None, tr, cols), lambda i, pos_ref: (pos_ref[1], i, 0))),
        out_shape=SDS((N_CHIP, rows, cols), BF16), compiler_params=_params("parallel"))(pos, w)


def _sum_terms(pos, grad, from_sib, from_far, name):
    _, rows, cols = grad.shape
    half = rows // 2
    tr = _row_tile(half, cols, mult=SUBLANES_BF16)
    nblk = half // tr

    def body(pos_ref, g_ref, s_ref, r_ref, o_ref):
        del pos_ref
        acc = g_ref[...] + s_ref[...]
        for k in range(N_REMOTE):
            acc = acc + r_ref[k].astype(F32)
        o_ref[...] = acc

    return pl.pallas_call(
        body, name=name,
        grid_spec=pltpu.PrefetchScalarGridSpec(
            num_scalar_prefetch=1, grid=(nblk,),
            in_specs=[pl.BlockSpec((None, tr, cols), lambda i, pos_ref: (pos_ref[1], pos_ref[0] * nblk + i, 0)),
                      pl.BlockSpec((tr, cols), lambda i, pos_ref: (i, 0)),
                      pl.BlockSpec((N_REMOTE, tr, cols), lambda i, pos_ref: (0, i, 0))],
            out_specs=pl.BlockSpec((tr, cols), lambda i, pos_ref: (pos_ref[0] * nblk + i, 0))),
        out_shape=SDS((rows, cols), F32),
        compiler_params=_params("parallel"),
    )(pos, grad, from_sib, from_far)


def _adamw(w, g, m, v, name):
    rows, cols = w.shape
    tr = _row_tile(rows, cols)

    def body(w_ref, g_ref, m_ref, v_ref, go_ref, d_ref, nm_ref, nv_ref):
        go_ref[...] = g_ref[...]
        _adamw_update(w_ref, g_ref, m_ref, v_ref, d_ref, nm_ref, nv_ref)

    spec = pl.BlockSpec((tr, cols), lambda i: (i, 0))
    return pl.pallas_call(body, name=name, grid=(rows // tr,), in_specs=[spec] * 4, out_specs=[spec] * 4,
                          out_shape=[SDS((rows, cols), F32)] * 4, compiler_params=_params("parallel"))(w, g, m, v)


def _adamw_update(w_ref, g_ref, m_ref, v_ref, d_ref, nm_ref, nv_ref):
    gg = g_ref[...]
    nm = ADAM_B1 * m_ref[...] + (1.0 - ADAM_B1) * gg
    nv = ADAM_B2 * v_ref[...] + (1.0 - ADAM_B2) * (gg * gg)
    m_hat = nm / (1.0 - ADAM_B1 ** ADAM_STEP)
    v_hat = nv / (1.0 - ADAM_B2 ** ADAM_STEP)
    d_ref[...] = -ADAM_LR * (m_hat / (jnp.sqrt(v_hat) + ADAM_EPS) + ADAM_WD * w_ref[...])
    nm_ref[...] = nm
    nv_ref[...] = nv


def _adamw_small(params):
    n_p = len(params)

    def body(*refs):
        ins, outs = refs[:4 * n_p], refs[4 * n_p:]
        for k in range(n_p):
            _adamw_update(*ins[4 * k:4 * k + 4], *outs[3 * k:3 * k + 3])

    flat = [a for tup in params for a in tup]
    res = pl.pallas_call(
        body, name="adamw_small", in_specs=[VMEM_SPEC] * (4 * n_p), out_specs=[VMEM_SPEC] * (3 * n_p),
        out_shape=[SDS(tup[0].shape, F32) for tup in params for _ in range(3)])(*flat)
    return [res[3 * k:3 * k + 3] for k in range(n_p)]


def _pack_sum(gathered):
    _, rows, cols = gathered.shape

    def body(g_ref, o_ref):
        acc = g_ref[0]
        for d in range(1, N_DEV):
            acc = acc + g_ref[d]
        o_ref[...] = acc

    return pl.pallas_call(body, name="pack_sum", in_specs=[VMEM_SPEC], out_specs=VMEM_SPEC,
                          out_shape=SDS((rows, cols), F32))(gathered)


def _ada_wgrad(c_all, dmod_sh):
    d = c_all.shape[1]
    n = dmod_sh.shape[1]
    tn = 512

    def body(c_ref, g_ref, o_ref):
        cc = c_ref[...]
        s = cc * jax.nn.sigmoid(cc)
        o_ref[...] = lax.dot_general(s, g_ref[...], (((0,), (0,)), ((), ())), preferred_element_type=F32,
                                     precision=lax.Precision.HIGHEST)

    return pl.pallas_call(
        body, name="ada_wgrad", grid=(n // tn,),
        in_specs=[pl.BlockSpec((N_DEV, d), lambda j: (0, 0)), pl.BlockSpec((N_DEV, tn), lambda j: (0, j))],
        out_specs=pl.BlockSpec((d, tn), lambda j: (0, j)),
        out_shape=SDS((d, n), F32), compiler_params=_params("parallel"))(c_all, dmod_sh)


def _rms(xf):
    return lax.rsqrt(jnp.mean(xf * xf, axis=-1, keepdims=True) + EPS)


def _in_proj(x, g, sc, sh, wt, b, tm, tn, ride=None):
    t, d = x.shape
    n = wt.shape[0]

    def body(x_ref, g_ref, sc_ref, sh_ref, w_ref, b_ref, z_ref, h_ref):
        @pl.when(pl.program_id(1) == 0)
        def _():
            xf = x_ref[...]
            h_ref[...] = ((xf * _rms(xf) * g_ref[...]) * (1.0 + sc_ref[...]) + sh_ref[...]).astype(BF16)

        acc = lax.dot_general(h_ref[...], w_ref[...], (((1,), (1,)), ((), ())), preferred_element_type=F32)
        z_ref[...] = (acc + b_ref[...]).astype(BF16)

    row = pl.BlockSpec((tm, d), lambda i, j: (i, 0))
    vecs, vec_specs = zip(*[_vec(v, d) for v in (g, sc, sh)])
    return _pallas(
        body, name="in_proj", grid=(t // tm, n // tn),
        in_specs=[row, *vec_specs, pl.BlockSpec((tn, d), lambda i, j: (j, 0)),
                  pl.BlockSpec((1, tn), lambda i, j: (0, j))],
        out_specs=[pl.BlockSpec((tm, tn), lambda i, j: (i, j)), row],
        out_shape=[SDS((t, n), BF16), SDS((t, d), BF16)], args=(x, *vecs, wt, b),
        sem=("parallel", "arbitrary"), ride=ride)


def _segments(d, kvw2):
    o = d + kvw2
    names = ("cb", "cc", "cx", "ga", "gc")
    seg = {nm: slice(o + k * d, o + (k + 1) * d) for k, nm in enumerate(names)}
    seg["q"], seg["kv"] = slice(0, d), slice(d, o)
    return seg


def _attn_masks():
    rows = 4 * BLOCK
    r = lax.broadcasted_iota(jnp.int32, (rows, 2 * BLOCK), 0) & (BLOCK - 1)
    col = lax.broadcasted_iota(jnp.int32, (rows, 2 * BLOCK), 1)
    return (col > r) & (col <= r + BLOCK), col


def _kv_variants(kv, n_kv_w):
    assert n_kv_w == LANES
    kb, vb = kv[:, :LANES] * (HEAD_DIM ** -0.5), kv[:, LANES:]
    kr, vr = pltpu.roll(kb, HEAD_DIM, 1), pltpu.roll(vb, HEAD_DIM, 1)
    lane = lax.broadcasted_iota(jnp.int32, kb.shape, 1)
    lo = lane < HEAD_DIM
    zero = jnp.zeros_like(kb)
    k_eff = [[None, None], [None, None]]
    v_eff = [[None, None], [None, None]]
    for h in range(2):
        for e in range(2):
            ksrc, vsrc = (kb, vb) if e == h else (kr, vr)
            keep = lo if e == 0 else jnp.logical_not(lo)
            k_eff[h][e] = jnp.where(keep, ksrc, zero)
            v_eff[h][e] = jnp.where(keep, vsrc, zero)
    return k_eff, v_eff


def _sink_column(sinks_ref, h, e):
    rowblk = lax.broadcasted_iota(jnp.int32, (4 * BLOCK, 1), 0) // BLOCK
    col = jnp.zeros((4 * BLOCK, 1), F32)
    for j in range(4):
        col = jnp.where(rowblk == j, sinks_ref[0, GROUP * h + 2 * j + e], col)
    return col


def _softmax_sink(s, valid, sink):
    s = jnp.where(valid, s, -jnp.inf)
    m = jnp.maximum(jnp.max(s, axis=-1, keepdims=True), sink)
    p = jnp.exp(s - m)
    psink = jnp.exp(sink - m)
    den = jnp.sum(p, axis=-1, keepdims=True) + psink
    inv = 1.0 / den
    return p * inv, psink * inv


def _shift_down(a, s, prev):
    rows = a.shape[0]
    out = pltpu.roll(a, s, 0)
    row = lax.broadcasted_iota(jnp.int32, a.shape, 0)
    for t in range(s):
        out = jnp.where(row == t, prev[SUBLANES_BF16 - s + t:SUBLANES_BF16 - s + t + 1, :], out)
    del rows
    return out


def _shift_up(a, s, nxt):
    rows = a.shape[0]
    out = pltpu.roll(a, rows - s, 0)
    row = lax.broadcasted_iota(jnp.int32, a.shape, 0)
    for t in range(s):
        out = jnp.where(row == rows - s + t, nxt[t:t + 1, :], out)
    return out


def _stack_pairs(ref, h, rows=slice(None)):
    return jnp.concatenate([ref[rows, (4 * h + j) * LANES:(4 * h + j + 1) * LANES] for j in range(4)], axis=0)


FWD_BLOCKS = 4


def _mixer_fwd(z, sinks, conv_w, d, ride=None):
    t, zw = z.shape
    kvw2 = zw - 6 * d
    tq = FWD_BLOCKS * BLOCK
    halo = tq // SUBLANES_BF16
    seg = _segments(d, kvw2)

    def body(z_ref, kvp_ref, prev_ref, sinks_ref, cw_ref, attn_ref, merged_ref):
        n = pl.program_id(0)
        band, col = _attn_masks()
        for b in range(FWD_BLOCKS):
            rows = slice(b * BLOCK, (b + 1) * BLOCK)
            before = slice((b - 1) * BLOCK, b * BLOCK)
            kv_prev = kvp_ref[...] if b == 0 else z_ref[before, seg["kv"]]
            kv = jnp.concatenate([kv_prev, z_ref[rows, seg["kv"]]], axis=0)
            k_eff, v_eff = _kv_variants(kv, kvw2 // 2)
            valid = band & ((n > 0) | (col >= BLOCK)) if b == 0 else band
            for h in range(2):
                q4 = _stack_pairs(z_ref, h, rows)
                o4 = jnp.zeros((4 * BLOCK, LANES), F32)
                for e in range(2):
                    s = lax.dot_general(q4, k_eff[h][e], (((1,), (1,)), ((), ())), preferred_element_type=F32)
                    p, _ = _softmax_sink(s, valid, _sink_column(sinks_ref, h, e))
                    o4 = o4 + jnp.dot(p.astype(BF16), v_eff[h][e], preferred_element_type=F32)
                for j in range(4):
                    attn_ref[rows, (4 * h + j) * LANES:(4 * h + j + 1) * LANES] = (
                        o4[j * BLOCK:(j + 1) * BLOCK].astype(BF16))
            cb = z_ref[rows, seg["cb"]].astype(F32)
            p_in = z_ref[rows, seg["cc"]].astype(F32) * z_ref[rows, seg["cx"]].astype(F32)
            if b == 0:
                prev = jnp.where(n > 0, prev_ref[:, seg["cc"]].astype(F32) * prev_ref[:, seg["cx"]].astype(F32), 0.0)
            else:
                tail = slice(b * BLOCK - SUBLANES_BF16, b * BLOCK)
                prev = z_ref[tail, seg["cc"]].astype(F32) * z_ref[tail, seg["cx"]].astype(F32)
            cconv = (cw_ref[0:1, :] * _shift_down(p_in, 2, prev) + cw_ref[1:2, :] * _shift_down(p_in, 1, prev)
                     + cw_ref[2:3, :] * p_in)
            sa = jax.nn.sigmoid(z_ref[rows, seg["ga"]].astype(F32))
            sg = jax.nn.sigmoid(z_ref[rows, seg["gc"]].astype(F32))
            merged_ref[rows, :] = (sa * attn_ref[rows, :].astype(F32) + sg * (cb * cconv)).astype(BF16)

    blk = pl.BlockSpec((tq, d), lambda n: (n, 0))
    return _pallas(
        body, name="mixer_fwd", grid=(t // tq,),
        in_specs=[pl.BlockSpec((tq, zw), lambda n: (n, 0)),
                  pl.BlockSpec((BLOCK, kvw2), lambda n: (jnp.maximum(n * FWD_BLOCKS - 1, 0), d // kvw2)),
                  pl.BlockSpec((SUBLANES_BF16, zw), lambda n: (jnp.maximum(n * halo - 1, 0), 0)),
                  SMEM_SPEC, pl.BlockSpec((3, d), lambda n: (0, 0))],
        out_specs=[blk, blk],
        out_shape=[SDS((t, d), BF16), SDS((t, d), BF16)],
        args=(z, z, z, sinks, conv_w), sem=("parallel",), ride=ride)


def _out_proj_fwd(merged, w_out, x, ga1, g_ffn, sc2, sh2, tm):
    t, d = x.shape

    def body(m_ref, w_ref, x_ref, ga_ref, g_ref, sc_ref, sh_ref, y_ref, x1_ref, h_ref):
        y = jnp.dot(m_ref[...], w_ref[...], preferred_element_type=F32)
        x1 = x_ref[...] + ga_ref[...] * y
        y_ref[...] = y.astype(BF16)
        x1_ref[...] = x1
        h_ref[...] = ((x1 * _rms(x1) * g_ref[...]) * (1.0 + sc_ref[...]) + sh_ref[...]).astype(BF16)

    row = pl.BlockSpec((tm, d), lambda i: (i, 0))
    vecs, vec_specs = zip(*[_vec(v, d) for v in (ga1, g_ffn, sc2, sh2)])
    return pl.pallas_call(
        body, name="out_proj_fwd", grid=(t // tm,),
        in_specs=[row, pl.BlockSpec((d, d), lambda i: (0, 0)), row, *vec_specs],
        out_specs=[row, row, row],
        out_shape=[SDS((t, d), BF16), SDS((t, d), F32), SDS((t, d), BF16)],
        compiler_params=_params("parallel"))(merged, w_out, x, *vecs)


def _ffn_in_fwd(h2, w, ff, tm, tn):
    t, d = h2.shape
    nj = ff // tn
    assert w.shape == (2 * nj, d, tn)

    def body(h_ref, wg_ref, wu_ref, gu_ref, act_ref):
        hh = h_ref[...]
        g = jnp.dot(hh, wg_ref[...], preferred_element_type=F32)
        u = jnp.dot(hh, wu_ref[...], preferred_element_type=F32)
        sg = jax.nn.sigmoid(g)
        silu = g * sg
        gu_ref[0] = (u * (sg + silu * (1.0 - sg))).astype(BF16)
        gu_ref[1] = silu.astype(BF16)
        act_ref[...] = (silu * u).astype(BF16)

    return pl.pallas_call(
        body, name="ffn_in_fwd", grid=(nj, t // tm),
        in_specs=[pl.BlockSpec((tm, d), lambda j, i: (i, 0)), pl.BlockSpec((None, d, tn), lambda j, i: (j, 0, 0)),
                  pl.BlockSpec((None, d, tn), lambda j, i: (j + nj, 0, 0))],
        out_specs=[pl.BlockSpec((2, tm, tn), lambda j, i: (0, i, j)), pl.BlockSpec((tm, tn), lambda j, i: (i, j))],
        out_shape=[SDS((2, t, ff), BF16), SDS((t, ff), BF16)],
        compiler_params=_params("parallel", "parallel"))(h2, w, w)


def _ffn_out_loss(act, w, x1, target, ga2, g_final, tm):
    t, d = x1.shape
    ff = act.shape[1]

    def body(a_ref, w_ref, x1_ref, tg_ref, ga_ref, gf_ref, dx2_ref, dy2_ref, st_ref):
        @pl.when(pl.program_id(0) == 0)
        def _():
            st_ref[...] = jnp.zeros_like(st_ref)

        halves = [slice(k * (tm // 2), (k + 1) * (tm // 2)) for k in range(2)]
        y2s = [jnp.dot(a_ref[rows, :], w_ref[...], preferred_element_type=F32) for rows in halves]
        for rows, y2 in zip(halves, y2s):
            x2 = x1_ref[rows, :] + ga_ref[...] * y2
            r = _rms(x2)
            yn = x2 * r
            err = yn * gf_ref[...] - tg_ref[rows, :]
            loss = 0.5 * jnp.sum(jnp.mean(err * err, axis=-1, keepdims=True), axis=0, keepdims=True)
            dy = err * (1.0 / d)
            u = dy * gf_ref[...]
            dx2 = r * (u - yn * jnp.mean(u * yn, axis=-1, keepdims=True))
            dx2_ref[rows, :] = dx2
            dy2_ref[rows, :] = (ga_ref[...] * dx2).astype(BF16)
            st_ref[0:1, :] += jnp.sum(dx2 * y2, axis=0, keepdims=True)
            st_ref[1:2, :] += jnp.sum(dy * yn, axis=0, keepdims=True)
            st_ref[2:3, :] += jnp.broadcast_to(loss, (1, d))

    row = pl.BlockSpec((tm, d), lambda i: (i, 0))
    vecs, vec_specs = zip(*[_vec(v, d) for v in (ga2, g_final)])
    return pl.pallas_call(
        body, name="ffn_out_loss", grid=(t // tm,),
        in_specs=[pl.BlockSpec((tm, ff), lambda i: (i, 0)),
                  pl.BlockSpec((ff, d), lambda i: (0, 0), pipeline_mode=pl.Buffered(1)), row, row, *vec_specs],
        out_specs=[row, row, pl.BlockSpec((8, d), lambda i: (0, 0))],
        out_shape=[SDS((t, d), F32), SDS((t, d), BF16), SDS((8, d), F32)],
        compiler_params=_params("arbitrary"))(act, w, x1, target, *vecs)


def _ffn_out_bwd(dy2, w, gu, tm, tn):
    t, d = dy2.shape
    ff = w.shape[0]

    def body(dy_ref, w_ref, gu_ref, o_ref):
        dy = dy_ref[...]
        for lo in range(0, tn, 3 * LANES):
            cols = slice(lo, min(lo + 3 * LANES, tn))
            dact = lax.dot_general(dy, w_ref[cols, :], (((1,), (1,)), ((), ())), preferred_element_type=F32)
            o_ref[0, :, cols] = (dact * gu_ref[0, :, cols].astype(F32)).astype(BF16)
            o_ref[1, :, cols] = (dact * gu_ref[1, :, cols].astype(F32)).astype(BF16)

    gu_spec = pl.BlockSpec((2, tm, tn), lambda j, i: (0, i, j))
    return pl.pallas_call(
        body, name="ffn_out_bwd", grid=(ff // tn, t // tm),
        in_specs=[pl.BlockSpec((tm, d), lambda j, i: (i, 0)), pl.BlockSpec((tn, d), lambda j, i: (j, 0)), gu_spec],
        out_specs=gu_spec, out_shape=SDS((2, t, ff), BF16),
        compiler_params=_params("parallel", "parallel"))(dy2, w, gu)


def _wgrad(a, b, a_spec, b_spec, out_spec, out_shape, grid, name, ride=None):
    def body(a_ref, b_ref, o_ref, o16_ref):
        k = pl.program_id(len(grid) - 1)

        @pl.when(k == 0)
        def _():
            o_ref[...] = jnp.zeros_like(o_ref)

        o_ref[...] += lax.dot_general(a_ref[...], b_ref[...], (((0,), (0,)), ((), ())), preferred_element_type=F32)

        @pl.when(k == grid[-1] - 1)
        def _():
            o16_ref[...] = o_ref[...].astype(BF16)

    return _pallas(
        body, name=name, grid=grid, in_specs=[a_spec, b_spec], out_specs=[out_spec, out_spec],
        out_shape=[out_shape, SDS(out_shape.shape, BF16)], args=(a, b),
        sem=["parallel"] * (len(grid) - 1) + ["arbitrary"], ride=ride)


def _ffn_in_bwd(dgu, w, x1, dx2, y1, g_ffn, sc2, ga1, tm):
    t, d = x1.shape
    ff = dgu.shape[2]
    n_sh, _, sw = w.shape
    per = ff // sw
    nt = (((1,), (1,)), ((), ()))

    def body(a_ref, w_ref, x1_ref, dx2_ref, y1_ref, g_ref, sc_ref, ga_ref, dx1_ref, dy1_ref, st_ref):
        @pl.when(pl.program_id(0) == 0)
        def _():
            st_ref[...] = jnp.zeros_like(st_ref)

        dh = None
        for j in range(n_sh):
            part = lax.dot_general(a_ref[j // per, :, (j % per) * sw:(j % per + 1) * sw], w_ref[j], nt,
                                   preferred_element_type=F32)
            dh = part if dh is None else dh + part
        x1 = x1_ref[...]
        r = _rms(x1)
        xn = x1 * r
        g = g_ref[...]
        dn = dh * (1.0 + sc_ref[...])
        u = dn * g
        dx1 = dx2_ref[...] + r * (u - xn * jnp.mean(u * xn, axis=-1, keepdims=True))
        dx1_ref[...] = dx1
        dy1_ref[...] = (ga_ref[...] * dx1).astype(BF16)
        st_ref[0:1, :] += jnp.sum(dh, axis=0, keepdims=True)
        st_ref[1:2, :] += jnp.sum(dh * (xn * g), axis=0, keepdims=True)
        st_ref[2:3, :] += jnp.sum(dn * xn, axis=0, keepdims=True)
        st_ref[3:4, :] += jnp.sum(dx1 * y1_ref[...].astype(F32), axis=0, keepdims=True)

    row = pl.BlockSpec((tm, d), lambda i: (i, 0))
    vecs, vec_specs = zip(*[_vec(v, d) for v in (g_ffn, sc2, ga1)])
    return pl.pallas_call(
        body, name="ffn_in_bwd", grid=(t // tm,),
        in_specs=[pl.BlockSpec((2, tm, ff), lambda i: (0, i, 0)),
                  pl.BlockSpec((n_sh, d, sw), lambda i: (0, 0, 0), pipeline_mode=pl.Buffered(1)),
                  row, row, row, *vec_specs],
        out_specs=[row, row, pl.BlockSpec((8, d), lambda i: (0, 0))],
        out_shape=[SDS((t, d), F32), SDS((t, d), BF16), SDS((8, d), F32)],
        compiler_params=_params("arbitrary"))(dgu, w, x1, dx2, y1, *vecs)


def _out_proj_bwd(dy1, w_out, tm, ride=None):
    t, d = dy1.shape

    def body(dy_ref, w_ref, o_ref):
        o_ref[...] = lax.dot_general(dy_ref[...], w_ref[...], (((1,), (1,)), ((), ())),
                                     preferred_element_type=F32).astype(BF16)

    row = pl.BlockSpec((tm, d), lambda i: (i, 0))
    return _pallas(body, name="out_proj_bwd", grid=(t // tm,),
                   in_specs=[row, pl.BlockSpec((d, d), lambda i: (0, 0))], out_specs=row,
                   out_shape=SDS((t, d), BF16), args=(dy1, w_out), sem=("parallel",), ride=ride)


BWD_BLOCKS = 2


def _mixer_bwd(z, dmerged, attn, sinks, conv_w, d, ride=None):
    t, zw = z.shape
    kvw2 = zw - 6 * d
    tq = BWD_BLOCKS * BLOCK
    steps = t // tq
    halo = tq // SUBLANES_BF16
    last_halo = t // SUBLANES_BF16 - 1
    scale = HEAD_DIM ** -0.5
    seg = _segments(d, kvw2)

    def body(z_ref, kvp_ref, prev_ref, next_ref, dm_ref, dmn_ref, attn_ref, sinks_ref, cw_ref,
             dz_ref, dkv_ref, db_ref, dbkv_ref, dcw_ref, dsk_ref, carry_ref):
        n = pl.program_id(0)

        @pl.when(n == 0)
        def _():
            carry_ref[...] = jnp.zeros_like(carry_ref)
            db_ref[...] = jnp.zeros_like(db_ref)
            dbkv_ref[...] = jnp.zeros_like(dbkv_ref)
            dcw_ref[...] = jnp.zeros_like(dcw_ref)
            dsk_ref[...] = jnp.zeros_like(dsk_ref)

        def one_block(b, pending):
            rows = slice(b * BLOCK, (b + 1) * BLOCK)
            before = slice((b - 1) * BLOCK, b * BLOCK)
            dm = dm_ref[rows, :].astype(F32)
            sa = jax.nn.sigmoid(z_ref[rows, seg["ga"]].astype(F32))
            dga = dm * attn_ref[rows, :].astype(F32) * sa * (1.0 - sa)
            dz_ref[rows, seg["ga"]] = dga.astype(BF16)
            db_ref[0:1, seg["ga"]] += jnp.sum(dga, axis=0, keepdims=True)
            dattn = (dm * sa).astype(BF16)

            kv_prev = kvp_ref[...] if b == 0 else z_ref[before, seg["kv"]]
            kv = jnp.concatenate([kv_prev, z_ref[rows, seg["kv"]]], axis=0)
            k_eff, v_eff = _kv_variants(kv, kvw2 // 2)
            band, col = _attn_masks()
            valid = band & ((n > 0) | (col >= BLOCK)) if b == 0 else band
            lane_lo = lax.broadcasted_iota(jnp.int32, (2 * BLOCK, LANES), 1) < HEAD_DIM
            sink_lane = lax.broadcasted_iota(jnp.int32, (1, LANES), 1)
            rowblk = lax.broadcasted_iota(jnp.int32, (4 * BLOCK, 1), 0) // BLOCK
            dk_acc = [jnp.zeros((2 * BLOCK, LANES), F32), jnp.zeros((2 * BLOCK, LANES), F32)]
            dv_acc = [jnp.zeros((2 * BLOCK, LANES), F32), jnp.zeros((2 * BLOCK, LANES), F32)]
            dsink = jnp.zeros((1, LANES), F32)
            for h in range(2):
                q4 = _stack_pairs(z_ref, h, rows)
                do4 = jnp.concatenate([dattn[:, (4 * h + j) * LANES:(4 * h + j + 1) * LANES] for j in range(4)],
                                      axis=0)
                dq4 = jnp.zeros((4 * BLOCK, LANES), F32)
                for e in range(2):
                    s = lax.dot_general(q4, k_eff[h][e], (((1,), (1,)), ((), ())), preferred_element_type=F32)
                    p, psink = _softmax_sink(s, valid, _sink_column(sinks_ref, h, e))
                    dp = lax.dot_general(do4, v_eff[h][e], (((1,), (1,)), ((), ())), preferred_element_type=F32)
                    delta = jnp.sum(p * dp, axis=-1, keepdims=True)
                    ds = (p * (dp - delta)).astype(BF16)
                    dq4 = dq4 + jnp.dot(ds, k_eff[h][e], preferred_element_type=F32)
                    dk = lax.dot_general(q4, ds, (((0,), (0,)), ((), ())), preferred_element_type=F32).T
                    dv = lax.dot_general(do4, p.astype(BF16), (((0,), (0,)), ((), ())), preferred_element_type=F32).T
                    keep = lane_lo if e == 0 else jnp.logical_not(lane_lo)
                    slot = 0 if e == h else 1
                    dk_acc[slot] = dk_acc[slot] + jnp.where(keep, dk, 0.0)
                    dv_acc[slot] = dv_acc[slot] + jnp.where(keep, dv, 0.0)
                    dsk = -(psink * delta)
                    for j in range(4):
                        tot = jnp.sum(jnp.where(rowblk == j, dsk, 0.0), axis=0, keepdims=True)
                        dsink = dsink + jnp.where(sink_lane == GROUP * h + 2 * j + e, tot, 0.0)
                for j in range(4):
                    cols = slice((4 * h + j) * LANES, (4 * h + j + 1) * LANES)
                    dqj = dq4[j * BLOCK:(j + 1) * BLOCK]
                    dz_ref[rows, cols] = dqj.astype(BF16)
                    db_ref[0:1, cols] += jnp.sum(dqj, axis=0, keepdims=True)
            dsk_ref[0:1, :] += dsink
            dkv_new = jnp.concatenate([(dk_acc[0] + pltpu.roll(dk_acc[1], HEAD_DIM, 1)) * scale,
                                       dv_acc[0] + pltpu.roll(dv_acc[1], HEAD_DIM, 1)], axis=1)
            done = pending + dkv_new[:BLOCK]
            dkv_ref[rows, :] = done.astype(BF16)
            dbkv_ref[0:1, :] += jnp.sum(done, axis=0, keepdims=True)

            cb = z_ref[rows, seg["cb"]].astype(F32)
            cc = z_ref[rows, seg["cc"]].astype(F32)
            cx = z_ref[rows, seg["cx"]].astype(F32)
            sg = jax.nn.sigmoid(z_ref[rows, seg["gc"]].astype(F32))
            p_in = cc * cx
            if b == 0:
                prev = jnp.where(n > 0, prev_ref[:, seg["cc"]].astype(F32) * prev_ref[:, seg["cx"]].astype(F32), 0.0)
            else:
                tail = slice(b * BLOCK - SUBLANES_BF16, b * BLOCK)
                prev = z_ref[tail, seg["cc"]].astype(F32) * z_ref[tail, seg["cx"]].astype(F32)
            p_m1 = _shift_down(p_in, 1, prev)
            p_m2 = _shift_down(p_in, 2, prev)
            w0, w1, w2 = cw_ref[0:1, :], cw_ref[1:2, :], cw_ref[2:3, :]
            cconv = w0 * p_m2 + w1 * p_m1 + w2 * p_in
            dconv = dm * sg
            dgc = dm * (cb * cconv) * sg * (1.0 - sg)
            dcb = dconv * cconv
            dcc_t = dconv * cb
            if b == BWD_BLOCKS - 1:
                nxt = jnp.where(n < steps - 1,
                                dmn_ref[...].astype(F32) * jax.nn.sigmoid(next_ref[:, seg["gc"]].astype(F32))
                                * next_ref[:, seg["cb"]].astype(F32), 0.0)
            else:
                head = slice((b + 1) * BLOCK, (b + 1) * BLOCK + SUBLANES_BF16)
                nxt = (dm_ref[head, :].astype(F32) * jax.nn.sigmoid(z_ref[head, seg["gc"]].astype(F32))
                       * z_ref[head, seg["cb"]].astype(F32))
            dpin = w2 * dcc_t + w1 * _shift_up(dcc_t, 1, nxt) + w0 * _shift_up(dcc_t, 2, nxt)
            for nm, val in (("cb", dcb), ("cc", dpin * cx), ("cx", dpin * cc), ("gc", dgc)):
                dz_ref[rows, seg[nm]] = val.astype(BF16)
                db_ref[0:1, seg[nm]] += jnp.sum(val, axis=0, keepdims=True)
            dcw_ref[0:1, :] += jnp.sum(dcc_t * p_m2, axis=0, keepdims=True)
            dcw_ref[1:2, :] += jnp.sum(dcc_t * p_m1, axis=0, keepdims=True)
            dcw_ref[2:3, :] += jnp.sum(dcc_t * p_in, axis=0, keepdims=True)
            return dkv_new[BLOCK:]

        @pl.when(n < steps)
        def _():
            pending = carry_ref[...]
            for b in range(BWD_BLOCKS):
                pending = one_block(b, pending)
            carry_ref[...] = pending

        @pl.when(n == steps)
        def _():
            done = carry_ref[...]
            dkv_ref[:BLOCK, :] = done.astype(BF16)
            dkv_ref[BLOCK:, :] = jnp.zeros((tq - BLOCK, kvw2), BF16)
            dbkv_ref[0:1, :] += jnp.sum(done, axis=0, keepdims=True)

    def cur(n):
        return jnp.minimum(n, steps - 1)

    def after(n):
        return jnp.minimum((cur(n) + 1) * halo, last_halo)

    blk = pl.BlockSpec((tq, d), lambda n: (cur(n), 0))
    return _pallas(
        body, name="mixer_bwd", grid=(steps + 1,), ride=ride, sem=("arbitrary",),
        args=(z, z, z, z, dmerged, dmerged, attn, sinks, conv_w),
        in_specs=[pl.BlockSpec((tq, zw), lambda n: (cur(n), 0)),
                  pl.BlockSpec((BLOCK, kvw2), lambda n: (jnp.maximum(cur(n) * BWD_BLOCKS - 1, 0), d // kvw2)),
                  pl.BlockSpec((SUBLANES_BF16, zw), lambda n: (jnp.maximum(cur(n) * halo - 1, 0), 0)),
                  pl.BlockSpec((SUBLANES_BF16, zw), lambda n: (after(n), 0)),
                  blk,
                  pl.BlockSpec((SUBLANES_BF16, d), lambda n: (after(n), 0)),
                  blk, SMEM_SPEC, pl.BlockSpec((3, d), lambda n: (0, 0))],
        out_specs=[pl.BlockSpec((tq, zw), lambda n: (cur(n), 0)),
                   pl.BlockSpec((tq, kvw2), lambda n: (n, 0)),
                   pl.BlockSpec((8, zw), lambda n: (0, 0)), pl.BlockSpec((8, kvw2), lambda n: (0, 0)),
                   pl.BlockSpec((8, d), lambda n: (0, 0)), pl.BlockSpec((8, LANES), lambda n: (0, 0))],
        out_shape=[SDS((t, zw), BF16), SDS((t + tq, kvw2), BF16), SDS((8, zw), F32), SDS((8, kvw2), F32),
                   SDS((8, d), F32), SDS((8, LANES), F32)],
        scratch=[pltpu.VMEM((BLOCK, kvw2), F32)])


def _wgrad_in(dz, dkv, h1, tk, ride=None):
    t, zw = dz.shape
    d = h1.shape[1]
    kvw2 = dkv.shape[1]
    blk = d + kvw2
    assert zw % blk == 0
    tn = (((0,), (0,)), ((), ()))

    def body(a_ref, akv_ref, h_ref, o_ref, o16_ref):
        n, k = pl.program_id(0), pl.program_id(1)

        @pl.when(k == 0)
        def _():
            o_ref[...] = jnp.zeros_like(o_ref)

        @pl.when(n == 0)
        def _():
            o_ref[:d, :] += lax.dot_general(a_ref[:, :d], h_ref[...], tn, preferred_element_type=F32)
            o_ref[d:, :] += lax.dot_general(akv_ref[...], h_ref[...], tn, preferred_element_type=F32)

        @pl.when(n > 0)
        def _():
            o_ref[...] += lax.dot_general(a_ref[...], h_ref[...], tn, preferred_element_type=F32)

        @pl.when(k == t // tk - 1)
        def _():
            o16_ref[...] = o_ref[...].astype(BF16)

    out_spec = pl.BlockSpec((blk, d), lambda n, k: (n, 0))
    return _pallas(
        body, name="wgrad_in", grid=(zw // blk, t // tk),
        in_specs=[pl.BlockSpec((tk, blk), lambda n, k: (k, n)), pl.BlockSpec((tk, kvw2), lambda n, k: (k, 0)),
                  pl.BlockSpec((tk, d), lambda n, k: (k, 0))],
        out_specs=[out_spec, out_spec], out_shape=[SDS((zw, d), F32), SDS((zw, d), BF16)],
        args=(dz, dkv, h1), sem=("parallel", "arbitrary"), ride=ride)


def _in_proj_bwd(dz, dkv, wt, x, dx1, g_mix, sc1, tm, ride=None):
    t, d = x.shape
    zw = dz.shape[1]
    kvw2 = dkv.shape[1]
    rest = d + kvw2

    def body(a_ref, akv_ref, w_ref, x_ref, dx1_ref, g_ref, sc_ref, gx_ref, st_ref):
        @pl.when(pl.program_id(0) == 0)
        def _():
            st_ref[...] = jnp.zeros_like(st_ref)

        dh = (jnp.dot(a_ref[:, :d], w_ref[:d, :], preferred_element_type=F32)
              + jnp.dot(akv_ref[...], w_ref[d:rest, :], preferred_element_type=F32)
              + jnp.dot(a_ref[:, rest:], w_ref[rest:, :], preferred_element_type=F32))
        xx = x_ref[...]
        r = _rms(xx)
        xn = xx * r
        g = g_ref[...]
        dn = dh * (1.0 + sc_ref[...])
        u = dn * g
        gx_ref[...] = dx1_ref[...] + r * (u - xn * jnp.mean(u * xn, axis=-1, keepdims=True))
        st_ref[0:1, :] += jnp.sum(dh, axis=0, keepdims=True)
        st_ref[1:2, :] += jnp.sum(dh * (xn * g), axis=0, keepdims=True)
        st_ref[2:3, :] += jnp.sum(dn * xn, axis=0, keepdims=True)

    row = pl.BlockSpec((tm, d), lambda i: (i, 0))
    vecs, vec_specs = zip(*[_vec(v, d) for v in (g_mix, sc1)])
    return _pallas(
        body, name="in_proj_bwd", grid=(t // tm,),
        in_specs=[pl.BlockSpec((tm, zw), lambda i: (i, 0)), pl.BlockSpec((tm, kvw2), lambda i: (i, 0)),
                  pl.BlockSpec((zw, d), lambda i: (0, 0), pipeline_mode=pl.Buffered(1)),
                  row, row, *vec_specs],
        out_specs=[row, pl.BlockSpec((8, d), lambda i: (0, 0))],
        out_shape=[SDS((t, d), F32), SDS((8, d), F32)],
        args=(dz, dkv, wt, x, dx1, *vecs), sem=("arbitrary",), ride=ride)


def _to_lanes(v, rows=None):
    flat = v.reshape(-1)
    need = -(-flat.shape[0] // LANES)
    need = -(-need // 8) * 8 if rows is None else rows
    return jnp.pad(flat, (0, need * LANES - flat.shape[0])).reshape(need, LANES)


def kernel(x, c, w_ada, b_ada, g_mix, w_in, b_in, sinks, conv_w, w_out, g_ffn, w_ffn_in, w_ffn_out, g_final, loss_target, m_w_ada, m_b_ada, m_g_mix, m_w_in, m_b_in, m_sinks, m_conv_w, m_w_out, m_g_ffn, m_w_ffn_in, m_w_ffn_out, m_g_final, v_w_ada, v_b_ada, v_g_mix, v_w_in, v_b_in, v_sinks, v_conv_w, v_w_out, v_g_ffn, v_w_ffn_in, v_w_ffn_out, v_g_final):
    xs, tgt = x[0], loss_target[0]
    t, d = xs.shape
    zw = w_in.shape[2] * N_CHIP
    kvw2 = zw - 6 * d
    ff = w_ffn_out.shape[1] * N_CHIP
    n_mod = w_ada.shape[2] * N_CHIP // d
    mod_sh = w_ada.shape[2]
    cw_sh = conv_w.shape[2]
    assert d % (8 * LANES) == 0 and kvw2 == 2 * LANES and t % 512 == 0 and n_mod == 6
    xi, yi, ci = _mesh_pos()
    j_me = 2 * xi + yi
    pos = jnp.stack([ci, j_me]).astype(jnp.int32)
    tm = 512

    w_in_t, m_w_in_t, v_w_in_t = w_in[0].T, m_w_in[0].T, v_w_in[0].T
    assert d == 8 * LANES
    pack1 = jnp.concatenate([c.reshape(d // LANES, LANES), conv_w[0].reshape(-1, LANES)], axis=0)
    pack1 = jnp.pad(pack1, ((0, 16 - pack1.shape[0]), (0, 0)))
    b_ada_sh = lax.dynamic_slice(b_ada, (0, j_me * mod_sh), (1, mod_sh))
    g1, mod, w_in_g, later = _startup(pack1, w_ada[0], b_ada_sh, _cast_into_block(pos, w_in_t, "cast_w_in"),
                                      [w_out[0], w_ffn_in[0], w_ffn_out[0]])
    c_all = g1[:, :d // LANES, :].reshape(N_DEV, d)
    cw_rows = 3 * cw_sh // LANES
    conv_w_full = jnp.concatenate(
        [g1[2 * j, d // LANES:d // LANES + cw_rows, :].reshape(3, cw_sh) for j in range(N_CHIP)], axis=1)
    sh1, sc1, ga1, sh2, sc2, ga2 = [(mod, k) for k in range(6)]
    w_in_tf = w_in_g.reshape(zw, d)

    (z, h1), later = _in_proj(xs, g_mix, sc1, sh1, w_in_tf, b_in, min(t, 1024), zw // 5, ride=_x_gather_ici(later))
    (attn, merged), later = _mixer_fwd(z, sinks, conv_w_full, d, ride=_x_gather_d2d(later))
    w_out_f = later[0].reshape(d, d)
    w_ffn_in_f = later[1]
    w_ffn_out_f = later[2].reshape(ff, d)
    tml = min(t, 1024)
    y1, x1, h2 = _out_proj_fwd(merged, w_out_f, xs, ga1, g_ffn, sc2, sh2, tml)
    gu, act = _ffn_in_fwd(h2, w_ffn_in_f, ff, tml, ff // 2)
    dx2, dy2, st_loss = _ffn_out_loss(act, w_ffn_out_f, x1, tgt, ga2, g_final.reshape(1, d), tml)

    dgu = _ffn_out_bwd(dy2, w_ffn_out_f, gu, tml, ff // 2)
    tk = min(t, 2048)
    dw_ffn_out, _ = _wgrad(
        act, dy2, pl.BlockSpec((tk, ff // 2), lambda m, k: (k, m)), pl.BlockSpec((tk, d), lambda m, k: (k, 0)),
        pl.BlockSpec((ff // 2, d), lambda m, k: (m, 0)), SDS((ff, d), F32), (2, t // tk), "wgrad_ffn_out")
    dx1, dy1, st_ffn = _ffn_in_bwd(dgu, w_ffn_in_f, x1, dx2, y1, g_ffn, sc2, ga1, tm)
    dw_ffn_in, _ = _wgrad(
        h2, dgu, pl.BlockSpec((tk, d), lambda n, k: (k, 0)),
        pl.BlockSpec((None, tk, ff // 2), lambda n, k: (n // 2, k, n % 2)),
        pl.BlockSpec((None, d, ff // 2), lambda n, k: (n, 0, 0)), SDS((N_CHIP, d, ff // 2), F32),
        (N_CHIP, t // tk), "wgrad_ffn_in")
    dw_out, _ = _wgrad(
        merged, dy1, pl.BlockSpec((tk, d), lambda m, k: (k, 0)), pl.BlockSpec((tk, d), lambda m, k: (k, 0)),
        pl.BlockSpec((d, d), lambda m, k: (0, 0)), SDS((d, d), F32), (1, t // tk), "wgrad_out")

    early = [[g.reshape(N_CHIP, -1, g.shape[-1]) for g in pair] for pair in (dw_out, dw_ffn_in, dw_ffn_out)]
    early_names = ["w_out", "w_ffn_in", "w_ffn_out"]
    dmerged, _ = _out_proj_bwd(dy1, w_out_f, tml)
    (dz, dkv_shifted, db_z, db_kv, dcw, dsk), terms = _mixer_bwd(
        z, dmerged, attn, sinks, conv_w_full, d, ride=_x_reduce([e[0] for e in early], [e[1] for e in early]))
    dkv = dkv_shifted[BLOCK:BLOCK + t]
    fulls = [_sum_terms(pos, e[0], s, r, "sum_terms_" + nm)
             for e, s, r, nm in zip(early, terms[:3], terms[3:], early_names)]
    dw_in_t, (g_w_out, g_w_ffn_in, g_w_ffn_out) = _wgrad_in(dz, dkv, h1, tk, ride=_x_pair_exchange(fulls))
    dw_in_t = [g.reshape(N_CHIP, zw // N_CHIP, d) for g in dw_in_t]

    (grad_x, st_in), (from_sib, from_far) = _in_proj_bwd(dz, dkv, w_in_tf, xs, dx1, g_mix, sc1, tm,
                                                         ride=_x_reduce([dw_in_t[0]], [dw_in_t[1]]))
    g_w_in_half = _sum_terms(pos, dw_in_t[0], from_sib, from_far, "sum_terms_w_in")

    dmod = jnp.concatenate([st_in[0:1], st_in[1:2], st_ffn[3:4], st_ffn[0:1], st_ffn[1:2], st_loss[0:1]], axis=1)
    db_in = jnp.concatenate([db_z[0:1, :d], db_kv[0:1], db_z[0:1, d + kvw2:]], axis=1)
    seg = [dmod, st_in[2:3], db_in, dsk[0:1], dcw[0:3].reshape(1, 3 * d), st_ffn[2:3], st_loss[1:2],
           st_loss[2:3, :LANES]]
    sizes = [s.shape[1] for s in seg]
    pack2 = _to_lanes(jnp.concatenate(seg, axis=1))
    packs, g_w_in_t = _tail_exchange(pack2, g_w_in_half)
    tot = _pack_sum(packs).reshape(-1)
    offs = [sum(sizes[:k]) for k in range(len(sizes))]
    gb_ada, gg_mix, gb_in, gsinks, gcw, gg_ffn, gg_final, loss_v = [tot[o:o + s] for o, s in zip(offs, sizes)]
    loss = loss_v[0]
    gsinks = gsinks[:sinks.shape[1]]
    gcw_sh = lax.dynamic_slice(gcw.reshape(3, d), (0, j_me * cw_sh), (3, cw_sh))

    dmod_all = packs[:, :n_mod * d // LANES, :].reshape(N_DEV, n_mod * d)
    g_w_ada = _ada_wgrad(c_all, lax.dynamic_slice(dmod_all, (0, j_me * mod_sh), (N_DEV, mod_sh)))

    out_g, out_d, out_m, out_v = {}, {}, {}, {}
    big = {"w_ada": (w_ada[0], g_w_ada, m_w_ada[0], v_w_ada[0]),
           "w_out": (w_out[0], g_w_out, m_w_out[0], v_w_out[0]),
           "w_ffn_in": (w_ffn_in[0], g_w_ffn_in, m_w_ffn_in[0], v_w_ffn_in[0]),
           "w_ffn_out": (w_ffn_out[0], g_w_ffn_out, m_w_ffn_out[0], v_w_ffn_out[0])}
    for nm, (w, g, m, v) in big.items():
        out_g[nm], out_d[nm], out_m[nm], out_v[nm] = [o[None] for o in _adamw(w, g, m, v, "adamw_" + nm)]
    out_g["w_in"], out_d["w_in"], out_m["w_in"], out_v["w_in"] = [
        o.T[None] for o in _adamw(w_in_t, g_w_in_t, m_w_in_t, v_w_in_t, "adamw_w_in")]
    small = {"b_ada": (b_ada, gb_ada, m_b_ada, v_b_ada), "g_mix": (g_mix, gg_mix, m_g_mix, v_g_mix),
             "b_in": (b_in, gb_in, m_b_in, v_b_in), "sinks": (sinks, gsinks, m_sinks, v_sinks),
             "conv_w": (conv_w, gcw_sh, m_conv_w, v_conv_w), "g_ffn": (g_ffn, gg_ffn, m_g_ffn, v_g_ffn),
             "g_final": (g_final, gg_final, m_g_final, v_g_final)}
    def two_d(a):
        return a.reshape(-1, a.shape[-1])

    s_out = _adamw_small([tuple(two_d(a.reshape(w.shape)) for a in (w, g, m, v)) for w, g, m, v in small.values()])
    for (nm, (w, g, _, _)), res in zip(small.items(), s_out):
        out_g[nm] = g.reshape(w.shape)
        out_d[nm], out_m[nm], out_v[nm] = [o.reshape(w.shape) for o in res]

    order = ["w_ada", "b_ada", "g_mix", "w_in", "b_in", "sinks", "conv_w", "w_out", "g_ffn", "w_ffn_in", "w_ffn_out",
             "g_final"]
    return (loss, grad_x[None], *[out_g[k] for k in order], *[out_d[k] for k in order],
            *[out_m[k] for k in order], *[out_v[k] for k in order])
```

```python
import functools

import jax
import jax.numpy as jnp
from jax import lax
from jax.experimental import pallas as pl
from jax.experimental.pallas import tpu as pltpu

F32 = jnp.float32
BF16 = jnp.bfloat16
EPS = 1e-6
HEAD_DIM = 64
GROUP = 8
BLOCK = 128
LANES = 128
SUBLANES_BF16 = 16
N_DEV = 8
N_CHIP = 4
VMEM_LIMIT = 56 * 1024 * 1024
MESH = pl.DeviceIdType.MESH

ADAM_LR = 0.001
ADAM_B1 = 0.9
ADAM_B2 = 0.999
ADAM_EPS = 1e-08
ADAM_WD = 0.01
ADAM_STEP = 10

SDS = jax.ShapeDtypeStruct
ANY = pl.BlockSpec(memory_space=pl.ANY)
VMEM_SPEC = pl.BlockSpec(memory_space=pltpu.VMEM)
SMEM_SPEC = pl.BlockSpec(memory_space=pltpu.SMEM)


def _params(*sem):
    return pltpu.CompilerParams(dimension_semantics=sem, vmem_limit_bytes=VMEM_LIMIT)


def _vec(v, d):
    arr, k = v if isinstance(v, tuple) else (v, 0)
    return arr, pl.BlockSpec((1, d), lambda *_: (0, k))


def _mesh_pos():
    return lax.axis_index("x"), lax.axis_index("y"), lax.axis_index("c")


def _row_tile(rows, cols, itemsize=4, budget=1 << 20, mult=8):
    best = None
    for t in range(mult, rows + 1, mult):
        if rows % t == 0 and t * cols * itemsize <= budget:
            best = t
    if best is None:
        best = rows
    return best


def _gather_all(v_ref, out_ref, send_sems, recv_sems, local_sem):
    x, y, c = _mesh_pos()
    me = 4 * x + 2 * y + c
    mine = pltpu.make_async_copy(v_ref, out_ref.at[me], local_sem)
    mine.start()
    peers = []
    for k in range(1, N_DEV):
        px = 1 - x if k & 4 else x
        py = 1 - y if k & 2 else y
        pc = 1 - c if k & 1 else c
        peers.append((px, py, pc))

    def copy(k, block):
        return pltpu.make_async_remote_copy(
            src_ref=v_ref, dst_ref=out_ref.at[block], send_sem=send_sems.at[k], recv_sem=recv_sems.at[k],
            device_id=peers[k], device_id_type=MESH)

    sends = [copy(k, me) for k in range(N_DEV - 1)]
    for cp in sends:
        cp.start()
    for k, (px, py, pc) in enumerate(peers):
        copy(k, 4 * px + 2 * py + pc).wait_recv()
    for cp in sends:
        cp.wait_send()
    mine.wait()


def _small_sems():
    return [pltpu.SemaphoreType.DMA((N_DEV - 1,)), pltpu.SemaphoreType.DMA((N_DEV - 1,)), pltpu.SemaphoreType.DMA]


def _tail_exchange(pack, full):
    def body(pack_ref, full_unused, packs_ref, full_ref, s1, r1, l1, send_sem, recv_sem):
        del full_unused
        x, y, c = _mesh_pos()
        half = full_ref.shape[0] // 2
        rows = pl.ds(pl.multiple_of(c * half, 8), half)
        swap = pltpu.make_async_remote_copy(
            src_ref=full_ref.at[rows], dst_ref=full_ref.at[rows], send_sem=send_sem, recv_sem=recv_sem,
            device_id=(x, y, 1 - c), device_id_type=MESH)
        swap.start()
        _gather_all(pack_ref, packs_ref, s1, r1, l1)
        swap.wait()

    return pl.pallas_call(
        body, name="tail_exchange", out_shape=[SDS((N_DEV,) + pack.shape, pack.dtype), SDS(full.shape, full.dtype)],
        in_specs=[VMEM_SPEC, ANY], out_specs=[VMEM_SPEC, ANY], input_output_aliases={1: 1},
        scratch_shapes=_small_sems() + [pltpu.SemaphoreType.DMA, pltpu.SemaphoreType.DMA])(pack, full)


def _other_chips(x, y):
    return [(1 - x, y), (x, 1 - y), (1 - x, 1 - y)]


def _startup(pack, w_ada_sh, b_ada_sh, w_buf, later):
    d, n = w_ada_sh.shape
    kc = d // LANES
    n_l = len(later)
    chunk_rows = [_row_tile(a.shape[0], a.shape[1], budget=3 << 19, mult=SUBLANES_BF16) for a in later]

    def body(*refs):
        pack_ref, wa_hbm, ba_ref, w_in_unused = refs[:4]
        later_src = refs[4:4 + n_l]
        packs_ref, mine_ref, w_ref = refs[4 + n_l:7 + n_l]
        later_dst = refs[7 + n_l:7 + 2 * n_l]
        wa_scr, mod_scr, mod_ref = refs[7 + 2 * n_l:10 + 2 * n_l]
        f32_bufs = refs[10 + 2 * n_l:10 + 3 * n_l]
        bf16_bufs = refs[10 + 3 * n_l:10 + 4 * n_l]
        (s1, r1, l1, s2, r2, l2, send_sems, recv_sems, fsend_sems, frecv_sems, relay_send, relay_recv, wa_sem,
         cast_sems) = refs[10 + 4 * n_l:]
        del w_in_unused
        x, y, c = _mesh_pos()
        j_me = 2 * x + y
        chips = _other_chips(x, y)
        half = w_ref.shape[1] // 2

        def rows_of(which):
            return pl.ds(pl.multiple_of(which * half, SUBLANES_BF16), half)

        def copy(p, block, rows, over_ici):
            sems = (send_sems, recv_sems) if over_ici else (fsend_sems, frecv_sems)
            return pltpu.make_async_remote_copy(
                src_ref=w_ref.at[block, rows], dst_ref=w_ref.at[block, rows], send_sem=sems[0].at[p],
                recv_sem=sems[1].at[p], device_id=(*chips[p], c) if over_ici else (x, y, 1 - c), device_id_type=MESH)

        def block_of(p):
            return 2 * chips[p][0] + chips[p][1]

        def relay(q, block):
            rows = pl.ds(pl.multiple_of(c * half + q * (half // 2), SUBLANES_BF16), half // 2)
            return pltpu.make_async_remote_copy(
                src_ref=w_ref.at[block, rows], dst_ref=w_ref.at[block, rows], send_sem=relay_send.at[q],
                recv_sem=relay_recv.at[q], device_id=(*chips[1 - q], c), device_id_type=MESH)

        load_wa = pltpu.make_async_copy(wa_hbm, wa_scr, wa_sem)
        load_wa.start()
        _gather_all(pack_ref, packs_ref, s1, r1, l1)
        sends = [copy(p, j_me, rows_of(c), True) for p in range(2)]
        for cp in sends:
            cp.start()
        load_wa.wait()
        acc = jnp.zeros((N_DEV, n), F32)
        for k in range(kc):
            ck = packs_ref[:, k, :]
            sk = (ck * jax.nn.sigmoid(ck)).astype(BF16)
            acc = acc + jnp.dot(sk, wa_scr[k * LANES:(k + 1) * LANES, :].astype(BF16), preferred_element_type=F32)
        mod_scr[...] = acc + ba_ref[...]
        _gather_all(mod_scr, mod_ref, s2, r2, l2)
        for j in range(N_CHIP):
            mine_ref[:, j * n:(j + 1) * n] = mod_ref[2 * j, pl.ds(4 * x + 2 * y + c, 1), :]
        passed = []
        for q in range(2):
            copy(q, block_of(q), rows_of(c), True).wait_recv()
            for cp in (relay(q, block_of(q)), copy(q, block_of(q), rows_of(c), False)):
                cp.start()
                passed.append(cp)
        for src, dst, fbuf, bbuf in zip(later_src, later_dst, f32_bufs, bf16_bufs):
            cr = fbuf.shape[0]
            for k in range(src.shape[0] // cr):
                rows = pl.ds(k * cr, cr)
                cin = pltpu.make_async_copy(src.at[rows], fbuf, cast_sems.at[0])
                cin.start()
                cin.wait()
                bbuf[...] = fbuf[...].astype(BF16)
                cout = pltpu.make_async_copy(bbuf, dst.at[j_me, rows], cast_sems.at[1])
                cout.start()
                cout.wait()
        for q in range(2):
            relay(q, block_of(2)).wait_recv()
        fw = copy(2, block_of(2), rows_of(c), False)
        fw.start()
        for p in range(3):
            copy(p, block_of(p), rows_of(1 - c), False).wait_recv()
        for cp in sends + passed + [fw]:
            cp.wait_send()

    res = pl.pallas_call(
        body, name="startup",
        out_shape=[SDS((N_DEV,) + pack.shape, F32), SDS((1, N_CHIP * n), F32), SDS(w_buf.shape, w_buf.dtype)]
        + [SDS((N_CHIP,) + a.shape, BF16) for a in later],
        in_specs=[VMEM_SPEC, ANY, VMEM_SPEC, ANY] + [ANY] * n_l, out_specs=[VMEM_SPEC, VMEM_SPEC, ANY] + [ANY] * n_l,
        input_output_aliases={3: 2},
        scratch_shapes=[pltpu.VMEM((d, n), F32), pltpu.VMEM((N_DEV, n), F32), pltpu.VMEM((N_DEV, N_DEV, n), F32)]
        + [pltpu.VMEM((cr, a.shape[1]), F32) for cr, a in zip(chunk_rows, later)]
        + [pltpu.VMEM((cr, a.shape[1]), BF16) for cr, a in zip(chunk_rows, later)]
        + _small_sems() + _small_sems()
        + [pltpu.SemaphoreType.DMA((3,))] * 4 + [pltpu.SemaphoreType.DMA((2,))] * 2 + [pltpu.SemaphoreType.DMA]
        + [pltpu.SemaphoreType.DMA((2,))],
        compiler_params=pltpu.CompilerParams(vmem_limit_bytes=VMEM_LIMIT),
    )(pack, w_ada_sh, b_ada_sh, w_buf, *later)
    return res[0], res[1], res[2], list(res[3:])


class _Exchange:
    def __init__(self, operands, out_shape, in_place, n_sems, copies):
        self.operands, self.out_shape, self.in_place, self.n_sems, self.copies = (
            list(operands), list(out_shape), in_place, n_sems, copies)

    def sems(self):
        return [pltpu.SemaphoreType.DMA((self.n_sems,)), pltpu.SemaphoreType.DMA((self.n_sems,))]


def _x_gather_ici(bufs):
    def copies(ins, outs, send_sems, recv_sems):
        x, y, c = _mesh_pos()
        chips = _other_chips(x, y)
        out = []
        for w in range(len(outs)):
            half = outs[w].shape[1] // 2
            rows = pl.ds(pl.multiple_of(c * half, SUBLANES_BF16), half)
            for p in range(3):
                out.append(pltpu.make_async_remote_copy(
                    src_ref=outs[w].at[2 * x + y, rows], dst_ref=outs[w].at[2 * x + y, rows],
                    send_sem=send_sems.at[w * 3 + p], recv_sem=recv_sems.at[w * 3 + p],
                    device_id=(*chips[p], c), device_id_type=MESH))
        return out

    return _Exchange(bufs, [SDS(b.shape, b.dtype) for b in bufs], True, 3 * len(bufs), copies)


def _x_gather_d2d(bufs):
    def copies(ins, outs, send_sems, recv_sems):
        x, y, c = _mesh_pos()
        chips = _other_chips(x, y)
        out = []
        for w in range(len(outs)):
            half = outs[w].shape[1] // 2
            rows = pl.ds(pl.multiple_of(c * half, SUBLANES_BF16), half)
            for p in range(3):
                block = 2 * chips[p][0] + chips[p][1]
                out.append(pltpu.make_async_remote_copy(
                    src_ref=outs[w].at[block, rows], dst_ref=outs[w].at[block, rows],
                    send_sem=send_sems.at[w * 3 + p], recv_sem=recv_sems.at[w * 3 + p],
                    device_id=(x, y, 1 - c), device_id_type=MESH))
        return out

    return _Exchange(bufs, [SDS(b.shape, b.dtype) for b in bufs], True, 3 * len(bufs), copies)


N_REMOTE = 6


def _x_reduce(grads32, grads16):
    n_w = len(grads32)

    def copies(ins, outs, send_sems, recv_sems):
        g32, g16 = ins[:n_w], ins[n_w:]
        from_sib, from_far = outs[:n_w], outs[n_w:]
        x, y, c = _mesh_pos()
        chips = _other_chips(x, y)
        out = []
        for w in range(n_w):
            half = g32[w].shape[1] // 2
            k0 = w * (N_REMOTE + 1)
            out.append(pltpu.make_async_remote_copy(
                src_ref=g32[w].at[2 * x + y, pl.ds(pl.multiple_of((1 - c) * half, SUBLANES_BF16), half), :],
                dst_ref=from_sib[w], send_sem=send_sems.at[k0], recv_sem=recv_sems.at[k0],
                device_id=(x, y, 1 - c), device_id_type=MESH))
            for p in range(3):
                for f in range(2):
                    tc = c if f == 0 else 1 - c
                    k = 2 * p + f
                    out.append(pltpu.make_async_remote_copy(
                        src_ref=g16[w].at[2 * chips[p][0] + chips[p][1],
                                          pl.ds(pl.multiple_of(tc * half, SUBLANES_BF16), half), :],
                        dst_ref=from_far[w].at[k], send_sem=send_sems.at[k0 + 1 + k], recv_sem=recv_sems.at[k0 + 1 + k],
                        device_id=(*chips[p], tc), device_id_type=MESH))
        return out

    shapes = ([SDS((g.shape[1] // 2, g.shape[2]), g.dtype) for g in grads32]
              + [SDS((N_REMOTE, g.shape[1] // 2, g.shape[2]), g.dtype) for g in grads16])
    return _Exchange(list(grads32) + list(grads16), shapes, False, (N_REMOTE + 1) * n_w, copies)


def _x_pair_exchange(fulls):
    def copies(ins, outs, send_sems, recv_sems):
        x, y, c = _mesh_pos()
        out = []
        for w in range(len(outs)):
            half = outs[w].shape[0] // 2
            rows = pl.ds(pl.multiple_of(c * half, 8), half)
            out.append(pltpu.make_async_remote_copy(
                src_ref=outs[w].at[rows], dst_ref=outs[w].at[rows], send_sem=send_sems.at[w],
                recv_sem=recv_sems.at[w], device_id=(x, y, 1 - c), device_id_type=MESH))
        return out

    return _Exchange(fulls, [SDS(f.shape, f.dtype) for f in fulls], True, len(fulls), copies)


def _pallas(body, *, name, grid, in_specs, out_specs, out_shape, args, scratch=(), sem=None, ride=None):
    single = not isinstance(out_specs, (list, tuple))
    out_specs_l = [out_specs] if single else list(out_specs)
    out_shape_l = [out_shape] if single else list(out_shape)
    n_in, n_out, n_scr = len(in_specs), len(out_specs_l), len(scratch)
    if ride is None:
        res = pl.pallas_call(body, name=name, grid=grid, in_specs=list(in_specs), out_specs=out_specs,
                             out_shape=out_shape, scratch_shapes=list(scratch), compiler_params=_params(*sem))(*args)
        return res, None
    n_x, n_xo = len(ride.operands), len(ride.out_shape)

    def full_body(*refs):
        ins, x_ins = refs[:n_in], refs[n_in:n_in + n_x]
        outs = refs[n_in + n_x:n_in + n_x + n_out]
        x_outs = refs[n_in + n_x + n_out:n_in + n_x + n_out + n_xo]
        rest = refs[n_in + n_x + n_out + n_xo:]
        scr, (send_sems, recv_sems) = rest[:n_scr], rest[n_scr:]
        first = functools.reduce(jnp.logical_and, [pl.program_id(a) == 0 for a in range(len(grid))])
        last = functools.reduce(jnp.logical_and, [pl.program_id(a) == grid[a] - 1 for a in range(len(grid))])

        @pl.when(first)
        def _():
            for cp in ride.copies(x_ins, x_outs, send_sems, recv_sems):
                cp.start()

        body(*ins, *outs, *scr)

        @pl.when(last)
        def _():
            for cp in ride.copies(x_ins, x_outs, send_sems, recv_sems):
                cp.wait()

    res = pl.pallas_call(
        full_body, name=name, grid=grid, in_specs=list(in_specs) + [ANY] * n_x,
        out_specs=out_specs_l + [ANY] * n_xo, out_shape=out_shape_l + ride.out_shape,
        input_output_aliases={n_in + k: n_out + k for k in range(n_x)} if ride.in_place else {},
        scratch_shapes=list(scratch) + ride.sems(),
        compiler_params=_params(*(["arbitrary"] * len(grid))))(*args, *ride.operands)
    own = res[0] if single else list(res[:n_out])
    return own, list(res[n_out:])


def _cast_into_block(pos, w, name):
    rows, cols = w.shape
    tr = _row_tile(rows, cols, mult=SUBLANES_BF16)

    def body(pos_ref, w_ref, o_ref):
        del pos_ref
        o_ref[...] = w_ref[...].astype(BF16)

    return pl.pallas_call(
        body, name=name,
        grid_spec=pltpu.PrefetchScalarGridSpec(
            num_scalar_prefetch=1, grid=(rows // tr,),
            in_specs=[pl.BlockSpec((tr, cols), lambda i, pos_ref: (i, 0))],
            out_specs=pl.BlockSpec((None, tr, cols), lambda i, pos_ref: (pos_ref[1], i, 0))),
        out_shape=SDS((N_CHIP, rows, cols), BF16), compiler_params=_params("parallel"))(pos, w)


def _sum_terms(pos, grad, from_sib, from_far, name):
    _, rows, cols = grad.shape
    half = rows // 2
    tr = _row_tile(half, cols, mult=SUBLANES_BF16)
    nblk = half // tr

    def body(pos_ref, g_ref, s_ref, r_ref, o_ref):
        del pos_ref
        acc = g_ref[...] + s_ref[...]
        for k in range(N_REMOTE):
            acc = acc + r_ref[k].astype(F32)
        o_ref[...] = acc

    return pl.pallas_call(
        body, name=name,
        grid_spec=pltpu.PrefetchScalarGridSpec(
            num_scalar_prefetch=1, grid=(nblk,),
            in_specs=[pl.BlockSpec((None, tr, cols), lambda i, pos_ref: (pos_ref[1], pos_ref[0] * nblk + i, 0)),
                      pl.BlockSpec((tr, cols), lambda i, pos_ref: (i, 0)),
                      pl.BlockSpec((N_REMOTE, tr, cols), lambda i, pos_ref: (0, i, 0))],
            out_specs=pl.BlockSpec((tr, cols), lambda i, pos_ref: (pos_ref[0] * nblk + i, 0))),
        out_shape=SDS((rows, cols), F32),
        compiler_params=_params("parallel"),
    )(pos, grad, from_sib, from_far)


def _adamw(w, g, m, v, name):
    rows, cols = w.shape
    tr = _row_tile(rows, cols)

    def body(w_ref, g_ref, m_ref, v_ref, go_ref, d_ref, nm_ref, nv_ref):
        go_ref[...] = g_ref[...]
        _adamw_update(w_ref, g_ref, m_ref, v_ref, d_ref, nm_ref, nv_ref)

    spec = pl.BlockSpec((tr, cols), lambda i: (i, 0))
    return pl.pallas_call(body, name=name, grid=(rows // tr,), in_specs=[spec] * 4, out_specs=[spec] * 4,
                          out_shape=[SDS((rows, cols), F32)] * 4, compiler_params=_params("parallel"))(w, g, m, v)


def _adamw_update(w_ref, g_ref, m_ref, v_ref, d_ref, nm_ref, nv_ref):
    gg = g_ref[...]
    nm = ADAM_B1 * m_ref[...] + (1.0 - ADAM_B1) * gg
    nv = ADAM_B2 * v_ref[...] + (1.0 - ADAM_B2) * (gg * gg)
    m_hat = nm / (1.0 - ADAM_B1 ** ADAM_STEP)
    v_hat = nv / (1.0 - ADAM_B2 ** ADAM_STEP)
    d_ref[...] = -ADAM_LR * (m_hat / (jnp.sqrt(v_hat) + ADAM_EPS) + ADAM_WD * w_ref[...])
    nm_ref[...] = nm
    nv_ref[...] = nv


def _adamw_small(params):
    n_p = len(params)

    def body(*refs):
        ins, outs = refs[:4 * n_p], refs[4 * n_p:]
        for k in range(n_p):
            _adamw_update(*ins[4 * k:4 * k + 4], *outs[3 * k:3 * k + 3])

    flat = [a for tup in params for a in tup]
    res = pl.pallas_call(
        body, name="adamw_small", in_specs=[VMEM_SPEC] * (4 * n_p), out_specs=[VMEM_SPEC] * (3 * n_p),
        out_shape=[SDS(tup[0].shape, F32) for tup in params for _ in range(3)])(*flat)
    return [res[3 * k:3 * k + 3] for k in range(n_p)]


def _pack_sum(gathered):
    _, rows, cols = gathered.shape

    def body(g_ref, o_ref):
        acc = g_ref[0]
        for d in range(1, N_DEV):
            acc = acc + g_ref[d]
        o_ref[...] = acc

    return pl.pallas_call(body, name="pack_sum", in_specs=[VMEM_SPEC], out_specs=VMEM_SPEC,
                          out_shape=SDS((rows, cols), F32))(gathered)


def _ada_wgrad(c_all, dmod_sh):
    d = c_all.shape[1]
    n = dmod_sh.shape[1]
    tn = 512

    def body(c_ref, g_ref, o_ref):
        cc = c_ref[...]
        s = cc * jax.nn.sigmoid(cc)
        o_ref[...] = lax.dot_general(s, g_ref[...], (((0,), (0,)), ((), ())), preferred_element_type=F32,
                                     precision=lax.Precision.HIGHEST)

    return pl.pallas_call(
        body, name="ada_wgrad", grid=(n // tn,),
        in_specs=[pl.BlockSpec((N_DEV, d), lambda j: (0, 0)), pl.BlockSpec((N_DEV, tn), lambda j: (0, j))],
        out_specs=pl.BlockSpec((d, tn), lambda j: (0, j)),
        out_shape=SDS((d, n), F32), compiler_params=_params("parallel"))(c_all, dmod_sh)


def _rms(xf):
    return lax.rsqrt(jnp.mean(xf * xf, axis=-1, keepdims=True) + EPS)


def _in_proj(x, g, sc, sh, wt, b, tm, tn, ride=None):
    t, d = x.shape
    n = wt.shape[0]

    def body(x_ref, g_ref, sc_ref, sh_ref, w_ref, b_ref, z_ref, h_ref):
        @pl.when(pl.program_id(1) == 0)
        def _():
            xf = x_ref[...]
            h_ref[...] = ((xf * _rms(xf) * g_ref[...]) * (1.0 + sc_ref[...]) + sh_ref[...]).astype(BF16)

        acc = lax.dot_general(h_ref[...], w_ref[...], (((1,), (1,)), ((), ())), preferred_element_type=F32)
        z_ref[...] = (acc + b_ref[...]).astype(BF16)

    row = pl.BlockSpec((tm, d), lambda i, j: (i, 0))
    vecs, vec_specs = zip(*[_vec(v, d) for v in (g, sc, sh)])
    return _pallas(
        body, name="in_proj", grid=(t // tm, n // tn),
        in_specs=[row, *vec_specs, pl.BlockSpec((tn, d), lambda i, j: (j, 0)),
                  pl.BlockSpec((1, tn), lambda i, j: (0, j))],
        out_specs=[pl.BlockSpec((tm, tn), lambda i, j: (i, j)), row],
        out_shape=[SDS((t, n), BF16), SDS((t, d), BF16)], args=(x, *vecs, wt, b),
        sem=("parallel", "arbitrary"), ride=ride)


def _segments(d, kvw2):
    o = d + kvw2
    names = ("cb", "cc", "cx", "ga", "gc")
    seg = {nm: slice(o + k * d, o + (k + 1) * d) for k, nm in enumerate(names)}
    seg["q"], seg["kv"] = slice(0, d), slice(d, o)
    return seg


def _attn_masks():
    rows = 4 * BLOCK
    r = lax.broadcasted_iota(jnp.int32, (rows, 2 * BLOCK), 0) & (BLOCK - 1)
    col = lax.broadcasted_iota(jnp.int32, (rows, 2 * BLOCK), 1)
    return (col > r) & (col <= r + BLOCK), col


def _kv_variants(kv, n_kv_w):
    assert n_kv_w == LANES
    kb, vb = kv[:, :LANES] * (HEAD_DIM ** -0.5), kv[:, LANES:]
    kr, vr = pltpu.roll(kb, HEAD_DIM, 1), pltpu.roll(vb, HEAD_DIM, 1)
    lane = lax.broadcasted_iota(jnp.int32, kb.shape, 1)
    lo = lane < HEAD_DIM
    zero = jnp.zeros_like(kb)
    k_eff = [[None, None], [None, None]]
    v_eff = [[None, None], [None, None]]
    for h in range(2):
        for e in range(2):
            ksrc, vsrc = (kb, vb) if e == h else (kr, vr)
            keep = lo if e == 0 else jnp.logical_not(lo)
            k_eff[h][e] = jnp.where(keep, ksrc, zero)
            v_eff[h][e] = jnp.where(keep, vsrc, zero)
    return k_eff, v_eff


def _sink_column(sinks_ref, h, e):
    rowblk = lax.broadcasted_iota(jnp.int32, (4 * BLOCK, 1), 0) // BLOCK
    col = jnp.zeros((4 * BLOCK, 1), F32)
    for j in range(4):
        col = jnp.where(rowblk == j, sinks_ref[0, GROUP * h + 2 * j + e], col)
    return col


def _softmax_sink(s, valid, sink):
    s = jnp.where(valid, s, -jnp.inf)
    m = jnp.maximum(jnp.max(s, axis=-1, keepdims=True), sink)
    p = jnp.exp(s - m)
    psink = jnp.exp(sink - m)
    den = jnp.sum(p, axis=-1, keepdims=True) + psink
    inv = 1.0 / den
    return p * inv, psink * inv


def _shift_down(a, s, prev):
    rows = a.shape[0]
    out = pltpu.roll(a, s, 0)
    row = lax.broadcasted_iota(jnp.int32, a.shape, 0)
    for t in range(s):
        out = jnp.where(row == t, prev[SUBLANES_BF16 - s + t:SUBLANES_BF16 - s + t + 1, :], out)
    del rows
    return out


def _shift_up(a, s, nxt):
    rows = a.shape[0]
    out = pltpu.roll(a, rows - s, 0)
    row = lax.broadcasted_iota(jnp.int32, a.shape, 0)
    for t in range(s):
        out = jnp.where(row == rows - s + t, nxt[t:t + 1, :], out)
    return out


def _stack_pairs(ref, h, rows=slice(None)):
    return jnp.concatenate([ref[rows, (4 * h + j) * LANES:(4 * h + j + 1) * LANES] for j in range(4)], axis=0)


FWD_BLOCKS = 4


def _mixer_fwd(z, sinks, conv_w, d, ride=None):
    t, zw = z.shape
    kvw2 = zw - 6 * d
    tq = FWD_BLOCKS * BLOCK
    halo = tq // SUBLANES_BF16
    seg = _segments(d, kvw2)

    def body(z_ref, kvp_ref, prev_ref, sinks_ref, cw_ref, attn_ref, merged_ref):
        n = pl.program_id(0)
        band, col = _attn_masks()
        for b in range(FWD_BLOCKS):
            rows = slice(b * BLOCK, (b + 1) * BLOCK)
            before = slice((b - 1) * BLOCK, b * BLOCK)
            kv_prev = kvp_ref[...] if b == 0 else z_ref[before, seg["kv"]]
            kv = jnp.concatenate([kv_prev, z_ref[rows, seg["kv"]]], axis=0)
            k_eff, v_eff = _kv_variants(kv, kvw2 // 2)
            valid = band & ((n > 0) | (col >= BLOCK)) if b == 0 else band
            for h in range(2):
                q4 = _stack_pairs(z_ref, h, rows)
                o4 = jnp.zeros((4 * BLOCK, LANES), F32)
                for e in range(2):
                    s = lax.dot_general(q4, k_eff[h][e], (((1,), (1,)), ((), ())), preferred_element_type=F32)
                    p, _ = _softmax_sink(s, valid, _sink_column(sinks_ref, h, e))
                    o4 = o4 + jnp.dot(p.astype(BF16), v_eff[h][e], preferred_element_type=F32)
                for j in range(4):
                    attn_ref[rows, (4 * h + j) * LANES:(4 * h + j + 1) * LANES] = (
                        o4[j * BLOCK:(j + 1) * BLOCK].astype(BF16))
            cb = z_ref[rows, seg["cb"]].astype(F32)
            p_in = z_ref[rows, seg["cc"]].astype(F32) * z_ref[rows, seg["cx"]].astype(F32)
            if b == 0:
                prev = jnp.where(n > 0, prev_ref[:, seg["cc"]].astype(F32) * prev_ref[:, seg["cx"]].astype(F32), 0.0)
            else:
                tail = slice(b * BLOCK - SUBLANES_BF16, b * BLOCK)
                prev = z_ref[tail, seg["cc"]].astype(F32) * z_ref[tail, seg["cx"]].astype(F32)
            cconv = (cw_ref[0:1, :] * _shift_down(p_in, 2, prev) + cw_ref[1:2, :] * _shift_down(p_in, 1, prev)
                     + cw_ref[2:3, :] * p_in)
            sa = jax.nn.sigmoid(z_ref[rows, seg["ga"]].astype(F32))
            sg = jax.nn.sigmoid(z_ref[rows, seg["gc"]].astype(F32))
            merged_ref[rows, :] = (sa * attn_ref[rows, :].astype(F32) + sg * (cb * cconv)).astype(BF16)

    blk = pl.BlockSpec((tq, d), lambda n: (n, 0))
    return _pallas(
        body, name="mixer_fwd", grid=(t // tq,),
        in_specs=[pl.BlockSpec((tq, zw), lambda n: (n, 0)),
                  pl.BlockSpec((BLOCK, kvw2), lambda n: (jnp.maximum(n * FWD_BLOCKS - 1, 0), d // kvw2)),
                  pl.BlockSpec((SUBLANES_BF16, zw), lambda n: (jnp.maximum(n * halo - 1, 0), 0)),
                  SMEM_SPEC, pl.BlockSpec((3, d), lambda n: (0, 0))],
        out_specs=[blk, blk],
        out_shape=[SDS((t, d), BF16), SDS((t, d), BF16)],
        args=(z, z, z, sinks, conv_w), sem=("parallel",), ride=ride)


def _out_proj_fwd(merged, w_out, x, ga1, g_ffn, sc2, sh2, tm):
    t, d = x.shape

    def body(m_ref, w_ref, x_ref, ga_ref, g_ref, sc_ref, sh_ref, y_ref, x1_ref, h_ref):
        y = jnp.dot(m_ref[...], w_ref[...], preferred_element_type=F32)
        x1 = x_ref[...] + ga_ref[...] * y
        y_ref[...] = y.astype(BF16)
        x1_ref[...] = x1
        h_ref[...] = ((x1 * _rms(x1) * g_ref[...]) * (1.0 + sc_ref[...]) + sh_ref[...]).astype(BF16)

    row = pl.BlockSpec((tm, d), lambda i: (i, 0))
    vecs, vec_specs = zip(*[_vec(v, d) for v in (ga1, g_ffn, sc2, sh2)])
    return pl.pallas_call(
        body, name="out_proj_fwd", grid=(t // tm,),
        in_specs=[row, pl.BlockSpec((d, d), lambda i: (0, 0)), row, *vec_specs],
        out_specs=[row, row, row],
        out_shape=[SDS((t, d), BF16), SDS((t, d), F32), SDS((t, d), BF16)],
        compiler_params=_params("parallel"))(merged, w_out, x, *vecs)


def _ffn_in_fwd(h2, w, ff, tm, tn):
    t, d = h2.shape
    nj = ff // tn
    assert w.shape == (2 * nj, d, tn)

    def body(h_ref, wg_ref, wu_ref, gu_ref, act_ref):
        hh = h_ref[...]
        g = jnp.dot(hh, wg_ref[...], preferred_element_type=F32)
        u = jnp.dot(hh, wu_ref[...], preferred_element_type=F32)
        sg = jax.nn.sigmoid(g)
        silu = g * sg
        gu_ref[0] = (u * (sg + silu * (1.0 - sg))).astype(BF16)
        gu_ref[1] = silu.astype(BF16)
        act_ref[...] = (silu * u).astype(BF16)

    return pl.pallas_call(
        body, name="ffn_in_fwd", grid=(nj, t // tm),
        in_specs=[pl.BlockSpec((tm, d), lambda j, i: (i, 0)), pl.BlockSpec((None, d, tn), lambda j, i: (j, 0, 0)),
                  pl.BlockSpec((None, d, tn), lambda j, i: (j + nj, 0, 0))],
        out_specs=[pl.BlockSpec((2, tm, tn), lambda j, i: (0, i, j)), pl.BlockSpec((tm, tn), lambda j, i: (i, j))],
        out_shape=[SDS((2, t, ff), BF16), SDS((t, ff), BF16)],
        compiler_params=_params("parallel", "parallel"))(h2, w, w)


def _ffn_out_loss(act, w, x1, target, ga2, g_final, tm):
    t, d = x1.shape
    ff = act.shape[1]

    def body(a_ref, w_ref, x1_ref, tg_ref, ga_ref, gf_ref, dx2_ref, dy2_ref, st_ref):
        @pl.when(pl.program_id(0) == 0)
        def _():
            st_ref[...] = jnp.zeros_like(st_ref)

        halves = [slice(k * (tm // 2), (k + 1) * (tm // 2)) for k in range(2)]
        y2s = [jnp.dot(a_ref[rows, :], w_ref[...], preferred_element_type=F32) for rows in halves]
        for rows, y2 in zip(halves, y2s):
            x2 = x1_ref[rows, :] + ga_ref[...] * y2
            r = _rms(x2)
            yn = x2 * r
            err = yn * gf_ref[...] - tg_ref[rows, :]
            loss = 0.5 * jnp.sum(jnp.mean(err * err, axis=-1, keepdims=True), axis=0, keepdims=True)
            dy = err * (1.0 / d)
            u = dy * gf_ref[...]
            dx2 = r * (u - yn * jnp.mean(u * yn, axis=-1, keepdims=True))
            dx2_ref[rows, :] = dx2
            dy2_ref[rows, :] = (ga_ref[...] * dx2).astype(BF16)
            st_ref[0:1, :] += jnp.sum(dx2 * y2, axis=0, keepdims=True)
            st_ref[1:2, :] += jnp.sum(dy * yn, axis=0, keepdims=True)
            st_ref[2:3, :] += jnp.broadcast_to(loss, (1, d))

    row = pl.BlockSpec((tm, d), lambda i: (i, 0))
    vecs, vec_specs = zip(*[_vec(v, d) for v in (ga2, g_final)])
    return pl.pallas_call(
        body, name="ffn_out_loss", grid=(t // tm,),
        in_specs=[pl.BlockSpec((tm, ff), lambda i: (i, 0)),
                  pl.BlockSpec((ff, d), lambda i: (0, 0), pipeline_mode=pl.Buffered(1)), row, row, *vec_specs],
        out_specs=[row, row, pl.BlockSpec((8, d), lambda i: (0, 0))],
        out_shape=[SDS((t, d), F32), SDS((t, d), BF16), SDS((8, d), F32)],
        compiler_params=_params("arbitrary"))(act, w, x1, target, *vecs)


def _ffn_out_bwd(dy2, w, gu, tm, tn):
    t, d = dy2.shape
    ff = w.shape[0]

    def body(dy_ref, w_ref, gu_ref, o_ref):
        dy = dy_ref[...]
        for lo in range(0, tn, 3 * LANES):
            cols = slice(lo, min(lo + 3 * LANES, tn))
            dact = lax.dot_general(dy, w_ref[cols, :], (((1,), (1,)), ((), ())), preferred_element_type=F32)
            o_ref[0, :, cols] = (dact * gu_ref[0, :, cols].astype(F32)).astype(BF16)
            o_ref[1, :, cols] = (dact * gu_ref[1, :, cols].astype(F32)).astype(BF16)

    gu_spec = pl.BlockSpec((2, tm, tn), lambda j, i: (0, i, j))
    return pl.pallas_call(
        body, name="ffn_out_bwd", grid=(ff // tn, t // tm),
        in_specs=[pl.BlockSpec((tm, d), lambda j, i: (i, 0)), pl.BlockSpec((tn, d), lambda j, i: (j, 0)), gu_spec],
        out_specs=gu_spec, out_shape=SDS((2, t, ff), BF16),
        compiler_params=_params("parallel", "parallel"))(dy2, w, gu)


def _wgrad(a, b, a_spec, b_spec, out_spec, out_shape, grid, name, ride=None):
    def body(a_ref, b_ref, o_ref, o16_ref):
        k = pl.program_id(len(grid) - 1)

        @pl.when(k == 0)
        def _():
            o_ref[...] = jnp.zeros_like(o_ref)

        o_ref[...] += lax.dot_general(a_ref[...], b_ref[...], (((0,), (0,)), ((), ())), preferred_element_type=F32)

        @pl.when(k == grid[-1] - 1)
        def _():
            o16_ref[...] = o_ref[...].astype(BF16)

    return _pallas(
        body, name=name, grid=grid, in_specs=[a_spec, b_spec], out_specs=[out_spec, out_spec],
        out_shape=[out_shape, SDS(out_shape.shape, BF16)], args=(a, b),
        sem=["parallel"] * (len(grid) - 1) + ["arbitrary"], ride=ride)


def _ffn_in_bwd(dgu, w, x1, dx2, y1, g_ffn, sc2, ga1, tm):
    t, d = x1.shape
    ff = dgu.shape[2]
    n_sh, _, sw = w.shape
    per = ff // sw
    nt = (((1,), (1,)), ((), ()))

    def body(a_ref, w_ref, x1_ref, dx2_ref, y1_ref, g_ref, sc_ref, ga_ref, dx1_ref, dy1_ref, st_ref):
        @pl.when(pl.program_id(0) == 0)
        def _():
            st_ref[...] = jnp.zeros_like(st_ref)

        dh = None
        for j in range(n_sh):
            part = lax.dot_general(a_ref[j // per, :, (j % per) * sw:(j % per + 1) * sw], w_ref[j], nt,
                                   preferred_element_type=F32)
            dh = part if dh is None else dh + part
        x1 = x1_ref[...]
        r = _rms(x1)
        xn = x1 * r
        g = g_ref[...]
        dn = dh * (1.0 + sc_ref[...])
        u = dn * g
        dx1 = dx2_ref[...] + r * (u - xn * jnp.mean(u * xn, axis=-1, keepdims=True))
        dx1_ref[...] = dx1
        dy1_ref[...] = (ga_ref[...] * dx1).astype(BF16)
        st_ref[0:1, :] += jnp.sum(dh, axis=0, keepdims=True)
        st_ref[1:2, :] += jnp.sum(dh * (xn * g), axis=0, keepdims=True)
        st_ref[2:3, :] += jnp.sum(dn * xn, axis=0, keepdims=True)
        st_ref[3:4, :] += jnp.sum(dx1 * y1_ref[...].astype(F32), axis=0, keepdims=True)

    row = pl.BlockSpec((tm, d), lambda i: (i, 0))
    vecs, vec_specs = zip(*[_vec(v, d) for v in (g_ffn, sc2, ga1)])
    return pl.pallas_call(
        body, name="ffn_in_bwd", grid=(t // tm,),
        in_specs=[pl.BlockSpec((2, tm, ff), lambda i: (0, i, 0)),
                  pl.BlockSpec((n_sh, d, sw), lambda i: (0, 0, 0), pipeline_mode=pl.Buffered(1)),
                  row, row, row, *vec_specs],
        out_specs=[row, row, pl.BlockSpec((8, d), lambda i: (0, 0))],
        out_shape=[SDS((t, d), F32), SDS((t, d), BF16), SDS((8, d), F32)],
        compiler_params=_params("arbitrary"))(dgu, w, x1, dx2, y1, *vecs)


def _out_proj_bwd(dy1, w_out, tm, ride=None):
    t, d = dy1.shape

    def body(dy_ref, w_ref, o_ref):
        o_ref[...] = lax.dot_general(dy_ref[...], w_ref[...], (((1,), (1,)), ((), ())),
                                     preferred_element_type=F32).astype(BF16)

    row = pl.BlockSpec((tm, d), lambda i: (i, 0))
    return _pallas(body, name="out_proj_bwd", grid=(t // tm,),
                   in_specs=[row, pl.BlockSpec((d, d), lambda i: (0, 0))], out_specs=row,
                   out_shape=SDS((t, d), BF16), args=(dy1, w_out), sem=("parallel",), ride=ride)


BWD_BLOCKS = 2


def _mixer_bwd(z, dmerged, attn, sinks, conv_w, d, ride=None):
    t, zw = z.shape
    kvw2 = zw - 6 * d
    tq = BWD_BLOCKS * BLOCK
    steps = t // tq
    halo = tq // SUBLANES_BF16
    last_halo = t // SUBLANES_BF16 - 1
    scale = HEAD_DIM ** -0.5
    seg = _segments(d, kvw2)

    def body(z_ref, kvp_ref, prev_ref, next_ref, dm_ref, dmn_ref, attn_ref, sinks_ref, cw_ref,
             dz_ref, dkv_ref, db_ref, dbkv_ref, dcw_ref, dsk_ref, carry_ref):
        n = pl.program_id(0)

        @pl.when(n == 0)
        def _():
            carry_ref[...] = jnp.zeros_like(carry_ref)
            db_ref[...] = jnp.zeros_like(db_ref)
            dbkv_ref[...] = jnp.zeros_like(dbkv_ref)
            dcw_ref[...] = jnp.zeros_like(dcw_ref)
            dsk_ref[...] = jnp.zeros_like(dsk_ref)

        def one_block(b, pending):
            rows = slice(b * BLOCK, (b + 1) * BLOCK)
            before = slice((b - 1) * BLOCK, b * BLOCK)
            dm = dm_ref[rows, :].astype(F32)
            sa = jax.nn.sigmoid(z_ref[rows, seg["ga"]].astype(F32))
            dga = dm * attn_ref[rows, :].astype(F32) * sa * (1.0 - sa)
            dz_ref[rows, seg["ga"]] = dga.astype(BF16)
            db_ref[0:1, seg["ga"]] += jnp.sum(dga, axis=0, keepdims=True)
            dattn = (dm * sa).astype(BF16)

            kv_prev = kvp_ref[...] if b == 0 else z_ref[before, seg["kv"]]
            kv = jnp.concatenate([kv_prev, z_ref[rows, seg["kv"]]], axis=0)
            k_eff, v_eff = _kv_variants(kv, kvw2 // 2)
            band, col = _attn_masks()
            valid = band & ((n > 0) | (col >= BLOCK)) if b == 0 else band
            lane_lo = lax.broadcasted_iota(jnp.int32, (2 * BLOCK, LANES), 1) < HEAD_DIM
            sink_lane = lax.broadcasted_iota(jnp.int32, (1, LANES), 1)
            rowblk = lax.broadcasted_iota(jnp.int32, (4 * BLOCK, 1), 0) // BLOCK
            dk_acc = [jnp.zeros((2 * BLOCK, LANES), F32), jnp.zeros((2 * BLOCK, LANES), F32)]
            dv_acc = [jnp.zeros((2 * BLOCK, LANES), F32), jnp.zeros((2 * BLOCK, LANES), F32)]
            dsink = jnp.zeros((1, LANES), F32)
            for h in range(2):
                q4 = _stack_pairs(z_ref, h, rows)
                do4 = jnp.concatenate([dattn[:, (4 * h + j) * LANES:(4 * h + j + 1) * LANES] for j in range(4)],
                                      axis=0)
                dq4 = jnp.zeros((4 * BLOCK, LANES), F32)
                for e in range(2):
                    s = lax.dot_general(q4, k_eff[h][e], (((1,), (1,)), ((), ())), preferred_element_type=F32)
                    p, psink = _softmax_sink(s, valid, _sink_column(sinks_ref, h, e))
                    dp = lax.dot_general(do4, v_eff[h][e], (((1,), (1,)), ((), ())), preferred_element_type=F32)
                    delta = jnp.sum(p * dp, axis=-1, keepdims=True)
                    ds = (p * (dp - delta)).astype(BF16)
                    dq4 = dq4 + jnp.dot(ds, k_eff[h][e], preferred_element_type=F32)
                    dk = lax.dot_general(q4, ds, (((0,), (0,)), ((), ())), preferred_element_type=F32).T
                    dv = lax.dot_general(do4, p.astype(BF16), (((0,), (0,)), ((), ())), preferred_element_type=F32).T
                    keep = lane_lo if e == 0 else jnp.logical_not(lane_lo)
                    slot = 0 if e == h else 1
                    dk_acc[slot] = dk_acc[slot] + jnp.where(keep, dk, 0.0)
                    dv_acc[slot] = dv_acc[slot] + jnp.where(keep, dv, 0.0)
                    dsk = -(psink * delta)
                    for j in range(4):
                        tot = jnp.sum(jnp.where(rowblk == j, dsk, 0.0), axis=0, keepdims=True)
                        dsink = dsink + jnp.where(sink_lane == GROUP * h + 2 * j + e, tot, 0.0)
                for j in range(4):
                    cols = slice((4 * h + j) * LANES, (4 * h + j + 1) * LANES)
                    dqj = dq4[j * BLOCK:(j + 1) * BLOCK]
                    dz_ref[rows, cols] = dqj.astype(BF16)
                    db_ref[0:1, cols] += jnp.sum(dqj, axis=0, keepdims=True)
            dsk_ref[0:1, :] += dsink
            dkv_new = jnp.concatenate([(dk_acc[0] + pltpu.roll(dk_acc[1], HEAD_DIM, 1)) * scale,
                                       dv_acc[0] + pltpu.roll(dv_acc[1], HEAD_DIM, 1)], axis=1)
            done = pending + dkv_new[:BLOCK]
            dkv_ref[rows, :] = done.astype(BF16)
            dbkv_ref[0:1, :] += jnp.sum(done, axis=0, keepdims=True)

            cb = z_ref[rows, seg["cb"]].astype(F32)
            cc = z_ref[rows, seg["cc"]].astype(F32)
            cx = z_ref[rows, seg["cx"]].astype(F32)
            sg = jax.nn.sigmoid(z_ref[rows, seg["gc"]].astype(F32))
            p_in = cc * cx
            if b == 0:
                prev = jnp.where(n > 0, prev_ref[:, seg["cc"]].astype(F32) * prev_ref[:, seg["cx"]].astype(F32), 0.0)
            else:
                tail = slice(b * BLOCK - SUBLANES_BF16, b * BLOCK)
                prev = z_ref[tail, seg["cc"]].astype(F32) * z_ref[tail, seg["cx"]].astype(F32)
            p_m1 = _shift_down(p_in, 1, prev)
            p_m2 = _shift_down(p_in, 2, prev)
            w0, w1, w2 = cw_ref[0:1, :], cw_ref[1:2, :], cw_ref[2:3, :]
            cconv = w0 * p_m2 + w1 * p_m1 + w2 * p_in
            dconv = dm * sg
            dgc = dm * (cb * cconv) * sg * (1.0 - sg)
            dcb = dconv * cconv
            dcc_t = dconv * cb
            if b == BWD_BLOCKS - 1:
                nxt = jnp.where(n < steps - 1,
                                dmn_ref[...].astype(F32) * jax.nn.sigmoid(next_ref[:, seg["gc"]].astype(F32))
                                * next_ref[:, seg["cb"]].astype(F32), 0.0)
            else:
                head = slice((b + 1) * BLOCK, (b + 1) * BLOCK + SUBLANES_BF16)
                nxt = (dm_ref[head, :].astype(F32) * jax.nn.sigmoid(z_ref[head, seg["gc"]].astype(F32))
                       * z_ref[head, seg["cb"]].astype(F32))
            dpin = w2 * dcc_t + w1 * _shift_up(dcc_t, 1, nxt) + w0 * _shift_up(dcc_t, 2, nxt)
            for nm, val in (("cb", dcb), ("cc", dpin * cx), ("cx", dpin * cc), ("gc", dgc)):
                dz_ref[rows, seg[nm]] = val.astype(BF16)
                db_ref[0:1, seg[nm]] += jnp.sum(val, axis=0, keepdims=True)
            dcw_ref[0:1, :] += jnp.sum(dcc_t * p_m2, axis=0, keepdims=True)
            dcw_ref[1:2, :] += jnp.sum(dcc_t * p_m1, axis=0, keepdims=True)
            dcw_ref[2:3, :] += jnp.sum(dcc_t * p_in, axis=0, keepdims=True)
            return dkv_new[BLOCK:]

        @pl.when(n < steps)
        def _():
            pending = carry_ref[...]
            for b in range(BWD_BLOCKS):
                pending = one_block(b, pending)
            carry_ref[...] = pending

        @pl.when(n == steps)
        def _():
            done = carry_ref[...]
            dkv_ref[:BLOCK, :] = done.astype(BF16)
            dkv_ref[BLOCK:, :] = jnp.zeros((tq - BLOCK, kvw2), BF16)
            dbkv_ref[0:1, :] += jnp.sum(done, axis=0, keepdims=True)

    def cur(n):
        return jnp.minimum(n, steps - 1)

    def after(n):
        return jnp.minimum((cur(n) + 1) * halo, last_halo)

    blk = pl.BlockSpec((tq, d), lambda n: (cur(n), 0))
    return _pallas(
        body, name="mixer_bwd", grid=(steps + 1,), ride=ride, sem=("arbitrary",),
        args=(z, z, z, z, dmerged, dmerged, attn, sinks, conv_w),
        in_specs=[pl.BlockSpec((tq, zw), lambda n: (cur(n), 0)),
                  pl.BlockSpec((BLOCK, kvw2), lambda n: (jnp.maximum(cur(n) * BWD_BLOCKS - 1, 0), d // kvw2)),
                  pl.BlockSpec((SUBLANES_BF16, zw), lambda n: (jnp.maximum(cur(n) * halo - 1, 0), 0)),
                  pl.BlockSpec((SUBLANES_BF16, zw), lambda n: (after(n), 0)),
                  blk,
                  pl.BlockSpec((SUBLANES_BF16, d), lambda n: (after(n), 0)),
                  blk, SMEM_SPEC, pl.BlockSpec((3, d), lambda n: (0, 0))],
        out_specs=[pl.BlockSpec((tq, zw), lambda n: (cur(n), 0)),
                   pl.BlockSpec((tq, kvw2), lambda n: (n, 0)),
                   pl.BlockSpec((8, zw), lambda n: (0, 0)), pl.BlockSpec((8, kvw2), lambda n: (0, 0)),
                   pl.BlockSpec((8, d), lambda n: (0, 0)), pl.BlockSpec((8, LANES), lambda n: (0, 0))],
        out_shape=[SDS((t, zw), BF16), SDS((t + tq, kvw2), BF16), SDS((8, zw), F32), SDS((8, kvw2), F32),
                   SDS((8, d), F32), SDS((8, LANES), F32)],
        scratch=[pltpu.VMEM((BLOCK, kvw2), F32)])


def _wgrad_in(dz, dkv, h1, tk, ride=None):
    t, zw = dz.shape
    d = h1.shape[1]
    kvw2 = dkv.shape[1]
    blk = d + kvw2
    assert zw % blk == 0
    tn = (((0,), (0,)), ((), ()))

    def body(a_ref, akv_ref, h_ref, o_ref, o16_ref):
        n, k = pl.program_id(0), pl.program_id(1)

        @pl.when(k == 0)
        def _():
            o_ref[...] = jnp.zeros_like(o_ref)

        @pl.when(n == 0)
        def _():
            o_ref[:d, :] += lax.dot_general(a_ref[:, :d], h_ref[...], tn, preferred_element_type=F32)
            o_ref[d:, :] += lax.dot_general(akv_ref[...], h_ref[...], tn, preferred_element_type=F32)

        @pl.when(n > 0)
        def _():
            o_ref[...] += lax.dot_general(a_ref[...], h_ref[...], tn, preferred_element_type=F32)

        @pl.when(k == t // tk - 1)
        def _():
            o16_ref[...] = o_ref[...].astype(BF16)

    out_spec = pl.BlockSpec((blk, d), lambda n, k: (n, 0))
    return _pallas(
        body, name="wgrad_in", grid=(zw // blk, t // tk),
        in_specs=[pl.BlockSpec((tk, blk), lambda n, k: (k, n)), pl.BlockSpec((tk, kvw2), lambda n, k: (k, 0)),
                  pl.BlockSpec((tk, d), lambda n, k: (k, 0))],
        out_specs=[out_spec, out_spec], out_shape=[SDS((zw, d), F32), SDS((zw, d), BF16)],
        args=(dz, dkv, h1), sem=("parallel", "arbitrary"), ride=ride)


def _in_proj_bwd(dz, dkv, wt, x, dx1, g_mix, sc1, tm, ride=None):
    t, d = x.shape
    zw = dz.shape[1]
    kvw2 = dkv.shape[1]
    rest = d + kvw2

    def body(a_ref, akv_ref, w_ref, x_ref, dx1_ref, g_ref, sc_ref, gx_ref, st_ref):
        @pl.when(pl.program_id(0) == 0)
        def _():
            st_ref[...] = jnp.zeros_like(st_ref)

        dh = (jnp.dot(a_ref[:, :d], w_ref[:d, :], preferred_element_type=F32)
              + jnp.dot(akv_ref[...], w_ref[d:rest, :], preferred_element_type=F32)
              + jnp.dot(a_ref[:, rest:], w_ref[rest:, :], preferred_element_type=F32))
        xx = x_ref[...]
        r = _rms(xx)
        xn = xx * r
        g = g_ref[...]
        dn = dh * (1.0 + sc_ref[...])
        u = dn * g
        gx_ref[...] = dx1_ref[...] + r * (u - xn * jnp.mean(u * xn, axis=-1, keepdims=True))
        st_ref[0:1, :] += jnp.sum(dh, axis=0, keepdims=True)
        st_ref[1:2, :] += jnp.sum(dh * (xn * g), axis=0, keepdims=True)
        st_ref[2:3, :] += jnp.sum(dn * xn, axis=0, keepdims=True)

    row = pl.BlockSpec((tm, d), lambda i: (i, 0))
    vecs, vec_specs = zip(*[_vec(v, d) for v in (g_mix, sc1)])
    return _pallas(
        body, name="in_proj_bwd", grid=(t // tm,),
        in_specs=[pl.BlockSpec((tm, zw), lambda i: (i, 0)), pl.BlockSpec((tm, kvw2), lambda i: (i, 0)),
                  pl.BlockSpec((zw, d), lambda i: (0, 0), pipeline_mode=pl.Buffered(1)),
                  row, row, *vec_specs],
        out_specs=[row, pl.BlockSpec((8, d), lambda i: (0, 0))],
        out_shape=[SDS((t, d), F32), SDS((8, d), F32)],
        args=(dz, dkv, wt, x, dx1, *vecs), sem=("arbitrary",), ride=ride)


def _to_lanes(v, rows=None):
    flat = v.reshape(-1)
    need = -(-flat.shape[0] // LANES)
    need = -(-need // 8) * 8 if rows is None else rows
    return jnp.pad(flat, (0, need * LANES - flat.shape[0])).reshape(need, LANES)


def kernel(x, c, w_ada, b_ada, g_mix, w_in, b_in, sinks, conv_w, w_out, g_ffn, w_ffn_in, w_ffn_out, g_final, loss_target, m_w_ada, m_b_ada, m_g_mix, m_w_in, m_b_in, m_sinks, m_conv_w, m_w_out, m_g_ffn, m_w_ffn_in, m_w_ffn_out, m_g_final, v_w_ada, v_b_ada, v_g_mix, v_w_in, v_b_in, v_sinks, v_conv_w, v_w_out, v_g_ffn, v_w_ffn_in, v_w_ffn_out, v_g_final):
    xs, tgt = x[0], loss_target[0]
    t, d = xs.shape
    zw = w_in.shape[2] * N_CHIP
    kvw2 = zw - 6 * d
    ff = w_ffn_out.shape[1] * N_CHIP
    n_mod = w_ada.shape[2] * N_CHIP // d
    mod_sh = w_ada.shape[2]
    cw_sh = conv_w.shape[2]
    assert d % (8 * LANES) == 0 and kvw2 == 2 * LANES and t % 512 == 0 and n_mod == 6
    xi, yi, ci = _mesh_pos()
    j_me = 2 * xi + yi
    pos = jnp.stack([ci, j_me]).astype(jnp.int32)
    tm = 512

    w_in_t, m_w_in_t, v_w_in_t = w_in[0].T, m_w_in[0].T, v_w_in[0].T
    assert d == 8 * LANES
    pack1 = jnp.concatenate([c.reshape(d // LANES, LANES), conv_w[0].reshape(-1, LANES)], axis=0)
    pack1 = jnp.pad(pack1, ((0, 16 - pack1.shape[0]), (0, 0)))
    b_ada_sh = lax.dynamic_slice(b_ada, (0, j_me * mod_sh), (1, mod_sh))
    g1, mod, w_in_g, later = _startup(pack1, w_ada[0], b_ada_sh, _cast_into_block(pos, w_in_t, "cast_w_in"),
                                      [w_out[0], w_ffn_in[0], w_ffn_out[0]])
    c_all = g1[:, :d // LANES, :].reshape(N_DEV, d)
    cw_rows = 3 * cw_sh // LANES
    conv_w_full = jnp.concatenate(
        [g1[2 * j, d // LANES:d // LANES + cw_rows, :].reshape(3, cw_sh) for j in range(N_CHIP)], axis=1)
    sh1, sc1, ga1, sh2, sc2, ga2 = [(mod, k) for k in range(6)]
    w_in_tf = w_in_g.reshape(zw, d)

    (z, h1), later = _in_proj(xs, g_mix, sc1, sh1, w_in_tf, b_in, min(t, 1024), zw // 5, ride=_x_gather_ici(later))
    (attn, merged), later = _mixer_fwd(z, sinks, conv_w_full, d, ride=_x_gather_d2d(later))
    w_out_f = later[0].reshape(d, d)
    w_ffn_in_f = later[1]
    w_ffn_out_f = later[2].reshape(ff, d)
    tml = min(t, 1024)
    y1, x1, h2 = _out_proj_fwd(merged, w_out_f, xs, ga1, g_ffn, sc2, sh2, tml)
    gu, act = _ffn_in_fwd(h2, w_ffn_in_f, ff, tml, ff // 2)
    dx2, dy2, st_loss = _ffn_out_loss(act, w_ffn_out_f, x1, tgt, ga2, g_final.reshape(1, d), tml)

    dgu = _ffn_out_bwd(dy2, w_ffn_out_f, gu, tml, ff // 2)
    tk = min(t, 2048)
    dw_ffn_out, _ = _wgrad(
        act, dy2, pl.BlockSpec((tk, ff // 2), lambda m, k: (k, m)), pl.BlockSpec((tk, d), lambda m, k: (k, 0)),
        pl.BlockSpec((ff // 2, d), lambda m, k: (m, 0)), SDS((ff, d), F32), (2, t // tk), "wgrad_ffn_out")
    dx1, dy1, st_ffn = _ffn_in_bwd(dgu, w_ffn_in_f, x1, dx2, y1, g_ffn, sc2, ga1, tm)
    dw_ffn_in, _ = _wgrad(
        h2, dgu, pl.BlockSpec((tk, d), lambda n, k: (k, 0)),
        pl.BlockSpec((None, tk, ff // 2), lambda n, k: (n // 2, k, n % 2)),
        pl.BlockSpec((None, d, ff // 2), lambda n, k: (n, 0, 0)), SDS((N_CHIP, d, ff // 2), F32),
        (N_CHIP, t // tk), "wgrad_ffn_in")
    dw_out, _ = _wgrad(
        merged, dy1, pl.BlockSpec((tk, d), lambda m, k: (k, 0)), pl.BlockSpec((tk, d), lambda m, k: (k, 0)),
        pl.BlockSpec((d, d), lambda m, k: (0, 0)), SDS((d, d), F32), (1, t // tk), "wgrad_out")

    early = [[g.reshape(N_CHIP, -1, g.shape[-1]) for g in pair] for pair in (dw_out, dw_ffn_in, dw_ffn_out)]
    early_names = ["w_out", "w_ffn_in", "w_ffn_out"]
    dmerged, _ = _out_proj_bwd(dy1, w_out_f, tml)
    (dz, dkv_shifted, db_z, db_kv, dcw, dsk), terms = _mixer_bwd(
        z, dmerged, attn, sinks, conv_w_full, d, ride=_x_reduce([e[0] for e in early], [e[1] for e in early]))
    dkv = dkv_shifted[BLOCK:BLOCK + t]
    fulls = [_sum_terms(pos, e[0], s, r, "sum_terms_" + nm)
             for e, s, r, nm in zip(early, terms[:3], terms[3:], early_names)]
    dw_in_t, (g_w_out, g_w_ffn_in, g_w_ffn_out) = _wgrad_in(dz, dkv, h1, tk, ride=_x_pair_exchange(fulls))
    dw_in_t = [g.reshape(N_CHIP, zw // N_CHIP, d) for g in dw_in_t]

    (grad_x, st_in), (from_sib, from_far) = _in_proj_bwd(dz, dkv, w_in_tf, xs, dx1, g_mix, sc1, tm,
                                                         ride=_x_reduce([dw_in_t[0]], [dw_in_t[1]]))
    g_w_in_half = _sum_terms(pos, dw_in_t[0], from_sib, from_far, "sum_terms_w_in")

    dmod = jnp.concatenate([st_in[0:1], st_in[1:2], st_ffn[3:4], st_ffn[0:1], st_ffn[1:2], st_loss[0:1]], axis=1)
    db_in = jnp.concatenate([db_z[0:1, :d], db_kv[0:1], db_z[0:1, d + kvw2:]], axis=1)
    seg = [dmod, st_in[2:3], db_in, dsk[0:1], dcw[0:3].reshape(1, 3 * d), st_ffn[2:3], st_loss[1:2],
           st_loss[2:3, :LANES]]
    sizes = [s.shape[1] for s in seg]
    pack2 = _to_lanes(jnp.concatenate(seg, axis=1))
    packs, g_w_in_t = _tail_exchange(pack2, g_w_in_half)
    tot = _pack_sum(packs).reshape(-1)
    offs = [sum(sizes[:k]) for k in range(len(sizes))]
    gb_ada, gg_mix, gb_in, gsinks, gcw, gg_ffn, gg_final, loss_v = [tot[o:o + s] for o, s in zip(offs, sizes)]
    loss = loss_v[0]
    gsinks = gsinks[:sinks.shape[1]]
    gcw_sh = lax.dynamic_slice(gcw.reshape(3, d), (0, j_me * cw_sh), (3, cw_sh))

    dmod_all = packs[:, :n_mod * d // LANES, :].reshape(N_DEV, n_mod * d)
    g_w_ada = _ada_wgrad(c_all, lax.dynamic_slice(dmod_all, (0, j_me * mod_sh), (N_DEV, mod_sh)))

    out_g, out_d, out_m, out_v = {}, {}, {}, {}
    big = {"w_ada": (w_ada[0], g_w_ada, m_w_ada[0], v_w_ada[0]),
           "w_out": (w_out[0], g_w_out, m_w_out[0], v_w_out[0]),
           "w_ffn_in": (w_ffn_in[0], g_w_ffn_in, m_w_ffn_in[0], v_w_ffn_in[0]),
           "w_ffn_out": (w_ffn_out[0], g_w_ffn_out, m_w_ffn_out[0], v_w_ffn_out[0])}
    for nm, (w, g, m, v) in big.items():
        out_g[nm], out_d[nm], out_m[nm], out_v[nm] = [o[None] for o in _adamw(w, g, m, v, "adamw_" + nm)]
    out_g["w_in"], out_d["w_in"], out_m["w_in"], out_v["w_in"] = [
        o.T[None] for o in _adamw(w_in_t, g_w_in_t, m_w_in_t, v_w_in_t, "adamw_w_in")]
    small = {"b_ada": (b_ada, gb_ada, m_b_ada, v_b_ada), "g_mix": (g_mix, gg_mix, m_g_mix, v_g_mix),
             "b_in": (b_in, gb_in, m_b_in, v_b_in), "sinks": (sinks, gsinks, m_sinks, v_sinks),
             "conv_w": (conv_w, gcw_sh, m_conv_w, v_conv_w), "g_ffn": (g_ffn, gg_ffn, m_g_ffn, v_g_ffn),
             "g_final": (g_final, gg_final, m_g_final, v_g_final)}
    def two_d(a):
        return a.reshape(-1, a.shape[-1])

    s_out = _adamw_small([tuple(two_d(a.reshape(w.shape)) for a in (w, g, m, v)) for w, g, m, v in small.values()])
    for (nm, (w, g, _, _)), res in zip(small.items(), s_out):
        out_g[nm] = g.reshape(w.shape)
        out_d[nm], out_m[nm], out_v[nm] = [o.reshape(w.shape) for o in res]

    order = ["w_ada", "b_ada", "g_mix", "w_in", "b_in", "sinks", "conv_w", "w_out", "g_ffn", "w_ffn_in", "w_ffn_out",
             "g_final"]
    return (loss, grad_x[None], *[out_g[k] for k in order], *[out_d[k] for k in order],
            *[out_m[k] for k in order], *[out_v[k] for k in order])
```

```python
import functools

import jax
import jax.numpy as jnp
from jax import lax
from jax.experimental import pallas as pl
from jax.experimental.pallas import tpu as pltpu

F32 = jnp.float32
BF16 = jnp.bfloat16
EPS = 1e-6
HEAD_DIM = 64
GROUP = 8
BLOCK = 128
LANES = 128
SUBLANES_BF16 = 16
N_DEV = 8
N_CHIP = 4
VMEM_LIMIT = 56 * 1024 * 1024
MESH = pl.DeviceIdType.MESH

ADAM_LR = 0.001
ADAM_B1 = 0.9
ADAM_B2 = 0.999
ADAM_EPS = 1e-08
ADAM_WD = 0.01
ADAM_STEP = 10

SDS = jax.ShapeDtypeStruct
ANY = pl.BlockSpec(memory_space=pl.ANY)
VMEM_SPEC = pl.BlockSpec(memory_space=pltpu.VMEM)
SMEM_SPEC = pl.BlockSpec(memory_space=pltpu.SMEM)


def _params(*sem):
    return pltpu.CompilerParams(dimension_semantics=sem, vmem_limit_bytes=VMEM_LIMIT)


def _vec(v, d):
    arr, k = v if isinstance(v, tuple) else (v, 0)
    return arr, pl.BlockSpec((1, d), lambda *_: (0, k))


def _mesh_pos():
    return lax.axis_index("x"), lax.axis_index("y"), lax.axis_index("c")


def _row_tile(rows, cols, itemsize=4, budget=1 << 20, mult=8):
    best = None
    for t in range(mult, rows + 1, mult):
        if rows % t == 0 and t * cols * itemsize <= budget:
            best = t
    if best is None:
        best = rows
    return best


def _gather_all(v_ref, out_ref, send_sems, recv_sems, local_sem):
    x, y, c = _mesh_pos()
    me = 4 * x + 2 * y + c
    mine = pltpu.make_async_copy(v_ref, out_ref.at[me], local_sem)
    mine.start()
    peers = []
    for k in range(1, N_DEV):
        px = 1 - x if k & 4 else x
        py = 1 - y if k & 2 else y
        pc = 1 - c if k & 1 else c
        peers.append((px, py, pc))

    def copy(k, block):
        return pltpu.make_async_remote_copy(
            src_ref=v_ref, dst_ref=out_ref.at[block], send_sem=send_sems.at[k], recv_sem=recv_sems.at[k],
            device_id=peers[k], device_id_type=MESH)

    sends = [copy(k, me) for k in range(N_DEV - 1)]
    for cp in sends:
        cp.start()
    for k, (px, py, pc) in enumerate(peers):
        copy(k, 4 * px + 2 * py + pc).wait_recv()
    for cp in sends:
        cp.wait_send()
    mine.wait()


def _small_sems():
    return [pltpu.SemaphoreType.DMA((N_DEV - 1,)), pltpu.SemaphoreType.DMA((N_DEV - 1,)), pltpu.SemaphoreType.DMA]


def _tail_exchange(pack, full):
    def body(pack_ref, full_unused, packs_ref, full_ref, s1, r1, l1, send_sem, recv_sem):
        del full_unused
        x, y, c = _mesh_pos()
        half = full_ref.shape[0] // 2
        rows = pl.ds(pl.multiple_of(c * half, 8), half)
        swap = pltpu.make_async_remote_copy(
            src_ref=full_ref.at[rows], dst_ref=full_ref.at[rows], send_sem=send_sem, recv_sem=recv_sem,
            device_id=(x, y, 1 - c), device_id_type=MESH)
        swap.start()
        _gather_all(pack_ref, packs_ref, s1, r1, l1)
        swap.wait()

    return pl.pallas_call(
        body, name="tail_exchange", out_shape=[SDS((N_DEV,) + pack.shape, pack.dtype), SDS(full.shape, full.dtype)],
        in_specs=[VMEM_SPEC, ANY], out_specs=[VMEM_SPEC, ANY], input_output_aliases={1: 1},
        scratch_shapes=_small_sems() + [pltpu.SemaphoreType.DMA, pltpu.SemaphoreType.DMA])(pack, full)


def _other_chips(x, y):
    return [(1 - x, y), (x, 1 - y), (1 - x, 1 - y)]


def _startup(pack, w_ada_sh, b_ada_sh, w_buf, later):
    d, n = w_ada_sh.shape
    kc = d // LANES
    n_l = len(later)
    chunk_rows = [_row_tile(a.shape[0], a.shape[1], budget=3 << 19, mult=SUBLANES_BF16) for a in later]

    def body(*refs):
        pack_ref, wa_hbm, ba_ref, w_in_unused = refs[:4]
        later_src = refs[4:4 + n_l]
        packs_ref, mine_ref, w_ref = refs[4 + n_l:7 + n_l]
        later_dst = refs[7 + n_l:7 + 2 * n_l]
        wa_scr, mod_scr, mod_ref = refs[7 + 2 * n_l:10 + 2 * n_l]
        f32_bufs = refs[10 + 2 * n_l:10 + 3 * n_l]
        bf16_bufs = refs[10 + 3 * n_l:10 + 4 * n_l]
        (s1, r1, l1, s2, r2, l2, send_sems, recv_sems, fsend_sems, frecv_sems, relay_send, relay_recv, wa_sem,
         cast_sems) = refs[10 + 4 * n_l:]
        del w_in_unused
        x, y, c = _mesh_pos()
        j_me = 2 * x + y
        chips = _other_chips(x, y)
        half = w_ref.shape[1] // 2

        def rows_of(which):
            return pl.ds(pl.multiple_of(which * half, SUBLANES_BF16), half)

        def copy(p, block, rows, over_ici):
            sems = (send_sems, recv_sems) if over_ici else (fsend_sems, frecv_sems)
            return pltpu.make_async_remote_copy(
                src_ref=w_ref.at[block, rows], dst_ref=w_ref.at[block, rows], send_sem=sems[0].at[p],
                recv_sem=sems[1].at[p], device_id=(*chips[p], c) if over_ici else (x, y, 1 - c), device_id_type=MESH)

        def block_of(p):
            return 2 * chips[p][0] + chips[p][1]

        def relay(q, block):
            rows = pl.ds(pl.multiple_of(c * half + q * (half // 2), SUBLANES_BF16), half // 2)
            return pltpu.make_async_remote_copy(
                src_ref=w_ref.at[block, rows], dst_ref=w_ref.at[block, rows], send_sem=relay_send.at[q],
                recv_sem=relay_recv.at[q], device_id=(*chips[1 - q], c), device_id_type=MESH)

        load_wa = pltpu.make_async_copy(wa_hbm, wa_scr, wa_sem)
        load_wa.start()
        _gather_all(pack_ref, packs_ref, s1, r1, l1)
        sends = [copy(p, j_me, rows_of(c), True) for p in range(2)]
        for cp in sends:
            cp.start()
        load_wa.wait()
        acc = jnp.zeros((N_DEV, n), F32)
        for k in range(kc):
            ck = packs_ref[:, k, :]
            sk = (ck * jax.nn.sigmoid(ck)).astype(BF16)
            acc = acc + jnp.dot(sk, wa_scr[k * LANES:(k + 1) * LANES, :].astype(BF16), preferred_element_type=F32)
        mod_scr[...] = acc + ba_ref[...]
        _gather_all(mod_scr, mod_ref, s2, r2, l2)
        for j in range(N_CHIP):
            mine_ref[:, j * n:(j + 1) * n] = mod_ref[2 * j, pl.ds(4 * x + 2 * y + c, 1), :]
        passed = []
        for q in range(2):
            copy(q, block_of(q), rows_of(c), True).wait_recv()
            for cp in (relay(q, block_of(q)), copy(q, block_of(q), rows_of(c), False)):
                cp.start()
                passed.append(cp)
        for src, dst, fbuf, bbuf in zip(later_src, later_dst, f32_bufs, bf16_bufs):
            cr = fbuf.shape[0]
            for k in range(src.shape[0] // cr):
                rows = pl.ds(k * cr, cr)
                cin = pltpu.make_async_copy(src.at[rows], fbuf, cast_sems.at[0])
                cin.start()
                cin.wait()
                bbuf[...] = fbuf[...].astype(BF16)
                cout = pltpu.make_async_copy(bbuf, dst.at[j_me, rows], cast_sems.at[1])
                cout.start()
                cout.wait()
        for q in range(2):
            relay(q, block_of(2)).wait_recv()
        fw = copy(2, block_of(2), rows_of(c), False)
        fw.start()
        for p in range(3):
            copy(p, block_of(p), rows_of(1 - c), False).wait_recv()
        for cp in sends + passed + [fw]:
            cp.wait_send()

    res = pl.pallas_call(
        body, name="startup",
        out_shape=[SDS((N_DEV,) + pack.shape, F32), SDS((1, N_CHIP * n), F32), SDS(w_buf.shape, w_buf.dtype)]
        + [SDS((N_CHIP,) + a.shape, BF16) for a in later],
        in_specs=[VMEM_SPEC, ANY, VMEM_SPEC, ANY] + [ANY] * n_l, out_specs=[VMEM_SPEC, VMEM_SPEC, ANY] + [ANY] * n_l,
        input_output_aliases={3: 2},
        scratch_shapes=[pltpu.VMEM((d, n), F32), pltpu.VMEM((N_DEV, n), F32), pltpu.VMEM((N_DEV, N_DEV, n), F32)]
        + [pltpu.VMEM((cr, a.shape[1]), F32) for cr, a in zip(chunk_rows, later)]
        + [pltpu.VMEM((cr, a.shape[1]), BF16) for cr, a in zip(chunk_rows, later)]
        + _small_sems() + _small_sems()
        + [pltpu.SemaphoreType.DMA((3,))] * 4 + [pltpu.SemaphoreType.DMA((2,))] * 2 + [pltpu.SemaphoreType.DMA]
        + [pltpu.SemaphoreType.DMA((2,))],
        compiler_params=pltpu.CompilerParams(vmem_limit_bytes=VMEM_LIMIT),
    )(pack, w_ada_sh, b_ada_sh, w_buf, *later)
    return res[0], res[1], res[2], list(res[3:])


class _Exchange:
    def __init__(self, operands, out_shape, in_place, n_sems, copies):
        self.operands, self.out_shape, self.in_place, self.n_sems, self.copies = (
            list(operands), list(out_shape), in_place, n_sems, copies)

    def sems(self):
        return [pltpu.SemaphoreType.DMA((self.n_sems,)), pltpu.SemaphoreType.DMA((self.n_sems,))]


def _x_gather_ici(bufs):
    def copies(ins, outs, send_sems, recv_sems):
        x, y, c = _mesh_pos()
        chips = _other_chips(x, y)
        out = []
        for w in range(len(outs)):
            half = outs[w].shape[1] // 2
            rows = pl.ds(pl.multiple_of(c * half, SUBLANES_BF16), half)
            for p in range(3):
                out.append(pltpu.make_async_remote_copy(
                    src_ref=outs[w].at[2 * x + y, rows], dst_ref=outs[w].at[2 * x + y, rows],
                    send_sem=send_sems.at[w * 3 + p], recv_sem=recv_sems.at[w * 3 + p],
                    device_id=(*chips[p], c), device_id_type=MESH))
        return out

    return _Exchange(bufs, [SDS(b.shape, b.dtype) for b in bufs], True, 3 * len(bufs), copies)


def _x_gather_d2d(bufs):
    def copies(ins, outs, send_sems, recv_sems):
        x, y, c = _mesh_pos()
        chips = _other_chips(x, y)
        out = []
        for w in range(len(outs)):
            half = outs[w].shape[1] // 2
            rows = pl.ds(pl.multiple_of(c * half, SUBLANES_BF16), half)
            for p in range(3):
                block = 2 * chips[p][0] + chips[p][1]
                out.append(pltpu.make_async_remote_copy(
                    src_ref=outs[w].at[block, rows], dst_ref=outs[w].at[block, rows],
                    send_sem=send_sems.at[w * 3 + p], recv_sem=recv_sems.at[w * 3 + p],
                    device_id=(x, y, 1 - c), device_id_type=MESH))
        return out

    return _Exchange(bufs, [SDS(b.shape, b.dtype) for b in bufs], True, 3 * len(bufs), copies)


N_REMOTE = 6


def _x_reduce(grads32, grads16):
    n_w = len(grads32)

    def copies(ins, outs, send_sems, recv_sems):
        g32, g16 = ins[:n_w], ins[n_w:]
        from_sib, from_far = outs[:n_w], outs[n_w:]
        x, y, c = _mesh_pos()
        chips = _other_chips(x, y)
        out = []
        for w in range(n_w):
            half = g32[w].shape[1] // 2
            k0 = w * (N_REMOTE + 1)
            out.append(pltpu.make_async_remote_copy(
                src_ref=g32[w].at[2 * x + y, pl.ds(pl.multiple_of((1 - c) * half, SUBLANES_BF16), half), :],
                dst_ref=from_sib[w], send_sem=send_sems.at[k0], recv_sem=recv_sems.at[k0],
                device_id=(x, y, 1 - c), device_id_type=MESH))
            for p in range(3):
                for f in range(2):
                    tc = c if f == 0 else 1 - c
                    k = 2 * p + f
                    out.append(pltpu.make_async_remote_copy(
                        src_ref=g16[w].at[2 * chips[p][0] + chips[p][1],
                                          pl.ds(pl.multiple_of(tc * half, SUBLANES_BF16), half), :],
                        dst_ref=from_far[w].at[k], send_sem=send_sems.at[k0 + 1 + k], recv_sem=recv_sems.at[k0 + 1 + k],
                        device_id=(*chips[p], tc), device_id_type=MESH))
        return out

    shapes = ([SDS((g.shape[1] // 2, g.shape[2]), g.dtype) for g in grads32]
              + [SDS((N_REMOTE, g.shape[1] // 2, g.shape[2]), g.dtype) for g in grads16])
    return _Exchange(list(grads32) + list(grads16), shapes, False, (N_REMOTE + 1) * n_w, copies)


def _x_pair_exchange(fulls):
    def copies(ins, outs, send_sems, recv_sems):
        x, y, c = _mesh_pos()
        out = []
        for w in range(len(outs)):
            half = outs[w].shape[0] // 2
            rows = pl.ds(pl.multiple_of(c * half, 8), half)
            out.append(pltpu.make_async_remote_copy(
                src_ref=outs[w].at[rows], dst_ref=outs[w].at[rows], send_sem=send_sems.at[w],
                recv_sem=recv_sems.at[w], device_id=(x, y, 1 - c), device_id_type=MESH))
        return out

    return _Exchange(fulls, [SDS(f.shape, f.dtype) for f in fulls], True, len(fulls), copies)


def _pallas(body, *, name, grid, in_specs, out_specs, out_shape, args, scratch=(), sem=None, ride=None):
    single = not isinstance(out_specs, (list, tuple))
    out_specs_l = [out_specs] if single else list(out_specs)
    out_shape_l = [out_shape] if single else list(out_shape)
    n_in, n_out, n_scr = len(in_specs), len(out_specs_l), len(scratch)
    if ride is None:
        res = pl.pallas_call(body, name=name, grid=grid, in_specs=list(in_specs), out_specs=out_specs,
                             out_shape=out_shape, scratch_shapes=list(scratch), compiler_params=_params(*sem))(*args)
        return res, None
    n_x, n_xo = len(ride.operands), len(ride.out_shape)

    def full_body(*refs):
        ins, x_ins = refs[:n_in], refs[n_in:n_in + n_x]
        outs = refs[n_in + n_x:n_in + n_x + n_out]
        x_outs = refs[n_in + n_x + n_out:n_in + n_x + n_out + n_xo]
        rest = refs[n_in + n_x + n_out + n_xo:]
        scr, (send_sems, recv_sems) = rest[:n_scr], rest[n_scr:]
        first = functools.reduce(jnp.logical_and, [pl.program_id(a) == 0 for a in range(len(grid))])
        last = functools.reduce(jnp.logical_and, [pl.program_id(a) == grid[a] - 1 for a in range(len(grid))])

        @pl.when(first)
        def _():
            for cp in ride.copies(x_ins, x_outs, send_sems, recv_sems):
                cp.start()

        body(*ins, *outs, *scr)

        @pl.when(last)
        def _():
            for cp in ride.copies(x_ins, x_outs, send_sems, recv_sems):
                cp.wait()

    res = pl.pallas_call(
        full_body, name=name, grid=grid, in_specs=list(in_specs) + [ANY] * n_x,
        out_specs=out_specs_l + [ANY] * n_xo, out_shape=out_shape_l + ride.out_shape,
        input_output_aliases={n_in + k: n_out + k for k in range(n_x)} if ride.in_place else {},
        scratch_shapes=list(scratch) + ride.sems(),
        compiler_params=_params(*(["arbitrary"] * len(grid))))(*args, *ride.operands)
    own = res[0] if single else list(res[:n_out])
    return own, list(res[n_out:])


def _cast_into_block(pos, w, name):
    rows, cols = w.shape
    tr = _row_tile(rows, cols, mult=SUBLANES_BF16)

    def body(pos_ref, w_ref, o_ref):
        del pos_ref
        o_ref[...] = w_ref[...].astype(BF16)

    return pl.pallas_call(
        body, name=name,
        grid_spec=pltpu.PrefetchScalarGridSpec(
            num_scalar_prefetch=1, grid=(rows // tr,),
            in_specs=[pl.BlockSpec((tr, cols), lambda i, pos_ref: (i, 0))],
            out_specs=pl.BlockSpec((None, tr, cols), lambda i, pos_ref: (pos_ref[1], i, 0))),
        out_shape=SDS((N_CHIP, rows, cols), BF16), compiler_params=_params("parallel"))(pos, w)


def _sum_terms(pos, grad, from_sib, from_far, name):
    _, rows, cols = grad.shape
    half = rows // 2
    tr = _row_tile(half, cols, mult=SUBLANES_BF16)
    nblk = half // tr

    def body(pos_ref, g_ref, s_ref, r_ref, o_ref):
        del pos_ref
        acc = g_ref[...] + s_ref[...]
        for k in range(N_REMOTE):
            acc = acc + r_ref[k].astype(F32)
        o_ref[...] = acc

    return pl.pallas_call(
        body, name=name,
        grid_spec=pltpu.PrefetchScalarGridSpec(
            num_scalar_prefetch=1, grid=(nblk,),
            in_specs=[pl.BlockSpec((None, tr, cols), lambda i, pos_ref: (pos_ref[1], pos_ref[0] * nblk + i, 0)),
                      pl.BlockSpec((tr, cols), lambda i, pos_ref: (i, 0)),
                      pl.BlockSpec((N_REMOTE, tr, cols), lambda i, pos_ref: (0, i, 0))],
            out_specs=pl.BlockSpec((tr, cols), lambda i, pos_ref: (pos_ref[0] * nblk + i, 0))),
        out_shape=SDS((rows, cols), F32),
        compiler_params=_params("parallel"),
    )(pos, grad, from_sib, from_far)


def _adamw(w, g, m, v, name):
    rows, cols = w.shape
    tr = _row_tile(rows, cols)

    def body(w_ref, g_ref, m_ref, v_ref, go_ref, d_ref, nm_ref, nv_ref):
        go_ref[...] = g_ref[...]
        _adamw_update(w_ref, g_ref, m_ref, v_ref, d_ref, nm_ref, nv_ref)

    spec = pl.BlockSpec((tr, cols), lambda i: (i, 0))
    return pl.pallas_call(body, name=name, grid=(rows // tr,), in_specs=[spec] * 4, out_specs=[spec] * 4,
                          out_shape=[SDS((rows, cols), F32)] * 4, compiler_params=_params("parallel"))(w, g, m, v)


def _adamw_update(w_ref, g_ref, m_ref, v_ref, d_ref, nm_ref, nv_ref):
    gg = g_ref[...]
    nm = ADAM_B1 * m_ref[...] + (1.0 - ADAM_B1) * gg
    nv = ADAM_B2 * v_ref[...] + (1.0 - ADAM_B2) * (gg * gg)
    m_hat = nm / (1.0 - ADAM_B1 ** ADAM_STEP)
    v_hat = nv / (1.0 - ADAM_B2 ** ADAM_STEP)
    d_ref[...] = -ADAM_LR * (m_hat / (jnp.sqrt(v_hat) + ADAM_EPS) + ADAM_WD * w_ref[...])
    nm_ref[...] = nm
    nv_ref[...] = nv


def _adamw_small(params):
    n_p = len(params)

    def body(*refs):
        ins, outs = refs[:4 * n_p], refs[4 * n_p:]
        for k in range(n_p):
            _adamw_update(*ins[4 * k:4 * k + 4], *outs[3 * k:3 * k + 3])

    flat = [a for tup in params for a in tup]
    res = pl.pallas_call(
        body, name="adamw_small", in_specs=[VMEM_SPEC] * (4 * n_p), out_specs=[VMEM_SPEC] * (3 * n_p),
        out_shape=[SDS(tup[0].shape, F32) for tup in params for _ in range(3)])(*flat)
    return [res[3 * k:3 * k + 3] for k in range(n_p)]


def _pack_sum(gathered):
    _, rows, cols = gathered.shape

    def body(g_ref, o_ref):
        acc = g_ref[0]
        for d in range(1, N_DEV):
            acc = acc + g_ref[d]
        o_ref[...] = acc

    return pl.pallas_call(body, name="pack_sum", in_specs=[VMEM_SPEC], out_specs=VMEM_SPEC,
                          out_shape=SDS((rows, cols), F32))(gathered)


def _ada_wgrad(c_all, dmod_sh):
    d = c_all.shape[1]
    n = dmod_sh.shape[1]
    tn = 512

    def body(c_ref, g_ref, o_ref):
        cc = c_ref[...]
        s = cc * jax.nn.sigmoid(cc)
        o_ref[...] = lax.dot_general(s, g_ref[...], (((0,), (0,)), ((), ())), preferred_element_type=F32,
                                     precision=lax.Precision.HIGHEST)

    return pl.pallas_call(
        body, name="ada_wgrad", grid=(n // tn,),
        in_specs=[pl.BlockSpec((N_DEV, d), lambda j: (0, 0)), pl.BlockSpec((N_DEV, tn), lambda j: (0, j))],
        out_specs=pl.BlockSpec((d, tn), lambda j: (0, j)),
        out_shape=SDS((d, n), F32), compiler_params=_params("parallel"))(c_all, dmod_sh)


def _rms(xf):
    return lax.rsqrt(jnp.mean(xf * xf, axis=-1, keepdims=True) + EPS)


def _in_proj(x, g, sc, sh, wt, b, tm, tn, ride=None):
    t, d = x.shape
    n = wt.shape[0]

    def body(x_ref, g_ref, sc_ref, sh_ref, w_ref, b_ref, z_ref, h_ref):
        @pl.when(pl.program_id(1) == 0)
        def _():
            xf = x_ref[...]
            h_ref[...] = ((xf * _rms(xf) * g_ref[...]) * (1.0 + sc_ref[...]) + sh_ref[...]).astype(BF16)

        acc = lax.dot_general(h_ref[...], w_ref[...], (((1,), (1,)), ((), ())), preferred_element_type=F32)
        z_ref[...] = (acc + b_ref[...]).astype(BF16)

    row = pl.BlockSpec((tm, d), lambda i, j: (i, 0))
    vecs, vec_specs = zip(*[_vec(v, d) for v in (g, sc, sh)])
    return _pallas(
        body, name="in_proj", grid=(t // tm, n // tn),
        in_specs=[row, *vec_specs, pl.BlockSpec((tn, d), lambda i, j: (j, 0)),
                  pl.BlockSpec((1, tn), lambda i, j: (0, j))],
        out_specs=[pl.BlockSpec((tm, tn), lambda i, j: (i, j)), row],
        out_shape=[SDS((t, n), BF16), SDS((t, d), BF16)], args=(x, *vecs, wt, b),
        sem=("parallel", "arbitrary"), ride=ride)


def _segments(d, kvw2):
    o = d + kvw2
    names = ("cb", "cc", "cx", "ga", "gc")
    seg = {nm: slice(o + k * d, o + (k + 1) * d) for k, nm in enumerate(names)}
    seg["q"], seg["kv"] = slice(0, d), slice(d, o)
    return seg


def _attn_masks():
    rows = 4 * BLOCK
    r = lax.broadcasted_iota(jnp.int32, (rows, 2 * BLOCK), 0) & (BLOCK - 1)
    col = lax.broadcasted_iota(jnp.int32, (rows, 2 * BLOCK), 1)
    return (col > r) & (col <= r + BLOCK), col


def _kv_variants(kv, n_kv_w):
    assert n_kv_w == LANES
    kb, vb = kv[:, :LANES] * (HEAD_DIM ** -0.5), kv[:, LANES:]
    kr, vr = pltpu.roll(kb, HEAD_DIM, 1), pltpu.roll(vb, HEAD_DIM, 1)
    lane = lax.broadcasted_iota(jnp.int32, kb.shape, 1)
    lo = lane < HEAD_DIM
    zero = jnp.zeros_like(kb)
    k_eff = [[None, None], [None, None]]
    v_eff = [[None, None], [None, None]]
    for h in range(2):
        for e in range(2):
            ksrc, vsrc = (kb, vb) if e == h else (kr, vr)
            keep = lo if e == 0 else jnp.logical_not(lo)
            k_eff[h][e] = jnp.where(keep, ksrc, zero)
            v_eff[h][e] = jnp.where(keep, vsrc, zero)
    return k_eff, v_eff


def _sink_column(sinks_ref, h, e):
    rowblk = lax.broadcasted_iota(jnp.int32, (4 * BLOCK, 1), 0) // BLOCK
    col = jnp.zeros((4 * BLOCK, 1), F32)
    for j in range(4):
        col = jnp.where(rowblk == j, sinks_ref[0, GROUP * h + 2 * j + e], col)
    return col


def _softmax_sink(s, valid, sink):
    s = jnp.where(valid, s, -jnp.inf)
    m = jnp.maximum(jnp.max(s, axis=-1, keepdims=True), sink)
    p = jnp.exp(s - m)
    psink = jnp.exp(sink - m)
    den = jnp.sum(p, axis=-1, keepdims=True) + psink
    inv = 1.0 / den
    return p * inv, psink * inv


def _shift_down(a, s, prev):
    rows = a.shape[0]
    out = pltpu.roll(a, s, 0)
    row = lax.broadcasted_iota(jnp.int32, a.shape, 0)
    for t in range(s):
        out = jnp.where(row == t, prev[SUBLANES_BF16 - s + t:SUBLANES_BF16 - s + t + 1, :], out)
    del rows
    return out


def _shift_up(a, s, nxt):
    rows = a.shape[0]
    out = pltpu.roll(a, rows - s, 0)
    row = lax.broadcasted_iota(jnp.int32, a.shape, 0)
    for t in range(s):
        out = jnp.where(row == rows - s + t, nxt[t:t + 1, :], out)
    return out


def _stack_pairs(ref, h, rows=slice(None)):
    return jnp.concatenate([ref[rows, (4 * h + j) * LANES:(4 * h + j + 1) * LANES] for j in range(4)], axis=0)


FWD_BLOCKS = 4


def _mixer_fwd(z, sinks, conv_w, d, ride=None):
    t, zw = z.shape
    kvw2 = zw - 6 * d
    tq = FWD_BLOCKS * BLOCK
    halo = tq // SUBLANES_BF16
    seg = _segments(d, kvw2)

    def body(z_ref, kvp_ref, prev_ref, sinks_ref, cw_ref, attn_ref, merged_ref):
        n = pl.program_id(0)
        band, col = _attn_masks()
        for b in range(FWD_BLOCKS):
            rows = slice(b * BLOCK, (b + 1) * BLOCK)
            before = slice((b - 1) * BLOCK, b * BLOCK)
            kv_prev = kvp_ref[...] if b == 0 else z_ref[before, seg["kv"]]
            kv = jnp.concatenate([kv_prev, z_ref[rows, seg["kv"]]], axis=0)
            k_eff, v_eff = _kv_variants(kv, kvw2 // 2)
            valid = band & ((n > 0) | (col >= BLOCK)) if b == 0 else band
            for h in range(2):
                q4 = _stack_pairs(z_ref, h, rows)
                o4 = jnp.zeros((4 * BLOCK, LANES), F32)
                for e in range(2):
                    s = lax.dot_general(q4, k_eff[h][e], (((1,), (1,)), ((), ())), preferred_element_type=F32)
                    p, _ = _softmax_sink(s, valid, _sink_column(sinks_ref, h, e))
                    o4 = o4 + jnp.dot(p.astype(BF16), v_eff[h][e], preferred_element_type=F32)
                for j in range(4):
                    attn_ref[rows, (4 * h + j) * LANES:(4 * h + j + 1) * LANES] = (
                        o4[j * BLOCK:(j + 1) * BLOCK].astype(BF16))
            cb = z_ref[rows, seg["cb"]].astype(F32)
            p_in = z_ref[rows, seg["cc"]].astype(F32) * z_ref[rows, seg["cx"]].astype(F32)
            if b == 0:
                prev = jnp.where(n > 0, prev_ref[:, seg["cc"]].astype(F32) * prev_ref[:, seg["cx"]].astype(F32), 0.0)
            else:
                tail = slice(b * BLOCK - SUBLANES_BF16, b * BLOCK)
                prev = z_ref[tail, seg["cc"]].astype(F32) * z_ref[tail, seg["cx"]].astype(F32)
            cconv = (cw_ref[0:1, :] * _shift_down(p_in, 2, prev) + cw_ref[1:2, :] * _shift_down(p_in, 1, prev)
                     + cw_ref[2:3, :] * p_in)
            sa = jax.nn.sigmoid(z_ref[rows, seg["ga"]].astype(F32))
            sg = jax.nn.sigmoid(z_ref[rows, seg["gc"]].astype(F32))
            merged_ref[rows, :] = (sa * attn_ref[rows, :].astype(F32) + sg * (cb * cconv)).astype(BF16)

    blk = pl.BlockSpec((tq, d), lambda n: (n, 0))
    return _pallas(
        body, name="mixer_fwd", grid=(t // tq,),
        in_specs=[pl.BlockSpec((tq, zw), lambda n: (n, 0)),
                  pl.BlockSpec((BLOCK, kvw2), lambda n: (jnp.maximum(n * FWD_BLOCKS - 1, 0), d // kvw2)),
                  pl.BlockSpec((SUBLANES_BF16, zw), lambda n: (jnp.maximum(n * halo - 1, 0), 0)),
                  SMEM_SPEC, pl.BlockSpec((3, d), lambda n: (0, 0))],
        out_specs=[blk, blk],
        out_shape=[SDS((t, d), BF16), SDS((t, d), BF16)],
        args=(z, z, z, sinks, conv_w), sem=("parallel",), ride=ride)


def _out_proj_fwd(merged, w_out, x, ga1, g_ffn, sc2, sh2, tm):
    t, d = x.shape

    def body(m_ref, w_ref, x_ref, ga_ref, g_ref, sc_ref, sh_ref, y_ref, x1_ref, h_ref):
        y = jnp.dot(m_ref[...], w_ref[...], preferred_element_type=F32)
        x1 = x_ref[...] + ga_ref[...] * y
        y_ref[...] = y.astype(BF16)
        x1_ref[...] = x1
        h_ref[...] = ((x1 * _rms(x1) * g_ref[...]) * (1.0 + sc_ref[...]) + sh_ref[...]).astype(BF16)

    row = pl.BlockSpec((tm, d), lambda i: (i, 0))
    vecs, vec_specs = zip(*[_vec(v, d) for v in (ga1, g_ffn, sc2, sh2)])
    return pl.pallas_call(
        body, name="out_proj_fwd", grid=(t // tm,),
        in_specs=[row, pl.BlockSpec((d, d), lambda i: (0, 0)), row, *vec_specs],
        out_specs=[row, row, row],
        out_shape=[SDS((t, d), BF16), SDS((t, d), F32), SDS((t, d), BF16)],
        compiler_params=_params("parallel"))(merged, w_out, x, *vecs)


def _ffn_in_fwd(h2, w, ff, tm, tn):
    t, d = h2.shape
    nj = ff // tn
    assert w.shape == (2 * nj, d, tn)

    def body(h_ref, wg_ref, wu_ref, gu_ref, act_ref):
        hh = h_ref[...]
        g = jnp.dot(hh, wg_ref[...], preferred_element_type=F32)
        u = jnp.dot(hh, wu_ref[...], preferred_element_type=F32)
        sg = jax.nn.sigmoid(g)
        silu = g * sg
        gu_ref[0] = (u * (sg + silu * (1.0 - sg))).astype(BF16)
        gu_ref[1] = silu.astype(BF16)
        act_ref[...] = (silu * u).astype(BF16)

    return pl.pallas_call(
        body, name="ffn_in_fwd", grid=(nj, t // tm),
        in_specs=[pl.BlockSpec((tm, d), lambda j, i: (i, 0)), pl.BlockSpec((None, d, tn), lambda j, i: (j, 0, 0)),
                  pl.BlockSpec((None, d, tn), lambda j, i: (j + nj, 0, 0))],
        out_specs=[pl.BlockSpec((2, tm, tn), lambda j, i: (0, i, j)), pl.BlockSpec((tm, tn), lambda j, i: (i, j))],
        out_shape=[SDS((2, t, ff), BF16), SDS((t, ff), BF16)],
        compiler_params=_params("parallel", "parallel"))(h2, w, w)


def _ffn_out_loss(act, w, x1, target, ga2, g_final, tm):
    t, d = x1.shape
    ff = act.shape[1]

    def body(a_ref, w_ref, x1_ref, tg_ref, ga_ref, gf_ref, dx2_ref, dy2_ref, st_ref):
        @pl.when(pl.program_id(0) == 0)
        def _():
            st_ref[...] = jnp.zeros_like(st_ref)

        halves = [slice(k * (tm // 2), (k + 1) * (tm // 2)) for k in range(2)]
        y2s = [jnp.dot(a_ref[rows, :], w_ref[...], preferred_element_type=F32) for rows in halves]
        for rows, y2 in zip(halves, y2s):
            x2 = x1_ref[rows, :] + ga_ref[...] * y2
            r = _rms(x2)
            yn = x2 * r
            err = yn * gf_ref[...] - tg_ref[rows, :]
            loss = 0.5 * jnp.sum(jnp.mean(err * err, axis=-1, keepdims=True), axis=0, keepdims=True)
            dy = err * (1.0 / d)
            u = dy * gf_ref[...]
            dx2 = r * (u - yn * jnp.mean(u * yn, axis=-1, keepdims=True))
            dx2_ref[rows, :] = dx2
            dy2_ref[rows, :] = (ga_ref[...] * dx2).astype(BF16)
            st_ref[0:1, :] += jnp.sum(dx2 * y2, axis=0, keepdims=True)
            st_ref[1:2, :] += jnp.sum(dy * yn, axis=0, keepdims=True)
            st_ref[2:3, :] += jnp.broadcast_to(loss, (1, d))

    row = pl.BlockSpec((tm, d), lambda i: (i, 0))
    vecs, vec_specs = zip(*[_vec(v, d) for v in (ga2, g_final)])
    return pl.pallas_call(
        body, name="ffn_out_loss", grid=(t // tm,),
        in_specs=[pl.BlockSpec((tm, ff), lambda i: (i, 0)),
                  pl.BlockSpec((ff, d), lambda i: (0, 0), pipeline_mode=pl.Buffered(1)), row, row, *vec_specs],
        out_specs=[row, row, pl.BlockSpec((8, d), lambda i: (0, 0))],
        out_shape=[SDS((t, d), F32), SDS((t, d), BF16), SDS((8, d), F32)],
        compiler_params=_params("arbitrary"))(act, w, x1, target, *vecs)


def _ffn_out_bwd(dy2, w, gu, tm, tn):
    t, d = dy2.shape
    ff = w.shape[0]

    def body(dy_ref, w_ref, gu_ref, o_ref):
        dy = dy_ref[...]
        for lo in range(0, tn, 3 * LANES):
            cols = slice(lo, min(lo + 3 * LANES, tn))
            dact = lax.dot_general(dy, w_ref[cols, :], (((1,), (1,)), ((), ())), preferred_element_type=F32)
            o_ref[0, :, cols] = (dact * gu_ref[0, :, cols].astype(F32)).astype(BF16)
            o_ref[1, :, cols] = (dact * gu_ref[1, :, cols].astype(F32)).astype(BF16)

    gu_spec = pl.BlockSpec((2, tm, tn), lambda j, i: (0, i, j))
    return pl.pallas_call(
        body, name="ffn_out_bwd", grid=(ff // tn, t // tm),
        in_specs=[pl.BlockSpec((tm, d), lambda j, i: (i, 0)), pl.BlockSpec((tn, d), lambda j, i: (j, 0)), gu_spec],
        out_specs=gu_spec, out_shape=SDS((2, t, ff), BF16),
        compiler_params=_params("parallel", "parallel"))(dy2, w, gu)


def _wgrad(a, b, a_spec, b_spec, out_spec, out_shape, grid, name, ride=None):
    def body(a_ref, b_ref, o_ref, o16_ref):
        k = pl.program_id(len(grid) - 1)

        @pl.when(k == 0)
        def _():
            o_ref[...] = jnp.zeros_like(o_ref)

        o_ref[...] += lax.dot_general(a_ref[...], b_ref[...], (((0,), (0,)), ((), ())), preferred_element_type=F32)

        @pl.when(k == grid[-1] - 1)
        def _():
            o16_ref[...] = o_ref[...].astype(BF16)

    return _pallas(
        body, name=name, grid=grid, in_specs=[a_spec, b_spec], out_specs=[out_spec, out_spec],
        out_shape=[out_shape, SDS(out_shape.shape, BF16)], args=(a, b),
        sem=["parallel"] * (len(grid) - 1) + ["arbitrary"], ride=ride)


def _ffn_in_bwd(dgu, w, x1, dx2, y1, g_ffn, sc2, ga1, tm):
    t, d = x1.shape
    ff = dgu.shape[2]
    n_sh, _, sw = w.shape
    per = ff // sw
    nt = (((1,), (1,)), ((), ()))

    def body(a_ref, w_ref, x1_ref, dx2_ref, y1_ref, g_ref, sc_ref, ga_ref, dx1_ref, dy1_ref, st_ref):
        @pl.when(pl.program_id(0) == 0)
        def _():
            st_ref[...] = jnp.zeros_like(st_ref)

        dh = None
        for j in range(n_sh):
            part = lax.dot_general(a_ref[j // per, :, (j % per) * sw:(j % per + 1) * sw], w_ref[j], nt,
                                   preferred_element_type=F32)
            dh = part if dh is None else dh + part
        x1 = x1_ref[...]
        r = _rms(x1)
        xn = x1 * r
        g = g_ref[...]
        dn = dh * (1.0 + sc_ref[...])
        u = dn * g
        dx1 = dx2_ref[...] + r * (u - xn * jnp.mean(u * xn, axis=-1, keepdims=True))
        dx1_ref[...] = dx1
        dy1_ref[...] = (ga_ref[...] * dx1).astype(BF16)
        st_ref[0:1, :] += jnp.sum(dh, axis=0, keepdims=True)
        st_ref[1:2, :] += jnp.sum(dh * (xn * g), axis=0, keepdims=True)
        st_ref[2:3, :] += jnp.sum(dn * xn, axis=0, keepdims=True)
        st_ref[3:4, :] += jnp.sum(dx1 * y1_ref[...].astype(F32), axis=0, keepdims=True)

    row = pl.BlockSpec((tm, d), lambda i: (i, 0))
    vecs, vec_specs = zip(*[_vec(v, d) for v in (g_ffn, sc2, ga1)])
    return pl.pallas_call(
        body, name="ffn_in_bwd", grid=(t // tm,),
        in_specs=[pl.BlockSpec((2, tm, ff), lambda i: (0, i, 0)),
                  pl.BlockSpec((n_sh, d, sw), lambda i: (0, 0, 0), pipeline_mode=pl.Buffered(1)),
                  row, row, row, *vec_specs],
        out_specs=[row, row, pl.BlockSpec((8, d), lambda i: (0, 0))],
        out_shape=[SDS((t, d), F32), SDS((t, d), BF16), SDS((8, d), F32)],
        compiler_params=_params("arbitrary"))(dgu, w, x1, dx2, y1, *vecs)


def _out_proj_bwd(dy1, w_out, tm, ride=None):
    t, d = dy1.shape

    def body(dy_ref, w_ref, o_ref):
        o_ref[...] = lax.dot_general(dy_ref[...], w_ref[...], (((1,), (1,)), ((), ())),
                                     preferred_element_type=F32).astype(BF16)

    row = pl.BlockSpec((tm, d), lambda i: (i, 0))
    return _pallas(body, name="out_proj_bwd", grid=(t // tm,),
                   in_specs=[row, pl.BlockSpec((d, d), lambda i: (0, 0))], out_specs=row,
                   out_shape=SDS((t, d), BF16), args=(dy1, w_out), sem=("parallel",), ride=ride)


BWD_BLOCKS = 2


def _mixer_bwd(z, dmerged, attn, sinks, conv_w, d, ride=None):
    t, zw = z.shape
    kvw2 = zw - 6 * d
    tq = BWD_BLOCKS * BLOCK
    steps = t // tq
    halo = tq // SUBLANES_BF16
    last_halo = t // SUBLANES_BF16 - 1
    scale = HEAD_DIM ** -0.5
    seg = _segments(d, kvw2)

    def body(z_ref, kvp_ref, prev_ref, next_ref, dm_ref, dmn_ref, attn_ref, sinks_ref, cw_ref,
             dz_ref, dkv_ref, db_ref, dbkv_ref, dcw_ref, dsk_ref, carry_ref):
        n = pl.program_id(0)

        @pl.when(n == 0)
        def _():
            carry_ref[...] = jnp.zeros_like(carry_ref)
            db_ref[...] = jnp.zeros_like(db_ref)
            dbkv_ref[...] = jnp.zeros_like(dbkv_ref)
            dcw_ref[...] = jnp.zeros_like(dcw_ref)
            dsk_ref[...] = jnp.zeros_like(dsk_ref)

        def one_block(b, pending):
            rows = slice(b * BLOCK, (b + 1) * BLOCK)
            before = slice((b - 1) * BLOCK, b * BLOCK)
            dm = dm_ref[rows, :].astype(F32)
            sa = jax.nn.sigmoid(z_ref[rows, seg["ga"]].astype(F32))
            dga = dm * attn_ref[rows, :].astype(F32) * sa * (1.0 - sa)
            dz_ref[rows, seg["ga"]] = dga.astype(BF16)
            db_ref[0:1, seg["ga"]] += jnp.sum(dga, axis=0, keepdims=True)
            dattn = (dm * sa).astype(BF16)

            kv_prev = kvp_ref[...] if b == 0 else z_ref[before, seg["kv"]]
            kv = jnp.concatenate([kv_prev, z_ref[rows, seg["kv"]]], axis=0)
            k_eff, v_eff = _kv_variants(kv, kvw2 // 2)
            band, col = _attn_masks()
            valid = band & ((n > 0) | (col >= BLOCK)) if b == 0 else band
            lane_lo = lax.broadcasted_iota(jnp.int32, (2 * BLOCK, LANES), 1) < HEAD_DIM
            sink_lane = lax.broadcasted_iota(jnp.int32, (1, LANES), 1)
            rowblk = lax.broadcasted_iota(jnp.int32, (4 * BLOCK, 1), 0) // BLOCK
            dk_acc = [jnp.zeros((2 * BLOCK, LANES), F32), jnp.zeros((2 * BLOCK, LANES), F32)]
            dv_acc = [jnp.zeros((2 * BLOCK, LANES), F32), jnp.zeros((2 * BLOCK, LANES), F32)]
            dsink = jnp.zeros((1, LANES), F32)
            for h in range(2):
                q4 = _stack_pairs(z_ref, h, rows)
                do4 = jnp.concatenate([dattn[:, (4 * h + j) * LANES:(4 * h + j + 1) * LANES] for j in range(4)],
                                      axis=0)
                dq4 = jnp.zeros((4 * BLOCK, LANES), F32)
                for e in range(2):
                    s = lax.dot_general(q4, k_eff[h][e], (((1,), (1,)), ((), ())), preferred_element_type=F32)
                    p, psink = _softmax_sink(s, valid, _sink_column(sinks_ref, h, e))
                    dp = lax.dot_general(do4, v_eff[h][e], (((1,), (1,)), ((), ())), preferred_element_type=F32)
                    delta = jnp.sum(p * dp, axis=-1, keepdims=True)
                    ds = (p * (dp - delta)).astype(BF16)
                    dq4 = dq4 + jnp.dot(ds, k_eff[h][e], preferred_element_type=F32)
                    dk = lax.dot_general(q4, ds, (((0,), (0,)), ((), ())), preferred_element_type=F32).T
                    dv = lax.dot_general(do4, p.astype(BF16), (((0,), (0,)), ((), ())), preferred_element_type=F32).T
                    keep = lane_lo if e == 0 else jnp.logical_not(lane_lo)
                    slot = 0 if e == h else 1
                    dk_acc[slot] = dk_acc[slot] + jnp.where(keep, dk, 0.0)
                    dv_acc[slot] = dv_acc[slot] + jnp.where(keep, dv, 0.0)
                    dsk = -(psink * delta)
                    for j in range(4):
                        tot = jnp.sum(jnp.where(rowblk == j, dsk, 0.0), axis=0, keepdims=True)
                        dsink = dsink + jnp.where(sink_lane == GROUP * h + 2 * j + e, tot, 0.0)
                for j in range(4):
                    cols = slice((4 * h + j) * LANES, (4 * h + j + 1) * LANES)
                    dqj = dq4[j * BLOCK:(j + 1) * BLOCK]
                    dz_ref[rows, cols] = dqj.astype(BF16)
                    db_ref[0:1, cols] += jnp.sum(dqj, axis=0, keepdims=True)
            dsk_ref[0:1, :] += dsink
            dkv_new = jnp.concatenate([(dk_acc[0] + pltpu.roll(dk_acc[1], HEAD_DIM, 1)) * scale,
                                       dv_acc[0] + pltpu.roll(dv_acc[1], HEAD_DIM, 1)], axis=1)
            done = pending + dkv_new[:BLOCK]
            dkv_ref[rows, :] = done.astype(BF16)
            dbkv_ref[0:1, :] += jnp.sum(done, axis=0, keepdims=True)

            cb = z_ref[rows, seg["cb"]].astype(F32)
            cc = z_ref[rows, seg["cc"]].astype(F32)
            cx = z_ref[rows, seg["cx"]].astype(F32)
            sg = jax.nn.sigmoid(z_ref[rows, seg["gc"]].astype(F32))
            p_in = cc * cx
            if b == 0:
                prev = jnp.where(n > 0, prev_ref[:, seg["cc"]].astype(F32) * prev_ref[:, seg["cx"]].astype(F32), 0.0)
            else:
                tail = slice(b * BLOCK - SUBLANES_BF16, b * BLOCK)
                prev = z_ref[tail, seg["cc"]].astype(F32) * z_ref[tail, seg["cx"]].astype(F32)
            p_m1 = _shift_down(p_in, 1, prev)
            p_m2 = _shift_down(p_in, 2, prev)
            w0, w1, w2 = cw_ref[0:1, :], cw_ref[1:2, :], cw_ref[2:3, :]
            cconv = w0 * p_m2 + w1 * p_m1 + w2 * p_in
            dconv = dm * sg
            dgc = dm * (cb * cconv) * sg * (1.0 - sg)
            dcb = dconv * cconv
            dcc_t = dconv * cb
            if b == BWD_BLOCKS - 1:
                nxt = jnp.where(n < steps - 1,
                                dmn_ref[...].astype(F32) * jax.nn.sigmoid(next_ref[:, seg["gc"]].astype(F32))
                                * next_ref[:, seg["cb"]].astype(F32), 0.0)
            else:
                head = slice((b + 1) * BLOCK, (b + 1) * BLOCK + SUBLANES_BF16)
                nxt = (dm_ref[head, :].astype(F32) * jax.nn.sigmoid(z_ref[head, seg["gc"]].astype(F32))
                       * z_ref[head, seg["cb"]].astype(F32))
            dpin = w2 * dcc_t + w1 * _shift_up(dcc_t, 1, nxt) + w0 * _shift_up(dcc_t, 2, nxt)
            for nm, val in (("cb", dcb), ("cc", dpin * cx), ("cx", dpin * cc), ("gc", dgc)):
                dz_ref[rows, seg[nm]] = val.astype(BF16)
                db_ref[0:1, seg[nm]] += jnp.sum(val, axis=0, keepdims=True)
            dcw_ref[0:1, :] += jnp.sum(dcc_t * p_m2, axis=0, keepdims=True)
            dcw_ref[1:2, :] += jnp.sum(dcc_t * p_m1, axis=0, keepdims=True)
            dcw_ref[2:3, :] += jnp.sum(dcc_t * p_in, axis=0, keepdims=True)
            return dkv_new[BLOCK:]

        @pl.when(n < steps)
        def _():
            pending = carry_ref[...]
            for b in range(BWD_BLOCKS):
                pending = one_block(b, pending)
            carry_ref[...] = pending

        @pl.when(n == steps)
        def _():
            done = carry_ref[...]
            dkv_ref[:BLOCK, :] = done.astype(BF16)
            dkv_ref[BLOCK:, :] = jnp.zeros((tq - BLOCK, kvw2), BF16)
            dbkv_ref[0:1, :] += jnp.sum(done, axis=0, keepdims=True)

    def cur(n):
        return jnp.minimum(n, steps - 1)

    def after(n):
        return jnp.minimum((cur(n) + 1) * halo, last_halo)

    blk = pl.BlockSpec((tq, d), lambda n: (cur(n), 0))
    return _pallas(
        body, name="mixer_bwd", grid=(steps + 1,), ride=ride, sem=("arbitrary",),
        args=(z, z, z, z, dmerged, dmerged, attn, sinks, conv_w),
        in_specs=[pl.BlockSpec((tq, zw), lambda n: (cur(n), 0)),
                  pl.BlockSpec((BLOCK, kvw2), lambda n: (jnp.maximum(cur(n) * BWD_BLOCKS - 1, 0), d // kvw2)),
                  pl.BlockSpec((SUBLANES_BF16, zw), lambda n: (jnp.maximum(cur(n) * halo - 1, 0), 0)),
                  pl.BlockSpec((SUBLANES_BF16, zw), lambda n: (after(n), 0)),
                  blk,
                  pl.BlockSpec((SUBLANES_BF16, d), lambda n: (after(n), 0)),
                  blk, SMEM_SPEC, pl.BlockSpec((3, d), lambda n: (0, 0))],
        out_specs=[pl.BlockSpec((tq, zw), lambda n: (cur(n), 0)),
                   pl.BlockSpec((tq, kvw2), lambda n: (n, 0)),
                   pl.BlockSpec((8, zw), lambda n: (0, 0)), pl.BlockSpec((8, kvw2), lambda n: (0, 0)),
                   pl.BlockSpec((8, d), lambda n: (0, 0)), pl.BlockSpec((8, LANES), lambda n: (0, 0))],
        out_shape=[SDS((t, zw), BF16), SDS((t + tq, kvw2), BF16), SDS((8, zw), F32), SDS((8, kvw2), F32),
                   SDS((8, d), F32), SDS((8, LANES), F32)],
        scratch=[pltpu.VMEM((BLOCK, kvw2), F32)])


def _wgrad_in(dz, dkv, h1, tk, ride=None):
    t, zw = dz.shape
    d = h1.shape[1]
    kvw2 = dkv.shape[1]
    blk = d + kvw2
    assert zw % blk == 0
    tn = (((0,), (0,)), ((), ()))

    def body(a_ref, akv_ref, h_ref, o_ref, o16_ref):
        n, k = pl.program_id(0), pl.program_id(1)

        @pl.when(k == 0)
        def _():
            o_ref[...] = jnp.zeros_like(o_ref)

        @pl.when(n == 0)
        def _():
            o_ref[:d, :] += lax.dot_general(a_ref[:, :d], h_ref[...], tn, preferred_element_type=F32)
            o_ref[d:, :] += lax.dot_general(akv_ref[...], h_ref[...], tn, preferred_element_type=F32)

        @pl.when(n > 0)
        def _():
            o_ref[...] += lax.dot_general(a_ref[...], h_ref[...], tn, preferred_element_type=F32)

        @pl.when(k == t // tk - 1)
        def _():
            o16_ref[...] = o_ref[...].astype(BF16)

    out_spec = pl.BlockSpec((blk, d), lambda n, k: (n, 0))
    return _pallas(
        body, name="wgrad_in", grid=(zw // blk, t // tk),
        in_specs=[pl.BlockSpec((tk, blk), lambda n, k: (k, n)), pl.BlockSpec((tk, kvw2), lambda n, k: (k, 0)),
                  pl.BlockSpec((tk, d), lambda n, k: (k, 0))],
        out_specs=[out_spec, out_spec], out_shape=[SDS((zw, d), F32), SDS((zw, d), BF16)],
        args=(dz, dkv, h1), sem=("parallel", "arbitrary"), ride=ride)


def _in_proj_bwd(dz, dkv, wt, x, dx1, g_mix, sc1, tm, ride=None):
    t, d = x.shape
    zw = dz.shape[1]
    kvw2 = dkv.shape[1]
    rest = d + kvw2

    def body(a_ref, akv_ref, w_ref, x_ref, dx1_ref, g_ref, sc_ref, gx_ref, st_ref):
        @pl.when(pl.program_id(0) == 0)
        def _():
            st_ref[...] = jnp.zeros_like(st_ref)

        dh = (jnp.dot(a_ref[:, :d], w_ref[:d, :], preferred_element_type=F32)
              + jnp.dot(akv_ref[...], w_ref[d:rest, :], preferred_element_type=F32)
              + jnp.dot(a_ref[:, rest:], w_ref[rest:, :], preferred_element_type=F32))
        xx = x_ref[...]
        r = _rms(xx)
        xn = xx * r
        g = g_ref[...]
        dn = dh * (1.0 + sc_ref[...])
        u = dn * g
        gx_ref[...] = dx1_ref[...] + r * (u - xn * jnp.mean(u * xn, axis=-1, keepdims=True))
        st_ref[0:1, :] += jnp.sum(dh, axis=0, keepdims=True)
        st_ref[1:2, :] += jnp.sum(dh * (xn * g), axis=0, keepdims=True)
        st_ref[2:3, :] += jnp.sum(dn * xn, axis=0, keepdims=True)

    row = pl.BlockSpec((tm, d), lambda i: (i, 0))
    vecs, vec_specs = zip(*[_vec(v, d) for v in (g_mix, sc1)])
    return _pallas(
        body, name="in_proj_bwd", grid=(t // tm,),
        in_specs=[pl.BlockSpec((tm, zw), lambda i: (i, 0)), pl.BlockSpec((tm, kvw2), lambda i: (i, 0)),
                  pl.BlockSpec((zw, d), lambda i: (0, 0), pipeline_mode=pl.Buffered(1)),
                  row, row, *vec_specs],
        out_specs=[row, pl.BlockSpec((8, d), lambda i: (0, 0))],
        out_shape=[SDS((t, d), F32), SDS((8, d), F32)],
        args=(dz, dkv, wt, x, dx1, *vecs), sem=("arbitrary",), ride=ride)


def _to_lanes(v, rows=None):
    flat = v.reshape(-1)
    need = -(-flat.shape[0] // LANES)
    need = -(-need // 8) * 8 if rows is None else rows
    return jnp.pad(flat, (0, need * LANES - flat.shape[0])).reshape(need, LANES)


def kernel(x, c, w_ada, b_ada, g_mix, w_in, b_in, sinks, conv_w, w_out, g_ffn, w_ffn_in, w_ffn_out, g_final, loss_target, m_w_ada, m_b_ada, m_g_mix, m_w_in, m_b_in, m_sinks, m_conv_w, m_w_out, m_g_ffn, m_w_ffn_in, m_w_ffn_out, m_g_final, v_w_ada, v_b_ada, v_g_mix, v_w_in, v_b_in, v_sinks, v_conv_w, v_w_out, v_g_ffn, v_w_ffn_in, v_w_ffn_out, v_g_final):
    xs, tgt = x[0], loss_target[0]
    t, d = xs.shape
    zw = w_in.shape[2] * N_CHIP
    kvw2 = zw - 6 * d
    ff = w_ffn_out.shape[1] * N_CHIP
    n_mod = w_ada.shape[2] * N_CHIP // d
    mod_sh = w_ada.shape[2]
    cw_sh = conv_w.shape[2]
    assert d % (8 * LANES) == 0 and kvw2 == 2 * LANES and t % 512 == 0 and n_mod == 6
    xi, yi, ci = _mesh_pos()
    j_me = 2 * xi + yi
    pos = jnp.stack([ci, j_me]).astype(jnp.int32)
    tm = 512

    w_in_t, m_w_in_t, v_w_in_t = w_in[0].T, m_w_in[0].T, v_w_in[0].T
    assert d == 8 * LANES
    pack1 = jnp.concatenate([c.reshape(d // LANES, LANES), conv_w[0].reshape(-1, LANES)], axis=0)
    pack1 = jnp.pad(pack1, ((0, 16 - pack1.shape[0]), (0, 0)))
    b_ada_sh = lax.dynamic_slice(b_ada, (0, j_me * mod_sh), (1, mod_sh))
    g1, mod, w_in_g, later = _startup(pack1, w_ada[0], b_ada_sh, _cast_into_block(pos, w_in_t, "cast_w_in"),
                                      [w_out[0], w_ffn_in[0], w_ffn_out[0]])
    c_all = g1[:, :d // LANES, :].reshape(N_DEV, d)
    cw_rows = 3 * cw_sh // LANES
    conv_w_full = jnp.concatenate(
        [g1[2 * j, d // LANES:d // LANES + cw_rows, :].reshape(3, cw_sh) for j in range(N_CHIP)], axis=1)
    sh1, sc1, ga1, sh2, sc2, ga2 = [(mod, k) for k in range(6)]
    w_in_tf = w_in_g.reshape(zw, d)

    (z, h1), later = _in_proj(xs, g_mix, sc1, sh1, w_in_tf, b_in, min(t, 2048), zw // 5, ride=_x_gather_ici(later))
    (attn, merged), later = _mixer_fwd(z, sinks, conv_w_full, d, ride=_x_gather_d2d(later))
    w_out_f = later[0].reshape(d, d)
    w_ffn_in_f = later[1]
    w_ffn_out_f = later[2].reshape(ff, d)
    tml = min(t, 1024)
    y1, x1, h2 = _out_proj_fwd(merged, w_out_f, xs, ga1, g_ffn, sc2, sh2, tml)
    gu, act = _ffn_in_fwd(h2, w_ffn_in_f, ff, tml, ff // 2)
    dx2, dy2, st_loss = _ffn_out_loss(act, w_ffn_out_f, x1, tgt, ga2, g_final.reshape(1, d), tml)

    dgu = _ffn_out_bwd(dy2, w_ffn_out_f, gu, tml, ff // 2)
    tk = min(t, 2048)
    dw_ffn_out, _ = _wgrad(
        act, dy2, pl.BlockSpec((tk, ff // 2), lambda m, k: (k, m)), pl.BlockSpec((tk, d), lambda m, k: (k, 0)),
        pl.BlockSpec((ff // 2, d), lambda m, k: (m, 0)), SDS((ff, d), F32), (2, t // tk), "wgrad_ffn_out")
    dx1, dy1, st_ffn = _ffn_in_bwd(dgu, w_ffn_in_f, x1, dx2, y1, g_ffn, sc2, ga1, tm)
    dw_ffn_in, _ = _wgrad(
        h2, dgu, pl.BlockSpec((tk, d), lambda n, k: (k, 0)),
        pl.BlockSpec((None, tk, ff // 2), lambda n, k: (n // 2, k, n % 2)),
        pl.BlockSpec((None, d, ff // 2), lambda n, k: (n, 0, 0)), SDS((N_CHIP, d, ff // 2), F32),
        (N_CHIP, t // tk), "wgrad_ffn_in")
    dw_out, _ = _wgrad(
        merged, dy1, pl.BlockSpec((tk, d), lambda m, k: (k, 0)), pl.BlockSpec((tk, d), lambda m, k: (k, 0)),
        pl.BlockSpec((d, d), lambda m, k: (0, 0)), SDS((d, d), F32), (1, t // tk), "wgrad_out")

    early = [[g.reshape(N_CHIP, -1, g.shape[-1]) for g in pair] for pair in (dw_out, dw_ffn_in, dw_ffn_out)]
    early_names = ["w_out", "w_ffn_in", "w_ffn_out"]
    dmerged, _ = _out_proj_bwd(dy1, w_out_f, tml)
    (dz, dkv_shifted, db_z, db_kv, dcw, dsk), terms = _mixer_bwd(
        z, dmerged, attn, sinks, conv_w_full, d, ride=_x_reduce([e[0] for e in early], [e[1] for e in early]))
    dkv = dkv_shifted[BLOCK:BLOCK + t]
    fulls = [_sum_terms(pos, e[0], s, r, "sum_terms_" + nm)
             for e, s, r, nm in zip(early, terms[:3], terms[3:], early_names)]
    dw_in_t, (g_w_out, g_w_ffn_in, g_w_ffn_out) = _wgrad_in(dz, dkv, h1, tk, ride=_x_pair_exchange(fulls))
    dw_in_t = [g.reshape(N_CHIP, zw // N_CHIP, d) for g in dw_in_t]

    (grad_x, st_in), (from_sib, from_far) = _in_proj_bwd(dz, dkv, w_in_tf, xs, dx1, g_mix, sc1, tm,
                                                         ride=_x_reduce([dw_in_t[0]], [dw_in_t[1]]))
    g_w_in_half = _sum_terms(pos, dw_in_t[0], from_sib, from_far, "sum_terms_w_in")

    dmod = jnp.concatenate([st_in[0:1], st_in[1:2], st_ffn[3:4], st_ffn[0:1], st_ffn[1:2], st_loss[0:1]], axis=1)
    db_in = jnp.concatenate([db_z[0:1, :d], db_kv[0:1], db_z[0:1, d + kvw2:]], axis=1)
    seg = [dmod, st_in[2:3], db_in, dsk[0:1], dcw[0:3].reshape(1, 3 * d), st_ffn[2:3], st_loss[1:2],
           st_loss[2:3, :LANES]]
    sizes = [s.shape[1] for s in seg]
    pack2 = _to_lanes(jnp.concatenate(seg, axis=1))
    packs, g_w_in_t = _tail_exchange(pack2, g_w_in_half)
    tot = _pack_sum(packs).reshape(-1)
    offs = [sum(sizes[:k]) for k in range(len(sizes))]
    gb_ada, gg_mix, gb_in, gsinks, gcw, gg_ffn, gg_final, loss_v = [tot[o:o + s] for o, s in zip(offs, sizes)]
    loss = loss_v[0]
    gsinks = gsinks[:sinks.shape[1]]
    gcw_sh = lax.dynamic_slice(gcw.reshape(3, d), (0, j_me * cw_sh), (3, cw_sh))

    dmod_all = packs[:, :n_mod * d // LANES, :].reshape(N_DEV, n_mod * d)
    g_w_ada = _ada_wgrad(c_all, lax.dynamic_slice(dmod_all, (0, j_me * mod_sh), (N_DEV, mod_sh)))

    out_g, out_d, out_m, out_v = {}, {}, {}, {}
    big = {"w_ada": (w_ada[0], g_w_ada, m_w_ada[0], v_w_ada[0]),
           "w_out": (w_out[0], g_w_out, m_w_out[0], v_w_out[0]),
           "w_ffn_in": (w_ffn_in[0], g_w_ffn_in, m_w_ffn_in[0], v_w_ffn_in[0]),
           "w_ffn_out": (w_ffn_out[0], g_w_ffn_out, m_w_ffn_out[0], v_w_ffn_out[0])}
    for nm, (w, g, m, v) in big.items():
        out_g[nm], out_d[nm], out_m[nm], out_v[nm] = [o[None] for o in _adamw(w, g, m, v, "adamw_" + nm)]
    out_g["w_in"], out_d["w_in"], out_m["w_in"], out_v["w_in"] = [
        o.T[None] for o in _adamw(w_in_t, g_w_in_t, m_w_in_t, v_w_in_t, "adamw_w_in")]
    small = {"b_ada": (b_ada, gb_ada, m_b_ada, v_b_ada), "g_mix": (g_mix, gg_mix, m_g_mix, v_g_mix),
             "b_in": (b_in, gb_in, m_b_in, v_b_in), "sinks": (sinks, gsinks, m_sinks, v_sinks),
             "conv_w": (conv_w, gcw_sh, m_conv_w, v_conv_w), "g_ffn": (g_ffn, gg_ffn, m_g_ffn, v_g_ffn),
             "g_final": (g_final, gg_final, m_g_final, v_g_final)}
    def two_d(a):
        return a.reshape(-1, a.shape[-1])

    s_out = _adamw_small([tuple(two_d(a.reshape(w.shape)) for a in (w, g, m, v)) for w, g, m, v in small.values()])
    for (nm, (w, g, _, _)), res in zip(small.items(), s_out):
        out_g[nm] = g.reshape(w.shape)
        out_d[nm], out_m[nm], out_v[nm] = [o.reshape(w.shape) for o in res]

    order = ["w_ada", "b_ada", "g_mix", "w_in", "b_in", "sinks", "conv_w", "w_out", "g_ffn", "w_ffn_in", "w_ffn_out",
             "g_final"]
    return (loss, grad_x[None], *[out_g[k] for k in order], *[out_d[k] for k in order],
            *[out_m[k] for k in order], *[out_v[k] for k in order])
```

```python
import functools

import jax
import jax.numpy as jnp
from jax import lax
from jax.experimental import pallas as pl
from jax.experimental.pallas import tpu as pltpu

F32 = jnp.float32
BF16 = jnp.bfloat16
EPS = 1e-6
HEAD_DIM = 64
GROUP = 8
BLOCK = 128
LANES = 128
SUBLANES_BF16 = 16
N_DEV = 8
N_CHIP = 4
VMEM_LIMIT = 56 * 1024 * 1024
MESH = pl.DeviceIdType.MESH

ADAM_LR = 0.001
ADAM_B1 = 0.9
ADAM_B2 = 0.999
ADAM_EPS = 1e-08
ADAM_WD = 0.01
ADAM_STEP = 10

SDS = jax.ShapeDtypeStruct
ANY = pl.BlockSpec(memory_space=pl.ANY)
VMEM_SPEC = pl.BlockSpec(memory_space=pltpu.VMEM)
SMEM_SPEC = pl.BlockSpec(memory_space=pltpu.SMEM)


def _params(*sem):
    return pltpu.CompilerParams(dimension_semantics=sem, vmem_limit_bytes=VMEM_LIMIT)


def _vec(v, d):
    arr, k = v if isinstance(v, tuple) else (v, 0)
    return arr, pl.BlockSpec((1, d), lambda *_: (0, k))


def _mesh_pos():
    return lax.axis_index("x"), lax.axis_index("y"), lax.axis_index("c")


def _row_tile(rows, cols, itemsize=4, budget=1 << 20, mult=8):
    best = None
    for t in range(mult, rows + 1, mult):
        if rows % t == 0 and t * cols * itemsize <= budget:
            best = t
    if best is None:
        best = rows
    return best


def _gather_all(v_ref, out_ref, send_sems, recv_sems, local_sem, between=None):
    x, y, c = _mesh_pos()
    me = 4 * x + 2 * y + c
    mine = pltpu.make_async_copy(v_ref, out_ref.at[me], local_sem)
    mine.start()
    peers = []
    for k in range(1, N_DEV):
        px = 1 - x if k & 4 else x
        py = 1 - y if k & 2 else y
        pc = 1 - c if k & 1 else c
        peers.append((px, py, pc))

    def copy(k, block):
        return pltpu.make_async_remote_copy(
            src_ref=v_ref, dst_ref=out_ref.at[block], send_sem=send_sems.at[k], recv_sem=recv_sems.at[k],
            device_id=peers[k], device_id_type=MESH)

    sends = [copy(k, me) for k in range(N_DEV - 1)]
    for cp in sends:
        cp.start()
    if between is not None:
        between()
    for k, (px, py, pc) in enumerate(peers):
        copy(k, 4 * px + 2 * py + pc).wait_recv()
    for cp in sends:
        cp.wait_send()
    mine.wait()


def _small_sems():
    return [pltpu.SemaphoreType.DMA((N_DEV - 1,)), pltpu.SemaphoreType.DMA((N_DEV - 1,)), pltpu.SemaphoreType.DMA]


def _tail_exchange(pack, full):
    def body(pack_ref, full_unused, packs_ref, full_ref, s1, r1, l1, send_sem, recv_sem):
        del full_unused
        x, y, c = _mesh_pos()
        half = full_ref.shape[0] // 2
        rows = pl.ds(pl.multiple_of(c * half, 8), half)
        swap = pltpu.make_async_remote_copy(
            src_ref=full_ref.at[rows], dst_ref=full_ref.at[rows], send_sem=send_sem, recv_sem=recv_sem,
            device_id=(x, y, 1 - c), device_id_type=MESH)
        swap.start()
        _gather_all(pack_ref, packs_ref, s1, r1, l1)
        swap.wait()

    return pl.pallas_call(
        body, name="tail_exchange", out_shape=[SDS((N_DEV,) + pack.shape, pack.dtype), SDS(full.shape, full.dtype)],
        in_specs=[VMEM_SPEC, ANY], out_specs=[VMEM_SPEC, ANY], input_output_aliases={1: 1},
        scratch_shapes=_small_sems() + [pltpu.SemaphoreType.DMA, pltpu.SemaphoreType.DMA])(pack, full)


def _other_chips(x, y):
    return [(1 - x, y), (x, 1 - y), (1 - x, 1 - y)]


def _startup(pack, w_ada_sh, b_ada_sh, w_in_t, later):
    d, n = w_ada_sh.shape
    kc = d // LANES
    casts = [w_in_t] + list(later)
    n_c = len(casts)
    chunk_rows = [_row_tile(a.shape[0], a.shape[1], budget=3 << 19, mult=SUBLANES_BF16) for a in casts]

    def body(*refs):
        pack_ref, wa_hbm, ba_ref = refs[:3]
        cast_src = refs[3:3 + n_c]
        packs_ref, mine_ref = refs[3 + n_c:5 + n_c]
        cast_dst = refs[5 + n_c:5 + 2 * n_c]
        wa_scr, mod_scr, mod_ref = refs[5 + 2 * n_c:8 + 2 * n_c]
        f32_bufs = refs[8 + 2 * n_c:8 + 3 * n_c]
        bf16_bufs = refs[8 + 3 * n_c:8 + 4 * n_c]
        (s1, r1, l1, s2, r2, l2, send_sems, recv_sems, fsend_sems, frecv_sems, relay_send, relay_recv, wa_sem,
         cast_sems) = refs[8 + 4 * n_c:]
        w_ref = cast_dst[0]
        x, y, c = _mesh_pos()
        j_me = 2 * x + y
        chips = _other_chips(x, y)
        half = w_ref.shape[1] // 2

        def cast_block(w):
            src, dst, fbuf, bbuf = cast_src[w], cast_dst[w], f32_bufs[w], bf16_bufs[w]
            cr = fbuf.shape[0]
            for k in range(src.shape[0] // cr):
                rows = pl.ds(k * cr, cr)
                cin = pltpu.make_async_copy(src.at[rows], fbuf, cast_sems.at[0])
                cin.start()
                cin.wait()
                bbuf[...] = fbuf[...].astype(BF16)
                cout = pltpu.make_async_copy(bbuf, dst.at[j_me, rows], cast_sems.at[1])
                cout.start()
                cout.wait()

        def rows_of(which):
            return pl.ds(pl.multiple_of(which * half, SUBLANES_BF16), half)

        def copy(p, block, rows, over_ici):
            sems = (send_sems, recv_sems) if over_ici else (fsend_sems, frecv_sems)
            return pltpu.make_async_remote_copy(
                src_ref=w_ref.at[block, rows], dst_ref=w_ref.at[block, rows], send_sem=sems[0].at[p],
                recv_sem=sems[1].at[p], device_id=(*chips[p], c) if over_ici else (x, y, 1 - c), device_id_type=MESH)

        def block_of(p):
            return 2 * chips[p][0] + chips[p][1]

        def relay(q, block):
            rows = pl.ds(pl.multiple_of(c * half + q * (half // 2), SUBLANES_BF16), half // 2)
            return pltpu.make_async_remote_copy(
                src_ref=w_ref.at[block, rows], dst_ref=w_ref.at[block, rows], send_sem=relay_send.at[q],
                recv_sem=relay_recv.at[q], device_id=(*chips[1 - q], c), device_id_type=MESH)

        load_wa = pltpu.make_async_copy(wa_hbm, wa_scr, wa_sem)
        load_wa.start()
        _gather_all(pack_ref, packs_ref, s1, r1, l1, between=functools.partial(cast_block, 0))
        sends = [copy(p, j_me, rows_of(c), True) for p in range(2)]
        for cp in sends:
            cp.start()
        load_wa.wait()
        acc = jnp.zeros((N_DEV, n), F32)
        for k in range(kc):
            ck = packs_ref[:, k, :]
            sk = (ck * jax.nn.sigmoid(ck)).astype(BF16)
            acc = acc + jnp.dot(sk, wa_scr[k * LANES:(k + 1) * LANES, :].astype(BF16), preferred_element_type=F32)
        mod_scr[...] = acc + ba_ref[...]
        _gather_all(mod_scr, mod_ref, s2, r2, l2)
        for j in range(N_CHIP):
            mine_ref[:, j * n:(j + 1) * n] = mod_ref[2 * j, pl.ds(4 * x + 2 * y + c, 1), :]
        passed = []
        for q in range(2):
            copy(q, block_of(q), rows_of(c), True).wait_recv()
            for cp in (relay(q, block_of(q)), copy(q, block_of(q), rows_of(c), False)):
                cp.start()
                passed.append(cp)
        for w in range(1, n_c):
            cast_block(w)
        for q in range(2):
            relay(q, block_of(2)).wait_recv()
        fw = copy(2, block_of(2), rows_of(c), False)
        fw.start()
        for p in range(3):
            copy(p, block_of(p), rows_of(1 - c), False).wait_recv()
        for cp in sends + passed + [fw]:
            cp.wait_send()

    res = pl.pallas_call(
        body, name="startup",
        out_shape=[SDS((N_DEV,) + pack.shape, F32), SDS((1, N_CHIP * n), F32)]
        + [SDS((N_CHIP,) + a.shape, BF16) for a in casts],
        in_specs=[VMEM_SPEC, ANY, VMEM_SPEC] + [ANY] * n_c, out_specs=[VMEM_SPEC, VMEM_SPEC] + [ANY] * n_c,
        scratch_shapes=[pltpu.VMEM((d, n), F32), pltpu.VMEM((N_DEV, n), F32), pltpu.VMEM((N_DEV, N_DEV, n), F32)]
        + [pltpu.VMEM((cr, a.shape[1]), F32) for cr, a in zip(chunk_rows, casts)]
        + [pltpu.VMEM((cr, a.shape[1]), BF16) for cr, a in zip(chunk_rows, casts)]
        + _small_sems() + _small_sems()
        + [pltpu.SemaphoreType.DMA((3,))] * 4 + [pltpu.SemaphoreType.DMA((2,))] * 2 + [pltpu.SemaphoreType.DMA]
        + [pltpu.SemaphoreType.DMA((2,))],
        compiler_params=pltpu.CompilerParams(vmem_limit_bytes=VMEM_LIMIT),
    )(pack, w_ada_sh, b_ada_sh, *casts)
    return res[0], res[1], res[2], list(res[3:])


class _Exchange:
    def __init__(self, operands, out_shape, in_place, n_sems, copies):
        self.operands, self.out_shape, self.in_place, self.n_sems, self.copies = (
            list(operands), list(out_shape), in_place, n_sems, copies)

    def sems(self):
        return [pltpu.SemaphoreType.DMA((self.n_sems,)), pltpu.SemaphoreType.DMA((self.n_sems,))]


def _x_gather_ici(bufs):
    def copies(ins, outs, send_sems, recv_sems):
        x, y, c = _mesh_pos()
        chips = _other_chips(x, y)
        out = []
        for w in range(len(outs)):
            half = outs[w].shape[1] // 2
            rows = pl.ds(pl.multiple_of(c * half, SUBLANES_BF16), half)
            for p in range(3):
                out.append(pltpu.make_async_remote_copy(
                    src_ref=outs[w].at[2 * x + y, rows], dst_ref=outs[w].at[2 * x + y, rows],
                    send_sem=send_sems.at[w * 3 + p], recv_sem=recv_sems.at[w * 3 + p],
                    device_id=(*chips[p], c), device_id_type=MESH))
        return out

    return _Exchange(bufs, [SDS(b.shape, b.dtype) for b in bufs], True, 3 * len(bufs), copies)


def _x_gather_d2d(bufs):
    def copies(ins, outs, send_sems, recv_sems):
        x, y, c = _mesh_pos()
        chips = _other_chips(x, y)
        out = []
        for w in range(len(outs)):
            half = outs[w].shape[1] // 2
            rows = pl.ds(pl.multiple_of(c * half, SUBLANES_BF16), half)
            for p in range(3):
                block = 2 * chips[p][0] + chips[p][1]
                out.append(pltpu.make_async_remote_copy(
                    src_ref=outs[w].at[block, rows], dst_ref=outs[w].at[block, rows],
                    send_sem=send_sems.at[w * 3 + p], recv_sem=recv_sems.at[w * 3 + p],
                    device_id=(x, y, 1 - c), device_id_type=MESH))
        return out

    return _Exchange(bufs, [SDS(b.shape, b.dtype) for b in bufs], True, 3 * len(bufs), copies)


N_REMOTE = 6


def _x_reduce(grads32, grads16):
    n_w = len(grads32)

    def copies(ins, outs, send_sems, recv_sems):
        g32, g16 = ins[:n_w], ins[n_w:]
        from_sib, from_far = outs[:n_w], outs[n_w:]
        x, y, c = _mesh_pos()
        chips = _other_chips(x, y)
        out = []
        for w in range(n_w):
            half = g32[w].shape[1] // 2
            k0 = w * (N_REMOTE + 1)
            out.append(pltpu.make_async_remote_copy(
                src_ref=g32[w].at[2 * x + y, pl.ds(pl.multiple_of((1 - c) * half, SUBLANES_BF16), half), :],
                dst_ref=from_sib[w], send_sem=send_sems.at[k0], recv_sem=recv_sems.at[k0],
                device_id=(x, y, 1 - c), device_id_type=MESH))
            for p in range(3):
                for f in range(2):
                    tc = c if f == 0 else 1 - c
                    k = 2 * p + f
                    out.append(pltpu.make_async_remote_copy(
                        src_ref=g16[w].at[2 * chips[p][0] + chips[p][1],
                                          pl.ds(pl.multiple_of(tc * half, SUBLANES_BF16), half), :],
                        dst_ref=from_far[w].at[k], send_sem=send_sems.at[k0 + 1 + k], recv_sem=recv_sems.at[k0 + 1 + k],
                        device_id=(*chips[p], tc), device_id_type=MESH))
        return out

    shapes = ([SDS((g.shape[1] // 2, g.shape[2]), g.dtype) for g in grads32]
              + [SDS((N_REMOTE, g.shape[1] // 2, g.shape[2]), g.dtype) for g in grads16])
    return _Exchange(list(grads32) + list(grads16), shapes, False, (N_REMOTE + 1) * n_w, copies)


def _x_pair_exchange(fulls):
    def copies(ins, outs, send_sems, recv_sems):
        x, y, c = _mesh_pos()
        out = []
        for w in range(len(outs)):
            half = outs[w].shape[0] // 2
            rows = pl.ds(pl.multiple_of(c * half, 8), half)
            out.append(pltpu.make_async_remote_copy(
                src_ref=outs[w].at[rows], dst_ref=outs[w].at[rows], send_sem=send_sems.at[w],
                recv_sem=recv_sems.at[w], device_id=(x, y, 1 - c), device_id_type=MESH))
        return out

    return _Exchange(fulls, [SDS(f.shape, f.dtype) for f in fulls], True, len(fulls), copies)


def _pallas(body, *, name, grid, in_specs, out_specs, out_shape, args, scratch=(), sem=None, ride=None):
    single = not isinstance(out_specs, (list, tuple))
    out_specs_l = [out_specs] if single else list(out_specs)
    out_shape_l = [out_shape] if single else list(out_shape)
    n_in, n_out, n_scr = len(in_specs), len(out_specs_l), len(scratch)
    if ride is None:
        res = pl.pallas_call(body, name=name, grid=grid, in_specs=list(in_specs), out_specs=out_specs,
                             out_shape=out_shape, scratch_shapes=list(scratch), compiler_params=_params(*sem))(*args)
        return res, None
    n_x, n_xo = len(ride.operands), len(ride.out_shape)

    def full_body(*refs):
        ins, x_ins = refs[:n_in], refs[n_in:n_in + n_x]
        outs = refs[n_in + n_x:n_in + n_x + n_out]
        x_outs = refs[n_in + n_x + n_out:n_in + n_x + n_out + n_xo]
        rest = refs[n_in + n_x + n_out + n_xo:]
        scr, (send_sems, recv_sems) = rest[:n_scr], rest[n_scr:]
        first = functools.reduce(jnp.logical_and, [pl.program_id(a) == 0 for a in range(len(grid))])
        last = functools.reduce(jnp.logical_and, [pl.program_id(a) == grid[a] - 1 for a in range(len(grid))])

        @pl.when(first)
        def _():
            for cp in ride.copies(x_ins, x_outs, send_sems, recv_sems):
                cp.start()

        body(*ins, *outs, *scr)

        @pl.when(last)
        def _():
            for cp in ride.copies(x_ins, x_outs, send_sems, recv_sems):
                cp.wait()

    res = pl.pallas_call(
        full_body, name=name, grid=grid, in_specs=list(in_specs) + [ANY] * n_x,
        out_specs=out_specs_l + [ANY] * n_xo, out_shape=out_shape_l + ride.out_shape,
        input_output_aliases={n_in + k: n_out + k for k in range(n_x)} if ride.in_place else {},
        scratch_shapes=list(scratch) + ride.sems(),
        compiler_params=_params(*(["arbitrary"] * len(grid))))(*args, *ride.operands)
    own = res[0] if single else list(res[:n_out])
    return own, list(res[n_out:])


def _sum_terms(pos, grad, from_sib, from_far, name):
    _, rows, cols = grad.shape
    half = rows // 2
    tr = _row_tile(half, cols, mult=SUBLANES_BF16)
    nblk = half // tr

    def body(pos_ref, g_ref, s_ref, r_ref, o_ref):
        del pos_ref
        acc = g_ref[...] + s_ref[...]
        for k in range(N_REMOTE):
            acc = acc + r_ref[k].astype(F32)
        o_ref[...] = acc

    return pl.pallas_call(
        body, name=name,
        grid_spec=pltpu.PrefetchScalarGridSpec(
            num_scalar_prefetch=1, grid=(nblk,),
            in_specs=[pl.BlockSpec((None, tr, cols), lambda i, pos_ref: (pos_ref[1], pos_ref[0] * nblk + i, 0)),
                      pl.BlockSpec((tr, cols), lambda i, pos_ref: (i, 0)),
                      pl.BlockSpec((N_REMOTE, tr, cols), lambda i, pos_ref: (0, i, 0))],
            out_specs=pl.BlockSpec((tr, cols), lambda i, pos_ref: (pos_ref[0] * nblk + i, 0))),
        out_shape=SDS((rows, cols), F32),
        compiler_params=_params("parallel"),
    )(pos, grad, from_sib, from_far)


def _adamw(w, g, m, v, name):
    rows, cols = w.shape
    tr = _row_tile(rows, cols)

    def body(w_ref, g_ref, m_ref, v_ref, go_ref, d_ref, nm_ref, nv_ref):
        go_ref[...] = g_ref[...]
        _adamw_update(w_ref, g_ref, m_ref, v_ref, d_ref, nm_ref, nv_ref)

    spec = pl.BlockSpec((tr, cols), lambda i: (i, 0))
    return pl.pallas_call(body, name=name, grid=(rows // tr,), in_specs=[spec] * 4, out_specs=[spec] * 4,
                          out_shape=[SDS((rows, cols), F32)] * 4, compiler_params=_params("parallel"))(w, g, m, v)


def _adamw_update(w_ref, g_ref, m_ref, v_ref, d_ref, nm_ref, nv_ref):
    gg = g_ref[...]
    nm = ADAM_B1 * m_ref[...] + (1.0 - ADAM_B1) * gg
    nv = ADAM_B2 * v_ref[...] + (1.0 - ADAM_B2) * (gg * gg)
    m_hat = nm / (1.0 - ADAM_B1 ** ADAM_STEP)
    v_hat = nv / (1.0 - ADAM_B2 ** ADAM_STEP)
    d_ref[...] = -ADAM_LR * (m_hat / (jnp.sqrt(v_hat) + ADAM_EPS) + ADAM_WD * w_ref[...])
    nm_ref[...] = nm
    nv_ref[...] = nv


def _adamw_small(params):
    n_p = len(params)

    def body(*refs):
        ins, outs = refs[:4 * n_p], refs[4 * n_p:]
        for k in range(n_p):
            _adamw_update(*ins[4 * k:4 * k + 4], *outs[3 * k:3 * k + 3])

    flat = [a for tup in params for a in tup]
    res = pl.pallas_call(
        body, name="adamw_small", in_specs=[VMEM_SPEC] * (4 * n_p), out_specs=[VMEM_SPEC] * (3 * n_p),
        out_shape=[SDS(tup[0].shape, F32) for tup in params for _ in range(3)])(*flat)
    return [res[3 * k:3 * k + 3] for k in range(n_p)]


def _pack_sum(gathered):
    _, rows, cols = gathered.shape

    def body(g_ref, o_ref):
        acc = g_ref[0]
        for d in range(1, N_DEV):
            acc = acc + g_ref[d]
        o_ref[...] = acc

    return pl.pallas_call(body, name="pack_sum", in_specs=[VMEM_SPEC], out_specs=VMEM_SPEC,
                          out_shape=SDS((rows, cols), F32))(gathered)


def _ada_wgrad(c_all, dmod_sh):
    d = c_all.shape[1]
    n = dmod_sh.shape[1]
    tn = 512

    def body(c_ref, g_ref, o_ref):
        cc = c_ref[...]
        s = cc * jax.nn.sigmoid(cc)
        o_ref[...] = lax.dot_general(s, g_ref[...], (((0,), (0,)), ((), ())), preferred_element_type=F32,
                                     precision=lax.Precision.HIGHEST)

    return pl.pallas_call(
        body, name="ada_wgrad", grid=(n // tn,),
        in_specs=[pl.BlockSpec((N_DEV, d), lambda j: (0, 0)), pl.BlockSpec((N_DEV, tn), lambda j: (0, j))],
        out_specs=pl.BlockSpec((d, tn), lambda j: (0, j)),
        out_shape=SDS((d, n), F32), compiler_params=_params("parallel"))(c_all, dmod_sh)


def _rms(xf):
    return lax.rsqrt(jnp.mean(xf * xf, axis=-1, keepdims=True) + EPS)


def _in_proj(x, g, sc, sh, wt, b, tm, tn, ride=None):
    t, d = x.shape
    n = wt.shape[0]

    def body(x_ref, g_ref, sc_ref, sh_ref, w_ref, b_ref, z_ref, h_ref):
        @pl.when(pl.program_id(1) == 0)
        def _():
            xf = x_ref[...]
            h_ref[...] = ((xf * _rms(xf) * g_ref[...]) * (1.0 + sc_ref[...]) + sh_ref[...]).astype(BF16)

        acc = lax.dot_general(h_ref[...], w_ref[...], (((1,), (1,)), ((), ())), preferred_element_type=F32)
        z_ref[...] = (acc + b_ref[...]).astype(BF16)

    row = pl.BlockSpec((tm, d), lambda i, j: (i, 0))
    vecs, vec_specs = zip(*[_vec(v, d) for v in (g, sc, sh)])
    return _pallas(
        body, name="in_proj", grid=(t // tm, n // tn),
        in_specs=[row, *vec_specs, pl.BlockSpec((tn, d), lambda i, j: (j, 0)),
                  pl.BlockSpec((1, tn), lambda i, j: (0, j))],
        out_specs=[pl.BlockSpec((tm, tn), lambda i, j: (i, j)), row],
        out_shape=[SDS((t, n), BF16), SDS((t, d), BF16)], args=(x, *vecs, wt, b),
        sem=("parallel", "arbitrary"), ride=ride)


def _segments(d, kvw2):
    o = d + kvw2
    names = ("cb", "cc", "cx", "ga", "gc")
    seg = {nm: slice(o + k * d, o + (k + 1) * d) for k, nm in enumerate(names)}
    seg["q"], seg["kv"] = slice(0, d), slice(d, o)
    return seg


def _attn_masks():
    rows = 4 * BLOCK
    r = lax.broadcasted_iota(jnp.int32, (rows, 2 * BLOCK), 0) & (BLOCK - 1)
    col = lax.broadcasted_iota(jnp.int32, (rows, 2 * BLOCK), 1)
    return (col > r) & (col <= r + BLOCK), col


def _kv_variants(kv, n_kv_w):
    assert n_kv_w == LANES
    kb, vb = kv[:, :LANES] * (HEAD_DIM ** -0.5), kv[:, LANES:]
    kr, vr = pltpu.roll(kb, HEAD_DIM, 1), pltpu.roll(vb, HEAD_DIM, 1)
    lane = lax.broadcasted_iota(jnp.int32, kb.shape, 1)
    lo = lane < HEAD_DIM
    zero = jnp.zeros_like(kb)
    k_eff = [[None, None], [None, None]]
    v_eff = [[None, None], [None, None]]
    for h in range(2):
        for e in range(2):
            ksrc, vsrc = (kb, vb) if e == h else (kr, vr)
            keep = lo if e == 0 else jnp.logical_not(lo)
            k_eff[h][e] = jnp.where(keep, ksrc, zero)
            v_eff[h][e] = jnp.where(keep, vsrc, zero)
    return k_eff, v_eff


def _sink_column(sinks_ref, h, e):
    rowblk = lax.broadcasted_iota(jnp.int32, (4 * BLOCK, 1), 0) // BLOCK
    col = jnp.zeros((4 * BLOCK, 1), F32)
    for j in range(4):
        col = jnp.where(rowblk == j, sinks_ref[0, GROUP * h + 2 * j + e], col)
    return col


def _softmax_sink(s, valid, sink):
    s = jnp.where(valid, s, -jnp.inf)
    m = jnp.maximum(jnp.max(s, axis=-1, keepdims=True), sink)
    p = jnp.exp(s - m)
    psink = jnp.exp(sink - m)
    den = jnp.sum(p, axis=-1, keepdims=True) + psink
    inv = 1.0 / den
    return p * inv, psink * inv


def _shift_down(a, s, prev):
    rows = a.shape[0]
    out = pltpu.roll(a, s, 0)
    row = lax.broadcasted_iota(jnp.int32, a.shape, 0)
    for t in range(s):
        out = jnp.where(row == t, prev[SUBLANES_BF16 - s + t:SUBLANES_BF16 - s + t + 1, :], out)
    del rows
    return out


def _shift_up(a, s, nxt):
    rows = a.shape[0]
    out = pltpu.roll(a, rows - s, 0)
    row = lax.broadcasted_iota(jnp.int32, a.shape, 0)
    for t in range(s):
        out = jnp.where(row == rows - s + t, nxt[t:t + 1, :], out)
    return out


def _stack_pairs(ref, h, rows=slice(None)):
    return jnp.concatenate([ref[rows, (4 * h + j) * LANES:(4 * h + j + 1) * LANES] for j in range(4)], axis=0)


FWD_BLOCKS = 4


def _mixer_fwd(z, sinks, conv_w, d, ride=None):
    t, zw = z.shape
    kvw2 = zw - 6 * d
    tq = FWD_BLOCKS * BLOCK
    halo = tq // SUBLANES_BF16
    seg = _segments(d, kvw2)

    def body(z_ref, kvp_ref, prev_ref, sinks_ref, cw_ref, attn_ref, merged_ref):
        n = pl.program_id(0)
        band, col = _attn_masks()
        for b in range(FWD_BLOCKS):
            rows = slice(b * BLOCK, (b + 1) * BLOCK)
            before = slice((b - 1) * BLOCK, b * BLOCK)
            kv_prev = kvp_ref[...] if b == 0 else z_ref[before, seg["kv"]]
            kv = jnp.concatenate([kv_prev, z_ref[rows, seg["kv"]]], axis=0)
            k_eff, v_eff = _kv_variants(kv, kvw2 // 2)
            valid = band & ((n > 0) | (col >= BLOCK)) if b == 0 else band
            for h in range(2):
                q4 = _stack_pairs(z_ref, h, rows)
                o4 = jnp.zeros((4 * BLOCK, LANES), F32)
                for e in range(2):
                    s = lax.dot_general(q4, k_eff[h][e], (((1,), (1,)), ((), ())), preferred_element_type=F32)
                    p, _ = _softmax_sink(s, valid, _sink_column(sinks_ref, h, e))
                    o4 = o4 + jnp.dot(p.astype(BF16), v_eff[h][e], preferred_element_type=F32)
                for j in range(4):
                    attn_ref[rows, (4 * h + j) * LANES:(4 * h + j + 1) * LANES] = (
                        o4[j * BLOCK:(j + 1) * BLOCK].astype(BF16))
            cb = z_ref[rows, seg["cb"]].astype(F32)
            p_in = z_ref[rows, seg["cc"]].astype(F32) * z_ref[rows, seg["cx"]].astype(F32)
            if b == 0:
                prev = jnp.where(n > 0, prev_ref[:, seg["cc"]].astype(F32) * prev_ref[:, seg["cx"]].astype(F32), 0.0)
            else:
                tail = slice(b * BLOCK - SUBLANES_BF16, b * BLOCK)
                prev = z_ref[tail, seg["cc"]].astype(F32) * z_ref[tail, seg["cx"]].astype(F32)
            cconv = (cw_ref[0:1, :] * _shift_down(p_in, 2, prev) + cw_ref[1:2, :] * _shift_down(p_in, 1, prev)
                     + cw_ref[2:3, :] * p_in)
            sa = jax.nn.sigmoid(z_ref[rows, seg["ga"]].astype(F32))
            sg = jax.nn.sigmoid(z_ref[rows, seg["gc"]].astype(F32))
            merged_ref[rows, :] = (sa * attn_ref[rows, :].astype(F32) + sg * (cb * cconv)).astype(BF16)

    blk = pl.BlockSpec((tq, d), lambda n: (n, 0))
    return _pallas(
        body, name="mixer_fwd", grid=(t // tq,),
        in_specs=[pl.BlockSpec((tq, zw), lambda n: (n, 0)),
                  pl.BlockSpec((BLOCK, kvw2), lambda n: (jnp.maximum(n * FWD_BLOCKS - 1, 0), d // kvw2)),
                  pl.BlockSpec((SUBLANES_BF16, zw), lambda n: (jnp.maximum(n * halo - 1, 0), 0)),
                  SMEM_SPEC, pl.BlockSpec((3, d), lambda n: (0, 0))],
        out_specs=[blk, blk],
        out_shape=[SDS((t, d), BF16), SDS((t, d), BF16)],
        args=(z, z, z, sinks, conv_w), sem=("parallel",), ride=ride)


def _out_proj_fwd(merged, w_out, x, ga1, g_ffn, sc2, sh2, tm):
    t, d = x.shape

    def body(m_ref, w_ref, x_ref, ga_ref, g_ref, sc_ref, sh_ref, y_ref, x1_ref, h_ref):
        y = jnp.dot(m_ref[...], w_ref[...], preferred_element_type=F32)
        x1 = x_ref[...] + ga_ref[...] * y
        y_ref[...] = y.astype(BF16)
        x1_ref[...] = x1
        h_ref[...] = ((x1 * _rms(x1) * g_ref[...]) * (1.0 + sc_ref[...]) + sh_ref[...]).astype(BF16)

    row = pl.BlockSpec((tm, d), lambda i: (i, 0))
    vecs, vec_specs = zip(*[_vec(v, d) for v in (ga1, g_ffn, sc2, sh2)])
    return pl.pallas_call(
        body, name="out_proj_fwd", grid=(t // tm,),
        in_specs=[row, pl.BlockSpec((d, d), lambda i: (0, 0)), row, *vec_specs],
        out_specs=[row, row, row],
        out_shape=[SDS((t, d), BF16), SDS((t, d), F32), SDS((t, d), BF16)],
        compiler_params=_params("parallel"))(merged, w_out, x, *vecs)


def _ffn_in_fwd(h2, w, ff, tm, tn):
    t, d = h2.shape
    nj = ff // tn
    assert w.shape == (2 * nj, d, tn)

    def body(h_ref, wg_ref, wu_ref, gu_ref, act_ref):
        hh = h_ref[...]
        g = jnp.dot(hh, wg_ref[...], preferred_element_type=F32)
        u = jnp.dot(hh, wu_ref[...], preferred_element_type=F32)
        sg = jax.nn.sigmoid(g)
        silu = g * sg
        gu_ref[0] = (u * (sg + silu * (1.0 - sg))).astype(BF16)
        gu_ref[1] = silu.astype(BF16)
        act_ref[...] = (silu * u).astype(BF16)

    return pl.pallas_call(
        body, name="ffn_in_fwd", grid=(nj, t // tm),
        in_specs=[pl.BlockSpec((tm, d), lambda j, i: (i, 0)), pl.BlockSpec((None, d, tn), lambda j, i: (j, 0, 0)),
                  pl.BlockSpec((None, d, tn), lambda j, i: (j + nj, 0, 0))],
        out_specs=[pl.BlockSpec((2, tm, tn), lambda j, i: (0, i, j)), pl.BlockSpec((tm, tn), lambda j, i: (i, j))],
        out_shape=[SDS((2, t, ff), BF16), SDS((t, ff), BF16)],
        compiler_params=_params("parallel", "parallel"))(h2, w, w)


def _ffn_out_loss(act, w, x1, target, ga2, g_final, tm):
    t, d = x1.shape
    ff = act.shape[1]

    def body(a_ref, w_ref, x1_ref, tg_ref, ga_ref, gf_ref, dx2_ref, dy2_ref, st_ref):
        @pl.when(pl.program_id(0) == 0)
        def _():
            st_ref[...] = jnp.zeros_like(st_ref)

        halves = [slice(k * (tm // 2), (k + 1) * (tm // 2)) for k in range(2)]
        y2s = [jnp.dot(a_ref[rows, :], w_ref[...], preferred_element_type=F32) for rows in halves]
        for rows, y2 in zip(halves, y2s):
            x2 = x1_ref[rows, :] + ga_ref[...] * y2
            r = _rms(x2)
            yn = x2 * r
            err = yn * gf_ref[...] - tg_ref[rows, :]
            loss = 0.5 * jnp.sum(jnp.mean(err * err, axis=-1, keepdims=True), axis=0, keepdims=True)
            dy = err * (1.0 / d)
            u = dy * gf_ref[...]
            dx2 = r * (u - yn * jnp.mean(u * yn, axis=-1, keepdims=True))
            dx2_ref[rows, :] = dx2
            dy2_ref[rows, :] = (ga_ref[...] * dx2).astype(BF16)
            st_ref[0:1, :] += jnp.sum(dx2 * y2, axis=0, keepdims=True)
            st_ref[1:2, :] += jnp.sum(dy * yn, axis=0, keepdims=True)
            st_ref[2:3, :] += jnp.broadcast_to(loss, (1, d))

    row = pl.BlockSpec((tm, d), lambda i: (i, 0))
    vecs, vec_specs = zip(*[_vec(v, d) for v in (ga2, g_final)])
    return pl.pallas_call(
        body, name="ffn_out_loss", grid=(t // tm,),
        in_specs=[pl.BlockSpec((tm, ff), lambda i: (i, 0)),
                  pl.BlockSpec((ff, d), lambda i: (0, 0), pipeline_mode=pl.Buffered(1)), row, row, *vec_specs],
        out_specs=[row, row, pl.BlockSpec((8, d), lambda i: (0, 0))],
        out_shape=[SDS((t, d), F32), SDS((t, d), BF16), SDS((8, d), F32)],
        compiler_params=_params("arbitrary"))(act, w, x1, target, *vecs)


def _ffn_out_bwd(dy2, w, gu, tm, tn):
    t, d = dy2.shape
    ff = w.shape[0]

    def body(dy_ref, w_ref, gu_ref, o_ref):
        dy = dy_ref[...]
        for lo in range(0, tn, 3 * LANES):
            cols = slice(lo, min(lo + 3 * LANES, tn))
            dact = lax.dot_general(dy, w_ref[cols, :], (((1,), (1,)), ((), ())), preferred_element_type=F32)
            o_ref[0, :, cols] = (dact * gu_ref[0, :, cols].astype(F32)).astype(BF16)
            o_ref[1, :, cols] = (dact * gu_ref[1, :, cols].astype(F32)).astype(BF16)

    gu_spec = pl.BlockSpec((2, tm, tn), lambda j, i: (0, i, j))
    return pl.pallas_call(
        body, name="ffn_out_bwd", grid=(ff // tn, t // tm),
        in_specs=[pl.BlockSpec((tm, d), lambda j, i: (i, 0)), pl.BlockSpec((tn, d), lambda j, i: (j, 0)), gu_spec],
        out_specs=gu_spec, out_shape=SDS((2, t, ff), BF16),
        compiler_params=_params("parallel", "parallel"))(dy2, w, gu)


def _wgrad(a, b, a_spec, b_spec, out_spec, out_shape, grid, name, ride=None):
    def body(a_ref, b_ref, o_ref, o16_ref):
        k = pl.program_id(len(grid) - 1)

        @pl.when(k == 0)
        def _():
            o_ref[...] = jnp.zeros_like(o_ref)

        o_ref[...] += lax.dot_general(a_ref[...], b_ref[...], (((0,), (0,)), ((), ())), preferred_element_type=F32)

        @pl.when(k == grid[-1] - 1)
        def _():
            o16_ref[...] = o_ref[...].astype(BF16)

    return _pallas(
        body, name=name, grid=grid, in_specs=[a_spec, b_spec], out_specs=[out_spec, out_spec],
        out_shape=[out_shape, SDS(out_shape.shape, BF16)], args=(a, b),
        sem=["parallel"] * (len(grid) - 1) + ["arbitrary"], ride=ride)


def _ffn_in_bwd(dgu, w, x1, dx2, y1, g_ffn, sc2, ga1, tm):
    t, d = x1.shape
    ff = dgu.shape[2]
    n_sh, _, sw = w.shape
    per = ff // sw
    nt = (((1,), (1,)), ((), ()))

    def body(a_ref, w_ref, x1_ref, dx2_ref, y1_ref, g_ref, sc_ref, ga_ref, dx1_ref, dy1_ref, st_ref):
        @pl.when(pl.program_id(0) == 0)
        def _():
            st_ref[...] = jnp.zeros_like(st_ref)

        dh = None
        for j in range(n_sh):
            part = lax.dot_general(a_ref[j // per, :, (j % per) * sw:(j % per + 1) * sw], w_ref[j], nt,
                                   preferred_element_type=F32)
            dh = part if dh is None else dh + part
        x1 = x1_ref[...]
        r = _rms(x1)
        xn = x1 * r
        g = g_ref[...]
        dn = dh * (1.0 + sc_ref[...])
        u = dn * g
        dx1 = dx2_ref[...] + r * (u - xn * jnp.mean(u * xn, axis=-1, keepdims=True))
        dx1_ref[...] = dx1
        dy1_ref[...] = (ga_ref[...] * dx1).astype(BF16)
        st_ref[0:1, :] += jnp.sum(dh, axis=0, keepdims=True)
        st_ref[1:2, :] += jnp.sum(dh * (xn * g), axis=0, keepdims=True)
        st_ref[2:3, :] += jnp.sum(dn * xn, axis=0, keepdims=True)
        st_ref[3:4, :] += jnp.sum(dx1 * y1_ref[...].astype(F32), axis=0, keepdims=True)

    row = pl.BlockSpec((tm, d), lambda i: (i, 0))
    vecs, vec_specs = zip(*[_vec(v, d) for v in (g_ffn, sc2, ga1)])
    return pl.pallas_call(
        body, name="ffn_in_bwd", grid=(t // tm,),
        in_specs=[pl.BlockSpec((2, tm, ff), lambda i: (0, i, 0)),
                  pl.BlockSpec((n_sh, d, sw), lambda i: (0, 0, 0), pipeline_mode=pl.Buffered(1)),
                  row, row, row, *vec_specs],
        out_specs=[row, row, pl.BlockSpec((8, d), lambda i: (0, 0))],
        out_shape=[SDS((t, d), F32), SDS((t, d), BF16), SDS((8, d), F32)],
        compiler_params=_params("arbitrary"))(dgu, w, x1, dx2, y1, *vecs)


def _out_proj_bwd(dy1, w_out, tm, ride=None):
    t, d = dy1.shape

    def body(dy_ref, w_ref, o_ref):
        o_ref[...] = lax.dot_general(dy_ref[...], w_ref[...], (((1,), (1,)), ((), ())),
                                     preferred_element_type=F32).astype(BF16)

    row = pl.BlockSpec((tm, d), lambda i: (i, 0))
    return _pallas(body, name="out_proj_bwd", grid=(t // tm,),
                   in_specs=[row, pl.BlockSpec((d, d), lambda i: (0, 0))], out_specs=row,
                   out_shape=SDS((t, d), BF16), args=(dy1, w_out), sem=("parallel",), ride=ride)


BWD_BLOCKS = 2


def _mixer_bwd(z, dmerged, attn, sinks, conv_w, d, ride=None):
    t, zw = z.shape
    kvw2 = zw - 6 * d
    tq = BWD_BLOCKS * BLOCK
    steps = t // tq
    halo = tq // SUBLANES_BF16
    last_halo = t // SUBLANES_BF16 - 1
    scale = HEAD_DIM ** -0.5
    seg = _segments(d, kvw2)

    def body(z_ref, kvp_ref, prev_ref, next_ref, dm_ref, dmn_ref, attn_ref, sinks_ref, cw_ref,
             dz_ref, dkv_ref, db_ref, dbkv_ref, dcw_ref, dsk_ref, carry_ref):
        n = pl.program_id(0)

        @pl.when(n == 0)
        def _():
            carry_ref[...] = jnp.zeros_like(carry_ref)
            db_ref[...] = jnp.zeros_like(db_ref)
            dbkv_ref[...] = jnp.zeros_like(dbkv_ref)
            dcw_ref[...] = jnp.zeros_like(dcw_ref)
            dsk_ref[...] = jnp.zeros_like(dsk_ref)

        def one_block(b, pending):
            rows = slice(b * BLOCK, (b + 1) * BLOCK)
            before = slice((b - 1) * BLOCK, b * BLOCK)
            dm = dm_ref[rows, :].astype(F32)
            sa = jax.nn.sigmoid(z_ref[rows, seg["ga"]].astype(F32))
            dga = dm * attn_ref[rows, :].astype(F32) * sa * (1.0 - sa)
            dz_ref[rows, seg["ga"]] = dga.astype(BF16)
            db_ref[0:1, seg["ga"]] += jnp.sum(dga, axis=0, keepdims=True)
            dattn = (dm * sa).astype(BF16)

            kv_prev = kvp_ref[...] if b == 0 else z_ref[before, seg["kv"]]
            kv = jnp.concatenate([kv_prev, z_ref[rows, seg["kv"]]], axis=0)
            k_eff, v_eff = _kv_variants(kv, kvw2 // 2)
            band, col = _attn_masks()
            valid = band & ((n > 0) | (col >= BLOCK)) if b == 0 else band
            lane_lo = lax.broadcasted_iota(jnp.int32, (2 * BLOCK, LANES), 1) < HEAD_DIM
            sink_lane = lax.broadcasted_iota(jnp.int32, (1, LANES), 1)
            rowblk = lax.broadcasted_iota(jnp.int32, (4 * BLOCK, 1), 0) // BLOCK
            dk_acc = [jnp.zeros((2 * BLOCK, LANES), F32), jnp.zeros((2 * BLOCK, LANES), F32)]
            dv_acc = [jnp.zeros((2 * BLOCK, LANES), F32), jnp.zeros((2 * BLOCK, LANES), F32)]
            dsink = jnp.zeros((1, LANES), F32)
            for h in range(2):
                q4 = _stack_pairs(z_ref, h, rows)
                do4 = jnp.concatenate([dattn[:, (4 * h + j) * LANES:(4 * h + j + 1) * LANES] for j in range(4)],
                                      axis=0)
                dq4 = jnp.zeros((4 * BLOCK, LANES), F32)
                for e in range(2):
                    s = lax.dot_general(q4, k_eff[h][e], (((1,), (1,)), ((), ())), preferred_element_type=F32)
                    p, psink = _softmax_sink(s, valid, _sink_column(sinks_ref, h, e))
                    dp = lax.dot_general(do4, v_eff[h][e], (((1,), (1,)), ((), ())), preferred_element_type=F32)
                    delta = jnp.sum(p * dp, axis=-1, keepdims=True)
                    ds = (p * (dp - delta)).astype(BF16)
                    dq4 = dq4 + jnp.dot(ds, k_eff[h][e], preferred_element_type=F32)
                    dk = lax.dot_general(q4, ds, (((0,), (0,)), ((), ())), preferred_element_type=F32).T
                    dv = lax.dot_general(do4, p.astype(BF16), (((0,), (0,)), ((), ())), preferred_element_type=F32).T
                    keep = lane_lo if e == 0 else jnp.logical_not(lane_lo)
                    slot = 0 if e == h else 1
                    dk_acc[slot] = dk_acc[slot] + jnp.where(keep, dk, 0.0)
                    dv_acc[slot] = dv_acc[slot] + jnp.where(keep, dv, 0.0)
                    dsk = -(psink * delta)
                    for j in range(4):
                        tot = jnp.sum(jnp.where(rowblk == j, dsk, 0.0), axis=0, keepdims=True)
                        dsink = dsink + jnp.where(sink_lane == GROUP * h + 2 * j + e, tot, 0.0)
                for j in range(4):
                    cols = slice((4 * h + j) * LANES, (4 * h + j + 1) * LANES)
                    dqj = dq4[j * BLOCK:(j + 1) * BLOCK]
                    dz_ref[rows, cols] = dqj.astype(BF16)
                    db_ref[0:1, cols] += jnp.sum(dqj, axis=0, keepdims=True)
            dsk_ref[0:1, :] += dsink
            dkv_new = jnp.concatenate([(dk_acc[0] + pltpu.roll(dk_acc[1], HEAD_DIM, 1)) * scale,
                                       dv_acc[0] + pltpu.roll(dv_acc[1], HEAD_DIM, 1)], axis=1)
            done = pending + dkv_new[:BLOCK]
            dkv_ref[rows, :] = done.astype(BF16)
            dbkv_ref[0:1, :] += jnp.sum(done, axis=0, keepdims=True)

            cb = z_ref[rows, seg["cb"]].astype(F32)
            cc = z_ref[rows, seg["cc"]].astype(F32)
            cx = z_ref[rows, seg["cx"]].astype(F32)
            sg = jax.nn.sigmoid(z_ref[rows, seg["gc"]].astype(F32))
            p_in = cc * cx
            if b == 0:
                prev = jnp.where(n > 0, prev_ref[:, seg["cc"]].astype(F32) * prev_ref[:, seg["cx"]].astype(F32), 0.0)
            else:
                tail = slice(b * BLOCK - SUBLANES_BF16, b * BLOCK)
                prev = z_ref[tail, seg["cc"]].astype(F32) * z_ref[tail, seg["cx"]].astype(F32)
            p_m1 = _shift_down(p_in, 1, prev)
            p_m2 = _shift_down(p_in, 2, prev)
            w0, w1, w2 = cw_ref[0:1, :], cw_ref[1:2, :], cw_ref[2:3, :]
            cconv = w0 * p_m2 + w1 * p_m1 + w2 * p_in
            dconv = dm * sg
            dgc = dm * (cb * cconv) * sg * (1.0 - sg)
            dcb = dconv * cconv
            dcc_t = dconv * cb
            if b == BWD_BLOCKS - 1:
                nxt = jnp.where(n < steps - 1,
                                dmn_ref[...].astype(F32) * jax.nn.sigmoid(next_ref[:, seg["gc"]].astype(F32))
                                * next_ref[:, seg["cb"]].astype(F32), 0.0)
            else:
                head = slice((b + 1) * BLOCK, (b + 1) * BLOCK + SUBLANES_BF16)
                nxt = (dm_ref[head, :].astype(F32) * jax.nn.sigmoid(z_ref[head, seg["gc"]].astype(F32))
                       * z_ref[head, seg["cb"]].astype(F32))
            dpin = w2 * dcc_t + w1 * _shift_up(dcc_t, 1, nxt) + w0 * _shift_up(dcc_t, 2, nxt)
            for nm, val in (("cb", dcb), ("cc", dpin * cx), ("cx", dpin * cc), ("gc", dgc)):
                dz_ref[rows, seg[nm]] = val.astype(BF16)
                db_ref[0:1, seg[nm]] += jnp.sum(val, axis=0, keepdims=True)
            dcw_ref[0:1, :] += jnp.sum(dcc_t * p_m2, axis=0, keepdims=True)
            dcw_ref[1:2, :] += jnp.sum(dcc_t * p_m1, axis=0, keepdims=True)
            dcw_ref[2:3, :] += jnp.sum(dcc_t * p_in, axis=0, keepdims=True)
            return dkv_new[BLOCK:]

        @pl.when(n < steps)
        def _():
            pending = carry_ref[...]
            for b in range(BWD_BLOCKS):
                pending = one_block(b, pending)
            carry_ref[...] = pending

        @pl.when(n == steps)
        def _():
            done = carry_ref[...]
            dkv_ref[:BLOCK, :] = done.astype(BF16)
            dkv_ref[BLOCK:, :] = jnp.zeros((tq - BLOCK, kvw2), BF16)
            dbkv_ref[0:1, :] += jnp.sum(done, axis=0, keepdims=True)

    def cur(n):
        return jnp.minimum(n, steps - 1)

    def after(n):
        return jnp.minimum((cur(n) + 1) * halo, last_halo)

    blk = pl.BlockSpec((tq, d), lambda n: (cur(n), 0))
    return _pallas(
        body, name="mixer_bwd", grid=(steps + 1,), ride=ride, sem=("arbitrary",),
        args=(z, z, z, z, dmerged, dmerged, attn, sinks, conv_w),
        in_specs=[pl.BlockSpec((tq, zw), lambda n: (cur(n), 0)),
                  pl.BlockSpec((BLOCK, kvw2), lambda n: (jnp.maximum(cur(n) * BWD_BLOCKS - 1, 0), d // kvw2)),
                  pl.BlockSpec((SUBLANES_BF16, zw), lambda n: (jnp.maximum(cur(n) * halo - 1, 0), 0)),
                  pl.BlockSpec((SUBLANES_BF16, zw), lambda n: (after(n), 0)),
                  blk,
                  pl.BlockSpec((SUBLANES_BF16, d), lambda n: (after(n), 0)),
                  blk, SMEM_SPEC, pl.BlockSpec((3, d), lambda n: (0, 0))],
        out_specs=[pl.BlockSpec((tq, zw), lambda n: (cur(n), 0)),
                   pl.BlockSpec((tq, kvw2), lambda n: (n, 0)),
                   pl.BlockSpec((8, zw), lambda n: (0, 0)), pl.BlockSpec((8, kvw2), lambda n: (0, 0)),
                   pl.BlockSpec((8, d), lambda n: (0, 0)), pl.BlockSpec((8, LANES), lambda n: (0, 0))],
        out_shape=[SDS((t, zw), BF16), SDS((t + tq, kvw2), BF16), SDS((8, zw), F32), SDS((8, kvw2), F32),
                   SDS((8, d), F32), SDS((8, LANES), F32)],
        scratch=[pltpu.VMEM((BLOCK, kvw2), F32)])


def _wgrad_in(dz, dkv, h1, tk, ride=None):
    t, zw = dz.shape
    d = h1.shape[1]
    kvw2 = dkv.shape[1]
    blk = d + kvw2
    assert zw % blk == 0
    tn = (((0,), (0,)), ((), ()))

    def body(a_ref, akv_ref, h_ref, o_ref, o16_ref):
        n, k = pl.program_id(0), pl.program_id(1)

        @pl.when(k == 0)
        def _():
            o_ref[...] = jnp.zeros_like(o_ref)

        @pl.when(n == 0)
        def _():
            o_ref[:d, :] += lax.dot_general(a_ref[:, :d], h_ref[...], tn, preferred_element_type=F32)
            o_ref[d:, :] += lax.dot_general(akv_ref[...], h_ref[...], tn, preferred_element_type=F32)

        @pl.when(n > 0)
        def _():
            o_ref[...] += lax.dot_general(a_ref[...], h_ref[...], tn, preferred_element_type=F32)

        @pl.when(k == t // tk - 1)
        def _():
            o16_ref[...] = o_ref[...].astype(BF16)

    out_spec = pl.BlockSpec((blk, d), lambda n, k: (n, 0))
    return _pallas(
        body, name="wgrad_in", grid=(zw // blk, t // tk),
        in_specs=[pl.BlockSpec((tk, blk), lambda n, k: (k, n)), pl.BlockSpec((tk, kvw2), lambda n, k: (k, 0)),
                  pl.BlockSpec((tk, d), lambda n, k: (k, 0))],
        out_specs=[out_spec, out_spec], out_shape=[SDS((zw, d), F32), SDS((zw, d), BF16)],
        args=(dz, dkv, h1), sem=("parallel", "arbitrary"), ride=ride)


def _in_proj_bwd(dz, dkv, wt, x, dx1, g_mix, sc1, tm, ride=None):
    t, d = x.shape
    zw = dz.shape[1]
    kvw2 = dkv.shape[1]
    rest = d + kvw2

    def body(a_ref, akv_ref, w_ref, x_ref, dx1_ref, g_ref, sc_ref, gx_ref, st_ref):
        @pl.when(pl.program_id(0) == 0)
        def _():
            st_ref[...] = jnp.zeros_like(st_ref)

        dh = (jnp.dot(a_ref[:, :d], w_ref[:d, :], preferred_element_type=F32)
              + jnp.dot(akv_ref[...], w_ref[d:rest, :], preferred_element_type=F32)
              + jnp.dot(a_ref[:, rest:], w_ref[rest:, :], preferred_element_type=F32))
        xx = x_ref[...]
        r = _rms(xx)
        xn = xx * r
        g = g_ref[...]
        dn = dh * (1.0 + sc_ref[...])
        u = dn * g
        gx_ref[...] = dx1_ref[...] + r * (u - xn * jnp.mean(u * xn, axis=-1, keepdims=True))
        st_ref[0:1, :] += jnp.sum(dh, axis=0, keepdims=True)
        st_ref[1:2, :] += jnp.sum(dh * (xn * g), axis=0, keepdims=True)
        st_ref[2:3, :] += jnp.sum(dn * xn, axis=0, keepdims=True)

    row = pl.BlockSpec((tm, d), lambda i: (i, 0))
    vecs, vec_specs = zip(*[_vec(v, d) for v in (g_mix, sc1)])
    return _pallas(
        body, name="in_proj_bwd", grid=(t // tm,),
        in_specs=[pl.BlockSpec((tm, zw), lambda i: (i, 0)), pl.BlockSpec((tm, kvw2), lambda i: (i, 0)),
                  pl.BlockSpec((zw, d), lambda i: (0, 0), pipeline_mode=pl.Buffered(1)),
                  row, row, *vec_specs],
        out_specs=[row, pl.BlockSpec((8, d), lambda i: (0, 0))],
        out_shape=[SDS((t, d), F32), SDS((8, d), F32)],
        args=(dz, dkv, wt, x, dx1, *vecs), sem=("arbitrary",), ride=ride)


def _to_lanes(v, rows=None):
    flat = v.reshape(-1)
    need = -(-flat.shape[0] // LANES)
    need = -(-need // 8) * 8 if rows is None else rows
    return jnp.pad(flat, (0, need * LANES - flat.shape[0])).reshape(need, LANES)


def kernel(x, c, w_ada, b_ada, g_mix, w_in, b_in, sinks, conv_w, w_out, g_ffn, w_ffn_in, w_ffn_out, g_final, loss_target, m_w_ada, m_b_ada, m_g_mix, m_w_in, m_b_in, m_sinks, m_conv_w, m_w_out, m_g_ffn, m_w_ffn_in, m_w_ffn_out, m_g_final, v_w_ada, v_b_ada, v_g_mix, v_w_in, v_b_in, v_sinks, v_conv_w, v_w_out, v_g_ffn, v_w_ffn_in, v_w_ffn_out, v_g_final):
    xs, tgt = x[0], loss_target[0]
    t, d = xs.shape
    zw = w_in.shape[2] * N_CHIP
    kvw2 = zw - 6 * d
    ff = w_ffn_out.shape[1] * N_CHIP
    n_mod = w_ada.shape[2] * N_CHIP // d
    mod_sh = w_ada.shape[2]
    cw_sh = conv_w.shape[2]
    assert d % (8 * LANES) == 0 and kvw2 == 2 * LANES and t % 512 == 0 and n_mod == 6
    xi, yi, ci = _mesh_pos()
    j_me = 2 * xi + yi
    pos = jnp.stack([ci, j_me]).astype(jnp.int32)
    tm = 512

    w_in_t, m_w_in_t, v_w_in_t = w_in[0].T, m_w_in[0].T, v_w_in[0].T
    assert d == 8 * LANES
    pack1 = jnp.concatenate([c.reshape(d // LANES, LANES), conv_w[0].reshape(-1, LANES)], axis=0)
    pack1 = jnp.pad(pack1, ((0, 16 - pack1.shape[0]), (0, 0)))
    b_ada_sh = lax.dynamic_slice(b_ada, (0, j_me * mod_sh), (1, mod_sh))
    g1, mod, w_in_g, later = _startup(pack1, w_ada[0], b_ada_sh, w_in_t, [w_out[0], w_ffn_in[0], w_ffn_out[0]])
    c_all = g1[:, :d // LANES, :].reshape(N_DEV, d)
    cw_rows = 3 * cw_sh // LANES
    conv_w_full = jnp.concatenate(
        [g1[2 * j, d // LANES:d // LANES + cw_rows, :].reshape(3, cw_sh) for j in range(N_CHIP)], axis=1)
    sh1, sc1, ga1, sh2, sc2, ga2 = [(mod, k) for k in range(6)]
    w_in_tf = w_in_g.reshape(zw, d)

    (z, h1), later = _in_proj(xs, g_mix, sc1, sh1, w_in_tf, b_in, min(t, 2048), zw // 5, ride=_x_gather_ici(later))
    (attn, merged), later = _mixer_fwd(z, sinks, conv_w_full, d, ride=_x_gather_d2d(later))
    w_out_f = later[0].reshape(d, d)
    w_ffn_in_f = later[1]
    w_ffn_out_f = later[2].reshape(ff, d)
    tml = min(t, 1024)
    y1, x1, h2 = _out_proj_fwd(merged, w_out_f, xs, ga1, g_ffn, sc2, sh2, tml)
    gu, act = _ffn_in_fwd(h2, w_ffn_in_f, ff, tml, ff // 2)
    dx2, dy2, st_loss = _ffn_out_loss(act, w_ffn_out_f, x1, tgt, ga2, g_final.reshape(1, d), tml)

    dgu = _ffn_out_bwd(dy2, w_ffn_out_f, gu, tml, ff // 2)
    tk = min(t, 2048)
    dw_ffn_out, _ = _wgrad(
        act, dy2, pl.BlockSpec((tk, ff // 2), lambda m, k: (k, m)), pl.BlockSpec((tk, d), lambda m, k: (k, 0)),
        pl.BlockSpec((ff // 2, d), lambda m, k: (m, 0)), SDS((ff, d), F32), (2, t // tk), "wgrad_ffn_out")
    dx1, dy1, st_ffn = _ffn_in_bwd(dgu, w_ffn_in_f, x1, dx2, y1, g_ffn, sc2, ga1, tm)
    dw_ffn_in, _ = _wgrad(
        h2, dgu, pl.BlockSpec((tk, d), lambda n, k: (k, 0)),
        pl.BlockSpec((None, tk, ff // 2), lambda n, k: (n // 2, k, n % 2)),
        pl.BlockSpec((None, d, ff // 2), lambda n, k: (n, 0, 0)), SDS((N_CHIP, d, ff // 2), F32),
        (N_CHIP, t // tk), "wgrad_ffn_in")
    dw_out, _ = _wgrad(
        merged, dy1, pl.BlockSpec((tk, d), lambda m, k: (k, 0)), pl.BlockSpec((tk, d), lambda m, k: (k, 0)),
        pl.BlockSpec((d, d), lambda m, k: (0, 0)), SDS((d, d), F32), (1, t // tk), "wgrad_out")

    early = [[g.reshape(N_CHIP, -1, g.shape[-1]) for g in pair] for pair in (dw_out, dw_ffn_in, dw_ffn_out)]
    early_names = ["w_out", "w_ffn_in", "w_ffn_out"]
    dmerged, _ = _out_proj_bwd(dy1, w_out_f, tml)
    (dz, dkv_shifted, db_z, db_kv, dcw, dsk), terms = _mixer_bwd(
        z, dmerged, attn, sinks, conv_w_full, d, ride=_x_reduce([e[0] for e in early], [e[1] for e in early]))
    dkv = dkv_shifted[BLOCK:BLOCK + t]
    fulls = [_sum_terms(pos, e[0], s, r, "sum_terms_" + nm)
             for e, s, r, nm in zip(early, terms[:3], terms[3:], early_names)]
    dw_in_t, (g_w_out, g_w_ffn_in, g_w_ffn_out) = _wgrad_in(dz, dkv, h1, tk, ride=_x_pair_exchange(fulls))
    dw_in_t = [g.reshape(N_CHIP, zw // N_CHIP, d) for g in dw_in_t]

    (grad_x, st_in), (from_sib, from_far) = _in_proj_bwd(dz, dkv, w_in_tf, xs, dx1, g_mix, sc1, tm,
                                                         ride=_x_reduce([dw_in_t[0]], [dw_in_t[1]]))
    g_w_in_half = _sum_terms(pos, dw_in_t[0], from_sib, from_far, "sum_terms_w_in")

    dmod = jnp.concatenate([st_in[0:1], st_in[1:2], st_ffn[3:4], st_ffn[0:1], st_ffn[1:2], st_loss[0:1]], axis=1)
    db_in = jnp.concatenate([db_z[0:1, :d], db_kv[0:1], db_z[0:1, d + kvw2:]], axis=1)
    seg = [dmod, st_in[2:3], db_in, dsk[0:1], dcw[0:3].reshape(1, 3 * d), st_ffn[2:3], st_loss[1:2],
           st_loss[2:3, :LANES]]
    sizes = [s.shape[1] for s in seg]
    pack2 = _to_lanes(jnp.concatenate(seg, axis=1))
    packs, g_w_in_t = _tail_exchange(pack2, g_w_in_half)
    tot = _pack_sum(packs).reshape(-1)
    offs = [sum(sizes[:k]) for k in range(len(sizes))]
    gb_ada, gg_mix, gb_in, gsinks, gcw, gg_ffn, gg_final, loss_v = [tot[o:o + s] for o, s in zip(offs, sizes)]
    loss = loss_v[0]
    gsinks = gsinks[:sinks.shape[1]]
    gcw_sh = lax.dynamic_slice(gcw.reshape(3, d), (0, j_me * cw_sh), (3, cw_sh))

    dmod_all = packs[:, :n_mod * d // LANES, :].reshape(N_DEV, n_mod * d)
    g_w_ada = _ada_wgrad(c_all, lax.dynamic_slice(dmod_all, (0, j_me * mod_sh), (N_DEV, mod_sh)))

    out_g, out_d, out_m, out_v = {}, {}, {}, {}
    big = {"w_ada": (w_ada[0], g_w_ada, m_w_ada[0], v_w_ada[0]),
           "w_out": (w_out[0], g_w_out, m_w_out[0], v_w_out[0]),
           "w_ffn_in": (w_ffn_in[0], g_w_ffn_in, m_w_ffn_in[0], v_w_ffn_in[0]),
           "w_ffn_out": (w_ffn_out[0], g_w_ffn_out, m_w_ffn_out[0], v_w_ffn_out[0])}
    for nm, (w, g, m, v) in big.items():
        out_g[nm], out_d[nm], out_m[nm], out_v[nm] = [o[None] for o in _adamw(w, g, m, v, "adamw_" + nm)]
    out_g["w_in"], out_d["w_in"], out_m["w_in"], out_v["w_in"] = [
        o.T[None] for o in _adamw(w_in_t, g_w_in_t, m_w_in_t, v_w_in_t, "adamw_w_in")]
    small = {"b_ada": (b_ada, gb_ada, m_b_ada, v_b_ada), "g_mix": (g_mix, gg_mix, m_g_mix, v_g_mix),
             "b_in": (b_in, gb_in, m_b_in, v_b_in), "sinks": (sinks, gsinks, m_sinks, v_sinks),
             "conv_w": (conv_w, gcw_sh, m_conv_w, v_conv_w), "g_ffn": (g_ffn, gg_ffn, m_g_ffn, v_g_ffn),
             "g_final": (g_final, gg_final, m_g_final, v_g_final)}
    def two_d(a):
        return a.reshape(-1, a.shape[-1])

    s_out = _adamw_small([tuple(two_d(a.reshape(w.shape)) for a in (w, g, m, v)) for w, g, m, v in small.values()])
    for (nm, (w, g, _, _)), res in zip(small.items(), s_out):
        out_g[nm] = g.reshape(w.shape)
        out_d[nm], out_m[nm], out_v[nm] = [o.reshape(w.shape) for o in res]

    order = ["w_ada", "b_ada", "g_mix", "w_in", "b_in", "sinks", "conv_w", "w_out", "g_ffn", "w_ffn_in", "w_ffn_out",
             "g_final"]
    return (loss, grad_x[None], *[out_g[k] for k in order], *[out_d[k] for k in order],
            *[out_m[k] for k in order], *[out_v[k] for k in order])
```

```python
import functools

import jax
import jax.numpy as jnp
from jax import lax
from jax.experimental import pallas as pl
from jax.experimental.pallas import tpu as pltpu

F32 = jnp.float32
BF16 = jnp.bfloat16
EPS = 1e-6
HEAD_DIM = 64
GROUP = 8
BLOCK = 128
LANES = 128
SUBLANES_BF16 = 16
N_DEV = 8
N_CHIP = 4
VMEM_LIMIT = 56 * 1024 * 1024
MESH = pl.DeviceIdType.MESH

ADAM_LR = 0.001
ADAM_B1 = 0.9
ADAM_B2 = 0.999
ADAM_EPS = 1e-08
ADAM_WD = 0.01
ADAM_STEP = 10

SDS = jax.ShapeDtypeStruct
ANY = pl.BlockSpec(memory_space=pl.ANY)
VMEM_SPEC = pl.BlockSpec(memory_space=pltpu.VMEM)
SMEM_SPEC = pl.BlockSpec(memory_space=pltpu.SMEM)


def _params(*sem):
    return pltpu.CompilerParams(dimension_semantics=sem, vmem_limit_bytes=VMEM_LIMIT)


def _vec(v, d):
    arr, k = v if isinstance(v, tuple) else (v, 0)
    return arr, pl.BlockSpec((1, d), lambda *_: (0, k))


def _mesh_pos():
    return lax.axis_index("x"), lax.axis_index("y"), lax.axis_index("c")


def _row_tile(rows, cols, itemsize=4, budget=1 << 20, mult=8):
    best = None
    for t in range(mult, rows + 1, mult):
        if rows % t == 0 and t * cols * itemsize <= budget:
            best = t
    if best is None:
        best = rows
    return best


def _gather_all(v_ref, out_ref, send_sems, recv_sems, local_sem):
    x, y, c = _mesh_pos()
    me = 4 * x + 2 * y + c
    mine = pltpu.make_async_copy(v_ref, out_ref.at[me], local_sem)
    mine.start()
    peers = []
    for k in range(1, N_DEV):
        px = 1 - x if k & 4 else x
        py = 1 - y if k & 2 else y
        pc = 1 - c if k & 1 else c
        peers.append((px, py, pc))

    def copy(k, block):
        return pltpu.make_async_remote_copy(
            src_ref=v_ref, dst_ref=out_ref.at[block], send_sem=send_sems.at[k], recv_sem=recv_sems.at[k],
            device_id=peers[k], device_id_type=MESH)

    sends = [copy(k, me) for k in range(N_DEV - 1)]
    for cp in sends:
        cp.start()
    for k, (px, py, pc) in enumerate(peers):
        copy(k, 4 * px + 2 * py + pc).wait_recv()
    for cp in sends:
        cp.wait_send()
    mine.wait()


def _small_sems():
    return [pltpu.SemaphoreType.DMA((N_DEV - 1,)), pltpu.SemaphoreType.DMA((N_DEV - 1,)), pltpu.SemaphoreType.DMA]


def _tail_exchange(pack, full):
    def body(pack_ref, full_unused, packs_ref, full_ref, s1, r1, l1, send_sem, recv_sem):
        del full_unused
        x, y, c = _mesh_pos()
        half = full_ref.shape[0] // 2
        rows = pl.ds(pl.multiple_of(c * half, 8), half)
        swap = pltpu.make_async_remote_copy(
            src_ref=full_ref.at[rows], dst_ref=full_ref.at[rows], send_sem=send_sem, recv_sem=recv_sem,
            device_id=(x, y, 1 - c), device_id_type=MESH)
        swap.start()
        _gather_all(pack_ref, packs_ref, s1, r1, l1)
        swap.wait()

    return pl.pallas_call(
        body, name="tail_exchange", out_shape=[SDS((N_DEV,) + pack.shape, pack.dtype), SDS(full.shape, full.dtype)],
        in_specs=[VMEM_SPEC, ANY], out_specs=[VMEM_SPEC, ANY], input_output_aliases={1: 1},
        scratch_shapes=_small_sems() + [pltpu.SemaphoreType.DMA, pltpu.SemaphoreType.DMA])(pack, full)


def _other_chips(x, y):
    return [(1 - x, y), (x, 1 - y), (1 - x, 1 - y)]


def _startup(pack, w_ada_sh, b_ada_sh, w_buf, later):
    d, n = w_ada_sh.shape
    kc = d // LANES
    n_l = len(later)
    chunk_rows = [_row_tile(a.shape[0], a.shape[1], budget=3 << 19, mult=SUBLANES_BF16) for a in later]

    def body(*refs):
        pack_ref, wa_hbm, ba_ref, w_in_unused = refs[:4]
        later_src = refs[4:4 + n_l]
        packs_ref, mine_ref, w_ref = refs[4 + n_l:7 + n_l]
        later_dst = refs[7 + n_l:7 + 2 * n_l]
        wa_scr, mod_scr, mod_ref = refs[7 + 2 * n_l:10 + 2 * n_l]
        f32_bufs = refs[10 + 2 * n_l:10 + 3 * n_l]
        bf16_bufs = refs[10 + 3 * n_l:10 + 4 * n_l]
        (s1, r1, l1, s2, r2, l2, send_sems, recv_sems, fsend_sems, frecv_sems, relay_send, relay_recv, wa_sem,
         cast_sems) = refs[10 + 4 * n_l:]
        del w_in_unused
        x, y, c = _mesh_pos()
        j_me = 2 * x + y
        chips = _other_chips(x, y)
        half = w_ref.shape[1] // 2

        def rows_of(which):
            return pl.ds(pl.multiple_of(which * half, SUBLANES_BF16), half)

        def copy(p, block, rows, over_ici):
            sems = (send_sems, recv_sems) if over_ici else (fsend_sems, frecv_sems)
            return pltpu.make_async_remote_copy(
                src_ref=w_ref.at[block, rows], dst_ref=w_ref.at[block, rows], send_sem=sems[0].at[p],
                recv_sem=sems[1].at[p], device_id=(*chips[p], c) if over_ici else (x, y, 1 - c), device_id_type=MESH)

        def block_of(p):
            return 2 * chips[p][0] + chips[p][1]

        def relay(q, block):
            rows = pl.ds(pl.multiple_of(c * half + q * (half // 2), SUBLANES_BF16), half // 2)
            return pltpu.make_async_remote_copy(
                src_ref=w_ref.at[block, rows], dst_ref=w_ref.at[block, rows], send_sem=relay_send.at[q],
                recv_sem=relay_recv.at[q], device_id=(*chips[1 - q], c), device_id_type=MESH)

        load_wa = pltpu.make_async_copy(wa_hbm, wa_scr, wa_sem)
        load_wa.start()
        _gather_all(pack_ref, packs_ref, s1, r1, l1)
        sends = [copy(p, j_me, rows_of(c), True) for p in range(2)]
        for cp in sends:
            cp.start()
        load_wa.wait()
        acc = jnp.zeros((N_DEV, n), F32)
        for k in range(kc):
            ck = packs_ref[:, k, :]
            sk = (ck * jax.nn.sigmoid(ck)).astype(BF16)
            acc = acc + jnp.dot(sk, wa_scr[k * LANES:(k + 1) * LANES, :].astype(BF16), preferred_element_type=F32)
        mod_scr[...] = acc + ba_ref[...]
        _gather_all(mod_scr, mod_ref, s2, r2, l2)
        for j in range(N_CHIP):
            mine_ref[:, j * n:(j + 1) * n] = mod_ref[2 * j, pl.ds(4 * x + 2 * y + c, 1), :]
        passed = []
        for q in range(2):
            copy(q, block_of(q), rows_of(c), True).wait_recv()
            for cp in (relay(q, block_of(q)), copy(q, block_of(q), rows_of(c), False)):
                cp.start()
                passed.append(cp)
        for src, dst, fbuf, bbuf in zip(later_src, later_dst, f32_bufs, bf16_bufs):
            cr = fbuf.shape[0]
            for k in range(src.shape[0] // cr):
                rows = pl.ds(k * cr, cr)
                cin = pltpu.make_async_copy(src.at[rows], fbuf, cast_sems.at[0])
                cin.start()
                cin.wait()
                bbuf[...] = fbuf[...].astype(BF16)
                cout = pltpu.make_async_copy(bbuf, dst.at[j_me, rows], cast_sems.at[1])
                cout.start()
                cout.wait()
        for q in range(2):
            relay(q, block_of(2)).wait_recv()
        fw = copy(2, block_of(2), rows_of(c), False)
        fw.start()
        for p in range(3):
            copy(p, block_of(p), rows_of(1 - c), False).wait_recv()
        for cp in sends + passed + [fw]:
            cp.wait_send()

    res = pl.pallas_call(
        body, name="startup",
        out_shape=[SDS((N_DEV,) + pack.shape, F32), SDS((1, N_CHIP * n), F32), SDS(w_buf.shape, w_buf.dtype)]
        + [SDS((N_CHIP,) + a.shape, BF16) for a in later],
        in_specs=[VMEM_SPEC, ANY, VMEM_SPEC, ANY] + [ANY] * n_l, out_specs=[VMEM_SPEC, VMEM_SPEC, ANY] + [ANY] * n_l,
        input_output_aliases={3: 2},
        scratch_shapes=[pltpu.VMEM((d, n), F32), pltpu.VMEM((N_DEV, n), F32), pltpu.VMEM((N_DEV, N_DEV, n), F32)]
        + [pltpu.VMEM((cr, a.shape[1]), F32) for cr, a in zip(chunk_rows, later)]
        + [pltpu.VMEM((cr, a.shape[1]), BF16) for cr, a in zip(chunk_rows, later)]
        + _small_sems() + _small_sems()
        + [pltpu.SemaphoreType.DMA((3,))] * 4 + [pltpu.SemaphoreType.DMA((2,))] * 2 + [pltpu.SemaphoreType.DMA]
        + [pltpu.SemaphoreType.DMA((2,))],
        compiler_params=pltpu.CompilerParams(vmem_limit_bytes=VMEM_LIMIT),
    )(pack, w_ada_sh, b_ada_sh, w_buf, *later)
    return res[0], res[1], res[2], list(res[3:])


class _Exchange:
    def __init__(self, operands, out_shape, in_place, n_sems, copies):
        self.operands, self.out_shape, self.in_place, self.n_sems, self.copies = (
            list(operands), list(out_shape), in_place, n_sems, copies)

    def sems(self):
        return [pltpu.SemaphoreType.DMA((self.n_sems,)), pltpu.SemaphoreType.DMA((self.n_sems,))]


def _x_gather_ici(bufs):
    def copies(ins, outs, send_sems, recv_sems):
        x, y, c = _mesh_pos()
        chips = _other_chips(x, y)
        out = []
        for w in range(len(outs)):
            half = outs[w].shape[1] // 2
            rows = pl.ds(pl.multiple_of(c * half, SUBLANES_BF16), half)
            for p in range(3):
                out.append(pltpu.make_async_remote_copy(
                    src_ref=outs[w].at[2 * x + y, rows], dst_ref=outs[w].at[2 * x + y, rows],
                    send_sem=send_sems.at[w * 3 + p], recv_sem=recv_sems.at[w * 3 + p],
                    device_id=(*chips[p], c), device_id_type=MESH))
        return out

    return _Exchange(bufs, [SDS(b.shape, b.dtype) for b in bufs], True, 3 * len(bufs), copies)


def _x_gather_d2d(bufs):
    def copies(ins, outs, send_sems, recv_sems):
        x, y, c = _mesh_pos()
        chips = _other_chips(x, y)
        out = []
        for w in range(len(outs)):
            half = outs[w].shape[1] // 2
            rows = pl.ds(pl.multiple_of(c * half, SUBLANES_BF16), half)
            for p in range(3):
                block = 2 * chips[p][0] + chips[p][1]
                out.append(pltpu.make_async_remote_copy(
                    src_ref=outs[w].at[block, rows], dst_ref=outs[w].at[block, rows],
                    send_sem=send_sems.at[w * 3 + p], recv_sem=recv_sems.at[w * 3 + p],
                    device_id=(x, y, 1 - c), device_id_type=MESH))
        return out

    return _Exchange(bufs, [SDS(b.shape, b.dtype) for b in bufs], True, 3 * len(bufs), copies)


N_REMOTE = 6


def _x_reduce(grads32, grads16):
    n_w = len(grads32)

    def copies(ins, outs, send_sems, recv_sems):
        g32, g16 = ins[:n_w], ins[n_w:]
        from_sib, from_far = outs[:n_w], outs[n_w:]
        x, y, c = _mesh_pos()
        chips = _other_chips(x, y)
        out = []
        for w in range(n_w):
            half = g32[w].shape[1] // 2
            k0 = w * (N_REMOTE + 1)
            out.append(pltpu.make_async_remote_copy(
                src_ref=g32[w].at[2 * x + y, pl.ds(pl.multiple_of((1 - c) * half, SUBLANES_BF16), half), :],
                dst_ref=from_sib[w], send_sem=send_sems.at[k0], recv_sem=recv_sems.at[k0],
                device_id=(x, y, 1 - c), device_id_type=MESH))
            for p in range(3):
                for f in range(2):
                    tc = c if f == 0 else 1 - c
                    k = 2 * p + f
                    out.append(pltpu.make_async_remote_copy(
                        src_ref=g16[w].at[2 * chips[p][0] + chips[p][1],
                                          pl.ds(pl.multiple_of(tc * half, SUBLANES_BF16), half), :],
                        dst_ref=from_far[w].at[k], send_sem=send_sems.at[k0 + 1 + k], recv_sem=recv_sems.at[k0 + 1 + k],
                        device_id=(*chips[p], tc), device_id_type=MESH))
        return out

    shapes = ([SDS((g.shape[1] // 2, g.shape[2]), g.dtype) for g in grads32]
              + [SDS((N_REMOTE, g.shape[1] // 2, g.shape[2]), g.dtype) for g in grads16])
    return _Exchange(list(grads32) + list(grads16), shapes, False, (N_REMOTE + 1) * n_w, copies)


def _x_pair_exchange(fulls):
    def copies(ins, outs, send_sems, recv_sems):
        x, y, c = _mesh_pos()
        out = []
        for w in range(len(outs)):
            half = outs[w].shape[0] // 2
            rows = pl.ds(pl.multiple_of(c * half, 8), half)
            out.append(pltpu.make_async_remote_copy(
                src_ref=outs[w].at[rows], dst_ref=outs[w].at[rows], send_sem=send_sems.at[w],
                recv_sem=recv_sems.at[w], device_id=(x, y, 1 - c), device_id_type=MESH))
        return out

    return _Exchange(fulls, [SDS(f.shape, f.dtype) for f in fulls], True, len(fulls), copies)


def _pallas(body, *, name, grid, in_specs, out_specs, out_shape, args, scratch=(), sem=None, ride=None):
    single = not isinstance(out_specs, (list, tuple))
    out_specs_l = [out_specs] if single else list(out_specs)
    out_shape_l = [out_shape] if single else list(out_shape)
    n_in, n_out, n_scr = len(in_specs), len(out_specs_l), len(scratch)
    if ride is None:
        res = pl.pallas_call(body, name=name, grid=grid, in_specs=list(in_specs), out_specs=out_specs,
                             out_shape=out_shape, scratch_shapes=list(scratch), compiler_params=_params(*sem))(*args)
        return res, None
    n_x, n_xo = len(ride.operands), len(ride.out_shape)

    def full_body(*refs):
        ins, x_ins = refs[:n_in], refs[n_in:n_in + n_x]
        outs = refs[n_in + n_x:n_in + n_x + n_out]
        x_outs = refs[n_in + n_x + n_out:n_in + n_x + n_out + n_xo]
        rest = refs[n_in + n_x + n_out + n_xo:]
        scr, (send_sems, recv_sems) = rest[:n_scr], rest[n_scr:]
        first = functools.reduce(jnp.logical_and, [pl.program_id(a) == 0 for a in range(len(grid))])
        last = functools.reduce(jnp.logical_and, [pl.program_id(a) == grid[a] - 1 for a in range(len(grid))])

        @pl.when(first)
        def _():
            for cp in ride.copies(x_ins, x_outs, send_sems, recv_sems):
                cp.start()

        body(*ins, *outs, *scr)

        @pl.when(last)
        def _():
            for cp in ride.copies(x_ins, x_outs, send_sems, recv_sems):
                cp.wait()

    res = pl.pallas_call(
        full_body, name=name, grid=grid, in_specs=list(in_specs) + [ANY] * n_x,
        out_specs=out_specs_l + [ANY] * n_xo, out_shape=out_shape_l + ride.out_shape,
        input_output_aliases={n_in + k: n_out + k for k in range(n_x)} if ride.in_place else {},
        scratch_shapes=list(scratch) + ride.sems(),
        compiler_params=_params(*(["arbitrary"] * len(grid))))(*args, *ride.operands)
    own = res[0] if single else list(res[:n_out])
    return own, list(res[n_out:])


def _cast_into_block(pos, w, name):
    rows, cols = w.shape
    tr = _row_tile(rows, cols, mult=SUBLANES_BF16)

    def body(pos_ref, w_ref, o_ref):
        del pos_ref
        o_ref[...] = w_ref[...].astype(BF16)

    return pl.pallas_call(
        body, name=name,
        grid_spec=pltpu.PrefetchScalarGridSpec(
            num_scalar_prefetch=1, grid=(rows // tr,),
            in_specs=[pl.BlockSpec((tr, cols), lambda i, pos_ref: (i, 0))],
            out_specs=pl.BlockSpec((None, tr, cols), lambda i, pos_ref: (pos_ref[1], i, 0))),
        out_shape=SDS((N_CHIP, rows, cols), BF16), compiler_params=_params("parallel"))(pos, w)


def _sum_terms(pos, grad, from_sib, from_far, name):
    _, rows, cols = grad.shape
    half = rows // 2
    tr = _row_tile(half, cols, mult=SUBLANES_BF16)
    nblk = half // tr

    def body(pos_ref, g_ref, s_ref, r_ref, o_ref):
        del pos_ref
        acc = g_ref[...] + s_ref[...]
        for k in range(N_REMOTE):
            acc = acc + r_ref[k].astype(F32)
        o_ref[...] = acc

    return pl.pallas_call(
        body, name=name,
        grid_spec=pltpu.PrefetchScalarGridSpec(
            num_scalar_prefetch=1, grid=(nblk,),
            in_specs=[pl.BlockSpec((None, tr, cols), lambda i, pos_ref: (pos_ref[1], pos_ref[0] * nblk + i, 0)),
                      pl.BlockSpec((tr, cols), lambda i, pos_ref: (i, 0)),
                      pl.BlockSpec((N_REMOTE, tr, cols), lambda i, pos_ref: (0, i, 0))],
            out_specs=pl.BlockSpec((tr, cols), lambda i, pos_ref: (pos_ref[0] * nblk + i, 0))),
        out_shape=SDS((rows, cols), F32),
        compiler_params=_params("parallel"),
    )(pos, grad, from_sib, from_far)


def _adamw(w, g, m, v, name):
    rows, cols = w.shape
    tr = _row_tile(rows, cols, budget=1 << 21)

    def body(w_ref, g_ref, m_ref, v_ref, go_ref, d_ref, nm_ref, nv_ref):
        go_ref[...] = g_ref[...]
        _adamw_update(w_ref, g_ref, m_ref, v_ref, d_ref, nm_ref, nv_ref)

    spec = pl.BlockSpec((tr, cols), lambda i: (i, 0))
    return pl.pallas_call(body, name=name, grid=(rows // tr,), in_specs=[spec] * 4, out_specs=[spec] * 4,
                          out_shape=[SDS((rows, cols), F32)] * 4, compiler_params=_params("parallel"))(w, g, m, v)


def _adamw_update(w_ref, g_ref, m_ref, v_ref, d_ref, nm_ref, nv_ref):
    gg = g_ref[...]
    nm = ADAM_B1 * m_ref[...] + (1.0 - ADAM_B1) * gg
    nv = ADAM_B2 * v_ref[...] + (1.0 - ADAM_B2) * (gg * gg)
    m_hat = nm / (1.0 - ADAM_B1 ** ADAM_STEP)
    v_hat = nv / (1.0 - ADAM_B2 ** ADAM_STEP)
    d_ref[...] = -ADAM_LR * (m_hat / (jnp.sqrt(v_hat) + ADAM_EPS) + ADAM_WD * w_ref[...])
    nm_ref[...] = nm
    nv_ref[...] = nv


def _adamw_small(params):
    n_p = len(params)

    def body(*refs):
        ins, outs = refs[:4 * n_p], refs[4 * n_p:]
        for k in range(n_p):
            _adamw_update(*ins[4 * k:4 * k + 4], *outs[3 * k:3 * k + 3])

    flat = [a for tup in params for a in tup]
    res = pl.pallas_call(
        body, name="adamw_small", in_specs=[VMEM_SPEC] * (4 * n_p), out_specs=[VMEM_SPEC] * (3 * n_p),
        out_shape=[SDS(tup[0].shape, F32) for tup in params for _ in range(3)])(*flat)
    return [res[3 * k:3 * k + 3] for k in range(n_p)]


def _pack_sum(gathered):
    _, rows, cols = gathered.shape

    def body(g_ref, o_ref):
        acc = g_ref[0]
        for d in range(1, N_DEV):
            acc = acc + g_ref[d]
        o_ref[...] = acc

    return pl.pallas_call(body, name="pack_sum", in_specs=[VMEM_SPEC], out_specs=VMEM_SPEC,
                          out_shape=SDS((rows, cols), F32))(gathered)


def _ada_wgrad(c_all, dmod_sh):
    d = c_all.shape[1]
    n = dmod_sh.shape[1]
    tn = 512

    def body(c_ref, g_ref, o_ref):
        cc = c_ref[...]
        s = cc * jax.nn.sigmoid(cc)
        o_ref[...] = lax.dot_general(s, g_ref[...], (((0,), (0,)), ((), ())), preferred_element_type=F32,
                                     precision=lax.Precision.HIGHEST)

    return pl.pallas_call(
        body, name="ada_wgrad", grid=(n // tn,),
        in_specs=[pl.BlockSpec((N_DEV, d), lambda j: (0, 0)), pl.BlockSpec((N_DEV, tn), lambda j: (0, j))],
        out_specs=pl.BlockSpec((d, tn), lambda j: (0, j)),
        out_shape=SDS((d, n), F32), compiler_params=_params("parallel"))(c_all, dmod_sh)


def _rms(xf):
    return lax.rsqrt(jnp.mean(xf * xf, axis=-1, keepdims=True) + EPS)


def _in_proj(x, g, sc, sh, wt, b, tm, tn, ride=None):
    t, d = x.shape
    n = wt.shape[0]

    def body(x_ref, g_ref, sc_ref, sh_ref, w_ref, b_ref, z_ref, h_ref):
        @pl.when(pl.program_id(1) == 0)
        def _():
            xf = x_ref[...]
            h_ref[...] = ((xf * _rms(xf) * g_ref[...]) * (1.0 + sc_ref[...]) + sh_ref[...]).astype(BF16)

        acc = lax.dot_general(h_ref[...], w_ref[...], (((1,), (1,)), ((), ())), preferred_element_type=F32)
        z_ref[...] = (acc + b_ref[...]).astype(BF16)

    row = pl.BlockSpec((tm, d), lambda i, j: (i, 0))
    vecs, vec_specs = zip(*[_vec(v, d) for v in (g, sc, sh)])
    return _pallas(
        body, name="in_proj", grid=(t // tm, n // tn),
        in_specs=[row, *vec_specs, pl.BlockSpec((tn, d), lambda i, j: (j, 0)),
                  pl.BlockSpec((1, tn), lambda i, j: (0, j))],
        out_specs=[pl.BlockSpec((tm, tn), lambda i, j: (i, j)), row],
        out_shape=[SDS((t, n), BF16), SDS((t, d), BF16)], args=(x, *vecs, wt, b),
        sem=("parallel", "arbitrary"), ride=ride)


def _segments(d, kvw2):
    o = d + kvw2
    names = ("cb", "cc", "cx", "ga", "gc")
    seg = {nm: slice(o + k * d, o + (k + 1) * d) for k, nm in enumerate(names)}
    seg["q"], seg["kv"] = slice(0, d), slice(d, o)
    return seg


def _attn_masks():
    rows = 4 * BLOCK
    r = lax.broadcasted_iota(jnp.int32, (rows, 2 * BLOCK), 0) & (BLOCK - 1)
    col = lax.broadcasted_iota(jnp.int32, (rows, 2 * BLOCK), 1)
    return (col > r) & (col <= r + BLOCK), col


def _kv_variants(kv, n_kv_w):
    assert n_kv_w == LANES
    kb, vb = kv[:, :LANES] * (HEAD_DIM ** -0.5), kv[:, LANES:]
    kr, vr = pltpu.roll(kb, HEAD_DIM, 1), pltpu.roll(vb, HEAD_DIM, 1)
    lane = lax.broadcasted_iota(jnp.int32, kb.shape, 1)
    lo = lane < HEAD_DIM
    zero = jnp.zeros_like(kb)
    k_eff = [[None, None], [None, None]]
    v_eff = [[None, None], [None, None]]
    for h in range(2):
        for e in range(2):
            ksrc, vsrc = (kb, vb) if e == h else (kr, vr)
            keep = lo if e == 0 else jnp.logical_not(lo)
            k_eff[h][e] = jnp.where(keep, ksrc, zero)
            v_eff[h][e] = jnp.where(keep, vsrc, zero)
    return k_eff, v_eff


def _sink_column(sinks_ref, h, e):
    rowblk = lax.broadcasted_iota(jnp.int32, (4 * BLOCK, 1), 0) // BLOCK
    col = jnp.zeros((4 * BLOCK, 1), F32)
    for j in range(4):
        col = jnp.where(rowblk == j, sinks_ref[0, GROUP * h + 2 * j + e], col)
    return col


def _softmax_sink(s, valid, sink):
    s = jnp.where(valid, s, -jnp.inf)
    m = jnp.maximum(jnp.max(s, axis=-1, keepdims=True), sink)
    p = jnp.exp(s - m)
    psink = jnp.exp(sink - m)
    den = jnp.sum(p, axis=-1, keepdims=True) + psink
    inv = 1.0 / den
    return p * inv, psink * inv


def _shift_down(a, s, prev):
    rows = a.shape[0]
    out = pltpu.roll(a, s, 0)
    row = lax.broadcasted_iota(jnp.int32, a.shape, 0)
    for t in range(s):
        out = jnp.where(row == t, prev[SUBLANES_BF16 - s + t:SUBLANES_BF16 - s + t + 1, :], out)
    del rows
    return out


def _shift_up(a, s, nxt):
    rows = a.shape[0]
    out = pltpu.roll(a, rows - s, 0)
    row = lax.broadcasted_iota(jnp.int32, a.shape, 0)
    for t in range(s):
        out = jnp.where(row == rows - s + t, nxt[t:t + 1, :], out)
    return out


def _stack_pairs(ref, h, rows=slice(None)):
    return jnp.concatenate([ref[rows, (4 * h + j) * LANES:(4 * h + j + 1) * LANES] for j in range(4)], axis=0)


FWD_BLOCKS = 4


def _mixer_fwd(z, sinks, conv_w, d, ride=None):
    t, zw = z.shape
    kvw2 = zw - 6 * d
    tq = FWD_BLOCKS * BLOCK
    halo = tq // SUBLANES_BF16
    seg = _segments(d, kvw2)

    def body(z_ref, kvp_ref, prev_ref, sinks_ref, cw_ref, attn_ref, merged_ref):
        n = pl.program_id(0)
        band, col = _attn_masks()
        for b in range(FWD_BLOCKS):
            rows = slice(b * BLOCK, (b + 1) * BLOCK)
            before = slice((b - 1) * BLOCK, b * BLOCK)
            kv_prev = kvp_ref[...] if b == 0 else z_ref[before, seg["kv"]]
            kv = jnp.concatenate([kv_prev, z_ref[rows, seg["kv"]]], axis=0)
            k_eff, v_eff = _kv_variants(kv, kvw2 // 2)
            valid = band & ((n > 0) | (col >= BLOCK)) if b == 0 else band
            for h in range(2):
                q4 = _stack_pairs(z_ref, h, rows)
                o4 = jnp.zeros((4 * BLOCK, LANES), F32)
                for e in range(2):
                    s = lax.dot_general(q4, k_eff[h][e], (((1,), (1,)), ((), ())), preferred_element_type=F32)
                    p, _ = _softmax_sink(s, valid, _sink_column(sinks_ref, h, e))
                    o4 = o4 + jnp.dot(p.astype(BF16), v_eff[h][e], preferred_element_type=F32)
                for j in range(4):
                    attn_ref[rows, (4 * h + j) * LANES:(4 * h + j + 1) * LANES] = (
                        o4[j * BLOCK:(j + 1) * BLOCK].astype(BF16))
            cb = z_ref[rows, seg["cb"]].astype(F32)
            p_in = z_ref[rows, seg["cc"]].astype(F32) * z_ref[rows, seg["cx"]].astype(F32)
            if b == 0:
                prev = jnp.where(n > 0, prev_ref[:, seg["cc"]].astype(F32) * prev_ref[:, seg["cx"]].astype(F32), 0.0)
            else:
                tail = slice(b * BLOCK - SUBLANES_BF16, b * BLOCK)
                prev = z_ref[tail, seg["cc"]].astype(F32) * z_ref[tail, seg["cx"]].astype(F32)
            cconv = (cw_ref[0:1, :] * _shift_down(p_in, 2, prev) + cw_ref[1:2, :] * _shift_down(p_in, 1, prev)
                     + cw_ref[2:3, :] * p_in)
            sa = jax.nn.sigmoid(z_ref[rows, seg["ga"]].astype(F32))
            sg = jax.nn.sigmoid(z_ref[rows, seg["gc"]].astype(F32))
            merged_ref[rows, :] = (sa * attn_ref[rows, :].astype(F32) + sg * (cb * cconv)).astype(BF16)

    blk = pl.BlockSpec((tq, d), lambda n: (n, 0))
    return _pallas(
        body, name="mixer_fwd", grid=(t // tq,),
        in_specs=[pl.BlockSpec((tq, zw), lambda n: (n, 0)),
                  pl.BlockSpec((BLOCK, kvw2), lambda n: (jnp.maximum(n * FWD_BLOCKS - 1, 0), d // kvw2)),
                  pl.BlockSpec((SUBLANES_BF16, zw), lambda n: (jnp.maximum(n * halo - 1, 0), 0)),
                  SMEM_SPEC, pl.BlockSpec((3, d), lambda n: (0, 0))],
        out_specs=[blk, blk],
        out_shape=[SDS((t, d), BF16), SDS((t, d), BF16)],
        args=(z, z, z, sinks, conv_w), sem=("parallel",), ride=ride)


def _out_proj_fwd(merged, w_out, x, ga1, g_ffn, sc2, sh2, tm):
    t, d = x.shape

    def body(m_ref, w_ref, x_ref, ga_ref, g_ref, sc_ref, sh_ref, y_ref, x1_ref, h_ref):
        y = jnp.dot(m_ref[...], w_ref[...], preferred_element_type=F32)
        x1 = x_ref[...] + ga_ref[...] * y
        y_ref[...] = y.astype(BF16)
        x1_ref[...] = x1
        h_ref[...] = ((x1 * _rms(x1) * g_ref[...]) * (1.0 + sc_ref[...]) + sh_ref[...]).astype(BF16)

    row = pl.BlockSpec((tm, d), lambda i: (i, 0))
    vecs, vec_specs = zip(*[_vec(v, d) for v in (ga1, g_ffn, sc2, sh2)])
    return pl.pallas_call(
        body, name="out_proj_fwd", grid=(t // tm,),
        in_specs=[row, pl.BlockSpec((d, d), lambda i: (0, 0)), row, *vec_specs],
        out_specs=[row, row, row],
        out_shape=[SDS((t, d), BF16), SDS((t, d), F32), SDS((t, d), BF16)],
        compiler_params=_params("parallel"))(merged, w_out, x, *vecs)


def _ffn_in_fwd(h2, w, ff, tm, tn):
    t, d = h2.shape
    nj = ff // tn
    assert w.shape == (2 * nj, d, tn)

    def body(h_ref, wg_ref, wu_ref, gu_ref, act_ref):
        hh = h_ref[...]
        g = jnp.dot(hh, wg_ref[...], preferred_element_type=F32)
        u = jnp.dot(hh, wu_ref[...], preferred_element_type=F32)
        sg = jax.nn.sigmoid(g)
        silu = g * sg
        gu_ref[0] = (u * (sg + silu * (1.0 - sg))).astype(BF16)
        gu_ref[1] = silu.astype(BF16)
        act_ref[...] = (silu * u).astype(BF16)

    return pl.pallas_call(
        body, name="ffn_in_fwd", grid=(nj, t // tm),
        in_specs=[pl.BlockSpec((tm, d), lambda j, i: (i, 0)), pl.BlockSpec((None, d, tn), lambda j, i: (j, 0, 0)),
                  pl.BlockSpec((None, d, tn), lambda j, i: (j + nj, 0, 0))],
        out_specs=[pl.BlockSpec((2, tm, tn), lambda j, i: (0, i, j)), pl.BlockSpec((tm, tn), lambda j, i: (i, j))],
        out_shape=[SDS((2, t, ff), BF16), SDS((t, ff), BF16)],
        compiler_params=_params("parallel", "parallel"))(h2, w, w)


def _ffn_out_loss(act, w, x1, target, ga2, g_final, tm):
    t, d = x1.shape
    ff = act.shape[1]

    def body(a_ref, w_ref, x1_ref, tg_ref, ga_ref, gf_ref, dx2_ref, dy2_ref, st_ref):
        @pl.when(pl.program_id(0) == 0)
        def _():
            st_ref[...] = jnp.zeros_like(st_ref)

        halves = [slice(k * (tm // 2), (k + 1) * (tm // 2)) for k in range(2)]
        y2s = [jnp.dot(a_ref[rows, :], w_ref[...], preferred_element_type=F32) for rows in halves]
        for rows, y2 in zip(halves, y2s):
            x2 = x1_ref[rows, :] + ga_ref[...] * y2
            r = _rms(x2)
            yn = x2 * r
            err = yn * gf_ref[...] - tg_ref[rows, :]
            loss = 0.5 * jnp.sum(jnp.mean(err * err, axis=-1, keepdims=True), axis=0, keepdims=True)
            dy = err * (1.0 / d)
            u = dy * gf_ref[...]
            dx2 = r * (u - yn * jnp.mean(u * yn, axis=-1, keepdims=True))
            dx2_ref[rows, :] = dx2
            dy2_ref[rows, :] = (ga_ref[...] * dx2).astype(BF16)
            st_ref[0:1, :] += jnp.sum(dx2 * y2, axis=0, keepdims=True)
            st_ref[1:2, :] += jnp.sum(dy * yn, axis=0, keepdims=True)
            st_ref[2:3, :] += jnp.broadcast_to(loss, (1, d))

    row = pl.BlockSpec((tm, d), lambda i: (i, 0))
    vecs, vec_specs = zip(*[_vec(v, d) for v in (ga2, g_final)])
    return pl.pallas_call(
        body, name="ffn_out_loss", grid=(t // tm,),
        in_specs=[pl.BlockSpec((tm, ff), lambda i: (i, 0)),
                  pl.BlockSpec((ff, d), lambda i: (0, 0), pipeline_mode=pl.Buffered(1)), row, row, *vec_specs],
        out_specs=[row, row, pl.BlockSpec((8, d), lambda i: (0, 0))],
        out_shape=[SDS((t, d), F32), SDS((t, d), BF16), SDS((8, d), F32)],
        compiler_params=_params("arbitrary"))(act, w, x1, target, *vecs)


def _ffn_out_bwd(dy2, w, gu, tm, tn):
    t, d = dy2.shape
    ff = w.shape[0]

    def body(dy_ref, w_ref, gu_ref, o_ref):
        dy = dy_ref[...]
        for lo in range(0, tn, 3 * LANES):
            cols = slice(lo, min(lo + 3 * LANES, tn))
            dact = lax.dot_general(dy, w_ref[cols, :], (((1,), (1,)), ((), ())), preferred_element_type=F32)
            o_ref[0, :, cols] = (dact * gu_ref[0, :, cols].astype(F32)).astype(BF16)
            o_ref[1, :, cols] = (dact * gu_ref[1, :, cols].astype(F32)).astype(BF16)

    gu_spec = pl.BlockSpec((2, tm, tn), lambda j, i: (0, i, j))
    return pl.pallas_call(
        body, name="ffn_out_bwd", grid=(ff // tn, t // tm),
        in_specs=[pl.BlockSpec((tm, d), lambda j, i: (i, 0)), pl.BlockSpec((tn, d), lambda j, i: (j, 0)), gu_spec],
        out_specs=gu_spec, out_shape=SDS((2, t, ff), BF16),
        compiler_params=_params("parallel", "parallel"))(dy2, w, gu)


def _wgrad(a, b, a_spec, b_spec, out_spec, out_shape, grid, name, ride=None):
    def body(a_ref, b_ref, o_ref, o16_ref):
        k = pl.program_id(len(grid) - 1)

        @pl.when(k == 0)
        def _():
            o_ref[...] = jnp.zeros_like(o_ref)

        o_ref[...] += lax.dot_general(a_ref[...], b_ref[...], (((0,), (0,)), ((), ())), preferred_element_type=F32)

        @pl.when(k == grid[-1] - 1)
        def _():
            o16_ref[...] = o_ref[...].astype(BF16)

    return _pallas(
        body, name=name, grid=grid, in_specs=[a_spec, b_spec], out_specs=[out_spec, out_spec],
        out_shape=[out_shape, SDS(out_shape.shape, BF16)], args=(a, b),
        sem=["parallel"] * (len(grid) - 1) + ["arbitrary"], ride=ride)


def _ffn_in_bwd(dgu, w, x1, dx2, y1, g_ffn, sc2, ga1, tm):
    t, d = x1.shape
    ff = dgu.shape[2]
    n_sh, _, sw = w.shape
    per = ff // sw
    nt = (((1,), (1,)), ((), ()))

    def body(a_ref, w_ref, x1_ref, dx2_ref, y1_ref, g_ref, sc_ref, ga_ref, dx1_ref, dy1_ref, st_ref):
        @pl.when(pl.program_id(0) == 0)
        def _():
            st_ref[...] = jnp.zeros_like(st_ref)

        dh = None
        for j in range(n_sh):
            part = lax.dot_general(a_ref[j // per, :, (j % per) * sw:(j % per + 1) * sw], w_ref[j], nt,
                                   preferred_element_type=F32)
            dh = part if dh is None else dh + part
        x1 = x1_ref[...]
        r = _rms(x1)
        xn = x1 * r
        g = g_ref[...]
        dn = dh * (1.0 + sc_ref[...])
        u = dn * g
        dx1 = dx2_ref[...] + r * (u - xn * jnp.mean(u * xn, axis=-1, keepdims=True))
        dx1_ref[...] = dx1
        dy1_ref[...] = (ga_ref[...] * dx1).astype(BF16)
        st_ref[0:1, :] += jnp.sum(dh, axis=0, keepdims=True)
        st_ref[1:2, :] += jnp.sum(dh * (xn * g), axis=0, keepdims=True)
        st_ref[2:3, :] += jnp.sum(dn * xn, axis=0, keepdims=True)
        st_ref[3:4, :] += jnp.sum(dx1 * y1_ref[...].astype(F32), axis=0, keepdims=True)

    row = pl.BlockSpec((tm, d), lambda i: (i, 0))
    vecs, vec_specs = zip(*[_vec(v, d) for v in (g_ffn, sc2, ga1)])
    return pl.pallas_call(
        body, name="ffn_in_bwd", grid=(t // tm,),
        in_specs=[pl.BlockSpec((2, tm, ff), lambda i: (0, i, 0)),
                  pl.BlockSpec((n_sh, d, sw), lambda i: (0, 0, 0), pipeline_mode=pl.Buffered(1)),
                  row, row, row, *vec_specs],
        out_specs=[row, row, pl.BlockSpec((8, d), lambda i: (0, 0))],
        out_shape=[SDS((t, d), F32), SDS((t, d), BF16), SDS((8, d), F32)],
        compiler_params=_params("arbitrary"))(dgu, w, x1, dx2, y1, *vecs)


def _out_proj_bwd(dy1, w_out, tm, ride=None):
    t, d = dy1.shape

    def body(dy_ref, w_ref, o_ref):
        o_ref[...] = lax.dot_general(dy_ref[...], w_ref[...], (((1,), (1,)), ((), ())),
                                     preferred_element_type=F32).astype(BF16)

    row = pl.BlockSpec((tm, d), lambda i: (i, 0))
    return _pallas(body, name="out_proj_bwd", grid=(t // tm,),
                   in_specs=[row, pl.BlockSpec((d, d), lambda i: (0, 0))], out_specs=row,
                   out_shape=SDS((t, d), BF16), args=(dy1, w_out), sem=("parallel",), ride=ride)


BWD_BLOCKS = 2


def _mixer_bwd(z, dmerged, attn, sinks, conv_w, d, ride=None):
    t, zw = z.shape
    kvw2 = zw - 6 * d
    tq = BWD_BLOCKS * BLOCK
    steps = t // tq
    halo = tq // SUBLANES_BF16
    last_halo = t // SUBLANES_BF16 - 1
    scale = HEAD_DIM ** -0.5
    seg = _segments(d, kvw2)

    def body(z_ref, kvp_ref, prev_ref, next_ref, dm_ref, dmn_ref, attn_ref, sinks_ref, cw_ref,
             dz_ref, dkv_ref, db_ref, dbkv_ref, dcw_ref, dsk_ref, carry_ref):
        n = pl.program_id(0)

        @pl.when(n == 0)
        def _():
            carry_ref[...] = jnp.zeros_like(carry_ref)
            db_ref[...] = jnp.zeros_like(db_ref)
            dbkv_ref[...] = jnp.zeros_like(dbkv_ref)
            dcw_ref[...] = jnp.zeros_like(dcw_ref)
            dsk_ref[...] = jnp.zeros_like(dsk_ref)

        def one_block(b, pending):
            rows = slice(b * BLOCK, (b + 1) * BLOCK)
            before = slice((b - 1) * BLOCK, b * BLOCK)
            dm = dm_ref[rows, :].astype(F32)
            sa = jax.nn.sigmoid(z_ref[rows, seg["ga"]].astype(F32))
            dga = dm * attn_ref[rows, :].astype(F32) * sa * (1.0 - sa)
            dz_ref[rows, seg["ga"]] = dga.astype(BF16)
            db_ref[0:1, seg["ga"]] += jnp.sum(dga, axis=0, keepdims=True)
            dattn = (dm * sa).astype(BF16)

            kv_prev = kvp_ref[...] if b == 0 else z_ref[before, seg["kv"]]
            kv = jnp.concatenate([kv_prev, z_ref[rows, seg["kv"]]], axis=0)
            k_eff, v_eff = _kv_variants(kv, kvw2 // 2)
            band, col = _attn_masks()
            valid = band & ((n > 0) | (col >= BLOCK)) if b == 0 else band
            lane_lo = lax.broadcasted_iota(jnp.int32, (2 * BLOCK, LANES), 1) < HEAD_DIM
            sink_lane = lax.broadcasted_iota(jnp.int32, (1, LANES), 1)
            rowblk = lax.broadcasted_iota(jnp.int32, (4 * BLOCK, 1), 0) // BLOCK
            dk_acc = [jnp.zeros((2 * BLOCK, LANES), F32), jnp.zeros((2 * BLOCK, LANES), F32)]
            dv_acc = [jnp.zeros((2 * BLOCK, LANES), F32), jnp.zeros((2 * BLOCK, LANES), F32)]
            dsink = jnp.zeros((1, LANES), F32)
            for h in range(2):
                q4 = _stack_pairs(z_ref, h, rows)
                do4 = jnp.concatenate([dattn[:, (4 * h + j) * LANES:(4 * h + j + 1) * LANES] for j in range(4)],
                                      axis=0)
                dq4 = jnp.zeros((4 * BLOCK, LANES), F32)
                for e in range(2):
                    s = lax.dot_general(q4, k_eff[h][e], (((1,), (1,)), ((), ())), preferred_element_type=F32)
                    p, psink = _softmax_sink(s, valid, _sink_column(sinks_ref, h, e))
                    dp = lax.dot_general(do4, v_eff[h][e], (((1,), (1,)), ((), ())), preferred_element_type=F32)
                    delta = jnp.sum(p * dp, axis=-1, keepdims=True)
                    ds = (p * (dp - delta)).astype(BF16)
                    dq4 = dq4 + jnp.dot(ds, k_eff[h][e], preferred_element_type=F32)
                    dk = lax.dot_general(q4, ds, (((0,), (0,)), ((), ())), preferred_element_type=F32).T
                    dv = lax.dot_general(do4, p.astype(BF16), (((0,), (0,)), ((), ())), preferred_element_type=F32).T
                    keep = lane_lo if e == 0 else jnp.logical_not(lane_lo)
                    slot = 0 if e == h else 1
                    dk_acc[slot] = dk_acc[slot] + jnp.where(keep, dk, 0.0)
                    dv_acc[slot] = dv_acc[slot] + jnp.where(keep, dv, 0.0)
                    dsk = -(psink * delta)
                    for j in range(4):
                        tot = jnp.sum(jnp.where(rowblk == j, dsk, 0.0), axis=0, keepdims=True)
                        dsink = dsink + jnp.where(sink_lane == GROUP * h + 2 * j + e, tot, 0.0)
                for j in range(4):
                    cols = slice((4 * h + j) * LANES, (4 * h + j + 1) * LANES)
                    dqj = dq4[j * BLOCK:(j + 1) * BLOCK]
                    dz_ref[rows, cols] = dqj.astype(BF16)
                    db_ref[0:1, cols] += jnp.sum(dqj, axis=0, keepdims=True)
            dsk_ref[0:1, :] += dsink
            dkv_new = jnp.concatenate([(dk_acc[0] + pltpu.roll(dk_acc[1], HEAD_DIM, 1)) * scale,
                                       dv_acc[0] + pltpu.roll(dv_acc[1], HEAD_DIM, 1)], axis=1)
            done = pending + dkv_new[:BLOCK]
            dkv_ref[rows, :] = done.astype(BF16)
            dbkv_ref[0:1, :] += jnp.sum(done, axis=0, keepdims=True)

            cb = z_ref[rows, seg["cb"]].astype(F32)
            cc = z_ref[rows, seg["cc"]].astype(F32)
            cx = z_ref[rows, seg["cx"]].astype(F32)
            sg = jax.nn.sigmoid(z_ref[rows, seg["gc"]].astype(F32))
            p_in = cc * cx
            if b == 0:
                prev = jnp.where(n > 0, prev_ref[:, seg["cc"]].astype(F32) * prev_ref[:, seg["cx"]].astype(F32), 0.0)
            else:
                tail = slice(b * BLOCK - SUBLANES_BF16, b * BLOCK)
                prev = z_ref[tail, seg["cc"]].astype(F32) * z_ref[tail, seg["cx"]].astype(F32)
            p_m1 = _shift_down(p_in, 1, prev)
            p_m2 = _shift_down(p_in, 2, prev)
            w0, w1, w2 = cw_ref[0:1, :], cw_ref[1:2, :], cw_ref[2:3, :]
            cconv = w0 * p_m2 + w1 * p_m1 + w2 * p_in
            dconv = dm * sg
            dgc = dm * (cb * cconv) * sg * (1.0 - sg)
            dcb = dconv * cconv
            dcc_t = dconv * cb
            if b == BWD_BLOCKS - 1:
                nxt = jnp.where(n < steps - 1,
                                dmn_ref[...].astype(F32) * jax.nn.sigmoid(next_ref[:, seg["gc"]].astype(F32))
                                * next_ref[:, seg["cb"]].astype(F32), 0.0)
            else:
                head = slice((b + 1) * BLOCK, (b + 1) * BLOCK + SUBLANES_BF16)
                nxt = (dm_ref[head, :].astype(F32) * jax.nn.sigmoid(z_ref[head, seg["gc"]].astype(F32))
                       * z_ref[head, seg["cb"]].astype(F32))
            dpin = w2 * dcc_t + w1 * _shift_up(dcc_t, 1, nxt) + w0 * _shift_up(dcc_t, 2, nxt)
            for nm, val in (("cb", dcb), ("cc", dpin * cx), ("cx", dpin * cc), ("gc", dgc)):
                dz_ref[rows, seg[nm]] = val.astype(BF16)
                db_ref[0:1, seg[nm]] += jnp.sum(val, axis=0, keepdims=True)
            dcw_ref[0:1, :] += jnp.sum(dcc_t * p_m2, axis=0, keepdims=True)
            dcw_ref[1:2, :] += jnp.sum(dcc_t * p_m1, axis=0, keepdims=True)
            dcw_ref[2:3, :] += jnp.sum(dcc_t * p_in, axis=0, keepdims=True)
            return dkv_new[BLOCK:]

        @pl.when(n < steps)
        def _():
            pending = carry_ref[...]
            for b in range(BWD_BLOCKS):
                pending = one_block(b, pending)
            carry_ref[...] = pending

        @pl.when(n == steps)
        def _():
            done = carry_ref[...]
            dkv_ref[:BLOCK, :] = done.astype(BF16)
            dkv_ref[BLOCK:, :] = jnp.zeros((tq - BLOCK, kvw2), BF16)
            dbkv_ref[0:1, :] += jnp.sum(done, axis=0, keepdims=True)

    def cur(n):
        return jnp.minimum(n, steps - 1)

    def after(n):
        return jnp.minimum((cur(n) + 1) * halo, last_halo)

    blk = pl.BlockSpec((tq, d), lambda n: (cur(n), 0))
    return _pallas(
        body, name="mixer_bwd", grid=(steps + 1,), ride=ride, sem=("arbitrary",),
        args=(z, z, z, z, dmerged, dmerged, attn, sinks, conv_w),
        in_specs=[pl.BlockSpec((tq, zw), lambda n: (cur(n), 0)),
                  pl.BlockSpec((BLOCK, kvw2), lambda n: (jnp.maximum(cur(n) * BWD_BLOCKS - 1, 0), d // kvw2)),
                  pl.BlockSpec((SUBLANES_BF16, zw), lambda n: (jnp.maximum(cur(n) * halo - 1, 0), 0)),
                  pl.BlockSpec((SUBLANES_BF16, zw), lambda n: (after(n), 0)),
                  blk,
                  pl.BlockSpec((SUBLANES_BF16, d), lambda n: (after(n), 0)),
                  blk, SMEM_SPEC, pl.BlockSpec((3, d), lambda n: (0, 0))],
        out_specs=[pl.BlockSpec((tq, zw), lambda n: (cur(n), 0)),
                   pl.BlockSpec((tq, kvw2), lambda n: (n, 0)),
                   pl.BlockSpec((8, zw), lambda n: (0, 0)), pl.BlockSpec((8, kvw2), lambda n: (0, 0)),
                   pl.BlockSpec((8, d), lambda n: (0, 0)), pl.BlockSpec((8, LANES), lambda n: (0, 0))],
        out_shape=[SDS((t, zw), BF16), SDS((t + tq, kvw2), BF16), SDS((8, zw), F32), SDS((8, kvw2), F32),
                   SDS((8, d), F32), SDS((8, LANES), F32)],
        scratch=[pltpu.VMEM((BLOCK, kvw2), F32)])


def _wgrad_in(dz, dkv, h1, tk, ride=None):
    t, zw = dz.shape
    d = h1.shape[1]
    kvw2 = dkv.shape[1]
    blk = d + kvw2
    assert zw % blk == 0
    tn = (((0,), (0,)), ((), ()))

    def body(a_ref, akv_ref, h_ref, o_ref, o16_ref):
        n, k = pl.program_id(0), pl.program_id(1)

        @pl.when(k == 0)
        def _():
            o_ref[...] = jnp.zeros_like(o_ref)

        @pl.when(n == 0)
        def _():
            o_ref[:d, :] += lax.dot_general(a_ref[:, :d], h_ref[...], tn, preferred_element_type=F32)
            o_ref[d:, :] += lax.dot_general(akv_ref[...], h_ref[...], tn, preferred_element_type=F32)

        @pl.when(n > 0)
        def _():
            o_ref[...] += lax.dot_general(a_ref[...], h_ref[...], tn, preferred_element_type=F32)

        @pl.when(k == t // tk - 1)
        def _():
            o16_ref[...] = o_ref[...].astype(BF16)

    out_spec = pl.BlockSpec((blk, d), lambda n, k: (n, 0))
    return _pallas(
        body, name="wgrad_in", grid=(zw // blk, t // tk),
        in_specs=[pl.BlockSpec((tk, blk), lambda n, k: (k, n)), pl.BlockSpec((tk, kvw2), lambda n, k: (k, 0)),
                  pl.BlockSpec((tk, d), lambda n, k: (k, 0))],
        out_specs=[out_spec, out_spec], out_shape=[SDS((zw, d), F32), SDS((zw, d), BF16)],
        args=(dz, dkv, h1), sem=("parallel", "arbitrary"), ride=ride)


def _in_proj_bwd(dz, dkv, wt, x, dx1, g_mix, sc1, tm, ride=None):
    t, d = x.shape
    zw = dz.shape[1]
    kvw2 = dkv.shape[1]
    rest = d + kvw2

    def body(a_ref, akv_ref, w_ref, x_ref, dx1_ref, g_ref, sc_ref, gx_ref, st_ref):
        @pl.when(pl.program_id(0) == 0)
        def _():
            st_ref[...] = jnp.zeros_like(st_ref)

        dh = (jnp.dot(a_ref[:, :d], w_ref[:d, :], preferred_element_type=F32)
              + jnp.dot(akv_ref[...], w_ref[d:rest, :], preferred_element_type=F32)
              + jnp.dot(a_ref[:, rest:], w_ref[rest:, :], preferred_element_type=F32))
        xx = x_ref[...]
        r = _rms(xx)
        xn = xx * r
        g = g_ref[...]
        dn = dh * (1.0 + sc_ref[...])
        u = dn * g
        gx_ref[...] = dx1_ref[...] + r * (u - xn * jnp.mean(u * xn, axis=-1, keepdims=True))
        st_ref[0:1, :] += jnp.sum(dh, axis=0, keepdims=True)
        st_ref[1:2, :] += jnp.sum(dh * (xn * g), axis=0, keepdims=True)
        st_ref[2:3, :] += jnp.sum(dn * xn, axis=0, keepdims=True)

    row = pl.BlockSpec((tm, d), lambda i: (i, 0))
    vecs, vec_specs = zip(*[_vec(v, d) for v in (g_mix, sc1)])
    return _pallas(
        body, name="in_proj_bwd", grid=(t // tm,),
        in_specs=[pl.BlockSpec((tm, zw), lambda i: (i, 0)), pl.BlockSpec((tm, kvw2), lambda i: (i, 0)),
                  pl.BlockSpec((zw, d), lambda i: (0, 0), pipeline_mode=pl.Buffered(1)),
                  row, row, *vec_specs],
        out_specs=[row, pl.BlockSpec((8, d), lambda i: (0, 0))],
        out_shape=[SDS((t, d), F32), SDS((8, d), F32)],
        args=(dz, dkv, wt, x, dx1, *vecs), sem=("arbitrary",), ride=ride)


def _to_lanes(v, rows=None):
    flat = v.reshape(-1)
    need = -(-flat.shape[0] // LANES)
    need = -(-need // 8) * 8 if rows is None else rows
    return jnp.pad(flat, (0, need * LANES - flat.shape[0])).reshape(need, LANES)


def kernel(x, c, w_ada, b_ada, g_mix, w_in, b_in, sinks, conv_w, w_out, g_ffn, w_ffn_in, w_ffn_out, g_final, loss_target, m_w_ada, m_b_ada, m_g_mix, m_w_in, m_b_in, m_sinks, m_conv_w, m_w_out, m_g_ffn, m_w_ffn_in, m_w_ffn_out, m_g_final, v_w_ada, v_b_ada, v_g_mix, v_w_in, v_b_in, v_sinks, v_conv_w, v_w_out, v_g_ffn, v_w_ffn_in, v_w_ffn_out, v_g_final):
    xs, tgt = x[0], loss_target[0]
    t, d = xs.shape
    zw = w_in.shape[2] * N_CHIP
    kvw2 = zw - 6 * d
    ff = w_ffn_out.shape[1] * N_CHIP
    n_mod = w_ada.shape[2] * N_CHIP // d
    mod_sh = w_ada.shape[2]
    cw_sh = conv_w.shape[2]
    assert d % (8 * LANES) == 0 and kvw2 == 2 * LANES and t % 512 == 0 and n_mod == 6
    xi, yi, ci = _mesh_pos()
    j_me = 2 * xi + yi
    pos = jnp.stack([ci, j_me]).astype(jnp.int32)
    tm = 512

    w_in_t, m_w_in_t, v_w_in_t = w_in[0].T, m_w_in[0].T, v_w_in[0].T
    assert d == 8 * LANES
    pack1 = jnp.concatenate([c.reshape(d // LANES, LANES), conv_w[0].reshape(-1, LANES)], axis=0)
    pack1 = jnp.pad(pack1, ((0, 16 - pack1.shape[0]), (0, 0)))
    b_ada_sh = lax.dynamic_slice(b_ada, (0, j_me * mod_sh), (1, mod_sh))
    g1, mod, w_in_g, later = _startup(pack1, w_ada[0], b_ada_sh, _cast_into_block(pos, w_in_t, "cast_w_in"),
                                      [w_out[0], w_ffn_in[0], w_ffn_out[0]])
    c_all = g1[:, :d // LANES, :].reshape(N_DEV, d)
    cw_rows = 3 * cw_sh // LANES
    conv_w_full = jnp.concatenate(
        [g1[2 * j, d // LANES:d // LANES + cw_rows, :].reshape(3, cw_sh) for j in range(N_CHIP)], axis=1)
    sh1, sc1, ga1, sh2, sc2, ga2 = [(mod, k) for k in range(6)]
    w_in_tf = w_in_g.reshape(zw, d)

    (z, h1), later = _in_proj(xs, g_mix, sc1, sh1, w_in_tf, b_in, min(t, 2048), zw // 5, ride=_x_gather_ici(later))
    (attn, merged), later = _mixer_fwd(z, sinks, conv_w_full, d, ride=_x_gather_d2d(later))
    w_out_f = later[0].reshape(d, d)
    w_ffn_in_f = later[1]
    w_ffn_out_f = later[2].reshape(ff, d)
    tml = min(t, 1024)
    y1, x1, h2 = _out_proj_fwd(merged, w_out_f, xs, ga1, g_ffn, sc2, sh2, tml)
    gu, act = _ffn_in_fwd(h2, w_ffn_in_f, ff, tml, ff // 2)
    dx2, dy2, st_loss = _ffn_out_loss(act, w_ffn_out_f, x1, tgt, ga2, g_final.reshape(1, d), tml)

    dgu = _ffn_out_bwd(dy2, w_ffn_out_f, gu, tml, ff // 2)
    tk = min(t, 2048)
    dw_ffn_out, _ = _wgrad(
        act, dy2, pl.BlockSpec((tk, ff // 2), lambda m, k: (k, m)), pl.BlockSpec((tk, d), lambda m, k: (k, 0)),
        pl.BlockSpec((ff // 2, d), lambda m, k: (m, 0)), SDS((ff, d), F32), (2, t // tk), "wgrad_ffn_out")
    dx1, dy1, st_ffn = _ffn_in_bwd(dgu, w_ffn_in_f, x1, dx2, y1, g_ffn, sc2, ga1, tm)
    dw_ffn_in, _ = _wgrad(
        h2, dgu, pl.BlockSpec((tk, d), lambda n, k: (k, 0)),
        pl.BlockSpec((None, tk, ff // 2), lambda n, k: (n // 2, k, n % 2)),
        pl.BlockSpec((None, d, ff // 2), lambda n, k: (n, 0, 0)), SDS((N_CHIP, d, ff // 2), F32),
        (N_CHIP, t // tk), "wgrad_ffn_in")
    dw_out, _ = _wgrad(
        merged, dy1, pl.BlockSpec((tk, d), lambda m, k: (k, 0)), pl.BlockSpec((tk, d), lambda m, k: (k, 0)),
        pl.BlockSpec((d, d), lambda m, k: (0, 0)), SDS((d, d), F32), (1, t // tk), "wgrad_out")

    early = [[g.reshape(N_CHIP, -1, g.shape[-1]) for g in pair] for pair in (dw_out, dw_ffn_in, dw_ffn_out)]
    early_names = ["w_out", "w_ffn_in", "w_ffn_out"]
    dmerged, _ = _out_proj_bwd(dy1, w_out_f, tml)
    (dz, dkv_shifted, db_z, db_kv, dcw, dsk), terms = _mixer_bwd(
        z, dmerged, attn, sinks, conv_w_full, d, ride=_x_reduce([e[0] for e in early], [e[1] for e in early]))
    dkv = dkv_shifted[BLOCK:BLOCK + t]
    fulls = [_sum_terms(pos, e[0], s, r, "sum_terms_" + nm)
             for e, s, r, nm in zip(early, terms[:3], terms[3:], early_names)]
    dw_in_t, (g_w_out, g_w_ffn_in, g_w_ffn_out) = _wgrad_in(dz, dkv, h1, tk, ride=_x_pair_exchange(fulls))
    dw_in_t = [g.reshape(N_CHIP, zw // N_CHIP, d) for g in dw_in_t]

    (grad_x, st_in), (from_sib, from_far) = _in_proj_bwd(dz, dkv, w_in_tf, xs, dx1, g_mix, sc1, tm,
                                                         ride=_x_reduce([dw_in_t[0]], [dw_in_t[1]]))
    g_w_in_half = _sum_terms(pos, dw_in_t[0], from_sib, from_far, "sum_terms_w_in")

    dmod = jnp.concatenate([st_in[0:1], st_in[1:2], st_ffn[3:4], st_ffn[0:1], st_ffn[1:2], st_loss[0:1]], axis=1)
    db_in = jnp.concatenate([db_z[0:1, :d], db_kv[0:1], db_z[0:1, d + kvw2:]], axis=1)
    seg = [dmod, st_in[2:3], db_in, dsk[0:1], dcw[0:3].reshape(1, 3 * d), st_ffn[2:3], st_loss[1:2],
           st_loss[2:3, :LANES]]
    sizes = [s.shape[1] for s in seg]
    pack2 = _to_lanes(jnp.concatenate(seg, axis=1))
    packs, g_w_in_t = _tail_exchange(pack2, g_w_in_half)
    tot = _pack_sum(packs).reshape(-1)
    offs = [sum(sizes[:k]) for k in range(len(sizes))]
    gb_ada, gg_mix, gb_in, gsinks, gcw, gg_ffn, gg_final, loss_v = [tot[o:o + s] for o, s in zip(offs, sizes)]
    loss = loss_v[0]
    gsinks = gsinks[:sinks.shape[1]]
    gcw_sh = lax.dynamic_slice(gcw.reshape(3, d), (0, j_me * cw_sh), (3, cw_sh))

    dmod_all = packs[:, :n_mod * d // LANES, :].reshape(N_DEV, n_mod * d)
    g_w_ada = _ada_wgrad(c_all, lax.dynamic_slice(dmod_all, (0, j_me * mod_sh), (N_DEV, mod_sh)))

    out_g, out_d, out_m, out_v = {}, {}, {}, {}
    big = {"w_ada": (w_ada[0], g_w_ada, m_w_ada[0], v_w_ada[0]),
           "w_out": (w_out[0], g_w_out, m_w_out[0], v_w_out[0]),
           "w_ffn_in": (w_ffn_in[0], g_w_ffn_in, m_w_ffn_in[0], v_w_ffn_in[0]),
           "w_ffn_out": (w_ffn_out[0], g_w_ffn_out, m_w_ffn_out[0], v_w_ffn_out[0])}
    for nm, (w, g, m, v) in big.items():
        out_g[nm], out_d[nm], out_m[nm], out_v[nm] = [o[None] for o in _adamw(w, g, m, v, "adamw_" + nm)]
    out_g["w_in"], out_d["w_in"], out_m["w_in"], out_v["w_in"] = [
        o.T[None] for o in _adamw(w_in_t, g_w_in_t, m_w_in_t, v_w_in_t, "adamw_w_in")]
    small = {"b_ada": (b_ada, gb_ada, m_b_ada, v_b_ada), "g_mix": (g_mix, gg_mix, m_g_mix, v_g_mix),
             "b_in": (b_in, gb_in, m_b_in, v_b_in), "sinks": (sinks, gsinks, m_sinks, v_sinks),
             "conv_w": (conv_w, gcw_sh, m_conv_w, v_conv_w), "g_ffn": (g_ffn, gg_ffn, m_g_ffn, v_g_ffn),
             "g_final": (g_final, gg_final, m_g_final, v_g_final)}
    def two_d(a):
        return a.reshape(-1, a.shape[-1])

    s_out = _adamw_small([tuple(two_d(a.reshape(w.shape)) for a in (w, g, m, v)) for w, g, m, v in small.values()])
    for (nm, (w, g, _, _)), res in zip(small.items(), s_out):
        out_g[nm] = g.reshape(w.shape)
        out_d[nm], out_m[nm], out_v[nm] = [o.reshape(w.shape) for o in res]

    order = ["w_ada", "b_ada", "g_mix", "w_in", "b_in", "sinks", "conv_w", "w_out", "g_ffn", "w_ffn_in", "w_ffn_out",
             "g_final"]
    return (loss, grad_x[None], *[out_g[k] for k in order], *[out_d[k] for k in order],
            *[out_m[k] for k in order], *[out_v[k] for k in order])
```

```python
import functools

import jax
import jax.numpy as jnp
from jax import lax
from jax.experimental import pallas as pl
from jax.experimental.pallas import tpu as pltpu

F32 = jnp.float32
BF16 = jnp.bfloat16
EPS = 1e-6
HEAD_DIM = 64
GROUP = 8
BLOCK = 128
LANES = 128
SUBLANES_BF16 = 16
N_DEV = 8
N_CHIP = 4
VMEM_LIMIT = 56 * 1024 * 1024
MESH = pl.DeviceIdType.MESH

ADAM_LR = 0.001
ADAM_B1 = 0.9
ADAM_B2 = 0.999
ADAM_EPS = 1e-08
ADAM_WD = 0.01
ADAM_STEP = 10

SDS = jax.ShapeDtypeStruct
ANY = pl.BlockSpec(memory_space=pl.ANY)
VMEM_SPEC = pl.BlockSpec(memory_space=pltpu.VMEM)
SMEM_SPEC = pl.BlockSpec(memory_space=pltpu.SMEM)


def _params(*sem):
    return pltpu.CompilerParams(dimension_semantics=sem, vmem_limit_bytes=VMEM_LIMIT)


def _vec(v, d):
    arr, k = v if isinstance(v, tuple) else (v, 0)
    return arr, pl.BlockSpec((1, d), lambda *_: (0, k))


def _mesh_pos():
    return lax.axis_index("x"), lax.axis_index("y"), lax.axis_index("c")


def _row_tile(rows, cols, itemsize=4, budget=1 << 20, mult=8):
    best = None
    for t in range(mult, rows + 1, mult):
        if rows % t == 0 and t * cols * itemsize <= budget:
            best = t
    if best is None:
        best = rows
    return best


def _gather_all(v_ref, out_ref, send_sems, recv_sems, local_sem):
    x, y, c = _mesh_pos()
    me = 4 * x + 2 * y + c
    mine = pltpu.make_async_copy(v_ref, out_ref.at[me], local_sem)
    mine.start()
    peers = []
    for k in range(1, N_DEV):
        px = 1 - x if k & 4 else x
        py = 1 - y if k & 2 else y
        pc = 1 - c if k & 1 else c
        peers.append((px, py, pc))

    def copy(k, block):
        return pltpu.make_async_remote_copy(
            src_ref=v_ref, dst_ref=out_ref.at[block], send_sem=send_sems.at[k], recv_sem=recv_sems.at[k],
            device_id=peers[k], device_id_type=MESH)

    sends = [copy(k, me) for k in range(N_DEV - 1)]
    for cp in sends:
        cp.start()
    for k, (px, py, pc) in enumerate(peers):
        copy(k, 4 * px + 2 * py + pc).wait_recv()
    for cp in sends:
        cp.wait_send()
    mine.wait()


def _small_sems():
    return [pltpu.SemaphoreType.DMA((N_DEV - 1,)), pltpu.SemaphoreType.DMA((N_DEV - 1,)), pltpu.SemaphoreType.DMA]


def _tail_exchange(pack, full):
    def body(pack_ref, full_unused, packs_ref, full_ref, s1, r1, l1, send_sem, recv_sem):
        del full_unused
        x, y, c = _mesh_pos()
        half = full_ref.shape[0] // 2
        rows = pl.ds(pl.multiple_of(c * half, 8), half)
        swap = pltpu.make_async_remote_copy(
            src_ref=full_ref.at[rows], dst_ref=full_ref.at[rows], send_sem=send_sem, recv_sem=recv_sem,
            device_id=(x, y, 1 - c), device_id_type=MESH)
        swap.start()
        _gather_all(pack_ref, packs_ref, s1, r1, l1)
        swap.wait()

    return pl.pallas_call(
        body, name="tail_exchange", out_shape=[SDS((N_DEV,) + pack.shape, pack.dtype), SDS(full.shape, full.dtype)],
        in_specs=[VMEM_SPEC, ANY], out_specs=[VMEM_SPEC, ANY], input_output_aliases={1: 1},
        scratch_shapes=_small_sems() + [pltpu.SemaphoreType.DMA, pltpu.SemaphoreType.DMA])(pack, full)


def _other_chips(x, y):
    return [(1 - x, y), (x, 1 - y), (1 - x, 1 - y)]


def _startup(pack, w_ada_sh, b_ada_sh, w_buf, later):
    d, n = w_ada_sh.shape
    kc = d // LANES
    n_l = len(later)
    chunk_rows = [_row_tile(a.shape[0], a.shape[1], budget=3 << 19, mult=SUBLANES_BF16) for a in later]

    def body(*refs):
        pack_ref, wa_hbm, ba_ref, w_in_unused = refs[:4]
        later_src = refs[4:4 + n_l]
        packs_ref, mine_ref, w_ref = refs[4 + n_l:7 + n_l]
        later_dst = refs[7 + n_l:7 + 2 * n_l]
        wa_scr, mod_scr, mod_ref = refs[7 + 2 * n_l:10 + 2 * n_l]
        f32_bufs = refs[10 + 2 * n_l:10 + 3 * n_l]
        bf16_bufs = refs[10 + 3 * n_l:10 + 4 * n_l]
        (s1, r1, l1, s2, r2, l2, send_sems, recv_sems, fsend_sems, frecv_sems, relay_send, relay_recv, wa_sem,
         cast_sems) = refs[10 + 4 * n_l:]
        del w_in_unused
        x, y, c = _mesh_pos()
        j_me = 2 * x + y
        chips = _other_chips(x, y)
        half = w_ref.shape[1] // 2

        def rows_of(which):
            return pl.ds(pl.multiple_of(which * half, SUBLANES_BF16), half)

        def copy(p, block, rows, over_ici):
            sems = (send_sems, recv_sems) if over_ici else (fsend_sems, frecv_sems)
            return pltpu.make_async_remote_copy(
                src_ref=w_ref.at[block, rows], dst_ref=w_ref.at[block, rows], send_sem=sems[0].at[p],
                recv_sem=sems[1].at[p], device_id=(*chips[p], c) if over_ici else (x, y, 1 - c), device_id_type=MESH)

        def block_of(p):
            return 2 * chips[p][0] + chips[p][1]

        def relay(q, block):
            rows = pl.ds(pl.multiple_of(c * half + q * (half // 2), SUBLANES_BF16), half // 2)
            return pltpu.make_async_remote_copy(
                src_ref=w_ref.at[block, rows], dst_ref=w_ref.at[block, rows], send_sem=relay_send.at[q],
                recv_sem=relay_recv.at[q], device_id=(*chips[1 - q], c), device_id_type=MESH)

        load_wa = pltpu.make_async_copy(wa_hbm, wa_scr, wa_sem)
        load_wa.start()
        _gather_all(pack_ref, packs_ref, s1, r1, l1)
        sends = [copy(p, j_me, rows_of(c), True) for p in range(2)]
        for cp in sends:
            cp.start()
        load_wa.wait()
        acc = jnp.zeros((N_DEV, n), F32)
        for k in range(kc):
            ck = packs_ref[:, k, :]
            sk = (ck * jax.nn.sigmoid(ck)).astype(BF16)
            acc = acc + jnp.dot(sk, wa_scr[k * LANES:(k + 1) * LANES, :].astype(BF16), preferred_element_type=F32)
        mod_scr[...] = acc + ba_ref[...]
        _gather_all(mod_scr, mod_ref, s2, r2, l2)
        for j in range(N_CHIP):
            mine_ref[:, j * n:(j + 1) * n] = mod_ref[2 * j, pl.ds(4 * x + 2 * y + c, 1), :]
        passed = []
        for q in range(2):
            copy(q, block_of(q), rows_of(c), True).wait_recv()
            for cp in (relay(q, block_of(q)), copy(q, block_of(q), rows_of(c), False)):
                cp.start()
                passed.append(cp)
        for src, dst, fbuf, bbuf in zip(later_src, later_dst, f32_bufs, bf16_bufs):
            cr = fbuf.shape[0]
            for k in range(src.shape[0] // cr):
                rows = pl.ds(k * cr, cr)
                cin = pltpu.make_async_copy(src.at[rows], fbuf, cast_sems.at[0])
                cin.start()
                cin.wait()
                bbuf[...] = fbuf[...].astype(BF16)
                cout = pltpu.make_async_copy(bbuf, dst.at[j_me, rows], cast_sems.at[1])
                cout.start()
                cout.wait()
        for q in range(2):
            relay(q, block_of(2)).wait_recv()
        fw = copy(2, block_of(2), rows_of(c), False)
        fw.start()
        for p in range(3):
            copy(p, block_of(p), rows_of(1 - c), False).wait_recv()
        for cp in sends + passed + [fw]:
            cp.wait_send()

    res = pl.pallas_call(
        body, name="startup",
        out_shape=[SDS((N_DEV,) + pack.shape, F32), SDS((1, N_CHIP * n), F32), SDS(w_buf.shape, w_buf.dtype)]
        + [SDS((N_CHIP,) + a.shape, BF16) for a in later],
        in_specs=[VMEM_SPEC, ANY, VMEM_SPEC, ANY] + [ANY] * n_l, out_specs=[VMEM_SPEC, VMEM_SPEC, ANY] + [ANY] * n_l,
        input_output_aliases={3: 2},
        scratch_shapes=[pltpu.VMEM((d, n), F32), pltpu.VMEM((N_DEV, n), F32), pltpu.VMEM((N_DEV, N_DEV, n), F32)]
        + [pltpu.VMEM((cr, a.shape[1]), F32) for cr, a in zip(chunk_rows, later)]
        + [pltpu.VMEM((cr, a.shape[1]), BF16) for cr, a in zip(chunk_rows, later)]
        + _small_sems() + _small_sems()
        + [pltpu.SemaphoreType.DMA((3,))] * 4 + [pltpu.SemaphoreType.DMA((2,))] * 2 + [pltpu.SemaphoreType.DMA]
        + [pltpu.SemaphoreType.DMA((2,))],
        compiler_params=pltpu.CompilerParams(vmem_limit_bytes=VMEM_LIMIT),
    )(pack, w_ada_sh, b_ada_sh, w_buf, *later)
    return res[0], res[1], res[2], list(res[3:])


class _Exchange:
    def __init__(self, operands, out_shape, in_place, n_sems, copies):
        self.operands, self.out_shape, self.in_place, self.n_sems, self.copies = (
            list(operands), list(out_shape), in_place, n_sems, copies)

    def sems(self):
        return [pltpu.SemaphoreType.DMA((self.n_sems,)), pltpu.SemaphoreType.DMA((self.n_sems,))]


def _x_gather_ici(bufs):
    def copies(ins, outs, send_sems, recv_sems):
        x, y, c = _mesh_pos()
        chips = _other_chips(x, y)
        out = []
        for w in range(len(outs)):
            half = outs[w].shape[1] // 2
            rows = pl.ds(pl.multiple_of(c * half, SUBLANES_BF16), half)
            for p in range(3):
                out.append(pltpu.make_async_remote_copy(
                    src_ref=outs[w].at[2 * x + y, rows], dst_ref=outs[w].at[2 * x + y, rows],
                    send_sem=send_sems.at[w * 3 + p], recv_sem=recv_sems.at[w * 3 + p],
                    device_id=(*chips[p], c), device_id_type=MESH))
        return out

    return _Exchange(bufs, [SDS(b.shape, b.dtype) for b in bufs], True, 3 * len(bufs), copies)


def _x_gather_d2d(bufs):
    def copies(ins, outs, send_sems, recv_sems):
        x, y, c = _mesh_pos()
        chips = _other_chips(x, y)
        out = []
        for w in range(len(outs)):
            half = outs[w].shape[1] // 2
            rows = pl.ds(pl.multiple_of(c * half, SUBLANES_BF16), half)
            for p in range(3):
                block = 2 * chips[p][0] + chips[p][1]
                out.append(pltpu.make_async_remote_copy(
                    src_ref=outs[w].at[block, rows], dst_ref=outs[w].at[block, rows],
                    send_sem=send_sems.at[w * 3 + p], recv_sem=recv_sems.at[w * 3 + p],
                    device_id=(x, y, 1 - c), device_id_type=MESH))
        return out

    return _Exchange(bufs, [SDS(b.shape, b.dtype) for b in bufs], True, 3 * len(bufs), copies)


N_REMOTE = 6


def _x_reduce(grads32, grads16):
    n_w = len(grads32)

    def copies(ins, outs, send_sems, recv_sems):
        g32, g16 = ins[:n_w], ins[n_w:]
        from_sib, from_far = outs[:n_w], outs[n_w:]
        x, y, c = _mesh_pos()
        chips = _other_chips(x, y)
        out = []
        for w in range(n_w):
            half = g32[w].shape[1] // 2
            k0 = w * (N_REMOTE + 1)
            out.append(pltpu.make_async_remote_copy(
                src_ref=g32[w].at[2 * x + y, pl.ds(pl.multiple_of((1 - c) * half, SUBLANES_BF16), half), :],
                dst_ref=from_sib[w], send_sem=send_sems.at[k0], recv_sem=recv_sems.at[k0],
                device_id=(x, y, 1 - c), device_id_type=MESH))
            for p in range(3):
                for f in range(2):
                    tc = c if f == 0 else 1 - c
                    k = 2 * p + f
                    out.append(pltpu.make_async_remote_copy(
                        src_ref=g16[w].at[2 * chips[p][0] + chips[p][1],
                                          pl.ds(pl.multiple_of(tc * half, SUBLANES_BF16), half), :],
                        dst_ref=from_far[w].at[k], send_sem=send_sems.at[k0 + 1 + k], recv_sem=recv_sems.at[k0 + 1 + k],
                        device_id=(*chips[p], tc), device_id_type=MESH))
        return out

    shapes = ([SDS((g.shape[1] // 2, g.shape[2]), g.dtype) for g in grads32]
              + [SDS((N_REMOTE, g.shape[1] // 2, g.shape[2]), g.dtype) for g in grads16])
    return _Exchange(list(grads32) + list(grads16), shapes, False, (N_REMOTE + 1) * n_w, copies)


def _x_pair_exchange(fulls):
    def copies(ins, outs, send_sems, recv_sems):
        x, y, c = _mesh_pos()
        out = []
        for w in range(len(outs)):
            half = outs[w].shape[0] // 2
            rows = pl.ds(pl.multiple_of(c * half, 8), half)
            out.append(pltpu.make_async_remote_copy(
                src_ref=outs[w].at[rows], dst_ref=outs[w].at[rows], send_sem=send_sems.at[w],
                recv_sem=recv_sems.at[w], device_id=(x, y, 1 - c), device_id_type=MESH))
        return out

    return _Exchange(fulls, [SDS(f.shape, f.dtype) for f in fulls], True, len(fulls), copies)


def _pallas(body, *, name, grid, in_specs, out_specs, out_shape, args, scratch=(), sem=None, ride=None):
    single = not isinstance(out_specs, (list, tuple))
    out_specs_l = [out_specs] if single else list(out_specs)
    out_shape_l = [out_shape] if single else list(out_shape)
    n_in, n_out, n_scr = len(in_specs), len(out_specs_l), len(scratch)
    if ride is None:
        res = pl.pallas_call(body, name=name, grid=grid, in_specs=list(in_specs), out_specs=out_specs,
                             out_shape=out_shape, scratch_shapes=list(scratch), compiler_params=_params(*sem))(*args)
        return res, None
    n_x, n_xo = len(ride.operands), len(ride.out_shape)

    def full_body(*refs):
        ins, x_ins = refs[:n_in], refs[n_in:n_in + n_x]
        outs = refs[n_in + n_x:n_in + n_x + n_out]
        x_outs = refs[n_in + n_x + n_out:n_in + n_x + n_out + n_xo]
        rest = refs[n_in + n_x + n_out + n_xo:]
        scr, (send_sems, recv_sems) = rest[:n_scr], rest[n_scr:]
        first = functools.reduce(jnp.logical_and, [pl.program_id(a) == 0 for a in range(len(grid))])
        last = functools.reduce(jnp.logical_and, [pl.program_id(a) == grid[a] - 1 for a in range(len(grid))])

        @pl.when(first)
        def _():
            for cp in ride.copies(x_ins, x_outs, send_sems, recv_sems):
                cp.start()

        body(*ins, *outs, *scr)

        @pl.when(last)
        def _():
            for cp in ride.copies(x_ins, x_outs, send_sems, recv_sems):
                cp.wait()

    res = pl.pallas_call(
        full_body, name=name, grid=grid, in_specs=list(in_specs) + [ANY] * n_x,
        out_specs=out_specs_l + [ANY] * n_xo, out_shape=out_shape_l + ride.out_shape,
        input_output_aliases={n_in + k: n_out + k for k in range(n_x)} if ride.in_place else {},
        scratch_shapes=list(scratch) + ride.sems(),
        compiler_params=_params(*(["arbitrary"] * len(grid))))(*args, *ride.operands)
    own = res[0] if single else list(res[:n_out])
    return own, list(res[n_out:])


def _cast_into_block(pos, w, name):
    rows, cols = w.shape
    tr = _row_tile(rows, cols, mult=SUBLANES_BF16)

    def body(pos_ref, w_ref, o_ref):
        del pos_ref
        o_ref[...] = w_ref[...].astype(BF16)

    return pl.pallas_call(
        body, name=name,
        grid_spec=pltpu.PrefetchScalarGridSpec(
            num_scalar_prefetch=1, grid=(rows // tr,),
            in_specs=[pl.BlockSpec((tr, cols), lambda i, pos_ref: (i, 0))],
            out_specs=pl.BlockSpec((None, tr, cols), lambda i, pos_ref: (pos_ref[1], i, 0))),
        out_shape=SDS((N_CHIP, rows, cols), BF16), compiler_params=_params("parallel"))(pos, w)


def _sum_terms(pos, grad, from_sib, from_far, name):
    _, rows, cols = grad.shape
    half = rows // 2
    tr = _row_tile(half, cols, budget=1 << 21, mult=SUBLANES_BF16)
    nblk = half // tr

    def body(pos_ref, g_ref, s_ref, r_ref, o_ref):
        del pos_ref
        acc = g_ref[...] + s_ref[...]
        for k in range(N_REMOTE):
            acc = acc + r_ref[k].astype(F32)
        o_ref[...] = acc

    return pl.pallas_call(
        body, name=name,
        grid_spec=pltpu.PrefetchScalarGridSpec(
            num_scalar_prefetch=1, grid=(nblk,),
            in_specs=[pl.BlockSpec((None, tr, cols), lambda i, pos_ref: (pos_ref[1], pos_ref[0] * nblk + i, 0)),
                      pl.BlockSpec((tr, cols), lambda i, pos_ref: (i, 0)),
                      pl.BlockSpec((N_REMOTE, tr, cols), lambda i, pos_ref: (0, i, 0))],
            out_specs=pl.BlockSpec((tr, cols), lambda i, pos_ref: (pos_ref[0] * nblk + i, 0))),
        out_shape=SDS((rows, cols), F32),
        compiler_params=_params("parallel"),
    )(pos, grad, from_sib, from_far)


def _adamw(w, g, m, v, name):
    rows, cols = w.shape
    tr = _row_tile(rows, cols, budget=1 << 21)

    def body(w_ref, g_ref, m_ref, v_ref, go_ref, d_ref, nm_ref, nv_ref):
        go_ref[...] = g_ref[...]
        _adamw_update(w_ref, g_ref, m_ref, v_ref, d_ref, nm_ref, nv_ref)

    spec = pl.BlockSpec((tr, cols), lambda i: (i, 0))
    return pl.pallas_call(body, name=name, grid=(rows // tr,), in_specs=[spec] * 4, out_specs=[spec] * 4,
                          out_shape=[SDS((rows, cols), F32)] * 4, compiler_params=_params("parallel"))(w, g, m, v)


def _adamw_update(w_ref, g_ref, m_ref, v_ref, d_ref, nm_ref, nv_ref):
    gg = g_ref[...]
    nm = ADAM_B1 * m_ref[...] + (1.0 - ADAM_B1) * gg
    nv = ADAM_B2 * v_ref[...] + (1.0 - ADAM_B2) * (gg * gg)
    m_hat = nm / (1.0 - ADAM_B1 ** ADAM_STEP)
    v_hat = nv / (1.0 - ADAM_B2 ** ADAM_STEP)
    d_ref[...] = -ADAM_LR * (m_hat / (jnp.sqrt(v_hat) + ADAM_EPS) + ADAM_WD * w_ref[...])
    nm_ref[...] = nm
    nv_ref[...] = nv


def _adamw_small(params):
    n_p = len(params)

    def body(*refs):
        ins, outs = refs[:4 * n_p], refs[4 * n_p:]
        for k in range(n_p):
            _adamw_update(*ins[4 * k:4 * k + 4], *outs[3 * k:3 * k + 3])

    flat = [a for tup in params for a in tup]
    res = pl.pallas_call(
        body, name="adamw_small", in_specs=[VMEM_SPEC] * (4 * n_p), out_specs=[VMEM_SPEC] * (3 * n_p),
        out_shape=[SDS(tup[0].shape, F32) for tup in params for _ in range(3)])(*flat)
    return [res[3 * k:3 * k + 3] for k in range(n_p)]


def _pack_sum(gathered):
    _, rows, cols = gathered.shape

    def body(g_ref, o_ref):
        acc = g_ref[0]
        for d in range(1, N_DEV):
            acc = acc + g_ref[d]
        o_ref[...] = acc

    return pl.pallas_call(body, name="pack_sum", in_specs=[VMEM_SPEC], out_specs=VMEM_SPEC,
                          out_shape=SDS((rows, cols), F32))(gathered)


def _ada_wgrad(c_all, dmod_sh):
    d = c_all.shape[1]
    n = dmod_sh.shape[1]
    tn = 512

    def body(c_ref, g_ref, o_ref):
        cc = c_ref[...]
        s = cc * jax.nn.sigmoid(cc)
        o_ref[...] = lax.dot_general(s, g_ref[...], (((0,), (0,)), ((), ())), preferred_element_type=F32,
                                     precision=lax.Precision.HIGHEST)

    return pl.pallas_call(
        body, name="ada_wgrad", grid=(n // tn,),
        in_specs=[pl.BlockSpec((N_DEV, d), lambda j: (0, 0)), pl.BlockSpec((N_DEV, tn), lambda j: (0, j))],
        out_specs=pl.BlockSpec((d, tn), lambda j: (0, j)),
        out_shape=SDS((d, n), F32), compiler_params=_params("parallel"))(c_all, dmod_sh)


def _rms(xf):
    return lax.rsqrt(jnp.mean(xf * xf, axis=-1, keepdims=True) + EPS)


def _in_proj(x, g, sc, sh, wt, b, tm, tn, ride=None):
    t, d = x.shape
    n = wt.shape[0]

    def body(x_ref, g_ref, sc_ref, sh_ref, w_ref, b_ref, z_ref, h_ref):
        @pl.when(pl.program_id(1) == 0)
        def _():
            xf = x_ref[...]
            h_ref[...] = ((xf * _rms(xf) * g_ref[...]) * (1.0 + sc_ref[...]) + sh_ref[...]).astype(BF16)

        acc = lax.dot_general(h_ref[...], w_ref[...], (((1,), (1,)), ((), ())), preferred_element_type=F32)
        z_ref[...] = (acc + b_ref[...]).astype(BF16)

    row = pl.BlockSpec((tm, d), lambda i, j: (i, 0))
    vecs, vec_specs = zip(*[_vec(v, d) for v in (g, sc, sh)])
    return _pallas(
        body, name="in_proj", grid=(t // tm, n // tn),
        in_specs=[row, *vec_specs, pl.BlockSpec((tn, d), lambda i, j: (j, 0)),
                  pl.BlockSpec((1, tn), lambda i, j: (0, j))],
        out_specs=[pl.BlockSpec((tm, tn), lambda i, j: (i, j)), row],
        out_shape=[SDS((t, n), BF16), SDS((t, d), BF16)], args=(x, *vecs, wt, b),
        sem=("parallel", "arbitrary"), ride=ride)


def _segments(d, kvw2):
    o = d + kvw2
    names = ("cb", "cc", "cx", "ga", "gc")
    seg = {nm: slice(o + k * d, o + (k + 1) * d) for k, nm in enumerate(names)}
    seg["q"], seg["kv"] = slice(0, d), slice(d, o)
    return seg


def _attn_masks():
    rows = 4 * BLOCK
    r = lax.broadcasted_iota(jnp.int32, (rows, 2 * BLOCK), 0) & (BLOCK - 1)
    col = lax.broadcasted_iota(jnp.int32, (rows, 2 * BLOCK), 1)
    return (col > r) & (col <= r + BLOCK), col


def _kv_variants(kv, n_kv_w):
    assert n_kv_w == LANES
    kb, vb = kv[:, :LANES] * (HEAD_DIM ** -0.5), kv[:, LANES:]
    kr, vr = pltpu.roll(kb, HEAD_DIM, 1), pltpu.roll(vb, HEAD_DIM, 1)
    lane = lax.broadcasted_iota(jnp.int32, kb.shape, 1)
    lo = lane < HEAD_DIM
    zero = jnp.zeros_like(kb)
    k_eff = [[None, None], [None, None]]
    v_eff = [[None, None], [None, None]]
    for h in range(2):
        for e in range(2):
            ksrc, vsrc = (kb, vb) if e == h else (kr, vr)
            keep = lo if e == 0 else jnp.logical_not(lo)
            k_eff[h][e] = jnp.where(keep, ksrc, zero)
            v_eff[h][e] = jnp.where(keep, vsrc, zero)
    return k_eff, v_eff


def _sink_column(sinks_ref, h, e):
    rowblk = lax.broadcasted_iota(jnp.int32, (4 * BLOCK, 1), 0) // BLOCK
    col = jnp.zeros((4 * BLOCK, 1), F32)
    for j in range(4):
        col = jnp.where(rowblk == j, sinks_ref[0, GROUP * h + 2 * j + e], col)
    return col


def _softmax_sink(s, valid, sink):
    s = jnp.where(valid, s, -jnp.inf)
    m = jnp.maximum(jnp.max(s, axis=-1, keepdims=True), sink)
    p = jnp.exp(s - m)
    psink = jnp.exp(sink - m)
    den = jnp.sum(p, axis=-1, keepdims=True) + psink
    inv = 1.0 / den
    return p * inv, psink * inv


def _shift_down(a, s, prev):
    rows = a.shape[0]
    out = pltpu.roll(a, s, 0)
    row = lax.broadcasted_iota(jnp.int32, a.shape, 0)
    for t in range(s):
        out = jnp.where(row == t, prev[SUBLANES_BF16 - s + t:SUBLANES_BF16 - s + t + 1, :], out)
    del rows
    return out


def _shift_up(a, s, nxt):
    rows = a.shape[0]
    out = pltpu.roll(a, rows - s, 0)
    row = lax.broadcasted_iota(jnp.int32, a.shape, 0)
    for t in range(s):
        out = jnp.where(row == rows - s + t, nxt[t:t + 1, :], out)
    return out


def _stack_pairs(ref, h, rows=slice(None)):
    return jnp.concatenate([ref[rows, (4 * h + j) * LANES:(4 * h + j + 1) * LANES] for j in range(4)], axis=0)


FWD_BLOCKS = 4


def _mixer_fwd(z, sinks, conv_w, d, ride=None):
    t, zw = z.shape
    kvw2 = zw - 6 * d
    tq = FWD_BLOCKS * BLOCK
    halo = tq // SUBLANES_BF16
    seg = _segments(d, kvw2)

    def body(z_ref, kvp_ref, prev_ref, sinks_ref, cw_ref, attn_ref, merged_ref):
        n = pl.program_id(0)
        band, col = _attn_masks()
        for b in range(FWD_BLOCKS):
            rows = slice(b * BLOCK, (b + 1) * BLOCK)
            before = slice((b - 1) * BLOCK, b * BLOCK)
            kv_prev = kvp_ref[...] if b == 0 else z_ref[before, seg["kv"]]
            kv = jnp.concatenate([kv_prev, z_ref[rows, seg["kv"]]], axis=0)
            k_eff, v_eff = _kv_variants(kv, kvw2 // 2)
            valid = band & ((n > 0) | (col >= BLOCK)) if b == 0 else band
            for h in range(2):
                q4 = _stack_pairs(z_ref, h, rows)
                o4 = jnp.zeros((4 * BLOCK, LANES), F32)
                for e in range(2):
                    s = lax.dot_general(q4, k_eff[h][e], (((1,), (1,)), ((), ())), preferred_element_type=F32)
                    p, _ = _softmax_sink(s, valid, _sink_column(sinks_ref, h, e))
                    o4 = o4 + jnp.dot(p.astype(BF16), v_eff[h][e], preferred_element_type=F32)
                for j in range(4):
                    attn_ref[rows, (4 * h + j) * LANES:(4 * h + j + 1) * LANES] = (
                        o4[j * BLOCK:(j + 1) * BLOCK].astype(BF16))
            cb = z_ref[rows, seg["cb"]].astype(F32)
            p_in = z_ref[rows, seg["cc"]].astype(F32) * z_ref[rows, seg["cx"]].astype(F32)
            if b == 0:
                prev = jnp.where(n > 0, prev_ref[:, seg["cc"]].astype(F32) * prev_ref[:, seg["cx"]].astype(F32), 0.0)
            else:
                tail = slice(b * BLOCK - SUBLANES_BF16, b * BLOCK)
                prev = z_ref[tail, seg["cc"]].astype(F32) * z_ref[tail, seg["cx"]].astype(F32)
            cconv = (cw_ref[0:1, :] * _shift_down(p_in, 2, prev) + cw_ref[1:2, :] * _shift_down(p_in, 1, prev)
                     + cw_ref[2:3, :] * p_in)
            sa = jax.nn.sigmoid(z_ref[rows, seg["ga"]].astype(F32))
            sg = jax.nn.sigmoid(z_ref[rows, seg["gc"]].astype(F32))
            merged_ref[rows, :] = (sa * attn_ref[rows, :].astype(F32) + sg * (cb * cconv)).astype(BF16)

    blk = pl.BlockSpec((tq, d), lambda n: (n, 0))
    return _pallas(
        body, name="mixer_fwd", grid=(t // tq,),
        in_specs=[pl.BlockSpec((tq, zw), lambda n: (n, 0)),
                  pl.BlockSpec((BLOCK, kvw2), lambda n: (jnp.maximum(n * FWD_BLOCKS - 1, 0), d // kvw2)),
                  pl.BlockSpec((SUBLANES_BF16, zw), lambda n: (jnp.maximum(n * halo - 1, 0), 0)),
                  SMEM_SPEC, pl.BlockSpec((3, d), lambda n: (0, 0))],
        out_specs=[blk, blk],
        out_shape=[SDS((t, d), BF16), SDS((t, d), BF16)],
        args=(z, z, z, sinks, conv_w), sem=("parallel",), ride=ride)


def _out_proj_fwd(merged, w_out, x, ga1, g_ffn, sc2, sh2, tm):
    t, d = x.shape

    def body(m_ref, w_ref, x_ref, ga_ref, g_ref, sc_ref, sh_ref, y_ref, x1_ref, h_ref):
        y = jnp.dot(m_ref[...], w_ref[...], preferred_element_type=F32)
        x1 = x_ref[...] + ga_ref[...] * y
        y_ref[...] = y.astype(BF16)
        x1_ref[...] = x1
        h_ref[...] = ((x1 * _rms(x1) * g_ref[...]) * (1.0 + sc_ref[...]) + sh_ref[...]).astype(BF16)

    row = pl.BlockSpec((tm, d), lambda i: (i, 0))
    vecs, vec_specs = zip(*[_vec(v, d) for v in (ga1, g_ffn, sc2, sh2)])
    return pl.pallas_call(
        body, name="out_proj_fwd", grid=(t // tm,),
        in_specs=[row, pl.BlockSpec((d, d), lambda i: (0, 0)), row, *vec_specs],
        out_specs=[row, row, row],
        out_shape=[SDS((t, d), BF16), SDS((t, d), F32), SDS((t, d), BF16)],
        compiler_params=_params("parallel"))(merged, w_out, x, *vecs)


def _ffn_in_fwd(h2, w, ff, tm, tn):
    t, d = h2.shape
    nj = ff // tn
    assert w.shape == (2 * nj, d, tn)

    def body(h_ref, wg_ref, wu_ref, gu_ref, act_ref):
        hh = h_ref[...]
        g = jnp.dot(hh, wg_ref[...], preferred_element_type=F32)
        u = jnp.dot(hh, wu_ref[...], preferred_element_type=F32)
        sg = jax.nn.sigmoid(g)
        silu = g * sg
        gu_ref[0] = (u * (sg + silu * (1.0 - sg))).astype(BF16)
        gu_ref[1] = silu.astype(BF16)
        act_ref[...] = (silu * u).astype(BF16)

    return pl.pallas_call(
        body, name="ffn_in_fwd", grid=(nj, t // tm),
        in_specs=[pl.BlockSpec((tm, d), lambda j, i: (i, 0)), pl.BlockSpec((None, d, tn), lambda j, i: (j, 0, 0)),
                  pl.BlockSpec((None, d, tn), lambda j, i: (j + nj, 0, 0))],
        out_specs=[pl.BlockSpec((2, tm, tn), lambda j, i: (0, i, j)), pl.BlockSpec((tm, tn), lambda j, i: (i, j))],
        out_shape=[SDS((2, t, ff), BF16), SDS((t, ff), BF16)],
        compiler_params=_params("parallel", "parallel"))(h2, w, w)


def _ffn_out_loss(act, w, x1, target, ga2, g_final, tm):
    t, d = x1.shape
    ff = act.shape[1]

    def body(a_ref, w_ref, x1_ref, tg_ref, ga_ref, gf_ref, dx2_ref, dy2_ref, st_ref):
        @pl.when(pl.program_id(0) == 0)
        def _():
            st_ref[...] = jnp.zeros_like(st_ref)

        halves = [slice(k * (tm // 2), (k + 1) * (tm // 2)) for k in range(2)]
        y2s = [jnp.dot(a_ref[rows, :], w_ref[...], preferred_element_type=F32) for rows in halves]
        for rows, y2 in zip(halves, y2s):
            x2 = x1_ref[rows, :] + ga_ref[...] * y2
            r = _rms(x2)
            yn = x2 * r
            err = yn * gf_ref[...] - tg_ref[rows, :]
            loss = 0.5 * jnp.sum(jnp.mean(err * err, axis=-1, keepdims=True), axis=0, keepdims=True)
            dy = err * (1.0 / d)
            u = dy * gf_ref[...]
            dx2 = r * (u - yn * jnp.mean(u * yn, axis=-1, keepdims=True))
            dx2_ref[rows, :] = dx2
            dy2_ref[rows, :] = (ga_ref[...] * dx2).astype(BF16)
            st_ref[0:1, :] += jnp.sum(dx2 * y2, axis=0, keepdims=True)
            st_ref[1:2, :] += jnp.sum(dy * yn, axis=0, keepdims=True)
            st_ref[2:3, :] += jnp.broadcast_to(loss, (1, d))

    row = pl.BlockSpec((tm, d), lambda i: (i, 0))
    vecs, vec_specs = zip(*[_vec(v, d) for v in (ga2, g_final)])
    return pl.pallas_call(
        body, name="ffn_out_loss", grid=(t // tm,),
        in_specs=[pl.BlockSpec((tm, ff), lambda i: (i, 0)),
                  pl.BlockSpec((ff, d), lambda i: (0, 0), pipeline_mode=pl.Buffered(1)), row, row, *vec_specs],
        out_specs=[row, row, pl.BlockSpec((8, d), lambda i: (0, 0))],
        out_shape=[SDS((t, d), F32), SDS((t, d), BF16), SDS((8, d), F32)],
        compiler_params=_params("arbitrary"))(act, w, x1, target, *vecs)


def _ffn_out_bwd(dy2, w, gu, tm, tn):
    t, d = dy2.shape
    ff = w.shape[0]

    def body(dy_ref, w_ref, gu_ref, o_ref):
        dy = dy_ref[...]
        for lo in range(0, tn, 3 * LANES):
            cols = slice(lo, min(lo + 3 * LANES, tn))
            dact = lax.dot_general(dy, w_ref[cols, :], (((1,), (1,)), ((), ())), preferred_element_type=F32)
            o_ref[0, :, cols] = (dact * gu_ref[0, :, cols].astype(F32)).astype(BF16)
            o_ref[1, :, cols] = (dact * gu_ref[1, :, cols].astype(F32)).astype(BF16)

    gu_spec = pl.BlockSpec((2, tm, tn), lambda j, i: (0, i, j))
    return pl.pallas_call(
        body, name="ffn_out_bwd", grid=(ff // tn, t // tm),
        in_specs=[pl.BlockSpec((tm, d), lambda j, i: (i, 0)), pl.BlockSpec((tn, d), lambda j, i: (j, 0)), gu_spec],
        out_specs=gu_spec, out_shape=SDS((2, t, ff), BF16),
        compiler_params=_params("parallel", "parallel"))(dy2, w, gu)


def _wgrad(a, b, a_spec, b_spec, out_spec, out_shape, grid, name, ride=None):
    def body(a_ref, b_ref, o_ref, o16_ref):
        k = pl.program_id(len(grid) - 1)

        @pl.when(k == 0)
        def _():
            o_ref[...] = jnp.zeros_like(o_ref)

        o_ref[...] += lax.dot_general(a_ref[...], b_ref[...], (((0,), (0,)), ((), ())), preferred_element_type=F32)

        @pl.when(k == grid[-1] - 1)
        def _():
            o16_ref[...] = o_ref[...].astype(BF16)

    return _pallas(
        body, name=name, grid=grid, in_specs=[a_spec, b_spec], out_specs=[out_spec, out_spec],
        out_shape=[out_shape, SDS(out_shape.shape, BF16)], args=(a, b),
        sem=["parallel"] * (len(grid) - 1) + ["arbitrary"], ride=ride)


def _ffn_in_bwd(dgu, w, x1, dx2, y1, g_ffn, sc2, ga1, tm):
    t, d = x1.shape
    ff = dgu.shape[2]
    n_sh, _, sw = w.shape
    per = ff // sw
    nt = (((1,), (1,)), ((), ()))

    def body(a_ref, w_ref, x1_ref, dx2_ref, y1_ref, g_ref, sc_ref, ga_ref, dx1_ref, dy1_ref, st_ref):
        @pl.when(pl.program_id(0) == 0)
        def _():
            st_ref[...] = jnp.zeros_like(st_ref)

        dh = None
        for j in range(n_sh):
            part = lax.dot_general(a_ref[j // per, :, (j % per) * sw:(j % per + 1) * sw], w_ref[j], nt,
                                   preferred_element_type=F32)
            dh = part if dh is None else dh + part
        x1 = x1_ref[...]
        r = _rms(x1)
        xn = x1 * r
        g = g_ref[...]
        dn = dh * (1.0 + sc_ref[...])
        u = dn * g
        dx1 = dx2_ref[...] + r * (u - xn * jnp.mean(u * xn, axis=-1, keepdims=True))
        dx1_ref[...] = dx1
        dy1_ref[...] = (ga_ref[...] * dx1).astype(BF16)
        st_ref[0:1, :] += jnp.sum(dh, axis=0, keepdims=True)
        st_ref[1:2, :] += jnp.sum(dh * (xn * g), axis=0, keepdims=True)
        st_ref[2:3, :] += jnp.sum(dn * xn, axis=0, keepdims=True)
        st_ref[3:4, :] += jnp.sum(dx1 * y1_ref[...].astype(F32), axis=0, keepdims=True)

    row = pl.BlockSpec((tm, d), lambda i: (i, 0))
    vecs, vec_specs = zip(*[_vec(v, d) for v in (g_ffn, sc2, ga1)])
    return pl.pallas_call(
        body, name="ffn_in_bwd", grid=(t // tm,),
        in_specs=[pl.BlockSpec((2, tm, ff), lambda i: (0, i, 0)),
                  pl.BlockSpec((n_sh, d, sw), lambda i: (0, 0, 0), pipeline_mode=pl.Buffered(1)),
                  row, row, row, *vec_specs],
        out_specs=[row, row, pl.BlockSpec((8, d), lambda i: (0, 0))],
        out_shape=[SDS((t, d), F32), SDS((t, d), BF16), SDS((8, d), F32)],
        compiler_params=_params("arbitrary"))(dgu, w, x1, dx2, y1, *vecs)


def _out_proj_bwd(dy1, w_out, tm, ride=None):
    t, d = dy1.shape

    def body(dy_ref, w_ref, o_ref):
        o_ref[...] = lax.dot_general(dy_ref[...], w_ref[...], (((1,), (1,)), ((), ())),
                                     preferred_element_type=F32).astype(BF16)

    row = pl.BlockSpec((tm, d), lambda i: (i, 0))
    return _pallas(body, name="out_proj_bwd", grid=(t // tm,),
                   in_specs=[row, pl.BlockSpec((d, d), lambda i: (0, 0))], out_specs=row,
                   out_shape=SDS((t, d), BF16), args=(dy1, w_out), sem=("parallel",), ride=ride)


BWD_BLOCKS = 2


def _mixer_bwd(z, dmerged, attn, sinks, conv_w, d, ride=None):
    t, zw = z.shape
    kvw2 = zw - 6 * d
    tq = BWD_BLOCKS * BLOCK
    steps = t // tq
    halo = tq // SUBLANES_BF16
    last_halo = t // SUBLANES_BF16 - 1
    scale = HEAD_DIM ** -0.5
    seg = _segments(d, kvw2)

    def body(z_ref, kvp_ref, prev_ref, next_ref, dm_ref, dmn_ref, attn_ref, sinks_ref, cw_ref,
             dz_ref, dkv_ref, db_ref, dbkv_ref, dcw_ref, dsk_ref, carry_ref):
        n = pl.program_id(0)

        @pl.when(n == 0)
        def _():
            carry_ref[...] = jnp.zeros_like(carry_ref)
            db_ref[...] = jnp.zeros_like(db_ref)
            dbkv_ref[...] = jnp.zeros_like(dbkv_ref)
            dcw_ref[...] = jnp.zeros_like(dcw_ref)
            dsk_ref[...] = jnp.zeros_like(dsk_ref)

        def one_block(b, pending):
            rows = slice(b * BLOCK, (b + 1) * BLOCK)
            before = slice((b - 1) * BLOCK, b * BLOCK)
            dm = dm_ref[rows, :].astype(F32)
            sa = jax.nn.sigmoid(z_ref[rows, seg["ga"]].astype(F32))
            dga = dm * attn_ref[rows, :].astype(F32) * sa * (1.0 - sa)
            dz_ref[rows, seg["ga"]] = dga.astype(BF16)
            db_ref[0:1, seg["ga"]] += jnp.sum(dga, axis=0, keepdims=True)
            dattn = (dm * sa).astype(BF16)

            kv_prev = kvp_ref[...] if b == 0 else z_ref[before, seg["kv"]]
            kv = jnp.concatenate([kv_prev, z_ref[rows, seg["kv"]]], axis=0)
            k_eff, v_eff = _kv_variants(kv, kvw2 // 2)
            band, col = _attn_masks()
            valid = band & ((n > 0) | (col >= BLOCK)) if b == 0 else band
            lane_lo = lax.broadcasted_iota(jnp.int32, (2 * BLOCK, LANES), 1) < HEAD_DIM
            sink_lane = lax.broadcasted_iota(jnp.int32, (1, LANES), 1)
            rowblk = lax.broadcasted_iota(jnp.int32, (4 * BLOCK, 1), 0) // BLOCK
            dk_acc = [jnp.zeros((2 * BLOCK, LANES), F32), jnp.zeros((2 * BLOCK, LANES), F32)]
            dv_acc = [jnp.zeros((2 * BLOCK, LANES), F32), jnp.zeros((2 * BLOCK, LANES), F32)]
            dsink = jnp.zeros((1, LANES), F32)
            for h in range(2):
                q4 = _stack_pairs(z_ref, h, rows)
                do4 = jnp.concatenate([dattn[:, (4 * h + j) * LANES:(4 * h + j + 1) * LANES] for j in range(4)],
                                      axis=0)
                dq4 = jnp.zeros((4 * BLOCK, LANES), F32)
                for e in range(2):
                    s = lax.dot_general(q4, k_eff[h][e], (((1,), (1,)), ((), ())), preferred_element_type=F32)
                    p, psink = _softmax_sink(s, valid, _sink_column(sinks_ref, h, e))
                    dp = lax.dot_general(do4, v_eff[h][e], (((1,), (1,)), ((), ())), preferred_element_type=F32)
                    delta = jnp.sum(p * dp, axis=-1, keepdims=True)
                    ds = (p * (dp - delta)).astype(BF16)
                    dq4 = dq4 + jnp.dot(ds, k_eff[h][e], preferred_element_type=F32)
                    dk = lax.dot_general(q4, ds, (((0,), (0,)), ((), ())), preferred_element_type=F32).T
                    dv = lax.dot_general(do4, p.astype(BF16), (((0,), (0,)), ((), ())), preferred_element_type=F32).T
                    keep = lane_lo if e == 0 else jnp.logical_not(lane_lo)
                    slot = 0 if e == h else 1
                    dk_acc[slot] = dk_acc[slot] + jnp.where(keep, dk, 0.0)
                    dv_acc[slot] = dv_acc[slot] + jnp.where(keep, dv, 0.0)
                    dsk = -(psink * delta)
                    for j in range(4):
                        tot = jnp.sum(jnp.where(rowblk == j, dsk, 0.0), axis=0, keepdims=True)
                        dsink = dsink + jnp.where(sink_lane == GROUP * h + 2 * j + e, tot, 0.0)
                for j in range(4):
                    cols = slice((4 * h + j) * LANES, (4 * h + j + 1) * LANES)
                    dqj = dq4[j * BLOCK:(j + 1) * BLOCK]
                    dz_ref[rows, cols] = dqj.astype(BF16)
                    db_ref[0:1, cols] += jnp.sum(dqj, axis=0, keepdims=True)
            dsk_ref[0:1, :] += dsink
            dkv_new = jnp.concatenate([(dk_acc[0] + pltpu.roll(dk_acc[1], HEAD_DIM, 1)) * scale,
                                       dv_acc[0] + pltpu.roll(dv_acc[1], HEAD_DIM, 1)], axis=1)
            done = pending + dkv_new[:BLOCK]
            dkv_ref[rows, :] = done.astype(BF16)
            dbkv_ref[0:1, :] += jnp.sum(done, axis=0, keepdims=True)

            cb = z_ref[rows, seg["cb"]].astype(F32)
            cc = z_ref[rows, seg["cc"]].astype(F32)
            cx = z_ref[rows, seg["cx"]].astype(F32)
            sg = jax.nn.sigmoid(z_ref[rows, seg["gc"]].astype(F32))
            p_in = cc * cx
            if b == 0:
                prev = jnp.where(n > 0, prev_ref[:, seg["cc"]].astype(F32) * prev_ref[:, seg["cx"]].astype(F32), 0.0)
            else:
                tail = slice(b * BLOCK - SUBLANES_BF16, b * BLOCK)
                prev = z_ref[tail, seg["cc"]].astype(F32) * z_ref[tail, seg["cx"]].astype(F32)
            p_m1 = _shift_down(p_in, 1, prev)
            p_m2 = _shift_down(p_in, 2, prev)
            w0, w1, w2 = cw_ref[0:1, :], cw_ref[1:2, :], cw_ref[2:3, :]
            cconv = w0 * p_m2 + w1 * p_m1 + w2 * p_in
            dconv = dm * sg
            dgc = dm * (cb * cconv) * sg * (1.0 - sg)
            dcb = dconv * cconv
            dcc_t = dconv * cb
            if b == BWD_BLOCKS - 1:
                nxt = jnp.where(n < steps - 1,
                                dmn_ref[...].astype(F32) * jax.nn.sigmoid(next_ref[:, seg["gc"]].astype(F32))
                                * next_ref[:, seg["cb"]].astype(F32), 0.0)
            else:
                head = slice((b + 1) * BLOCK, (b + 1) * BLOCK + SUBLANES_BF16)
                nxt = (dm_ref[head, :].astype(F32) * jax.nn.sigmoid(z_ref[head, seg["gc"]].astype(F32))
                       * z_ref[head, seg["cb"]].astype(F32))
            dpin = w2 * dcc_t + w1 * _shift_up(dcc_t, 1, nxt) + w0 * _shift_up(dcc_t, 2, nxt)
            for nm, val in (("cb", dcb), ("cc", dpin * cx), ("cx", dpin * cc), ("gc", dgc)):
                dz_ref[rows, seg[nm]] = val.astype(BF16)
                db_ref[0:1, seg[nm]] += jnp.sum(val, axis=0, keepdims=True)
            dcw_ref[0:1, :] += jnp.sum(dcc_t * p_m2, axis=0, keepdims=True)
            dcw_ref[1:2, :] += jnp.sum(dcc_t * p_m1, axis=0, keepdims=True)
            dcw_ref[2:3, :] += jnp.sum(dcc_t * p_in, axis=0, keepdims=True)
            return dkv_new[BLOCK:]

        @pl.when(n < steps)
        def _():
            pending = carry_ref[...]
            for b in range(BWD_BLOCKS):
                pending = one_block(b, pending)
            carry_ref[...] = pending

        @pl.when(n == steps)
        def _():
            done = carry_ref[...]
            dkv_ref[:BLOCK, :] = done.astype(BF16)
            dkv_ref[BLOCK:, :] = jnp.zeros((tq - BLOCK, kvw2), BF16)
            dbkv_ref[0:1, :] += jnp.sum(done, axis=0, keepdims=True)

    def cur(n):
        return jnp.minimum(n, steps - 1)

    def after(n):
        return jnp.minimum((cur(n) + 1) * halo, last_halo)

    blk = pl.BlockSpec((tq, d), lambda n: (cur(n), 0))
    return _pallas(
        body, name="mixer_bwd", grid=(steps + 1,), ride=ride, sem=("arbitrary",),
        args=(z, z, z, z, dmerged, dmerged, attn, sinks, conv_w),
        in_specs=[pl.BlockSpec((tq, zw), lambda n: (cur(n), 0)),
                  pl.BlockSpec((BLOCK, kvw2), lambda n: (jnp.maximum(cur(n) * BWD_BLOCKS - 1, 0), d // kvw2)),
                  pl.BlockSpec((SUBLANES_BF16, zw), lambda n: (jnp.maximum(cur(n) * halo - 1, 0), 0)),
                  pl.BlockSpec((SUBLANES_BF16, zw), lambda n: (after(n), 0)),
                  blk,
                  pl.BlockSpec((SUBLANES_BF16, d), lambda n: (after(n), 0)),
                  blk, SMEM_SPEC, pl.BlockSpec((3, d), lambda n: (0, 0))],
        out_specs=[pl.BlockSpec((tq, zw), lambda n: (cur(n), 0)),
                   pl.BlockSpec((tq, kvw2), lambda n: (n, 0)),
                   pl.BlockSpec((8, zw), lambda n: (0, 0)), pl.BlockSpec((8, kvw2), lambda n: (0, 0)),
                   pl.BlockSpec((8, d), lambda n: (0, 0)), pl.BlockSpec((8, LANES), lambda n: (0, 0))],
        out_shape=[SDS((t, zw), BF16), SDS((t + tq, kvw2), BF16), SDS((8, zw), F32), SDS((8, kvw2), F32),
                   SDS((8, d), F32), SDS((8, LANES), F32)],
        scratch=[pltpu.VMEM((BLOCK, kvw2), F32)])


def _wgrad_in(dz, dkv, h1, tk, ride=None):
    t, zw = dz.shape
    d = h1.shape[1]
    kvw2 = dkv.shape[1]
    blk = d + kvw2
    assert zw % blk == 0
    tn = (((0,), (0,)), ((), ()))

    def body(a_ref, akv_ref, h_ref, o_ref, o16_ref):
        n, k = pl.program_id(0), pl.program_id(1)

        @pl.when(k == 0)
        def _():
            o_ref[...] = jnp.zeros_like(o_ref)

        @pl.when(n == 0)
        def _():
            o_ref[:d, :] += lax.dot_general(a_ref[:, :d], h_ref[...], tn, preferred_element_type=F32)
            o_ref[d:, :] += lax.dot_general(akv_ref[...], h_ref[...], tn, preferred_element_type=F32)

        @pl.when(n > 0)
        def _():
            o_ref[...] += lax.dot_general(a_ref[...], h_ref[...], tn, preferred_element_type=F32)

        @pl.when(k == t // tk - 1)
        def _():
            o16_ref[...] = o_ref[...].astype(BF16)

    out_spec = pl.BlockSpec((blk, d), lambda n, k: (n, 0))
    return _pallas(
        body, name="wgrad_in", grid=(zw // blk, t // tk),
        in_specs=[pl.BlockSpec((tk, blk), lambda n, k: (k, n)), pl.BlockSpec((tk, kvw2), lambda n, k: (k, 0)),
                  pl.BlockSpec((tk, d), lambda n, k: (k, 0))],
        out_specs=[out_spec, out_spec], out_shape=[SDS((zw, d), F32), SDS((zw, d), BF16)],
        args=(dz, dkv, h1), sem=("parallel", "arbitrary"), ride=ride)


def _in_proj_bwd(dz, dkv, wt, x, dx1, g_mix, sc1, tm, ride=None):
    t, d = x.shape
    zw = dz.shape[1]
    kvw2 = dkv.shape[1]
    rest = d + kvw2

    def body(a_ref, akv_ref, w_ref, x_ref, dx1_ref, g_ref, sc_ref, gx_ref, st_ref):
        @pl.when(pl.program_id(0) == 0)
        def _():
            st_ref[...] = jnp.zeros_like(st_ref)

        dh = (jnp.dot(a_ref[:, :d], w_ref[:d, :], preferred_element_type=F32)
              + jnp.dot(akv_ref[...], w_ref[d:rest, :], preferred_element_type=F32)
              + jnp.dot(a_ref[:, rest:], w_ref[rest:, :], preferred_element_type=F32))
        xx = x_ref[...]
        r = _rms(xx)
        xn = xx * r
        g = g_ref[...]
        dn = dh * (1.0 + sc_ref[...])
        u = dn * g
        gx_ref[...] = dx1_ref[...] + r * (u - xn * jnp.mean(u * xn, axis=-1, keepdims=True))
        st_ref[0:1, :] += jnp.sum(dh, axis=0, keepdims=True)
        st_ref[1:2, :] += jnp.sum(dh * (xn * g), axis=0, keepdims=True)
        st_ref[2:3, :] += jnp.sum(dn * xn, axis=0, keepdims=True)

    row = pl.BlockSpec((tm, d), lambda i: (i, 0))
    vecs, vec_specs = zip(*[_vec(v, d) for v in (g_mix, sc1)])
    return _pallas(
        body, name="in_proj_bwd", grid=(t // tm,),
        in_specs=[pl.BlockSpec((tm, zw), lambda i: (i, 0)), pl.BlockSpec((tm, kvw2), lambda i: (i, 0)),
                  pl.BlockSpec((zw, d), lambda i: (0, 0), pipeline_mode=pl.Buffered(1)),
                  row, row, *vec_specs],
        out_specs=[row, pl.BlockSpec((8, d), lambda i: (0, 0))],
        out_shape=[SDS((t, d), F32), SDS((8, d), F32)],
        args=(dz, dkv, wt, x, dx1, *vecs), sem=("arbitrary",), ride=ride)


def _to_lanes(v, rows=None):
    flat = v.reshape(-1)
    need = -(-flat.shape[0] // LANES)
    need = -(-need // 8) * 8 if rows is None else rows
    return jnp.pad(flat, (0, need * LANES - flat.shape[0])).reshape(need, LANES)


def kernel(x, c, w_ada, b_ada, g_mix, w_in, b_in, sinks, conv_w, w_out, g_ffn, w_ffn_in, w_ffn_out, g_final, loss_target, m_w_ada, m_b_ada, m_g_mix, m_w_in, m_b_in, m_sinks, m_conv_w, m_w_out, m_g_ffn, m_w_ffn_in, m_w_ffn_out, m_g_final, v_w_ada, v_b_ada, v_g_mix, v_w_in, v_b_in, v_sinks, v_conv_w, v_w_out, v_g_ffn, v_w_ffn_in, v_w_ffn_out, v_g_final):
    xs, tgt = x[0], loss_target[0]
    t, d = xs.shape
    zw = w_in.shape[2] * N_CHIP
    kvw2 = zw - 6 * d
    ff = w_ffn_out.shape[1] * N_CHIP
    n_mod = w_ada.shape[2] * N_CHIP // d
    mod_sh = w_ada.shape[2]
    cw_sh = conv_w.shape[2]
    assert d % (8 * LANES) == 0 and kvw2 == 2 * LANES and t % 512 == 0 and n_mod == 6
    xi, yi, ci = _mesh_pos()
    j_me = 2 * xi + yi
    pos = jnp.stack([ci, j_me]).astype(jnp.int32)
    tm = 512

    w_in_t, m_w_in_t, v_w_in_t = w_in[0].T, m_w_in[0].T, v_w_in[0].T
    assert d == 8 * LANES
    pack1 = jnp.concatenate([c.reshape(d // LANES, LANES), conv_w[0].reshape(-1, LANES)], axis=0)
    pack1 = jnp.pad(pack1, ((0, 16 - pack1.shape[0]), (0, 0)))
    b_ada_sh = lax.dynamic_slice(b_ada, (0, j_me * mod_sh), (1, mod_sh))
    g1, mod, w_in_g, later = _startup(pack1, w_ada[0], b_ada_sh, _cast_into_block(pos, w_in_t, "cast_w_in"),
                                      [w_out[0], w_ffn_in[0], w_ffn_out[0]])
    c_all = g1[:, :d // LANES, :].reshape(N_DEV, d)
    cw_rows = 3 * cw_sh // LANES
    conv_w_full = jnp.concatenate(
        [g1[2 * j, d // LANES:d // LANES + cw_rows, :].reshape(3, cw_sh) for j in range(N_CHIP)], axis=1)
    sh1, sc1, ga1, sh2, sc2, ga2 = [(mod, k) for k in range(6)]
    w_in_tf = w_in_g.reshape(zw, d)

    (z, h1), later = _in_proj(xs, g_mix, sc1, sh1, w_in_tf, b_in, min(t, 2048), zw // 5, ride=_x_gather_ici(later))
    (attn, merged), later = _mixer_fwd(z, sinks, conv_w_full, d, ride=_x_gather_d2d(later))
    w_out_f = later[0].reshape(d, d)
    w_ffn_in_f = later[1]
    w_ffn_out_f = later[2].reshape(ff, d)
    tml = min(t, 1024)
    y1, x1, h2 = _out_proj_fwd(merged, w_out_f, xs, ga1, g_ffn, sc2, sh2, tml)
    gu, act = _ffn_in_fwd(h2, w_ffn_in_f, ff, tml, ff // 2)
    dx2, dy2, st_loss = _ffn_out_loss(act, w_ffn_out_f, x1, tgt, ga2, g_final.reshape(1, d), tml)

    dgu = _ffn_out_bwd(dy2, w_ffn_out_f, gu, tml, ff // 2)
    tk = min(t, 2048)
    dw_ffn_out, _ = _wgrad(
        act, dy2, pl.BlockSpec((tk, ff // 2), lambda m, k: (k, m)), pl.BlockSpec((tk, d), lambda m, k: (k, 0)),
        pl.BlockSpec((ff // 2, d), lambda m, k: (m, 0)), SDS((ff, d), F32), (2, t // tk), "wgrad_ffn_out")
    dx1, dy1, st_ffn = _ffn_in_bwd(dgu, w_ffn_in_f, x1, dx2, y1, g_ffn, sc2, ga1, tm)
    dw_ffn_in, _ = _wgrad(
        h2, dgu, pl.BlockSpec((tk, d), lambda n, k: (k, 0)),
        pl.BlockSpec((None, tk, ff // 2), lambda n, k: (n // 2, k, n % 2)),
        pl.BlockSpec((None, d, ff // 2), lambda n, k: (n, 0, 0)), SDS((N_CHIP, d, ff // 2), F32),
        (N_CHIP, t // tk), "wgrad_ffn_in")
    tko = min(t, 4096)
    dw_out, _ = _wgrad(
        merged, dy1, pl.BlockSpec((tko, d), lambda m, k: (k, 0)), pl.BlockSpec((tko, d), lambda m, k: (k, 0)),
        pl.BlockSpec((d, d), lambda m, k: (0, 0)), SDS((d, d), F32), (1, t // tko), "wgrad_out")

    early = [[g.reshape(N_CHIP, -1, g.shape[-1]) for g in pair] for pair in (dw_out, dw_ffn_in, dw_ffn_out)]
    early_names = ["w_out", "w_ffn_in", "w_ffn_out"]
    dmerged, _ = _out_proj_bwd(dy1, w_out_f, tml)
    (dz, dkv_shifted, db_z, db_kv, dcw, dsk), terms = _mixer_bwd(
        z, dmerged, attn, sinks, conv_w_full, d, ride=_x_reduce([e[0] for e in early], [e[1] for e in early]))
    dkv = dkv_shifted[BLOCK:BLOCK + t]
    fulls = [_sum_terms(pos, e[0], s, r, "sum_terms_" + nm)
             for e, s, r, nm in zip(early, terms[:3], terms[3:], early_names)]
    dw_in_t, (g_w_out, g_w_ffn_in, g_w_ffn_out) = _wgrad_in(dz, dkv, h1, tk, ride=_x_pair_exchange(fulls))
    dw_in_t = [g.reshape(N_CHIP, zw // N_CHIP, d) for g in dw_in_t]

    (grad_x, st_in), (from_sib, from_far) = _in_proj_bwd(dz, dkv, w_in_tf, xs, dx1, g_mix, sc1, tm,
                                                         ride=_x_reduce([dw_in_t[0]], [dw_in_t[1]]))
    g_w_in_half = _sum_terms(pos, dw_in_t[0], from_sib, from_far, "sum_terms_w_in")

    dmod = jnp.concatenate([st_in[0:1], st_in[1:2], st_ffn[3:4], st_ffn[0:1], st_ffn[1:2], st_loss[0:1]], axis=1)
    db_in = jnp.concatenate([db_z[0:1, :d], db_kv[0:1], db_z[0:1, d + kvw2:]], axis=1)
    seg = [dmod, st_in[2:3], db_in, dsk[0:1], dcw[0:3].reshape(1, 3 * d), st_ffn[2:3], st_loss[1:2],
           st_loss[2:3, :LANES]]
    sizes = [s.shape[1] for s in seg]
    pack2 = _to_lanes(jnp.concatenate(seg, axis=1))
    packs, g_w_in_t = _tail_exchange(pack2, g_w_in_half)
    tot = _pack_sum(packs).reshape(-1)
    offs = [sum(sizes[:k]) for k in range(len(sizes))]
    gb_ada, gg_mix, gb_in, gsinks, gcw, gg_ffn, gg_final, loss_v = [tot[o:o + s] for o, s in zip(offs, sizes)]
    loss = loss_v[0]
    gsinks = gsinks[:sinks.shape[1]]
    gcw_sh = lax.dynamic_slice(gcw.reshape(3, d), (0, j_me * cw_sh), (3, cw_sh))

    dmod_all = packs[:, :n_mod * d // LANES, :].reshape(N_DEV, n_mod * d)
    g_w_ada = _ada_wgrad(c_all, lax.dynamic_slice(dmod_all, (0, j_me * mod_sh), (N_DEV, mod_sh)))

    out_g, out_d, out_m, out_v = {}, {}, {}, {}
    big = {"w_ada": (w_ada[0], g_w_ada, m_w_ada[0], v_w_ada[0]),
           "w_out": (w_out[0], g_w_out, m_w_out[0], v_w_out[0]),
           "w_ffn_in": (w_ffn_in[0], g_w_ffn_in, m_w_ffn_in[0], v_w_ffn_in[0]),
           "w_ffn_out": (w_ffn_out[0], g_w_ffn_out, m_w_ffn_out[0], v_w_ffn_out[0])}
    for nm, (w, g, m, v) in big.items():
        out_g[nm], out_d[nm], out_m[nm], out_v[nm] = [o[None] for o in _adamw(w, g, m, v, "adamw_" + nm)]
    out_g["w_in"], out_d["w_in"], out_m["w_in"], out_v["w_in"] = [
        o.T[None] for o in _adamw(w_in_t, g_w_in_t, m_w_in_t, v_w_in_t, "adamw_w_in")]
    small = {"b_ada": (b_ada, gb_ada, m_b_ada, v_b_ada), "g_mix": (g_mix, gg_mix, m_g_mix, v_g_mix),
             "b_in": (b_in, gb_in, m_b_in, v_b_in), "sinks": (sinks, gsinks, m_sinks, v_sinks),
             "conv_w": (conv_w, gcw_sh, m_conv_w, v_conv_w), "g_ffn": (g_ffn, gg_ffn, m_g_ffn, v_g_ffn),
             "g_final": (g_final, gg_final, m_g_final, v_g_final)}
    def two_d(a):
        return a.reshape(-1, a.shape[-1])

    s_out = _adamw_small([tuple(two_d(a.reshape(w.shape)) for a in (w, g, m, v)) for w, g, m, v in small.values()])
    for (nm, (w, g, _, _)), res in zip(small.items(), s_out):
        out_g[nm] = g.reshape(w.shape)
        out_d[nm], out_m[nm], out_v[nm] = [o.reshape(w.shape) for o in res]

    order = ["w_ada", "b_ada", "g_mix", "w_in", "b_in", "sinks", "conv_w", "w_out", "g_ffn", "w_ffn_in", "w_ffn_out",
             "g_final"]
    return (loss, grad_x[None], *[out_g[k] for k in order], *[out_d[k] for k in order],
            *[out_m[k] for k in order], *[out_v[k] for k in order])
```

```python
import functools

import jax
import jax.numpy as jnp
from jax import lax
from jax.experimental import pallas as pl
from jax.experimental.pallas import tpu as pltpu

F32 = jnp.float32
BF16 = jnp.bfloat16
EPS = 1e-6
HEAD_DIM = 64
GROUP = 8
BLOCK = 128
LANES = 128
SUBLANES_BF16 = 16
N_DEV = 8
N_CHIP = 4
VMEM_LIMIT = 56 * 1024 * 1024
MESH = pl.DeviceIdType.MESH

ADAM_LR = 0.001
ADAM_B1 = 0.9
ADAM_B2 = 0.999
ADAM_EPS = 1e-08
ADAM_WD = 0.01
ADAM_STEP = 10

SDS = jax.ShapeDtypeStruct
ANY = pl.BlockSpec(memory_space=pl.ANY)
VMEM_SPEC = pl.BlockSpec(memory_space=pltpu.VMEM)
SMEM_SPEC = pl.BlockSpec(memory_space=pltpu.SMEM)


def _params(*sem):
    return pltpu.CompilerParams(dimension_semantics=sem, vmem_limit_bytes=VMEM_LIMIT)


def _vec(v, d):
    arr, k = v if isinstance(v, tuple) else (v, 0)
    return arr, pl.BlockSpec((1, d), lambda *_: (0, k))


def _mesh_pos():
    return lax.axis_index("x"), lax.axis_index("y"), lax.axis_index("c")


def _row_tile(rows, cols, itemsize=4, budget=1 << 20, mult=8):
    best = None
    for t in range(mult, rows + 1, mult):
        if rows % t == 0 and t * cols * itemsize <= budget:
            best = t
    if best is None:
        best = rows
    return best


def _gather_all(v_ref, out_ref, send_sems, recv_sems, local_sem):
    x, y, c = _mesh_pos()
    me = 4 * x + 2 * y + c
    mine = pltpu.make_async_copy(v_ref, out_ref.at[me], local_sem)
    mine.start()
    peers = []
    for k in range(1, N_DEV):
        px = 1 - x if k & 4 else x
        py = 1 - y if k & 2 else y
        pc = 1 - c if k & 1 else c
        peers.append((px, py, pc))

    def copy(k, block):
        return pltpu.make_async_remote_copy(
            src_ref=v_ref, dst_ref=out_ref.at[block], send_sem=send_sems.at[k], recv_sem=recv_sems.at[k],
            device_id=peers[k], device_id_type=MESH)

    sends = [copy(k, me) for k in range(N_DEV - 1)]
    for cp in sends:
        cp.start()
    for k, (px, py, pc) in enumerate(peers):
        copy(k, 4 * px + 2 * py + pc).wait_recv()
    for cp in sends:
        cp.wait_send()
    mine.wait()


def _small_sems():
    return [pltpu.SemaphoreType.DMA((N_DEV - 1,)), pltpu.SemaphoreType.DMA((N_DEV - 1,)), pltpu.SemaphoreType.DMA]


def _tail_exchange(pack, full):
    def body(pack_ref, full_unused, packs_ref, full_ref, s1, r1, l1, send_sem, recv_sem):
        del full_unused
        x, y, c = _mesh_pos()
        half = full_ref.shape[0] // 2
        rows = pl.ds(pl.multiple_of(c * half, 8), half)
        swap = pltpu.make_async_remote_copy(
            src_ref=full_ref.at[rows], dst_ref=full_ref.at[rows], send_sem=send_sem, recv_sem=recv_sem,
            device_id=(x, y, 1 - c), device_id_type=MESH)
        swap.start()
        _gather_all(pack_ref, packs_ref, s1, r1, l1)
        swap.wait()

    return pl.pallas_call(
        body, name="tail_exchange", out_shape=[SDS((N_DEV,) + pack.shape, pack.dtype), SDS(full.shape, full.dtype)],
        in_specs=[VMEM_SPEC, ANY], out_specs=[VMEM_SPEC, ANY], input_output_aliases={1: 1},
        scratch_shapes=_small_sems() + [pltpu.SemaphoreType.DMA, pltpu.SemaphoreType.DMA])(pack, full)


def _other_chips(x, y):
    return [(1 - x, y), (x, 1 - y), (1 - x, 1 - y)]


def _startup(pack, w_ada_sh, b_ada_sh, w_buf, later):
    d, n = w_ada_sh.shape
    kc = d // LANES
    n_l = len(later)
    chunk_rows = [_row_tile(a.shape[0], a.shape[1], budget=3 << 19, mult=SUBLANES_BF16) for a in later]

    def body(*refs):
        pack_ref, wa_hbm, ba_ref, w_in_unused = refs[:4]
        later_src = refs[4:4 + n_l]
        packs_ref, mine_ref, w_ref = refs[4 + n_l:7 + n_l]
        later_dst = refs[7 + n_l:7 + 2 * n_l]
        wa_scr, mod_scr, mod_ref = refs[7 + 2 * n_l:10 + 2 * n_l]
        f32_bufs = refs[10 + 2 * n_l:10 + 3 * n_l]
        bf16_bufs = refs[10 + 3 * n_l:10 + 4 * n_l]
        (s1, r1, l1, s2, r2, l2, send_sems, recv_sems, fsend_sems, frecv_sems, relay_send, relay_recv, wa_sem,
         cast_sems) = refs[10 + 4 * n_l:]
        del w_in_unused
        x, y, c = _mesh_pos()
        j_me = 2 * x + y
        chips = _other_chips(x, y)
        half = w_ref.shape[1] // 2

        def rows_of(which):
            return pl.ds(pl.multiple_of(which * half, SUBLANES_BF16), half)

        def copy(p, block, rows, over_ici):
            sems = (send_sems, recv_sems) if over_ici else (fsend_sems, frecv_sems)
            return pltpu.make_async_remote_copy(
                src_ref=w_ref.at[block, rows], dst_ref=w_ref.at[block, rows], send_sem=sems[0].at[p],
                recv_sem=sems[1].at[p], device_id=(*chips[p], c) if over_ici else (x, y, 1 - c), device_id_type=MESH)

        def block_of(p):
            return 2 * chips[p][0] + chips[p][1]

        def relay(q, block):
            rows = pl.ds(pl.multiple_of(c * half + q * (half // 2), SUBLANES_BF16), half // 2)
            return pltpu.make_async_remote_copy(
                src_ref=w_ref.at[block, rows], dst_ref=w_ref.at[block, rows], send_sem=relay_send.at[q],
                recv_sem=relay_recv.at[q], device_id=(*chips[1 - q], c), device_id_type=MESH)

        load_wa = pltpu.make_async_copy(wa_hbm, wa_scr, wa_sem)
        load_wa.start()
        _gather_all(pack_ref, packs_ref, s1, r1, l1)
        sends = [copy(p, j_me, rows_of(c), True) for p in range(2)]
        for cp in sends:
            cp.start()
        load_wa.wait()
        acc = jnp.zeros((N_DEV, n), F32)
        for k in range(kc):
            ck = packs_ref[:, k, :]
            sk = (ck * jax.nn.sigmoid(ck)).astype(BF16)
            acc = acc + jnp.dot(sk, wa_scr[k * LANES:(k + 1) * LANES, :].astype(BF16), preferred_element_type=F32)
        mod_scr[...] = acc + ba_ref[...]
        _gather_all(mod_scr, mod_ref, s2, r2, l2)
        for j in range(N_CHIP):
            mine_ref[:, j * n:(j + 1) * n] = mod_ref[2 * j, pl.ds(4 * x + 2 * y + c, 1), :]
        passed = []
        for q in range(2):
            copy(q, block_of(q), rows_of(c), True).wait_recv()
            for cp in (relay(q, block_of(q)), copy(q, block_of(q), rows_of(c), False)):
                cp.start()
                passed.append(cp)
        for src, dst, fbuf, bbuf in zip(later_src, later_dst, f32_bufs, bf16_bufs):
            cr = fbuf.shape[0]
            for k in range(src.shape[0] // cr):
                rows = pl.ds(k * cr, cr)
                cin = pltpu.make_async_copy(src.at[rows], fbuf, cast_sems.at[0])
                cin.start()
                cin.wait()
                bbuf[...] = fbuf[...].astype(BF16)
                cout = pltpu.make_async_copy(bbuf, dst.at[j_me, rows], cast_sems.at[1])
                cout.start()
                cout.wait()
        for q in range(2):
            relay(q, block_of(2)).wait_recv()
        fw = copy(2, block_of(2), rows_of(c), False)
        fw.start()
        for p in range(3):
            copy(p, block_of(p), rows_of(1 - c), False).wait_recv()
        for cp in sends + passed + [fw]:
            cp.wait_send()

    res = pl.pallas_call(
        body, name="startup",
        out_shape=[SDS((N_DEV,) + pack.shape, F32), SDS((1, N_CHIP * n), F32), SDS(w_buf.shape, w_buf.dtype)]
        + [SDS((N_CHIP,) + a.shape, BF16) for a in later],
        in_specs=[VMEM_SPEC, ANY, VMEM_SPEC, ANY] + [ANY] * n_l, out_specs=[VMEM_SPEC, VMEM_SPEC, ANY] + [ANY] * n_l,
        input_output_aliases={3: 2},
        scratch_shapes=[pltpu.VMEM((d, n), F32), pltpu.VMEM((N_DEV, n), F32), pltpu.VMEM((N_DEV, N_DEV, n), F32)]
        + [pltpu.VMEM((cr, a.shape[1]), F32) for cr, a in zip(chunk_rows, later)]
        + [pltpu.VMEM((cr, a.shape[1]), BF16) for cr, a in zip(chunk_rows, later)]
        + _small_sems() + _small_sems()
        + [pltpu.SemaphoreType.DMA((3,))] * 4 + [pltpu.SemaphoreType.DMA((2,))] * 2 + [pltpu.SemaphoreType.DMA]
        + [pltpu.SemaphoreType.DMA((2,))],
        compiler_params=pltpu.CompilerParams(vmem_limit_bytes=VMEM_LIMIT),
    )(pack, w_ada_sh, b_ada_sh, w_buf, *later)
    return res[0], res[1], res[2], list(res[3:])


class _Exchange:
    def __init__(self, operands, out_shape, in_place, n_sems, copies):
        self.operands, self.out_shape, self.in_place, self.n_sems, self.copies = (
            list(operands), list(out_shape), in_place, n_sems, copies)

    def sems(self):
        return [pltpu.SemaphoreType.DMA((self.n_sems,)), pltpu.SemaphoreType.DMA((self.n_sems,))]


def _x_gather_ici(bufs):
    def copies(ins, outs, send_sems, recv_sems):
        x, y, c = _mesh_pos()
        chips = _other_chips(x, y)
        out = []
        for w in range(len(outs)):
            half = outs[w].shape[1] // 2
            rows = pl.ds(pl.multiple_of(c * half, SUBLANES_BF16), half)
            for p in range(3):
                out.append(pltpu.make_async_remote_copy(
                    src_ref=outs[w].at[2 * x + y, rows], dst_ref=outs[w].at[2 * x + y, rows],
                    send_sem=send_sems.at[w * 3 + p], recv_sem=recv_sems.at[w * 3 + p],
                    device_id=(*chips[p], c), device_id_type=MESH))
        return out

    return _Exchange(bufs, [SDS(b.shape, b.dtype) for b in bufs], True, 3 * len(bufs), copies)


def _x_gather_d2d(bufs):
    def copies(ins, outs, send_sems, recv_sems):
        x, y, c = _mesh_pos()
        chips = _other_chips(x, y)
        out = []
        for w in range(len(outs)):
            half = outs[w].shape[1] // 2
            rows = pl.ds(pl.multiple_of(c * half, SUBLANES_BF16), half)
            for p in range(3):
                block = 2 * chips[p][0] + chips[p][1]
                out.append(pltpu.make_async_remote_copy(
                    src_ref=outs[w].at[block, rows], dst_ref=outs[w].at[block, rows],
                    send_sem=send_sems.at[w * 3 + p], recv_sem=recv_sems.at[w * 3 + p],
                    device_id=(x, y, 1 - c), device_id_type=MESH))
        return out

    return _Exchange(bufs, [SDS(b.shape, b.dtype) for b in bufs], True, 3 * len(bufs), copies)


N_REMOTE = 6


def _x_reduce(grads32, grads16):
    n_w = len(grads32)

    def copies(ins, outs, send_sems, recv_sems):
        g32, g16 = ins[:n_w], ins[n_w:]
        from_sib, from_far = outs[:n_w], outs[n_w:]
        x, y, c = _mesh_pos()
        chips = _other_chips(x, y)
        out = []
        for w in range(n_w):
            half = g32[w].shape[1] // 2
            k0 = w * (N_REMOTE + 1)
            out.append(pltpu.make_async_remote_copy(
                src_ref=g32[w].at[2 * x + y, pl.ds(pl.multiple_of((1 - c) * half, SUBLANES_BF16), half), :],
                dst_ref=from_sib[w], send_sem=send_sems.at[k0], recv_sem=recv_sems.at[k0],
                device_id=(x, y, 1 - c), device_id_type=MESH))
            for p in range(3):
                for f in range(2):
                    tc = c if f == 0 else 1 - c
                    k = 2 * p + f
                    out.append(pltpu.make_async_remote_copy(
                        src_ref=g16[w].at[2 * chips[p][0] + chips[p][1],
                                          pl.ds(pl.multiple_of(tc * half, SUBLANES_BF16), half), :],
                        dst_ref=from_far[w].at[k], send_sem=send_sems.at[k0 + 1 + k], recv_sem=recv_sems.at[k0 + 1 + k],
                        device_id=(*chips[p], tc), device_id_type=MESH))
        return out

    shapes = ([SDS((g.shape[1] // 2, g.shape[2]), g.dtype) for g in grads32]
              + [SDS((N_REMOTE, g.shape[1] // 2, g.shape[2]), g.dtype) for g in grads16])
    return _Exchange(list(grads32) + list(grads16), shapes, False, (N_REMOTE + 1) * n_w, copies)


def _x_pair_exchange(fulls):
    def copies(ins, outs, send_sems, recv_sems):
        x, y, c = _mesh_pos()
        out = []
        for w in range(len(outs)):
            half = outs[w].shape[0] // 2
            rows = pl.ds(pl.multiple_of(c * half, 8), half)
            out.append(pltpu.make_async_remote_copy(
                src_ref=outs[w].at[rows], dst_ref=outs[w].at[rows], send_sem=send_sems.at[w],
                recv_sem=recv_sems.at[w], device_id=(x, y, 1 - c), device_id_type=MESH))
        return out

    return _Exchange(fulls, [SDS(f.shape, f.dtype) for f in fulls], True, len(fulls), copies)


def _pallas(body, *, name, grid, in_specs, out_specs, out_shape, args, scratch=(), sem=None, ride=None):
    single = not isinstance(out_specs, (list, tuple))
    out_specs_l = [out_specs] if single else list(out_specs)
    out_shape_l = [out_shape] if single else list(out_shape)
    n_in, n_out, n_scr = len(in_specs), len(out_specs_l), len(scratch)
    if ride is None:
        res = pl.pallas_call(body, name=name, grid=grid, in_specs=list(in_specs), out_specs=out_specs,
                             out_shape=out_shape, scratch_shapes=list(scratch), compiler_params=_params(*sem))(*args)
        return res, None
    n_x, n_xo = len(ride.operands), len(ride.out_shape)

    def full_body(*refs):
        ins, x_ins = refs[:n_in], refs[n_in:n_in + n_x]
        outs = refs[n_in + n_x:n_in + n_x + n_out]
        x_outs = refs[n_in + n_x + n_out:n_in + n_x + n_out + n_xo]
        rest = refs[n_in + n_x + n_out + n_xo:]
        scr, (send_sems, recv_sems) = rest[:n_scr], rest[n_scr:]
        first = functools.reduce(jnp.logical_and, [pl.program_id(a) == 0 for a in range(len(grid))])
        last = functools.reduce(jnp.logical_and, [pl.program_id(a) == grid[a] - 1 for a in range(len(grid))])

        @pl.when(first)
        def _():
            for cp in ride.copies(x_ins, x_outs, send_sems, recv_sems):
                cp.start()

        body(*ins, *outs, *scr)

        @pl.when(last)
        def _():
            for cp in ride.copies(x_ins, x_outs, send_sems, recv_sems):
                cp.wait()

    res = pl.pallas_call(
        full_body, name=name, grid=grid, in_specs=list(in_specs) + [ANY] * n_x,
        out_specs=out_specs_l + [ANY] * n_xo, out_shape=out_shape_l + ride.out_shape,
        input_output_aliases={n_in + k: n_out + k for k in range(n_x)} if ride.in_place else {},
        scratch_shapes=list(scratch) + ride.sems(),
        compiler_params=_params(*(["arbitrary"] * len(grid))))(*args, *ride.operands)
    own = res[0] if single else list(res[:n_out])
    return own, list(res[n_out:])


def _cast_into_block(pos, w, name):
    rows, cols = w.shape
    tr = _row_tile(rows, cols, mult=SUBLANES_BF16)

    def body(pos_ref, w_ref, o_ref):
        del pos_ref
        o_ref[...] = w_ref[...].astype(BF16)

    return pl.pallas_call(
        body, name=name,
        grid_spec=pltpu.PrefetchScalarGridSpec(
            num_scalar_prefetch=1, grid=(rows // tr,),
            in_specs=[pl.BlockSpec((tr, cols), lambda i, pos_ref: (i, 0))],
            out_specs=pl.BlockSpec((None, tr, cols), lambda i, pos_ref: (pos_ref[1], i, 0))),
        out_shape=SDS((N_CHIP, rows, cols), BF16), compiler_params=_params("parallel"))(pos, w)


def _sum_terms(pos, grad, from_sib, from_far, name):
    _, rows, cols = grad.shape
    half = rows // 2
    tr = _row_tile(half, cols, mult=SUBLANES_BF16)
    nblk = half // tr

    def body(pos_ref, g_ref, s_ref, r_ref, o_ref):
        del pos_ref
        acc = g_ref[...] + s_ref[...]
        for k in range(N_REMOTE):
            acc = acc + r_ref[k].astype(F32)
        o_ref[...] = acc

    return pl.pallas_call(
        body, name=name,
        grid_spec=pltpu.PrefetchScalarGridSpec(
            num_scalar_prefetch=1, grid=(nblk,),
            in_specs=[pl.BlockSpec((None, tr, cols), lambda i, pos_ref: (pos_ref[1], pos_ref[0] * nblk + i, 0)),
                      pl.BlockSpec((tr, cols), lambda i, pos_ref: (i, 0)),
                      pl.BlockSpec((N_REMOTE, tr, cols), lambda i, pos_ref: (0, i, 0))],
            out_specs=pl.BlockSpec((tr, cols), lambda i, pos_ref: (pos_ref[0] * nblk + i, 0))),
        out_shape=SDS((rows, cols), F32),
        compiler_params=_params("parallel"),
    )(pos, grad, from_sib, from_far)


def _adamw(w, g, m, v, name):
    rows, cols = w.shape
    tr = _row_tile(rows, cols, budget=1 << 21)

    def body(w_ref, g_ref, m_ref, v_ref, go_ref, d_ref, nm_ref, nv_ref):
        go_ref[...] = g_ref[...]
        _adamw_update(w_ref, g_ref, m_ref, v_ref, d_ref, nm_ref, nv_ref)

    spec = pl.BlockSpec((tr, cols), lambda i: (i, 0))
    return pl.pallas_call(body, name=name, grid=(rows // tr,), in_specs=[spec] * 4, out_specs=[spec] * 4,
                          out_shape=[SDS((rows, cols), F32)] * 4, compiler_params=_params("parallel"))(w, g, m, v)


def _adamw_update(w_ref, g_ref, m_ref, v_ref, d_ref, nm_ref, nv_ref):
    gg = g_ref[...]
    nm = ADAM_B1 * m_ref[...] + (1.0 - ADAM_B1) * gg
    nv = ADAM_B2 * v_ref[...] + (1.0 - ADAM_B2) * (gg * gg)
    m_hat = nm / (1.0 - ADAM_B1 ** ADAM_STEP)
    v_hat = nv / (1.0 - ADAM_B2 ** ADAM_STEP)
    d_ref[...] = -ADAM_LR * (m_hat / (jnp.sqrt(v_hat) + ADAM_EPS) + ADAM_WD * w_ref[...])
    nm_ref[...] = nm
    nv_ref[...] = nv


def _adamw_small(params):
    n_p = len(params)

    def body(*refs):
        ins, outs = refs[:4 * n_p], refs[4 * n_p:]
        for k in range(n_p):
            _adamw_update(*ins[4 * k:4 * k + 4], *outs[3 * k:3 * k + 3])

    flat = [a for tup in params for a in tup]
    res = pl.pallas_call(
        body, name="adamw_small", in_specs=[VMEM_SPEC] * (4 * n_p), out_specs=[VMEM_SPEC] * (3 * n_p),
        out_shape=[SDS(tup[0].shape, F32) for tup in params for _ in range(3)])(*flat)
    return [res[3 * k:3 * k + 3] for k in range(n_p)]


def _pack_sum(gathered):
    _, rows, cols = gathered.shape

    def body(g_ref, o_ref):
        acc = g_ref[0]
        for d in range(1, N_DEV):
            acc = acc + g_ref[d]
        o_ref[...] = acc

    return pl.pallas_call(body, name="pack_sum", in_specs=[VMEM_SPEC], out_specs=VMEM_SPEC,
                          out_shape=SDS((rows, cols), F32))(gathered)


def _ada_wgrad(c_all, dmod_sh):
    d = c_all.shape[1]
    n = dmod_sh.shape[1]
    tn = 512

    def body(c_ref, g_ref, o_ref):
        cc = c_ref[...]
        s = cc * jax.nn.sigmoid(cc)
        o_ref[...] = lax.dot_general(s, g_ref[...], (((0,), (0,)), ((), ())), preferred_element_type=F32,
                                     precision=lax.Precision.HIGHEST)

    return pl.pallas_call(
        body, name="ada_wgrad", grid=(n // tn,),
        in_specs=[pl.BlockSpec((N_DEV, d), lambda j: (0, 0)), pl.BlockSpec((N_DEV, tn), lambda j: (0, j))],
        out_specs=pl.BlockSpec((d, tn), lambda j: (0, j)),
        out_shape=SDS((d, n), F32), compiler_params=_params("parallel"))(c_all, dmod_sh)


def _rms(xf):
    return lax.rsqrt(jnp.mean(xf * xf, axis=-1, keepdims=True) + EPS)


def _in_proj(x, g, sc, sh, wt, b, tm, tn, ride=None):
    t, d = x.shape
    n = wt.shape[0]

    def body(x_ref, g_ref, sc_ref, sh_ref, w_ref, b_ref, z_ref, h_ref):
        @pl.when(pl.program_id(1) == 0)
        def _():
            xf = x_ref[...]
            h_ref[...] = ((xf * _rms(xf) * g_ref[...]) * (1.0 + sc_ref[...]) + sh_ref[...]).astype(BF16)

        acc = lax.dot_general(h_ref[...], w_ref[...], (((1,), (1,)), ((), ())), preferred_element_type=F32)
        z_ref[...] = (acc + b_ref[...]).astype(BF16)

    row = pl.BlockSpec((tm, d), lambda i, j: (i, 0))
    vecs, vec_specs = zip(*[_vec(v, d) for v in (g, sc, sh)])
    return _pallas(
        body, name="in_proj", grid=(t // tm, n // tn),
        in_specs=[row, *vec_specs, pl.BlockSpec((tn, d), lambda i, j: (j, 0)),
                  pl.BlockSpec((1, tn), lambda i, j: (0, j))],
        out_specs=[pl.BlockSpec((tm, tn), lambda i, j: (i, j)), row],
        out_shape=[SDS((t, n), BF16), SDS((t, d), BF16)], args=(x, *vecs, wt, b),
        sem=("parallel", "arbitrary"), ride=ride)


def _segments(d, kvw2):
    o = d + kvw2
    names = ("cb", "cc", "cx", "ga", "gc")
    seg = {nm: slice(o + k * d, o + (k + 1) * d) for k, nm in enumerate(names)}
    seg["q"], seg["kv"] = slice(0, d), slice(d, o)
    return seg


def _attn_masks():
    rows = 4 * BLOCK
    r = lax.broadcasted_iota(jnp.int32, (rows, 2 * BLOCK), 0) & (BLOCK - 1)
    col = lax.broadcasted_iota(jnp.int32, (rows, 2 * BLOCK), 1)
    return (col > r) & (col <= r + BLOCK), col


def _kv_variants(kv, n_kv_w):
    assert n_kv_w == LANES
    kb, vb = kv[:, :LANES] * (HEAD_DIM ** -0.5), kv[:, LANES:]
    kr, vr = pltpu.roll(kb, HEAD_DIM, 1), pltpu.roll(vb, HEAD_DIM, 1)
    lane = lax.broadcasted_iota(jnp.int32, kb.shape, 1)
    lo = lane < HEAD_DIM
    zero = jnp.zeros_like(kb)
    k_eff = [[None, None], [None, None]]
    v_eff = [[None, None], [None, None]]
    for h in range(2):
        for e in range(2):
            ksrc, vsrc = (kb, vb) if e == h else (kr, vr)
            keep = lo if e == 0 else jnp.logical_not(lo)
            k_eff[h][e] = jnp.where(keep, ksrc, zero)
            v_eff[h][e] = jnp.where(keep, vsrc, zero)
    return k_eff, v_eff


def _sink_column(sinks_ref, h, e):
    rowblk = lax.broadcasted_iota(jnp.int32, (4 * BLOCK, 1), 0) // BLOCK
    col = jnp.zeros((4 * BLOCK, 1), F32)
    for j in range(4):
        col = jnp.where(rowblk == j, sinks_ref[0, GROUP * h + 2 * j + e], col)
    return col


def _softmax_sink(s, valid, sink):
    s = jnp.where(valid, s, -jnp.inf)
    m = jnp.maximum(jnp.max(s, axis=-1, keepdims=True), sink)
    p = jnp.exp(s - m)
    psink = jnp.exp(sink - m)
    den = jnp.sum(p, axis=-1, keepdims=True) + psink
    inv = 1.0 / den
    return p * inv, psink * inv


def _shift_down(a, s, prev):
    rows = a.shape[0]
    out = pltpu.roll(a, s, 0)
    row = lax.broadcasted_iota(jnp.int32, a.shape, 0)
    for t in range(s):
        out = jnp.where(row == t, prev[SUBLANES_BF16 - s + t:SUBLANES_BF16 - s + t + 1, :], out)
    del rows
    return out


def _shift_up(a, s, nxt):
    rows = a.shape[0]
    out = pltpu.roll(a, rows - s, 0)
    row = lax.broadcasted_iota(jnp.int32, a.shape, 0)
    for t in range(s):
        out = jnp.where(row == rows - s + t, nxt[t:t + 1, :], out)
    return out


def _stack_pairs(ref, h, rows=slice(None)):
    return jnp.concatenate([ref[rows, (4 * h + j) * LANES:(4 * h + j + 1) * LANES] for j in range(4)], axis=0)


FWD_BLOCKS = 4


def _mixer_fwd(z, sinks, conv_w, d, ride=None):
    t, zw = z.shape
    kvw2 = zw - 6 * d
    tq = FWD_BLOCKS * BLOCK
    halo = tq // SUBLANES_BF16
    seg = _segments(d, kvw2)

    def body(z_ref, kvp_ref, prev_ref, sinks_ref, cw_ref, attn_ref, merged_ref):
        n = pl.program_id(0)
        band, col = _attn_masks()
        for b in range(FWD_BLOCKS):
            rows = slice(b * BLOCK, (b + 1) * BLOCK)
            before = slice((b - 1) * BLOCK, b * BLOCK)
            kv_prev = kvp_ref[...] if b == 0 else z_ref[before, seg["kv"]]
            kv = jnp.concatenate([kv_prev, z_ref[rows, seg["kv"]]], axis=0)
            k_eff, v_eff = _kv_variants(kv, kvw2 // 2)
            valid = band & ((n > 0) | (col >= BLOCK)) if b == 0 else band
            for h in range(2):
                q4 = _stack_pairs(z_ref, h, rows)
                o4 = jnp.zeros((4 * BLOCK, LANES), F32)
                for e in range(2):
                    s = lax.dot_general(q4, k_eff[h][e], (((1,), (1,)), ((), ())), preferred_element_type=F32)
                    p, _ = _softmax_sink(s, valid, _sink_column(sinks_ref, h, e))
                    o4 = o4 + jnp.dot(p.astype(BF16), v_eff[h][e], preferred_element_type=F32)
                for j in range(4):
                    attn_ref[rows, (4 * h + j) * LANES:(4 * h + j + 1) * LANES] = (
                        o4[j * BLOCK:(j + 1) * BLOCK].astype(BF16))
            cb = z_ref[rows, seg["cb"]].astype(F32)
            p_in = z_ref[rows, seg["cc"]].astype(F32) * z_ref[rows, seg["cx"]].astype(F32)
            if b == 0:
                prev = jnp.where(n > 0, prev_ref[:, seg["cc"]].astype(F32) * prev_ref[:, seg["cx"]].astype(F32), 0.0)
            else:
                tail = slice(b * BLOCK - SUBLANES_BF16, b * BLOCK)
                prev = z_ref[tail, seg["cc"]].astype(F32) * z_ref[tail, seg["cx"]].astype(F32)
            cconv = (cw_ref[0:1, :] * _shift_down(p_in, 2, prev) + cw_ref[1:2, :] * _shift_down(p_in, 1, prev)
                     + cw_ref[2:3, :] * p_in)
            sa = jax.nn.sigmoid(z_ref[rows, seg["ga"]].astype(F32))
            sg = jax.nn.sigmoid(z_ref[rows, seg["gc"]].astype(F32))
            merged_ref[rows, :] = (sa * attn_ref[rows, :].astype(F32) + sg * (cb * cconv)).astype(BF16)

    blk = pl.BlockSpec((tq, d), lambda n: (n, 0))
    return _pallas(
        body, name="mixer_fwd", grid=(t // tq,),
        in_specs=[pl.BlockSpec((tq, zw), lambda n: (n, 0)),
                  pl.BlockSpec((BLOCK, kvw2), lambda n: (jnp.maximum(n * FWD_BLOCKS - 1, 0), d // kvw2)),
                  pl.BlockSpec((SUBLANES_BF16, zw), lambda n: (jnp.maximum(n * halo - 1, 0), 0)),
                  SMEM_SPEC, pl.BlockSpec((3, d), lambda n: (0, 0))],
        out_specs=[blk, blk],
        out_shape=[SDS((t, d), BF16), SDS((t, d), BF16)],
        args=(z, z, z, sinks, conv_w), sem=("parallel",), ride=ride)


def _out_proj_fwd(merged, w_out, x, ga1, g_ffn, sc2, sh2, tm):
    t, d = x.shape

    def body(m_ref, w_ref, x_ref, ga_ref, g_ref, sc_ref, sh_ref, y_ref, x1_ref, h_ref):
        y = jnp.dot(m_ref[...], w_ref[...], preferred_element_type=F32)
        x1 = x_ref[...] + ga_ref[...] * y
        y_ref[...] = y.astype(BF16)
        x1_ref[...] = x1
        h_ref[...] = ((x1 * _rms(x1) * g_ref[...]) * (1.0 + sc_ref[...]) + sh_ref[...]).astype(BF16)

    row = pl.BlockSpec((tm, d), lambda i: (i, 0))
    vecs, vec_specs = zip(*[_vec(v, d) for v in (ga1, g_ffn, sc2, sh2)])
    return pl.pallas_call(
        body, name="out_proj_fwd", grid=(t // tm,),
        in_specs=[row, pl.BlockSpec((d, d), lambda i: (0, 0)), row, *vec_specs],
        out_specs=[row, row, row],
        out_shape=[SDS((t, d), BF16), SDS((t, d), F32), SDS((t, d), BF16)],
        compiler_params=_params("parallel"))(merged, w_out, x, *vecs)


def _ffn_in_fwd(h2, w, ff, tm, tn):
    t, d = h2.shape
    nj = ff // tn
    assert w.shape == (2 * nj, d, tn)

    def body(h_ref, wg_ref, wu_ref, gu_ref, act_ref):
        hh = h_ref[...]
        g = jnp.dot(hh, wg_ref[...], preferred_element_type=F32)
        u = jnp.dot(hh, wu_ref[...], preferred_element_type=F32)
        sg = jax.nn.sigmoid(g)
        silu = g * sg
        gu_ref[0] = (u * (sg + silu * (1.0 - sg))).astype(BF16)
        gu_ref[1] = silu.astype(BF16)
        act_ref[...] = (silu * u).astype(BF16)

    return pl.pallas_call(
        body, name="ffn_in_fwd", grid=(nj, t // tm),
        in_specs=[pl.BlockSpec((tm, d), lambda j, i: (i, 0)), pl.BlockSpec((None, d, tn), lambda j, i: (j, 0, 0)),
                  pl.BlockSpec((None, d, tn), lambda j, i: (j + nj, 0, 0))],
        out_specs=[pl.BlockSpec((2, tm, tn), lambda j, i: (0, i, j)), pl.BlockSpec((tm, tn), lambda j, i: (i, j))],
        out_shape=[SDS((2, t, ff), BF16), SDS((t, ff), BF16)],
        compiler_params=_params("parallel", "parallel"))(h2, w, w)


def _ffn_out_loss(act, w, x1, target, ga2, g_final, tm):
    t, d = x1.shape
    ff = act.shape[1]

    def body(a_ref, w_ref, x1_ref, tg_ref, ga_ref, gf_ref, dx2_ref, dy2_ref, st_ref):
        @pl.when(pl.program_id(0) == 0)
        def _():
            st_ref[...] = jnp.zeros_like(st_ref)

        halves = [slice(k * (tm // 2), (k + 1) * (tm // 2)) for k in range(2)]
        y2s = [jnp.dot(a_ref[rows, :], w_ref[...], preferred_element_type=F32) for rows in halves]
        for rows, y2 in zip(halves, y2s):
            x2 = x1_ref[rows, :] + ga_ref[...] * y2
            r = _rms(x2)
            yn = x2 * r
            err = yn * gf_ref[...] - tg_ref[rows, :]
            loss = 0.5 * jnp.sum(jnp.mean(err * err, axis=-1, keepdims=True), axis=0, keepdims=True)
            dy = err * (1.0 / d)
            u = dy * gf_ref[...]
            dx2 = r * (u - yn * jnp.mean(u * yn, axis=-1, keepdims=True))
            dx2_ref[rows, :] = dx2
            dy2_ref[rows, :] = (ga_ref[...] * dx2).astype(BF16)
            st_ref[0:1, :] += jnp.sum(dx2 * y2, axis=0, keepdims=True)
            st_ref[1:2, :] += jnp.sum(dy * yn, axis=0, keepdims=True)
            st_ref[2:3, :] += jnp.broadcast_to(loss, (1, d))

    row = pl.BlockSpec((tm, d), lambda i: (i, 0))
    vecs, vec_specs = zip(*[_vec(v, d) for v in (ga2, g_final)])
    return pl.pallas_call(
        body, name="ffn_out_loss", grid=(t // tm,),
        in_specs=[pl.BlockSpec((tm, ff), lambda i: (i, 0)),
                  pl.BlockSpec((ff, d), lambda i: (0, 0), pipeline_mode=pl.Buffered(1)), row, row, *vec_specs],
        out_specs=[row, row, pl.BlockSpec((8, d), lambda i: (0, 0))],
        out_shape=[SDS((t, d), F32), SDS((t, d), BF16), SDS((8, d), F32)],
        compiler_params=_params("arbitrary"))(act, w, x1, target, *vecs)


def _ffn_out_bwd(dy2, w, gu, tm, tn):
    t, d = dy2.shape
    ff = w.shape[0]

    def body(dy_ref, w_ref, gu_ref, o_ref):
        dy = dy_ref[...]
        for lo in range(0, tn, 3 * LANES):
            cols = slice(lo, min(lo + 3 * LANES, tn))
            dact = lax.dot_general(dy, w_ref[cols, :], (((1,), (1,)), ((), ())), preferred_element_type=F32)
            o_ref[0, :, cols] = (dact * gu_ref[0, :, cols].astype(F32)).astype(BF16)
            o_ref[1, :, cols] = (dact * gu_ref[1, :, cols].astype(F32)).astype(BF16)

    gu_spec = pl.BlockSpec((2, tm, tn), lambda j, i: (0, i, j))
    return pl.pallas_call(
        body, name="ffn_out_bwd", grid=(ff // tn, t // tm),
        in_specs=[pl.BlockSpec((tm, d), lambda j, i: (i, 0)), pl.BlockSpec((tn, d), lambda j, i: (j, 0)), gu_spec],
        out_specs=gu_spec, out_shape=SDS((2, t, ff), BF16),
        compiler_params=_params("parallel", "parallel"))(dy2, w, gu)


def _wgrad(a, b, a_spec, b_spec, out_spec, out_shape, grid, name, ride=None):
    def body(a_ref, b_ref, o_ref, o16_ref):
        k = pl.program_id(len(grid) - 1)

        @pl.when(k == 0)
        def _():
            o_ref[...] = jnp.zeros_like(o_ref)

        o_ref[...] += lax.dot_general(a_ref[...], b_ref[...], (((0,), (0,)), ((), ())), preferred_element_type=F32)

        @pl.when(k == grid[-1] - 1)
        def _():
            o16_ref[...] = o_ref[...].astype(BF16)

    return _pallas(
        body, name=name, grid=grid, in_specs=[a_spec, b_spec], out_specs=[out_spec, out_spec],
        out_shape=[out_shape, SDS(out_shape.shape, BF16)], args=(a, b),
        sem=["parallel"] * (len(grid) - 1) + ["arbitrary"], ride=ride)


def _ffn_in_bwd(dgu, w, x1, dx2, y1, g_ffn, sc2, ga1, tm):
    t, d = x1.shape
    ff = dgu.shape[2]
    n_sh, _, sw = w.shape
    per = ff // sw
    nt = (((1,), (1,)), ((), ()))

    def body(a_ref, w_ref, x1_ref, dx2_ref, y1_ref, g_ref, sc_ref, ga_ref, dx1_ref, dy1_ref, st_ref):
        @pl.when(pl.program_id(0) == 0)
        def _():
            st_ref[...] = jnp.zeros_like(st_ref)

        dh = None
        for j in range(n_sh):
            part = lax.dot_general(a_ref[j // per, :, (j % per) * sw:(j % per + 1) * sw], w_ref[j], nt,
                                   preferred_element_type=F32)
            dh = part if dh is None else dh + part
        x1 = x1_ref[...]
        r = _rms(x1)
        xn = x1 * r
        g = g_ref[...]
        dn = dh * (1.0 + sc_ref[...])
        u = dn * g
        dx1 = dx2_ref[...] + r * (u - xn * jnp.mean(u * xn, axis=-1, keepdims=True))
        dx1_ref[...] = dx1
        dy1_ref[...] = (ga_ref[...] * dx1).astype(BF16)
        st_ref[0:1, :] += jnp.sum(dh, axis=0, keepdims=True)
        st_ref[1:2, :] += jnp.sum(dh * (xn * g), axis=0, keepdims=True)
        st_ref[2:3, :] += jnp.sum(dn * xn, axis=0, keepdims=True)
        st_ref[3:4, :] += jnp.sum(dx1 * y1_ref[...].astype(F32), axis=0, keepdims=True)

    row = pl.BlockSpec((tm, d), lambda i: (i, 0))
    vecs, vec_specs = zip(*[_vec(v, d) for v in (g_ffn, sc2, ga1)])
    return pl.pallas_call(
        body, name="ffn_in_bwd", grid=(t // tm,),
        in_specs=[pl.BlockSpec((2, tm, ff), lambda i: (0, i, 0)),
                  pl.BlockSpec((n_sh, d, sw), lambda i: (0, 0, 0), pipeline_mode=pl.Buffered(1)),
                  row, row, row, *vec_specs],
        out_specs=[row, row, pl.BlockSpec((8, d), lambda i: (0, 0))],
        out_shape=[SDS((t, d), F32), SDS((t, d), BF16), SDS((8, d), F32)],
        compiler_params=_params("arbitrary"))(dgu, w, x1, dx2, y1, *vecs)


def _out_proj_bwd(dy1, w_out, tm, ride=None):
    t, d = dy1.shape

    def body(dy_ref, w_ref, o_ref):
        o_ref[...] = lax.dot_general(dy_ref[...], w_ref[...], (((1,), (1,)), ((), ())),
                                     preferred_element_type=F32).astype(BF16)

    row = pl.BlockSpec((tm, d), lambda i: (i, 0))
    return _pallas(body, name="out_proj_bwd", grid=(t // tm,),
                   in_specs=[row, pl.BlockSpec((d, d), lambda i: (0, 0))], out_specs=row,
                   out_shape=SDS((t, d), BF16), args=(dy1, w_out), sem=("parallel",), ride=ride)


BWD_BLOCKS = 2


def _mixer_bwd(z, dmerged, attn, sinks, conv_w, d, ride=None):
    t, zw = z.shape
    kvw2 = zw - 6 * d
    tq = BWD_BLOCKS * BLOCK
    steps = t // tq
    halo = tq // SUBLANES_BF16
    last_halo = t // SUBLANES_BF16 - 1
    scale = HEAD_DIM ** -0.5
    seg = _segments(d, kvw2)

    def body(z_ref, kvp_ref, prev_ref, next_ref, dm_ref, dmn_ref, attn_ref, sinks_ref, cw_ref,
             dz_ref, dkv_ref, db_ref, dbkv_ref, dcw_ref, dsk_ref, carry_ref):
        n = pl.program_id(0)

        @pl.when(n == 0)
        def _():
            carry_ref[...] = jnp.zeros_like(carry_ref)
            db_ref[...] = jnp.zeros_like(db_ref)
            dbkv_ref[...] = jnp.zeros_like(dbkv_ref)
            dcw_ref[...] = jnp.zeros_like(dcw_ref)
            dsk_ref[...] = jnp.zeros_like(dsk_ref)

        def one_block(b, pending):
            rows = slice(b * BLOCK, (b + 1) * BLOCK)
            before = slice((b - 1) * BLOCK, b * BLOCK)
            dm = dm_ref[rows, :].astype(F32)
            sa = jax.nn.sigmoid(z_ref[rows, seg["ga"]].astype(F32))
            dga = dm * attn_ref[rows, :].astype(F32) * sa * (1.0 - sa)
            dz_ref[rows, seg["ga"]] = dga.astype(BF16)
            db_ref[0:1, seg["ga"]] += jnp.sum(dga, axis=0, keepdims=True)
            dattn = (dm * sa).astype(BF16)

            kv_prev = kvp_ref[...] if b == 0 else z_ref[before, seg["kv"]]
            kv = jnp.concatenate([kv_prev, z_ref[rows, seg["kv"]]], axis=0)
            k_eff, v_eff = _kv_variants(kv, kvw2 // 2)
            band, col = _attn_masks()
            valid = band & ((n > 0) | (col >= BLOCK)) if b == 0 else band
            lane_lo = lax.broadcasted_iota(jnp.int32, (2 * BLOCK, LANES), 1) < HEAD_DIM
            sink_lane = lax.broadcasted_iota(jnp.int32, (1, LANES), 1)
            rowblk = lax.broadcasted_iota(jnp.int32, (4 * BLOCK, 1), 0) // BLOCK
            dk_acc = [jnp.zeros((2 * BLOCK, LANES), F32), jnp.zeros((2 * BLOCK, LANES), F32)]
            dv_acc = [jnp.zeros((2 * BLOCK, LANES), F32), jnp.zeros((2 * BLOCK, LANES), F32)]
            dsink = jnp.zeros((1, LANES), F32)
            for h in range(2):
                q4 = _stack_pairs(z_ref, h, rows)
                do4 = jnp.concatenate([dattn[:, (4 * h + j) * LANES:(4 * h + j + 1) * LANES] for j in range(4)],
                                      axis=0)
                dq4 = jnp.zeros((4 * BLOCK, LANES), F32)
                for e in range(2):
                    s = lax.dot_general(q4, k_eff[h][e], (((1,), (1,)), ((), ())), preferred_element_type=F32)
                    p, psink = _softmax_sink(s, valid, _sink_column(sinks_ref, h, e))
                    dp = lax.dot_general(do4, v_eff[h][e], (((1,), (1,)), ((), ())), preferred_element_type=F32)
                    delta = jnp.sum(p * dp, axis=-1, keepdims=True)
                    ds = (p * (dp - delta)).astype(BF16)
                    dq4 = dq4 + jnp.dot(ds, k_eff[h][e], preferred_element_type=F32)
                    dk = lax.dot_general(q4, ds, (((0,), (0,)), ((), ())), preferred_element_type=F32).T
                    dv = lax.dot_general(do4, p.astype(BF16), (((0,), (0,)), ((), ())), preferred_element_type=F32).T
                    keep = lane_lo if e == 0 else jnp.logical_not(lane_lo)
                    slot = 0 if e == h else 1
                    dk_acc[slot] = dk_acc[slot] + jnp.where(keep, dk, 0.0)
                    dv_acc[slot] = dv_acc[slot] + jnp.where(keep, dv, 0.0)
                    dsk = -(psink * delta)
                    for j in range(4):
                        tot = jnp.sum(jnp.where(rowblk == j, dsk, 0.0), axis=0, keepdims=True)
                        dsink = dsink + jnp.where(sink_lane == GROUP * h + 2 * j + e, tot, 0.0)
                for j in range(4):
                    cols = slice((4 * h + j) * LANES, (4 * h + j + 1) * LANES)
                    dqj = dq4[j * BLOCK:(j + 1) * BLOCK]
                    dz_ref[rows, cols] = dqj.astype(BF16)
                    db_ref[0:1, cols] += jnp.sum(dqj, axis=0, keepdims=True)
            dsk_ref[0:1, :] += dsink
            dkv_new = jnp.concatenate([(dk_acc[0] + pltpu.roll(dk_acc[1], HEAD_DIM, 1)) * scale,
                                       dv_acc[0] + pltpu.roll(dv_acc[1], HEAD_DIM, 1)], axis=1)
            done = pending + dkv_new[:BLOCK]
            dkv_ref[rows, :] = done.astype(BF16)
            dbkv_ref[0:1, :] += jnp.sum(done, axis=0, keepdims=True)

            cb = z_ref[rows, seg["cb"]].astype(F32)
            cc = z_ref[rows, seg["cc"]].astype(F32)
            cx = z_ref[rows, seg["cx"]].astype(F32)
            sg = jax.nn.sigmoid(z_ref[rows, seg["gc"]].astype(F32))
            p_in = cc * cx
            if b == 0:
                prev = jnp.where(n > 0, prev_ref[:, seg["cc"]].astype(F32) * prev_ref[:, seg["cx"]].astype(F32), 0.0)
            else:
                tail = slice(b * BLOCK - SUBLANES_BF16, b * BLOCK)
                prev = z_ref[tail, seg["cc"]].astype(F32) * z_ref[tail, seg["cx"]].astype(F32)
            p_m1 = _shift_down(p_in, 1, prev)
            p_m2 = _shift_down(p_in, 2, prev)
            w0, w1, w2 = cw_ref[0:1, :], cw_ref[1:2, :], cw_ref[2:3, :]
            cconv = w0 * p_m2 + w1 * p_m1 + w2 * p_in
            dconv = dm * sg
            dgc = dm * (cb * cconv) * sg * (1.0 - sg)
            dcb = dconv * cconv
            dcc_t = dconv * cb
            if b == BWD_BLOCKS - 1:
                nxt = jnp.where(n < steps - 1,
                                dmn_ref[...].astype(F32) * jax.nn.sigmoid(next_ref[:, seg["gc"]].astype(F32))
                                * next_ref[:, seg["cb"]].astype(F32), 0.0)
            else:
                head = slice((b + 1) * BLOCK, (b + 1) * BLOCK + SUBLANES_BF16)
                nxt = (dm_ref[head, :].astype(F32) * jax.nn.sigmoid(z_ref[head, seg["gc"]].astype(F32))
                       * z_ref[head, seg["cb"]].astype(F32))
            dpin = w2 * dcc_t + w1 * _shift_up(dcc_t, 1, nxt) + w0 * _shift_up(dcc_t, 2, nxt)
            for nm, val in (("cb", dcb), ("cc", dpin * cx), ("cx", dpin * cc), ("gc", dgc)):
                dz_ref[rows, seg[nm]] = val.astype(BF16)
                db_ref[0:1, seg[nm]] += jnp.sum(val, axis=0, keepdims=True)
            dcw_ref[0:1, :] += jnp.sum(dcc_t * p_m2, axis=0, keepdims=True)
            dcw_ref[1:2, :] += jnp.sum(dcc_t * p_m1, axis=0, keepdims=True)
            dcw_ref[2:3, :] += jnp.sum(dcc_t * p_in, axis=0, keepdims=True)
            return dkv_new[BLOCK:]

        @pl.when(n < steps)
        def _():
            pending = carry_ref[...]
            for b in range(BWD_BLOCKS):
                pending = one_block(b, pending)
            carry_ref[...] = pending

        @pl.when(n == steps)
        def _():
            done = carry_ref[...]
            dkv_ref[:BLOCK, :] = done.astype(BF16)
            dkv_ref[BLOCK:, :] = jnp.zeros((tq - BLOCK, kvw2), BF16)
            dbkv_ref[0:1, :] += jnp.sum(done, axis=0, keepdims=True)

    def cur(n):
        return jnp.minimum(n, steps - 1)

    def after(n):
        return jnp.minimum((cur(n) + 1) * halo, last_halo)

    blk = pl.BlockSpec((tq, d), lambda n: (cur(n), 0))
    return _pallas(
        body, name="mixer_bwd", grid=(steps + 1,), ride=ride, sem=("arbitrary",),
        args=(z, z, z, z, dmerged, dmerged, attn, sinks, conv_w),
        in_specs=[pl.BlockSpec((tq, zw), lambda n: (cur(n), 0)),
                  pl.BlockSpec((BLOCK, kvw2), lambda n: (jnp.maximum(cur(n) * BWD_BLOCKS - 1, 0), d // kvw2)),
                  pl.BlockSpec((SUBLANES_BF16, zw), lambda n: (jnp.maximum(cur(n) * halo - 1, 0), 0)),
                  pl.BlockSpec((SUBLANES_BF16, zw), lambda n: (after(n), 0)),
                  blk,
                  pl.BlockSpec((SUBLANES_BF16, d), lambda n: (after(n), 0)),
                  blk, SMEM_SPEC, pl.BlockSpec((3, d), lambda n: (0, 0))],
        out_specs=[pl.BlockSpec((tq, zw), lambda n: (cur(n), 0)),
                   pl.BlockSpec((tq, kvw2), lambda n: (n, 0)),
                   pl.BlockSpec((8, zw), lambda n: (0, 0)), pl.BlockSpec((8, kvw2), lambda n: (0, 0)),
                   pl.BlockSpec((8, d), lambda n: (0, 0)), pl.BlockSpec((8, LANES), lambda n: (0, 0))],
        out_shape=[SDS((t, zw), BF16), SDS((t + tq, kvw2), BF16), SDS((8, zw), F32), SDS((8, kvw2), F32),
                   SDS((8, d), F32), SDS((8, LANES), F32)],
        scratch=[pltpu.VMEM((BLOCK, kvw2), F32)])


def _wgrad_in(dz, dkv, h1, tk, ride=None):
    t, zw = dz.shape
    d = h1.shape[1]
    kvw2 = dkv.shape[1]
    blk = d + kvw2
    assert zw % blk == 0
    tn = (((0,), (0,)), ((), ()))

    def body(a_ref, akv_ref, h_ref, o_ref, o16_ref):
        n, k = pl.program_id(0), pl.program_id(1)

        @pl.when(k == 0)
        def _():
            o_ref[...] = jnp.zeros_like(o_ref)

        @pl.when(n == 0)
        def _():
            o_ref[:d, :] += lax.dot_general(a_ref[:, :d], h_ref[...], tn, preferred_element_type=F32)
            o_ref[d:, :] += lax.dot_general(akv_ref[...], h_ref[...], tn, preferred_element_type=F32)

        @pl.when(n > 0)
        def _():
            o_ref[...] += lax.dot_general(a_ref[...], h_ref[...], tn, preferred_element_type=F32)

        @pl.when(k == t // tk - 1)
        def _():
            o16_ref[...] = o_ref[...].astype(BF16)

    out_spec = pl.BlockSpec((blk, d), lambda n, k: (n, 0))
    return _pallas(
        body, name="wgrad_in", grid=(zw // blk, t // tk),
        in_specs=[pl.BlockSpec((tk, blk), lambda n, k: (k, n)), pl.BlockSpec((tk, kvw2), lambda n, k: (k, 0)),
                  pl.BlockSpec((tk, d), lambda n, k: (k, 0))],
        out_specs=[out_spec, out_spec], out_shape=[SDS((zw, d), F32), SDS((zw, d), BF16)],
        args=(dz, dkv, h1), sem=("parallel", "arbitrary"), ride=ride)


def _in_proj_bwd(dz, dkv, wt, x, dx1, g_mix, sc1, tm, ride=None):
    t, d = x.shape
    zw = dz.shape[1]
    kvw2 = dkv.shape[1]
    rest = d + kvw2

    def body(a_ref, akv_ref, w_ref, x_ref, dx1_ref, g_ref, sc_ref, gx_ref, st_ref):
        @pl.when(pl.program_id(0) == 0)
        def _():
            st_ref[...] = jnp.zeros_like(st_ref)

        dh = (jnp.dot(a_ref[:, :d], w_ref[:d, :], preferred_element_type=F32)
              + jnp.dot(akv_ref[...], w_ref[d:rest, :], preferred_element_type=F32)
              + jnp.dot(a_ref[:, rest:], w_ref[rest:, :], preferred_element_type=F32))
        xx = x_ref[...]
        r = _rms(xx)
        xn = xx * r
        g = g_ref[...]
        dn = dh * (1.0 + sc_ref[...])
        u = dn * g
        gx_ref[...] = dx1_ref[...] + r * (u - xn * jnp.mean(u * xn, axis=-1, keepdims=True))
        st_ref[0:1, :] += jnp.sum(dh, axis=0, keepdims=True)
        st_ref[1:2, :] += jnp.sum(dh * (xn * g), axis=0, keepdims=True)
        st_ref[2:3, :] += jnp.sum(dn * xn, axis=0, keepdims=True)

    row = pl.BlockSpec((tm, d), lambda i: (i, 0))
    vecs, vec_specs = zip(*[_vec(v, d) for v in (g_mix, sc1)])
    return _pallas(
        body, name="in_proj_bwd", grid=(t // tm,),
        in_specs=[pl.BlockSpec((tm, zw), lambda i: (i, 0)), pl.BlockSpec((tm, kvw2), lambda i: (i, 0)),
                  pl.BlockSpec((zw, d), lambda i: (0, 0), pipeline_mode=pl.Buffered(1)),
                  row, row, *vec_specs],
        out_specs=[row, pl.BlockSpec((8, d), lambda i: (0, 0))],
        out_shape=[SDS((t, d), F32), SDS((8, d), F32)],
        args=(dz, dkv, wt, x, dx1, *vecs), sem=("arbitrary",), ride=ride)


def _to_lanes(v, rows=None):
    flat = v.reshape(-1)
    need = -(-flat.shape[0] // LANES)
    need = -(-need // 8) * 8 if rows is None else rows
    return jnp.pad(flat, (0, need * LANES - flat.shape[0])).reshape(need, LANES)


def kernel(x, c, w_ada, b_ada, g_mix, w_in, b_in, sinks, conv_w, w_out, g_ffn, w_ffn_in, w_ffn_out, g_final, loss_target, m_w_ada, m_b_ada, m_g_mix, m_w_in, m_b_in, m_sinks, m_conv_w, m_w_out, m_g_ffn, m_w_ffn_in, m_w_ffn_out, m_g_final, v_w_ada, v_b_ada, v_g_mix, v_w_in, v_b_in, v_sinks, v_conv_w, v_w_out, v_g_ffn, v_w_ffn_in, v_w_ffn_out, v_g_final):
    xs, tgt = x[0], loss_target[0]
    t, d = xs.shape
    zw = w_in.shape[2] * N_CHIP
    kvw2 = zw - 6 * d
    ff = w_ffn_out.shape[1] * N_CHIP
    n_mod = w_ada.shape[2] * N_CHIP // d
    mod_sh = w_ada.shape[2]
    cw_sh = conv_w.shape[2]
    assert d % (8 * LANES) == 0 and kvw2 == 2 * LANES and t % 512 == 0 and n_mod == 6
    xi, yi, ci = _mesh_pos()
    j_me = 2 * xi + yi
    pos = jnp.stack([ci, j_me]).astype(jnp.int32)
    tm = 512

    w_in_t, m_w_in_t, v_w_in_t = w_in[0].T, m_w_in[0].T, v_w_in[0].T
    assert d == 8 * LANES
    pack1 = jnp.concatenate([c.reshape(d // LANES, LANES), conv_w[0].reshape(-1, LANES)], axis=0)
    pack1 = jnp.pad(pack1, ((0, 16 - pack1.shape[0]), (0, 0)))
    b_ada_sh = lax.dynamic_slice(b_ada, (0, j_me * mod_sh), (1, mod_sh))
    g1, mod, w_in_g, later = _startup(pack1, w_ada[0], b_ada_sh, _cast_into_block(pos, w_in_t, "cast_w_in"),
                                      [w_out[0], w_ffn_in[0], w_ffn_out[0]])
    c_all = g1[:, :d // LANES, :].reshape(N_DEV, d)
    cw_rows = 3 * cw_sh // LANES
    conv_w_full = jnp.concatenate(
        [g1[2 * j, d // LANES:d // LANES + cw_rows, :].reshape(3, cw_sh) for j in range(N_CHIP)], axis=1)
    sh1, sc1, ga1, sh2, sc2, ga2 = [(mod, k) for k in range(6)]
    w_in_tf = w_in_g.reshape(zw, d)

    (z, h1), later = _in_proj(xs, g_mix, sc1, sh1, w_in_tf, b_in, min(t, 2048), zw // 5, ride=_x_gather_ici(later))
    (attn, merged), later = _mixer_fwd(z, sinks, conv_w_full, d, ride=_x_gather_d2d(later))
    w_out_f = later[0].reshape(d, d)
    w_ffn_in_f = later[1]
    w_ffn_out_f = later[2].reshape(ff, d)
    tml = min(t, 1024)
    y1, x1, h2 = _out_proj_fwd(merged, w_out_f, xs, ga1, g_ffn, sc2, sh2, tml)
    gu, act = _ffn_in_fwd(h2, w_ffn_in_f, ff, tml, ff // 2)
    dx2, dy2, st_loss = _ffn_out_loss(act, w_ffn_out_f, x1, tgt, ga2, g_final.reshape(1, d), tml)

    dgu = _ffn_out_bwd(dy2, w_ffn_out_f, gu, tm, ff)
    tk = min(t, 2048)
    dw_ffn_out, _ = _wgrad(
        act, dy2, pl.BlockSpec((tk, ff // 2), lambda m, k: (k, m)), pl.BlockSpec((tk, d), lambda m, k: (k, 0)),
        pl.BlockSpec((ff // 2, d), lambda m, k: (m, 0)), SDS((ff, d), F32), (2, t // tk), "wgrad_ffn_out")
    dx1, dy1, st_ffn = _ffn_in_bwd(dgu, w_ffn_in_f, x1, dx2, y1, g_ffn, sc2, ga1, tm)
    dw_ffn_in, _ = _wgrad(
        h2, dgu, pl.BlockSpec((tk, d), lambda n, k: (k, 0)),
        pl.BlockSpec((None, tk, ff // 2), lambda n, k: (n // 2, k, n % 2)),
        pl.BlockSpec((None, d, ff // 2), lambda n, k: (n, 0, 0)), SDS((N_CHIP, d, ff // 2), F32),
        (N_CHIP, t // tk), "wgrad_ffn_in")
    dw_out, _ = _wgrad(
        merged, dy1, pl.BlockSpec((tk, d), lambda m, k: (k, 0)), pl.BlockSpec((tk, d), lambda m, k: (k, 0)),
        pl.BlockSpec((d, d), lambda m, k: (0, 0)), SDS((d, d), F32), (1, t // tk), "wgrad_out")

    early = [[g.reshape(N_CHIP, -1, g.shape[-1]) for g in pair] for pair in (dw_out, dw_ffn_in, dw_ffn_out)]
    early_names = ["w_out", "w_ffn_in", "w_ffn_out"]
    dmerged, _ = _out_proj_bwd(dy1, w_out_f, tml)
    (dz, dkv_shifted, db_z, db_kv, dcw, dsk), terms = _mixer_bwd(
        z, dmerged, attn, sinks, conv_w_full, d, ride=_x_reduce([e[0] for e in early], [e[1] for e in early]))
    dkv = dkv_shifted[BLOCK:BLOCK + t]
    fulls = [_sum_terms(pos, e[0], s, r, "sum_terms_" + nm)
             for e, s, r, nm in zip(early, terms[:3], terms[3:], early_names)]
    dw_in_t, (g_w_out, g_w_ffn_in, g_w_ffn_out) = _wgrad_in(dz, dkv, h1, tk, ride=_x_pair_exchange(fulls))
    dw_in_t = [g.reshape(N_CHIP, zw // N_CHIP, d) for g in dw_in_t]

    (grad_x, st_in), (from_sib, from_far) = _in_proj_bwd(dz, dkv, w_in_tf, xs, dx1, g_mix, sc1, tm,
                                                         ride=_x_reduce([dw_in_t[0]], [dw_in_t[1]]))
    g_w_in_half = _sum_terms(pos, dw_in_t[0], from_sib, from_far, "sum_terms_w_in")

    dmod = jnp.concatenate([st_in[0:1], st_in[1:2], st_ffn[3:4], st_ffn[0:1], st_ffn[1:2], st_loss[0:1]], axis=1)
    db_in = jnp.concatenate([db_z[0:1, :d], db_kv[0:1], db_z[0:1, d + kvw2:]], axis=1)
    seg = [dmod, st_in[2:3], db_in, dsk[0:1], dcw[0:3].reshape(1, 3 * d), st_ffn[2:3], st_loss[1:2],
           st_loss[2:3, :LANES]]
    sizes = [s.shape[1] for s in seg]
    pack2 = _to_lanes(jnp.concatenate(seg, axis=1))
    packs, g_w_in_t = _tail_exchange(pack2, g_w_in_half)
    tot = _pack_sum(packs).reshape(-1)
    offs = [sum(sizes[:k]) for k in range(len(sizes))]
    gb_ada, gg_mix, gb_in, gsinks, gcw, gg_ffn, gg_final, loss_v = [tot[o:o + s] for o, s in zip(offs, sizes)]
    loss = loss_v[0]
    gsinks = gsinks[:sinks.shape[1]]
    gcw_sh = lax.dynamic_slice(gcw.reshape(3, d), (0, j_me * cw_sh), (3, cw_sh))

    dmod_all = packs[:, :n_mod * d // LANES, :].reshape(N_DEV, n_mod * d)
    g_w_ada = _ada_wgrad(c_all, lax.dynamic_slice(dmod_all, (0, j_me * mod_sh), (N_DEV, mod_sh)))

    out_g, out_d, out_m, out_v = {}, {}, {}, {}
    big = {"w_ada": (w_ada[0], g_w_ada, m_w_ada[0], v_w_ada[0]),
           "w_out": (w_out[0], g_w_out, m_w_out[0], v_w_out[0]),
           "w_ffn_in": (w_ffn_in[0], g_w_ffn_in, m_w_ffn_in[0], v_w_ffn_in[0]),
           "w_ffn_out": (w_ffn_out[0], g_w_ffn_out, m_w_ffn_out[0], v_w_ffn_out[0])}
    for nm, (w, g, m, v) in big.items():
        out_g[nm], out_d[nm], out_m[nm], out_v[nm] = [o[None] for o in _adamw(w, g, m, v, "adamw_" + nm)]
    out_g["w_in"], out_d["w_in"], out_m["w_in"], out_v["w_in"] = [
        o.T[None] for o in _adamw(w_in_t, g_w_in_t, m_w_in_t, v_w_in_t, "adamw_w_in")]
    small = {"b_ada": (b_ada, gb_ada, m_b_ada, v_b_ada), "g_mix": (g_mix, gg_mix, m_g_mix, v_g_mix),
             "b_in": (b_in, gb_in, m_b_in, v_b_in), "sinks": (sinks, gsinks, m_sinks, v_sinks),
             "conv_w": (conv_w, gcw_sh, m_conv_w, v_conv_w), "g_ffn": (g_ffn, gg_ffn, m_g_ffn, v_g_ffn),
             "g_final": (g_final, gg_final, m_g_final, v_g_final)}
    def two_d(a):
        return a.reshape(-1, a.shape[-1])

    s_out = _adamw_small([tuple(two_d(a.reshape(w.shape)) for a in (w, g, m, v)) for w, g, m, v in small.values()])
    for (nm, (w, g, _, _)), res in zip(small.items(), s_out):
        out_g[nm] = g.reshape(w.shape)
        out_d[nm], out_m[nm], out_v[nm] = [o.reshape(w.shape) for o in res]

    order = ["w_ada", "b_ada", "g_mix", "w_in", "b_in", "sinks", "conv_w", "w_out", "g_ffn", "w_ffn_in", "w_ffn_out",
             "g_final"]
    return (loss, grad_x[None], *[out_g[k] for k in order], *[out_d[k] for k in order],
            *[out_m[k] for k in order], *[out_v[k] for k in order])
```

```python
import functools

import jax
import jax.numpy as jnp
from jax import lax
from jax.experimental import pallas as pl
from jax.experimental.pallas import tpu as pltpu

F32 = jnp.float32
BF16 = jnp.bfloat16
EPS = 1e-6
HEAD_DIM = 64
GROUP = 8
BLOCK = 128
LANES = 128
SUBLANES_BF16 = 16
N_DEV = 8
N_CHIP = 4
VMEM_LIMIT = 56 * 1024 * 1024
MESH = pl.DeviceIdType.MESH

ADAM_LR = 0.001
ADAM_B1 = 0.9
ADAM_B2 = 0.999
ADAM_EPS = 1e-08
ADAM_WD = 0.01
ADAM_STEP = 10

SDS = jax.ShapeDtypeStruct
ANY = pl.BlockSpec(memory_space=pl.ANY)
VMEM_SPEC = pl.BlockSpec(memory_space=pltpu.VMEM)
SMEM_SPEC = pl.BlockSpec(memory_space=pltpu.SMEM)


def _params(*sem):
    return pltpu.CompilerParams(dimension_semantics=sem, vmem_limit_bytes=VMEM_LIMIT)


def _vec(v, d):
    arr, k = v if isinstance(v, tuple) else (v, 0)
    return arr, pl.BlockSpec((1, d), lambda *_: (0, k))


def _mesh_pos():
    return lax.axis_index("x"), lax.axis_index("y"), lax.axis_index("c")


def _row_tile(rows, cols, itemsize=4, budget=1 << 20, mult=8):
    best = None
    for t in range(mult, rows + 1, mult):
        if rows % t == 0 and t * cols * itemsize <= budget:
            best = t
    if best is None:
        best = rows
    return best


def _gather_all(v_ref, out_ref, send_sems, recv_sems, local_sem):
    x, y, c = _mesh_pos()
    me = 4 * x + 2 * y + c
    mine = pltpu.make_async_copy(v_ref, out_ref.at[me], local_sem)
    mine.start()
    peers = []
    for k in range(1, N_DEV):
        px = 1 - x if k & 4 else x
        py = 1 - y if k & 2 else y
        pc = 1 - c if k & 1 else c
        peers.append((px, py, pc))

    def copy(k, block):
        return pltpu.make_async_remote_copy(
            src_ref=v_ref, dst_ref=out_ref.at[block], send_sem=send_sems.at[k], recv_sem=recv_sems.at[k],
            device_id=peers[k], device_id_type=MESH)

    sends = [copy(k, me) for k in range(N_DEV - 1)]
    for cp in sends:
        cp.start()
    for k, (px, py, pc) in enumerate(peers):
        copy(k, 4 * px + 2 * py + pc).wait_recv()
    for cp in sends:
        cp.wait_send()
    mine.wait()


def _small_sems():
    return [pltpu.SemaphoreType.DMA((N_DEV - 1,)), pltpu.SemaphoreType.DMA((N_DEV - 1,)), pltpu.SemaphoreType.DMA]


def _tail_exchange(pack, full):
    def body(pack_ref, full_unused, packs_ref, full_ref, s1, r1, l1, send_sem, recv_sem):
        del full_unused
        x, y, c = _mesh_pos()
        half = full_ref.shape[0] // 2
        rows = pl.ds(pl.multiple_of(c * half, 8), half)
        swap = pltpu.make_async_remote_copy(
            src_ref=full_ref.at[rows], dst_ref=full_ref.at[rows], send_sem=send_sem, recv_sem=recv_sem,
            device_id=(x, y, 1 - c), device_id_type=MESH)
        swap.start()
        _gather_all(pack_ref, packs_ref, s1, r1, l1)
        swap.wait()

    return pl.pallas_call(
        body, name="tail_exchange", out_shape=[SDS((N_DEV,) + pack.shape, pack.dtype), SDS(full.shape, full.dtype)],
        in_specs=[VMEM_SPEC, ANY], out_specs=[VMEM_SPEC, ANY], input_output_aliases={1: 1},
        scratch_shapes=_small_sems() + [pltpu.SemaphoreType.DMA, pltpu.SemaphoreType.DMA])(pack, full)


def _other_chips(x, y):
    return [(1 - x, y), (x, 1 - y), (1 - x, 1 - y)]


def _startup(pack, w_ada_sh, b_ada_sh, w_buf, later):
    d, n = w_ada_sh.shape
    kc = d // LANES
    n_l = len(later)
    chunk_rows = [_row_tile(a.shape[0], a.shape[1], budget=3 << 19, mult=SUBLANES_BF16) for a in later]

    def body(*refs):
        pack_ref, wa_hbm, ba_ref, w_in_unused = refs[:4]
        later_src = refs[4:4 + n_l]
        packs_ref, mine_ref, w_ref = refs[4 + n_l:7 + n_l]
        later_dst = refs[7 + n_l:7 + 2 * n_l]
        wa_scr, mod_scr, mod_ref = refs[7 + 2 * n_l:10 + 2 * n_l]
        f32_bufs = refs[10 + 2 * n_l:10 + 3 * n_l]
        bf16_bufs = refs[10 + 3 * n_l:10 + 4 * n_l]
        (s1, r1, l1, s2, r2, l2, send_sems, recv_sems, fsend_sems, frecv_sems, relay_send, relay_recv, wa_sem,
         cast_sems) = refs[10 + 4 * n_l:]
        del w_in_unused
        x, y, c = _mesh_pos()
        j_me = 2 * x + y
        chips = _other_chips(x, y)
        half = w_ref.shape[1] // 2

        def rows_of(which):
            return pl.ds(pl.multiple_of(which * half, SUBLANES_BF16), half)

        def copy(p, block, rows, over_ici):
            sems = (send_sems, recv_sems) if over_ici else (fsend_sems, frecv_sems)
            return pltpu.make_async_remote_copy(
                src_ref=w_ref.at[block, rows], dst_ref=w_ref.at[block, rows], send_sem=sems[0].at[p],
                recv_sem=sems[1].at[p], device_id=(*chips[p], c) if over_ici else (x, y, 1 - c), device_id_type=MESH)

        def block_of(p):
            return 2 * chips[p][0] + chips[p][1]

        def relay(q, block):
            rows = pl.ds(pl.multiple_of(c * half + q * (half // 2), SUBLANES_BF16), half // 2)
            return pltpu.make_async_remote_copy(
                src_ref=w_ref.at[block, rows], dst_ref=w_ref.at[block, rows], send_sem=relay_send.at[q],
                recv_sem=relay_recv.at[q], device_id=(*chips[1 - q], c), device_id_type=MESH)

        load_wa = pltpu.make_async_copy(wa_hbm, wa_scr, wa_sem)
        load_wa.start()
        _gather_all(pack_ref, packs_ref, s1, r1, l1)
        sends = [copy(p, j_me, rows_of(c), True) for p in range(2)]
        for cp in sends:
            cp.start()
        load_wa.wait()
        acc = jnp.zeros((N_DEV, n), F32)
        for k in range(kc):
            ck = packs_ref[:, k, :]
            sk = (ck * jax.nn.sigmoid(ck)).astype(BF16)
            acc = acc + jnp.dot(sk, wa_scr[k * LANES:(k + 1) * LANES, :].astype(BF16), preferred_element_type=F32)
        mod_scr[...] = acc + ba_ref[...]
        _gather_all(mod_scr, mod_ref, s2, r2, l2)
        for j in range(N_CHIP):
            mine_ref[:, j * n:(j + 1) * n] = mod_ref[2 * j, pl.ds(4 * x + 2 * y + c, 1), :]
        passed = []
        for q in range(2):
            copy(q, block_of(q), rows_of(c), True).wait_recv()
            for cp in (relay(q, block_of(q)), copy(q, block_of(q), rows_of(c), False)):
                cp.start()
                passed.append(cp)
        for src, dst, fbuf, bbuf in zip(later_src, later_dst, f32_bufs, bf16_bufs):
            cr = fbuf.shape[0]
            for k in range(src.shape[0] // cr):
                rows = pl.ds(k * cr, cr)
                cin = pltpu.make_async_copy(src.at[rows], fbuf, cast_sems.at[0])
                cin.start()
                cin.wait()
                bbuf[...] = fbuf[...].astype(BF16)
                cout = pltpu.make_async_copy(bbuf, dst.at[j_me, rows], cast_sems.at[1])
                cout.start()
                cout.wait()
        for q in range(2):
            relay(q, block_of(2)).wait_recv()
        fw = copy(2, block_of(2), rows_of(c), False)
        fw.start()
        for p in range(3):
            copy(p, block_of(p), rows_of(1 - c), False).wait_recv()
        for cp in sends + passed + [fw]:
            cp.wait_send()

    res = pl.pallas_call(
        body, name="startup",
        out_shape=[SDS((N_DEV,) + pack.shape, F32), SDS((1, N_CHIP * n), F32), SDS(w_buf.shape, w_buf.dtype)]
        + [SDS((N_CHIP,) + a.shape, BF16) for a in later],
        in_specs=[VMEM_SPEC, ANY, VMEM_SPEC, ANY] + [ANY] * n_l, out_specs=[VMEM_SPEC, VMEM_SPEC, ANY] + [ANY] * n_l,
        input_output_aliases={3: 2},
        scratch_shapes=[pltpu.VMEM((d, n), F32), pltpu.VMEM((N_DEV, n), F32), pltpu.VMEM((N_DEV, N_DEV, n), F32)]
        + [pltpu.VMEM((cr, a.shape[1]), F32) for cr, a in zip(chunk_rows, later)]
        + [pltpu.VMEM((cr, a.shape[1]), BF16) for cr, a in zip(chunk_rows, later)]
        + _small_sems() + _small_sems()
        + [pltpu.SemaphoreType.DMA((3,))] * 4 + [pltpu.SemaphoreType.DMA((2,))] * 2 + [pltpu.SemaphoreType.DMA]
        + [pltpu.SemaphoreType.DMA((2,))],
        compiler_params=pltpu.CompilerParams(vmem_limit_bytes=VMEM_LIMIT),
    )(pack, w_ada_sh, b_ada_sh, w_buf, *later)
    return res[0], res[1], res[2], list(res[3:])


class _Exchange:
    def __init__(self, operands, out_shape, in_place, n_sems, copies):
        self.operands, self.out_shape, self.in_place, self.n_sems, self.copies = (
            list(operands), list(out_shape), in_place, n_sems, copies)

    def sems(self):
        return [pltpu.SemaphoreType.DMA((self.n_sems,)), pltpu.SemaphoreType.DMA((self.n_sems,))]


def _x_gather_ici(bufs):
    def copies(ins, outs, send_sems, recv_sems):
        x, y, c = _mesh_pos()
        chips = _other_chips(x, y)
        out = []
        for w in range(len(outs)):
            half = outs[w].shape[1] // 2
            rows = pl.ds(pl.multiple_of(c * half, SUBLANES_BF16), half)
            for p in range(3):
                out.append(pltpu.make_async_remote_copy(
                    src_ref=outs[w].at[2 * x + y, rows], dst_ref=outs[w].at[2 * x + y, rows],
                    send_sem=send_sems.at[w * 3 + p], recv_sem=recv_sems.at[w * 3 + p],
                    device_id=(*chips[p], c), device_id_type=MESH))
        return out

    return _Exchange(bufs, [SDS(b.shape, b.dtype) for b in bufs], True, 3 * len(bufs), copies)


def _x_gather_d2d(bufs):
    def copies(ins, outs, send_sems, recv_sems):
        x, y, c = _mesh_pos()
        chips = _other_chips(x, y)
        out = []
        for w in range(len(outs)):
            half = outs[w].shape[1] // 2
            rows = pl.ds(pl.multiple_of(c * half, SUBLANES_BF16), half)
            for p in range(3):
                block = 2 * chips[p][0] + chips[p][1]
                out.append(pltpu.make_async_remote_copy(
                    src_ref=outs[w].at[block, rows], dst_ref=outs[w].at[block, rows],
                    send_sem=send_sems.at[w * 3 + p], recv_sem=recv_sems.at[w * 3 + p],
                    device_id=(x, y, 1 - c), device_id_type=MESH))
        return out

    return _Exchange(bufs, [SDS(b.shape, b.dtype) for b in bufs], True, 3 * len(bufs), copies)


N_REMOTE = 6


def _x_reduce(grads32, grads16):
    n_w = len(grads32)

    def copies(ins, outs, send_sems, recv_sems):
        g32, g16 = ins[:n_w], ins[n_w:]
        from_sib, from_far = outs[:n_w], outs[n_w:]
        x, y, c = _mesh_pos()
        chips = _other_chips(x, y)
        out = []
        for w in range(n_w):
            half = g32[w].shape[1] // 2
            k0 = w * (N_REMOTE + 1)
            out.append(pltpu.make_async_remote_copy(
                src_ref=g32[w].at[2 * x + y, pl.ds(pl.multiple_of((1 - c) * half, SUBLANES_BF16), half), :],
                dst_ref=from_sib[w], send_sem=send_sems.at[k0], recv_sem=recv_sems.at[k0],
                device_id=(x, y, 1 - c), device_id_type=MESH))
            for p in range(3):
                for f in range(2):
                    tc = c if f == 0 else 1 - c
                    k = 2 * p + f
                    out.append(pltpu.make_async_remote_copy(
                        src_ref=g16[w].at[2 * chips[p][0] + chips[p][1],
                                          pl.ds(pl.multiple_of(tc * half, SUBLANES_BF16), half), :],
                        dst_ref=from_far[w].at[k], send_sem=send_sems.at[k0 + 1 + k], recv_sem=recv_sems.at[k0 + 1 + k],
                        device_id=(*chips[p], tc), device_id_type=MESH))
        return out

    shapes = ([SDS((g.shape[1] // 2, g.shape[2]), g.dtype) for g in grads32]
              + [SDS((N_REMOTE, g.shape[1] // 2, g.shape[2]), g.dtype) for g in grads16])
    return _Exchange(list(grads32) + list(grads16), shapes, False, (N_REMOTE + 1) * n_w, copies)


def _x_pair_exchange(fulls):
    def copies(ins, outs, send_sems, recv_sems):
        x, y, c = _mesh_pos()
        out = []
        for w in range(len(outs)):
            half = outs[w].shape[0] // 2
            rows = pl.ds(pl.multiple_of(c * half, 8), half)
            out.append(pltpu.make_async_remote_copy(
                src_ref=outs[w].at[rows], dst_ref=outs[w].at[rows], send_sem=send_sems.at[w],
                recv_sem=recv_sems.at[w], device_id=(x, y, 1 - c), device_id_type=MESH))
        return out

    return _Exchange(fulls, [SDS(f.shape, f.dtype) for f in fulls], True, len(fulls), copies)


def _pallas(body, *, name, grid, in_specs, out_specs, out_shape, args, scratch=(), sem=None, ride=None):
    single = not isinstance(out_specs, (list, tuple))
    out_specs_l = [out_specs] if single else list(out_specs)
    out_shape_l = [out_shape] if single else list(out_shape)
    n_in, n_out, n_scr = len(in_specs), len(out_specs_l), len(scratch)
    if ride is None:
        res = pl.pallas_call(body, name=name, grid=grid, in_specs=list(in_specs), out_specs=out_specs,
                             out_shape=out_shape, scratch_shapes=list(scratch), compiler_params=_params(*sem))(*args)
        return res, None
    n_x, n_xo = len(ride.operands), len(ride.out_shape)

    def full_body(*refs):
        ins, x_ins = refs[:n_in], refs[n_in:n_in + n_x]
        outs = refs[n_in + n_x:n_in + n_x + n_out]
        x_outs = refs[n_in + n_x + n_out:n_in + n_x + n_out + n_xo]
        rest = refs[n_in + n_x + n_out + n_xo:]
        scr, (send_sems, recv_sems) = rest[:n_scr], rest[n_scr:]
        first = functools.reduce(jnp.logical_and, [pl.program_id(a) == 0 for a in range(len(grid))])
        last = functools.reduce(jnp.logical_and, [pl.program_id(a) == grid[a] - 1 for a in range(len(grid))])

        @pl.when(first)
        def _():
            for cp in ride.copies(x_ins, x_outs, send_sems, recv_sems):
                cp.start()

        body(*ins, *outs, *scr)

        @pl.when(last)
        def _():
            for cp in ride.copies(x_ins, x_outs, send_sems, recv_sems):
                cp.wait()

    res = pl.pallas_call(
        full_body, name=name, grid=grid, in_specs=list(in_specs) + [ANY] * n_x,
        out_specs=out_specs_l + [ANY] * n_xo, out_shape=out_shape_l + ride.out_shape,
        input_output_aliases={n_in + k: n_out + k for k in range(n_x)} if ride.in_place else {},
        scratch_shapes=list(scratch) + ride.sems(),
        compiler_params=_params(*(["arbitrary"] * len(grid))))(*args, *ride.operands)
    own = res[0] if single else list(res[:n_out])
    return own, list(res[n_out:])


def _cast_into_block(pos, w, name):
    rows, cols = w.shape
    tr = _row_tile(rows, cols, mult=SUBLANES_BF16)

    def body(pos_ref, w_ref, o_ref):
        del pos_ref
        o_ref[...] = w_ref[...].astype(BF16)

    return pl.pallas_call(
        body, name=name,
        grid_spec=pltpu.PrefetchScalarGridSpec(
            num_scalar_prefetch=1, grid=(rows // tr,),
            in_specs=[pl.BlockSpec((tr, cols), lambda i, pos_ref: (i, 0))],
            out_specs=pl.BlockSpec((None, tr, cols), lambda i, pos_ref: (pos_ref[1], i, 0))),
        out_shape=SDS((N_CHIP, rows, cols), BF16), compiler_params=_params("parallel"))(pos, w)


def _sum_terms(pos, grad, from_sib, from_far, name):
    _, rows, cols = grad.shape
    half = rows // 2
    tr = _row_tile(half, cols, mult=SUBLANES_BF16)
    nblk = half // tr

    def body(pos_ref, g_ref, s_ref, r_ref, o_ref):
        del pos_ref
        acc = g_ref[...] + s_ref[...]
        for k in range(N_REMOTE):
            acc = acc + r_ref[k].astype(F32)
        o_ref[...] = acc

    return pl.pallas_call(
        body, name=name,
        grid_spec=pltpu.PrefetchScalarGridSpec(
            num_scalar_prefetch=1, grid=(nblk,),
            in_specs=[pl.BlockSpec((None, tr, cols), lambda i, pos_ref: (pos_ref[1], pos_ref[0] * nblk + i, 0)),
                      pl.BlockSpec((tr, cols), lambda i, pos_ref: (i, 0)),
                      pl.BlockSpec((N_REMOTE, tr, cols), lambda i, pos_ref: (0, i, 0))],
            out_specs=pl.BlockSpec((tr, cols), lambda i, pos_ref: (pos_ref[0] * nblk + i, 0))),
        out_shape=SDS((rows, cols), F32),
        compiler_params=_params("parallel"),
    )(pos, grad, from_sib, from_far)


def _adamw(w, g, m, v, name):
    rows, cols = w.shape
    tr = _row_tile(rows, cols, budget=1 << 21)

    def body(w_ref, g_ref, m_ref, v_ref, go_ref, d_ref, nm_ref, nv_ref):
        go_ref[...] = g_ref[...]
        _adamw_update(w_ref, g_ref, m_ref, v_ref, d_ref, nm_ref, nv_ref)

    spec = pl.BlockSpec((tr, cols), lambda i: (i, 0))
    return pl.pallas_call(body, name=name, grid=(rows // tr,), in_specs=[spec] * 4, out_specs=[spec] * 4,
                          out_shape=[SDS((rows, cols), F32)] * 4, compiler_params=_params("parallel"))(w, g, m, v)


def _adamw_update(w_ref, g_ref, m_ref, v_ref, d_ref, nm_ref, nv_ref):
    gg = g_ref[...]
    nm = ADAM_B1 * m_ref[...] + (1.0 - ADAM_B1) * gg
    nv = ADAM_B2 * v_ref[...] + (1.0 - ADAM_B2) * (gg * gg)
    m_hat = nm / (1.0 - ADAM_B1 ** ADAM_STEP)
    v_hat = nv / (1.0 - ADAM_B2 ** ADAM_STEP)
    d_ref[...] = -ADAM_LR * (m_hat / (jnp.sqrt(v_hat) + ADAM_EPS) + ADAM_WD * w_ref[...])
    nm_ref[...] = nm
    nv_ref[...] = nv


def _adamw_small(params):
    n_p = len(params)

    def body(*refs):
        ins, outs = refs[:4 * n_p], refs[4 * n_p:]
        for k in range(n_p):
            _adamw_update(*ins[4 * k:4 * k + 4], *outs[3 * k:3 * k + 3])

    flat = [a for tup in params for a in tup]
    res = pl.pallas_call(
        body, name="adamw_small", in_specs=[VMEM_SPEC] * (4 * n_p), out_specs=[VMEM_SPEC] * (3 * n_p),
        out_shape=[SDS(tup[0].shape, F32) for tup in params for _ in range(3)])(*flat)
    return [res[3 * k:3 * k + 3] for k in range(n_p)]


def _pack_sum(gathered):
    _, rows, cols = gathered.shape

    def body(g_ref, o_ref):
        acc = g_ref[0]
        for d in range(1, N_DEV):
            acc = acc + g_ref[d]
        o_ref[...] = acc

    return pl.pallas_call(body, name="pack_sum", in_specs=[VMEM_SPEC], out_specs=VMEM_SPEC,
                          out_shape=SDS((rows, cols), F32))(gathered)


def _ada_wgrad(c_all, dmod_sh):
    d = c_all.shape[1]
    n = dmod_sh.shape[1]
    tn = 512

    def body(c_ref, g_ref, o_ref):
        cc = c_ref[...]
        s = cc * jax.nn.sigmoid(cc)
        o_ref[...] = lax.dot_general(s, g_ref[...], (((0,), (0,)), ((), ())), preferred_element_type=F32,
                                     precision=lax.Precision.HIGHEST)

    return pl.pallas_call(
        body, name="ada_wgrad", grid=(n // tn,),
        in_specs=[pl.BlockSpec((N_DEV, d), lambda j: (0, 0)), pl.BlockSpec((N_DEV, tn), lambda j: (0, j))],
        out_specs=pl.BlockSpec((d, tn), lambda j: (0, j)),
        out_shape=SDS((d, n), F32), compiler_params=_params("parallel"))(c_all, dmod_sh)


def _rms(xf):
    return lax.rsqrt(jnp.mean(xf * xf, axis=-1, keepdims=True) + EPS)


def _in_proj(x, g, sc, sh, wt, b, tm, tn, ride=None):
    t, d = x.shape
    n = wt.shape[0]

    def body(x_ref, g_ref, sc_ref, sh_ref, w_ref, b_ref, z_ref, h_ref):
        @pl.when(pl.program_id(1) == 0)
        def _():
            xf = x_ref[...]
            h_ref[...] = ((xf * _rms(xf) * g_ref[...]) * (1.0 + sc_ref[...]) + sh_ref[...]).astype(BF16)

        acc = lax.dot_general(h_ref[...], w_ref[...], (((1,), (1,)), ((), ())), preferred_element_type=F32)
        z_ref[...] = (acc + b_ref[...]).astype(BF16)

    row = pl.BlockSpec((tm, d), lambda i, j: (i, 0))
    vecs, vec_specs = zip(*[_vec(v, d) for v in (g, sc, sh)])
    return _pallas(
        body, name="in_proj", grid=(t // tm, n // tn),
        in_specs=[row, *vec_specs, pl.BlockSpec((tn, d), lambda i, j: (j, 0)),
                  pl.BlockSpec((1, tn), lambda i, j: (0, j))],
        out_specs=[pl.BlockSpec((tm, tn), lambda i, j: (i, j)), row],
        out_shape=[SDS((t, n), BF16), SDS((t, d), BF16)], args=(x, *vecs, wt, b),
        sem=("parallel", "arbitrary"), ride=ride)


def _segments(d, kvw2):
    o = d + kvw2
    names = ("cb", "cc", "cx", "ga", "gc")
    seg = {nm: slice(o + k * d, o + (k + 1) * d) for k, nm in enumerate(names)}
    seg["q"], seg["kv"] = slice(0, d), slice(d, o)
    return seg


def _attn_masks():
    rows = 4 * BLOCK
    r = lax.broadcasted_iota(jnp.int32, (rows, 2 * BLOCK), 0) & (BLOCK - 1)
    col = lax.broadcasted_iota(jnp.int32, (rows, 2 * BLOCK), 1)
    return (col > r) & (col <= r + BLOCK), col


def _kv_variants(kv, n_kv_w):
    assert n_kv_w == LANES
    kb, vb = kv[:, :LANES] * (HEAD_DIM ** -0.5), kv[:, LANES:]
    kr, vr = pltpu.roll(kb, HEAD_DIM, 1), pltpu.roll(vb, HEAD_DIM, 1)
    lane = lax.broadcasted_iota(jnp.int32, kb.shape, 1)
    lo = lane < HEAD_DIM
    zero = jnp.zeros_like(kb)
    k_eff = [[None, None], [None, None]]
    v_eff = [[None, None], [None, None]]
    for h in range(2):
        for e in range(2):
            ksrc, vsrc = (kb, vb) if e == h else (kr, vr)
            keep = lo if e == 0 else jnp.logical_not(lo)
            k_eff[h][e] = jnp.where(keep, ksrc, zero)
            v_eff[h][e] = jnp.where(keep, vsrc, zero)
    return k_eff, v_eff


def _sink_column(sinks_ref, h, e):
    rowblk = lax.broadcasted_iota(jnp.int32, (4 * BLOCK, 1), 0) // BLOCK
    col = jnp.zeros((4 * BLOCK, 1), F32)
    for j in range(4):
        col = jnp.where(rowblk == j, sinks_ref[0, GROUP * h + 2 * j + e], col)
    return col


def _softmax_sink(s, valid, sink):
    s = jnp.where(valid, s, -jnp.inf)
    m = jnp.maximum(jnp.max(s, axis=-1, keepdims=True), sink)
    p = jnp.exp(s - m)
    psink = jnp.exp(sink - m)
    den = jnp.sum(p, axis=-1, keepdims=True) + psink
    inv = 1.0 / den
    return p * inv, psink * inv


def _shift_down(a, s, prev):
    rows = a.shape[0]
    out = pltpu.roll(a, s, 0)
    row = lax.broadcasted_iota(jnp.int32, a.shape, 0)
    for t in range(s):
        out = jnp.where(row == t, prev[SUBLANES_BF16 - s + t:SUBLANES_BF16 - s + t + 1, :], out)
    del rows
    return out


def _shift_up(a, s, nxt):
    rows = a.shape[0]
    out = pltpu.roll(a, rows - s, 0)
    row = lax.broadcasted_iota(jnp.int32, a.shape, 0)
    for t in range(s):
        out = jnp.where(row == rows - s + t, nxt[t:t + 1, :], out)
    return out


def _stack_pairs(ref, h, rows=slice(None)):
    return jnp.concatenate([ref[rows, (4 * h + j) * LANES:(4 * h + j + 1) * LANES] for j in range(4)], axis=0)


FWD_BLOCKS = 4


def _mixer_fwd(z, sinks, conv_w, d, ride=None):
    t, zw = z.shape
    kvw2 = zw - 6 * d
    tq = FWD_BLOCKS * BLOCK
    halo = tq // SUBLANES_BF16
    seg = _segments(d, kvw2)

    def body(z_ref, kvp_ref, prev_ref, sinks_ref, cw_ref, attn_ref, merged_ref):
        n = pl.program_id(0)
        band, col = _attn_masks()
        lane_lo = lax.broadcasted_iota(jnp.int32, (4 * BLOCK, LANES), 1) < HEAD_DIM
        for b in range(FWD_BLOCKS):
            rows = slice(b * BLOCK, (b + 1) * BLOCK)
            before = slice((b - 1) * BLOCK, b * BLOCK)
            kv_prev = kvp_ref[...] if b == 0 else z_ref[before, seg["kv"]]
            kv = jnp.concatenate([kv_prev, z_ref[rows, seg["kv"]]], axis=0)
            k_eff, v_eff = _kv_variants(kv, kvw2 // 2)
            valid = band & ((n > 0) | (col >= BLOCK)) if b == 0 else band
            for h in range(2):
                q4 = _stack_pairs(z_ref, h, rows)
                o4 = jnp.zeros((4 * BLOCK, LANES), F32)
                inv = []
                for e in range(2):
                    s = lax.dot_general(q4, k_eff[h][e], (((1,), (1,)), ((), ())), preferred_element_type=F32)
                    sink = _sink_column(sinks_ref, h, e)
                    s = jnp.where(valid, s, -jnp.inf)
                    m = jnp.maximum(jnp.max(s, axis=-1, keepdims=True), sink)
                    p = jnp.exp(s - m)
                    inv.append(1.0 / (jnp.sum(p, axis=-1, keepdims=True) + jnp.exp(sink - m)))
                    o4 = o4 + jnp.dot(p.astype(BF16), v_eff[h][e], preferred_element_type=F32)
                o4 = o4 * jnp.where(lane_lo, inv[0], inv[1])
                for j in range(4):
                    attn_ref[rows, (4 * h + j) * LANES:(4 * h + j + 1) * LANES] = (
                        o4[j * BLOCK:(j + 1) * BLOCK].astype(BF16))
            cb = z_ref[rows, seg["cb"]].astype(F32)
            p_in = z_ref[rows, seg["cc"]].astype(F32) * z_ref[rows, seg["cx"]].astype(F32)
            if b == 0:
                prev = jnp.where(n > 0, prev_ref[:, seg["cc"]].astype(F32) * prev_ref[:, seg["cx"]].astype(F32), 0.0)
            else:
                tail = slice(b * BLOCK - SUBLANES_BF16, b * BLOCK)
                prev = z_ref[tail, seg["cc"]].astype(F32) * z_ref[tail, seg["cx"]].astype(F32)
            cconv = (cw_ref[0:1, :] * _shift_down(p_in, 2, prev) + cw_ref[1:2, :] * _shift_down(p_in, 1, prev)
                     + cw_ref[2:3, :] * p_in)
            sa = jax.nn.sigmoid(z_ref[rows, seg["ga"]].astype(F32))
            sg = jax.nn.sigmoid(z_ref[rows, seg["gc"]].astype(F32))
            merged_ref[rows, :] = (sa * attn_ref[rows, :].astype(F32) + sg * (cb * cconv)).astype(BF16)

    blk = pl.BlockSpec((tq, d), lambda n: (n, 0))
    return _pallas(
        body, name="mixer_fwd", grid=(t // tq,),
        in_specs=[pl.BlockSpec((tq, zw), lambda n: (n, 0)),
                  pl.BlockSpec((BLOCK, kvw2), lambda n: (jnp.maximum(n * FWD_BLOCKS - 1, 0), d // kvw2)),
                  pl.BlockSpec((SUBLANES_BF16, zw), lambda n: (jnp.maximum(n * halo - 1, 0), 0)),
                  SMEM_SPEC, pl.BlockSpec((3, d), lambda n: (0, 0))],
        out_specs=[blk, blk],
        out_shape=[SDS((t, d), BF16), SDS((t, d), BF16)],
        args=(z, z, z, sinks, conv_w), sem=("parallel",), ride=ride)


def _out_proj_fwd(merged, w_out, x, ga1, g_ffn, sc2, sh2, tm):
    t, d = x.shape

    def body(m_ref, w_ref, x_ref, ga_ref, g_ref, sc_ref, sh_ref, y_ref, x1_ref, h_ref):
        y = jnp.dot(m_ref[...], w_ref[...], preferred_element_type=F32)
        x1 = x_ref[...] + ga_ref[...] * y
        y_ref[...] = y.astype(BF16)
        x1_ref[...] = x1
        h_ref[...] = ((x1 * _rms(x1) * g_ref[...]) * (1.0 + sc_ref[...]) + sh_ref[...]).astype(BF16)

    row = pl.BlockSpec((tm, d), lambda i: (i, 0))
    vecs, vec_specs = zip(*[_vec(v, d) for v in (ga1, g_ffn, sc2, sh2)])
    return pl.pallas_call(
        body, name="out_proj_fwd", grid=(t // tm,),
        in_specs=[row, pl.BlockSpec((d, d), lambda i: (0, 0)), row, *vec_specs],
        out_specs=[row, row, row],
        out_shape=[SDS((t, d), BF16), SDS((t, d), F32), SDS((t, d), BF16)],
        compiler_params=_params("parallel"))(merged, w_out, x, *vecs)


def _ffn_in_fwd(h2, w, ff, tm, tn):
    t, d = h2.shape
    nj = ff // tn
    assert w.shape == (2 * nj, d, tn)

    def body(h_ref, wg_ref, wu_ref, gu_ref, act_ref):
        hh = h_ref[...]
        g = jnp.dot(hh, wg_ref[...], preferred_element_type=F32)
        u = jnp.dot(hh, wu_ref[...], preferred_element_type=F32)
        sg = jax.nn.sigmoid(g)
        silu = g * sg
        gu_ref[0] = (u * (sg + silu * (1.0 - sg))).astype(BF16)
        gu_ref[1] = silu.astype(BF16)
        act_ref[...] = (silu * u).astype(BF16)

    return pl.pallas_call(
        body, name="ffn_in_fwd", grid=(nj, t // tm),
        in_specs=[pl.BlockSpec((tm, d), lambda j, i: (i, 0)), pl.BlockSpec((None, d, tn), lambda j, i: (j, 0, 0)),
                  pl.BlockSpec((None, d, tn), lambda j, i: (j + nj, 0, 0))],
        out_specs=[pl.BlockSpec((2, tm, tn), lambda j, i: (0, i, j)), pl.BlockSpec((tm, tn), lambda j, i: (i, j))],
        out_shape=[SDS((2, t, ff), BF16), SDS((t, ff), BF16)],
        compiler_params=_params("parallel", "parallel"))(h2, w, w)


def _ffn_out_loss(act, w, x1, target, ga2, g_final, tm):
    t, d = x1.shape
    ff = act.shape[1]

    def body(a_ref, w_ref, x1_ref, tg_ref, ga_ref, gf_ref, dx2_ref, dy2_ref, st_ref):
        @pl.when(pl.program_id(0) == 0)
        def _():
            st_ref[...] = jnp.zeros_like(st_ref)

        halves = [slice(k * (tm // 2), (k + 1) * (tm // 2)) for k in range(2)]
        y2s = [jnp.dot(a_ref[rows, :], w_ref[...], preferred_element_type=F32) for rows in halves]
        for rows, y2 in zip(halves, y2s):
            x2 = x1_ref[rows, :] + ga_ref[...] * y2
            r = _rms(x2)
            yn = x2 * r
            err = yn * gf_ref[...] - tg_ref[rows, :]
            loss = 0.5 * jnp.sum(jnp.mean(err * err, axis=-1, keepdims=True), axis=0, keepdims=True)
            dy = err * (1.0 / d)
            u = dy * gf_ref[...]
            dx2 = r * (u - yn * jnp.mean(u * yn, axis=-1, keepdims=True))
            dx2_ref[rows, :] = dx2
            dy2_ref[rows, :] = (ga_ref[...] * dx2).astype(BF16)
            st_ref[0:1, :] += jnp.sum(dx2 * y2, axis=0, keepdims=True)
            st_ref[1:2, :] += jnp.sum(dy * yn, axis=0, keepdims=True)
            st_ref[2:3, :] += jnp.broadcast_to(loss, (1, d))

    row = pl.BlockSpec((tm, d), lambda i: (i, 0))
    vecs, vec_specs = zip(*[_vec(v, d) for v in (ga2, g_final)])
    return pl.pallas_call(
        body, name="ffn_out_loss", grid=(t // tm,),
        in_specs=[pl.BlockSpec((tm, ff), lambda i: (i, 0)),
                  pl.BlockSpec((ff, d), lambda i: (0, 0), pipeline_mode=pl.Buffered(1)), row, row, *vec_specs],
        out_specs=[row, row, pl.BlockSpec((8, d), lambda i: (0, 0))],
        out_shape=[SDS((t, d), F32), SDS((t, d), BF16), SDS((8, d), F32)],
        compiler_params=_params("arbitrary"))(act, w, x1, target, *vecs)


def _ffn_out_bwd(dy2, w, gu, tm, tn):
    t, d = dy2.shape
    ff = w.shape[0]

    def body(dy_ref, w_ref, gu_ref, o_ref):
        dy = dy_ref[...]
        for lo in range(0, tn, 3 * LANES):
            cols = slice(lo, min(lo + 3 * LANES, tn))
            dact = lax.dot_general(dy, w_ref[cols, :], (((1,), (1,)), ((), ())), preferred_element_type=F32)
            o_ref[0, :, cols] = (dact * gu_ref[0, :, cols].astype(F32)).astype(BF16)
            o_ref[1, :, cols] = (dact * gu_ref[1, :, cols].astype(F32)).astype(BF16)

    gu_spec = pl.BlockSpec((2, tm, tn), lambda j, i: (0, i, j))
    return pl.pallas_call(
        body, name="ffn_out_bwd", grid=(ff // tn, t // tm),
        in_specs=[pl.BlockSpec((tm, d), lambda j, i: (i, 0)), pl.BlockSpec((tn, d), lambda j, i: (j, 0)), gu_spec],
        out_specs=gu_spec, out_shape=SDS((2, t, ff), BF16),
        compiler_params=_params("parallel", "parallel"))(dy2, w, gu)


def _wgrad(a, b, a_spec, b_spec, out_spec, out_shape, grid, name, ride=None):
    def body(a_ref, b_ref, o_ref, o16_ref):
        k = pl.program_id(len(grid) - 1)

        @pl.when(k == 0)
        def _():
            o_ref[...] = jnp.zeros_like(o_ref)

        o_ref[...] += lax.dot_general(a_ref[...], b_ref[...], (((0,), (0,)), ((), ())), preferred_element_type=F32)

        @pl.when(k == grid[-1] - 1)
        def _():
            o16_ref[...] = o_ref[...].astype(BF16)

    return _pallas(
        body, name=name, grid=grid, in_specs=[a_spec, b_spec], out_specs=[out_spec, out_spec],
        out_shape=[out_shape, SDS(out_shape.shape, BF16)], args=(a, b),
        sem=["parallel"] * (len(grid) - 1) + ["arbitrary"], ride=ride)


def _ffn_in_bwd(dgu, w, x1, dx2, y1, g_ffn, sc2, ga1, tm):
    t, d = x1.shape
    ff = dgu.shape[2]
    n_sh, _, sw = w.shape
    per = ff // sw
    nt = (((1,), (1,)), ((), ()))

    def body(a_ref, w_ref, x1_ref, dx2_ref, y1_ref, g_ref, sc_ref, ga_ref, dx1_ref, dy1_ref, st_ref):
        @pl.when(pl.program_id(0) == 0)
        def _():
            st_ref[...] = jnp.zeros_like(st_ref)

        dh = None
        for j in range(n_sh):
            part = lax.dot_general(a_ref[j // per, :, (j % per) * sw:(j % per + 1) * sw], w_ref[j], nt,
                                   preferred_element_type=F32)
            dh = part if dh is None else dh + part
        x1 = x1_ref[...]
        r = _rms(x1)
        xn = x1 * r
        g = g_ref[...]
        dn = dh * (1.0 + sc_ref[...])
        u = dn * g
        dx1 = dx2_ref[...] + r * (u - xn * jnp.mean(u * xn, axis=-1, keepdims=True))
        dx1_ref[...] = dx1
        dy1_ref[...] = (ga_ref[...] * dx1).astype(BF16)
        st_ref[0:1, :] += jnp.sum(dh, axis=0, keepdims=True)
        st_ref[1:2, :] += jnp.sum(dh * (xn * g), axis=0, keepdims=True)
        st_ref[2:3, :] += jnp.sum(dn * xn, axis=0, keepdims=True)
        st_ref[3:4, :] += jnp.sum(dx1 * y1_ref[...].astype(F32), axis=0, keepdims=True)

    row = pl.BlockSpec((tm, d), lambda i: (i, 0))
    vecs, vec_specs = zip(*[_vec(v, d) for v in (g_ffn, sc2, ga1)])
    return pl.pallas_call(
        body, name="ffn_in_bwd", grid=(t // tm,),
        in_specs=[pl.BlockSpec((2, tm, ff), lambda i: (0, i, 0)),
                  pl.BlockSpec((n_sh, d, sw), lambda i: (0, 0, 0), pipeline_mode=pl.Buffered(1)),
                  row, row, row, *vec_specs],
        out_specs=[row, row, pl.BlockSpec((8, d), lambda i: (0, 0))],
        out_shape=[SDS((t, d), F32), SDS((t, d), BF16), SDS((8, d), F32)],
        compiler_params=_params("arbitrary"))(dgu, w, x1, dx2, y1, *vecs)


def _out_proj_bwd(dy1, w_out, tm, ride=None):
    t, d = dy1.shape

    def body(dy_ref, w_ref, o_ref):
        o_ref[...] = lax.dot_general(dy_ref[...], w_ref[...], (((1,), (1,)), ((), ())),
                                     preferred_element_type=F32).astype(BF16)

    row = pl.BlockSpec((tm, d), lambda i: (i, 0))
    return _pallas(body, name="out_proj_bwd", grid=(t // tm,),
                   in_specs=[row, pl.BlockSpec((d, d), lambda i: (0, 0))], out_specs=row,
                   out_shape=SDS((t, d), BF16), args=(dy1, w_out), sem=("parallel",), ride=ride)


BWD_BLOCKS = 2


def _mixer_bwd(z, dmerged, attn, sinks, conv_w, d, ride=None):
    t, zw = z.shape
    kvw2 = zw - 6 * d
    tq = BWD_BLOCKS * BLOCK
    steps = t // tq
    halo = tq // SUBLANES_BF16
    last_halo = t // SUBLANES_BF16 - 1
    scale = HEAD_DIM ** -0.5
    seg = _segments(d, kvw2)

    def body(z_ref, kvp_ref, prev_ref, next_ref, dm_ref, dmn_ref, attn_ref, sinks_ref, cw_ref,
             dz_ref, dkv_ref, db_ref, dbkv_ref, dcw_ref, dsk_ref, carry_ref):
        n = pl.program_id(0)

        @pl.when(n == 0)
        def _():
            carry_ref[...] = jnp.zeros_like(carry_ref)
            db_ref[...] = jnp.zeros_like(db_ref)
            dbkv_ref[...] = jnp.zeros_like(dbkv_ref)
            dcw_ref[...] = jnp.zeros_like(dcw_ref)
            dsk_ref[...] = jnp.zeros_like(dsk_ref)

        def one_block(b, pending):
            rows = slice(b * BLOCK, (b + 1) * BLOCK)
            before = slice((b - 1) * BLOCK, b * BLOCK)
            dm = dm_ref[rows, :].astype(F32)
            sa = jax.nn.sigmoid(z_ref[rows, seg["ga"]].astype(F32))
            dga = dm * attn_ref[rows, :].astype(F32) * sa * (1.0 - sa)
            dz_ref[rows, seg["ga"]] = dga.astype(BF16)
            db_ref[0:1, seg["ga"]] += jnp.sum(dga, axis=0, keepdims=True)
            dattn = (dm * sa).astype(BF16)

            kv_prev = kvp_ref[...] if b == 0 else z_ref[before, seg["kv"]]
            kv = jnp.concatenate([kv_prev, z_ref[rows, seg["kv"]]], axis=0)
            k_eff, v_eff = _kv_variants(kv, kvw2 // 2)
            band, col = _attn_masks()
            valid = band & ((n > 0) | (col >= BLOCK)) if b == 0 else band
            lane_lo = lax.broadcasted_iota(jnp.int32, (2 * BLOCK, LANES), 1) < HEAD_DIM
            sink_lane = lax.broadcasted_iota(jnp.int32, (1, LANES), 1)
            rowblk = lax.broadcasted_iota(jnp.int32, (4 * BLOCK, 1), 0) // BLOCK
            dk_acc = [jnp.zeros((2 * BLOCK, LANES), F32), jnp.zeros((2 * BLOCK, LANES), F32)]
            dv_acc = [jnp.zeros((2 * BLOCK, LANES), F32), jnp.zeros((2 * BLOCK, LANES), F32)]
            dsink = jnp.zeros((1, LANES), F32)
            for h in range(2):
                q4 = _stack_pairs(z_ref, h, rows)
                do4 = jnp.concatenate([dattn[:, (4 * h + j) * LANES:(4 * h + j + 1) * LANES] for j in range(4)],
                                      axis=0)
                dq4 = jnp.zeros((4 * BLOCK, LANES), F32)
                for e in range(2):
                    s = lax.dot_general(q4, k_eff[h][e], (((1,), (1,)), ((), ())), preferred_element_type=F32)
                    p, psink = _softmax_sink(s, valid, _sink_column(sinks_ref, h, e))
                    dp = lax.dot_general(do4, v_eff[h][e], (((1,), (1,)), ((), ())), preferred_element_type=F32)
                    delta = jnp.sum(p * dp, axis=-1, keepdims=True)
                    ds = (p * (dp - delta)).astype(BF16)
                    dq4 = dq4 + jnp.dot(ds, k_eff[h][e], preferred_element_type=F32)
                    dk = lax.dot_general(q4, ds, (((0,), (0,)), ((), ())), preferred_element_type=F32).T
                    dv = lax.dot_general(do4, p.astype(BF16), (((0,), (0,)), ((), ())), preferred_element_type=F32).T
                    keep = lane_lo if e == 0 else jnp.logical_not(lane_lo)
                    slot = 0 if e == h else 1
                    dk_acc[slot] = dk_acc[slot] + jnp.where(keep, dk, 0.0)
                    dv_acc[slot] = dv_acc[slot] + jnp.where(keep, dv, 0.0)
                    dsk = -(psink * delta)
                    for j in range(4):
                        tot = jnp.sum(jnp.where(rowblk == j, dsk, 0.0), axis=0, keepdims=True)
                        dsink = dsink + jnp.where(sink_lane == GROUP * h + 2 * j + e, tot, 0.0)
                for j in range(4):
                    cols = slice((4 * h + j) * LANES, (4 * h + j + 1) * LANES)
                    dqj = dq4[j * BLOCK:(j + 1) * BLOCK]
                    dz_ref[rows, cols] = dqj.astype(BF16)
                    db_ref[0:1, cols] += jnp.sum(dqj, axis=0, keepdims=True)
            dsk_ref[0:1, :] += dsink
            dkv_new = jnp.concatenate([(dk_acc[0] + pltpu.roll(dk_acc[1], HEAD_DIM, 1)) * scale,
                                       dv_acc[0] + pltpu.roll(dv_acc[1], HEAD_DIM, 1)], axis=1)
            done = pending + dkv_new[:BLOCK]
            dkv_ref[rows, :] = done.astype(BF16)
            dbkv_ref[0:1, :] += jnp.sum(done, axis=0, keepdims=True)

            cb = z_ref[rows, seg["cb"]].astype(F32)
            cc = z_ref[rows, seg["cc"]].astype(F32)
            cx = z_ref[rows, seg["cx"]].astype(F32)
            sg = jax.nn.sigmoid(z_ref[rows, seg["gc"]].astype(F32))
            p_in = cc * cx
            if b == 0:
                prev = jnp.where(n > 0, prev_ref[:, seg["cc"]].astype(F32) * prev_ref[:, seg["cx"]].astype(F32), 0.0)
            else:
                tail = slice(b * BLOCK - SUBLANES_BF16, b * BLOCK)
                prev = z_ref[tail, seg["cc"]].astype(F32) * z_ref[tail, seg["cx"]].astype(F32)
            p_m1 = _shift_down(p_in, 1, prev)
            p_m2 = _shift_down(p_in, 2, prev)
            w0, w1, w2 = cw_ref[0:1, :], cw_ref[1:2, :], cw_ref[2:3, :]
            cconv = w0 * p_m2 + w1 * p_m1 + w2 * p_in
            dconv = dm * sg
            dgc = dm * (cb * cconv) * sg * (1.0 - sg)
            dcb = dconv * cconv
            dcc_t = dconv * cb
            if b == BWD_BLOCKS - 1:
                nxt = jnp.where(n < steps - 1,
                                dmn_ref[...].astype(F32) * jax.nn.sigmoid(next_ref[:, seg["gc"]].astype(F32))
                                * next_ref[:, seg["cb"]].astype(F32), 0.0)
            else:
                head = slice((b + 1) * BLOCK, (b + 1) * BLOCK + SUBLANES_BF16)
                nxt = (dm_ref[head, :].astype(F32) * jax.nn.sigmoid(z_ref[head, seg["gc"]].astype(F32))
                       * z_ref[head, seg["cb"]].astype(F32))
            dpin = w2 * dcc_t + w1 * _shift_up(dcc_t, 1, nxt) + w0 * _shift_up(dcc_t, 2, nxt)
            for nm, val in (("cb", dcb), ("cc", dpin * cx), ("cx", dpin * cc), ("gc", dgc)):
                dz_ref[rows, seg[nm]] = val.astype(BF16)
                db_ref[0:1, seg[nm]] += jnp.sum(val, axis=0, keepdims=True)
            dcw_ref[0:1, :] += jnp.sum(dcc_t * p_m2, axis=0, keepdims=True)
            dcw_ref[1:2, :] += jnp.sum(dcc_t * p_m1, axis=0, keepdims=True)
            dcw_ref[2:3, :] += jnp.sum(dcc_t * p_in, axis=0, keepdims=True)
            return dkv_new[BLOCK:]

        @pl.when(n < steps)
        def _():
            pending = carry_ref[...]
            for b in range(BWD_BLOCKS):
                pending = one_block(b, pending)
            carry_ref[...] = pending

        @pl.when(n == steps)
        def _():
            done = carry_ref[...]
            dkv_ref[:BLOCK, :] = done.astype(BF16)
            dkv_ref[BLOCK:, :] = jnp.zeros((tq - BLOCK, kvw2), BF16)
            dbkv_ref[0:1, :] += jnp.sum(done, axis=0, keepdims=True)

    def cur(n):
        return jnp.minimum(n, steps - 1)

    def after(n):
        return jnp.minimum((cur(n) + 1) * halo, last_halo)

    blk = pl.BlockSpec((tq, d), lambda n: (cur(n), 0))
    return _pallas(
        body, name="mixer_bwd", grid=(steps + 1,), ride=ride, sem=("arbitrary",),
        args=(z, z, z, z, dmerged, dmerged, attn, sinks, conv_w),
        in_specs=[pl.BlockSpec((tq, zw), lambda n: (cur(n), 0)),
                  pl.BlockSpec((BLOCK, kvw2), lambda n: (jnp.maximum(cur(n) * BWD_BLOCKS - 1, 0), d // kvw2)),
                  pl.BlockSpec((SUBLANES_BF16, zw), lambda n: (jnp.maximum(cur(n) * halo - 1, 0), 0)),
                  pl.BlockSpec((SUBLANES_BF16, zw), lambda n: (after(n), 0)),
                  blk,
                  pl.BlockSpec((SUBLANES_BF16, d), lambda n: (after(n), 0)),
                  blk, SMEM_SPEC, pl.BlockSpec((3, d), lambda n: (0, 0))],
        out_specs=[pl.BlockSpec((tq, zw), lambda n: (cur(n), 0)),
                   pl.BlockSpec((tq, kvw2), lambda n: (n, 0)),
                   pl.BlockSpec((8, zw), lambda n: (0, 0)), pl.BlockSpec((8, kvw2), lambda n: (0, 0)),
                   pl.BlockSpec((8, d), lambda n: (0, 0)), pl.BlockSpec((8, LANES), lambda n: (0, 0))],
        out_shape=[SDS((t, zw), BF16), SDS((t + tq, kvw2), BF16), SDS((8, zw), F32), SDS((8, kvw2), F32),
                   SDS((8, d), F32), SDS((8, LANES), F32)],
        scratch=[pltpu.VMEM((BLOCK, kvw2), F32)])


def _wgrad_in(dz, dkv, h1, tk, ride=None):
    t, zw = dz.shape
    d = h1.shape[1]
    kvw2 = dkv.shape[1]
    blk = d + kvw2
    assert zw % blk == 0
    tn = (((0,), (0,)), ((), ()))

    def body(a_ref, akv_ref, h_ref, o_ref, o16_ref):
        n, k = pl.program_id(0), pl.program_id(1)

        @pl.when(k == 0)
        def _():
            o_ref[...] = jnp.zeros_like(o_ref)

        @pl.when(n == 0)
        def _():
            o_ref[:d, :] += lax.dot_general(a_ref[:, :d], h_ref[...], tn, preferred_element_type=F32)
            o_ref[d:, :] += lax.dot_general(akv_ref[...], h_ref[...], tn, preferred_element_type=F32)

        @pl.when(n > 0)
        def _():
            o_ref[...] += lax.dot_general(a_ref[...], h_ref[...], tn, preferred_element_type=F32)

        @pl.when(k == t // tk - 1)
        def _():
            o16_ref[...] = o_ref[...].astype(BF16)

    out_spec = pl.BlockSpec((blk, d), lambda n, k: (n, 0))
    return _pallas(
        body, name="wgrad_in", grid=(zw // blk, t // tk),
        in_specs=[pl.BlockSpec((tk, blk), lambda n, k: (k, n)), pl.BlockSpec((tk, kvw2), lambda n, k: (k, 0)),
                  pl.BlockSpec((tk, d), lambda n, k: (k, 0))],
        out_specs=[out_spec, out_spec], out_shape=[SDS((zw, d), F32), SDS((zw, d), BF16)],
        args=(dz, dkv, h1), sem=("parallel", "arbitrary"), ride=ride)


def _in_proj_bwd(dz, dkv, wt, x, dx1, g_mix, sc1, tm, ride=None):
    t, d = x.shape
    zw = dz.shape[1]
    kvw2 = dkv.shape[1]
    rest = d + kvw2

    def body(a_ref, akv_ref, w_ref, x_ref, dx1_ref, g_ref, sc_ref, gx_ref, st_ref):
        @pl.when(pl.program_id(0) == 0)
        def _():
            st_ref[...] = jnp.zeros_like(st_ref)

        dh = (jnp.dot(a_ref[:, :d], w_ref[:d, :], preferred_element_type=F32)
              + jnp.dot(akv_ref[...], w_ref[d:rest, :], preferred_element_type=F32)
              + jnp.dot(a_ref[:, rest:], w_ref[rest:, :], preferred_element_type=F32))
        xx = x_ref[...]
        r = _rms(xx)
        xn = xx * r
        g = g_ref[...]
        dn = dh * (1.0 + sc_ref[...])
        u = dn * g
        gx_ref[...] = dx1_ref[...] + r * (u - xn * jnp.mean(u * xn, axis=-1, keepdims=True))
        st_ref[0:1, :] += jnp.sum(dh, axis=0, keepdims=True)
        st_ref[1:2, :] += jnp.sum(dh * (xn * g), axis=0, keepdims=True)
        st_ref[2:3, :] += jnp.sum(dn * xn, axis=0, keepdims=True)

    row = pl.BlockSpec((tm, d), lambda i: (i, 0))
    vecs, vec_specs = zip(*[_vec(v, d) for v in (g_mix, sc1)])
    return _pallas(
        body, name="in_proj_bwd", grid=(t // tm,),
        in_specs=[pl.BlockSpec((tm, zw), lambda i: (i, 0)), pl.BlockSpec((tm, kvw2), lambda i: (i, 0)),
                  pl.BlockSpec((zw, d), lambda i: (0, 0), pipeline_mode=pl.Buffered(1)),
                  row, row, *vec_specs],
        out_specs=[row, pl.BlockSpec((8, d), lambda i: (0, 0))],
        out_shape=[SDS((t, d), F32), SDS((8, d), F32)],
        args=(dz, dkv, wt, x, dx1, *vecs), sem=("arbitrary",), ride=ride)


def _to_lanes(v, rows=None):
    flat = v.reshape(-1)
    need = -(-flat.shape[0] // LANES)
    need = -(-need // 8) * 8 if rows is None else rows
    return jnp.pad(flat, (0, need * LANES - flat.shape[0])).reshape(need, LANES)


def kernel(x, c, w_ada, b_ada, g_mix, w_in, b_in, sinks, conv_w, w_out, g_ffn, w_ffn_in, w_ffn_out, g_final, loss_target, m_w_ada, m_b_ada, m_g_mix, m_w_in, m_b_in, m_sinks, m_conv_w, m_w_out, m_g_ffn, m_w_ffn_in, m_w_ffn_out, m_g_final, v_w_ada, v_b_ada, v_g_mix, v_w_in, v_b_in, v_sinks, v_conv_w, v_w_out, v_g_ffn, v_w_ffn_in, v_w_ffn_out, v_g_final):
    xs, tgt = x[0], loss_target[0]
    t, d = xs.shape
    zw = w_in.shape[2] * N_CHIP
    kvw2 = zw - 6 * d
    ff = w_ffn_out.shape[1] * N_CHIP
    n_mod = w_ada.shape[2] * N_CHIP // d
    mod_sh = w_ada.shape[2]
    cw_sh = conv_w.shape[2]
    assert d % (8 * LANES) == 0 and kvw2 == 2 * LANES and t % 512 == 0 and n_mod == 6
    xi, yi, ci = _mesh_pos()
    j_me = 2 * xi + yi
    pos = jnp.stack([ci, j_me]).astype(jnp.int32)
    tm = 512

    w_in_t, m_w_in_t, v_w_in_t = w_in[0].T, m_w_in[0].T, v_w_in[0].T
    assert d == 8 * LANES
    pack1 = jnp.concatenate([c.reshape(d // LANES, LANES), conv_w[0].reshape(-1, LANES)], axis=0)
    pack1 = jnp.pad(pack1, ((0, 16 - pack1.shape[0]), (0, 0)))
    b_ada_sh = lax.dynamic_slice(b_ada, (0, j_me * mod_sh), (1, mod_sh))
    g1, mod, w_in_g, later = _startup(pack1, w_ada[0], b_ada_sh, _cast_into_block(pos, w_in_t, "cast_w_in"),
                                      [w_out[0], w_ffn_in[0], w_ffn_out[0]])
    c_all = g1[:, :d // LANES, :].reshape(N_DEV, d)
    cw_rows = 3 * cw_sh // LANES
    conv_w_full = jnp.concatenate(
        [g1[2 * j, d // LANES:d // LANES + cw_rows, :].reshape(3, cw_sh) for j in range(N_CHIP)], axis=1)
    sh1, sc1, ga1, sh2, sc2, ga2 = [(mod, k) for k in range(6)]
    w_in_tf = w_in_g.reshape(zw, d)

    (z, h1), later = _in_proj(xs, g_mix, sc1, sh1, w_in_tf, b_in, min(t, 2048), zw // 5, ride=_x_gather_ici(later))
    (attn, merged), later = _mixer_fwd(z, sinks, conv_w_full, d, ride=_x_gather_d2d(later))
    w_out_f = later[0].reshape(d, d)
    w_ffn_in_f = later[1]
    w_ffn_out_f = later[2].reshape(ff, d)
    tml = min(t, 1024)
    y1, x1, h2 = _out_proj_fwd(merged, w_out_f, xs, ga1, g_ffn, sc2, sh2, tml)
    gu, act = _ffn_in_fwd(h2, w_ffn_in_f, ff, tml, ff // 2)
    dx2, dy2, st_loss = _ffn_out_loss(act, w_ffn_out_f, x1, tgt, ga2, g_final.reshape(1, d), tml)

    dgu = _ffn_out_bwd(dy2, w_ffn_out_f, gu, tml, ff // 2)
    tk = min(t, 2048)
    dw_ffn_out, _ = _wgrad(
        act, dy2, pl.BlockSpec((tk, ff // 2), lambda m, k: (k, m)), pl.BlockSpec((tk, d), lambda m, k: (k, 0)),
        pl.BlockSpec((ff // 2, d), lambda m, k: (m, 0)), SDS((ff, d), F32), (2, t // tk), "wgrad_ffn_out")
    dx1, dy1, st_ffn = _ffn_in_bwd(dgu, w_ffn_in_f, x1, dx2, y1, g_ffn, sc2, ga1, tm)
    dw_ffn_in, _ = _wgrad(
        h2, dgu, pl.BlockSpec((tk, d), lambda n, k: (k, 0)),
        pl.BlockSpec((None, tk, ff // 2), lambda n, k: (n // 2, k, n % 2)),
        pl.BlockSpec((None, d, ff // 2), lambda n, k: (n, 0, 0)), SDS((N_CHIP, d, ff // 2), F32),
        (N_CHIP, t // tk), "wgrad_ffn_in")
    dw_out, _ = _wgrad(
        merged, dy1, pl.BlockSpec((tk, d), lambda m, k: (k, 0)), pl.BlockSpec((tk, d), lambda m, k: (k, 0)),
        pl.BlockSpec((d, d), lambda m, k: (0, 0)), SDS((d, d), F32), (1, t // tk), "wgrad_out")

    early = [[g.reshape(N_CHIP, -1, g.shape[-1]) for g in pair] for pair in (dw_out, dw_ffn_in, dw_ffn_out)]
    early_names = ["w_out", "w_ffn_in", "w_ffn_out"]
    dmerged, _ = _out_proj_bwd(dy1, w_out_f, tml)
    (dz, dkv_shifted, db_z, db_kv, dcw, dsk), terms = _mixer_bwd(
        z, dmerged, attn, sinks, conv_w_full, d, ride=_x_reduce([e[0] for e in early], [e[1] for e in early]))
    dkv = dkv_shifted[BLOCK:BLOCK + t]
    fulls = [_sum_terms(pos, e[0], s, r, "sum_terms_" + nm)
             for e, s, r, nm in zip(early, terms[:3], terms[3:], early_names)]
    dw_in_t, (g_w_out, g_w_ffn_in, g_w_ffn_out) = _wgrad_in(dz, dkv, h1, tk, ride=_x_pair_exchange(fulls))
    dw_in_t = [g.reshape(N_CHIP, zw // N_CHIP, d) for g in dw_in_t]

    (grad_x, st_in), (from_sib, from_far) = _in_proj_bwd(dz, dkv, w_in_tf, xs, dx1, g_mix, sc1, tm,
                                                         ride=_x_reduce([dw_in_t[0]], [dw_in_t[1]]))
    g_w_in_half = _sum_terms(pos, dw_in_t[0], from_sib, from_far, "sum_terms_w_in")

    dmod = jnp.concatenate([st_in[0:1], st_in[1:2], st_ffn[3:4], st_ffn[0:1], st_ffn[1:2], st_loss[0:1]], axis=1)
    db_in = jnp.concatenate([db_z[0:1, :d], db_kv[0:1], db_z[0:1, d + kvw2:]], axis=1)
    seg = [dmod, st_in[2:3], db_in, dsk[0:1], dcw[0:3].reshape(1, 3 * d), st_ffn[2:3], st_loss[1:2],
           st_loss[2:3, :LANES]]
    sizes = [s.shape[1] for s in seg]
    pack2 = _to_lanes(jnp.concatenate(seg, axis=1))
    packs, g_w_in_t = _tail_exchange(pack2, g_w_in_half)
    tot = _pack_sum(packs).reshape(-1)
    offs = [sum(sizes[:k]) for k in range(len(sizes))]
    gb_ada, gg_mix, gb_in, gsinks, gcw, gg_ffn, gg_final, loss_v = [tot[o:o + s] for o, s in zip(offs, sizes)]
    loss = loss_v[0]
    gsinks = gsinks[:sinks.shape[1]]
    gcw_sh = lax.dynamic_slice(gcw.reshape(3, d), (0, j_me * cw_sh), (3, cw_sh))

    dmod_all = packs[:, :n_mod * d // LANES, :].reshape(N_DEV, n_mod * d)
    g_w_ada = _ada_wgrad(c_all, lax.dynamic_slice(dmod_all, (0, j_me * mod_sh), (N_DEV, mod_sh)))

    out_g, out_d, out_m, out_v = {}, {}, {}, {}
    big = {"w_ada": (w_ada[0], g_w_ada, m_w_ada[0], v_w_ada[0]),
           "w_out": (w_out[0], g_w_out, m_w_out[0], v_w_out[0]),
           "w_ffn_in": (w_ffn_in[0], g_w_ffn_in, m_w_ffn_in[0], v_w_ffn_in[0]),
           "w_ffn_out": (w_ffn_out[0], g_w_ffn_out, m_w_ffn_out[0], v_w_ffn_out[0])}
    for nm, (w, g, m, v) in big.items():
        out_g[nm], out_d[nm], out_m[nm], out_v[nm] = [o[None] for o in _adamw(w, g, m, v, "adamw_" + nm)]
    out_g["w_in"], out_d["w_in"], out_m["w_in"], out_v["w_in"] = [
        o.T[None] for o in _adamw(w_in_t, g_w_in_t, m_w_in_t, v_w_in_t, "adamw_w_in")]
    small = {"b_ada": (b_ada, gb_ada, m_b_ada, v_b_ada), "g_mix": (g_mix, gg_mix, m_g_mix, v_g_mix),
             "b_in": (b_in, gb_in, m_b_in, v_b_in), "sinks": (sinks, gsinks, m_sinks, v_sinks),
             "conv_w": (conv_w, gcw_sh, m_conv_w, v_conv_w), "g_ffn": (g_ffn, gg_ffn, m_g_ffn, v_g_ffn),
             "g_final": (g_final, gg_final, m_g_final, v_g_final)}
    def two_d(a):
        return a.reshape(-1, a.shape[-1])

    s_out = _adamw_small([tuple(two_d(a.reshape(w.shape)) for a in (w, g, m, v)) for w, g, m, v in small.values()])
    for (nm, (w, g, _, _)), res in zip(small.items(), s_out):
        out_g[nm] = g.reshape(w.shape)
        out_d[nm], out_m[nm], out_v[nm] = [o.reshape(w.shape) for o in res]

    order = ["w_ada", "b_ada", "g_mix", "w_in", "b_in", "sinks", "conv_w", "w_out", "g_ffn", "w_ffn_in", "w_ffn_out",
             "g_final"]
    return (loss, grad_x[None], *[out_g[k] for k in order], *[out_d[k] for k in order],
            *[out_m[k] for k in order], *[out_v[k] for k in order])
```

```python
import functools

import jax
import jax.numpy as jnp
from jax import lax
from jax.experimental import pallas as pl
from jax.experimental.pallas import tpu as pltpu

F32 = jnp.float32
BF16 = jnp.bfloat16
EPS = 1e-6
HEAD_DIM = 64
GROUP = 8
BLOCK = 128
LANES = 128
SUBLANES_BF16 = 16
N_DEV = 8
N_CHIP = 4
VMEM_LIMIT = 56 * 1024 * 1024
MESH = pl.DeviceIdType.MESH

ADAM_LR = 0.001
ADAM_B1 = 0.9
ADAM_B2 = 0.999
ADAM_EPS = 1e-08
ADAM_WD = 0.01
ADAM_STEP = 10

SDS = jax.ShapeDtypeStruct
ANY = pl.BlockSpec(memory_space=pl.ANY)
VMEM_SPEC = pl.BlockSpec(memory_space=pltpu.VMEM)
SMEM_SPEC = pl.BlockSpec(memory_space=pltpu.SMEM)


def _params(*sem):
    return pltpu.CompilerParams(dimension_semantics=sem, vmem_limit_bytes=VMEM_LIMIT)


def _vec(v, d):
    arr, k = v if isinstance(v, tuple) else (v, 0)
    return arr, pl.BlockSpec((1, d), lambda *_: (0, k))


def _mesh_pos():
    return lax.axis_index("x"), lax.axis_index("y"), lax.axis_index("c")


def _row_tile(rows, cols, itemsize=4, budget=1 << 20, mult=8):
    best = None
    for t in range(mult, rows + 1, mult):
        if rows % t == 0 and t * cols * itemsize <= budget:
            best = t
    if best is None:
        best = rows
    return best


def _gather_all(v_ref, out_ref, send_sems, recv_sems, local_sem):
    x, y, c = _mesh_pos()
    me = 4 * x + 2 * y + c
    mine = pltpu.make_async_copy(v_ref, out_ref.at[me], local_sem)
    mine.start()
    peers = []
    for k in range(1, N_DEV):
        px = 1 - x if k & 4 else x
        py = 1 - y if k & 2 else y
        pc = 1 - c if k & 1 else c
        peers.append((px, py, pc))

    def copy(k, block):
        return pltpu.make_async_remote_copy(
            src_ref=v_ref, dst_ref=out_ref.at[block], send_sem=send_sems.at[k], recv_sem=recv_sems.at[k],
            device_id=peers[k], device_id_type=MESH)

    sends = [copy(k, me) for k in range(N_DEV - 1)]
    for cp in sends:
        cp.start()
    for k, (px, py, pc) in enumerate(peers):
        copy(k, 4 * px + 2 * py + pc).wait_recv()
    for cp in sends:
        cp.wait_send()
    mine.wait()


def _small_sems():
    return [pltpu.SemaphoreType.DMA((N_DEV - 1,)), pltpu.SemaphoreType.DMA((N_DEV - 1,)), pltpu.SemaphoreType.DMA]


def _tail_exchange(pack, full):
    def body(pack_ref, full_unused, packs_ref, full_ref, s1, r1, l1, send_sem, recv_sem):
        del full_unused
        x, y, c = _mesh_pos()
        half = full_ref.shape[0] // 2
        rows = pl.ds(pl.multiple_of(c * half, 8), half)
        swap = pltpu.make_async_remote_copy(
            src_ref=full_ref.at[rows], dst_ref=full_ref.at[rows], send_sem=send_sem, recv_sem=recv_sem,
            device_id=(x, y, 1 - c), device_id_type=MESH)
        swap.start()
        _gather_all(pack_ref, packs_ref, s1, r1, l1)
        swap.wait()

    return pl.pallas_call(
        body, name="tail_exchange", out_shape=[SDS((N_DEV,) + pack.shape, pack.dtype), SDS(full.shape, full.dtype)],
        in_specs=[VMEM_SPEC, ANY], out_specs=[VMEM_SPEC, ANY], input_output_aliases={1: 1},
        scratch_shapes=_small_sems() + [pltpu.SemaphoreType.DMA, pltpu.SemaphoreType.DMA])(pack, full)


def _other_chips(x, y):
    return [(1 - x, y), (x, 1 - y), (1 - x, 1 - y)]


def _startup(pack, w_ada_sh, b_ada_sh, w_buf, later):
    d, n = w_ada_sh.shape
    kc = d // LANES
    n_l = len(later)
    chunk_rows = [_row_tile(a.shape[0], a.shape[1], budget=3 << 19, mult=SUBLANES_BF16) for a in later]

    def body(*refs):
        pack_ref, wa_hbm, ba_ref, w_in_unused = refs[:4]
        later_src = refs[4:4 + n_l]
        packs_ref, mine_ref, w_ref = refs[4 + n_l:7 + n_l]
        later_dst = refs[7 + n_l:7 + 2 * n_l]
        wa_scr, mod_scr, mod_ref = refs[7 + 2 * n_l:10 + 2 * n_l]
        f32_bufs = refs[10 + 2 * n_l:10 + 3 * n_l]
        bf16_bufs = refs[10 + 3 * n_l:10 + 4 * n_l]
        (s1, r1, l1, s2, r2, l2, send_sems, recv_sems, fsend_sems, frecv_sems, relay_send, relay_recv, wa_sem,
         cast_sems) = refs[10 + 4 * n_l:]
        del w_in_unused
        x, y, c = _mesh_pos()
        j_me = 2 * x + y
        chips = _other_chips(x, y)
        half = w_ref.shape[1] // 2

        def rows_of(which):
            return pl.ds(pl.multiple_of(which * half, SUBLANES_BF16), half)

        def copy(p, block, rows, over_ici):
            sems = (send_sems, recv_sems) if over_ici else (fsend_sems, frecv_sems)
            return pltpu.make_async_remote_copy(
                src_ref=w_ref.at[block, rows], dst_ref=w_ref.at[block, rows], send_sem=sems[0].at[p],
                recv_sem=sems[1].at[p], device_id=(*chips[p], c) if over_ici else (x, y, 1 - c), device_id_type=MESH)

        def block_of(p):
            return 2 * chips[p][0] + chips[p][1]

        def relay(q, block):
            rows = pl.ds(pl.multiple_of(c * half + q * (half // 2), SUBLANES_BF16), half // 2)
            return pltpu.make_async_remote_copy(
                src_ref=w_ref.at[block, rows], dst_ref=w_ref.at[block, rows], send_sem=relay_send.at[q],
                recv_sem=relay_recv.at[q], device_id=(*chips[1 - q], c), device_id_type=MESH)

        load_wa = pltpu.make_async_copy(wa_hbm, wa_scr, wa_sem)
        load_wa.start()
        _gather_all(pack_ref, packs_ref, s1, r1, l1)
        sends = [copy(p, j_me, rows_of(c), True) for p in range(2)]
        for cp in sends:
            cp.start()
        load_wa.wait()
        acc = jnp.zeros((N_DEV, n), F32)
        for k in range(kc):
            ck = packs_ref[:, k, :]
            sk = (ck * jax.nn.sigmoid(ck)).astype(BF16)
            acc = acc + jnp.dot(sk, wa_scr[k * LANES:(k + 1) * LANES, :].astype(BF16), preferred_element_type=F32)
        mod_scr[...] = acc + ba_ref[...]
        _gather_all(mod_scr, mod_ref, s2, r2, l2)
        for j in range(N_CHIP):
            mine_ref[:, j * n:(j + 1) * n] = mod_ref[2 * j, pl.ds(4 * x + 2 * y + c, 1), :]
        passed = []
        for q in range(2):
            copy(q, block_of(q), rows_of(c), True).wait_recv()
            for cp in (relay(q, block_of(q)), copy(q, block_of(q), rows_of(c), False)):
                cp.start()
                passed.append(cp)
        for src, dst, fbuf, bbuf in zip(later_src, later_dst, f32_bufs, bf16_bufs):
            cr = fbuf.shape[0]
            for k in range(src.shape[0] // cr):
                rows = pl.ds(k * cr, cr)
                cin = pltpu.make_async_copy(src.at[rows], fbuf, cast_sems.at[0])
                cin.start()
                cin.wait()
                bbuf[...] = fbuf[...].astype(BF16)
                cout = pltpu.make_async_copy(bbuf, dst.at[j_me, rows], cast_sems.at[1])
                cout.start()
                cout.wait()
        for q in range(2):
            relay(q, block_of(2)).wait_recv()
        fw = copy(2, block_of(2), rows_of(c), False)
        fw.start()
        for p in range(3):
            copy(p, block_of(p), rows_of(1 - c), False).wait_recv()
        for cp in sends + passed + [fw]:
            cp.wait_send()

    res = pl.pallas_call(
        body, name="startup",
        out_shape=[SDS((N_DEV,) + pack.shape, F32), SDS((1, N_CHIP * n), F32), SDS(w_buf.shape, w_buf.dtype)]
        + [SDS((N_CHIP,) + a.shape, BF16) for a in later],
        in_specs=[VMEM_SPEC, ANY, VMEM_SPEC, ANY] + [ANY] * n_l, out_specs=[VMEM_SPEC, VMEM_SPEC, ANY] + [ANY] * n_l,
        input_output_aliases={3: 2},
        scratch_shapes=[pltpu.VMEM((d, n), F32), pltpu.VMEM((N_DEV, n), F32), pltpu.VMEM((N_DEV, N_DEV, n), F32)]
        + [pltpu.VMEM((cr, a.shape[1]), F32) for cr, a in zip(chunk_rows, later)]
        + [pltpu.VMEM((cr, a.shape[1]), BF16) for cr, a in zip(chunk_rows, later)]
        + _small_sems() + _small_sems()
        + [pltpu.SemaphoreType.DMA((3,))] * 4 + [pltpu.SemaphoreType.DMA((2,))] * 2 + [pltpu.SemaphoreType.DMA]
        + [pltpu.SemaphoreType.DMA((2,))],
        compiler_params=pltpu.CompilerParams(vmem_limit_bytes=VMEM_LIMIT),
    )(pack, w_ada_sh, b_ada_sh, w_buf, *later)
    return res[0], res[1], res[2], list(res[3:])


class _Exchange:
    def __init__(self, operands, out_shape, in_place, n_sems, copies):
        self.operands, self.out_shape, self.in_place, self.n_sems, self.copies = (
            list(operands), list(out_shape), in_place, n_sems, copies)

    def sems(self):
        return [pltpu.SemaphoreType.DMA((self.n_sems,)), pltpu.SemaphoreType.DMA((self.n_sems,))]


def _x_gather_ici(bufs):
    def copies(ins, outs, send_sems, recv_sems):
        x, y, c = _mesh_pos()
        chips = _other_chips(x, y)
        out = []
        for w in range(len(outs)):
            half = outs[w].shape[1] // 2
            rows = pl.ds(pl.multiple_of(c * half, SUBLANES_BF16), half)
            for p in range(3):
                out.append(pltpu.make_async_remote_copy(
                    src_ref=outs[w].at[2 * x + y, rows], dst_ref=outs[w].at[2 * x + y, rows],
                    send_sem=send_sems.at[w * 3 + p], recv_sem=recv_sems.at[w * 3 + p],
                    device_id=(*chips[p], c), device_id_type=MESH))
        return out

    return _Exchange(bufs, [SDS(b.shape, b.dtype) for b in bufs], True, 3 * len(bufs), copies)


def _x_gather_d2d(bufs):
    def copies(ins, outs, send_sems, recv_sems):
        x, y, c = _mesh_pos()
        chips = _other_chips(x, y)
        out = []
        for w in range(len(outs)):
            half = outs[w].shape[1] // 2
            rows = pl.ds(pl.multiple_of(c * half, SUBLANES_BF16), half)
            for p in range(3):
                block = 2 * chips[p][0] + chips[p][1]
                out.append(pltpu.make_async_remote_copy(
                    src_ref=outs[w].at[block, rows], dst_ref=outs[w].at[block, rows],
                    send_sem=send_sems.at[w * 3 + p], recv_sem=recv_sems.at[w * 3 + p],
                    device_id=(x, y, 1 - c), device_id_type=MESH))
        return out

    return _Exchange(bufs, [SDS(b.shape, b.dtype) for b in bufs], True, 3 * len(bufs), copies)


N_REMOTE = 6


def _x_reduce(grads32, grads16):
    n_w = len(grads32)

    def copies(ins, outs, send_sems, recv_sems):
        g32, g16 = ins[:n_w], ins[n_w:]
        from_sib, from_far = outs[:n_w], outs[n_w:]
        x, y, c = _mesh_pos()
        chips = _other_chips(x, y)
        out = []
        for w in range(n_w):
            half = g32[w].shape[1] // 2
            k0 = w * (N_REMOTE + 1)
            out.append(pltpu.make_async_remote_copy(
                src_ref=g32[w].at[2 * x + y, pl.ds(pl.multiple_of((1 - c) * half, SUBLANES_BF16), half), :],
                dst_ref=from_sib[w], send_sem=send_sems.at[k0], recv_sem=recv_sems.at[k0],
                device_id=(x, y, 1 - c), device_id_type=MESH))
            for p in range(3):
                for f in range(2):
                    tc = c if f == 0 else 1 - c
                    k = 2 * p + f
                    out.append(pltpu.make_async_remote_copy(
                        src_ref=g16[w].at[2 * chips[p][0] + chips[p][1],
                                          pl.ds(pl.multiple_of(tc * half, SUBLANES_BF16), half), :],
                        dst_ref=from_far[w].at[k], send_sem=send_sems.at[k0 + 1 + k], recv_sem=recv_sems.at[k0 + 1 + k],
                        device_id=(*chips[p], tc), device_id_type=MESH))
        return out

    shapes = ([SDS((g.shape[1] // 2, g.shape[2]), g.dtype) for g in grads32]
              + [SDS((N_REMOTE, g.shape[1] // 2, g.shape[2]), g.dtype) for g in grads16])
    return _Exchange(list(grads32) + list(grads16), shapes, False, (N_REMOTE + 1) * n_w, copies)


def _x_pair_exchange(fulls):
    def copies(ins, outs, send_sems, recv_sems):
        x, y, c = _mesh_pos()
        out = []
        for w in range(len(outs)):
            half = outs[w].shape[0] // 2
            rows = pl.ds(pl.multiple_of(c * half, 8), half)
            out.append(pltpu.make_async_remote_copy(
                src_ref=outs[w].at[rows], dst_ref=outs[w].at[rows], send_sem=send_sems.at[w],
                recv_sem=recv_sems.at[w], device_id=(x, y, 1 - c), device_id_type=MESH))
        return out

    return _Exchange(fulls, [SDS(f.shape, f.dtype) for f in fulls], True, len(fulls), copies)


def _pallas(body, *, name, grid, in_specs, out_specs, out_shape, args, scratch=(), sem=None, ride=None):
    single = not isinstance(out_specs, (list, tuple))
    out_specs_l = [out_specs] if single else list(out_specs)
    out_shape_l = [out_shape] if single else list(out_shape)
    n_in, n_out, n_scr = len(in_specs), len(out_specs_l), len(scratch)
    if ride is None:
        res = pl.pallas_call(body, name=name, grid=grid, in_specs=list(in_specs), out_specs=out_specs,
                             out_shape=out_shape, scratch_shapes=list(scratch), compiler_params=_params(*sem))(*args)
        return res, None
    n_x, n_xo = len(ride.operands), len(ride.out_shape)

    def full_body(*refs):
        ins, x_ins = refs[:n_in], refs[n_in:n_in + n_x]
        outs = refs[n_in + n_x:n_in + n_x + n_out]
        x_outs = refs[n_in + n_x + n_out:n_in + n_x + n_out + n_xo]
        rest = refs[n_in + n_x + n_out + n_xo:]
        scr, (send_sems, recv_sems) = rest[:n_scr], rest[n_scr:]
        first = functools.reduce(jnp.logical_and, [pl.program_id(a) == 0 for a in range(len(grid))])
        last = functools.reduce(jnp.logical_and, [pl.program_id(a) == grid[a] - 1 for a in range(len(grid))])

        @pl.when(first)
        def _():
            for cp in ride.copies(x_ins, x_outs, send_sems, recv_sems):
                cp.start()

        body(*ins, *outs, *scr)

        @pl.when(last)
        def _():
            for cp in ride.copies(x_ins, x_outs, send_sems, recv_sems):
                cp.wait()

    res = pl.pallas_call(
        full_body, name=name, grid=grid, in_specs=list(in_specs) + [ANY] * n_x,
        out_specs=out_specs_l + [ANY] * n_xo, out_shape=out_shape_l + ride.out_shape,
        input_output_aliases={n_in + k: n_out + k for k in range(n_x)} if ride.in_place else {},
        scratch_shapes=list(scratch) + ride.sems(),
        compiler_params=_params(*(["arbitrary"] * len(grid))))(*args, *ride.operands)
    own = res[0] if single else list(res[:n_out])
    return own, list(res[n_out:])


def _cast_into_block(pos, w, name):
    rows, cols = w.shape
    tr = _row_tile(rows, cols, mult=SUBLANES_BF16)

    def body(pos_ref, w_ref, o_ref):
        del pos_ref
        o_ref[...] = w_ref[...].astype(BF16)

    return pl.pallas_call(
        body, name=name,
        grid_spec=pltpu.PrefetchScalarGridSpec(
            num_scalar_prefetch=1, grid=(rows // tr,),
            in_specs=[pl.BlockSpec((tr, cols), lambda i, pos_ref: (i, 0))],
            out_specs=pl.BlockSpec((None, tr, cols), lambda i, pos_ref: (pos_ref[1], i, 0))),
        out_shape=SDS((N_CHIP, rows, cols), BF16), compiler_params=_params("parallel"))(pos, w)


def _sum_terms(pos, grad, from_sib, from_far, name):
    _, rows, cols = grad.shape
    half = rows // 2
    tr = _row_tile(half, cols, mult=SUBLANES_BF16)
    nblk = half // tr

    def body(pos_ref, g_ref, s_ref, r_ref, o_ref):
        del pos_ref
        acc = g_ref[...] + s_ref[...]
        for k in range(N_REMOTE):
            acc = acc + r_ref[k].astype(F32)
        o_ref[...] = acc

    return pl.pallas_call(
        body, name=name,
        grid_spec=pltpu.PrefetchScalarGridSpec(
            num_scalar_prefetch=1, grid=(nblk,),
            in_specs=[pl.BlockSpec((None, tr, cols), lambda i, pos_ref: (pos_ref[1], pos_ref[0] * nblk + i, 0)),
                      pl.BlockSpec((tr, cols), lambda i, pos_ref: (i, 0)),
                      pl.BlockSpec((N_REMOTE, tr, cols), lambda i, pos_ref: (0, i, 0))],
            out_specs=pl.BlockSpec((tr, cols), lambda i, pos_ref: (pos_ref[0] * nblk + i, 0))),
        out_shape=SDS((rows, cols), F32),
        compiler_params=_params("parallel"),
    )(pos, grad, from_sib, from_far)


def _adamw(w, g, m, v, name):
    rows, cols = w.shape
    tr = _row_tile(rows, cols, budget=1 << 21)

    def body(w_ref, g_ref, m_ref, v_ref, go_ref, d_ref, nm_ref, nv_ref):
        go_ref[...] = g_ref[...]
        _adamw_update(w_ref, g_ref, m_ref, v_ref, d_ref, nm_ref, nv_ref)

    spec = pl.BlockSpec((tr, cols), lambda i: (i, 0))
    return pl.pallas_call(body, name=name, grid=(rows // tr,), in_specs=[spec] * 4, out_specs=[spec] * 4,
                          out_shape=[SDS((rows, cols), F32)] * 4, compiler_params=_params("parallel"))(w, g, m, v)


def _adamw_update(w_ref, g_ref, m_ref, v_ref, d_ref, nm_ref, nv_ref):
    gg = g_ref[...]
    nm = ADAM_B1 * m_ref[...] + (1.0 - ADAM_B1) * gg
    nv = ADAM_B2 * v_ref[...] + (1.0 - ADAM_B2) * (gg * gg)
    m_hat = nm / (1.0 - ADAM_B1 ** ADAM_STEP)
    v_hat = nv / (1.0 - ADAM_B2 ** ADAM_STEP)
    d_ref[...] = -ADAM_LR * (m_hat / (jnp.sqrt(v_hat) + ADAM_EPS) + ADAM_WD * w_ref[...])
    nm_ref[...] = nm
    nv_ref[...] = nv


def _adamw_small(params):
    n_p = len(params)

    def body(*refs):
        ins, outs = refs[:4 * n_p], refs[4 * n_p:]
        for k in range(n_p):
            _adamw_update(*ins[4 * k:4 * k + 4], *outs[3 * k:3 * k + 3])

    flat = [a for tup in params for a in tup]
    res = pl.pallas_call(
        body, name="adamw_small", in_specs=[VMEM_SPEC] * (4 * n_p), out_specs=[VMEM_SPEC] * (3 * n_p),
        out_shape=[SDS(tup[0].shape, F32) for tup in params for _ in range(3)])(*flat)
    return [res[3 * k:3 * k + 3] for k in range(n_p)]


def _pack_sum(gathered):
    _, rows, cols = gathered.shape

    def body(g_ref, o_ref):
        acc = g_ref[0]
        for d in range(1, N_DEV):
            acc = acc + g_ref[d]
        o_ref[...] = acc

    return pl.pallas_call(body, name="pack_sum", in_specs=[VMEM_SPEC], out_specs=VMEM_SPEC,
                          out_shape=SDS((rows, cols), F32))(gathered)


def _ada_wgrad(c_all, dmod_sh):
    d = c_all.shape[1]
    n = dmod_sh.shape[1]
    tn = 512

    def body(c_ref, g_ref, o_ref):
        cc = c_ref[...]
        s = cc * jax.nn.sigmoid(cc)
        o_ref[...] = lax.dot_general(s, g_ref[...], (((0,), (0,)), ((), ())), preferred_element_type=F32,
                                     precision=lax.Precision.HIGHEST)

    return pl.pallas_call(
        body, name="ada_wgrad", grid=(n // tn,),
        in_specs=[pl.BlockSpec((N_DEV, d), lambda j: (0, 0)), pl.BlockSpec((N_DEV, tn), lambda j: (0, j))],
        out_specs=pl.BlockSpec((d, tn), lambda j: (0, j)),
        out_shape=SDS((d, n), F32), compiler_params=_params("parallel"))(c_all, dmod_sh)


def _rms(xf):
    return lax.rsqrt(jnp.mean(xf * xf, axis=-1, keepdims=True) + EPS)


def _in_proj(x, g, sc, sh, wt, b, tm, tn, ride=None):
    t, d = x.shape
    n = wt.shape[0]

    def body(x_ref, g_ref, sc_ref, sh_ref, w_ref, b_ref, z_ref, h_ref):
        @pl.when(pl.program_id(1) == 0)
        def _():
            xf = x_ref[...]
            h_ref[...] = ((xf * _rms(xf) * g_ref[...]) * (1.0 + sc_ref[...]) + sh_ref[...]).astype(BF16)

        acc = lax.dot_general(h_ref[...], w_ref[...], (((1,), (1,)), ((), ())), preferred_element_type=F32)
        z_ref[...] = (acc + b_ref[...]).astype(BF16)

    row = pl.BlockSpec((tm, d), lambda i, j: (i, 0))
    vecs, vec_specs = zip(*[_vec(v, d) for v in (g, sc, sh)])
    return _pallas(
        body, name="in_proj", grid=(t // tm, n // tn),
        in_specs=[row, *vec_specs, pl.BlockSpec((tn, d), lambda i, j: (j, 0)),
                  pl.BlockSpec((1, tn), lambda i, j: (0, j))],
        out_specs=[pl.BlockSpec((tm, tn), lambda i, j: (i, j)), row],
        out_shape=[SDS((t, n), BF16), SDS((t, d), BF16)], args=(x, *vecs, wt, b),
        sem=("parallel", "arbitrary"), ride=ride)


def _segments(d, kvw2):
    o = d + kvw2
    names = ("cb", "cc", "cx", "ga", "gc")
    seg = {nm: slice(o + k * d, o + (k + 1) * d) for k, nm in enumerate(names)}
    seg["q"], seg["kv"] = slice(0, d), slice(d, o)
    return seg


def _attn_masks():
    rows = 4 * BLOCK
    r = lax.broadcasted_iota(jnp.int32, (rows, 2 * BLOCK), 0) & (BLOCK - 1)
    col = lax.broadcasted_iota(jnp.int32, (rows, 2 * BLOCK), 1)
    return (col > r) & (col <= r + BLOCK), col


def _kv_variants(kv, n_kv_w):
    assert n_kv_w == LANES
    kb, vb = kv[:, :LANES] * (HEAD_DIM ** -0.5), kv[:, LANES:]
    kr, vr = pltpu.roll(kb, HEAD_DIM, 1), pltpu.roll(vb, HEAD_DIM, 1)
    lane = lax.broadcasted_iota(jnp.int32, kb.shape, 1)
    lo = lane < HEAD_DIM
    zero = jnp.zeros_like(kb)
    k_eff = [[None, None], [None, None]]
    v_eff = [[None, None], [None, None]]
    for h in range(2):
        for e in range(2):
            ksrc, vsrc = (kb, vb) if e == h else (kr, vr)
            keep = lo if e == 0 else jnp.logical_not(lo)
            k_eff[h][e] = jnp.where(keep, ksrc, zero)
            v_eff[h][e] = jnp.where(keep, vsrc, zero)
    return k_eff, v_eff


def _sink_column(sinks_ref, h, e):
    return jnp.concatenate([jnp.full((BLOCK, 1), sinks_ref[0, GROUP * h + 2 * j + e], F32) for j in range(4)], axis=0)


def _softmax_sink(s, valid, sink):
    s = jnp.where(valid, s, -jnp.inf)
    m = jnp.maximum(jnp.max(s, axis=-1, keepdims=True), sink)
    p = jnp.exp(s - m)
    psink = jnp.exp(sink - m)
    den = jnp.sum(p, axis=-1, keepdims=True) + psink
    inv = 1.0 / den
    return p * inv, psink * inv


def _shift_down(a, s, prev):
    rows = a.shape[0]
    out = pltpu.roll(a, s, 0)
    row = lax.broadcasted_iota(jnp.int32, a.shape, 0)
    for t in range(s):
        out = jnp.where(row == t, prev[SUBLANES_BF16 - s + t:SUBLANES_BF16 - s + t + 1, :], out)
    del rows
    return out


def _shift_up(a, s, nxt):
    rows = a.shape[0]
    out = pltpu.roll(a, rows - s, 0)
    row = lax.broadcasted_iota(jnp.int32, a.shape, 0)
    for t in range(s):
        out = jnp.where(row == rows - s + t, nxt[t:t + 1, :], out)
    return out


def _stack_pairs(ref, h, rows=slice(None)):
    return jnp.concatenate([ref[rows, (4 * h + j) * LANES:(4 * h + j + 1) * LANES] for j in range(4)], axis=0)


FWD_BLOCKS = 4


def _mixer_fwd(z, sinks, conv_w, d, ride=None):
    t, zw = z.shape
    kvw2 = zw - 6 * d
    tq = FWD_BLOCKS * BLOCK
    halo = tq // SUBLANES_BF16
    seg = _segments(d, kvw2)

    def body(z_ref, kvp_ref, prev_ref, sinks_ref, cw_ref, attn_ref, merged_ref):
        n = pl.program_id(0)
        band, col = _attn_masks()
        lane_lo = lax.broadcasted_iota(jnp.int32, (4 * BLOCK, LANES), 1) < HEAD_DIM
        for b in range(FWD_BLOCKS):
            rows = slice(b * BLOCK, (b + 1) * BLOCK)
            before = slice((b - 1) * BLOCK, b * BLOCK)
            kv_prev = kvp_ref[...] if b == 0 else z_ref[before, seg["kv"]]
            kv = jnp.concatenate([kv_prev, z_ref[rows, seg["kv"]]], axis=0)
            k_eff, v_eff = _kv_variants(kv, kvw2 // 2)
            valid = band & ((n > 0) | (col >= BLOCK)) if b == 0 else band
            for h in range(2):
                q4 = _stack_pairs(z_ref, h, rows)
                o4 = jnp.zeros((4 * BLOCK, LANES), F32)
                inv = []
                for e in range(2):
                    s = lax.dot_general(q4, k_eff[h][e], (((1,), (1,)), ((), ())), preferred_element_type=F32)
                    sink = _sink_column(sinks_ref, h, e)
                    s = jnp.where(valid, s, -jnp.inf)
                    m = jnp.maximum(jnp.max(s, axis=-1, keepdims=True), sink)
                    p = jnp.exp(s - m)
                    inv.append(1.0 / (jnp.sum(p, axis=-1, keepdims=True) + jnp.exp(sink - m)))
                    o4 = o4 + jnp.dot(p.astype(BF16), v_eff[h][e], preferred_element_type=F32)
                o4 = o4 * jnp.where(lane_lo, inv[0], inv[1])
                for j in range(4):
                    attn_ref[rows, (4 * h + j) * LANES:(4 * h + j + 1) * LANES] = (
                        o4[j * BLOCK:(j + 1) * BLOCK].astype(BF16))
            cb = z_ref[rows, seg["cb"]].astype(F32)
            p_in = z_ref[rows, seg["cc"]].astype(F32) * z_ref[rows, seg["cx"]].astype(F32)
            if b == 0:
                prev = jnp.where(n > 0, prev_ref[:, seg["cc"]].astype(F32) * prev_ref[:, seg["cx"]].astype(F32), 0.0)
            else:
                tail = slice(b * BLOCK - SUBLANES_BF16, b * BLOCK)
                prev = z_ref[tail, seg["cc"]].astype(F32) * z_ref[tail, seg["cx"]].astype(F32)
            cconv = (cw_ref[0:1, :] * _shift_down(p_in, 2, prev) + cw_ref[1:2, :] * _shift_down(p_in, 1, prev)
                     + cw_ref[2:3, :] * p_in)
            sa = jax.nn.sigmoid(z_ref[rows, seg["ga"]].astype(F32))
            sg = jax.nn.sigmoid(z_ref[rows, seg["gc"]].astype(F32))
            merged_ref[rows, :] = (sa * attn_ref[rows, :].astype(F32) + sg * (cb * cconv)).astype(BF16)

    blk = pl.BlockSpec((tq, d), lambda n: (n, 0))
    return _pallas(
        body, name="mixer_fwd", grid=(t // tq,),
        in_specs=[pl.BlockSpec((tq, zw), lambda n: (n, 0)),
                  pl.BlockSpec((BLOCK, kvw2), lambda n: (jnp.maximum(n * FWD_BLOCKS - 1, 0), d // kvw2)),
                  pl.BlockSpec((SUBLANES_BF16, zw), lambda n: (jnp.maximum(n * halo - 1, 0), 0)),
                  SMEM_SPEC, pl.BlockSpec((3, d), lambda n: (0, 0))],
        out_specs=[blk, blk],
        out_shape=[SDS((t, d), BF16), SDS((t, d), BF16)],
        args=(z, z, z, sinks, conv_w), sem=("parallel",), ride=ride)


def _out_proj_fwd(merged, w_out, x, ga1, g_ffn, sc2, sh2, tm):
    t, d = x.shape

    def body(m_ref, w_ref, x_ref, ga_ref, g_ref, sc_ref, sh_ref, y_ref, x1_ref, h_ref):
        y = jnp.dot(m_ref[...], w_ref[...], preferred_element_type=F32)
        x1 = x_ref[...] + ga_ref[...] * y
        y_ref[...] = y.astype(BF16)
        x1_ref[...] = x1
        h_ref[...] = ((x1 * _rms(x1) * g_ref[...]) * (1.0 + sc_ref[...]) + sh_ref[...]).astype(BF16)

    row = pl.BlockSpec((tm, d), lambda i: (i, 0))
    vecs, vec_specs = zip(*[_vec(v, d) for v in (ga1, g_ffn, sc2, sh2)])
    return pl.pallas_call(
        body, name="out_proj_fwd", grid=(t // tm,),
        in_specs=[row, pl.BlockSpec((d, d), lambda i: (0, 0)), row, *vec_specs],
        out_specs=[row, row, row],
        out_shape=[SDS((t, d), BF16), SDS((t, d), F32), SDS((t, d), BF16)],
        compiler_params=_params("parallel"))(merged, w_out, x, *vecs)


def _ffn_in_fwd(h2, w, ff, tm, tn):
    t, d = h2.shape
    nj = ff // tn
    assert w.shape == (2 * nj, d, tn)

    def body(h_ref, wg_ref, wu_ref, gu_ref, act_ref):
        hh = h_ref[...]
        g = jnp.dot(hh, wg_ref[...], preferred_element_type=F32)
        u = jnp.dot(hh, wu_ref[...], preferred_element_type=F32)
        sg = jax.nn.sigmoid(g)
        silu = g * sg
        gu_ref[0] = (u * (sg + silu * (1.0 - sg))).astype(BF16)
        gu_ref[1] = silu.astype(BF16)
        act_ref[...] = (silu * u).astype(BF16)

    return pl.pallas_call(
        body, name="ffn_in_fwd", grid=(nj, t // tm),
        in_specs=[pl.BlockSpec((tm, d), lambda j, i: (i, 0)), pl.BlockSpec((None, d, tn), lambda j, i: (j, 0, 0)),
                  pl.BlockSpec((None, d, tn), lambda j, i: (j + nj, 0, 0))],
        out_specs=[pl.BlockSpec((2, tm, tn), lambda j, i: (0, i, j)), pl.BlockSpec((tm, tn), lambda j, i: (i, j))],
        out_shape=[SDS((2, t, ff), BF16), SDS((t, ff), BF16)],
        compiler_params=_params("parallel", "parallel"))(h2, w, w)


def _ffn_out_loss(act, w, x1, target, ga2, g_final, tm):
    t, d = x1.shape
    ff = act.shape[1]

    def body(a_ref, w_ref, x1_ref, tg_ref, ga_ref, gf_ref, dx2_ref, dy2_ref, st_ref):
        @pl.when(pl.program_id(0) == 0)
        def _():
            st_ref[...] = jnp.zeros_like(st_ref)

        halves = [slice(k * (tm // 2), (k + 1) * (tm // 2)) for k in range(2)]
        y2s = [jnp.dot(a_ref[rows, :], w_ref[...], preferred_element_type=F32) for rows in halves]
        for rows, y2 in zip(halves, y2s):
            x2 = x1_ref[rows, :] + ga_ref[...] * y2
            r = _rms(x2)
            yn = x2 * r
            err = yn * gf_ref[...] - tg_ref[rows, :]
            loss = 0.5 * jnp.sum(jnp.mean(err * err, axis=-1, keepdims=True), axis=0, keepdims=True)
            dy = err * (1.0 / d)
            u = dy * gf_ref[...]
            dx2 = r * (u - yn * jnp.mean(u * yn, axis=-1, keepdims=True))
            dx2_ref[rows, :] = dx2
            dy2_ref[rows, :] = (ga_ref[...] * dx2).astype(BF16)
            st_ref[0:1, :] += jnp.sum(dx2 * y2, axis=0, keepdims=True)
            st_ref[1:2, :] += jnp.sum(dy * yn, axis=0, keepdims=True)
            st_ref[2:3, :] += jnp.broadcast_to(loss, (1, d))

    row = pl.BlockSpec((tm, d), lambda i: (i, 0))
    vecs, vec_specs = zip(*[_vec(v, d) for v in (ga2, g_final)])
    return pl.pallas_call(
        body, name="ffn_out_loss", grid=(t // tm,),
        in_specs=[pl.BlockSpec((tm, ff), lambda i: (i, 0)),
                  pl.BlockSpec((ff, d), lambda i: (0, 0), pipeline_mode=pl.Buffered(1)), row, row, *vec_specs],
        out_specs=[row, row, pl.BlockSpec((8, d), lambda i: (0, 0))],
        out_shape=[SDS((t, d), F32), SDS((t, d), BF16), SDS((8, d), F32)],
        compiler_params=_params("arbitrary"))(act, w, x1, target, *vecs)


def _ffn_out_bwd(dy2, w, gu, tm, tn):
    t, d = dy2.shape
    ff = w.shape[0]

    def body(dy_ref, w_ref, gu_ref, o_ref):
        dy = dy_ref[...]
        for lo in range(0, tn, 3 * LANES):
            cols = slice(lo, min(lo + 3 * LANES, tn))
            dact = lax.dot_general(dy, w_ref[cols, :], (((1,), (1,)), ((), ())), preferred_element_type=F32)
            o_ref[0, :, cols] = (dact * gu_ref[0, :, cols].astype(F32)).astype(BF16)
            o_ref[1, :, cols] = (dact * gu_ref[1, :, cols].astype(F32)).astype(BF16)

    gu_spec = pl.BlockSpec((2, tm, tn), lambda j, i: (0, i, j))
    return pl.pallas_call(
        body, name="ffn_out_bwd", grid=(ff // tn, t // tm),
        in_specs=[pl.BlockSpec((tm, d), lambda j, i: (i, 0)), pl.BlockSpec((tn, d), lambda j, i: (j, 0)), gu_spec],
        out_specs=gu_spec, out_shape=SDS((2, t, ff), BF16),
        compiler_params=_params("parallel", "parallel"))(dy2, w, gu)


def _wgrad(a, b, a_spec, b_spec, out_spec, out_shape, grid, name, ride=None):
    def body(a_ref, b_ref, o_ref, o16_ref):
        k = pl.program_id(len(grid) - 1)

        @pl.when(k == 0)
        def _():
            o_ref[...] = jnp.zeros_like(o_ref)

        o_ref[...] += lax.dot_general(a_ref[...], b_ref[...], (((0,), (0,)), ((), ())), preferred_element_type=F32)

        @pl.when(k == grid[-1] - 1)
        def _():
            o16_ref[...] = o_ref[...].astype(BF16)

    return _pallas(
        body, name=name, grid=grid, in_specs=[a_spec, b_spec], out_specs=[out_spec, out_spec],
        out_shape=[out_shape, SDS(out_shape.shape, BF16)], args=(a, b),
        sem=["parallel"] * (len(grid) - 1) + ["arbitrary"], ride=ride)


def _ffn_in_bwd(dgu, w, x1, dx2, y1, g_ffn, sc2, ga1, tm):
    t, d = x1.shape
    ff = dgu.shape[2]
    n_sh, _, sw = w.shape
    per = ff // sw
    nt = (((1,), (1,)), ((), ()))

    def body(a_ref, w_ref, x1_ref, dx2_ref, y1_ref, g_ref, sc_ref, ga_ref, dx1_ref, dy1_ref, st_ref):
        @pl.when(pl.program_id(0) == 0)
        def _():
            st_ref[...] = jnp.zeros_like(st_ref)

        dh = None
        for j in range(n_sh):
            part = lax.dot_general(a_ref[j // per, :, (j % per) * sw:(j % per + 1) * sw], w_ref[j], nt,
                                   preferred_element_type=F32)
            dh = part if dh is None else dh + part
        x1 = x1_ref[...]
        r = _rms(x1)
        xn = x1 * r
        g = g_ref[...]
        dn = dh * (1.0 + sc_ref[...])
        u = dn * g
        dx1 = dx2_ref[...] + r * (u - xn * jnp.mean(u * xn, axis=-1, keepdims=True))
        dx1_ref[...] = dx1
        dy1_ref[...] = (ga_ref[...] * dx1).astype(BF16)
        st_ref[0:1, :] += jnp.sum(dh, axis=0, keepdims=True)
        st_ref[1:2, :] += jnp.sum(dh * (xn * g), axis=0, keepdims=True)
        st_ref[2:3, :] += jnp.sum(dn * xn, axis=0, keepdims=True)
        st_ref[3:4, :] += jnp.sum(dx1 * y1_ref[...].astype(F32), axis=0, keepdims=True)

    row = pl.BlockSpec((tm, d), lambda i: (i, 0))
    vecs, vec_specs = zip(*[_vec(v, d) for v in (g_ffn, sc2, ga1)])
    return pl.pallas_call(
        body, name="ffn_in_bwd", grid=(t // tm,),
        in_specs=[pl.BlockSpec((2, tm, ff), lambda i: (0, i, 0)),
                  pl.BlockSpec((n_sh, d, sw), lambda i: (0, 0, 0), pipeline_mode=pl.Buffered(1)),
                  row, row, row, *vec_specs],
        out_specs=[row, row, pl.BlockSpec((8, d), lambda i: (0, 0))],
        out_shape=[SDS((t, d), F32), SDS((t, d), BF16), SDS((8, d), F32)],
        compiler_params=_params("arbitrary"))(dgu, w, x1, dx2, y1, *vecs)


def _out_proj_bwd(dy1, w_out, tm, ride=None):
    t, d = dy1.shape

    def body(dy_ref, w_ref, o_ref):
        o_ref[...] = lax.dot_general(dy_ref[...], w_ref[...], (((1,), (1,)), ((), ())),
                                     preferred_element_type=F32).astype(BF16)

    row = pl.BlockSpec((tm, d), lambda i: (i, 0))
    return _pallas(body, name="out_proj_bwd", grid=(t // tm,),
                   in_specs=[row, pl.BlockSpec((d, d), lambda i: (0, 0))], out_specs=row,
                   out_shape=SDS((t, d), BF16), args=(dy1, w_out), sem=("parallel",), ride=ride)


BWD_BLOCKS = 2


def _mixer_bwd(z, dmerged, attn, sinks, conv_w, d, ride=None):
    t, zw = z.shape
    kvw2 = zw - 6 * d
    tq = BWD_BLOCKS * BLOCK
    steps = t // tq
    halo = tq // SUBLANES_BF16
    last_halo = t // SUBLANES_BF16 - 1
    scale = HEAD_DIM ** -0.5
    seg = _segments(d, kvw2)

    def body(z_ref, kvp_ref, prev_ref, next_ref, dm_ref, dmn_ref, attn_ref, sinks_ref, cw_ref,
             dz_ref, dkv_ref, db_ref, dbkv_ref, dcw_ref, dsk_ref, carry_ref):
        n = pl.program_id(0)

        @pl.when(n == 0)
        def _():
            carry_ref[...] = jnp.zeros_like(carry_ref)
            db_ref[...] = jnp.zeros_like(db_ref)
            dbkv_ref[...] = jnp.zeros_like(dbkv_ref)
            dcw_ref[...] = jnp.zeros_like(dcw_ref)
            dsk_ref[...] = jnp.zeros_like(dsk_ref)

        def one_block(b, pending):
            rows = slice(b * BLOCK, (b + 1) * BLOCK)
            before = slice((b - 1) * BLOCK, b * BLOCK)
            dm = dm_ref[rows, :].astype(F32)
            sa = jax.nn.sigmoid(z_ref[rows, seg["ga"]].astype(F32))
            dga = dm * attn_ref[rows, :].astype(F32) * sa * (1.0 - sa)
            dz_ref[rows, seg["ga"]] = dga.astype(BF16)
            db_ref[0:1, seg["ga"]] += jnp.sum(dga, axis=0, keepdims=True)
            dattn = (dm * sa).astype(BF16)

            kv_prev = kvp_ref[...] if b == 0 else z_ref[before, seg["kv"]]
            kv = jnp.concatenate([kv_prev, z_ref[rows, seg["kv"]]], axis=0)
            k_eff, v_eff = _kv_variants(kv, kvw2 // 2)
            band, col = _attn_masks()
            valid = band & ((n > 0) | (col >= BLOCK)) if b == 0 else band
            lane_lo = lax.broadcasted_iota(jnp.int32, (2 * BLOCK, LANES), 1) < HEAD_DIM
            sink_lane = lax.broadcasted_iota(jnp.int32, (1, LANES), 1)
            dk_acc = [jnp.zeros((2 * BLOCK, LANES), F32), jnp.zeros((2 * BLOCK, LANES), F32)]
            dv_acc = [jnp.zeros((2 * BLOCK, LANES), F32), jnp.zeros((2 * BLOCK, LANES), F32)]
            dsink = jnp.zeros((1, LANES), F32)
            for h in range(2):
                q4 = _stack_pairs(z_ref, h, rows)
                do4 = jnp.concatenate([dattn[:, (4 * h + j) * LANES:(4 * h + j + 1) * LANES] for j in range(4)],
                                      axis=0)
                dq4 = jnp.zeros((4 * BLOCK, LANES), F32)
                for e in range(2):
                    s = lax.dot_general(q4, k_eff[h][e], (((1,), (1,)), ((), ())), preferred_element_type=F32)
                    p, psink = _softmax_sink(s, valid, _sink_column(sinks_ref, h, e))
                    dp = lax.dot_general(do4, v_eff[h][e], (((1,), (1,)), ((), ())), preferred_element_type=F32)
                    delta = jnp.sum(p * dp, axis=-1, keepdims=True)
                    ds = (p * (dp - delta)).astype(BF16)
                    dq4 = dq4 + jnp.dot(ds, k_eff[h][e], preferred_element_type=F32)
                    dk = lax.dot_general(q4, ds, (((0,), (0,)), ((), ())), preferred_element_type=F32).T
                    dv = lax.dot_general(do4, p.astype(BF16), (((0,), (0,)), ((), ())), preferred_element_type=F32).T
                    keep = lane_lo if e == 0 else jnp.logical_not(lane_lo)
                    slot = 0 if e == h else 1
                    dk_acc[slot] = dk_acc[slot] + jnp.where(keep, dk, 0.0)
                    dv_acc[slot] = dv_acc[slot] + jnp.where(keep, dv, 0.0)
                    dsk = -(psink * delta)
                    for j in range(4):
                        tot = jnp.sum(dsk[j * BLOCK:(j + 1) * BLOCK], axis=0, keepdims=True)
                        dsink = dsink + jnp.where(sink_lane == GROUP * h + 2 * j + e, tot, 0.0)
                for j in range(4):
                    cols = slice((4 * h + j) * LANES, (4 * h + j + 1) * LANES)
                    dqj = dq4[j * BLOCK:(j + 1) * BLOCK]
                    dz_ref[rows, cols] = dqj.astype(BF16)
                    db_ref[0:1, cols] += jnp.sum(dqj, axis=0, keepdims=True)
            dsk_ref[0:1, :] += dsink
            dkv_new = jnp.concatenate([(dk_acc[0] + pltpu.roll(dk_acc[1], HEAD_DIM, 1)) * scale,
                                       dv_acc[0] + pltpu.roll(dv_acc[1], HEAD_DIM, 1)], axis=1)
            done = pending + dkv_new[:BLOCK]
            dkv_ref[rows, :] = done.astype(BF16)
            dbkv_ref[0:1, :] += jnp.sum(done, axis=0, keepdims=True)

            cb = z_ref[rows, seg["cb"]].astype(F32)
            cc = z_ref[rows, seg["cc"]].astype(F32)
            cx = z_ref[rows, seg["cx"]].astype(F32)
            sg = jax.nn.sigmoid(z_ref[rows, seg["gc"]].astype(F32))
            p_in = cc * cx
            if b == 0:
                prev = jnp.where(n > 0, prev_ref[:, seg["cc"]].astype(F32) * prev_ref[:, seg["cx"]].astype(F32), 0.0)
            else:
                tail = slice(b * BLOCK - SUBLANES_BF16, b * BLOCK)
                prev = z_ref[tail, seg["cc"]].astype(F32) * z_ref[tail, seg["cx"]].astype(F32)
            p_m1 = _shift_down(p_in, 1, prev)
            p_m2 = _shift_down(p_in, 2, prev)
            w0, w1, w2 = cw_ref[0:1, :], cw_ref[1:2, :], cw_ref[2:3, :]
            cconv = w0 * p_m2 + w1 * p_m1 + w2 * p_in
            dconv = dm * sg
            dgc = dm * (cb * cconv) * sg * (1.0 - sg)
            dcb = dconv * cconv
            dcc_t = dconv * cb
            if b == BWD_BLOCKS - 1:
                nxt = jnp.where(n < steps - 1,
                                dmn_ref[...].astype(F32) * jax.nn.sigmoid(next_ref[:, seg["gc"]].astype(F32))
                                * next_ref[:, seg["cb"]].astype(F32), 0.0)
            else:
                head = slice((b + 1) * BLOCK, (b + 1) * BLOCK + SUBLANES_BF16)
                nxt = (dm_ref[head, :].astype(F32) * jax.nn.sigmoid(z_ref[head, seg["gc"]].astype(F32))
                       * z_ref[head, seg["cb"]].astype(F32))
            dpin = w2 * dcc_t + w1 * _shift_up(dcc_t, 1, nxt) + w0 * _shift_up(dcc_t, 2, nxt)
            for nm, val in (("cb", dcb), ("cc", dpin * cx), ("cx", dpin * cc), ("gc", dgc)):
                dz_ref[rows, seg[nm]] = val.astype(BF16)
                db_ref[0:1, seg[nm]] += jnp.sum(val, axis=0, keepdims=True)
            dcw_ref[0:1, :] += jnp.sum(dcc_t * p_m2, axis=0, keepdims=True)
            dcw_ref[1:2, :] += jnp.sum(dcc_t * p_m1, axis=0, keepdims=True)
            dcw_ref[2:3, :] += jnp.sum(dcc_t * p_in, axis=0, keepdims=True)
            return dkv_new[BLOCK:]

        @pl.when(n < steps)
        def _():
            pending = carry_ref[...]
            for b in range(BWD_BLOCKS):
                pending = one_block(b, pending)
            carry_ref[...] = pending

        @pl.when(n == steps)
        def _():
            done = carry_ref[...]
            dkv_ref[:BLOCK, :] = done.astype(BF16)
            dkv_ref[BLOCK:, :] = jnp.zeros((tq - BLOCK, kvw2), BF16)
            dbkv_ref[0:1, :] += jnp.sum(done, axis=0, keepdims=True)

    def cur(n):
        return jnp.minimum(n, steps - 1)

    def after(n):
        return jnp.minimum((cur(n) + 1) * halo, last_halo)

    blk = pl.BlockSpec((tq, d), lambda n: (cur(n), 0))
    return _pallas(
        body, name="mixer_bwd", grid=(steps + 1,), ride=ride, sem=("arbitrary",),
        args=(z, z, z, z, dmerged, dmerged, attn, sinks, conv_w),
        in_specs=[pl.BlockSpec((tq, zw), lambda n: (cur(n), 0)),
                  pl.BlockSpec((BLOCK, kvw2), lambda n: (jnp.maximum(cur(n) * BWD_BLOCKS - 1, 0), d // kvw2)),
                  pl.BlockSpec((SUBLANES_BF16, zw), lambda n: (jnp.maximum(cur(n) * halo - 1, 0), 0)),
                  pl.BlockSpec((SUBLANES_BF16, zw), lambda n: (after(n), 0)),
                  blk,
                  pl.BlockSpec((SUBLANES_BF16, d), lambda n: (after(n), 0)),
                  blk, SMEM_SPEC, pl.BlockSpec((3, d), lambda n: (0, 0))],
        out_specs=[pl.BlockSpec((tq, zw), lambda n: (cur(n), 0)),
                   pl.BlockSpec((tq, kvw2), lambda n: (n, 0)),
                   pl.BlockSpec((8, zw), lambda n: (0, 0)), pl.BlockSpec((8, kvw2), lambda n: (0, 0)),
                   pl.BlockSpec((8, d), lambda n: (0, 0)), pl.BlockSpec((8, LANES), lambda n: (0, 0))],
        out_shape=[SDS((t, zw), BF16), SDS((t + tq, kvw2), BF16), SDS((8, zw), F32), SDS((8, kvw2), F32),
                   SDS((8, d), F32), SDS((8, LANES), F32)],
        scratch=[pltpu.VMEM((BLOCK, kvw2), F32)])


def _wgrad_in(dz, dkv, h1, tk, ride=None):
    t, zw = dz.shape
    d = h1.shape[1]
    kvw2 = dkv.shape[1]
    blk = d + kvw2
    assert zw % blk == 0
    tn = (((0,), (0,)), ((), ()))

    def body(a_ref, akv_ref, h_ref, o_ref, o16_ref):
        n, k = pl.program_id(0), pl.program_id(1)

        @pl.when(k == 0)
        def _():
            o_ref[...] = jnp.zeros_like(o_ref)

        @pl.when(n == 0)
        def _():
            o_ref[:d, :] += lax.dot_general(a_ref[:, :d], h_ref[...], tn, preferred_element_type=F32)
            o_ref[d:, :] += lax.dot_general(akv_ref[...], h_ref[...], tn, preferred_element_type=F32)

        @pl.when(n > 0)
        def _():
            o_ref[...] += lax.dot_general(a_ref[...], h_ref[...], tn, preferred_element_type=F32)

        @pl.when(k == t // tk - 1)
        def _():
            o16_ref[...] = o_ref[...].astype(BF16)

    out_spec = pl.BlockSpec((blk, d), lambda n, k: (n, 0))
    return _pallas(
        body, name="wgrad_in", grid=(zw // blk, t // tk),
        in_specs=[pl.BlockSpec((tk, blk), lambda n, k: (k, n)), pl.BlockSpec((tk, kvw2), lambda n, k: (k, 0)),
                  pl.BlockSpec((tk, d), lambda n, k: (k, 0))],
        out_specs=[out_spec, out_spec], out_shape=[SDS((zw, d), F32), SDS((zw, d), BF16)],
        args=(dz, dkv, h1), sem=("parallel", "arbitrary"), ride=ride)


def _in_proj_bwd(dz, dkv, wt, x, dx1, g_mix, sc1, tm, ride=None):
    t, d = x.shape
    zw = dz.shape[1]
    kvw2 = dkv.shape[1]
    rest = d + kvw2

    def body(a_ref, akv_ref, w_ref, x_ref, dx1_ref, g_ref, sc_ref, gx_ref, st_ref):
        @pl.when(pl.program_id(0) == 0)
        def _():
            st_ref[...] = jnp.zeros_like(st_ref)

        dh = (jnp.dot(a_ref[:, :d], w_ref[:d, :], preferred_element_type=F32)
              + jnp.dot(akv_ref[...], w_ref[d:rest, :], preferred_element_type=F32)
              + jnp.dot(a_ref[:, rest:], w_ref[rest:, :], preferred_element_type=F32))
        xx = x_ref[...]
        r = _rms(xx)
        xn = xx * r
        g = g_ref[...]
        dn = dh * (1.0 + sc_ref[...])
        u = dn * g
        gx_ref[...] = dx1_ref[...] + r * (u - xn * jnp.mean(u * xn, axis=-1, keepdims=True))
        st_ref[0:1, :] += jnp.sum(dh, axis=0, keepdims=True)
        st_ref[1:2, :] += jnp.sum(dh * (xn * g), axis=0, keepdims=True)
        st_ref[2:3, :] += jnp.sum(dn * xn, axis=0, keepdims=True)

    row = pl.BlockSpec((tm, d), lambda i: (i, 0))
    vecs, vec_specs = zip(*[_vec(v, d) for v in (g_mix, sc1)])
    return _pallas(
        body, name="in_proj_bwd", grid=(t // tm,),
        in_specs=[pl.BlockSpec((tm, zw), lambda i: (i, 0)), pl.BlockSpec((tm, kvw2), lambda i: (i, 0)),
                  pl.BlockSpec((zw, d), lambda i: (0, 0), pipeline_mode=pl.Buffered(1)),
                  row, row, *vec_specs],
        out_specs=[row, pl.BlockSpec((8, d), lambda i: (0, 0))],
        out_shape=[SDS((t, d), F32), SDS((8, d), F32)],
        args=(dz, dkv, wt, x, dx1, *vecs), sem=("arbitrary",), ride=ride)


def _to_lanes(v, rows=None):
    flat = v.reshape(-1)
    need = -(-flat.shape[0] // LANES)
    need = -(-need // 8) * 8 if rows is None else rows
    return jnp.pad(flat, (0, need * LANES - flat.shape[0])).reshape(need, LANES)


def kernel(x, c, w_ada, b_ada, g_mix, w_in, b_in, sinks, conv_w, w_out, g_ffn, w_ffn_in, w_ffn_out, g_final, loss_target, m_w_ada, m_b_ada, m_g_mix, m_w_in, m_b_in, m_sinks, m_conv_w, m_w_out, m_g_ffn, m_w_ffn_in, m_w_ffn_out, m_g_final, v_w_ada, v_b_ada, v_g_mix, v_w_in, v_b_in, v_sinks, v_conv_w, v_w_out, v_g_ffn, v_w_ffn_in, v_w_ffn_out, v_g_final):
    xs, tgt = x[0], loss_target[0]
    t, d = xs.shape
    zw = w_in.shape[2] * N_CHIP
    kvw2 = zw - 6 * d
    ff = w_ffn_out.shape[1] * N_CHIP
    n_mod = w_ada.shape[2] * N_CHIP // d
    mod_sh = w_ada.shape[2]
    cw_sh = conv_w.shape[2]
    assert d % (8 * LANES) == 0 and kvw2 == 2 * LANES and t % 512 == 0 and n_mod == 6
    xi, yi, ci = _mesh_pos()
    j_me = 2 * xi + yi
    pos = jnp.stack([ci, j_me]).astype(jnp.int32)
    tm = 512

    w_in_t, m_w_in_t, v_w_in_t = w_in[0].T, m_w_in[0].T, v_w_in[0].T
    assert d == 8 * LANES
    pack1 = jnp.concatenate([c.reshape(d // LANES, LANES), conv_w[0].reshape(-1, LANES)], axis=0)
    pack1 = jnp.pad(pack1, ((0, 16 - pack1.shape[0]), (0, 0)))
    b_ada_sh = lax.dynamic_slice(b_ada, (0, j_me * mod_sh), (1, mod_sh))
    g1, mod, w_in_g, later = _startup(pack1, w_ada[0], b_ada_sh, _cast_into_block(pos, w_in_t, "cast_w_in"),
                                      [w_out[0], w_ffn_in[0], w_ffn_out[0]])
    c_all = g1[:, :d // LANES, :].reshape(N_DEV, d)
    cw_rows = 3 * cw_sh // LANES
    conv_w_full = jnp.concatenate(
        [g1[2 * j, d // LANES:d // LANES + cw_rows, :].reshape(3, cw_sh) for j in range(N_CHIP)], axis=1)
    sh1, sc1, ga1, sh2, sc2, ga2 = [(mod, k) for k in range(6)]
    w_in_tf = w_in_g.reshape(zw, d)

    (z, h1), later = _in_proj(xs, g_mix, sc1, sh1, w_in_tf, b_in, min(t, 2048), zw // 5, ride=_x_gather_ici(later))
    (attn, merged), later = _mixer_fwd(z, sinks, conv_w_full, d, ride=_x_gather_d2d(later))
    w_out_f = later[0].reshape(d, d)
    w_ffn_in_f = later[1]
    w_ffn_out_f = later[2].reshape(ff, d)
    tml = min(t, 1024)
    y1, x1, h2 = _out_proj_fwd(merged, w_out_f, xs, ga1, g_ffn, sc2, sh2, tml)
    gu, act = _ffn_in_fwd(h2, w_ffn_in_f, ff, tml, ff // 2)
    dx2, dy2, st_loss = _ffn_out_loss(act, w_ffn_out_f, x1, tgt, ga2, g_final.reshape(1, d), tml)

    dgu = _ffn_out_bwd(dy2, w_ffn_out_f, gu, tml, ff // 2)
    tk = min(t, 2048)
    dw_ffn_out, _ = _wgrad(
        act, dy2, pl.BlockSpec((tk, ff // 2), lambda m, k: (k, m)), pl.BlockSpec((tk, d), lambda m, k: (k, 0)),
        pl.BlockSpec((ff // 2, d), lambda m, k: (m, 0)), SDS((ff, d), F32), (2, t // tk), "wgrad_ffn_out")
    dx1, dy1, st_ffn = _ffn_in_bwd(dgu, w_ffn_in_f, x1, dx2, y1, g_ffn, sc2, ga1, tm)
    dw_ffn_in, _ = _wgrad(
        h2, dgu, pl.BlockSpec((tk, d), lambda n, k: (k, 0)),
        pl.BlockSpec((None, tk, ff // 2), lambda n, k: (n // 2, k, n % 2)),
        pl.BlockSpec((None, d, ff // 2), lambda n, k: (n, 0, 0)), SDS((N_CHIP, d, ff // 2), F32),
        (N_CHIP, t // tk), "wgrad_ffn_in")
    dw_out, _ = _wgrad(
        merged, dy1, pl.BlockSpec((tk, d), lambda m, k: (k, 0)), pl.BlockSpec((tk, d), lambda m, k: (k, 0)),
        pl.BlockSpec((d, d), lambda m, k: (0, 0)), SDS((d, d), F32), (1, t // tk), "wgrad_out")

    early = [[g.reshape(N_CHIP, -1, g.shape[-1]) for g in pair] for pair in (dw_out, dw_ffn_in, dw_ffn_out)]
    early_names = ["w_out", "w_ffn_in", "w_ffn_out"]
    dmerged, _ = _out_proj_bwd(dy1, w_out_f, tml)
    (dz, dkv_shifted, db_z, db_kv, dcw, dsk), terms = _mixer_bwd(
        z, dmerged, attn, sinks, conv_w_full, d, ride=_x_reduce([e[0] for e in early], [e[1] for e in early]))
    dkv = dkv_shifted[BLOCK:BLOCK + t]
    fulls = [_sum_terms(pos, e[0], s, r, "sum_terms_" + nm)
             for e, s, r, nm in zip(early, terms[:3], terms[3:], early_names)]
    dw_in_t, (g_w_out, g_w_ffn_in, g_w_ffn_out) = _wgrad_in(dz, dkv, h1, tk, ride=_x_pair_exchange(fulls))
    dw_in_t = [g.reshape(N_CHIP, zw // N_CHIP, d) for g in dw_in_t]

    (grad_x, st_in), (from_sib, from_far) = _in_proj_bwd(dz, dkv, w_in_tf, xs, dx1, g_mix, sc1, tm,
                                                         ride=_x_reduce([dw_in_t[0]], [dw_in_t[1]]))
    g_w_in_half = _sum_terms(pos, dw_in_t[0], from_sib, from_far, "sum_terms_w_in")

    dmod = jnp.concatenate([st_in[0:1], st_in[1:2], st_ffn[3:4], st_ffn[0:1], st_ffn[1:2], st_loss[0:1]], axis=1)
    db_in = jnp.concatenate([db_z[0:1, :d], db_kv[0:1], db_z[0:1, d + kvw2:]], axis=1)
    seg = [dmod, st_in[2:3], db_in, dsk[0:1], dcw[0:3].reshape(1, 3 * d), st_ffn[2:3], st_loss[1:2],
           st_loss[2:3, :LANES]]
    sizes = [s.shape[1] for s in seg]
    pack2 = _to_lanes(jnp.concatenate(seg, axis=1))
    packs, g_w_in_t = _tail_exchange(pack2, g_w_in_half)
    tot = _pack_sum(packs).reshape(-1)
    offs = [sum(sizes[:k]) for k in range(len(sizes))]
    gb_ada, gg_mix, gb_in, gsinks, gcw, gg_ffn, gg_final, loss_v = [tot[o:o + s] for o, s in zip(offs, sizes)]
    loss = loss_v[0]
    gsinks = gsinks[:sinks.shape[1]]
    gcw_sh = lax.dynamic_slice(gcw.reshape(3, d), (0, j_me * cw_sh), (3, cw_sh))

    dmod_all = packs[:, :n_mod * d // LANES, :].reshape(N_DEV, n_mod * d)
    g_w_ada = _ada_wgrad(c_all, lax.dynamic_slice(dmod_all, (0, j_me * mod_sh), (N_DEV, mod_sh)))

    out_g, out_d, out_m, out_v = {}, {}, {}, {}
    big = {"w_ada": (w_ada[0], g_w_ada, m_w_ada[0], v_w_ada[0]),
           "w_out": (w_out[0], g_w_out, m_w_out[0], v_w_out[0]),
           "w_ffn_in": (w_ffn_in[0], g_w_ffn_in, m_w_ffn_in[0], v_w_ffn_in[0]),
           "w_ffn_out": (w_ffn_out[0], g_w_ffn_out, m_w_ffn_out[0], v_w_ffn_out[0])}
    for nm, (w, g, m, v) in big.items():
        out_g[nm], out_d[nm], out_m[nm], out_v[nm] = [o[None] for o in _adamw(w, g, m, v, "adamw_" + nm)]
    out_g["w_in"], out_d["w_in"], out_m["w_in"], out_v["w_in"] = [
        o.T[None] for o in _adamw(w_in_t, g_w_in_t, m_w_in_t, v_w_in_t, "adamw_w_in")]
    small = {"b_ada": (b_ada, gb_ada, m_b_ada, v_b_ada), "g_mix": (g_mix, gg_mix, m_g_mix, v_g_mix),
             "b_in": (b_in, gb_in, m_b_in, v_b_in), "sinks": (sinks, gsinks, m_sinks, v_sinks),
             "conv_w": (conv_w, gcw_sh, m_conv_w, v_conv_w), "g_ffn": (g_ffn, gg_ffn, m_g_ffn, v_g_ffn),
             "g_final": (g_final, gg_final, m_g_final, v_g_final)}
    def two_d(a):
        return a.reshape(-1, a.shape[-1])

    s_out = _adamw_small([tuple(two_d(a.reshape(w.shape)) for a in (w, g, m, v)) for w, g, m, v in small.values()])
    for (nm, (w, g, _, _)), res in zip(small.items(), s_out):
        out_g[nm] = g.reshape(w.shape)
        out_d[nm], out_m[nm], out_v[nm] = [o.reshape(w.shape) for o in res]

    order = ["w_ada", "b_ada", "g_mix", "w_in", "b_in", "sinks", "conv_w", "w_out", "g_ffn", "w_ffn_in", "w_ffn_out",
             "g_final"]
    return (loss, grad_x[None], *[out_g[k] for k in order], *[out_d[k] for k in order],
            *[out_m[k] for k in order], *[out_v[k] for k in order])
```
